```python
import math
import jax, jax.numpy as jnp
from jax import lax
import numpy as np

D_MODEL = 2048
BATCH = 8
SEQ = 4096
DEPTH = 1

MIX_WIDTH = D_MODEL
CONV_CH = MIX_WIDTH // 2
CONV_GROUPS = 8
CONV_WIDTH = 31
SG_CH = MIX_WIDTH - CONV_CH
SG_HEADS = 8
SG_HEAD_DIM = SG_CH // SG_HEADS
CHUNK = 128
D_FF = 5632
LN_EPS = 1e-5
DN_ALPHA = (2.0 * DEPTH) ** 0.25
DN_BETA = (8.0 * DEPTH) ** -0.25

kernel_name = "hybrid_conv_sgmlp_macaron_deepnorm"


def layer_norm(x, g, b):
    xf = x.astype(jnp.float32)
    mu = jnp.mean(xf, axis=-1, keepdims=True)
    var = jnp.mean(jnp.square(xf - mu), axis=-1, keepdims=True)
    y = (xf - mu) * lax.rsqrt(var + LN_EPS)
    return (y * g + b).astype(x.dtype)


def swiglu_ffn(x, w_gate_up, w_down):
    gu = x @ w_gate_up
    g, u = jnp.split(gu, 2, axis=-1)
    return (jax.nn.silu(g) * u) @ w_down


def conv_mixer(a_val, a_gate, conv_w, conv_b, ln_g, ln_b):
    h = a_val * jax.nn.sigmoid(a_gate)
    rhs = conv_w[:, None, :].astype(h.dtype)
    h = lax.conv_general_dilated(
        h, rhs, window_strides=(1,), padding=[(CONV_WIDTH - 1, 0)],
        dimension_numbers=("NWC", "WIO", "NWC"),
        feature_group_count=CONV_CH) + conv_b
    h = layer_norm(h, ln_g, ln_b)
    return jax.nn.silu(h)


def spatial_gating_mixer(b_u, b_v, ln_g, ln_b, w_s, b_s):
    bsz, seq, _ = b_u.shape
    n_chunks = seq // CHUNK
    u = jax.nn.gelu(b_u).reshape(bsz, n_chunks, CHUNK, SG_HEADS, SG_HEAD_DIM)
    v = jax.nn.gelu(b_v).reshape(bsz, n_chunks, CHUNK, SG_HEADS, SG_HEAD_DIM)
    v = layer_norm(v, ln_g, ln_b)
    causal = jnp.tril(jnp.ones((CHUNK, CHUNK), dtype=w_s.dtype))
    w = w_s * causal
    mixed = jnp.einsum("hts,bcshd->bcthd", w, v) + b_s.T[None, None, :, :, None]
    return (u * mixed).reshape(bsz, seq, SG_CH)


def _fwd_setup_inputs(seed: int = 0) -> dict:
    key = jax.random.key(seed)
    ks = jax.random.split(key, 24)
    L = DEPTH
    nrm = lambda k, shape, s: jax.random.normal(k, shape, jnp.float32) * s
    gain = lambda k, shape: 1.0 + nrm(k, shape, 0.02)
    return {
        "x": nrm(ks[0], (BATCH, SEQ, D_MODEL), 1.0),
        "ffn1_w_gate_up": nrm(ks[1], (L, D_MODEL, 2 * D_FF), D_MODEL ** -0.5),
        "ffn1_w_down": nrm(ks[2], (L, D_FF, D_MODEL), DN_BETA * D_FF ** -0.5),
        "ln1_g": gain(ks[3], (L, D_MODEL)),
        "ln1_b": nrm(ks[4], (L, D_MODEL), 0.02),
        "mix_w_in": nrm(ks[5], (L, D_MODEL, 2 * CONV_CH + 2 * SG_CH), D_MODEL ** -0.5),
        "conv_w": nrm(ks[6], (L, CONV_WIDTH, CONV_CH), CONV_WIDTH ** -0.5),
        "conv_b": nrm(ks[7], (L, CONV_CH), 0.02),
        "conv_ln_g": gain(ks[8], (L, CONV_CH)),
        "conv_ln_b": nrm(ks[9], (L, CONV_CH), 0.02),
        "sg_ln_g": gain(ks[10], (L, SG_HEADS, SG_HEAD_DIM)),
        "sg_ln_b": nrm(ks[11], (L, SG_HEADS, SG_HEAD_DIM), 0.02),
        "sg_w": nrm(ks[12], (L, SG_HEADS, CHUNK, CHUNK), CHUNK ** -0.5),
        "sg_b": gain(ks[13], (L, SG_HEADS, CHUNK)),
        "mix_w_out": nrm(ks[14], (L, MIX_WIDTH, D_MODEL), DN_BETA * MIX_WIDTH ** -0.5),
        "ln2_g": gain(ks[15], (L, D_MODEL)),
        "ln2_b": nrm(ks[16], (L, D_MODEL), 0.02),
        "ffn2_w_gate_up": nrm(ks[17], (L, D_MODEL, 2 * D_FF), D_MODEL ** -0.5),
        "ffn2_w_down": nrm(ks[18], (L, D_FF, D_MODEL), DN_BETA * D_FF ** -0.5),
        "ln3_g": gain(ks[19], (L, D_MODEL)),
        "ln3_b": nrm(ks[20], (L, D_MODEL), 0.02),
    }


def _fwd_reference(x, ffn1_w_gate_up, ffn1_w_down, ln1_g, ln1_b, mix_w_in, conv_w,
              conv_b, conv_ln_g, conv_ln_b, sg_ln_g, sg_ln_b, sg_w, sg_b,
              mix_w_out, ln2_g, ln2_b, ffn2_w_gate_up, ffn2_w_down, ln3_g, ln3_b):
    for l in range(DEPTH):
        x = layer_norm(DN_ALPHA * x + 0.5 * swiglu_ffn(x, ffn1_w_gate_up[l], ffn1_w_down[l]),
                       ln1_g[l], ln1_b[l])
        proj = x @ mix_w_in[l]
        a_val, a_gate, b_u, b_v = jnp.split(
            proj, [CONV_CH, 2 * CONV_CH, 2 * CONV_CH + SG_CH], axis=-1)
        y_a = conv_mixer(a_val, a_gate, conv_w[l], conv_b[l], conv_ln_g[l], conv_ln_b[l])
        y_b = spatial_gating_mixer(b_u, b_v, sg_ln_g[l], sg_ln_b[l], sg_w[l], sg_b[l])
        mix = jnp.concatenate([y_a, y_b], axis=-1) @ mix_w_out[l]
        x = layer_norm(DN_ALPHA * x + mix, ln2_g[l], ln2_b[l])
        x = layer_norm(DN_ALPHA * x + 0.5 * swiglu_ffn(x, ffn2_w_gate_up[l], ffn2_w_down[l]),
                       ln3_g[l], ln3_b[l])
    return x


import jax as _jax
import jax.numpy as _jnp

TWIN_FORMAT = 'train_step'
FWD_PARAMS = ['x', 'ffn1_w_gate_up', 'ffn1_w_down', 'ln1_g', 'ln1_b', 'mix_w_in', 'conv_w', 'conv_b', 'conv_ln_g', 'conv_ln_b', 'sg_ln_g', 'sg_ln_b', 'sg_w', 'sg_b', 'mix_w_out', 'ln2_g', 'ln2_b', 'ffn2_w_gate_up', 'ffn2_w_down', 'ln3_g', 'ln3_b']
TWIN_WEIGHTS = ['ffn1_w_gate_up', 'ffn1_w_down', 'ln1_g', 'ln1_b', 'mix_w_in', 'conv_w', 'conv_b', 'conv_ln_g', 'conv_ln_b', 'sg_ln_g', 'sg_ln_b', 'sg_w', 'sg_b', 'mix_w_out', 'ln2_g', 'ln2_b', 'ffn2_w_gate_up', 'ffn2_w_down', 'ln3_g', 'ln3_b']
TWIN_DIFF_INPUT = 'x'
TWIN_INPUTS = ['x', 'ffn1_w_gate_up', 'ffn1_w_down', 'ln1_g', 'ln1_b', 'mix_w_in', 'conv_w', 'conv_b', 'conv_ln_g', 'conv_ln_b', 'sg_ln_g', 'sg_ln_b', 'sg_w', 'sg_b', 'mix_w_out', 'ln2_g', 'ln2_b', 'ffn2_w_gate_up', 'ffn2_w_down', 'ln3_g', 'ln3_b', 'loss_target', 'm_ffn1_w_gate_up', 'm_ffn1_w_down', 'm_ln1_g', 'm_ln1_b', 'm_mix_w_in', 'm_conv_w', 'm_conv_b', 'm_conv_ln_g', 'm_conv_ln_b', 'm_sg_ln_g', 'm_sg_ln_b', 'm_sg_w', 'm_sg_b', 'm_mix_w_out', 'm_ln2_g', 'm_ln2_b', 'm_ffn2_w_gate_up', 'm_ffn2_w_down', 'm_ln3_g', 'm_ln3_b', 'v_ffn1_w_gate_up', 'v_ffn1_w_down', 'v_ln1_g', 'v_ln1_b', 'v_mix_w_in', 'v_conv_w', 'v_conv_b', 'v_conv_ln_g', 'v_conv_ln_b', 'v_sg_ln_g', 'v_sg_ln_b', 'v_sg_w', 'v_sg_b', 'v_mix_w_out', 'v_ln2_g', 'v_ln2_b', 'v_ffn2_w_gate_up', 'v_ffn2_w_down', 'v_ln3_g', 'v_ln3_b']
TWIN_OUTPUTS = ['loss', 'grad_x', 'grad_ffn1_w_gate_up', 'grad_ffn1_w_down', 'grad_ln1_g', 'grad_ln1_b', 'grad_mix_w_in', 'grad_conv_w', 'grad_conv_b', 'grad_conv_ln_g', 'grad_conv_ln_b', 'grad_sg_ln_g', 'grad_sg_ln_b', 'grad_sg_w', 'grad_sg_b', 'grad_mix_w_out', 'grad_ln2_g', 'grad_ln2_b', 'grad_ffn2_w_gate_up', 'grad_ffn2_w_down', 'grad_ln3_g', 'grad_ln3_b', 'delta_ffn1_w_gate_up', 'delta_ffn1_w_down', 'delta_ln1_g', 'delta_ln1_b', 'delta_mix_w_in', 'delta_conv_w', 'delta_conv_b', 'delta_conv_ln_g', 'delta_conv_ln_b', 'delta_sg_ln_g', 'delta_sg_ln_b', 'delta_sg_w', 'delta_sg_b', 'delta_mix_w_out', 'delta_ln2_g', 'delta_ln2_b', 'delta_ffn2_w_gate_up', 'delta_ffn2_w_down', 'delta_ln3_g', 'delta_ln3_b', 'new_m_ffn1_w_gate_up', 'new_m_ffn1_w_down', 'new_m_ln1_g', 'new_m_ln1_b', 'new_m_mix_w_in', 'new_m_conv_w', 'new_m_conv_b', 'new_m_conv_ln_g', 'new_m_conv_ln_b', 'new_m_sg_ln_g', 'new_m_sg_ln_b', 'new_m_sg_w', 'new_m_sg_b', 'new_m_mix_w_out', 'new_m_ln2_g', 'new_m_ln2_b', 'new_m_ffn2_w_gate_up', 'new_m_ffn2_w_down', 'new_m_ln3_g', 'new_m_ln3_b', 'new_v_ffn1_w_gate_up', 'new_v_ffn1_w_down', 'new_v_ln1_g', 'new_v_ln1_b', 'new_v_mix_w_in', 'new_v_conv_w', 'new_v_conv_b', 'new_v_conv_ln_g', 'new_v_conv_ln_b', 'new_v_sg_ln_g', 'new_v_sg_ln_b', 'new_v_sg_w', 'new_v_sg_b', 'new_v_mix_w_out', 'new_v_ln2_g', 'new_v_ln2_b', 'new_v_ffn2_w_gate_up', 'new_v_ffn2_w_down', 'new_v_ln3_g', 'new_v_ln3_b']
TWIN_LEAF_KINDS = {'loss': 'loss', 'grad_x': 'grad_x', 'grad_ffn1_w_gate_up': 'grad_w', 'grad_ffn1_w_down': 'grad_w', 'grad_ln1_g': 'grad_w', 'grad_ln1_b': 'grad_w', 'grad_mix_w_in': 'grad_w', 'grad_conv_w': 'grad_w', 'grad_conv_b': 'grad_w', 'grad_conv_ln_g': 'grad_w', 'grad_conv_ln_b': 'grad_w', 'grad_sg_ln_g': 'grad_w', 'grad_sg_ln_b': 'grad_w', 'grad_sg_w': 'grad_w', 'grad_sg_b': 'grad_w', 'grad_mix_w_out': 'grad_w', 'grad_ln2_g': 'grad_w', 'grad_ln2_b': 'grad_w', 'grad_ffn2_w_gate_up': 'grad_w', 'grad_ffn2_w_down': 'grad_w', 'grad_ln3_g': 'grad_w', 'grad_ln3_b': 'grad_w', 'delta_ffn1_w_gate_up': 'delta_w', 'delta_ffn1_w_down': 'delta_w', 'delta_ln1_g': 'delta_w', 'delta_ln1_b': 'delta_w', 'delta_mix_w_in': 'delta_w', 'delta_conv_w': 'delta_w', 'delta_conv_b': 'delta_w', 'delta_conv_ln_g': 'delta_w', 'delta_conv_ln_b': 'delta_w', 'delta_sg_ln_g': 'delta_w', 'delta_sg_ln_b': 'delta_w', 'delta_sg_w': 'delta_w', 'delta_sg_b': 'delta_w', 'delta_mix_w_out': 'delta_w', 'delta_ln2_g': 'delta_w', 'delta_ln2_b': 'delta_w', 'delta_ffn2_w_gate_up': 'delta_w', 'delta_ffn2_w_down': 'delta_w', 'delta_ln3_g': 'delta_w', 'delta_ln3_b': 'delta_w', 'new_m_ffn1_w_gate_up': 'new_m', 'new_m_ffn1_w_down': 'new_m', 'new_m_ln1_g': 'new_m', 'new_m_ln1_b': 'new_m', 'new_m_mix_w_in': 'new_m', 'new_m_conv_w': 'new_m', 'new_m_conv_b': 'new_m', 'new_m_conv_ln_g': 'new_m', 'new_m_conv_ln_b': 'new_m', 'new_m_sg_ln_g': 'new_m', 'new_m_sg_ln_b': 'new_m', 'new_m_sg_w': 'new_m', 'new_m_sg_b': 'new_m', 'new_m_mix_w_out': 'new_m', 'new_m_ln2_g': 'new_m', 'new_m_ln2_b': 'new_m', 'new_m_ffn2_w_gate_up': 'new_m', 'new_m_ffn2_w_down': 'new_m', 'new_m_ln3_g': 'new_m', 'new_m_ln3_b': 'new_m', 'new_v_ffn1_w_gate_up': 'new_v', 'new_v_ffn1_w_down': 'new_v', 'new_v_ln1_g': 'new_v', 'new_v_ln1_b': 'new_v', 'new_v_mix_w_in': 'new_v', 'new_v_conv_w': 'new_v', 'new_v_conv_b': 'new_v', 'new_v_conv_ln_g': 'new_v', 'new_v_conv_ln_b': 'new_v', 'new_v_sg_ln_g': 'new_v', 'new_v_sg_ln_b': 'new_v', 'new_v_sg_w': 'new_v', 'new_v_sg_b': 'new_v', 'new_v_mix_w_out': 'new_v', 'new_v_ln2_g': 'new_v', 'new_v_ln2_b': 'new_v', 'new_v_ffn2_w_gate_up': 'new_v', 'new_v_ffn2_w_down': 'new_v', 'new_v_ln3_g': 'new_v', 'new_v_ln3_b': 'new_v'}


def _forward(args):
    return _fwd_reference(*[args[k] for k in FWD_PARAMS])


def _output_shape():
    def fwd():
        inp = _fwd_setup_inputs(0)
        return _fwd_reference(*[inp[k] for k in FWD_PARAMS])
    out = _jax.eval_shape(fwd)
    return out.shape, out.dtype

N_MICROBATCH = 1
ADAM_LR = 0.001
ADAM_B1 = 0.9
ADAM_B2 = 0.999
ADAM_EPS = 1e-08
ADAM_WD = 0.01
ADAM_STEP = 10
PER_EXAMPLE_BATCH_AXIS = {'x': 0, 'loss_target': 0}
SHARED_INPUTS = []
_WEIGHT_DTYPES = {'ffn1_w_gate_up': _jnp.float32, 'ffn1_w_down': _jnp.float32, 'ln1_g': _jnp.float32, 'ln1_b': _jnp.float32, 'mix_w_in': _jnp.float32, 'conv_w': _jnp.float32, 'conv_b': _jnp.float32, 'conv_ln_g': _jnp.float32, 'conv_ln_b': _jnp.float32, 'sg_ln_g': _jnp.float32, 'sg_ln_b': _jnp.float32, 'sg_w': _jnp.float32, 'sg_b': _jnp.float32, 'mix_w_out': _jnp.float32, 'ln2_g': _jnp.float32, 'ln2_b': _jnp.float32, 'ffn2_w_gate_up': _jnp.float32, 'ffn2_w_down': _jnp.float32, 'ln3_g': _jnp.float32, 'ln3_b': _jnp.float32}
MOMENT_SCALE = {'ffn1_w_gate_up': 8.280720e-03, 'ffn1_w_down': 2.271271e-02, 'ln1_g': 4.967960e-01, 'ln1_b': 2.765669e-01, 'mix_w_in': 2.511954e-02, 'conv_w': 2.695874e-02, 'conv_b': 1.094637e-01, 'conv_ln_g': 4.424259e-02, 'conv_ln_b': 6.309435e-02, 'sg_ln_g': 1.990224e-02, 'sg_ln_b': 1.946930e-02, 'sg_w': 1.942513e-02, 'sg_b': 2.753374e-02, 'mix_w_out': 6.762555e-02, 'ln2_g': 5.634857e-01, 'ln2_b': 2.877410e-01, 'ffn2_w_gate_up': 8.063709e-03, 'ffn2_w_down': 2.217980e-02, 'ln3_g': 1.601601e+01, 'ln3_b': 1.865130e+00}


def _to_microbatches(a, axis):
    t = _jnp.moveaxis(a, axis, 0)
    t = t.reshape((N_MICROBATCH, t.shape[0] // N_MICROBATCH) + t.shape[1:])
    return _jnp.moveaxis(t, 1, axis + 1)


def setup_inputs(seed: int = 0) -> dict:
    inp = _fwd_setup_inputs(seed)
    key = _jax.random.fold_in(_jax.random.key(seed), 7919)
    shape, _ = _output_shape()
    out = dict(inp)
    out["loss_target"] = _jax.random.normal(_jax.random.fold_in(key, 0), shape, _jnp.float32)
    for i, name in enumerate(TWIN_WEIGHTS):
        w = inp[name].astype(_jnp.float32)
        if MOMENT_SCALE is None:
            s = _jnp.sqrt(_jnp.mean(_jnp.square(w)) + 1e-30)
        else:
            s = MOMENT_SCALE[name]
        km, kv = _jax.random.split(_jax.random.fold_in(key, i + 1))
        out[name] = w
        out["m_" + name] = s * _jax.random.normal(km, w.shape, _jnp.float32)
        out["v_" + name] = (s * s) * _jax.random.uniform(kv, w.shape, _jnp.float32, 0.5, 1.5)
    if N_MICROBATCH > 1:
        for name, axis in PER_EXAMPLE_BATCH_AXIS.items():
            out[name] = _to_microbatches(out[name], axis)
    return {'x': out['x'], 'ffn1_w_gate_up': out['ffn1_w_gate_up'], 'ffn1_w_down': out['ffn1_w_down'], 'ln1_g': out['ln1_g'], 'ln1_b': out['ln1_b'], 'mix_w_in': out['mix_w_in'], 'conv_w': out['conv_w'], 'conv_b': out['conv_b'], 'conv_ln_g': out['conv_ln_g'], 'conv_ln_b': out['conv_ln_b'], 'sg_ln_g': out['sg_ln_g'], 'sg_ln_b': out['sg_ln_b'], 'sg_w': out['sg_w'], 'sg_b': out['sg_b'], 'mix_w_out': out['mix_w_out'], 'ln2_g': out['ln2_g'], 'ln2_b': out['ln2_b'], 'ffn2_w_gate_up': out['ffn2_w_gate_up'], 'ffn2_w_down': out['ffn2_w_down'], 'ln3_g': out['ln3_g'], 'ln3_b': out['ln3_b'], 'loss_target': out['loss_target'], 'm_ffn1_w_gate_up': out['m_ffn1_w_gate_up'], 'm_ffn1_w_down': out['m_ffn1_w_down'], 'm_ln1_g': out['m_ln1_g'], 'm_ln1_b': out['m_ln1_b'], 'm_mix_w_in': out['m_mix_w_in'], 'm_conv_w': out['m_conv_w'], 'm_conv_b': out['m_conv_b'], 'm_conv_ln_g': out['m_conv_ln_g'], 'm_conv_ln_b': out['m_conv_ln_b'], 'm_sg_ln_g': out['m_sg_ln_g'], 'm_sg_ln_b': out['m_sg_ln_b'], 'm_sg_w': out['m_sg_w'], 'm_sg_b': out['m_sg_b'], 'm_mix_w_out': out['m_mix_w_out'], 'm_ln2_g': out['m_ln2_g'], 'm_ln2_b': out['m_ln2_b'], 'm_ffn2_w_gate_up': out['m_ffn2_w_gate_up'], 'm_ffn2_w_down': out['m_ffn2_w_down'], 'm_ln3_g': out['m_ln3_g'], 'm_ln3_b': out['m_ln3_b'], 'v_ffn1_w_gate_up': out['v_ffn1_w_gate_up'], 'v_ffn1_w_down': out['v_ffn1_w_down'], 'v_ln1_g': out['v_ln1_g'], 'v_ln1_b': out['v_ln1_b'], 'v_mix_w_in': out['v_mix_w_in'], 'v_conv_w': out['v_conv_w'], 'v_conv_b': out['v_conv_b'], 'v_conv_ln_g': out['v_conv_ln_g'], 'v_conv_ln_b': out['v_conv_ln_b'], 'v_sg_ln_g': out['v_sg_ln_g'], 'v_sg_ln_b': out['v_sg_ln_b'], 'v_sg_w': out['v_sg_w'], 'v_sg_b': out['v_sg_b'], 'v_mix_w_out': out['v_mix_w_out'], 'v_ln2_g': out['v_ln2_g'], 'v_ln2_b': out['v_ln2_b'], 'v_ffn2_w_gate_up': out['v_ffn2_w_gate_up'], 'v_ffn2_w_down': out['v_ffn2_w_down'], 'v_ln3_g': out['v_ln3_g'], 'v_ln3_b': out['v_ln3_b']}


def _loss(weights, diff, rest, loss_target):
    with _jax.named_scope("forward"):
        args = {**rest, TWIN_DIFF_INPUT: diff, **{k: w.astype(_WEIGHT_DTYPES[k]) for k, w in weights.items()}}
        y = _forward(args)
    with _jax.named_scope("loss_head"):
        err = _jnp.square(y.astype(_jnp.float32) - loss_target)
        return 0.5 * _jnp.sum(_jnp.mean(err, axis=-1)) if err.ndim else 0.5 * err


def _adamw(w, g, m, v):
    m = ADAM_B1 * m + (1.0 - ADAM_B1) * g
    v = ADAM_B2 * v + (1.0 - ADAM_B2) * _jnp.square(g)
    m_hat = m / (1.0 - ADAM_B1 ** ADAM_STEP)
    v_hat = v / (1.0 - ADAM_B2 ** ADAM_STEP)
    delta = -ADAM_LR * (m_hat / (_jnp.sqrt(v_hat) + ADAM_EPS) + ADAM_WD * w)
    return delta, m, v


def reference(x, ffn1_w_gate_up, ffn1_w_down, ln1_g, ln1_b, mix_w_in, conv_w, conv_b, conv_ln_g, conv_ln_b, sg_ln_g, sg_ln_b, sg_w, sg_b, mix_w_out, ln2_g, ln2_b, ffn2_w_gate_up, ffn2_w_down, ln3_g, ln3_b, loss_target, m_ffn1_w_gate_up, m_ffn1_w_down, m_ln1_g, m_ln1_b, m_mix_w_in, m_conv_w, m_conv_b, m_conv_ln_g, m_conv_ln_b, m_sg_ln_g, m_sg_ln_b, m_sg_w, m_sg_b, m_mix_w_out, m_ln2_g, m_ln2_b, m_ffn2_w_gate_up, m_ffn2_w_down, m_ln3_g, m_ln3_b, v_ffn1_w_gate_up, v_ffn1_w_down, v_ln1_g, v_ln1_b, v_mix_w_in, v_conv_w, v_conv_b, v_conv_ln_g, v_conv_ln_b, v_sg_ln_g, v_sg_ln_b, v_sg_w, v_sg_b, v_mix_w_out, v_ln2_g, v_ln2_b, v_ffn2_w_gate_up, v_ffn2_w_down, v_ln3_g, v_ln3_b):
    given = dict(x=x, ffn1_w_gate_up=ffn1_w_gate_up, ffn1_w_down=ffn1_w_down, ln1_g=ln1_g, ln1_b=ln1_b, mix_w_in=mix_w_in, conv_w=conv_w, conv_b=conv_b, conv_ln_g=conv_ln_g, conv_ln_b=conv_ln_b, sg_ln_g=sg_ln_g, sg_ln_b=sg_ln_b, sg_w=sg_w, sg_b=sg_b, mix_w_out=mix_w_out, ln2_g=ln2_g, ln2_b=ln2_b, ffn2_w_gate_up=ffn2_w_gate_up, ffn2_w_down=ffn2_w_down, ln3_g=ln3_g, ln3_b=ln3_b, loss_target=loss_target, m_ffn1_w_gate_up=m_ffn1_w_gate_up, m_ffn1_w_down=m_ffn1_w_down, m_ln1_g=m_ln1_g, m_ln1_b=m_ln1_b, m_mix_w_in=m_mix_w_in, m_conv_w=m_conv_w, m_conv_b=m_conv_b, m_conv_ln_g=m_conv_ln_g, m_conv_ln_b=m_conv_ln_b, m_sg_ln_g=m_sg_ln_g, m_sg_ln_b=m_sg_ln_b, m_sg_w=m_sg_w, m_sg_b=m_sg_b, m_mix_w_out=m_mix_w_out, m_ln2_g=m_ln2_g, m_ln2_b=m_ln2_b, m_ffn2_w_gate_up=m_ffn2_w_gate_up, m_ffn2_w_down=m_ffn2_w_down, m_ln3_g=m_ln3_g, m_ln3_b=m_ln3_b, v_ffn1_w_gate_up=v_ffn1_w_gate_up, v_ffn1_w_down=v_ffn1_w_down, v_ln1_g=v_ln1_g, v_ln1_b=v_ln1_b, v_mix_w_in=v_mix_w_in, v_conv_w=v_conv_w, v_conv_b=v_conv_b, v_conv_ln_g=v_conv_ln_g, v_conv_ln_b=v_conv_ln_b, v_sg_ln_g=v_sg_ln_g, v_sg_ln_b=v_sg_ln_b, v_sg_w=v_sg_w, v_sg_b=v_sg_b, v_mix_w_out=v_mix_w_out, v_ln2_g=v_ln2_g, v_ln2_b=v_ln2_b, v_ffn2_w_gate_up=v_ffn2_w_gate_up, v_ffn2_w_down=v_ffn2_w_down, v_ln3_g=v_ln3_g, v_ln3_b=v_ln3_b)
    weights = {n: given[n] for n in TWIN_WEIGHTS}
    shared = {n: given[n] for n in SHARED_INPUTS}
    per_example = {n: given[n] for n in ['x']}
    grad_fn = _jax.value_and_grad(_loss, argnums=(0, 1))

    def one_microbatch(ex, loss_target):
        ex = dict(ex)
        diff = ex.pop(TWIN_DIFF_INPUT)
        return grad_fn(weights, diff, {**shared, **ex}, loss_target)

    if N_MICROBATCH == 1:
        loss, (grad_w, grad_x) = one_microbatch(per_example, given["loss_target"])
    else:
        def body(carry, xs):
            loss_sum, grad_sum = carry
            l_k, (gw_k, gx_k) = one_microbatch(xs[0], xs[1])
            with _jax.named_scope("update"):
                return (loss_sum + l_k, _jax.tree.map(_jnp.add, grad_sum, gw_k)), gx_k

        init = (_jnp.zeros((), _jnp.float32), _jax.tree.map(_jnp.zeros_like, weights))
        (loss, grad_w), grad_x = _jax.lax.scan(body, init, (per_example, given["loss_target"]))
    with _jax.named_scope("update"):
        delta_w, new_m, new_v = {}, {}, {}
        for n in TWIN_WEIGHTS:
            delta_w[n], new_m[n], new_v[n] = _adamw(weights[n], grad_w[n], given["m_" + n], given["v_" + n])
    return (loss, grad_x, *[grad_w[n] for n in TWIN_WEIGHTS], *[delta_w[n] for n in TWIN_WEIGHTS],
            *[new_m[n] for n in TWIN_WEIGHTS], *[new_v[n] for n in TWIN_WEIGHTS])
```

```python
import functools
import math

import jax
import jax.numpy as jnp
from jax import lax
from jax.experimental import pallas as pl
from jax.experimental.pallas import tpu as pltpu

F32, BF16 = jnp.float32, jnp.bfloat16
MESH = pl.DeviceIdType.MESH
ANY = pl.BlockSpec(memory_space=pl.ANY)

N_DEV = 8
LN_EPS = 1e-5
ALPHA = 2.0 ** 0.25
CONV_CH = 1024
CONV_TAPS = 31
HALO = 32
HEADS = 8
HEAD_DIM = 128
CHUNK = 128
ADAM_LR, ADAM_B1, ADAM_B2, ADAM_EPS, ADAM_WD, ADAM_STEP = 0.001, 0.9, 0.999, 1e-08, 0.01, 10
V7X_VMEM_LIMIT = 56 * 2 ** 20


def _cparams(*sem):
    return pltpu.CompilerParams(dimension_semantics=sem, vmem_limit_bytes=V7X_VMEM_LIMIT)


def _tile(n, pref, mult):
    best = None
    for t in range(mult, min(n, pref) + 1, mult):
        if n % t == 0:
            best = t
    return best if best is not None else n


def _dot(a, b):
    return jnp.dot(a, b, preferred_element_type=F32)


def _dot_nt(a, b):
    return lax.dot_general(a, b, (((1,), (1,)), ((), ())), preferred_element_type=F32)


def _sigmoid(x):
    return 1.0 / (1.0 + jnp.exp(-x))


def _ln_stats(z):
    mu = jnp.mean(z, axis=-1, keepdims=True)
    zc = z - mu
    var = jnp.mean(zc * zc, axis=-1, keepdims=True)
    rstd = lax.rsqrt(var + LN_EPS)
    return zc * rstd, rstd


def _ln(z, g, b):
    xh, _ = _ln_stats(z)
    return xh * g + b


def _ln_bwd(dxh, xh, rstd):
    m1 = jnp.mean(dxh, axis=-1, keepdims=True)
    m2 = jnp.mean(dxh * xh, axis=-1, keepdims=True)
    return rstd * (dxh - m1 - xh * m2)


_GK = math.sqrt(2.0 / math.pi)
_GA = 0.044715


def _gelu_and_grad(x):
    x2 = x * x
    t = jnp.tanh(_GK * (x + _GA * x * x2))
    y = 0.5 * x * (1.0 + t)
    dy = 0.5 * (1.0 + t) + 0.5 * x * (1.0 - t * t) * (_GK * (1.0 + 3.0 * _GA * x2))
    return y, dy


def _silu_grad(a):
    s = _sigmoid(a)
    return s * (1.0 + a * (1.0 - s))


def _place():
    return lax.axis_index("x"), lax.axis_index("y"), lax.axis_index("c")


def _other_chips(x, y):
    return [(1 - x, y), (x, 1 - y), (1 - x, 1 - y)]


def _allgather(shards, col_major):
    n = len(shards)

    def body(*refs):
        srcs, dsts = refs[:n], refs[n:2 * n]
        send_sems, recv_sems, local_sems = refs[2 * n:]
        x, y, c = _place()
        me, sib = (x, y, c), (x, y, 1 - c)
        chips = _other_chips(x, y)

        def slot(w, p):
            k = 4 * p[0] + 2 * p[1] + p[2]
            if col_major[w]:
                cols = shards[w].shape[1]
                return dsts[w].at[:, pl.ds(pl.multiple_of(k * cols, 128), cols)]
            return dsts[w].at[k]

        def copy(w, s, block, to, from_src=False):
            return pltpu.make_async_remote_copy(
                src_ref=srcs[w] if from_src else slot(w, block), dst_ref=slot(w, block),
                send_sem=send_sems.at[7 * w + s], recv_sem=recv_sems.at[7 * w + s],
                device_id=to, device_id_type=MESH)

        started = []
        for w in range(n):
            mine = pltpu.make_async_copy(srcs[w], slot(w, me), local_sems.at[w])
            mine.start()
            started.append(mine)
        first = []
        for w in range(n):
            first.append(copy(w, 0, me, sib, from_src=True))
            first += [copy(w, 1 + j, me, (*chip, c), from_src=True) for j, chip in enumerate(chips)]
        for cp in first:
            cp.start()
        passed = []
        for w in range(n):
            for j, chip in enumerate(chips):
                copy(w, 1 + j, (*chip, c), me).wait_recv()
                fwd = copy(w, 4 + j, (*chip, c), sib)
                fwd.start()
                passed.append(fwd)
        for w in range(n):
            copy(w, 0, sib, me).wait_recv()
            for j, chip in enumerate(chips):
                copy(w, 4 + j, (*chip, 1 - c), me).wait_recv()
        for cp in first + passed:
            cp.wait_send()
        for cp in started:
            cp.wait()

    out_shape = []
    for w, s in enumerate(shards):
        r, cdim = s.shape
        out_shape.append(jax.ShapeDtypeStruct((r, N_DEV * cdim) if col_major[w] else (N_DEV, r, cdim), s.dtype))
    return pl.pallas_call(
        body, name="weights_allgather", out_shape=out_shape,
        in_specs=[ANY] * n, out_specs=[ANY] * n,
        scratch_shapes=[pltpu.SemaphoreType.DMA((7 * n,)), pltpu.SemaphoreType.DMA((7 * n,)),
                        pltpu.SemaphoreType.DMA((n,))],
    )(*shards)


def _rs_sibling_exchange(parts):
    n = len(parts)

    def body(*refs):
        srcs, dsts = refs[:n], refs[n:2 * n]
        send_sems, recv_sems = refs[2 * n:]
        x, y, c = _place()
        sib = (x, y, 1 - c)
        copies = []
        for w in range(n):
            for j in range(4):
                copies.append(pltpu.make_async_remote_copy(
                    src_ref=srcs[w].at[2 * j + (1 - c)], dst_ref=dsts[w].at[j],
                    send_sem=send_sems.at[4 * w + j], recv_sem=recv_sems.at[4 * w + j],
                    device_id=sib, device_id_type=MESH))
        for cp in copies:
            cp.start()
        for cp in copies:
            cp.wait()

    return pl.pallas_call(
        body, name="grads_sibling_exchange",
        out_shape=[jax.ShapeDtypeStruct((4,) + p.shape[1:], p.dtype) for p in parts],
        in_specs=[ANY] * n, out_specs=[ANY] * n,
        scratch_shapes=[pltpu.SemaphoreType.DMA((4 * n,)), pltpu.SemaphoreType.DMA((4 * n,))],
    )(*parts)


def _rs_chip_exchange(chip_parts):
    n = len(chip_parts)

    def body(*refs):
        srcs, dsts = refs[:n], refs[n:2 * n]
        send_sems, recv_sems = refs[2 * n:]
        x, y, c = _place()
        copies = []
        for w in range(n):
            for rel, (px, py) in enumerate(_other_chips(x, y)):
                copies.append(pltpu.make_async_remote_copy(
                    src_ref=srcs[w].at[2 * px + py], dst_ref=dsts[w].at[rel],
                    send_sem=send_sems.at[3 * w + rel], recv_sem=recv_sems.at[3 * w + rel],
                    device_id=(px, py, c), device_id_type=MESH))
        for cp in copies:
            cp.start()
        for cp in copies:
            cp.wait()

    return pl.pallas_call(
        body, name="grads_chip_exchange",
        out_shape=[jax.ShapeDtypeStruct((3,) + p.shape[1:], p.dtype) for p in chip_parts],
        in_specs=[ANY] * n, out_specs=[ANY] * n,
        scratch_shapes=[pltpu.SemaphoreType.DMA((3 * n,)), pltpu.SemaphoreType.DMA((3 * n,))],
    )(*chip_parts)


def _small_allreduce(part):
    rows, lanes = part.shape

    def body(x_ref, o_ref, buf, send_sems, recv_sems):
        x, y, c = _place()
        me = 4 * x + 2 * y + c
        peers = []
        for d in range(1, N_DEV):
            peers.append((1 - x if d & 4 else x, 1 - y if d & 2 else y, 1 - c if d & 1 else c))
        copies = [pltpu.make_async_remote_copy(
            src_ref=x_ref, dst_ref=buf.at[me], send_sem=send_sems.at[d], recv_sem=recv_sems.at[d],
            device_id=p, device_id_type=MESH) for d, p in enumerate(peers)]
        for cp in copies:
            cp.start()
        buf[me] = x_ref[...]
        for d, p in enumerate(peers):
            src = 4 * p[0] + 2 * p[1] + p[2]
            pltpu.make_async_remote_copy(
                src_ref=x_ref, dst_ref=buf.at[src], send_sem=send_sems.at[d], recv_sem=recv_sems.at[d],
                device_id=p, device_id_type=MESH).wait_recv()
        for cp in copies:
            cp.wait_send()
        acc = buf[0]
        for k in range(1, N_DEV):
            acc = acc + buf[k]
        o_ref[...] = acc

    vm = pl.BlockSpec(memory_space=pltpu.VMEM)
    return pl.pallas_call(
        body, name="small_grads_allreduce", out_shape=jax.ShapeDtypeStruct(part.shape, F32),
        in_specs=[vm], out_specs=vm,
        scratch_shapes=[pltpu.VMEM((N_DEV, rows, lanes), F32), pltpu.SemaphoreType.DMA((7,)),
                        pltpu.SemaphoreType.DMA((7,))],
        compiler_params=pltpu.CompilerParams(vmem_limit_bytes=V7X_VMEM_LIMIT),
    )(part)


def _transpose_bf16(a, name):
    r, c = a.shape
    tr, tc = _tile(r, 512, 128), _tile(c, 512, 128)

    def body(a_ref, o_ref):
        o_ref[...] = a_ref[...].astype(F32).T.astype(BF16)

    return pl.pallas_call(
        body, name=name, grid=(r // tr, c // tc), out_shape=jax.ShapeDtypeStruct((c, r), BF16),
        in_specs=[pl.BlockSpec((tr, tc), lambda i, j: (i, j))],
        out_specs=pl.BlockSpec((tc, tr), lambda i, j: (j, i)),
        compiler_params=_cparams("parallel", "parallel"),
    )(a)


def _ffn_fwd(x, wgu, wd, ln_g, ln_b, name):
    t, d = x.shape
    f = wd.shape[0]
    tm, tf = _tile(t, 512, 128), _tile(f, 512, 128)
    nf = f // tf

    def body(x_ref, wg_ref, wu_ref, wd_ref, g_ref, b_ref, go_ref, uo_ref, ht_ref, z_ref, xn_ref, xb, acc):
        j = pl.program_id(1)

        @pl.when(j == 0)
        def _():
            xb[...] = x_ref[...].astype(BF16)
            acc[...] = jnp.zeros_like(acc)

        g = _dot(xb[...], wg_ref[...])
        u = _dot(xb[...], wu_ref[...])
        h = g * _sigmoid(g) * u
        go_ref[...] = g.astype(BF16)
        uo_ref[...] = u.astype(BF16)
        ht_ref[...] = h.T.astype(BF16)
        acc[...] += _dot(h.astype(BF16), wd_ref[...])

        @pl.when(j == nf - 1)
        def _():
            z = ALPHA * x_ref[...] + 0.5 * acc[...]
            z_ref[...] = z
            xn_ref[...] = _ln(z, g_ref[...], b_ref[...])

    row = lambda i, j: (i, 0)
    return pl.pallas_call(
        body, name=name, grid=(t // tm, nf),
        out_shape=[jax.ShapeDtypeStruct((t, f), BF16), jax.ShapeDtypeStruct((t, f), BF16),
                   jax.ShapeDtypeStruct((f, t), BF16), jax.ShapeDtypeStruct((t, d), F32),
                   jax.ShapeDtypeStruct((t, d), F32)],
        in_specs=[pl.BlockSpec((tm, d), row),
                  pl.BlockSpec((d, tf), lambda i, j: (0, j)),
                  pl.BlockSpec((d, tf), lambda i, j: (0, j + nf)),
                  pl.BlockSpec((tf, d), lambda i, j: (j, 0)),
                  pl.BlockSpec((1, d), lambda i, j: (0, 0)),
                  pl.BlockSpec((1, d), lambda i, j: (0, 0))],
        out_specs=[pl.BlockSpec((tm, tf), lambda i, j: (i, j)), pl.BlockSpec((tm, tf), lambda i, j: (i, j)),
                   pl.BlockSpec((tf, tm), lambda i, j: (j, i)), pl.BlockSpec((tm, d), row),
                   pl.BlockSpec((tm, d), row)],
        scratch_shapes=[pltpu.VMEM((tm, d), BF16), pltpu.VMEM((tm, d), F32)],
        compiler_params=_cparams("parallel", "arbitrary"),
    )(x, wgu, wgu, wd, ln_g, ln_b)


def _ffn_bwd(dz, g, u, wgu, wd, name):
    t, d = dz.shape
    f = wd.shape[0]
    tm, tf = _tile(t, 512, 128), _tile(f, 512, 128)
    nf = f // tf

    def body(dz_ref, g_ref, u_ref, wg_ref, wu_ref, wd_ref, dg_ref, du_ref, dob_ref, dx_ref, acc):
        j = pl.program_id(1)

        @pl.when(j == 0)
        def _():
            dob_ref[...] = (0.5 * dz_ref[...]).astype(BF16)
            acc[...] = jnp.zeros_like(acc)

        dh = _dot_nt(dob_ref[...], wd_ref[...])
        gg = g_ref[...].astype(F32)
        uu = u_ref[...].astype(F32)
        s = _sigmoid(gg)
        du = (dh * (gg * s)).astype(BF16)
        dg = (dh * uu * (s * (1.0 + gg * (1.0 - s)))).astype(BF16)
        dg_ref[...] = dg
        du_ref[...] = du
        acc[...] += _dot_nt(dg, wg_ref[...]) + _dot_nt(du, wu_ref[...])

        @pl.when(j == nf - 1)
        def _():
            dx_ref[...] = ALPHA * dz_ref[...] + acc[...]

    row = lambda i, j: (i, 0)
    tile = lambda i, j: (i, j)
    return pl.pallas_call(
        body, name=name, grid=(t // tm, nf),
        out_shape=[jax.ShapeDtypeStruct((t, f), BF16), jax.ShapeDtypeStruct((t, f), BF16),
                   jax.ShapeDtypeStruct((t, d), BF16), jax.ShapeDtypeStruct((t, d), F32)],
        in_specs=[pl.BlockSpec((tm, d), row), pl.BlockSpec((tm, tf), tile), pl.BlockSpec((tm, tf), tile),
                  pl.BlockSpec((d, tf), lambda i, j: (0, j)),
                  pl.BlockSpec((d, tf), lambda i, j: (0, j + nf)),
                  pl.BlockSpec((tf, d), lambda i, j: (j, 0))],
        out_specs=[pl.BlockSpec((tm, tf), tile), pl.BlockSpec((tm, tf), tile),
                   pl.BlockSpec((tm, d), row), pl.BlockSpec((tm, d), row)],
        scratch_shapes=[pltpu.VMEM((tm, d), F32)],
        compiler_params=_cparams("parallel", "arbitrary"),
    )(dz, g, u, wgu, wgu, wd)


def _weight_grad(at, b, tn, tmm, name, blocks=None, block_offset=0, into=None):
    m, t = at.shape
    nn = b.shape[1]
    tmm = _tile(m, tmm, 16)
    assert nn % tn == 0

    def body(*refs):
        at_ref, b_ref, o_ref = refs[0], refs[1], refs[-1]
        r = _dot(at_ref[...], b_ref[...]).astype(BF16)
        if blocks is None:
            o_ref[...] = r
        else:
            o_ref[0] = r

    in_specs = [pl.BlockSpec((tmm, t), lambda n, i: (i, 0)), pl.BlockSpec((t, tn), lambda n, i: (0, n))]
    args = [at, b]
    aliases = {}
    if into is not None:
        in_specs.append(ANY)
        args.append(into)
        aliases = {2: 0}
    if blocks is None:
        out_shape = jax.ShapeDtypeStruct((m, nn), BF16)
        out_spec = pl.BlockSpec((tmm, tn), lambda n, i: (i, n))
    else:
        out_shape = jax.ShapeDtypeStruct((blocks, m, tn), BF16)
        out_spec = pl.BlockSpec((1, tmm, tn), lambda n, i: (n + block_offset, i, 0))
    return pl.pallas_call(
        body, name=name, grid=(nn // tn, m // tmm), out_shape=out_shape,
        in_specs=in_specs, out_specs=out_spec, input_output_aliases=aliases,
        compiler_params=_cparams("parallel", "parallel"),
    )(*args)


def _mix_in_proj(x, w_in, name):
    t, d = x.shape
    nb, _, cb = w_in.shape
    tm = _tile(t, 512, 128)

    def body(x_ref, w_ref, o_ref, xb):
        @pl.when(pl.program_id(1) == 0)
        def _():
            xb[...] = x_ref[...].astype(BF16)

        o_ref[...] = _dot(xb[...], w_ref[0])

    return pl.pallas_call(
        body, name=name, grid=(t // tm, nb), out_shape=jax.ShapeDtypeStruct((t, nb * cb), F32),
        in_specs=[pl.BlockSpec((tm, d), lambda i, k: (i, 0)), pl.BlockSpec((1, d, cb), lambda i, k: (k, 0, 0))],
        out_specs=pl.BlockSpec((tm, cb), lambda i, k: (i, k)),
        scratch_shapes=[pltpu.VMEM((tm, d), BF16)],
        compiler_params=_cparams("parallel", "arbitrary"),
    )(x, w_in)


def _mix_in_bwd(dproj, w_in, dz, name):
    t, d = dz.shape
    nb, _, cb = w_in.shape
    tm = _tile(t, 512, 128)

    def body(dp_ref, w_ref, dz_ref, dx_ref, acc):
        k = pl.program_id(1)

        @pl.when(k == 0)
        def _():
            acc[...] = jnp.zeros_like(acc)

        acc[...] += _dot_nt(dp_ref[...], w_ref[0])

        @pl.when(k == nb - 1)
        def _():
            dx_ref[...] = ALPHA * dz_ref[...] + acc[...]

    return pl.pallas_call(
        body, name=name, grid=(t // tm, nb), out_shape=jax.ShapeDtypeStruct((t, d), F32),
        in_specs=[pl.BlockSpec((tm, cb), lambda i, k: (i, k)), pl.BlockSpec((1, d, cb), lambda i, k: (k, 0, 0)),
                  pl.BlockSpec((tm, d), lambda i, k: (i, 0))],
        out_specs=pl.BlockSpec((tm, d), lambda i, k: (i, 0)),
        scratch_shapes=[pltpu.VMEM((tm, d), F32)],
        compiler_params=_cparams("parallel", "arbitrary"),
    )(dproj, w_in, dz)


def _mix_out_fwd(y, w_out, x, ln_g, ln_b, name):
    t, d = x.shape
    kk = y.shape[1]
    tm = _tile(t, 256, 128)

    def body(y_ref, w_ref, x_ref, g_ref, b_ref, z_ref, xn_ref):
        z = ALPHA * x_ref[...] + _dot(y_ref[...], w_ref[...])
        z_ref[...] = z
        xn_ref[...] = _ln(z, g_ref[...], b_ref[...])

    row = lambda i: (i, 0)
    fixed = lambda i: (0, 0)
    return pl.pallas_call(
        body, name=name, grid=(t // tm,),
        out_shape=[jax.ShapeDtypeStruct((t, d), F32), jax.ShapeDtypeStruct((t, d), F32)],
        in_specs=[pl.BlockSpec((tm, kk), row), pl.BlockSpec((kk, d), fixed), pl.BlockSpec((tm, d), row),
                  pl.BlockSpec((1, d), fixed), pl.BlockSpec((1, d), fixed)],
        out_specs=[pl.BlockSpec((tm, d), row), pl.BlockSpec((tm, d), row)],
        compiler_params=_cparams("parallel"),
    )(y, w_out, x, ln_g, ln_b)


def _mix_out_bwd(dzb, w_out, name):
    t, d = dzb.shape
    kk = w_out.shape[0]
    tm = _tile(t, 256, 128)

    def body(dz_ref, w_ref, dy_ref):
        dy_ref[...] = _dot_nt(dz_ref[...], w_ref[...])

    return pl.pallas_call(
        body, name=name, grid=(t // tm,), out_shape=jax.ShapeDtypeStruct((t, kk), F32),
        in_specs=[pl.BlockSpec((tm, d), lambda i: (i, 0)), pl.BlockSpec((kk, d), lambda i: (0, 0))],
        out_specs=pl.BlockSpec((tm, kk), lambda i: (i, 0)),
        compiler_params=_cparams("parallel"),
    )(dzb, w_out)


def _loss_grad(xn, target, name):
    t, d = xn.shape
    tm = _tile(t, 512, 8)

    def body(x_ref, t_ref, dy_ref, loss_ref):
        @pl.when(pl.program_id(0) == 0)
        def _():
            loss_ref[...] = jnp.zeros_like(loss_ref)

        e = x_ref[...] - t_ref[...]
        dy_ref[...] = e * (1.0 / d)
        loss_ref[...] += 0.5 * jnp.sum(jnp.sum(e * e, axis=-1, keepdims=True) * (1.0 / d), axis=0, keepdims=True)

    row = lambda i: (i, 0)
    return pl.pallas_call(
        body, name=name, grid=(t // tm,),
        out_shape=[jax.ShapeDtypeStruct((t, d), F32), jax.ShapeDtypeStruct((8, 128), F32)],
        in_specs=[pl.BlockSpec((tm, d), row), pl.BlockSpec((tm, d), row)],
        out_specs=[pl.BlockSpec((tm, d), row), pl.BlockSpec((8, 128), lambda i: (0, 0))],
        compiler_params=_cparams("arbitrary"),
    )(xn, target)


def _ln_bwd_call(z, dy, ln_g, name):
    t, d = z.shape
    tm = _tile(t, 512, 8)

    def body(z_ref, dy_ref, g_ref, dz_ref, dzb_ref, dg_ref, db_ref):
        @pl.when(pl.program_id(0) == 0)
        def _():
            dg_ref[...] = jnp.zeros_like(dg_ref)
            db_ref[...] = jnp.zeros_like(db_ref)

        xh, rstd = _ln_stats(z_ref[...])
        dy = dy_ref[...]
        dz = _ln_bwd(dy * g_ref[...], xh, rstd)
        dz_ref[...] = dz
        dzb_ref[...] = dz.astype(BF16)
        dg_ref[...] += jnp.sum(dy * xh, axis=0, keepdims=True)
        db_ref[...] += jnp.sum(dy, axis=0, keepdims=True)

    row = lambda i: (i, 0)
    fixed = lambda i: (0, 0)
    return pl.pallas_call(
        body, name=name, grid=(t // tm,),
        out_shape=[jax.ShapeDtypeStruct((t, d), F32), jax.ShapeDtypeStruct((t, d), BF16),
                   jax.ShapeDtypeStruct((1, d), F32), jax.ShapeDtypeStruct((1, d), F32)],
        in_specs=[pl.BlockSpec((tm, d), row), pl.BlockSpec((tm, d), row), pl.BlockSpec((1, d), fixed)],
        out_specs=[pl.BlockSpec((tm, d), row), pl.BlockSpec((tm, d), row), pl.BlockSpec((1, d), fixed),
                   pl.BlockSpec((1, d), fixed)],
        compiler_params=_cparams("arbitrary"),
    )(z, dy, ln_g)


CONV_ROWS = 32


def _mixer_fwd(proj, conv_w, conv_b, cln_g, cln_b, sln_g, sln_b, sg_wm, sg_bb, name):
    t = proj.shape[0]
    tm = _tile(t, 256, CHUNK)
    hb = tm // HALO
    nc = tm // CHUNK
    ch = CONV_CH

    def body(av_ref, ag_ref, bu_ref, bv_ref, hv_ref, hg_ref, cw_ref, cb_ref, lg_ref, lb_ref, sg_ref, sb_ref,
             w_ref, bb_ref, y_ref, c_ref, ext):
        i = pl.program_id(0)
        halo = hv_ref[...] * _sigmoid(hg_ref[...])
        ext[0:HALO, :] = jnp.where(i > 0, halo, 0.0)
        ext[HALO:HALO + tm, :] = av_ref[...] * _sigmoid(ag_ref[...])
        for r in range(0, tm, CONV_ROWS):
            acc = jnp.zeros((CONV_ROWS, ch), F32) + cb_ref[...]
            for k in range(CONV_TAPS):
                lo = r + k + HALO - (CONV_TAPS - 1)
                acc = acc + cw_ref[k:k + 1, :] * ext[lo:lo + CONV_ROWS, :]
            c_ref[r:r + CONV_ROWS, :] = acc
        a = _ln(c_ref[...], lg_ref[...], lb_ref[...])
        y_ref[:, 0:ch] = (a * _sigmoid(a)).astype(BF16)
        for h in range(HEADS):
            sl = slice(h * HEAD_DIM, (h + 1) * HEAD_DIM)
            u, _ = _gelu_and_grad(bu_ref[:, sl])
            v, _ = _gelu_and_grad(bv_ref[:, sl])
            vn = _ln(v, sg_ref[h:h + 1, :], sb_ref[h:h + 1, :])
            vn3 = vn.astype(BF16).reshape(nc, CHUNK, HEAD_DIM)
            wb = jnp.broadcast_to(w_ref[h][None], (nc, CHUNK, CHUNK))
            mixed = jnp.einsum("cts,csd->ctd", wb, vn3, preferred_element_type=F32) + bb_ref[h][None]
            y_ref[:, ch + h * HEAD_DIM:ch + (h + 1) * HEAD_DIM] = (u * mixed.reshape(tm, HEAD_DIM)).astype(BF16)

    col = lambda cidx: (lambda i: (i, cidx))
    prev = lambda cidx: (lambda i: (jnp.maximum(i * hb - 1, 0), cidx))
    fix2 = lambda i: (0, 0)
    fix3 = lambda i: (0, 0, 0)
    return pl.pallas_call(
        body, name=name, grid=(t // tm,),
        out_shape=[jax.ShapeDtypeStruct((t, 2 * ch), BF16), jax.ShapeDtypeStruct((t, ch), F32)],
        in_specs=[pl.BlockSpec((tm, ch), col(0)), pl.BlockSpec((tm, ch), col(1)), pl.BlockSpec((tm, ch), col(2)),
                  pl.BlockSpec((tm, ch), col(3)), pl.BlockSpec((HALO, ch), prev(0)), pl.BlockSpec((HALO, ch), prev(1)),
                  pl.BlockSpec((CONV_TAPS, ch), fix2), pl.BlockSpec((1, ch), fix2), pl.BlockSpec((1, ch), fix2),
                  pl.BlockSpec((1, ch), fix2), pl.BlockSpec((HEADS, HEAD_DIM), fix2), pl.BlockSpec((HEADS, HEAD_DIM), fix2),
                  pl.BlockSpec((HEADS, CHUNK, CHUNK), fix3), pl.BlockSpec((HEADS, CHUNK, HEAD_DIM), fix3)],
        out_specs=[pl.BlockSpec((tm, 2 * ch), lambda i: (i, 0)), pl.BlockSpec((tm, ch), lambda i: (i, 0))],
        scratch_shapes=[pltpu.VMEM((HALO + tm, ch), F32)],
        compiler_params=_cparams("parallel"),
    )(proj, proj, proj, proj, proj, proj, conv_w, conv_b, cln_g, cln_b, sln_g, sln_b, sg_wm, sg_bb)


def _mixer_bwd(proj, conv_c, dy, conv_w, cln_g, cln_b, sln_g, sln_b, sg_wm, sg_wmt, sg_bb, name):
    t = proj.shape[0]
    tm = _tile(t, 256, CHUNK)
    hb = tm // HALO
    nc = tm // CHUNK
    nt = t // tm
    ch = CONV_CH
    last_halo = t // HALO - 1

    def body(av_ref, ag_ref, bu_ref, bv_ref, hv_ref, hg_ref, c_ref, cn_ref, dya_ref, dyan_ref, dyb_ref,
             cw_ref, lg_ref, lb_ref, sg_ref, sb_ref, w_ref, wt_ref, bb_ref,
             dp_ref, dcw_ref, dcb_ref, dlg_ref, dlb_ref, dsg_ref, dsb_ref, dw_ref, dbs_ref,
             ext_h, ext_dc, acc_cw):
        i = pl.program_id(0)

        @pl.when(i == 0)
        def _():
            acc_cw[...] = jnp.zeros_like(acc_cw)
            for ref in (dcb_ref, dlg_ref, dlb_ref, dsg_ref, dsb_ref, dw_ref, dbs_ref):
                ref[...] = jnp.zeros_like(ref)

        lg = lg_ref[...]
        lb = lb_ref[...]

        def conv_ln_bwd(c, dya):
            xh, rstd = _ln_stats(c)
            a = xh * lg + lb
            da = dya * _silu_grad(a)
            return _ln_bwd(da * lg, xh, rstd), da, xh

        dc, da, xh = conv_ln_bwd(c_ref[...], dya_ref[...])
        dlg_ref[...] += jnp.sum(da * xh, axis=0, keepdims=True)
        dlb_ref[...] += jnp.sum(da, axis=0, keepdims=True)
        dcb_ref[...] += jnp.sum(dc, axis=0, keepdims=True)
        dcn, _, _ = conv_ln_bwd(cn_ref[...], dyan_ref[...])
        ext_dc[0:tm, :] = dc
        ext_dc[tm:tm + HALO, :] = jnp.where(i < nt - 1, dcn, 0.0)
        sig_g = _sigmoid(ag_ref[...])
        halo = hv_ref[...] * _sigmoid(hg_ref[...])
        ext_h[0:HALO, :] = jnp.where(i > 0, halo, 0.0)
        ext_h[HALO:HALO + tm, :] = av_ref[...] * sig_g
        for r in range(0, tm, CONV_ROWS):
            dcr = ext_dc[r:r + CONV_ROWS, :]
            acc = jnp.zeros((CONV_ROWS, ch), F32)
            for k in range(CONV_TAPS):
                lo = r + k + HALO - (CONV_TAPS - 1)
                prod = dcr * ext_h[lo:lo + CONV_ROWS, :]
                acc_cw[k] += jnp.sum(prod.reshape(CONV_ROWS // 8, 8, ch), axis=0)
                hi = r + (CONV_TAPS - 1) - k
                acc = acc + cw_ref[k:k + 1, :] * ext_dc[hi:hi + CONV_ROWS, :]
            sg_r = sig_g[r:r + CONV_ROWS, :]
            av_r = av_ref[r:r + CONV_ROWS, :]
            dp_ref[r:r + CONV_ROWS, 0:ch] = (acc * sg_r).astype(BF16)
            dp_ref[r:r + CONV_ROWS, ch:2 * ch] = (acc * av_r * sg_r * (1.0 - sg_r)).astype(BF16)

        @pl.when(i == nt - 1)
        def _():
            dcw_ref[...] = jnp.sum(acc_cw[...], axis=1)

        tril = (lax.broadcasted_iota(jnp.int32, (CHUNK, CHUNK), 0)
                >= lax.broadcasted_iota(jnp.int32, (CHUNK, CHUNK), 1)).astype(F32)
        for h in range(HEADS):
            sl = slice(h * HEAD_DIM, (h + 1) * HEAD_DIM)
            u, du_dx = _gelu_and_grad(bu_ref[:, sl])
            v, dv_dx = _gelu_and_grad(bv_ref[:, sl])
            xhv, rstdv = _ln_stats(v)
            gh = sg_ref[h:h + 1, :]
            vn3 = (xhv * gh + sb_ref[h:h + 1, :]).astype(BF16).reshape(nc, CHUNK, HEAD_DIM)
            wb = jnp.broadcast_to(w_ref[h][None], (nc, CHUNK, CHUNK))
            mixed = jnp.einsum("cts,csd->ctd", wb, vn3, preferred_element_type=F32) + bb_ref[h][None]
            dyb = dyb_ref[:, sl]
            d_u = dyb * mixed.reshape(tm, HEAD_DIM)
            dm = dyb * u
            dm3 = dm.reshape(nc, CHUNK, HEAD_DIM)
            dbs_ref[h:h + 1, :] += jnp.sum(jnp.sum(dm3, axis=0).T, axis=0, keepdims=True)
            dm3b = dm3.astype(BF16)
            dw_h = jnp.sum(jnp.einsum("ctd,csd->cts", dm3b, vn3, preferred_element_type=F32), axis=0)
            dw_ref[h] += dw_h * tril
            wtb = jnp.broadcast_to(wt_ref[h][None], (nc, CHUNK, CHUNK))
            d_vn = jnp.einsum("cst,ctd->csd", wtb, dm3b, preferred_element_type=F32).reshape(tm, HEAD_DIM)
            dsg_ref[h:h + 1, :] += jnp.sum(d_vn * xhv, axis=0, keepdims=True)
            dsb_ref[h:h + 1, :] += jnp.sum(d_vn, axis=0, keepdims=True)
            dv = _ln_bwd(d_vn * gh, xhv, rstdv)
            dp_ref[:, 2 * ch + h * HEAD_DIM:2 * ch + (h + 1) * HEAD_DIM] = (d_u * du_dx).astype(BF16)
            dp_ref[:, 3 * ch + h * HEAD_DIM:3 * ch + (h + 1) * HEAD_DIM] = (dv * dv_dx).astype(BF16)

    col = lambda cidx: (lambda i: (i, cidx))
    prev = lambda cidx: (lambda i: (jnp.maximum(i * hb - 1, 0), cidx))
    nxt = lambda i: (jnp.minimum((i + 1) * hb, last_halo), 0)
    fix2 = lambda i: (0, 0)
    fix3 = lambda i: (0, 0, 0)
    out_shape = [jax.ShapeDtypeStruct((t, 4 * ch), BF16), jax.ShapeDtypeStruct((CONV_TAPS, ch), F32),
                 jax.ShapeDtypeStruct((1, ch), F32), jax.ShapeDtypeStruct((1, ch), F32), jax.ShapeDtypeStruct((1, ch), F32),
                 jax.ShapeDtypeStruct((HEADS, HEAD_DIM), F32), jax.ShapeDtypeStruct((HEADS, HEAD_DIM), F32),
                 jax.ShapeDtypeStruct((HEADS, CHUNK, CHUNK), F32), jax.ShapeDtypeStruct((HEADS, CHUNK), F32)]
    out_specs = [pl.BlockSpec((tm, 4 * ch), lambda i: (i, 0)), pl.BlockSpec((CONV_TAPS, ch), fix2),
                 pl.BlockSpec((1, ch), fix2), pl.BlockSpec((1, ch), fix2), pl.BlockSpec((1, ch), fix2),
                 pl.BlockSpec((HEADS, HEAD_DIM), fix2), pl.BlockSpec((HEADS, HEAD_DIM), fix2),
                 pl.BlockSpec((HEADS, CHUNK, CHUNK), fix3), pl.BlockSpec((HEADS, CHUNK), fix2)]
    in_specs = [pl.BlockSpec((tm, ch), col(0)), pl.BlockSpec((tm, ch), col(1)), pl.BlockSpec((tm, ch), col(2)),
                pl.BlockSpec((tm, ch), col(3)), pl.BlockSpec((HALO, ch), prev(0)), pl.BlockSpec((HALO, ch), prev(1)),
                pl.BlockSpec((tm, ch), col(0)), pl.BlockSpec((HALO, ch), nxt),
                pl.BlockSpec((tm, ch), col(0)), pl.BlockSpec((HALO, ch), nxt), pl.BlockSpec((tm, ch), col(1)),
                pl.BlockSpec((CONV_TAPS, ch), fix2), pl.BlockSpec((1, ch), fix2), pl.BlockSpec((1, ch), fix2),
                pl.BlockSpec((HEADS, HEAD_DIM), fix2), pl.BlockSpec((HEADS, HEAD_DIM), fix2),
                pl.BlockSpec((HEADS, CHUNK, CHUNK), fix3), pl.BlockSpec((HEADS, CHUNK, CHUNK), fix3),
                pl.BlockSpec((HEADS, CHUNK, HEAD_DIM), fix3)]
    return pl.pallas_call(
        body, name=name, grid=(nt,), out_shape=out_shape, in_specs=in_specs, out_specs=out_specs,
        scratch_shapes=[pltpu.VMEM((HALO + tm, ch), F32), pltpu.VMEM((tm + HALO, ch), F32),
                        pltpu.VMEM((CONV_TAPS, 8, ch), F32)],
        compiler_params=_cparams("arbitrary"),
    )(proj, proj, proj, proj, proj, proj, conv_c, conv_c, dy, dy, dy,
      conv_w, cln_g, cln_b, sln_g, sln_b, sg_wm, sg_wmt, sg_bb)


def _pair_sum(parts, from_sibling, c, name):
    _, r, cc = parts.shape
    tr = _tile(r, max(16, (1 << 20) // (2 * cc)), 16)

    def body(c_ref, p_ref, s_ref, o_ref):
        o_ref[...] = (p_ref[...].astype(F32) + s_ref[...].astype(F32)).astype(BF16)

    grid_spec = pltpu.PrefetchScalarGridSpec(
        num_scalar_prefetch=1, grid=(4, r // tr),
        in_specs=[pl.BlockSpec((1, tr, cc), lambda j, i, c_ref: (2 * j + c_ref[0], i, 0)),
                  pl.BlockSpec((1, tr, cc), lambda j, i, c_ref: (j, i, 0))],
        out_specs=pl.BlockSpec((1, tr, cc), lambda j, i, c_ref: (j, i, 0)))
    return pl.pallas_call(
        body, name=name, grid_spec=grid_spec, out_shape=jax.ShapeDtypeStruct((4, r, cc), BF16),
        compiler_params=_cparams("parallel", "parallel"),
    )(c, parts, from_sibling)


def _adamw_math(w, g, m, v):
    m = ADAM_B1 * m + (1.0 - ADAM_B1) * g
    v = ADAM_B2 * v + (1.0 - ADAM_B2) * (g * g)
    m_hat = m / (1.0 - ADAM_B1 ** ADAM_STEP)
    v_hat = v / (1.0 - ADAM_B2 ** ADAM_STEP)
    delta = -ADAM_LR * (m_hat / (jnp.sqrt(v_hat) + ADAM_EPS) + ADAM_WD * w)
    return delta, m, v


def _adamw_sharded(w, m, v, chip_parts, from_chips, chip, name):
    r, cc = w.shape
    tr = _tile(r, max(16, (1 << 19) // (4 * cc) * 2), 16)

    def body(j_ref, w_ref, m_ref, v_ref, q_ref, o_ref, g_out, d_out, m_out, v_out):
        g = q_ref[0].astype(F32)
        for k in range(3):
            g = g + o_ref[k].astype(F32)
        d, mm, vv = _adamw_math(w_ref[...], g, m_ref[...], v_ref[...])
        g_out[...] = g
        d_out[...] = d
        m_out[...] = mm
        v_out[...] = vv

    row = lambda i, j_ref: (i, 0)
    grid_spec = pltpu.PrefetchScalarGridSpec(
        num_scalar_prefetch=1, grid=(r // tr,),
        in_specs=[pl.BlockSpec((tr, cc), row), pl.BlockSpec((tr, cc), row), pl.BlockSpec((tr, cc), row),
                  pl.BlockSpec((1, tr, cc), lambda i, j_ref: (j_ref[0], i, 0)),
                  pl.BlockSpec((3, tr, cc), lambda i, j_ref: (0, i, 0))],
        out_specs=[pl.BlockSpec((tr, cc), row)] * 4)
    return pl.pallas_call(
        body, name=name, grid_spec=grid_spec, out_shape=[jax.ShapeDtypeStruct((r, cc), F32)] * 4,
        compiler_params=_cparams("parallel"),
    )(chip, w, m, v, chip_parts, from_chips)


def _adamw_small(w, g, m, v, name):
    r, cc = w.shape

    def body(w_ref, g_ref, m_ref, v_ref, d_out, m_out, v_out):
        d, mm, vv = _adamw_math(w_ref[...], g_ref[...], m_ref[...], v_ref[...])
        d_out[...] = d
        m_out[...] = mm
        v_out[...] = vv

    full = pl.BlockSpec((r, cc), lambda i: (0, 0))
    return pl.pallas_call(
        body, name=name, grid=(1,), out_shape=[jax.ShapeDtypeStruct((r, cc), F32)] * 3,
        in_specs=[full] * 4, out_specs=[full] * 3, compiler_params=_cparams("arbitrary"),
    )(w, g, m, v)


SMALL = ("ln1_g", "ln1_b", "conv_b", "conv_ln_g", "conv_ln_b", "sg_ln_g", "sg_ln_b", "sg_w", "sg_b",
         "ln2_g", "ln2_b", "ln3_g", "ln3_b")
ORDER = ("ffn1_w_gate_up", "ffn1_w_down", "ln1_g", "ln1_b", "mix_w_in", "conv_w", "conv_b", "conv_ln_g", "conv_ln_b",
         "sg_ln_g", "sg_ln_b", "sg_w", "sg_b", "mix_w_out", "ln2_g", "ln2_b", "ffn2_w_gate_up", "ffn2_w_down",
         "ln3_g", "ln3_b")


def _rows128(a):
    return a.reshape(-1, 128)


def kernel(x, ffn1_w_gate_up, ffn1_w_down, ln1_g, ln1_b, mix_w_in, conv_w, conv_b, conv_ln_g, conv_ln_b, sg_ln_g, sg_ln_b, sg_w, sg_b, mix_w_out, ln2_g, ln2_b, ffn2_w_gate_up, ffn2_w_down, ln3_g, ln3_b, loss_target, m_ffn1_w_gate_up, m_ffn1_w_down, m_ln1_g, m_ln1_b, m_mix_w_in, m_conv_w, m_conv_b, m_conv_ln_g, m_conv_ln_b, m_sg_ln_g, m_sg_ln_b, m_sg_w, m_sg_b, m_mix_w_out, m_ln2_g, m_ln2_b, m_ffn2_w_gate_up, m_ffn2_w_down, m_ln3_g, m_ln3_b, v_ffn1_w_gate_up, v_ffn1_w_down, v_ln1_g, v_ln1_b, v_mix_w_in, v_conv_w, v_conv_b, v_conv_ln_g, v_conv_ln_b, v_sg_ln_g, v_sg_ln_b, v_sg_w, v_sg_b, v_mix_w_out, v_ln2_g, v_ln2_b, v_ffn2_w_gate_up, v_ffn2_w_down, v_ln3_g, v_ln3_b):
    args = dict(locals())
    w = {n: args[n][0] for n in ORDER}
    mom = {n: args["m_" + n][0] for n in ORDER}
    var = {n: args["v_" + n][0] for n in ORDER}
    x0 = x[0]
    target = loss_target[0]
    t, d = x0.shape
    my_x, my_y, my_c = lax.axis_index("x"), lax.axis_index("y"), lax.axis_index("c")
    my_chip = (2 * my_x + my_y).astype(jnp.int32).reshape(1)
    my_core = my_c.astype(jnp.int32).reshape(1)
    me = 4 * my_x + 2 * my_y + my_c

    big = ("ffn1_w_gate_up", "ffn1_w_down", "mix_w_in", "mix_w_out", "ffn2_w_gate_up", "ffn2_w_down")
    shards = [w[n].astype(BF16) for n in big] + [w["conv_w"]]
    col_major = [True, False, False, False, True, False, False]
    wgu1, wd1, w_in, w_out, wgu2, wd2, conv_w_all = _allgather(shards, col_major)
    wd1 = wd1.reshape(-1, d)
    wd2 = wd2.reshape(-1, d)
    w_out = w_out.reshape(-1, d)
    conv_w_full = jnp.transpose(conv_w_all, (1, 0, 2)).reshape(CONV_TAPS, CONV_CH)
    tril = jnp.tril(jnp.ones((CHUNK, CHUNK), F32))
    sg_wm = w["sg_w"] * tril
    sg_wm_b = sg_wm.astype(BF16)
    sg_wmt_b = jnp.swapaxes(sg_wm, 1, 2).astype(BF16)
    sg_bb = jnp.broadcast_to(w["sg_b"][:, :, None], (HEADS, CHUNK, HEAD_DIM))
    row = lambda a: a.reshape(1, -1)

    x0t = _transpose_bf16(x0, "x0_transpose")
    g1, u1, h1t, z1, x1 = _ffn_fwd(x0, wgu1, wd1, row(w["ln1_g"]), row(w["ln1_b"]), "ffn1_fwd")
    x1t = _transpose_bf16(x1, "x1_transpose")
    proj = _mix_in_proj(x1, w_in, "mix_in_fwd")
    y, conv_c = _mixer_fwd(proj, conv_w_full, row(w["conv_b"]), row(w["conv_ln_g"]), row(w["conv_ln_b"]),
                           w["sg_ln_g"], w["sg_ln_b"], sg_wm_b, sg_bb, "mixer_fwd")
    yt = _transpose_bf16(y, "y_transpose")
    z2, x2 = _mix_out_fwd(y, w_out, x1, row(w["ln2_g"]), row(w["ln2_b"]), "mix_out_fwd")
    x2t = _transpose_bf16(x2, "x2_transpose")
    g2, u2, h2t, z3, x3 = _ffn_fwd(x2, wgu2, wd2, row(w["ln3_g"]), row(w["ln3_b"]), "ffn2_fwd")

    f2s = w["ffn1_w_gate_up"].shape[1]
    f = wd1.shape[0]
    grads = {}
    dy3, loss_tile = _loss_grad(x3, target, "loss_grad")
    dz3, _, grads["ln3_g"], grads["ln3_b"] = _ln_bwd_call(z3, dy3, row(w["ln3_g"]), "ln3_bwd")
    dg2, du2, do2, dx2 = _ffn_bwd(dz3, g2, u2, wgu2, wd2, "ffn2_bwd")
    p_gu2 = _weight_grad(x2t, dg2, f2s, 512, "ffn2_dw_gate", blocks=N_DEV)
    p_gu2 = _weight_grad(x2t, du2, f2s, 512, "ffn2_dw_up", blocks=N_DEV, block_offset=4, into=p_gu2)
    p_d2 = _weight_grad(h2t, do2, _tile(d, 1024, 128), 512, "ffn2_dw_down").reshape(N_DEV, f // N_DEV, d)
    dz2, dz2b, grads["ln2_g"], grads["ln2_b"] = _ln_bwd_call(z2, dx2, row(w["ln2_g"]), "ln2_bwd")
    dy = _mix_out_bwd(dz2b, w_out, "mix_out_bwd")
    p_out = _weight_grad(yt, dz2b, _tile(d, 1024, 128), 512, "mix_out_dw").reshape(N_DEV, -1, d)
    (dproj, grads["conv_w"], grads["conv_b"], grads["conv_ln_g"], grads["conv_ln_b"], grads["sg_ln_g"],
     grads["sg_ln_b"], grads["sg_w"], grads["sg_b"]) = _mixer_bwd(
        proj, conv_c, dy, conv_w_full, row(w["conv_ln_g"]), row(w["conv_ln_b"]), w["sg_ln_g"], w["sg_ln_b"],
        sg_wm_b, sg_wmt_b, sg_bb, "mixer_bwd")
    dx1 = _mix_in_bwd(dproj, w_in, dz2, "mix_in_bwd")
    p_in = _weight_grad(x1t, dproj, w_in.shape[2], 512, "mix_in_dw", blocks=N_DEV)
    dz1, _, grads["ln1_g"], grads["ln1_b"] = _ln_bwd_call(z1, dx1, row(w["ln1_g"]), "ln1_bwd")
    dg1, du1, do1, grad_x = _ffn_bwd(dz1, g1, u1, wgu1, wd1, "ffn1_bwd")
    p_gu1 = _weight_grad(x0t, dg1, f2s, 512, "ffn1_dw_gate", blocks=N_DEV)
    p_gu1 = _weight_grad(x0t, du1, f2s, 512, "ffn1_dw_up", blocks=N_DEV, block_offset=4, into=p_gu1)
    p_d1 = _weight_grad(h1t, do1, _tile(d, 1024, 128), 512, "ffn1_dw_down").reshape(N_DEV, f // N_DEV, d)

    parts = [p_gu1, p_d1, p_in, p_out, p_gu2, p_d2]
    from_sib = _rs_sibling_exchange(parts)
    chip_parts = [_pair_sum(p, s, my_core, "grads_pair_sum_%d" % k) for k, (p, s) in enumerate(zip(parts, from_sib))]
    from_chips = _rs_chip_exchange(chip_parts)
    out = {}
    for k, n in enumerate(big):
        out[n] = _adamw_sharded(w[n], mom[n], var[n], chip_parts[k], from_chips[k], my_chip, "adamw_" + n)

    small_parts = [_rows128(grads[n]) for n in SMALL]
    cw_rows = CONV_TAPS * CONV_CH // 128
    packed = jnp.concatenate(small_parts + [_rows128(grads["conv_w"]), loss_tile], axis=0)
    total = _small_allreduce(packed)
    offs = [0]
    for p in small_parts:
        offs.append(offs[-1] + p.shape[0])
    n_small = offs[-1]
    loss = total[n_small + cw_rows, 0]
    g_conv_w = lax.dynamic_slice_in_dim(total[n_small:n_small + cw_rows].reshape(CONV_TAPS, CONV_CH),
                                        me * (CONV_CH // N_DEV), CONV_CH // N_DEV, axis=1)
    pad8 = lambda a: jnp.pad(a, ((0, -a.shape[0] % 8), (0, 0)))
    pack = lambda tree, cw: jnp.concatenate([_rows128(tree[n]) for n in SMALL] + [pad8(cw)], axis=0)
    g_pack = jnp.concatenate([total[:n_small], pad8(g_conv_w)], axis=0)
    d_pack, m_pack, v_pack = _adamw_small(pack(w, w["conv_w"]), g_pack, pack(mom, mom["conv_w"]),
                                          pack(var, var["conv_w"]), "adamw_small")
    for k, n in enumerate(SMALL):
        sl = slice(offs[k], offs[k + 1])
        shp = w[n].shape
        out[n] = (total[sl].reshape(shp), d_pack[sl].reshape(shp), m_pack[sl].reshape(shp), v_pack[sl].reshape(shp))
    sl = slice(n_small, n_small + CONV_TAPS)
    out["conv_w"] = (g_conv_w, d_pack[sl], m_pack[sl], v_pack[sl])

    lead = lambda a: a[None]
    res = [loss, grad_x[None]]
    for kind in range(4):
        res += [lead(out[n][kind]) for n in ORDER]
    return tuple(res)
```

```python
import functools
import math

import jax
import jax.numpy as jnp
from jax import lax
from jax.experimental import pallas as pl
from jax.experimental.pallas import tpu as pltpu

F32, BF16 = jnp.float32, jnp.bfloat16
MESH = pl.DeviceIdType.MESH
ANY = pl.BlockSpec(memory_space=pl.ANY)

N_DEV = 8
LN_EPS = 1e-5
ALPHA = 2.0 ** 0.25
CONV_CH = 1024
CONV_TAPS = 31
HALO = 32
HEADS = 8
HEAD_DIM = 128
CHUNK = 128
ADAM_LR, ADAM_B1, ADAM_B2, ADAM_EPS, ADAM_WD, ADAM_STEP = 0.001, 0.9, 0.999, 1e-08, 0.01, 10
V7X_VMEM_LIMIT = 56 * 2 ** 20


def _cparams(*sem):
    return pltpu.CompilerParams(dimension_semantics=sem, vmem_limit_bytes=V7X_VMEM_LIMIT)


def _tile(n, pref, mult):
    best = None
    for t in range(mult, min(n, pref) + 1, mult):
        if n % t == 0:
            best = t
    return best if best is not None else n


def _dot(a, b):
    return jnp.dot(a, b, preferred_element_type=F32)


def _dot_nt(a, b):
    return lax.dot_general(a, b, (((1,), (1,)), ((), ())), preferred_element_type=F32)


def _sigmoid(x):
    return 1.0 / (1.0 + jnp.exp(-x))


def _ln_stats(z):
    mu = jnp.mean(z, axis=-1, keepdims=True)
    zc = z - mu
    var = jnp.mean(zc * zc, axis=-1, keepdims=True)
    rstd = lax.rsqrt(var + LN_EPS)
    return zc * rstd, rstd


def _ln(z, g, b):
    xh, _ = _ln_stats(z)
    return xh * g + b


def _ln_bwd(dxh, xh, rstd):
    m1 = jnp.mean(dxh, axis=-1, keepdims=True)
    m2 = jnp.mean(dxh * xh, axis=-1, keepdims=True)
    return rstd * (dxh - m1 - xh * m2)


_GK = math.sqrt(2.0 / math.pi)
_GA = 0.044715


def _gelu_and_grad(x):
    x2 = x * x
    t = jnp.tanh(_GK * (x + _GA * x * x2))
    y = 0.5 * x * (1.0 + t)
    dy = 0.5 * (1.0 + t) + 0.5 * x * (1.0 - t * t) * (_GK * (1.0 + 3.0 * _GA * x2))
    return y, dy


def _silu_grad(a):
    s = _sigmoid(a)
    return s * (1.0 + a * (1.0 - s))


def _place():
    return lax.axis_index("x"), lax.axis_index("y"), lax.axis_index("c")


def _other_chips(x, y):
    return [(1 - x, y), (x, 1 - y), (1 - x, 1 - y)]


def _allgather(shards, col_major):
    n = len(shards)

    def body(*refs):
        srcs, dsts = refs[:n], refs[n:2 * n]
        send_sems, recv_sems, local_sems = refs[2 * n:]
        x, y, c = _place()
        me, sib = (x, y, c), (x, y, 1 - c)
        chips = _other_chips(x, y)

        def slot(w, p):
            k = 4 * p[0] + 2 * p[1] + p[2]
            if col_major[w]:
                cols = shards[w].shape[1]
                return dsts[w].at[:, pl.ds(pl.multiple_of(k * cols, 128), cols)]
            return dsts[w].at[k]

        def copy(w, s, block, to, from_src=False):
            return pltpu.make_async_remote_copy(
                src_ref=srcs[w] if from_src else slot(w, block), dst_ref=slot(w, block),
                send_sem=send_sems.at[7 * w + s], recv_sem=recv_sems.at[7 * w + s],
                device_id=to, device_id_type=MESH)

        started = []
        for w in range(n):
            mine = pltpu.make_async_copy(srcs[w], slot(w, me), local_sems.at[w])
            mine.start()
            started.append(mine)
        first = []
        for w in range(n):
            first.append(copy(w, 0, me, sib, from_src=True))
            first += [copy(w, 1 + j, me, (*chip, c), from_src=True) for j, chip in enumerate(chips)]
        for cp in first:
            cp.start()
        passed = []
        for w in range(n):
            for j, chip in enumerate(chips):
                copy(w, 1 + j, (*chip, c), me).wait_recv()
                fwd = copy(w, 4 + j, (*chip, c), sib)
                fwd.start()
                passed.append(fwd)
        for w in range(n):
            copy(w, 0, sib, me).wait_recv()
            for j, chip in enumerate(chips):
                copy(w, 4 + j, (*chip, 1 - c), me).wait_recv()
        for cp in first + passed:
            cp.wait_send()
        for cp in started:
            cp.wait()

    out_shape = []
    for w, s in enumerate(shards):
        r, cdim = s.shape
        out_shape.append(jax.ShapeDtypeStruct((r, N_DEV * cdim) if col_major[w] else (N_DEV, r, cdim), s.dtype))
    return pl.pallas_call(
        body, name="weights_allgather", out_shape=out_shape,
        in_specs=[ANY] * n, out_specs=[ANY] * n,
        scratch_shapes=[pltpu.SemaphoreType.DMA((7 * n,)), pltpu.SemaphoreType.DMA((7 * n,)),
                        pltpu.SemaphoreType.DMA((n,))],
    )(*shards)


class _Exchange:
    def __init__(self, ins, io, new, n_sems, n_local, make):
        self.ins, self.io, self.new = list(ins), list(io), list(new)
        self.n_sems, self.n_local, self.make = n_sems, n_local, make


def _block_slot(ref, col_major, cols, place):
    k = 4 * place[0] + 2 * place[1] + place[2]
    if col_major:
        return ref.at[:, pl.ds(pl.multiple_of(k * cols, 128), cols)]
    return ref.at[k]


def _gather_first(shards, col_major):
    n = len(shards)
    new = [jax.ShapeDtypeStruct((s.shape[0], N_DEV * s.shape[1]) if cm else (N_DEV,) + s.shape, s.dtype)
           for s, cm in zip(shards, col_major)]

    def make(in_refs, io_refs, new_refs, send_sems, recv_sems, local_sems):
        x, y, c = _place()
        targets = [(x, y, 1 - c)] + [(*chip, c) for chip in _other_chips(x, y)]
        copies = []
        for w in range(n):
            slot = _block_slot(new_refs[w], col_major[w], shards[w].shape[1], (x, y, c))
            copies.append(pltpu.make_async_copy(in_refs[w], slot, local_sems.at[w]))
            for s, to in enumerate(targets):
                copies.append(pltpu.make_async_remote_copy(
                    src_ref=in_refs[w], dst_ref=slot, send_sem=send_sems.at[4 * w + s],
                    recv_sem=recv_sems.at[4 * w + s], device_id=to, device_id_type=MESH))
        return copies

    return _Exchange(shards, [], new, 4 * n, n, make)


def _gather_forward(gathered, col_major, cols):
    n = len(gathered)

    def make(in_refs, io_refs, new_refs, send_sems, recv_sems, local_sems):
        x, y, c = _place()
        copies = []
        for w in range(n):
            for j, chip in enumerate(_other_chips(x, y)):
                slot = _block_slot(io_refs[w], col_major[w], cols[w], (*chip, c))
                copies.append(pltpu.make_async_remote_copy(
                    src_ref=slot, dst_ref=slot, send_sem=send_sems.at[3 * w + j], recv_sem=recv_sems.at[3 * w + j],
                    device_id=(x, y, 1 - c), device_id_type=MESH))
        return copies

    return _Exchange([], gathered, [], 3 * n, 0, make)


def _rs_sibling(parts):
    n = len(parts)

    def make(in_refs, io_refs, new_refs, send_sems, recv_sems, local_sems):
        x, y, c = _place()
        copies = []
        for w in range(n):
            for j in range(4):
                copies.append(pltpu.make_async_remote_copy(
                    src_ref=in_refs[w].at[2 * j + (1 - c)], dst_ref=new_refs[w].at[j],
                    send_sem=send_sems.at[4 * w + j], recv_sem=recv_sems.at[4 * w + j],
                    device_id=(x, y, 1 - c), device_id_type=MESH))
        return copies

    return _Exchange(parts, [], [jax.ShapeDtypeStruct((4,) + p.shape[1:], p.dtype) for p in parts], 4 * n, 0, make)


def _rs_chips(chip_parts):
    n = len(chip_parts)

    def make(in_refs, io_refs, new_refs, send_sems, recv_sems, local_sems):
        x, y, c = _place()
        copies = []
        for w in range(n):
            for rel, (px, py) in enumerate(_other_chips(x, y)):
                copies.append(pltpu.make_async_remote_copy(
                    src_ref=in_refs[w].at[2 * px + py], dst_ref=new_refs[w].at[rel],
                    send_sem=send_sems.at[3 * w + rel], recv_sem=recv_sems.at[3 * w + rel],
                    device_id=(px, py, c), device_id_type=MESH))
        return copies

    return _Exchange(chip_parts, [], [jax.ShapeDtypeStruct((3,) + p.shape[1:], p.dtype) for p in chip_parts],
                     3 * n, 0, make)


def _call(body, exch, *, name, grid, in_specs, out_specs, out_shape, scratch_shapes=(), semantics,
          input_output_aliases=None):
    exch = list(exch)
    in_specs, out_specs, out_shape = list(in_specs), list(out_specs), list(out_shape)
    scratch_shapes = list(scratch_shapes)
    if not exch:
        fn = pl.pallas_call(body, name=name, grid=grid, in_specs=in_specs, out_specs=out_specs, out_shape=out_shape,
                            scratch_shapes=scratch_shapes, input_output_aliases=input_output_aliases or {},
                            compiler_params=_cparams(*semantics))
        return lambda *args: (fn(*args), [])
    n_in, n_out, n_scr = len(in_specs), len(out_specs), len(scratch_shapes)
    aliases = dict(input_output_aliases or {})
    all_in, all_out_specs, all_out_shape, all_scr = list(in_specs), list(out_specs), list(out_shape), list(scratch_shapes)
    extra_args = []
    for ex in exch:
        for k, a in enumerate(ex.io):
            aliases[len(all_in) + len(ex.ins) + k] = len(all_out_specs) + k
        all_in += [ANY] * (len(ex.ins) + len(ex.io))
        extra_args += ex.ins + ex.io
        all_out_specs += [ANY] * (len(ex.io) + len(ex.new))
        all_out_shape += [jax.ShapeDtypeStruct(a.shape, a.dtype) for a in ex.io] + ex.new
        all_scr += [pltpu.SemaphoreType.DMA((ex.n_sems,)), pltpu.SemaphoreType.DMA((ex.n_sems,)),
                    pltpu.SemaphoreType.DMA((max(ex.n_local, 1),))]

    def wrapped(*refs):
        pos = n_in
        ex_in = []
        for ex in exch:
            k = len(ex.ins) + len(ex.io)
            ex_in.append(refs[pos:pos + k])
            pos += k
        outs = refs[pos:pos + n_out]
        pos += n_out
        ex_out = []
        for ex in exch:
            k = len(ex.io) + len(ex.new)
            ex_out.append(refs[pos:pos + k])
            pos += k
        scr = refs[pos:pos + n_scr]
        pos += n_scr
        sems = [refs[pos + 3 * k:pos + 3 * k + 3] for k in range(len(exch))]
        first = functools.reduce(jnp.logical_and, [pl.program_id(a) == 0 for a in range(len(grid))])
        last = functools.reduce(jnp.logical_and, [pl.program_id(a) == g - 1 for a, g in enumerate(grid)])

        def copies():
            out = []
            for ex, ei, eo, es in zip(exch, ex_in, ex_out, sems):
                out += ex.make(ei[:len(ex.ins)], eo[:len(ex.io)], eo[len(ex.io):], *es)
            return out

        @pl.when(first)
        def _():
            for cp in copies():
                cp.start()

        body(*refs[:n_in], *outs, *scr)

        @pl.when(last)
        def _():
            for cp in copies():
                cp.wait()

    fn = pl.pallas_call(wrapped, name=name, grid=grid, in_specs=all_in, out_specs=all_out_specs,
                        out_shape=all_out_shape, scratch_shapes=all_scr, input_output_aliases=aliases,
                        compiler_params=_cparams(*(["arbitrary"] * len(grid))))

    def run(*args):
        res = fn(*args, *extra_args)
        outs, pos, ex_res = res[:n_out], n_out, []
        for ex in exch:
            k = len(ex.io) + len(ex.new)
            ex_res.append(list(res[pos:pos + k]))
            pos += k
        return outs, ex_res

    return run


def _exchange_alone(ex, name):
    def body():
        pass

    _, res = _call(body, [ex], name=name, grid=(1,), in_specs=[], out_specs=[], out_shape=[], semantics=("arbitrary",))()
    return res[0]


def _small_allreduce(part):
    rows, lanes = part.shape

    def body(x_ref, o_ref, buf, send_sems, recv_sems):
        x, y, c = _place()
        me = 4 * x + 2 * y + c
        peers = []
        for d in range(1, N_DEV):
            peers.append((1 - x if d & 4 else x, 1 - y if d & 2 else y, 1 - c if d & 1 else c))
        copies = [pltpu.make_async_remote_copy(
            src_ref=x_ref, dst_ref=buf.at[me], send_sem=send_sems.at[d], recv_sem=recv_sems.at[d],
            device_id=p, device_id_type=MESH) for d, p in enumerate(peers)]
        for cp in copies:
            cp.start()
        buf[me] = x_ref[...]
        for d, p in enumerate(peers):
            src = 4 * p[0] + 2 * p[1] + p[2]
            pltpu.make_async_remote_copy(
                src_ref=x_ref, dst_ref=buf.at[src], send_sem=send_sems.at[d], recv_sem=recv_sems.at[d],
                device_id=p, device_id_type=MESH).wait_recv()
        for cp in copies:
            cp.wait_send()
        acc = buf[0]
        for k in range(1, N_DEV):
            acc = acc + buf[k]
        o_ref[...] = acc

    vm = pl.BlockSpec(memory_space=pltpu.VMEM)
    return pl.pallas_call(
        body, name="small_grads_allreduce", out_shape=jax.ShapeDtypeStruct(part.shape, F32),
        in_specs=[vm], out_specs=vm,
        scratch_shapes=[pltpu.VMEM((N_DEV, rows, lanes), F32), pltpu.SemaphoreType.DMA((7,)),
                        pltpu.SemaphoreType.DMA((7,))],
        compiler_params=pltpu.CompilerParams(vmem_limit_bytes=V7X_VMEM_LIMIT),
    )(part)


def _transpose_bf16(a, name, exch=()):
    r, c = a.shape
    tr, tc = _tile(r, 512, 128), _tile(c, 512, 128)

    def body(a_ref, o_ref):
        o_ref[...] = a_ref[...].astype(F32).T.astype(BF16)

    (out,), ex = _call(
        body, exch, name=name, grid=(r // tr, c // tc), out_shape=[jax.ShapeDtypeStruct((c, r), BF16)],
        in_specs=[pl.BlockSpec((tr, tc), lambda i, j: (i, j))],
        out_specs=[pl.BlockSpec((tc, tr), lambda i, j: (j, i))],
        semantics=("parallel", "parallel"),
    )(a)
    return out, ex


def _ffn_fwd(x, wgu, wd, ln_g, ln_b, name, exch=()):
    t, d = x.shape
    f = wd.shape[0]
    tm, tf = _tile(t, 512, 128), _tile(f, 512, 128)
    nf = f // tf

    def body(x_ref, wg_ref, wu_ref, wd_ref, g_ref, b_ref, go_ref, uo_ref, ht_ref, z_ref, xn_ref, xb, acc):
        j = pl.program_id(1)

        @pl.when(j == 0)
        def _():
            xb[...] = x_ref[...].astype(BF16)
            acc[...] = jnp.zeros_like(acc)

        g = _dot(xb[...], wg_ref[...])
        u = _dot(xb[...], wu_ref[...])
        h = g * _sigmoid(g) * u
        go_ref[...] = g.astype(BF16)
        uo_ref[...] = u.astype(BF16)
        ht_ref[...] = h.T.astype(BF16)
        acc[...] += _dot(h.astype(BF16), wd_ref[...])

        @pl.when(j == nf - 1)
        def _():
            z = ALPHA * x_ref[...] + 0.5 * acc[...]
            z_ref[...] = z
            xn_ref[...] = _ln(z, g_ref[...], b_ref[...])

    row = lambda i, j: (i, 0)
    return _call(
        body, exch, name=name, grid=(t // tm, nf),
        out_shape=[jax.ShapeDtypeStruct((t, f), BF16), jax.ShapeDtypeStruct((t, f), BF16),
                   jax.ShapeDtypeStruct((f, t), BF16), jax.ShapeDtypeStruct((t, d), F32),
                   jax.ShapeDtypeStruct((t, d), F32)],
        in_specs=[pl.BlockSpec((tm, d), row),
                  pl.BlockSpec((d, tf), lambda i, j: (0, j)),
                  pl.BlockSpec((d, tf), lambda i, j: (0, j + nf)),
                  pl.BlockSpec((tf, d), lambda i, j: (j, 0)),
                  pl.BlockSpec((1, d), lambda i, j: (0, 0)),
                  pl.BlockSpec((1, d), lambda i, j: (0, 0))],
        out_specs=[pl.BlockSpec((tm, tf), lambda i, j: (i, j)), pl.BlockSpec((tm, tf), lambda i, j: (i, j)),
                   pl.BlockSpec((tf, tm), lambda i, j: (j, i)), pl.BlockSpec((tm, d), row),
                   pl.BlockSpec((tm, d), row)],
        scratch_shapes=[pltpu.VMEM((tm, d), BF16), pltpu.VMEM((tm, d), F32)],
        semantics=("parallel", "arbitrary"),
    )(x, wgu, wgu, wd, ln_g, ln_b)


def _ffn_act_grads(dh, g_ref, u_ref):
    gg = g_ref[...].astype(F32)
    uu = u_ref[...].astype(F32)
    s = _sigmoid(gg)
    du = (dh * (gg * s)).astype(BF16)
    dg = (dh * uu * (s * (1.0 + gg * (1.0 - s)))).astype(BF16)
    return dg, du


def _ffn_bwd(dz, do, g, u, wgu, wd, name, exch=()):
    t, d = dz.shape
    f = wd.shape[0]
    tm, tf = _tile(t, 512, 128), _tile(f, 512, 128)
    nf = f // tf

    def body(dz_ref, do_ref, g_ref, u_ref, wg_ref, wu_ref, wd_ref, dg_ref, du_ref, dx_ref, acc):
        j = pl.program_id(1)

        @pl.when(j == 0)
        def _():
            acc[...] = jnp.zeros_like(acc)

        dg, du = _ffn_act_grads(_dot_nt(do_ref[...], wd_ref[...]), g_ref, u_ref)
        dg_ref[...] = dg
        du_ref[...] = du
        acc[...] += _dot_nt(dg, wg_ref[...]) + _dot_nt(du, wu_ref[...])

        @pl.when(j == nf - 1)
        def _():
            dx_ref[...] = ALPHA * dz_ref[...] + acc[...]

    row = lambda i, j: (i, 0)
    tile = lambda i, j: (i, j)
    return _call(
        body, exch, name=name, grid=(t // tm, nf),
        out_shape=[jax.ShapeDtypeStruct((t, f), BF16), jax.ShapeDtypeStruct((t, f), BF16),
                   jax.ShapeDtypeStruct((t, d), F32)],
        in_specs=[pl.BlockSpec((tm, d), row), pl.BlockSpec((tm, d), row),
                  pl.BlockSpec((tm, tf), tile), pl.BlockSpec((tm, tf), tile),
                  pl.BlockSpec((d, tf), lambda i, j: (0, j)),
                  pl.BlockSpec((d, tf), lambda i, j: (0, j + nf)),
                  pl.BlockSpec((tf, d), lambda i, j: (j, 0))],
        out_specs=[pl.BlockSpec((tm, tf), tile), pl.BlockSpec((tm, tf), tile), pl.BlockSpec((tm, d), row)],
        scratch_shapes=[pltpu.VMEM((tm, d), F32)],
        semantics=("parallel", "arbitrary"),
    )(dz, do, g, u, wgu, wgu, wd)


def _ffn_bwd_act(do, g, u, wd, name, exch=()):
    t, d = do.shape
    f = wd.shape[0]
    tm, tf = _tile(t, 512, 128), _tile(f, 512, 128)

    def body(do_ref, g_ref, u_ref, wd_ref, dg_ref, du_ref):
        dg, du = _ffn_act_grads(_dot_nt(do_ref[...], wd_ref[...]), g_ref, u_ref)
        dg_ref[...] = dg
        du_ref[...] = du

    tile = lambda i, j: (i, j)
    return _call(
        body, exch, name=name, grid=(t // tm, f // tf),
        out_shape=[jax.ShapeDtypeStruct((t, f), BF16), jax.ShapeDtypeStruct((t, f), BF16)],
        in_specs=[pl.BlockSpec((tm, d), lambda i, j: (i, 0)), pl.BlockSpec((tm, tf), tile),
                  pl.BlockSpec((tm, tf), tile), pl.BlockSpec((tf, d), lambda i, j: (j, 0))],
        out_specs=[pl.BlockSpec((tm, tf), tile), pl.BlockSpec((tm, tf), tile)],
        semantics=("parallel", "parallel"),
    )(do, g, u, wd)


def _ffn_bwd_dx(dz, dg, du, wgu, name, exch=()):
    t, d = dz.shape
    f = dg.shape[1]
    tm, tf = _tile(t, 512, 128), _tile(f, 512, 128)
    nf = f // tf

    def body(dz_ref, dg_ref, du_ref, wg_ref, wu_ref, dx_ref, acc):
        j = pl.program_id(1)

        @pl.when(j == 0)
        def _():
            acc[...] = jnp.zeros_like(acc)

        acc[...] += _dot_nt(dg_ref[...], wg_ref[...]) + _dot_nt(du_ref[...], wu_ref[...])

        @pl.when(j == nf - 1)
        def _():
            dx_ref[...] = ALPHA * dz_ref[...] + acc[...]

    row = lambda i, j: (i, 0)
    tile = lambda i, j: (i, j)
    return _call(
        body, exch, name=name, grid=(t // tm, nf), out_shape=[jax.ShapeDtypeStruct((t, d), F32)],
        in_specs=[pl.BlockSpec((tm, d), row), pl.BlockSpec((tm, tf), tile), pl.BlockSpec((tm, tf), tile),
                  pl.BlockSpec((d, tf), lambda i, j: (0, j)), pl.BlockSpec((d, tf), lambda i, j: (0, j + nf))],
        out_specs=[pl.BlockSpec((tm, d), row)],
        scratch_shapes=[pltpu.VMEM((tm, d), F32)],
        semantics=("parallel", "arbitrary"),
    )(dz, dg, du, wgu, wgu)


def _weight_grad(at, b, tn, tmm, name, blocks=None, block_offset=0, into=None, exch=()):
    m, t = at.shape
    nn = b.shape[1]
    tmm = _tile(m, tmm, 16)
    assert nn % tn == 0

    def body(*refs):
        at_ref, b_ref, o_ref = refs[0], refs[1], refs[-1]
        r = _dot(at_ref[...], b_ref[...]).astype(BF16)
        if blocks is None:
            o_ref[...] = r
        else:
            o_ref[0] = r

    in_specs = [pl.BlockSpec((tmm, t), lambda n, i: (i, 0)), pl.BlockSpec((t, tn), lambda n, i: (0, n))]
    args = [at, b]
    aliases = {}
    if into is not None:
        in_specs.append(ANY)
        args.append(into)
        aliases = {2: 0}
    if blocks is None:
        out_shape = jax.ShapeDtypeStruct((m, nn), BF16)
        out_spec = pl.BlockSpec((tmm, tn), lambda n, i: (i, n))
    else:
        out_shape = jax.ShapeDtypeStruct((blocks, m, tn), BF16)
        out_spec = pl.BlockSpec((1, tmm, tn), lambda n, i: (n + block_offset, i, 0))
    (out,), ex = _call(
        body, exch, name=name, grid=(nn // tn, m // tmm), out_shape=[out_shape],
        in_specs=in_specs, out_specs=[out_spec], input_output_aliases=aliases,
        semantics=("parallel", "parallel"),
    )(*args)
    return out, ex


def _mix_in_proj(x, w_in, name, exch=()):
    t, d = x.shape
    nb, _, cb = w_in.shape
    tm = _tile(t, 512, 128)

    def body(x_ref, w_ref, o_ref, xb):
        @pl.when(pl.program_id(1) == 0)
        def _():
            xb[...] = x_ref[...].astype(BF16)

        o_ref[...] = _dot(xb[...], w_ref[0])

    (out,), ex = _call(
        body, exch, name=name, grid=(t // tm, nb), out_shape=[jax.ShapeDtypeStruct((t, nb * cb), F32)],
        in_specs=[pl.BlockSpec((tm, d), lambda i, k: (i, 0)), pl.BlockSpec((1, d, cb), lambda i, k: (k, 0, 0))],
        out_specs=[pl.BlockSpec((tm, cb), lambda i, k: (i, k))],
        scratch_shapes=[pltpu.VMEM((tm, d), BF16)],
        semantics=("parallel", "arbitrary"),
    )(x, w_in)
    return out, ex


def _mix_in_bwd(dproj, w_in, dz, name, exch=()):
    t, d = dz.shape
    nb, _, cb = w_in.shape
    tm = _tile(t, 512, 128)

    def body(dp_ref, w_ref, dz_ref, dx_ref, acc):
        k = pl.program_id(1)

        @pl.when(k == 0)
        def _():
            acc[...] = jnp.zeros_like(acc)

        acc[...] += _dot_nt(dp_ref[...], w_ref[0])

        @pl.when(k == nb - 1)
        def _():
            dx_ref[...] = ALPHA * dz_ref[...] + acc[...]

    (out,), ex = _call(
        body, exch, name=name, grid=(t // tm, nb), out_shape=[jax.ShapeDtypeStruct((t, d), F32)],
        in_specs=[pl.BlockSpec((tm, cb), lambda i, k: (i, k)), pl.BlockSpec((1, d, cb), lambda i, k: (k, 0, 0)),
                  pl.BlockSpec((tm, d), lambda i, k: (i, 0))],
        out_specs=[pl.BlockSpec((tm, d), lambda i, k: (i, 0))],
        scratch_shapes=[pltpu.VMEM((tm, d), F32)],
        semantics=("parallel", "arbitrary"),
    )(dproj, w_in, dz)
    return out, ex


def _mix_out_fwd(y, w_out, x, ln_g, ln_b, name):
    t, d = x.shape
    kk = y.shape[1]
    tm = _tile(t, 256, 128)

    def body(y_ref, w_ref, x_ref, g_ref, b_ref, z_ref, xn_ref):
        z = ALPHA * x_ref[...] + _dot(y_ref[...], w_ref[...])
        z_ref[...] = z
        xn_ref[...] = _ln(z, g_ref[...], b_ref[...])

    row = lambda i: (i, 0)
    fixed = lambda i: (0, 0)
    return pl.pallas_call(
        body, name=name, grid=(t // tm,),
        out_shape=[jax.ShapeDtypeStruct((t, d), F32), jax.ShapeDtypeStruct((t, d), F32)],
        in_specs=[pl.BlockSpec((tm, kk), row), pl.BlockSpec((kk, d), fixed), pl.BlockSpec((tm, d), row),
                  pl.BlockSpec((1, d), fixed), pl.BlockSpec((1, d), fixed)],
        out_specs=[pl.BlockSpec((tm, d), row), pl.BlockSpec((tm, d), row)],
        compiler_params=_cparams("parallel"),
    )(y, w_out, x, ln_g, ln_b)


def _mix_out_bwd(dzb, w_out, name):
    t, d = dzb.shape
    kk = w_out.shape[0]
    tm = _tile(t, 256, 128)

    def body(dz_ref, w_ref, dy_ref):
        dy_ref[...] = _dot_nt(dz_ref[...], w_ref[...])

    return pl.pallas_call(
        body, name=name, grid=(t // tm,), out_shape=jax.ShapeDtypeStruct((t, kk), F32),
        in_specs=[pl.BlockSpec((tm, d), lambda i: (i, 0)), pl.BlockSpec((kk, d), lambda i: (0, 0))],
        out_specs=pl.BlockSpec((tm, kk), lambda i: (i, 0)),
        compiler_params=_cparams("parallel"),
    )(dzb, w_out)


def _loss_grad(xn, target, name):
    t, d = xn.shape
    tm = _tile(t, 512, 8)

    def body(x_ref, t_ref, dy_ref, loss_ref):
        @pl.when(pl.program_id(0) == 0)
        def _():
            loss_ref[...] = jnp.zeros_like(loss_ref)

        e = x_ref[...] - t_ref[...]
        dy_ref[...] = e * (1.0 / d)
        loss_ref[...] += 0.5 * jnp.sum(jnp.sum(e * e, axis=-1, keepdims=True) * (1.0 / d), axis=0, keepdims=True)

    row = lambda i: (i, 0)
    return pl.pallas_call(
        body, name=name, grid=(t // tm,),
        out_shape=[jax.ShapeDtypeStruct((t, d), F32), jax.ShapeDtypeStruct((8, 128), F32)],
        in_specs=[pl.BlockSpec((tm, d), row), pl.BlockSpec((tm, d), row)],
        out_specs=[pl.BlockSpec((tm, d), row), pl.BlockSpec((8, 128), lambda i: (0, 0))],
        compiler_params=_cparams("arbitrary"),
    )(xn, target)


def _ln_bwd_call(z, dy, ln_g, bf16_scale, name, exch=()):
    t, d = z.shape
    tm = _tile(t, 512, 8)

    def body(z_ref, dy_ref, g_ref, dz_ref, dzb_ref, dg_ref, db_ref):
        @pl.when(pl.program_id(0) == 0)
        def _():
            dg_ref[...] = jnp.zeros_like(dg_ref)
            db_ref[...] = jnp.zeros_like(db_ref)

        xh, rstd = _ln_stats(z_ref[...])
        dy = dy_ref[...]
        dz = _ln_bwd(dy * g_ref[...], xh, rstd)
        dz_ref[...] = dz
        dzb_ref[...] = (bf16_scale * dz).astype(BF16)
        dg_ref[...] += jnp.sum(dy * xh, axis=0, keepdims=True)
        db_ref[...] += jnp.sum(dy, axis=0, keepdims=True)

    row = lambda i: (i, 0)
    fixed = lambda i: (0, 0)
    return _call(
        body, exch, name=name, grid=(t // tm,),
        out_shape=[jax.ShapeDtypeStruct((t, d), F32), jax.ShapeDtypeStruct((t, d), BF16),
                   jax.ShapeDtypeStruct((1, d), F32), jax.ShapeDtypeStruct((1, d), F32)],
        in_specs=[pl.BlockSpec((tm, d), row), pl.BlockSpec((tm, d), row), pl.BlockSpec((1, d), fixed)],
        out_specs=[pl.BlockSpec((tm, d), row), pl.BlockSpec((tm, d), row), pl.BlockSpec((1, d), fixed),
                   pl.BlockSpec((1, d), fixed)],
        semantics=("arbitrary",),
    )(z, dy, ln_g)


CONV_ROWS = 32


def _mixer_fwd(proj, conv_w, conv_b, cln_g, cln_b, sln_g, sln_b, sg_wm, sg_bb, name, exch=()):
    t = proj.shape[0]
    tm = _tile(t, 256, CHUNK)
    hb = tm // HALO
    nc = tm // CHUNK
    ch = CONV_CH

    def body(av_ref, ag_ref, bu_ref, bv_ref, hv_ref, hg_ref, cw_ref, cb_ref, lg_ref, lb_ref, sg_ref, sb_ref,
             w_ref, bb_ref, y_ref, c_ref, ext):
        i = pl.program_id(0)
        halo = hv_ref[...] * _sigmoid(hg_ref[...])
        ext[0:HALO, :] = jnp.where(i > 0, halo, 0.0)
        ext[HALO:HALO + tm, :] = av_ref[...] * _sigmoid(ag_ref[...])
        for r in range(0, tm, CONV_ROWS):
            acc = jnp.zeros((CONV_ROWS, ch), F32) + cb_ref[...]
            for k in range(CONV_TAPS):
                lo = r + k + HALO - (CONV_TAPS - 1)
                acc = acc + cw_ref[k:k + 1, :] * ext[lo:lo + CONV_ROWS, :]
            c_ref[r:r + CONV_ROWS, :] = acc
        a = _ln(c_ref[...], lg_ref[...], lb_ref[...])
        y_ref[:, 0:ch] = (a * _sigmoid(a)).astype(BF16)
        for h in range(HEADS):
            sl = slice(h * HEAD_DIM, (h + 1) * HEAD_DIM)
            u, _ = _gelu_and_grad(bu_ref[:, sl])
            v, _ = _gelu_and_grad(bv_ref[:, sl])
            vn = _ln(v, sg_ref[h:h + 1, :], sb_ref[h:h + 1, :])
            vn3 = vn.astype(BF16).reshape(nc, CHUNK, HEAD_DIM)
            wb = jnp.broadcast_to(w_ref[h][None], (nc, CHUNK, CHUNK))
            mixed = jnp.einsum("cts,csd->ctd", wb, vn3, preferred_element_type=F32) + bb_ref[h][None]
            y_ref[:, ch + h * HEAD_DIM:ch + (h + 1) * HEAD_DIM] = (u * mixed.reshape(tm, HEAD_DIM)).astype(BF16)

    col = lambda cidx: (lambda i: (i, cidx))
    prev = lambda cidx: (lambda i: (jnp.maximum(i * hb - 1, 0), cidx))
    fix2 = lambda i: (0, 0)
    fix3 = lambda i: (0, 0, 0)
    return _call(
        body, exch, name=name, grid=(t // tm,),
        out_shape=[jax.ShapeDtypeStruct((t, 2 * ch), BF16), jax.ShapeDtypeStruct((t, ch), F32)],
        in_specs=[pl.BlockSpec((tm, ch), col(0)), pl.BlockSpec((tm, ch), col(1)), pl.BlockSpec((tm, ch), col(2)),
                  pl.BlockSpec((tm, ch), col(3)), pl.BlockSpec((HALO, ch), prev(0)), pl.BlockSpec((HALO, ch), prev(1)),
                  pl.BlockSpec((CONV_TAPS, ch), fix2), pl.BlockSpec((1, ch), fix2), pl.BlockSpec((1, ch), fix2),
                  pl.BlockSpec((1, ch), fix2), pl.BlockSpec((HEADS, HEAD_DIM), fix2), pl.BlockSpec((HEADS, HEAD_DIM), fix2),
                  pl.BlockSpec((HEADS, CHUNK, CHUNK), fix3), pl.BlockSpec((HEADS, CHUNK, HEAD_DIM), fix3)],
        out_specs=[pl.BlockSpec((tm, 2 * ch), lambda i: (i, 0)), pl.BlockSpec((tm, ch), lambda i: (i, 0))],
        scratch_shapes=[pltpu.VMEM((HALO + tm, ch), F32)],
        semantics=("parallel",),
    )(proj, proj, proj, proj, proj, proj, conv_w, conv_b, cln_g, cln_b, sln_g, sln_b, sg_wm, sg_bb)


def _mixer_bwd(proj, conv_c, dy, conv_w, cln_g, cln_b, sln_g, sln_b, sg_wm, sg_wmt, sg_bb, name, exch=()):
    t = proj.shape[0]
    tm = _tile(t, 256, CHUNK)
    hb = tm // HALO
    nc = tm // CHUNK
    nt = t // tm
    ch = CONV_CH
    last_halo = t // HALO - 1

    def body(av_ref, ag_ref, bu_ref, bv_ref, hv_ref, hg_ref, c_ref, cn_ref, dya_ref, dyan_ref, dyb_ref,
             cw_ref, lg_ref, lb_ref, sg_ref, sb_ref, w_ref, wt_ref, bb_ref,
             dp_ref, dcw_ref, dcb_ref, dlg_ref, dlb_ref, dsg_ref, dsb_ref, dw_ref, dbs_ref,
             ext_h, ext_dc, acc_cw):
        i = pl.program_id(0)

        @pl.when(i == 0)
        def _():
            acc_cw[...] = jnp.zeros_like(acc_cw)
            for ref in (dcb_ref, dlg_ref, dlb_ref, dsg_ref, dsb_ref, dw_ref, dbs_ref):
                ref[...] = jnp.zeros_like(ref)

        lg = lg_ref[...]
        lb = lb_ref[...]

        def conv_ln_bwd(c, dya):
            xh, rstd = _ln_stats(c)
            a = xh * lg + lb
            da = dya * _silu_grad(a)
            return _ln_bwd(da * lg, xh, rstd), da, xh

        dc, da, xh = conv_ln_bwd(c_ref[...], dya_ref[...])
        dlg_ref[...] += jnp.sum(da * xh, axis=0, keepdims=True)
        dlb_ref[...] += jnp.sum(da, axis=0, keepdims=True)
        dcb_ref[...] += jnp.sum(dc, axis=0, keepdims=True)
        dcn, _, _ = conv_ln_bwd(cn_ref[...], dyan_ref[...])
        ext_dc[0:tm, :] = dc
        ext_dc[tm:tm + HALO, :] = jnp.where(i < nt - 1, dcn, 0.0)
        sig_g = _sigmoid(ag_ref[...])
        halo = hv_ref[...] * _sigmoid(hg_ref[...])
        ext_h[0:HALO, :] = jnp.where(i > 0, halo, 0.0)
        ext_h[HALO:HALO + tm, :] = av_ref[...] * sig_g
        for r in range(0, tm, CONV_ROWS):
            dcr = ext_dc[r:r + CONV_ROWS, :]
            acc = jnp.zeros((CONV_ROWS, ch), F32)
            for k in range(CONV_TAPS):
                lo = r + k + HALO - (CONV_TAPS - 1)
                prod = dcr * ext_h[lo:lo + CONV_ROWS, :]
                acc_cw[k] += jnp.sum(prod.reshape(CONV_ROWS // 8, 8, ch), axis=0)
                hi = r + (CONV_TAPS - 1) - k
                acc = acc + cw_ref[k:k + 1, :] * ext_dc[hi:hi + CONV_ROWS, :]
            sg_r = sig_g[r:r + CONV_ROWS, :]
            av_r = av_ref[r:r + CONV_ROWS, :]
            dp_ref[r:r + CONV_ROWS, 0:ch] = (acc * sg_r).astype(BF16)
            dp_ref[r:r + CONV_ROWS, ch:2 * ch] = (acc * av_r * sg_r * (1.0 - sg_r)).astype(BF16)

        @pl.when(i == nt - 1)
        def _():
            dcw_ref[...] = jnp.sum(acc_cw[...], axis=1)

        tril = (lax.broadcasted_iota(jnp.int32, (CHUNK, CHUNK), 0)
                >= lax.broadcasted_iota(jnp.int32, (CHUNK, CHUNK), 1)).astype(F32)
        for h in range(HEADS):
            sl = slice(h * HEAD_DIM, (h + 1) * HEAD_DIM)
            u, du_dx = _gelu_and_grad(bu_ref[:, sl])
            v, dv_dx = _gelu_and_grad(bv_ref[:, sl])
            xhv, rstdv = _ln_stats(v)
            gh = sg_ref[h:h + 1, :]
            vn3 = (xhv * gh + sb_ref[h:h + 1, :]).astype(BF16).reshape(nc, CHUNK, HEAD_DIM)
            wb = jnp.broadcast_to(w_ref[h][None], (nc, CHUNK, CHUNK))
            mixed = jnp.einsum("cts,csd->ctd", wb, vn3, preferred_element_type=F32) + bb_ref[h][None]
            dyb = dyb_ref[:, sl]
            d_u = dyb * mixed.reshape(tm, HEAD_DIM)
            dm = dyb * u
            dm3 = dm.reshape(nc, CHUNK, HEAD_DIM)
            dbs_ref[h:h + 1, :] += jnp.sum(jnp.sum(dm3, axis=0).T, axis=0, keepdims=True)
            dm3b = dm3.astype(BF16)
            dw_h = jnp.sum(jnp.einsum("ctd,csd->cts", dm3b, vn3, preferred_element_type=F32), axis=0)
            dw_ref[h] += dw_h * tril
            wtb = jnp.broadcast_to(wt_ref[h][None], (nc, CHUNK, CHUNK))
            d_vn = jnp.einsum("cst,ctd->csd", wtb, dm3b, preferred_element_type=F32).reshape(tm, HEAD_DIM)
            dsg_ref[h:h + 1, :] += jnp.sum(d_vn * xhv, axis=0, keepdims=True)
            dsb_ref[h:h + 1, :] += jnp.sum(d_vn, axis=0, keepdims=True)
            dv = _ln_bwd(d_vn * gh, xhv, rstdv)
            dp_ref[:, 2 * ch + h * HEAD_DIM:2 * ch + (h + 1) * HEAD_DIM] = (d_u * du_dx).astype(BF16)
            dp_ref[:, 3 * ch + h * HEAD_DIM:3 * ch + (h + 1) * HEAD_DIM] = (dv * dv_dx).astype(BF16)

    col = lambda cidx: (lambda i: (i, cidx))
    prev = lambda cidx: (lambda i: (jnp.maximum(i * hb - 1, 0), cidx))
    nxt = lambda i: (jnp.minimum((i + 1) * hb, last_halo), 0)
    fix2 = lambda i: (0, 0)
    fix3 = lambda i: (0, 0, 0)
    out_shape = [jax.ShapeDtypeStruct((t, 4 * ch), BF16), jax.ShapeDtypeStruct((CONV_TAPS, ch), F32),
                 jax.ShapeDtypeStruct((1, ch), F32), jax.ShapeDtypeStruct((1, ch), F32), jax.ShapeDtypeStruct((1, ch), F32),
                 jax.ShapeDtypeStruct((HEADS, HEAD_DIM), F32), jax.ShapeDtypeStruct((HEADS, HEAD_DIM), F32),
                 jax.ShapeDtypeStruct((HEADS, CHUNK, CHUNK), F32), jax.ShapeDtypeStruct((HEADS, CHUNK), F32)]
    out_specs = [pl.BlockSpec((tm, 4 * ch), lambda i: (i, 0)), pl.BlockSpec((CONV_TAPS, ch), fix2),
                 pl.BlockSpec((1, ch), fix2), pl.BlockSpec((1, ch), fix2), pl.BlockSpec((1, ch), fix2),
                 pl.BlockSpec((HEADS, HEAD_DIM), fix2), pl.BlockSpec((HEADS, HEAD_DIM), fix2),
                 pl.BlockSpec((HEADS, CHUNK, CHUNK), fix3), pl.BlockSpec((HEADS, CHUNK), fix2)]
    in_specs = [pl.BlockSpec((tm, ch), col(0)), pl.BlockSpec((tm, ch), col(1)), pl.BlockSpec((tm, ch), col(2)),
                pl.BlockSpec((tm, ch), col(3)), pl.BlockSpec((HALO, ch), prev(0)), pl.BlockSpec((HALO, ch), prev(1)),
                pl.BlockSpec((tm, ch), col(0)), pl.BlockSpec((HALO, ch), nxt),
                pl.BlockSpec((tm, ch), col(0)), pl.BlockSpec((HALO, ch), nxt), pl.BlockSpec((tm, ch), col(1)),
                pl.BlockSpec((CONV_TAPS, ch), fix2), pl.BlockSpec((1, ch), fix2), pl.BlockSpec((1, ch), fix2),
                pl.BlockSpec((HEADS, HEAD_DIM), fix2), pl.BlockSpec((HEADS, HEAD_DIM), fix2),
                pl.BlockSpec((HEADS, CHUNK, CHUNK), fix3), pl.BlockSpec((HEADS, CHUNK, CHUNK), fix3),
                pl.BlockSpec((HEADS, CHUNK, HEAD_DIM), fix3)]
    return _call(
        body, exch, name=name, grid=(nt,), out_shape=out_shape, in_specs=in_specs, out_specs=out_specs,
        scratch_shapes=[pltpu.VMEM((HALO + tm, ch), F32), pltpu.VMEM((tm + HALO, ch), F32),
                        pltpu.VMEM((CONV_TAPS, 8, ch), F32)],
        semantics=("arbitrary",),
    )(proj, proj, proj, proj, proj, proj, conv_c, conv_c, dy, dy, dy,
      conv_w, cln_g, cln_b, sln_g, sln_b, sg_wm, sg_wmt, sg_bb)


def _pair_sum(parts, from_sibling, c, name):
    _, r, cc = parts.shape
    tr = _tile(r, max(16, (1 << 20) // (2 * cc)), 16)

    def body(c_ref, p_ref, s_ref, o_ref):
        o_ref[...] = (p_ref[...].astype(F32) + s_ref[...].astype(F32)).astype(BF16)

    grid_spec = pltpu.PrefetchScalarGridSpec(
        num_scalar_prefetch=1, grid=(4, r // tr),
        in_specs=[pl.BlockSpec((1, tr, cc), lambda j, i, c_ref: (2 * j + c_ref[0], i, 0)),
                  pl.BlockSpec((1, tr, cc), lambda j, i, c_ref: (j, i, 0))],
        out_specs=pl.BlockSpec((1, tr, cc), lambda j, i, c_ref: (j, i, 0)))
    return pl.pallas_call(
        body, name=name, grid_spec=grid_spec, out_shape=jax.ShapeDtypeStruct((4, r, cc), BF16),
        compiler_params=_cparams("parallel", "parallel"),
    )(c, parts, from_sibling)


def _adamw_math(w, g, m, v):
    m = ADAM_B1 * m + (1.0 - ADAM_B1) * g
    v = ADAM_B2 * v + (1.0 - ADAM_B2) * (g * g)
    m_hat = m / (1.0 - ADAM_B1 ** ADAM_STEP)
    v_hat = v / (1.0 - ADAM_B2 ** ADAM_STEP)
    delta = -ADAM_LR * (m_hat / (jnp.sqrt(v_hat) + ADAM_EPS) + ADAM_WD * w)
    return delta, m, v


def _adamw_sharded(w, m, v, chip_parts, from_chips, chip, name):
    r, cc = w.shape
    tr = _tile(r, max(16, (1 << 19) // (4 * cc) * 2), 16)

    def body(j_ref, w_ref, m_ref, v_ref, q_ref, o_ref, g_out, d_out, m_out, v_out):
        g = q_ref[0].astype(F32)
        for k in range(3):
            g = g + o_ref[k].astype(F32)
        d, mm, vv = _adamw_math(w_ref[...], g, m_ref[...], v_ref[...])
        g_out[...] = g
        d_out[...] = d
        m_out[...] = mm
        v_out[...] = vv

    row = lambda i, j_ref: (i, 0)
    grid_spec = pltpu.PrefetchScalarGridSpec(
        num_scalar_prefetch=1, grid=(r // tr,),
        in_specs=[pl.BlockSpec((tr, cc), row), pl.BlockSpec((tr, cc), row), pl.BlockSpec((tr, cc), row),
                  pl.BlockSpec((1, tr, cc), lambda i, j_ref: (j_ref[0], i, 0)),
                  pl.BlockSpec((3, tr, cc), lambda i, j_ref: (0, i, 0))],
        out_specs=[pl.BlockSpec((tr, cc), row)] * 4)
    return pl.pallas_call(
        body, name=name, grid_spec=grid_spec, out_shape=[jax.ShapeDtypeStruct((r, cc), F32)] * 4,
        compiler_params=_cparams("parallel"),
    )(chip, w, m, v, chip_parts, from_chips)


def _adamw_small(w, g, m, v, name):
    r, cc = w.shape

    def body(w_ref, g_ref, m_ref, v_ref, d_out, m_out, v_out):
        d, mm, vv = _adamw_math(w_ref[...], g_ref[...], m_ref[...], v_ref[...])
        d_out[...] = d
        m_out[...] = mm
        v_out[...] = vv

    full = pl.BlockSpec((r, cc), lambda i: (0, 0))
    return pl.pallas_call(
        body, name=name, grid=(1,), out_shape=[jax.ShapeDtypeStruct((r, cc), F32)] * 3,
        in_specs=[full] * 4, out_specs=[full] * 3, compiler_params=_cparams("arbitrary"),
    )(w, g, m, v)


SMALL = ("ln1_g", "ln1_b", "conv_b", "conv_ln_g", "conv_ln_b", "sg_ln_g", "sg_ln_b", "sg_w", "sg_b",
         "ln2_g", "ln2_b", "ln3_g", "ln3_b")
ORDER = ("ffn1_w_gate_up", "ffn1_w_down", "ln1_g", "ln1_b", "mix_w_in", "conv_w", "conv_b", "conv_ln_g", "conv_ln_b",
         "sg_ln_g", "sg_ln_b", "sg_w", "sg_b", "mix_w_out", "ln2_g", "ln2_b", "ffn2_w_gate_up", "ffn2_w_down",
         "ln3_g", "ln3_b")


def _rows128(a):
    return a.reshape(-1, 128)


def kernel(x, ffn1_w_gate_up, ffn1_w_down, ln1_g, ln1_b, mix_w_in, conv_w, conv_b, conv_ln_g, conv_ln_b, sg_ln_g, sg_ln_b, sg_w, sg_b, mix_w_out, ln2_g, ln2_b, ffn2_w_gate_up, ffn2_w_down, ln3_g, ln3_b, loss_target, m_ffn1_w_gate_up, m_ffn1_w_down, m_ln1_g, m_ln1_b, m_mix_w_in, m_conv_w, m_conv_b, m_conv_ln_g, m_conv_ln_b, m_sg_ln_g, m_sg_ln_b, m_sg_w, m_sg_b, m_mix_w_out, m_ln2_g, m_ln2_b, m_ffn2_w_gate_up, m_ffn2_w_down, m_ln3_g, m_ln3_b, v_ffn1_w_gate_up, v_ffn1_w_down, v_ln1_g, v_ln1_b, v_mix_w_in, v_conv_w, v_conv_b, v_conv_ln_g, v_conv_ln_b, v_sg_ln_g, v_sg_ln_b, v_sg_w, v_sg_b, v_mix_w_out, v_ln2_g, v_ln2_b, v_ffn2_w_gate_up, v_ffn2_w_down, v_ln3_g, v_ln3_b):
    args = dict(locals())
    w = {n: args[n][0] for n in ORDER}
    mom = {n: args["m_" + n][0] for n in ORDER}
    var = {n: args["v_" + n][0] for n in ORDER}
    x0 = x[0]
    target = loss_target[0]
    t, d = x0.shape
    my_x, my_y, my_c = lax.axis_index("x"), lax.axis_index("y"), lax.axis_index("c")
    my_chip = (2 * my_x + my_y).astype(jnp.int32).reshape(1)
    my_core = my_c.astype(jnp.int32).reshape(1)
    me = 4 * my_x + 2 * my_y + my_c

    big = ("ffn1_w_gate_up", "ffn1_w_down", "mix_w_in", "mix_w_out", "ffn2_w_gate_up", "ffn2_w_down")
    sh = {n: w[n].astype(BF16) for n in big}
    f2s = sh["ffn2_w_gate_up"].shape[1]
    wgu1, wd1, conv_w_all = _allgather([sh["ffn1_w_gate_up"], sh["ffn1_w_down"], w["conv_w"]], [True, False, False])
    wd1 = wd1.reshape(-1, d)
    conv_w_full = jnp.transpose(conv_w_all, (1, 0, 2)).reshape(CONV_TAPS, CONV_CH)
    tril = jnp.tril(jnp.ones((CHUNK, CHUNK), F32))
    sg_wm = w["sg_w"] * tril
    sg_wm_b = sg_wm.astype(BF16)
    sg_wmt_b = jnp.swapaxes(sg_wm, 1, 2).astype(BF16)
    sg_bb = jnp.broadcast_to(w["sg_b"][:, :, None], (HEADS, CHUNK, HEAD_DIM))
    row = lambda a: a.reshape(1, -1)

    x0t, _ = _transpose_bf16(x0, "x0_transpose")
    (g1, u1, h1t, z1, x1), ((g_in, g_out, g_gu2),) = _ffn_fwd(
        x0, wgu1, wd1, row(w["ln1_g"]), row(w["ln1_b"]), "ffn1_fwd",
        exch=[_gather_first([sh["mix_w_in"], sh["mix_w_out"], sh["ffn2_w_gate_up"]], [False, False, True])])
    x1t, ((w_in, w_out),) = _transpose_bf16(
        x1, "x1_transpose", exch=[_gather_forward([g_in, g_out], [False, False], [None, None])])
    w_out = w_out.reshape(-1, d)
    proj, ((g_d2,), (wgu2,)) = _mix_in_proj(
        x1, w_in, "mix_in_fwd",
        exch=[_gather_first([sh["ffn2_w_down"]], [False]), _gather_forward([g_gu2], [True], [f2s])])
    (y, conv_c), ((wd2,),) = _mixer_fwd(
        proj, conv_w_full, row(w["conv_b"]), row(w["conv_ln_g"]), row(w["conv_ln_b"]),
        w["sg_ln_g"], w["sg_ln_b"], sg_wm_b, sg_bb, "mixer_fwd", exch=[_gather_forward([g_d2], [False], [None])])
    wd2 = wd2.reshape(-1, d)
    yt, _ = _transpose_bf16(y, "y_transpose")
    z2, x2 = _mix_out_fwd(y, w_out, x1, row(w["ln2_g"]), row(w["ln2_b"]), "mix_out_fwd")
    x2t, _ = _transpose_bf16(x2, "x2_transpose")
    (g2, u2, h2t, z3, x3), _ = _ffn_fwd(x2, wgu2, wd2, row(w["ln3_g"]), row(w["ln3_b"]), "ffn2_fwd")

    f = wd1.shape[0]
    dn = _tile(d, 1024, 128)
    grads = {}
    pair = lambda p, s, label: _pair_sum(p, s, my_core, "pair_sum_" + label)
    dy3, loss_tile = _loss_grad(x3, target, "loss_grad")
    (dz3, do2, grads["ln3_g"], grads["ln3_b"]), _ = _ln_bwd_call(z3, dy3, row(w["ln3_g"]), 0.5, "ln3_bwd")
    p_d2, _ = _weight_grad(h2t, do2, dn, 512, "ffn2_dw_down")
    p_d2 = p_d2.reshape(N_DEV, f // N_DEV, d)
    (dg2, du2, dx2), ((s_d2,),) = _ffn_bwd(dz3, do2, g2, u2, wgu2, wd2, "ffn2_bwd", exch=[_rs_sibling([p_d2])])
    q_d2 = pair(p_d2, s_d2, "ffn2_down")
    p_gu2, ((r_d2,),) = _weight_grad(x2t, dg2, f2s, 512, "ffn2_dw_gate", blocks=N_DEV, exch=[_rs_chips([q_d2])])
    p_gu2, _ = _weight_grad(x2t, du2, f2s, 512, "ffn2_dw_up", blocks=N_DEV, block_offset=4, into=p_gu2)
    (dz2, dz2b, grads["ln2_g"], grads["ln2_b"]), ((s_gu2,),) = _ln_bwd_call(
        z2, dx2, row(w["ln2_g"]), 1.0, "ln2_bwd", exch=[_rs_sibling([p_gu2])])
    q_gu2 = pair(p_gu2, s_gu2, "ffn2_gate_up")
    dy = _mix_out_bwd(dz2b, w_out, "mix_out_bwd")
    p_out, _ = _weight_grad(yt, dz2b, dn, 512, "mix_out_dw")
    p_out = p_out.reshape(N_DEV, -1, d)
    (dproj, grads["conv_w"], grads["conv_b"], grads["conv_ln_g"], grads["conv_ln_b"], grads["sg_ln_g"],
     grads["sg_ln_b"], grads["sg_w"], grads["sg_b"]), ((r_gu2,),) = _mixer_bwd(
        proj, conv_c, dy, conv_w_full, row(w["conv_ln_g"]), row(w["conv_ln_b"]), w["sg_ln_g"], w["sg_ln_b"],
        sg_wm_b, sg_wmt_b, sg_bb, "mixer_bwd", exch=[_rs_chips([q_gu2])])
    dx1, ((s_out,),) = _mix_in_bwd(dproj, w_in, dz2, "mix_in_bwd", exch=[_rs_sibling([p_out])])
    p_in, _ = _weight_grad(x1t, dproj, w_in.shape[2], 512, "mix_in_dw", blocks=N_DEV)
    (dz1, do1, grads["ln1_g"], grads["ln1_b"]), ((s_in,),) = _ln_bwd_call(
        z1, dx1, row(w["ln1_g"]), 0.5, "ln1_bwd", exch=[_rs_sibling([p_in])])
    q_out = pair(p_out, s_out, "mix_out")
    q_in = pair(p_in, s_in, "mix_in")
    p_d1, _ = _weight_grad(h1t, do1, dn, 512, "ffn1_dw_down")
    p_d1 = p_d1.reshape(N_DEV, f // N_DEV, d)
    (dg1, du1), ((s_d1,), (r_out, r_in)) = _ffn_bwd_act(
        do1, g1, u1, wd1, "ffn1_bwd_act", exch=[_rs_sibling([p_d1]), _rs_chips([q_out, q_in])])
    q_d1 = pair(p_d1, s_d1, "ffn1_down")
    p_gu1, ((r_d1,),) = _weight_grad(x0t, dg1, f2s, 512, "ffn1_dw_gate", blocks=N_DEV, exch=[_rs_chips([q_d1])])
    p_gu1, _ = _weight_grad(x0t, du1, f2s, 512, "ffn1_dw_up", blocks=N_DEV, block_offset=4, into=p_gu1)
    (s_gu1,) = _exchange_alone(_rs_sibling([p_gu1]), "ffn1_gate_up_sibling_exchange")
    q_gu1 = pair(p_gu1, s_gu1, "ffn1_gate_up")
    (grad_x,), ((r_gu1,),) = _ffn_bwd_dx(dz1, dg1, du1, wgu1, "ffn1_bwd_dx", exch=[_rs_chips([q_gu1])])

    chip_parts = [q_gu1, q_d1, q_in, q_out, q_gu2, q_d2]
    from_chips = [r_gu1, r_d1, r_in, r_out, r_gu2, r_d2]
    out = {}
    for k, n in enumerate(big):
        out[n] = _adamw_sharded(w[n], mom[n], var[n], chip_parts[k], from_chips[k], my_chip, "adamw_" + n)

    small_parts = [_rows128(grads[n]) for n in SMALL]
    cw_rows = CONV_TAPS * CONV_CH // 128
    packed = jnp.concatenate(small_parts + [_rows128(grads["conv_w"]), loss_tile], axis=0)
    total = _small_allreduce(packed)
    offs = [0]
    for p in small_parts:
        offs.append(offs[-1] + p.shape[0])
    n_small = offs[-1]
    loss = total[n_small + cw_rows, 0]
    g_conv_w = lax.dynamic_slice_in_dim(total[n_small:n_small + cw_rows].reshape(CONV_TAPS, CONV_CH),
                                        me * (CONV_CH // N_DEV), CONV_CH // N_DEV, axis=1)
    pad8 = lambda a: jnp.pad(a, ((0, -a.shape[0] % 8), (0, 0)))
    pack = lambda tree, cw: jnp.concatenate([_rows128(tree[n]) for n in SMALL] + [pad8(cw)], axis=0)
    g_pack = jnp.concatenate([total[:n_small], pad8(g_conv_w)], axis=0)
    d_pack, m_pack, v_pack = _adamw_small(pack(w, w["conv_w"]), g_pack, pack(mom, mom["conv_w"]),
                                          pack(var, var["conv_w"]), "adamw_small")
    for k, n in enumerate(SMALL):
        sl = slice(offs[k], offs[k + 1])
        shp = w[n].shape
        out[n] = (total[sl].reshape(shp), d_pack[sl].reshape(shp), m_pack[sl].reshape(shp), v_pack[sl].reshape(shp))
    sl = slice(n_small, n_small + CONV_TAPS)
    out["conv_w"] = (g_conv_w, d_pack[sl], m_pack[sl], v_pack[sl])

    lead = lambda a: a[None]
    res = [loss, grad_x[None]]
    for kind in range(4):
        res += [lead(out[n][kind]) for n in ORDER]
    return tuple(res)
```

```python
import functools
import math

import jax
import jax.numpy as jnp
from jax import lax
from jax.experimental import pallas as pl
from jax.experimental.pallas import tpu as pltpu

F32, BF16 = jnp.float32, jnp.bfloat16
MESH = pl.DeviceIdType.MESH
ANY = pl.BlockSpec(memory_space=pl.ANY)

N_DEV = 8
LN_EPS = 1e-5
ALPHA = 2.0 ** 0.25
CONV_CH = 1024
CONV_TAPS = 31
HALO = 32
HEADS = 8
HEAD_DIM = 128
CHUNK = 128
ADAM_LR, ADAM_B1, ADAM_B2, ADAM_EPS, ADAM_WD, ADAM_STEP = 0.001, 0.9, 0.999, 1e-08, 0.01, 10
V7X_VMEM_LIMIT = 56 * 2 ** 20


def _cparams(*sem):
    return pltpu.CompilerParams(dimension_semantics=sem, vmem_limit_bytes=V7X_VMEM_LIMIT)


def _tile(n, pref, mult):
    best = None
    for t in range(mult, min(n, pref) + 1, mult):
        if n % t == 0:
            best = t
    return best if best is not None else n


def _dot(a, b):
    return jnp.dot(a, b, preferred_element_type=F32)


def _dot_nt(a, b):
    return lax.dot_general(a, b, (((1,), (1,)), ((), ())), preferred_element_type=F32)


def _sigmoid(x):
    return 1.0 / (1.0 + jnp.exp(-x))


def _ln_stats(z):
    mu = jnp.mean(z, axis=-1, keepdims=True)
    zc = z - mu
    var = jnp.mean(zc * zc, axis=-1, keepdims=True)
    rstd = lax.rsqrt(var + LN_EPS)
    return zc * rstd, rstd


def _ln(z, g, b):
    xh, _ = _ln_stats(z)
    return xh * g + b


def _ln_bwd(dxh, xh, rstd):
    m1 = jnp.mean(dxh, axis=-1, keepdims=True)
    m2 = jnp.mean(dxh * xh, axis=-1, keepdims=True)
    return rstd * (dxh - m1 - xh * m2)


_GK = math.sqrt(2.0 / math.pi)
_GA = 0.044715


def _gelu_and_grad(x):
    x2 = x * x
    t = jnp.tanh(_GK * (x + _GA * x * x2))
    y = 0.5 * x * (1.0 + t)
    dy = 0.5 * (1.0 + t) + 0.5 * x * (1.0 - t * t) * (_GK * (1.0 + 3.0 * _GA * x2))
    return y, dy


def _silu_grad(a):
    s = _sigmoid(a)
    return s * (1.0 + a * (1.0 - s))


def _place():
    return lax.axis_index("x"), lax.axis_index("y"), lax.axis_index("c")


def _other_chips(x, y):
    return [(1 - x, y), (x, 1 - y), (1 - x, 1 - y)]


def _allgather(shards, col_major):
    n = len(shards)

    def body(*refs):
        srcs, dsts = refs[:n], refs[n:2 * n]
        send_sems, recv_sems, local_sems = refs[2 * n:]
        x, y, c = _place()
        me, sib = (x, y, c), (x, y, 1 - c)
        chips = _other_chips(x, y)

        def slot(w, p):
            k = 4 * p[0] + 2 * p[1] + p[2]
            if col_major[w]:
                cols = shards[w].shape[1]
                return dsts[w].at[:, pl.ds(pl.multiple_of(k * cols, 128), cols)]
            return dsts[w].at[k]

        def copy(w, s, block, to, from_src=False):
            return pltpu.make_async_remote_copy(
                src_ref=srcs[w] if from_src else slot(w, block), dst_ref=slot(w, block),
                send_sem=send_sems.at[7 * w + s], recv_sem=recv_sems.at[7 * w + s],
                device_id=to, device_id_type=MESH)

        started = []
        for w in range(n):
            mine = pltpu.make_async_copy(srcs[w], slot(w, me), local_sems.at[w])
            mine.start()
            started.append(mine)
        first = []
        for w in range(n):
            first.append(copy(w, 0, me, sib, from_src=True))
            first += [copy(w, 1 + j, me, (*chip, c), from_src=True) for j, chip in enumerate(chips)]
        for cp in first:
            cp.start()
        passed = []
        for w in range(n):
            for j, chip in enumerate(chips):
                copy(w, 1 + j, (*chip, c), me).wait_recv()
                fwd = copy(w, 4 + j, (*chip, c), sib)
                fwd.start()
                passed.append(fwd)
        for w in range(n):
            copy(w, 0, sib, me).wait_recv()
            for j, chip in enumerate(chips):
                copy(w, 4 + j, (*chip, 1 - c), me).wait_recv()
        for cp in first + passed:
            cp.wait_send()
        for cp in started:
            cp.wait()

    out_shape = []
    for w, s in enumerate(shards):
        r, cdim = s.shape
        out_shape.append(jax.ShapeDtypeStruct((r, N_DEV * cdim) if col_major[w] else (N_DEV, r, cdim), s.dtype))
    return pl.pallas_call(
        body, name="weights_allgather", out_shape=out_shape,
        in_specs=[ANY] * n, out_specs=[ANY] * n,
        scratch_shapes=[pltpu.SemaphoreType.DMA((7 * n,)), pltpu.SemaphoreType.DMA((7 * n,)),
                        pltpu.SemaphoreType.DMA((n,))],
    )(*shards)


class _Exchange:
    def __init__(self, ins, io, new, n_sems, n_local, make):
        self.ins, self.io, self.new = list(ins), list(io), list(new)
        self.n_sems, self.n_local, self.make = n_sems, n_local, make


def _block_slot(ref, col_major, cols, place):
    k = 4 * place[0] + 2 * place[1] + place[2]
    if col_major:
        return ref.at[:, pl.ds(pl.multiple_of(k * cols, 128), cols)]
    return ref.at[k]


def _gather_first(shards, col_major):
    n = len(shards)
    new = [jax.ShapeDtypeStruct((s.shape[0], N_DEV * s.shape[1]) if cm else (N_DEV,) + s.shape, s.dtype)
           for s, cm in zip(shards, col_major)]

    def make(in_refs, io_refs, new_refs, send_sems, recv_sems, local_sems):
        x, y, c = _place()
        targets = [(x, y, 1 - c)] + [(*chip, c) for chip in _other_chips(x, y)]
        copies = []
        for w in range(n):
            slot = _block_slot(new_refs[w], col_major[w], shards[w].shape[1], (x, y, c))
            copies.append(pltpu.make_async_copy(in_refs[w], slot, local_sems.at[w]))
            for s, to in enumerate(targets):
                copies.append(pltpu.make_async_remote_copy(
                    src_ref=in_refs[w], dst_ref=slot, send_sem=send_sems.at[4 * w + s],
                    recv_sem=recv_sems.at[4 * w + s], device_id=to, device_id_type=MESH))
        return copies

    return _Exchange(shards, [], new, 4 * n, n, make)


def _gather_forward(gathered, col_major, cols):
    n = len(gathered)

    def make(in_refs, io_refs, new_refs, send_sems, recv_sems, local_sems):
        x, y, c = _place()
        copies = []
        for w in range(n):
            for j, chip in enumerate(_other_chips(x, y)):
                slot = _block_slot(io_refs[w], col_major[w], cols[w], (*chip, c))
                copies.append(pltpu.make_async_remote_copy(
                    src_ref=slot, dst_ref=slot, send_sem=send_sems.at[3 * w + j], recv_sem=recv_sems.at[3 * w + j],
                    device_id=(x, y, 1 - c), device_id_type=MESH))
        return copies

    return _Exchange([], gathered, [], 3 * n, 0, make)


def _rs_sibling(parts):
    n = len(parts)

    def make(in_refs, io_refs, new_refs, send_sems, recv_sems, local_sems):
        x, y, c = _place()
        copies = []
        for w in range(n):
            for j in range(4):
                copies.append(pltpu.make_async_remote_copy(
                    src_ref=in_refs[w].at[2 * j + (1 - c)], dst_ref=new_refs[w].at[j],
                    send_sem=send_sems.at[4 * w + j], recv_sem=recv_sems.at[4 * w + j],
                    device_id=(x, y, 1 - c), device_id_type=MESH))
        return copies

    return _Exchange(parts, [], [jax.ShapeDtypeStruct((4,) + p.shape[1:], p.dtype) for p in parts], 4 * n, 0, make)


def _rs_chips(chip_parts):
    n = len(chip_parts)

    def make(in_refs, io_refs, new_refs, send_sems, recv_sems, local_sems):
        x, y, c = _place()
        copies = []
        for w in range(n):
            for rel, (px, py) in enumerate(_other_chips(x, y)):
                copies.append(pltpu.make_async_remote_copy(
                    src_ref=in_refs[w].at[2 * px + py], dst_ref=new_refs[w].at[rel],
                    send_sem=send_sems.at[3 * w + rel], recv_sem=recv_sems.at[3 * w + rel],
                    device_id=(px, py, c), device_id_type=MESH))
        return copies

    return _Exchange(chip_parts, [], [jax.ShapeDtypeStruct((3,) + p.shape[1:], p.dtype) for p in chip_parts],
                     3 * n, 0, make)


def _call(body, exch, *, name, grid, in_specs, out_specs, out_shape, scratch_shapes=(), semantics,
          input_output_aliases=None):
    exch = list(exch)
    in_specs, out_specs, out_shape = list(in_specs), list(out_specs), list(out_shape)
    scratch_shapes = list(scratch_shapes)
    if not exch:
        fn = pl.pallas_call(body, name=name, grid=grid, in_specs=in_specs, out_specs=out_specs, out_shape=out_shape,
                            scratch_shapes=scratch_shapes, input_output_aliases=input_output_aliases or {},
                            compiler_params=_cparams(*semantics))
        return lambda *args: (fn(*args), [])
    n_in, n_out, n_scr = len(in_specs), len(out_specs), len(scratch_shapes)
    aliases = dict(input_output_aliases or {})
    all_in, all_out_specs, all_out_shape, all_scr = list(in_specs), list(out_specs), list(out_shape), list(scratch_shapes)
    extra_args = []
    for ex in exch:
        for k, a in enumerate(ex.io):
            aliases[len(all_in) + len(ex.ins) + k] = len(all_out_specs) + k
        all_in += [ANY] * (len(ex.ins) + len(ex.io))
        extra_args += ex.ins + ex.io
        all_out_specs += [ANY] * (len(ex.io) + len(ex.new))
        all_out_shape += [jax.ShapeDtypeStruct(a.shape, a.dtype) for a in ex.io] + ex.new
        all_scr += [pltpu.SemaphoreType.DMA((ex.n_sems,)), pltpu.SemaphoreType.DMA((ex.n_sems,)),
                    pltpu.SemaphoreType.DMA((max(ex.n_local, 1),))]

    def wrapped(*refs):
        pos = n_in
        ex_in = []
        for ex in exch:
            k = len(ex.ins) + len(ex.io)
            ex_in.append(refs[pos:pos + k])
            pos += k
        outs = refs[pos:pos + n_out]
        pos += n_out
        ex_out = []
        for ex in exch:
            k = len(ex.io) + len(ex.new)
            ex_out.append(refs[pos:pos + k])
            pos += k
        scr = refs[pos:pos + n_scr]
        pos += n_scr
        sems = [refs[pos + 3 * k:pos + 3 * k + 3] for k in range(len(exch))]
        first = functools.reduce(jnp.logical_and, [pl.program_id(a) == 0 for a in range(len(grid))])
        last = functools.reduce(jnp.logical_and, [pl.program_id(a) == g - 1 for a, g in enumerate(grid)])

        def copies():
            out = []
            for ex, ei, eo, es in zip(exch, ex_in, ex_out, sems):
                out += ex.make(ei[:len(ex.ins)], eo[:len(ex.io)], eo[len(ex.io):], *es)
            return out

        @pl.when(first)
        def _():
            for cp in copies():
                cp.start()

        body(*refs[:n_in], *outs, *scr)

        @pl.when(last)
        def _():
            for cp in copies():
                cp.wait()

    fn = pl.pallas_call(wrapped, name=name, grid=grid, in_specs=all_in, out_specs=all_out_specs,
                        out_shape=all_out_shape, scratch_shapes=all_scr, input_output_aliases=aliases,
                        compiler_params=_cparams(*(["arbitrary"] * len(grid))))

    def run(*args):
        res = fn(*args, *extra_args)
        outs, pos, ex_res = res[:n_out], n_out, []
        for ex in exch:
            k = len(ex.io) + len(ex.new)
            ex_res.append(list(res[pos:pos + k]))
            pos += k
        return outs, ex_res

    return run


def _exchange_alone(ex, name):
    def body():
        pass

    _, res = _call(body, [ex], name=name, grid=(1,), in_specs=[], out_specs=[], out_shape=[], semantics=("arbitrary",))()
    return res[0]


def _small_gather(part):
    def make(in_refs, io_refs, new_refs, send_sems, recv_sems, local_sems):
        x, y, c = _place()
        slot = new_refs[0].at[4 * x + 2 * y + c]
        copies = [pltpu.make_async_copy(in_refs[0], slot, local_sems.at[0])]
        for d in range(1, N_DEV):
            peer = (1 - x if d & 4 else x, 1 - y if d & 2 else y, 1 - c if d & 1 else c)
            copies.append(pltpu.make_async_remote_copy(
                src_ref=in_refs[0], dst_ref=slot, send_sem=send_sems.at[d - 1], recv_sem=recv_sems.at[d - 1],
                device_id=peer, device_id_type=MESH))
        return copies

    return _Exchange([part], [], [jax.ShapeDtypeStruct((N_DEV,) + part.shape, part.dtype)], N_DEV - 1, 1, make)


def _sum_over_devices(parts):
    _, rows, lanes = parts.shape

    def body(p_ref, o_ref):
        acc = p_ref[0]
        for k in range(1, N_DEV):
            acc = acc + p_ref[k]
        o_ref[...] = acc

    return pl.pallas_call(
        body, name="small_grads_sum", grid=(1,), out_shape=jax.ShapeDtypeStruct((rows, lanes), F32),
        in_specs=[pl.BlockSpec((N_DEV, rows, lanes), lambda i: (0, 0, 0))],
        out_specs=pl.BlockSpec((rows, lanes), lambda i: (0, 0)),
        compiler_params=_cparams("arbitrary"),
    )(parts)


def _transpose_bf16(a, name, exch=()):
    r, c = a.shape
    tr, tc = _tile(r, 512, 128), _tile(c, 512, 128)

    def body(a_ref, o_ref):
        o_ref[...] = a_ref[...].astype(F32).T.astype(BF16)

    (out,), ex = _call(
        body, exch, name=name, grid=(r // tr, c // tc), out_shape=[jax.ShapeDtypeStruct((c, r), BF16)],
        in_specs=[pl.BlockSpec((tr, tc), lambda i, j: (i, j))],
        out_specs=[pl.BlockSpec((tc, tr), lambda i, j: (j, i))],
        semantics=("parallel", "parallel"),
    )(a)
    return out, ex


def _ffn_fwd(x, wgu, wd, ln_g, ln_b, name, exch=(), with_ln=True):
    t, d = x.shape
    f = wd.shape[0]
    tm, tf = _tile(t, 512, 128), _tile(f, 512, 128)
    nf = f // tf

    def body(x_ref, wg_ref, wu_ref, wd_ref, g_ref, b_ref, go_ref, uo_ref, ht_ref, z_ref, *rest):
        xn_ref = rest[0] if with_ln else None
        xb, acc = rest[-2:]
        j = pl.program_id(1)

        @pl.when(j == 0)
        def _():
            xb[...] = x_ref[...].astype(BF16)
            acc[...] = jnp.zeros_like(acc)

        g = _dot(xb[...], wg_ref[...])
        u = _dot(xb[...], wu_ref[...])
        h = g * _sigmoid(g) * u
        go_ref[...] = g.astype(BF16)
        uo_ref[...] = u.astype(BF16)
        ht_ref[...] = h.T.astype(BF16)
        acc[...] += _dot(h.astype(BF16), wd_ref[...])

        @pl.when(j == nf - 1)
        def _():
            z = ALPHA * x_ref[...] + 0.5 * acc[...]
            z_ref[...] = z
            if with_ln:
                xn_ref[...] = _ln(z, g_ref[...], b_ref[...])

    row = lambda i, j: (i, 0)
    n_td = 2 if with_ln else 1
    return _call(
        body, exch, name=name, grid=(t // tm, nf),
        out_shape=[jax.ShapeDtypeStruct((t, f), BF16), jax.ShapeDtypeStruct((t, f), BF16),
                   jax.ShapeDtypeStruct((f, t), BF16)] + [jax.ShapeDtypeStruct((t, d), F32)] * n_td,
        in_specs=[pl.BlockSpec((tm, d), row),
                  pl.BlockSpec((d, tf), lambda i, j: (0, j)),
                  pl.BlockSpec((d, tf), lambda i, j: (0, j + nf)),
                  pl.BlockSpec((tf, d), lambda i, j: (j, 0)),
                  pl.BlockSpec((1, d), lambda i, j: (0, 0)),
                  pl.BlockSpec((1, d), lambda i, j: (0, 0))],
        out_specs=[pl.BlockSpec((tm, tf), lambda i, j: (i, j)), pl.BlockSpec((tm, tf), lambda i, j: (i, j)),
                   pl.BlockSpec((tf, tm), lambda i, j: (j, i))] + [pl.BlockSpec((tm, d), row)] * n_td,
        scratch_shapes=[pltpu.VMEM((tm, d), BF16), pltpu.VMEM((tm, d), F32)],
        semantics=("parallel", "arbitrary"),
    )(x, wgu, wgu, wd, ln_g, ln_b)


def _ffn_act_grads(dh, g_ref, u_ref):
    gg = g_ref[...].astype(F32)
    uu = u_ref[...].astype(F32)
    s = _sigmoid(gg)
    du = (dh * (gg * s)).astype(BF16)
    dg = (dh * uu * (s * (1.0 + gg * (1.0 - s)))).astype(BF16)
    return dg, du


def _ffn_bwd(dz, do, g, u, wgu, wd, name, exch=()):
    t, d = dz.shape
    f = wd.shape[0]
    tm, tf = _tile(t, 512, 128), _tile(f, 512, 128)
    nf = f // tf

    def body(dz_ref, do_ref, g_ref, u_ref, wg_ref, wu_ref, wd_ref, dg_ref, du_ref, dx_ref, acc):
        j = pl.program_id(1)

        @pl.when(j == 0)
        def _():
            acc[...] = jnp.zeros_like(acc)

        dg, du = _ffn_act_grads(_dot_nt(do_ref[...], wd_ref[...]), g_ref, u_ref)
        dg_ref[...] = dg
        du_ref[...] = du
        acc[...] += _dot_nt(dg, wg_ref[...]) + _dot_nt(du, wu_ref[...])

        @pl.when(j == nf - 1)
        def _():
            dx_ref[...] = ALPHA * dz_ref[...] + acc[...]

    row = lambda i, j: (i, 0)
    tile = lambda i, j: (i, j)
    return _call(
        body, exch, name=name, grid=(t // tm, nf),
        out_shape=[jax.ShapeDtypeStruct((t, f), BF16), jax.ShapeDtypeStruct((t, f), BF16),
                   jax.ShapeDtypeStruct((t, d), F32)],
        in_specs=[pl.BlockSpec((tm, d), row), pl.BlockSpec((tm, d), row),
                  pl.BlockSpec((tm, tf), tile), pl.BlockSpec((tm, tf), tile),
                  pl.BlockSpec((d, tf), lambda i, j: (0, j)),
                  pl.BlockSpec((d, tf), lambda i, j: (0, j + nf)),
                  pl.BlockSpec((tf, d), lambda i, j: (j, 0))],
        out_specs=[pl.BlockSpec((tm, tf), tile), pl.BlockSpec((tm, tf), tile), pl.BlockSpec((tm, d), row)],
        scratch_shapes=[pltpu.VMEM((tm, d), F32)],
        semantics=("parallel", "arbitrary"),
    )(dz, do, g, u, wgu, wgu, wd)


def _ffn_bwd_act(do, g, u, wd, name, exch=()):
    t, d = do.shape
    f = wd.shape[0]
    tm, tf = _tile(t, 512, 128), _tile(f, 512, 128)

    def body(do_ref, g_ref, u_ref, wd_ref, dg_ref, du_ref):
        dg, du = _ffn_act_grads(_dot_nt(do_ref[...], wd_ref[...]), g_ref, u_ref)
        dg_ref[...] = dg
        du_ref[...] = du

    tile = lambda i, j: (i, j)
    return _call(
        body, exch, name=name, grid=(t // tm, f // tf),
        out_shape=[jax.ShapeDtypeStruct((t, f), BF16), jax.ShapeDtypeStruct((t, f), BF16)],
        in_specs=[pl.BlockSpec((tm, d), lambda i, j: (i, 0)), pl.BlockSpec((tm, tf), tile),
                  pl.BlockSpec((tm, tf), tile), pl.BlockSpec((tf, d), lambda i, j: (j, 0))],
        out_specs=[pl.BlockSpec((tm, tf), tile), pl.BlockSpec((tm, tf), tile)],
        semantics=("parallel", "parallel"),
    )(do, g, u, wd)


def _ffn_bwd_dx(dz, dg, du, wgu, name, exch=()):
    t, d = dz.shape
    f = dg.shape[1]
    tm, tf = _tile(t, 512, 128), _tile(f, 512, 128)
    nf = f // tf

    def body(dz_ref, dg_ref, du_ref, wg_ref, wu_ref, dx_ref, acc):
        j = pl.program_id(1)

        @pl.when(j == 0)
        def _():
            acc[...] = jnp.zeros_like(acc)

        acc[...] += _dot_nt(dg_ref[...], wg_ref[...]) + _dot_nt(du_ref[...], wu_ref[...])

        @pl.when(j == nf - 1)
        def _():
            dx_ref[...] = ALPHA * dz_ref[...] + acc[...]

    row = lambda i, j: (i, 0)
    tile = lambda i, j: (i, j)
    return _call(
        body, exch, name=name, grid=(t // tm, nf), out_shape=[jax.ShapeDtypeStruct((t, d), F32)],
        in_specs=[pl.BlockSpec((tm, d), row), pl.BlockSpec((tm, tf), tile), pl.BlockSpec((tm, tf), tile),
                  pl.BlockSpec((d, tf), lambda i, j: (0, j)), pl.BlockSpec((d, tf), lambda i, j: (0, j + nf))],
        out_specs=[pl.BlockSpec((tm, d), row)],
        scratch_shapes=[pltpu.VMEM((tm, d), F32)],
        semantics=("parallel", "arbitrary"),
    )(dz, dg, du, wgu, wgu)


def _weight_grad(at, b, tn, tmm, name, blocks=None, block_offset=0, into=None, exch=()):
    m, t = at.shape
    nn = b.shape[1]
    tmm = _tile(m, tmm, 16)
    assert nn % tn == 0

    def body(*refs):
        at_ref, b_ref, o_ref = refs[0], refs[1], refs[-1]
        r = _dot(at_ref[...], b_ref[...]).astype(BF16)
        if blocks is None:
            o_ref[...] = r
        else:
            o_ref[0] = r

    in_specs = [pl.BlockSpec((tmm, t), lambda n, i: (i, 0)), pl.BlockSpec((t, tn), lambda n, i: (0, n))]
    args = [at, b]
    aliases = {}
    if into is not None:
        in_specs.append(ANY)
        args.append(into)
        aliases = {2: 0}
    if blocks is None:
        out_shape = jax.ShapeDtypeStruct((m, nn), BF16)
        out_spec = pl.BlockSpec((tmm, tn), lambda n, i: (i, n))
    else:
        out_shape = jax.ShapeDtypeStruct((blocks, m, tn), BF16)
        out_spec = pl.BlockSpec((1, tmm, tn), lambda n, i: (n + block_offset, i, 0))
    (out,), ex = _call(
        body, exch, name=name, grid=(nn // tn, m // tmm), out_shape=[out_shape],
        in_specs=in_specs, out_specs=[out_spec], input_output_aliases=aliases,
        semantics=("parallel", "parallel"),
    )(*args)
    return out, ex


def _mix_in_proj(x, w_in, name, exch=()):
    t, d = x.shape
    nb, _, cb = w_in.shape
    tm = _tile(t, 512, 128)

    def body(x_ref, w_ref, o_ref, xb):
        @pl.when(pl.program_id(1) == 0)
        def _():
            xb[...] = x_ref[...].astype(BF16)

        o_ref[...] = _dot(xb[...], w_ref[0])

    (out,), ex = _call(
        body, exch, name=name, grid=(t // tm, nb), out_shape=[jax.ShapeDtypeStruct((t, nb * cb), F32)],
        in_specs=[pl.BlockSpec((tm, d), lambda i, k: (i, 0)), pl.BlockSpec((1, d, cb), lambda i, k: (k, 0, 0))],
        out_specs=[pl.BlockSpec((tm, cb), lambda i, k: (i, k))],
        scratch_shapes=[pltpu.VMEM((tm, d), BF16)],
        semantics=("parallel", "arbitrary"),
    )(x, w_in)
    return out, ex


def _mix_in_bwd(dproj, w_in, dz, name, exch=()):
    t, d = dz.shape
    nb, _, cb = w_in.shape
    tm = _tile(t, 512, 128)

    def body(dp_ref, w_ref, dz_ref, dx_ref, acc):
        k = pl.program_id(1)

        @pl.when(k == 0)
        def _():
            acc[...] = jnp.zeros_like(acc)

        acc[...] += _dot_nt(dp_ref[...], w_ref[0])

        @pl.when(k == nb - 1)
        def _():
            dx_ref[...] = ALPHA * dz_ref[...] + acc[...]

    (out,), ex = _call(
        body, exch, name=name, grid=(t // tm, nb), out_shape=[jax.ShapeDtypeStruct((t, d), F32)],
        in_specs=[pl.BlockSpec((tm, cb), lambda i, k: (i, k)), pl.BlockSpec((1, d, cb), lambda i, k: (k, 0, 0)),
                  pl.BlockSpec((tm, d), lambda i, k: (i, 0))],
        out_specs=[pl.BlockSpec((tm, d), lambda i, k: (i, 0))],
        scratch_shapes=[pltpu.VMEM((tm, d), F32)],
        semantics=("parallel", "arbitrary"),
    )(dproj, w_in, dz)
    return out, ex


def _mix_out_fwd(y, w_out, x, ln_g, ln_b, name):
    t, d = x.shape
    kk = y.shape[1]
    tm = _tile(t, 256, 128)

    def body(y_ref, w_ref, x_ref, g_ref, b_ref, z_ref, xn_ref, xnt_ref):
        z = ALPHA * x_ref[...] + _dot(y_ref[...], w_ref[...])
        z_ref[...] = z
        xn = _ln(z, g_ref[...], b_ref[...])
        xn_ref[...] = xn
        xnt_ref[...] = xn.T.astype(BF16)

    row = lambda i: (i, 0)
    fixed = lambda i: (0, 0)
    return pl.pallas_call(
        body, name=name, grid=(t // tm,),
        out_shape=[jax.ShapeDtypeStruct((t, d), F32), jax.ShapeDtypeStruct((t, d), F32),
                   jax.ShapeDtypeStruct((d, t), BF16)],
        in_specs=[pl.BlockSpec((tm, kk), row), pl.BlockSpec((kk, d), fixed), pl.BlockSpec((tm, d), row),
                  pl.BlockSpec((1, d), fixed), pl.BlockSpec((1, d), fixed)],
        out_specs=[pl.BlockSpec((tm, d), row), pl.BlockSpec((tm, d), row), pl.BlockSpec((d, tm), lambda i: (0, i))],
        compiler_params=_cparams("parallel"),
    )(y, w_out, x, ln_g, ln_b)


def _mix_out_bwd(dzb, w_out, name):
    t, d = dzb.shape
    kk = w_out.shape[0]
    tm = _tile(t, 256, 128)

    def body(dz_ref, w_ref, dy_ref):
        dy_ref[...] = _dot_nt(dz_ref[...], w_ref[...])

    return pl.pallas_call(
        body, name=name, grid=(t // tm,), out_shape=jax.ShapeDtypeStruct((t, kk), F32),
        in_specs=[pl.BlockSpec((tm, d), lambda i: (i, 0)), pl.BlockSpec((kk, d), lambda i: (0, 0))],
        out_specs=pl.BlockSpec((tm, kk), lambda i: (i, 0)),
        compiler_params=_cparams("parallel"),
    )(dzb, w_out)


def _loss_ln_bwd(z, target, ln_g, ln_b, bf16_scale, name):
    t, d = z.shape
    tm = _tile(t, 512, 8)

    def body(z_ref, t_ref, g_ref, b_ref, dz_ref, dzb_ref, dg_ref, db_ref, loss_ref):
        @pl.when(pl.program_id(0) == 0)
        def _():
            dg_ref[...] = jnp.zeros_like(dg_ref)
            db_ref[...] = jnp.zeros_like(db_ref)
            loss_ref[...] = jnp.zeros_like(loss_ref)

        xh, rstd = _ln_stats(z_ref[...])
        e = xh * g_ref[...] + b_ref[...] - t_ref[...]
        loss_ref[...] += 0.5 * jnp.sum(jnp.sum(e * e, axis=-1, keepdims=True) * (1.0 / d), axis=0, keepdims=True)
        dy = e * (1.0 / d)
        dz = _ln_bwd(dy * g_ref[...], xh, rstd)
        dz_ref[...] = dz
        dzb_ref[...] = (bf16_scale * dz).astype(BF16)
        dg_ref[...] += jnp.sum(dy * xh, axis=0, keepdims=True)
        db_ref[...] += jnp.sum(dy, axis=0, keepdims=True)

    row = lambda i: (i, 0)
    fixed = lambda i: (0, 0)
    return pl.pallas_call(
        body, name=name, grid=(t // tm,),
        out_shape=[jax.ShapeDtypeStruct((t, d), F32), jax.ShapeDtypeStruct((t, d), BF16),
                   jax.ShapeDtypeStruct((1, d), F32), jax.ShapeDtypeStruct((1, d), F32),
                   jax.ShapeDtypeStruct((8, 128), F32)],
        in_specs=[pl.BlockSpec((tm, d), row), pl.BlockSpec((tm, d), row), pl.BlockSpec((1, d), fixed),
                  pl.BlockSpec((1, d), fixed)],
        out_specs=[pl.BlockSpec((tm, d), row), pl.BlockSpec((tm, d), row), pl.BlockSpec((1, d), fixed),
                   pl.BlockSpec((1, d), fixed), pl.BlockSpec((8, 128), fixed)],
        compiler_params=_cparams("arbitrary"),
    )(z, target, ln_g, ln_b)


def _ln_bwd_call(z, dy, ln_g, bf16_scale, name, exch=()):
    t, d = z.shape
    tm = _tile(t, 512, 8)

    def body(z_ref, dy_ref, g_ref, dz_ref, dzb_ref, dg_ref, db_ref):
        @pl.when(pl.program_id(0) == 0)
        def _():
            dg_ref[...] = jnp.zeros_like(dg_ref)
            db_ref[...] = jnp.zeros_like(db_ref)

        xh, rstd = _ln_stats(z_ref[...])
        dy = dy_ref[...]
        dz = _ln_bwd(dy * g_ref[...], xh, rstd)
        dz_ref[...] = dz
        dzb_ref[...] = (bf16_scale * dz).astype(BF16)
        dg_ref[...] += jnp.sum(dy * xh, axis=0, keepdims=True)
        db_ref[...] += jnp.sum(dy, axis=0, keepdims=True)

    row = lambda i: (i, 0)
    fixed = lambda i: (0, 0)
    return _call(
        body, exch, name=name, grid=(t // tm,),
        out_shape=[jax.ShapeDtypeStruct((t, d), F32), jax.ShapeDtypeStruct((t, d), BF16),
                   jax.ShapeDtypeStruct((1, d), F32), jax.ShapeDtypeStruct((1, d), F32)],
        in_specs=[pl.BlockSpec((tm, d), row), pl.BlockSpec((tm, d), row), pl.BlockSpec((1, d), fixed)],
        out_specs=[pl.BlockSpec((tm, d), row), pl.BlockSpec((tm, d), row), pl.BlockSpec((1, d), fixed),
                   pl.BlockSpec((1, d), fixed)],
        semantics=("arbitrary",),
    )(z, dy, ln_g)


CONV_ROWS = 32


def _mixer_fwd(proj, conv_w, conv_b, cln_g, cln_b, sln_g, sln_b, sg_wm, sg_bb, name, exch=()):
    t = proj.shape[0]
    tm = _tile(t, 256, CHUNK)
    hb = tm // HALO
    nc = tm // CHUNK
    ch = CONV_CH

    def body(av_ref, ag_ref, bu_ref, bv_ref, hv_ref, hg_ref, cw_ref, cb_ref, lg_ref, lb_ref, sg_ref, sb_ref,
             w_ref, bb_ref, y_ref, yt_ref, c_ref, ext):
        i = pl.program_id(0)
        halo = hv_ref[...] * _sigmoid(hg_ref[...])
        ext[0:HALO, :] = jnp.where(i > 0, halo, 0.0)
        ext[HALO:HALO + tm, :] = av_ref[...] * _sigmoid(ag_ref[...])
        for r in range(0, tm, CONV_ROWS):
            acc = jnp.zeros((CONV_ROWS, ch), F32) + cb_ref[...]
            for k in range(CONV_TAPS):
                lo = r + k + HALO - (CONV_TAPS - 1)
                acc = acc + cw_ref[k:k + 1, :] * ext[lo:lo + CONV_ROWS, :]
            c_ref[r:r + CONV_ROWS, :] = acc
        a = _ln(c_ref[...], lg_ref[...], lb_ref[...])
        ya = a * _sigmoid(a)
        y_ref[:, 0:ch] = ya.astype(BF16)
        yt_ref[0:ch, :] = ya.T.astype(BF16)
        for h in range(HEADS):
            sl = slice(h * HEAD_DIM, (h + 1) * HEAD_DIM)
            u, _ = _gelu_and_grad(bu_ref[:, sl])
            v, _ = _gelu_and_grad(bv_ref[:, sl])
            vn = _ln(v, sg_ref[h:h + 1, :], sb_ref[h:h + 1, :])
            vn3 = vn.astype(BF16).reshape(nc, CHUNK, HEAD_DIM)
            wb = jnp.broadcast_to(w_ref[h][None], (nc, CHUNK, CHUNK))
            mixed = jnp.einsum("cts,csd->ctd", wb, vn3, preferred_element_type=F32) + bb_ref[h][None]
            yb = u * mixed.reshape(tm, HEAD_DIM)
            y_ref[:, ch + h * HEAD_DIM:ch + (h + 1) * HEAD_DIM] = yb.astype(BF16)
            yt_ref[ch + h * HEAD_DIM:ch + (h + 1) * HEAD_DIM, :] = yb.T.astype(BF16)

    col = lambda cidx: (lambda i: (i, cidx))
    prev = lambda cidx: (lambda i: (jnp.maximum(i * hb - 1, 0), cidx))
    fix2 = lambda i: (0, 0)
    fix3 = lambda i: (0, 0, 0)
    return _call(
        body, exch, name=name, grid=(t // tm,),
        out_shape=[jax.ShapeDtypeStruct((t, 2 * ch), BF16), jax.ShapeDtypeStruct((2 * ch, t), BF16),
                   jax.ShapeDtypeStruct((t, ch), F32)],
        in_specs=[pl.BlockSpec((tm, ch), col(0)), pl.BlockSpec((tm, ch), col(1)), pl.BlockSpec((tm, ch), col(2)),
                  pl.BlockSpec((tm, ch), col(3)), pl.BlockSpec((HALO, ch), prev(0)), pl.BlockSpec((HALO, ch), prev(1)),
                  pl.BlockSpec((CONV_TAPS, ch), fix2), pl.BlockSpec((1, ch), fix2), pl.BlockSpec((1, ch), fix2),
                  pl.BlockSpec((1, ch), fix2), pl.BlockSpec((HEADS, HEAD_DIM), fix2), pl.BlockSpec((HEADS, HEAD_DIM), fix2),
                  pl.BlockSpec((HEADS, CHUNK, CHUNK), fix3), pl.BlockSpec((HEADS, CHUNK, HEAD_DIM), fix3)],
        out_specs=[pl.BlockSpec((tm, 2 * ch), lambda i: (i, 0)), pl.BlockSpec((2 * ch, tm), lambda i: (0, i)),
                   pl.BlockSpec((tm, ch), lambda i: (i, 0))],
        scratch_shapes=[pltpu.VMEM((HALO + tm, ch), F32)],
        semantics=("parallel",),
    )(proj, proj, proj, proj, proj, proj, conv_w, conv_b, cln_g, cln_b, sln_g, sln_b, sg_wm, sg_bb)


def _mixer_bwd(proj, conv_c, dy, conv_w, cln_g, cln_b, sln_g, sln_b, sg_wm, sg_wmt, sg_bb, name, exch=()):
    t = proj.shape[0]
    tm = _tile(t, 256, CHUNK)
    hb = tm // HALO
    nc = tm // CHUNK
    nt = t // tm
    ch = CONV_CH
    last_halo = t // HALO - 1

    def body(av_ref, ag_ref, bu_ref, bv_ref, hv_ref, hg_ref, c_ref, cn_ref, dya_ref, dyan_ref, dyb_ref,
             cw_ref, lg_ref, lb_ref, sg_ref, sb_ref, w_ref, wt_ref, bb_ref,
             dp_ref, dcw_ref, dcb_ref, dlg_ref, dlb_ref, dsg_ref, dsb_ref, dw_ref, dbs_ref,
             ext_h, ext_dc, acc_cw):
        i = pl.program_id(0)

        @pl.when(i == 0)
        def _():
            acc_cw[...] = jnp.zeros_like(acc_cw)
            for ref in (dcb_ref, dlg_ref, dlb_ref, dsg_ref, dsb_ref, dw_ref, dbs_ref):
                ref[...] = jnp.zeros_like(ref)

        lg = lg_ref[...]
        lb = lb_ref[...]

        def conv_ln_bwd(c, dya):
            xh, rstd = _ln_stats(c)
            a = xh * lg + lb
            da = dya * _silu_grad(a)
            return _ln_bwd(da * lg, xh, rstd), da, xh

        dc, da, xh = conv_ln_bwd(c_ref[...], dya_ref[...])
        dlg_ref[...] += jnp.sum(da * xh, axis=0, keepdims=True)
        dlb_ref[...] += jnp.sum(da, axis=0, keepdims=True)
        dcb_ref[...] += jnp.sum(dc, axis=0, keepdims=True)
        dcn, _, _ = conv_ln_bwd(cn_ref[...], dyan_ref[...])
        ext_dc[0:tm, :] = dc
        ext_dc[tm:tm + HALO, :] = jnp.where(i < nt - 1, dcn, 0.0)
        sig_g = _sigmoid(ag_ref[...])
        halo = hv_ref[...] * _sigmoid(hg_ref[...])
        ext_h[0:HALO, :] = jnp.where(i > 0, halo, 0.0)
        ext_h[HALO:HALO + tm, :] = av_ref[...] * sig_g
        for r in range(0, tm, CONV_ROWS):
            dcr = ext_dc[r:r + CONV_ROWS, :]
            acc = jnp.zeros((CONV_ROWS, ch), F32)
            for k in range(CONV_TAPS):
                lo = r + k + HALO - (CONV_TAPS - 1)
                prod = dcr * ext_h[lo:lo + CONV_ROWS, :]
                acc_cw[k] += jnp.sum(prod.reshape(CONV_ROWS // 8, 8, ch), axis=0)
                hi = r + (CONV_TAPS - 1) - k
                acc = acc + cw_ref[k:k + 1, :] * ext_dc[hi:hi + CONV_ROWS, :]
            sg_r = sig_g[r:r + CONV_ROWS, :]
            av_r = av_ref[r:r + CONV_ROWS, :]
            dp_ref[r:r + CONV_ROWS, 0:ch] = (acc * sg_r).astype(BF16)
            dp_ref[r:r + CONV_ROWS, ch:2 * ch] = (acc * av_r * sg_r * (1.0 - sg_r)).astype(BF16)

        @pl.when(i == nt - 1)
        def _():
            dcw_ref[...] = jnp.sum(acc_cw[...], axis=1)

        tril = (lax.broadcasted_iota(jnp.int32, (CHUNK, CHUNK), 0)
                >= lax.broadcasted_iota(jnp.int32, (CHUNK, CHUNK), 1)).astype(F32)
        for h in range(HEADS):
            sl = slice(h * HEAD_DIM, (h + 1) * HEAD_DIM)
            u, du_dx = _gelu_and_grad(bu_ref[:, sl])
            v, dv_dx = _gelu_and_grad(bv_ref[:, sl])
            xhv, rstdv = _ln_stats(v)
            gh = sg_ref[h:h + 1, :]
            vn3 = (xhv * gh + sb_ref[h:h + 1, :]).astype(BF16).reshape(nc, CHUNK, HEAD_DIM)
            wb = jnp.broadcast_to(w_ref[h][None], (nc, CHUNK, CHUNK))
            mixed = jnp.einsum("cts,csd->ctd", wb, vn3, preferred_element_type=F32) + bb_ref[h][None]
            dyb = dyb_ref[:, sl]
            d_u = dyb * mixed.reshape(tm, HEAD_DIM)
            dm = dyb * u
            dm3 = dm.reshape(nc, CHUNK, HEAD_DIM)
            dbs_ref[h:h + 1, :] += jnp.sum(jnp.sum(dm3, axis=0).T, axis=0, keepdims=True)
            dm3b = dm3.astype(BF16)
            dw_h = jnp.sum(jnp.einsum("ctd,csd->cts", dm3b, vn3, preferred_element_type=F32), axis=0)
            dw_ref[h] += dw_h * tril
            wtb = jnp.broadcast_to(wt_ref[h][None], (nc, CHUNK, CHUNK))
            d_vn = jnp.einsum("cst,ctd->csd", wtb, dm3b, preferred_element_type=F32).reshape(tm, HEAD_DIM)
            dsg_ref[h:h + 1, :] += jnp.sum(d_vn * xhv, axis=0, keepdims=True)
            dsb_ref[h:h + 1, :] += jnp.sum(d_vn, axis=0, keepdims=True)
            dv = _ln_bwd(d_vn * gh, xhv, rstdv)
            dp_ref[:, 2 * ch + h * HEAD_DIM:2 * ch + (h + 1) * HEAD_DIM] = (d_u * du_dx).astype(BF16)
            dp_ref[:, 3 * ch + h * HEAD_DIM:3 * ch + (h + 1) * HEAD_DIM] = (dv * dv_dx).astype(BF16)

    col = lambda cidx: (lambda i: (i, cidx))
    prev = lambda cidx: (lambda i: (jnp.maximum(i * hb - 1, 0), cidx))
    nxt = lambda i: (jnp.minimum((i + 1) * hb, last_halo), 0)
    fix2 = lambda i: (0, 0)
    fix3 = lambda i: (0, 0, 0)
    out_shape = [jax.ShapeDtypeStruct((t, 4 * ch), BF16), jax.ShapeDtypeStruct((CONV_TAPS, ch), F32),
                 jax.ShapeDtypeStruct((1, ch), F32), jax.ShapeDtypeStruct((1, ch), F32), jax.ShapeDtypeStruct((1, ch), F32),
                 jax.ShapeDtypeStruct((HEADS, HEAD_DIM), F32), jax.ShapeDtypeStruct((HEADS, HEAD_DIM), F32),
                 jax.ShapeDtypeStruct((HEADS, CHUNK, CHUNK), F32), jax.ShapeDtypeStruct((HEADS, CHUNK), F32)]
    out_specs = [pl.BlockSpec((tm, 4 * ch), lambda i: (i, 0)), pl.BlockSpec((CONV_TAPS, ch), fix2),
                 pl.BlockSpec((1, ch), fix2), pl.BlockSpec((1, ch), fix2), pl.BlockSpec((1, ch), fix2),
                 pl.BlockSpec((HEADS, HEAD_DIM), fix2), pl.BlockSpec((HEADS, HEAD_DIM), fix2),
                 pl.BlockSpec((HEADS, CHUNK, CHUNK), fix3), pl.BlockSpec((HEADS, CHUNK), fix2)]
    in_specs = [pl.BlockSpec((tm, ch), col(0)), pl.BlockSpec((tm, ch), col(1)), pl.BlockSpec((tm, ch), col(2)),
                pl.BlockSpec((tm, ch), col(3)), pl.BlockSpec((HALO, ch), prev(0)), pl.BlockSpec((HALO, ch), prev(1)),
                pl.BlockSpec((tm, ch), col(0)), pl.BlockSpec((HALO, ch), nxt),
                pl.BlockSpec((tm, ch), col(0)), pl.BlockSpec((HALO, ch), nxt), pl.BlockSpec((tm, ch), col(1)),
                pl.BlockSpec((CONV_TAPS, ch), fix2), pl.BlockSpec((1, ch), fix2), pl.BlockSpec((1, ch), fix2),
                pl.BlockSpec((HEADS, HEAD_DIM), fix2), pl.BlockSpec((HEADS, HEAD_DIM), fix2),
                pl.BlockSpec((HEADS, CHUNK, CHUNK), fix3), pl.BlockSpec((HEADS, CHUNK, CHUNK), fix3),
                pl.BlockSpec((HEADS, CHUNK, HEAD_DIM), fix3)]
    return _call(
        body, exch, name=name, grid=(nt,), out_shape=out_shape, in_specs=in_specs, out_specs=out_specs,
        scratch_shapes=[pltpu.VMEM((HALO + tm, ch), F32), pltpu.VMEM((tm + HALO, ch), F32),
                        pltpu.VMEM((CONV_TAPS, 8, ch), F32)],
        semantics=("arbitrary",),
    )(proj, proj, proj, proj, proj, proj, conv_c, conv_c, dy, dy, dy,
      conv_w, cln_g, cln_b, sln_g, sln_b, sg_wm, sg_wmt, sg_bb)


def _pair_sum(parts, from_sibling, c, name):
    _, r, cc = parts.shape
    tr = _tile(r, max(16, (1 << 20) // (2 * cc)), 16)

    def body(c_ref, p_ref, s_ref, o_ref):
        o_ref[...] = (p_ref[...].astype(F32) + s_ref[...].astype(F32)).astype(BF16)

    grid_spec = pltpu.PrefetchScalarGridSpec(
        num_scalar_prefetch=1, grid=(4, r // tr),
        in_specs=[pl.BlockSpec((1, tr, cc), lambda j, i, c_ref: (2 * j + c_ref[0], i, 0)),
                  pl.BlockSpec((1, tr, cc), lambda j, i, c_ref: (j, i, 0))],
        out_specs=pl.BlockSpec((1, tr, cc), lambda j, i, c_ref: (j, i, 0)))
    return pl.pallas_call(
        body, name=name, grid_spec=grid_spec, out_shape=jax.ShapeDtypeStruct((4, r, cc), BF16),
        compiler_params=_cparams("parallel", "parallel"),
    )(c, parts, from_sibling)


def _adamw_math(w, g, m, v):
    m = ADAM_B1 * m + (1.0 - ADAM_B1) * g
    v = ADAM_B2 * v + (1.0 - ADAM_B2) * (g * g)
    m_hat = m / (1.0 - ADAM_B1 ** ADAM_STEP)
    v_hat = v / (1.0 - ADAM_B2 ** ADAM_STEP)
    delta = -ADAM_LR * (m_hat / (jnp.sqrt(v_hat) + ADAM_EPS) + ADAM_WD * w)
    return delta, m, v


def _adamw_sharded(w, m, v, chip_parts, from_chips, chip, name):
    r, cc = w.shape
    tr = _tile(r, max(16, (1 << 19) // (4 * cc) * 2), 16)

    def body(j_ref, w_ref, m_ref, v_ref, q_ref, o_ref, g_out, d_out, m_out, v_out):
        g = q_ref[0].astype(F32)
        for k in range(3):
            g = g + o_ref[k].astype(F32)
        d, mm, vv = _adamw_math(w_ref[...], g, m_ref[...], v_ref[...])
        g_out[...] = g
        d_out[...] = d
        m_out[...] = mm
        v_out[...] = vv

    row = lambda i, j_ref: (i, 0)
    grid_spec = pltpu.PrefetchScalarGridSpec(
        num_scalar_prefetch=1, grid=(r // tr,),
        in_specs=[pl.BlockSpec((tr, cc), row), pl.BlockSpec((tr, cc), row), pl.BlockSpec((tr, cc), row),
                  pl.BlockSpec((1, tr, cc), lambda i, j_ref: (j_ref[0], i, 0)),
                  pl.BlockSpec((3, tr, cc), lambda i, j_ref: (0, i, 0))],
        out_specs=[pl.BlockSpec((tr, cc), row)] * 4)
    return pl.pallas_call(
        body, name=name, grid_spec=grid_spec, out_shape=[jax.ShapeDtypeStruct((r, cc), F32)] * 4,
        compiler_params=_cparams("parallel"),
    )(chip, w, m, v, chip_parts, from_chips)


def _adamw_small(w, g, m, v, name):
    r, cc = w.shape

    def body(w_ref, g_ref, m_ref, v_ref, d_out, m_out, v_out):
        d, mm, vv = _adamw_math(w_ref[...], g_ref[...], m_ref[...], v_ref[...])
        d_out[...] = d
        m_out[...] = mm
        v_out[...] = vv

    full = pl.BlockSpec((r, cc), lambda i: (0, 0))
    return pl.pallas_call(
        body, name=name, grid=(1,), out_shape=[jax.ShapeDtypeStruct((r, cc), F32)] * 3,
        in_specs=[full] * 4, out_specs=[full] * 3, compiler_params=_cparams("arbitrary"),
    )(w, g, m, v)


SMALL = ("ln1_g", "ln1_b", "conv_b", "conv_ln_g", "conv_ln_b", "sg_ln_g", "sg_ln_b", "sg_w", "sg_b",
         "ln2_g", "ln2_b", "ln3_g", "ln3_b")
ORDER = ("ffn1_w_gate_up", "ffn1_w_down", "ln1_g", "ln1_b", "mix_w_in", "conv_w", "conv_b", "conv_ln_g", "conv_ln_b",
         "sg_ln_g", "sg_ln_b", "sg_w", "sg_b", "mix_w_out", "ln2_g", "ln2_b", "ffn2_w_gate_up", "ffn2_w_down",
         "ln3_g", "ln3_b")


def _rows128(a):
    return a.reshape(-1, 128)


def kernel(x, ffn1_w_gate_up, ffn1_w_down, ln1_g, ln1_b, mix_w_in, conv_w, conv_b, conv_ln_g, conv_ln_b, sg_ln_g, sg_ln_b, sg_w, sg_b, mix_w_out, ln2_g, ln2_b, ffn2_w_gate_up, ffn2_w_down, ln3_g, ln3_b, loss_target, m_ffn1_w_gate_up, m_ffn1_w_down, m_ln1_g, m_ln1_b, m_mix_w_in, m_conv_w, m_conv_b, m_conv_ln_g, m_conv_ln_b, m_sg_ln_g, m_sg_ln_b, m_sg_w, m_sg_b, m_mix_w_out, m_ln2_g, m_ln2_b, m_ffn2_w_gate_up, m_ffn2_w_down, m_ln3_g, m_ln3_b, v_ffn1_w_gate_up, v_ffn1_w_down, v_ln1_g, v_ln1_b, v_mix_w_in, v_conv_w, v_conv_b, v_conv_ln_g, v_conv_ln_b, v_sg_ln_g, v_sg_ln_b, v_sg_w, v_sg_b, v_mix_w_out, v_ln2_g, v_ln2_b, v_ffn2_w_gate_up, v_ffn2_w_down, v_ln3_g, v_ln3_b):
    args = dict(locals())
    w = {n: args[n][0] for n in ORDER}
    mom = {n: args["m_" + n][0] for n in ORDER}
    var = {n: args["v_" + n][0] for n in ORDER}
    x0 = x[0]
    target = loss_target[0]
    t, d = x0.shape
    my_x, my_y, my_c = lax.axis_index("x"), lax.axis_index("y"), lax.axis_index("c")
    my_chip = (2 * my_x + my_y).astype(jnp.int32).reshape(1)
    my_core = my_c.astype(jnp.int32).reshape(1)
    me = 4 * my_x + 2 * my_y + my_c

    big = ("ffn1_w_gate_up", "ffn1_w_down", "mix_w_in", "mix_w_out", "ffn2_w_gate_up", "ffn2_w_down")
    sh = {n: w[n].astype(BF16) for n in big}
    f2s = sh["ffn2_w_gate_up"].shape[1]
    wgu1, wd1, conv_w_all = _allgather([sh["ffn1_w_gate_up"], sh["ffn1_w_down"], w["conv_w"]], [True, False, False])
    wd1 = wd1.reshape(-1, d)
    conv_w_full = jnp.transpose(conv_w_all, (1, 0, 2)).reshape(CONV_TAPS, CONV_CH)
    tril = jnp.tril(jnp.ones((CHUNK, CHUNK), F32))
    sg_wm = w["sg_w"] * tril
    sg_wm_b = sg_wm.astype(BF16)
    sg_wmt_b = jnp.swapaxes(sg_wm, 1, 2).astype(BF16)
    sg_bb = jnp.broadcast_to(w["sg_b"][:, :, None], (HEADS, CHUNK, HEAD_DIM))
    row = lambda a: a.reshape(1, -1)

    x0t, _ = _transpose_bf16(x0, "x0_transpose")
    (g1, u1, h1t, z1, x1), ((g_in, g_out, g_gu2),) = _ffn_fwd(
        x0, wgu1, wd1, row(w["ln1_g"]), row(w["ln1_b"]), "ffn1_fwd",
        exch=[_gather_first([sh["mix_w_in"], sh["mix_w_out"], sh["ffn2_w_gate_up"]], [False, False, True])])
    x1t, ((w_in, w_out),) = _transpose_bf16(
        x1, "x1_transpose", exch=[_gather_forward([g_in, g_out], [False, False], [None, None])])
    w_out = w_out.reshape(-1, d)
    proj, ((g_d2,), (wgu2,)) = _mix_in_proj(
        x1, w_in, "mix_in_fwd",
        exch=[_gather_first([sh["ffn2_w_down"]], [False]), _gather_forward([g_gu2], [True], [f2s])])
    (y, yt, conv_c), ((wd2,),) = _mixer_fwd(
        proj, conv_w_full, row(w["conv_b"]), row(w["conv_ln_g"]), row(w["conv_ln_b"]),
        w["sg_ln_g"], w["sg_ln_b"], sg_wm_b, sg_bb, "mixer_fwd", exch=[_gather_forward([g_d2], [False], [None])])
    wd2 = wd2.reshape(-1, d)
    z2, x2, x2t = _mix_out_fwd(y, w_out, x1, row(w["ln2_g"]), row(w["ln2_b"]), "mix_out_fwd")
    (g2, u2, h2t, z3), _ = _ffn_fwd(x2, wgu2, wd2, row(w["ln3_g"]), row(w["ln3_b"]), "ffn2_fwd", with_ln=False)

    f = wd1.shape[0]
    dn = _tile(d, 1024, 128)
    grads = {}
    pair = lambda p, s, label: _pair_sum(p, s, my_core, "pair_sum_" + label)
    dz3, do2, grads["ln3_g"], grads["ln3_b"], loss_tile = _loss_ln_bwd(
        z3, target, row(w["ln3_g"]), row(w["ln3_b"]), 0.5, "loss_ln3_bwd")
    p_d2, _ = _weight_grad(h2t, do2, dn, 512, "ffn2_dw_down")
    p_d2 = p_d2.reshape(N_DEV, f // N_DEV, d)
    (dg2, du2, dx2), ((s_d2,),) = _ffn_bwd(dz3, do2, g2, u2, wgu2, wd2, "ffn2_bwd", exch=[_rs_sibling([p_d2])])
    q_d2 = pair(p_d2, s_d2, "ffn2_down")
    p_gu2, ((r_d2,),) = _weight_grad(x2t, dg2, f2s, 512, "ffn2_dw_gate", blocks=N_DEV, exch=[_rs_chips([q_d2])])
    p_gu2, _ = _weight_grad(x2t, du2, f2s, 512, "ffn2_dw_up", blocks=N_DEV, block_offset=4, into=p_gu2)
    (dz2, dz2b, grads["ln2_g"], grads["ln2_b"]), ((s_gu2,),) = _ln_bwd_call(
        z2, dx2, row(w["ln2_g"]), 1.0, "ln2_bwd", exch=[_rs_sibling([p_gu2])])
    q_gu2 = pair(p_gu2, s_gu2, "ffn2_gate_up")
    dy = _mix_out_bwd(dz2b, w_out, "mix_out_bwd")
    p_out, _ = _weight_grad(yt, dz2b, dn, 512, "mix_out_dw")
    p_out = p_out.reshape(N_DEV, -1, d)
    (dproj, grads["conv_w"], grads["conv_b"], grads["conv_ln_g"], grads["conv_ln_b"], grads["sg_ln_g"],
     grads["sg_ln_b"], grads["sg_w"], grads["sg_b"]), ((r_gu2,),) = _mixer_bwd(
        proj, conv_c, dy, conv_w_full, row(w["conv_ln_g"]), row(w["conv_ln_b"]), w["sg_ln_g"], w["sg_ln_b"],
        sg_wm_b, sg_wmt_b, sg_bb, "mixer_bwd", exch=[_rs_chips([q_gu2])])
    dx1, ((s_out,),) = _mix_in_bwd(dproj, w_in, dz2, "mix_in_bwd", exch=[_rs_sibling([p_out])])
    p_in, _ = _weight_grad(x1t, dproj, w_in.shape[2], 512, "mix_in_dw", blocks=N_DEV)
    (dz1, do1, grads["ln1_g"], grads["ln1_b"]), ((s_in,),) = _ln_bwd_call(
        z1, dx1, row(w["ln1_g"]), 0.5, "ln1_bwd", exch=[_rs_sibling([p_in])])
    q_out = pair(p_out, s_out, "mix_out")
    q_in = pair(p_in, s_in, "mix_in")
    small_parts = [_rows128(grads[n]) for n in SMALL]
    packed = jnp.concatenate(small_parts + [_rows128(grads["conv_w"]), loss_tile], axis=0)
    p_d1, ((r_in,),) = _weight_grad(h1t, do1, dn, 512, "ffn1_dw_down", exch=[_rs_chips([q_in])])
    p_d1 = p_d1.reshape(N_DEV, f // N_DEV, d)
    (dg1, du1), ((s_d1,), (r_out,), (small_all,)) = _ffn_bwd_act(
        do1, g1, u1, wd1, "ffn1_bwd_act",
        exch=[_rs_sibling([p_d1]), _rs_chips([q_out]), _small_gather(packed)])
    q_d1 = pair(p_d1, s_d1, "ffn1_down")
    p_gu1, ((r_d1,),) = _weight_grad(x0t, dg1, f2s, 512, "ffn1_dw_gate", blocks=N_DEV, exch=[_rs_chips([q_d1])])
    p_gu1, _ = _weight_grad(x0t, du1, f2s, 512, "ffn1_dw_up", blocks=N_DEV, block_offset=4, into=p_gu1)
    (s_gu1,) = _exchange_alone(_rs_sibling([p_gu1]), "ffn1_gate_up_sibling_exchange")
    q_gu1 = pair(p_gu1, s_gu1, "ffn1_gate_up")
    (grad_x,), ((r_gu1,),) = _ffn_bwd_dx(dz1, dg1, du1, wgu1, "ffn1_bwd_dx", exch=[_rs_chips([q_gu1])])

    chip_parts = [q_gu1, q_d1, q_in, q_out, q_gu2, q_d2]
    from_chips = [r_gu1, r_d1, r_in, r_out, r_gu2, r_d2]
    out = {}
    for k, n in enumerate(big):
        out[n] = _adamw_sharded(w[n], mom[n], var[n], chip_parts[k], from_chips[k], my_chip, "adamw_" + n)

    cw_rows = CONV_TAPS * CONV_CH // 128
    total = _sum_over_devices(small_all)
    offs = [0]
    for p in small_parts:
        offs.append(offs[-1] + p.shape[0])
    n_small = offs[-1]
    loss = total[n_small + cw_rows, 0]
    g_conv_w = lax.dynamic_slice_in_dim(total[n_small:n_small + cw_rows].reshape(CONV_TAPS, CONV_CH),
                                        me * (CONV_CH // N_DEV), CONV_CH // N_DEV, axis=1)
    pad8 = lambda a: jnp.pad(a, ((0, -a.shape[0] % 8), (0, 0)))
    pack = lambda tree, cw: jnp.concatenate([_rows128(tree[n]) for n in SMALL] + [pad8(cw)], axis=0)
    g_pack = jnp.concatenate([total[:n_small], pad8(g_conv_w)], axis=0)
    d_pack, m_pack, v_pack = _adamw_small(pack(w, w["conv_w"]), g_pack, pack(mom, mom["conv_w"]),
                                          pack(var, var["conv_w"]), "adamw_small")
    for k, n in enumerate(SMALL):
        sl = slice(offs[k], offs[k + 1])
        shp = w[n].shape
        out[n] = (total[sl].reshape(shp), d_pack[sl].reshape(shp), m_pack[sl].reshape(shp), v_pack[sl].reshape(shp))
    sl = slice(n_small, n_small + CONV_TAPS)
    out["conv_w"] = (g_conv_w, d_pack[sl], m_pack[sl], v_pack[sl])

    lead = lambda a: a[None]
    res = [loss, grad_x[None]]
    for kind in range(4):
        res += [lead(out[n][kind]) for n in ORDER]
    return tuple(res)
```

```python
import functools
import math

import jax
import jax.numpy as jnp
from jax import lax
from jax.experimental import pallas as pl
from jax.experimental.pallas import tpu as pltpu

F32, BF16 = jnp.float32, jnp.bfloat16
MESH = pl.DeviceIdType.MESH
ANY = pl.BlockSpec(memory_space=pl.ANY)

N_DEV = 8
LN_EPS = 1e-5
ALPHA = 2.0 ** 0.25
CONV_CH = 1024
CONV_TAPS = 31
HALO = 32
HEADS = 8
HEAD_DIM = 128
CHUNK = 128
ADAM_LR, ADAM_B1, ADAM_B2, ADAM_EPS, ADAM_WD, ADAM_STEP = 0.001, 0.9, 0.999, 1e-08, 0.01, 10
V7X_VMEM_LIMIT = 56 * 2 ** 20


def _cparams(*sem):
    return pltpu.CompilerParams(dimension_semantics=sem, vmem_limit_bytes=V7X_VMEM_LIMIT)


def _tile(n, pref, mult):
    best = None
    for t in range(mult, min(n, pref) + 1, mult):
        if n % t == 0:
            best = t
    return best if best is not None else n


def _dot(a, b):
    return jnp.dot(a, b, preferred_element_type=F32)


def _dot_nt(a, b):
    return lax.dot_general(a, b, (((1,), (1,)), ((), ())), preferred_element_type=F32)


def _sigmoid(x):
    return 1.0 / (1.0 + jnp.exp(-x))


def _ln_stats(z):
    mu = jnp.mean(z, axis=-1, keepdims=True)
    zc = z - mu
    var = jnp.mean(zc * zc, axis=-1, keepdims=True)
    rstd = lax.rsqrt(var + LN_EPS)
    return zc * rstd, rstd


def _ln(z, g, b):
    xh, _ = _ln_stats(z)
    return xh * g + b


def _ln_bwd(dxh, xh, rstd):
    m1 = jnp.mean(dxh, axis=-1, keepdims=True)
    m2 = jnp.mean(dxh * xh, axis=-1, keepdims=True)
    return rstd * (dxh - m1 - xh * m2)


_GK = math.sqrt(2.0 / math.pi)
_GA = 0.044715


def _gelu_and_grad(x):
    x2 = x * x
    t = jnp.tanh(_GK * (x + _GA * x * x2))
    y = 0.5 * x * (1.0 + t)
    dy = 0.5 * (1.0 + t) + 0.5 * x * (1.0 - t * t) * (_GK * (1.0 + 3.0 * _GA * x2))
    return y, dy


def _silu_grad(a):
    s = _sigmoid(a)
    return s * (1.0 + a * (1.0 - s))


def _place():
    return lax.axis_index("x"), lax.axis_index("y"), lax.axis_index("c")


def _other_chips(x, y):
    return [(1 - x, y), (x, 1 - y), (1 - x, 1 - y)]


def _visit_order(x, y, c):
    chips = _other_chips(x, y)
    return [(x, y, c), (x, y, 1 - c), (*chips[0], c), (*chips[1], c), (*chips[0], 1 - c), (*chips[1], 1 - c),
            (*chips[2], c), (*chips[2], 1 - c)]


def _gather_and_gate_up(xb, shards, order, name):
    n = len(shards)
    t, d = xb.shape
    cols = shards[0].shape[1]
    tm = _tile(t, 512, 128)
    ni = t // tm
    col_major = [True] + [False] * (n - 1)

    def body(order_ref, x_ref, *refs):
        srcs, gu_ref, dsts = refs[:n], refs[n], refs[n + 1:2 * n + 1]
        wbuf, send_sems, recv_sems, local_sems, load_sem = refs[2 * n + 1:]
        b, i = pl.program_id(0), pl.program_id(1)
        x, y, c = _place()
        me, sib = (x, y, c), (x, y, 1 - c)
        chips = _other_chips(x, y)

        def slot(w, p):
            return _block_slot(dsts[w], col_major[w], shards[w].shape[1], p)

        def copy(w, s, block, to, from_src=False):
            return pltpu.make_async_remote_copy(
                src_ref=srcs[w] if from_src else slot(w, block), dst_ref=slot(w, block),
                send_sem=send_sems.at[7 * w + s], recv_sem=recv_sems.at[7 * w + s],
                device_id=to, device_id_type=MESH)

        def own(w):
            return pltpu.make_async_copy(srcs[w], slot(w, me), local_sems.at[w])

        def first(w):
            return [copy(w, 0, me, sib, True)] + [copy(w, 1 + j, me, (*chip, c), True) for j, chip in enumerate(chips)]

        def forward(w, j):
            return copy(w, 4 + j, (*chips[j], c), sib)

        @pl.when((b == 0) & (i == 0))
        def _():
            for w in range(n):
                own(w).start()
            for w in range(n):
                for cp in first(w):
                    cp.start()

        def arrive(k):
            if k == 0:
                own(0).wait()
            elif k == 1:
                copy(0, 0, sib, me).wait_recv()
            elif k in (2, 3, 6):
                j = {2: 0, 3: 1, 6: 2}[k]
                copy(0, 1 + j, (*chips[j], c), me).wait_recv()
                forward(0, j).start()
            else:
                j = {4: 0, 5: 1, 7: 2}[k]
                copy(0, 4 + j, (*chips[j], 1 - c), me).wait_recv()

        for k in range(N_DEV):
            @pl.when((b == k) & (i == 0))
            def _(k=k):
                arrive(k)
                at = pl.multiple_of(order_ref[k] * cols, 128)
                load = pltpu.make_async_copy(dsts[0].at[:, pl.ds(at, cols)], wbuf, load_sem.at[0])
                load.start()
                load.wait()

        gu_ref[...] = _dot(x_ref[...], wbuf[...]).astype(BF16)

        @pl.when((b == N_DEV - 1) & (i == ni - 1))
        def _():
            for w in range(1, n):
                for j in range(3):
                    copy(w, 1 + j, (*chips[j], c), me).wait_recv()
                    forward(w, j).start()
            for w in range(1, n):
                copy(w, 0, sib, me).wait_recv()
                for j in range(3):
                    copy(w, 4 + j, (*chips[j], 1 - c), me).wait_recv()
                own(w).wait()
            for w in range(n):
                for cp in first(w):
                    cp.wait_send()
                for j in range(3):
                    forward(w, j).wait_send()

    grid_spec = pltpu.PrefetchScalarGridSpec(
        num_scalar_prefetch=1, grid=(N_DEV, ni),
        in_specs=[pl.BlockSpec((tm, d), lambda b, i, o: (i, 0))] + [ANY] * n,
        out_specs=[pl.BlockSpec((tm, cols), lambda b, i, o: (i, o[b]))] + [ANY] * n,
        scratch_shapes=[pltpu.VMEM((d, cols), BF16), pltpu.SemaphoreType.DMA((7 * n,)),
                        pltpu.SemaphoreType.DMA((7 * n,)), pltpu.SemaphoreType.DMA((n,)),
                        pltpu.SemaphoreType.DMA((1,))])
    res = pl.pallas_call(
        body, name=name, grid_spec=grid_spec,
        out_shape=[jax.ShapeDtypeStruct((t, N_DEV * cols), BF16)]
        + [_gathered_shape(s, cm) for s, cm in zip(shards, col_major)],
        compiler_params=_cparams("arbitrary", "arbitrary"),
    )(order, xb, *shards)
    return res[0], res[1:]


class _Exchange:
    def __init__(self, ins, io, new, n_sems, n_local, make):
        self.ins, self.io, self.new = list(ins), list(io), list(new)
        self.n_sems, self.n_local, self.make = n_sems, n_local, make


def _block_slot(ref, col_major, cols, place, rows=None):
    k = 4 * place[0] + 2 * place[1] + place[2]
    band = slice(None) if rows is None else pl.ds(rows[0], rows[1])
    if col_major:
        return ref.at[band, pl.ds(pl.multiple_of(k * cols, 128), cols)]
    return ref.at[k] if rows is None else ref.at[k, band]


def _gathered_shape(s, col_major):
    return jax.ShapeDtypeStruct((s.shape[0], N_DEV * s.shape[1]) if col_major else (N_DEV,) + s.shape, s.dtype)


def _gather_first(shards, col_major, rows=None, into=None):
    n = len(shards)
    new = [] if into is not None else [_gathered_shape(s, cm) for s, cm in zip(shards, col_major)]

    def make(in_refs, io_refs, new_refs, send_sems, recv_sems, local_sems, base=0, local_base=0):
        x, y, c = _place()
        targets = [(x, y, 1 - c)] + [(*chip, c) for chip in _other_chips(x, y)]
        gathered = io_refs if into is not None else new_refs
        copies = []
        for w in range(n):
            src = in_refs[w] if rows is None else in_refs[w].at[pl.ds(rows[0], rows[1])]
            slot = _block_slot(gathered[w], col_major[w], shards[w].shape[1], (x, y, c), rows)
            copies.append(pltpu.make_async_copy(src, slot, local_sems.at[local_base + w]))
            for s, to in enumerate(targets):
                copies.append(pltpu.make_async_remote_copy(
                    src_ref=src, dst_ref=slot, send_sem=send_sems.at[base + 4 * w + s],
                    recv_sem=recv_sems.at[base + 4 * w + s], device_id=to, device_id_type=MESH))
        return copies

    return _Exchange(shards, into or [], new, 4 * n, n, make)


def _gather_forward(gathered, col_major, cols, rows=None):
    n = len(gathered)

    def make(in_refs, io_refs, new_refs, send_sems, recv_sems, local_sems, base=0, local_base=0):
        x, y, c = _place()
        copies = []
        for w in range(n):
            for j, chip in enumerate(_other_chips(x, y)):
                slot = _block_slot(io_refs[w], col_major[w], cols[w], (*chip, c), rows)
                copies.append(pltpu.make_async_remote_copy(
                    src_ref=slot, dst_ref=slot, send_sem=send_sems.at[base + 3 * w + j],
                    recv_sem=recv_sems.at[base + 3 * w + j], device_id=(x, y, 1 - c), device_id_type=MESH))
        return copies

    return _Exchange([], gathered, [], 3 * n, 0, make)


def _both(a, b):
    def make(in_refs, io_refs, new_refs, send_sems, recv_sems, local_sems):
        na = len(a.ins)
        return (a.make(in_refs[:na], io_refs, [], send_sems, recv_sems, local_sems, 0, 0)
                + b.make(in_refs[na:], io_refs, [], send_sems, recv_sems, local_sems, a.n_sems, a.n_local))

    return _Exchange(a.ins + b.ins, a.io, [], a.n_sems + b.n_sems, a.n_local + b.n_local, make)


def _rs_sibling(parts):
    n = len(parts)

    def make(in_refs, io_refs, new_refs, send_sems, recv_sems, local_sems):
        x, y, c = _place()
        copies = []
        for w in range(n):
            for j in range(4):
                copies.append(pltpu.make_async_remote_copy(
                    src_ref=in_refs[w].at[2 * j + (1 - c)], dst_ref=new_refs[w].at[j],
                    send_sem=send_sems.at[4 * w + j], recv_sem=recv_sems.at[4 * w + j],
                    device_id=(x, y, 1 - c), device_id_type=MESH))
        return copies

    return _Exchange(parts, [], [jax.ShapeDtypeStruct((4,) + p.shape[1:], p.dtype) for p in parts], 4 * n, 0, make)


def _rs_chips(chip_parts):
    n = len(chip_parts)

    def make(in_refs, io_refs, new_refs, send_sems, recv_sems, local_sems):
        x, y, c = _place()
        copies = []
        for w in range(n):
            for rel, (px, py) in enumerate(_other_chips(x, y)):
                copies.append(pltpu.make_async_remote_copy(
                    src_ref=in_refs[w].at[2 * px + py], dst_ref=new_refs[w].at[rel],
                    send_sem=send_sems.at[3 * w + rel], recv_sem=recv_sems.at[3 * w + rel],
                    device_id=(px, py, c), device_id_type=MESH))
        return copies

    return _Exchange(chip_parts, [], [jax.ShapeDtypeStruct((3,) + p.shape[1:], p.dtype) for p in chip_parts],
                     3 * n, 0, make)


def _call(body, exch, *, name, grid, in_specs, out_specs, out_shape, scratch_shapes=(), semantics,
          input_output_aliases=None):
    exch = list(exch)
    in_specs, out_specs, out_shape = list(in_specs), list(out_specs), list(out_shape)
    scratch_shapes = list(scratch_shapes)
    if not exch:
        fn = pl.pallas_call(body, name=name, grid=grid, in_specs=in_specs, out_specs=out_specs, out_shape=out_shape,
                            scratch_shapes=scratch_shapes, input_output_aliases=input_output_aliases or {},
                            compiler_params=_cparams(*semantics))
        return lambda *args: (fn(*args), [])
    n_in, n_out, n_scr = len(in_specs), len(out_specs), len(scratch_shapes)
    aliases = dict(input_output_aliases or {})
    all_in, all_out_specs, all_out_shape, all_scr = list(in_specs), list(out_specs), list(out_shape), list(scratch_shapes)
    extra_args = []
    for ex in exch:
        for k, a in enumerate(ex.io):
            aliases[len(all_in) + len(ex.ins) + k] = len(all_out_specs) + k
        all_in += [ANY] * (len(ex.ins) + len(ex.io))
        extra_args += ex.ins + ex.io
        all_out_specs += [ANY] * (len(ex.io) + len(ex.new))
        all_out_shape += [jax.ShapeDtypeStruct(a.shape, a.dtype) for a in ex.io] + ex.new
        all_scr += [pltpu.SemaphoreType.DMA((ex.n_sems,)), pltpu.SemaphoreType.DMA((ex.n_sems,)),
                    pltpu.SemaphoreType.DMA((max(ex.n_local, 1),))]

    def wrapped(*refs):
        pos = n_in
        ex_in = []
        for ex in exch:
            k = len(ex.ins) + len(ex.io)
            ex_in.append(refs[pos:pos + k])
            pos += k
        outs = refs[pos:pos + n_out]
        pos += n_out
        ex_out = []
        for ex in exch:
            k = len(ex.io) + len(ex.new)
            ex_out.append(refs[pos:pos + k])
            pos += k
        scr = refs[pos:pos + n_scr]
        pos += n_scr
        sems = [refs[pos + 3 * k:pos + 3 * k + 3] for k in range(len(exch))]
        first = functools.reduce(jnp.logical_and, [pl.program_id(a) == 0 for a in range(len(grid))])
        last = functools.reduce(jnp.logical_and, [pl.program_id(a) == g - 1 for a, g in enumerate(grid)])

        def copies():
            out = []
            for ex, ei, eo, es in zip(exch, ex_in, ex_out, sems):
                out += ex.make(ei[:len(ex.ins)], eo[:len(ex.io)], eo[len(ex.io):], *es)
            return out

        @pl.when(first)
        def _():
            for cp in copies():
                cp.start()

        body(*refs[:n_in], *outs, *scr)

        @pl.when(last)
        def _():
            for cp in copies():
                cp.wait()

    fn = pl.pallas_call(wrapped, name=name, grid=grid, in_specs=all_in, out_specs=all_out_specs,
                        out_shape=all_out_shape, scratch_shapes=all_scr, input_output_aliases=aliases,
                        compiler_params=_cparams(*(["arbitrary"] * len(grid))))

    def run(*args):
        res = fn(*args, *extra_args)
        outs, pos, ex_res = res[:n_out], n_out, []
        for ex in exch:
            k = len(ex.io) + len(ex.new)
            ex_res.append(list(res[pos:pos + k]))
            pos += k
        return outs, ex_res

    return run


def _exchange_alone(ex, name):
    def body():
        pass

    _, res = _call(body, [ex], name=name, grid=(1,), in_specs=[], out_specs=[], out_shape=[], semantics=("arbitrary",))()
    return res[0]


def _small_gather(part):
    def make(in_refs, io_refs, new_refs, send_sems, recv_sems, local_sems):
        x, y, c = _place()
        slot = new_refs[0].at[4 * x + 2 * y + c]
        copies = [pltpu.make_async_copy(in_refs[0], slot, local_sems.at[0])]
        for d in range(1, N_DEV):
            peer = (1 - x if d & 4 else x, 1 - y if d & 2 else y, 1 - c if d & 1 else c)
            copies.append(pltpu.make_async_remote_copy(
                src_ref=in_refs[0], dst_ref=slot, send_sem=send_sems.at[d - 1], recv_sem=recv_sems.at[d - 1],
                device_id=peer, device_id_type=MESH))
        return copies

    return _Exchange([part], [], [jax.ShapeDtypeStruct((N_DEV,) + part.shape, part.dtype)], N_DEV - 1, 1, make)


def _sum_over_devices(parts):
    _, rows, lanes = parts.shape

    def body(p_ref, o_ref):
        acc = p_ref[0]
        for k in range(1, N_DEV):
            acc = acc + p_ref[k]
        o_ref[...] = acc

    return pl.pallas_call(
        body, name="small_grads_sum", grid=(1,), out_shape=jax.ShapeDtypeStruct((rows, lanes), F32),
        in_specs=[pl.BlockSpec((N_DEV, rows, lanes), lambda i: (0, 0, 0))],
        out_specs=pl.BlockSpec((rows, lanes), lambda i: (0, 0)),
        compiler_params=_cparams("arbitrary"),
    )(parts)


def _transpose_bf16(a, name, exch=(), with_copy=False):
    r, c = a.shape
    tr, tc = _tile(r, 512, 128), _tile(c, 512, 128)

    def body(a_ref, o_ref, *copy_ref):
        v = a_ref[...].astype(F32)
        o_ref[...] = v.T.astype(BF16)
        if with_copy:
            copy_ref[0][...] = v.astype(BF16)

    outs, ex = _call(
        body, exch, name=name, grid=(r // tr, c // tc),
        out_shape=[jax.ShapeDtypeStruct((c, r), BF16)] + [jax.ShapeDtypeStruct((r, c), BF16)] * with_copy,
        in_specs=[pl.BlockSpec((tr, tc), lambda i, j: (i, j))],
        out_specs=[pl.BlockSpec((tc, tr), lambda i, j: (j, i))] + [pl.BlockSpec((tr, tc), lambda i, j: (i, j))] * with_copy,
        semantics=("parallel", "parallel"),
    )(a)
    return (outs if with_copy else outs[0]), ex


def _ffn_fwd(x, wgu, wd, ln_g, ln_b, name, exch=(), with_ln=True):
    t, d = x.shape
    f = wd.shape[0]
    tm, tf = _tile(t, 512, 128), _tile(f, 512, 128)
    nf = f // tf

    def body(x_ref, wg_ref, wu_ref, wd_ref, g_ref, b_ref, go_ref, uo_ref, ht_ref, z_ref, *rest):
        xn_ref = rest[0] if with_ln else None
        xb, acc = rest[-2:]
        j = pl.program_id(1)

        @pl.when(j == 0)
        def _():
            xb[...] = x_ref[...].astype(BF16)
            acc[...] = jnp.zeros_like(acc)

        g = _dot(xb[...], wg_ref[...])
        u = _dot(xb[...], wu_ref[...])
        h = g * _sigmoid(g) * u
        go_ref[...] = g.astype(BF16)
        uo_ref[...] = u.astype(BF16)
        ht_ref[...] = h.T.astype(BF16)
        acc[...] += _dot(h.astype(BF16), wd_ref[...])

        @pl.when(j == nf - 1)
        def _():
            z = ALPHA * x_ref[...] + 0.5 * acc[...]
            z_ref[...] = z
            if with_ln:
                xn_ref[...] = _ln(z, g_ref[...], b_ref[...])

    row = lambda i, j: (i, 0)
    n_td = 2 if with_ln else 1
    return _call(
        body, exch, name=name, grid=(t // tm, nf),
        out_shape=[jax.ShapeDtypeStruct((t, f), BF16), jax.ShapeDtypeStruct((t, f), BF16),
                   jax.ShapeDtypeStruct((f, t), BF16)] + [jax.ShapeDtypeStruct((t, d), F32)] * n_td,
        in_specs=[pl.BlockSpec((tm, d), row),
                  pl.BlockSpec((d, tf), lambda i, j: (0, j)),
                  pl.BlockSpec((d, tf), lambda i, j: (0, j + nf)),
                  pl.BlockSpec((tf, d), lambda i, j: (j, 0)),
                  pl.BlockSpec((1, d), lambda i, j: (0, 0)),
                  pl.BlockSpec((1, d), lambda i, j: (0, 0))],
        out_specs=[pl.BlockSpec((tm, tf), lambda i, j: (i, j)), pl.BlockSpec((tm, tf), lambda i, j: (i, j)),
                   pl.BlockSpec((tf, tm), lambda i, j: (j, i))] + [pl.BlockSpec((tm, d), row)] * n_td,
        scratch_shapes=[pltpu.VMEM((tm, d), BF16), pltpu.VMEM((tm, d), F32)],
        semantics=("parallel", "arbitrary"),
    )(x, wgu, wgu, wd, ln_g, ln_b)


def _ffn_down_fwd(gu, x, wd, ln_g, ln_b, name, exch=()):
    t, d = x.shape
    f = wd.shape[0]
    tm, tf = _tile(t, 512, 128), _tile(f, 512, 128)
    nf = f // tf

    def body(g_ref, u_ref, wd_ref, x_ref, lg_ref, lb_ref, ht_ref, z_ref, xn_ref, acc):
        j = pl.program_id(1)

        @pl.when(j == 0)
        def _():
            acc[...] = jnp.zeros_like(acc)

        g = g_ref[...].astype(F32)
        h = g * _sigmoid(g) * u_ref[...].astype(F32)
        ht_ref[...] = h.T.astype(BF16)
        acc[...] += _dot(h.astype(BF16), wd_ref[...])

        @pl.when(j == nf - 1)
        def _():
            z = ALPHA * x_ref[...] + 0.5 * acc[...]
            z_ref[...] = z
            xn_ref[...] = _ln(z, lg_ref[...], lb_ref[...])

    row = lambda i, j: (i, 0)
    fixed = lambda i, j: (0, 0)
    return _call(
        body, exch, name=name, grid=(t // tm, nf),
        out_shape=[jax.ShapeDtypeStruct((f, t), BF16), jax.ShapeDtypeStruct((t, d), F32),
                   jax.ShapeDtypeStruct((t, d), F32)],
        in_specs=[pl.BlockSpec((tm, tf), lambda i, j: (i, j)), pl.BlockSpec((tm, tf), lambda i, j: (i, j + nf)),
                  pl.BlockSpec((tf, d), lambda i, j: (j, 0)), pl.BlockSpec((tm, d), row),
                  pl.BlockSpec((1, d), fixed), pl.BlockSpec((1, d), fixed)],
        out_specs=[pl.BlockSpec((tf, tm), lambda i, j: (j, i)), pl.BlockSpec((tm, d), row), pl.BlockSpec((tm, d), row)],
        scratch_shapes=[pltpu.VMEM((tm, d), F32)],
        semantics=("parallel", "arbitrary"),
    )(gu, gu, wd, x, ln_g, ln_b)


def _ffn_act_grads(dh, g_ref, u_ref):
    gg = g_ref[...].astype(F32)
    uu = u_ref[...].astype(F32)
    s = _sigmoid(gg)
    du = (dh * (gg * s)).astype(BF16)
    dg = (dh * uu * (s * (1.0 + gg * (1.0 - s)))).astype(BF16)
    return dg, du


def _ffn_bwd(dz, do, g, u, wgu, wd, name, exch=()):
    t, d = dz.shape
    f = wd.shape[0]
    tm, tf = _tile(t, 512, 128), _tile(f, 512, 128)
    nf = f // tf

    def body(dz_ref, do_ref, g_ref, u_ref, wg_ref, wu_ref, wd_ref, dg_ref, du_ref, dx_ref, acc):
        j = pl.program_id(1)

        @pl.when(j == 0)
        def _():
            acc[...] = jnp.zeros_like(acc)

        dg, du = _ffn_act_grads(_dot_nt(do_ref[...], wd_ref[...]), g_ref, u_ref)
        dg_ref[...] = dg
        du_ref[...] = du
        acc[...] += _dot_nt(dg, wg_ref[...]) + _dot_nt(du, wu_ref[...])

        @pl.when(j == nf - 1)
        def _():
            dx_ref[...] = ALPHA * dz_ref[...] + acc[...]

    row = lambda i, j: (i, 0)
    tile = lambda i, j: (i, j)
    return _call(
        body, exch, name=name, grid=(t // tm, nf),
        out_shape=[jax.ShapeDtypeStruct((t, f), BF16), jax.ShapeDtypeStruct((t, f), BF16),
                   jax.ShapeDtypeStruct((t, d), F32)],
        in_specs=[pl.BlockSpec((tm, d), row), pl.BlockSpec((tm, d), row),
                  pl.BlockSpec((tm, tf), tile), pl.BlockSpec((tm, tf), tile),
                  pl.BlockSpec((d, tf), lambda i, j: (0, j)),
                  pl.BlockSpec((d, tf), lambda i, j: (0, j + nf)),
                  pl.BlockSpec((tf, d), lambda i, j: (j, 0))],
        out_specs=[pl.BlockSpec((tm, tf), tile), pl.BlockSpec((tm, tf), tile), pl.BlockSpec((tm, d), row)],
        scratch_shapes=[pltpu.VMEM((tm, d), F32)],
        semantics=("parallel", "arbitrary"),
    )(dz, do, g, u, wgu, wgu, wd)


def _ffn_bwd_act(do, gu, wd, name, exch=()):
    t, d = do.shape
    f = wd.shape[0]
    tm, tf = _tile(t, 512, 128), _tile(f, 512, 128)
    nf = f // tf

    def body(do_ref, g_ref, u_ref, wd_ref, dg_ref, du_ref):
        dg, du = _ffn_act_grads(_dot_nt(do_ref[...], wd_ref[...]), g_ref, u_ref)
        dg_ref[...] = dg
        du_ref[...] = du

    tile = lambda i, j: (i, j)
    return _call(
        body, exch, name=name, grid=(t // tm, f // tf),
        out_shape=[jax.ShapeDtypeStruct((t, f), BF16), jax.ShapeDtypeStruct((t, f), BF16)],
        in_specs=[pl.BlockSpec((tm, d), lambda i, j: (i, 0)), pl.BlockSpec((tm, tf), tile),
                  pl.BlockSpec((tm, tf), lambda i, j: (i, j + nf)), pl.BlockSpec((tf, d), lambda i, j: (j, 0))],
        out_specs=[pl.BlockSpec((tm, tf), tile), pl.BlockSpec((tm, tf), tile)],
        semantics=("parallel", "parallel"),
    )(do, gu, gu, wd)


def _ffn_bwd_dx(dz, dg, du, wgu, name, exch=()):
    t, d = dz.shape
    f = dg.shape[1]
    tm, tf = _tile(t, 512, 128), _tile(f, 512, 128)
    nf = f // tf

    def body(dz_ref, dg_ref, du_ref, wg_ref, wu_ref, dx_ref, acc):
        j = pl.program_id(1)

        @pl.when(j == 0)
        def _():
            acc[...] = jnp.zeros_like(acc)

        acc[...] += _dot_nt(dg_ref[...], wg_ref[...]) + _dot_nt(du_ref[...], wu_ref[...])

        @pl.when(j == nf - 1)
        def _():
            dx_ref[...] = ALPHA * dz_ref[...] + acc[...]

    row = lambda i, j: (i, 0)
    tile = lambda i, j: (i, j)
    return _call(
        body, exch, name=name, grid=(t // tm, nf), out_shape=[jax.ShapeDtypeStruct((t, d), F32)],
        in_specs=[pl.BlockSpec((tm, d), row), pl.BlockSpec((tm, tf), tile), pl.BlockSpec((tm, tf), tile),
                  pl.BlockSpec((d, tf), lambda i, j: (0, j)), pl.BlockSpec((d, tf), lambda i, j: (0, j + nf))],
        out_specs=[pl.BlockSpec((tm, d), row)],
        scratch_shapes=[pltpu.VMEM((tm, d), F32)],
        semantics=("parallel", "arbitrary"),
    )(dz, dg, du, wgu, wgu)


def _weight_grad(at, b, tn, tmm, name, blocks=None, block_offset=0, into=None, exch=()):
    m, t = at.shape
    nn = b.shape[1]
    tmm = _tile(m, tmm, 16)
    assert nn % tn == 0

    def body(*refs):
        at_ref, b_ref, o_ref = refs[0], refs[1], refs[-1]
        r = _dot(at_ref[...], b_ref[...]).astype(BF16)
        if blocks is None:
            o_ref[...] = r
        else:
            o_ref[0] = r

    in_specs = [pl.BlockSpec((tmm, t), lambda n, i: (i, 0)), pl.BlockSpec((t, tn), lambda n, i: (0, n))]
    args = [at, b]
    aliases = {}
    if into is not None:
        in_specs.append(ANY)
        args.append(into)
        aliases = {2: 0}
    if blocks is None:
        out_shape = jax.ShapeDtypeStruct((m, nn), BF16)
        out_spec = pl.BlockSpec((tmm, tn), lambda n, i: (i, n))
    else:
        out_shape = jax.ShapeDtypeStruct((blocks, m, tn), BF16)
        out_spec = pl.BlockSpec((1, tmm, tn), lambda n, i: (n + block_offset, i, 0))
    (out,), ex = _call(
        body, exch, name=name, grid=(nn // tn, m // tmm), out_shape=[out_shape],
        in_specs=in_specs, out_specs=[out_spec], input_output_aliases=aliases,
        semantics=("parallel", "parallel"),
    )(*args)
    return out, ex


def _mix_in_proj(x, w_in, name, exch=()):
    t, d = x.shape
    nb, _, cb = w_in.shape
    tm = _tile(t, 512, 128)

    def body(x_ref, w_ref, o_ref, xb):
        @pl.when(pl.program_id(1) == 0)
        def _():
            xb[...] = x_ref[...].astype(BF16)

        o_ref[...] = _dot(xb[...], w_ref[0])

    (out,), ex = _call(
        body, exch, name=name, grid=(t // tm, nb), out_shape=[jax.ShapeDtypeStruct((t, nb * cb), F32)],
        in_specs=[pl.BlockSpec((tm, d), lambda i, k: (i, 0)), pl.BlockSpec((1, d, cb), lambda i, k: (k, 0, 0))],
        out_specs=[pl.BlockSpec((tm, cb), lambda i, k: (i, k))],
        scratch_shapes=[pltpu.VMEM((tm, d), BF16)],
        semantics=("parallel", "arbitrary"),
    )(x, w_in)
    return out, ex


def _mix_in_bwd(dproj, w_in, dz, name, exch=()):
    t, d = dz.shape
    nb, _, cb = w_in.shape
    tm = _tile(t, 512, 128)

    def body(dp_ref, w_ref, dz_ref, dx_ref, acc):
        k = pl.program_id(1)

        @pl.when(k == 0)
        def _():
            acc[...] = jnp.zeros_like(acc)

        acc[...] += _dot_nt(dp_ref[...], w_ref[0])

        @pl.when(k == nb - 1)
        def _():
            dx_ref[...] = ALPHA * dz_ref[...] + acc[...]

    (out,), ex = _call(
        body, exch, name=name, grid=(t // tm, nb), out_shape=[jax.ShapeDtypeStruct((t, d), F32)],
        in_specs=[pl.BlockSpec((tm, cb), lambda i, k: (i, k)), pl.BlockSpec((1, d, cb), lambda i, k: (k, 0, 0)),
                  pl.BlockSpec((tm, d), lambda i, k: (i, 0))],
        out_specs=[pl.BlockSpec((tm, d), lambda i, k: (i, 0))],
        scratch_shapes=[pltpu.VMEM((tm, d), F32)],
        semantics=("parallel", "arbitrary"),
    )(dproj, w_in, dz)
    return out, ex


def _mix_out_fwd(y, w_out, x, ln_g, ln_b, name, exch=()):
    t, d = x.shape
    kk = y.shape[1]
    tm = _tile(t, 256, 128)

    def body(y_ref, w_ref, x_ref, g_ref, b_ref, z_ref, xn_ref, xnt_ref):
        z = ALPHA * x_ref[...] + _dot(y_ref[...], w_ref[...])
        z_ref[...] = z
        xn = _ln(z, g_ref[...], b_ref[...])
        xn_ref[...] = xn
        xnt_ref[...] = xn.T.astype(BF16)

    row = lambda i: (i, 0)
    fixed = lambda i: (0, 0)
    return _call(
        body, exch, name=name, grid=(t // tm,),
        out_shape=[jax.ShapeDtypeStruct((t, d), F32), jax.ShapeDtypeStruct((t, d), F32),
                   jax.ShapeDtypeStruct((d, t), BF16)],
        in_specs=[pl.BlockSpec((tm, kk), row), pl.BlockSpec((kk, d), fixed), pl.BlockSpec((tm, d), row),
                  pl.BlockSpec((1, d), fixed), pl.BlockSpec((1, d), fixed)],
        out_specs=[pl.BlockSpec((tm, d), row), pl.BlockSpec((tm, d), row), pl.BlockSpec((d, tm), lambda i: (0, i))],
        semantics=("parallel",),
    )(y, w_out, x, ln_g, ln_b)


def _mix_out_bwd(dzb, w_out, name):
    t, d = dzb.shape
    kk = w_out.shape[0]
    tm = _tile(t, 256, 128)

    def body(dz_ref, w_ref, dy_ref):
        dy_ref[...] = _dot_nt(dz_ref[...], w_ref[...])

    return pl.pallas_call(
        body, name=name, grid=(t // tm,), out_shape=jax.ShapeDtypeStruct((t, kk), F32),
        in_specs=[pl.BlockSpec((tm, d), lambda i: (i, 0)), pl.BlockSpec((kk, d), lambda i: (0, 0))],
        out_specs=pl.BlockSpec((tm, kk), lambda i: (i, 0)),
        compiler_params=_cparams("parallel"),
    )(dzb, w_out)


def _loss_ln_bwd(z, target, ln_g, ln_b, bf16_scale, name):
    t, d = z.shape
    tm = _tile(t, 512, 8)

    def body(z_ref, t_ref, g_ref, b_ref, dz_ref, dzb_ref, dg_ref, db_ref, loss_ref):
        @pl.when(pl.program_id(0) == 0)
        def _():
            dg_ref[...] = jnp.zeros_like(dg_ref)
            db_ref[...] = jnp.zeros_like(db_ref)
            loss_ref[...] = jnp.zeros_like(loss_ref)

        xh, rstd = _ln_stats(z_ref[...])
        e = xh * g_ref[...] + b_ref[...] - t_ref[...]
        loss_ref[...] += 0.5 * jnp.sum(jnp.sum(e * e, axis=-1, keepdims=True) * (1.0 / d), axis=0, keepdims=True)
        dy = e * (1.0 / d)
        dz = _ln_bwd(dy * g_ref[...], xh, rstd)
        dz_ref[...] = dz
        dzb_ref[...] = (bf16_scale * dz).astype(BF16)
        dg_ref[...] += jnp.sum(dy * xh, axis=0, keepdims=True)
        db_ref[...] += jnp.sum(dy, axis=0, keepdims=True)

    row = lambda i: (i, 0)
    fixed = lambda i: (0, 0)
    return pl.pallas_call(
        body, name=name, grid=(t // tm,),
        out_shape=[jax.ShapeDtypeStruct((t, d), F32), jax.ShapeDtypeStruct((t, d), BF16),
                   jax.ShapeDtypeStruct((1, d), F32), jax.ShapeDtypeStruct((1, d), F32),
                   jax.ShapeDtypeStruct((8, 128), F32)],
        in_specs=[pl.BlockSpec((tm, d), row), pl.BlockSpec((tm, d), row), pl.BlockSpec((1, d), fixed),
                  pl.BlockSpec((1, d), fixed)],
        out_specs=[pl.BlockSpec((tm, d), row), pl.BlockSpec((tm, d), row), pl.BlockSpec((1, d), fixed),
                   pl.BlockSpec((1, d), fixed), pl.BlockSpec((8, 128), fixed)],
        compiler_params=_cparams("arbitrary"),
    )(z, target, ln_g, ln_b)


def _ln_bwd_call(z, dy, ln_g, bf16_scale, name, exch=()):
    t, d = z.shape
    tm = _tile(t, 512, 8)

    def body(z_ref, dy_ref, g_ref, dz_ref, dzb_ref, dg_ref, db_ref):
        @pl.when(pl.program_id(0) == 0)
        def _():
            dg_ref[...] = jnp.zeros_like(dg_ref)
            db_ref[...] = jnp.zeros_like(db_ref)

        xh, rstd = _ln_stats(z_ref[...])
        dy = dy_ref[...]
        dz = _ln_bwd(dy * g_ref[...], xh, rstd)
        dz_ref[...] = dz
        dzb_ref[...] = (bf16_scale * dz).astype(BF16)
        dg_ref[...] += jnp.sum(dy * xh, axis=0, keepdims=True)
        db_ref[...] += jnp.sum(dy, axis=0, keepdims=True)

    row = lambda i: (i, 0)
    fixed = lambda i: (0, 0)
    return _call(
        body, exch, name=name, grid=(t // tm,),
        out_shape=[jax.ShapeDtypeStruct((t, d), F32), jax.ShapeDtypeStruct((t, d), BF16),
                   jax.ShapeDtypeStruct((1, d), F32), jax.ShapeDtypeStruct((1, d), F32)],
        in_specs=[pl.BlockSpec((tm, d), row), pl.BlockSpec((tm, d), row), pl.BlockSpec((1, d), fixed)],
        out_specs=[pl.BlockSpec((tm, d), row), pl.BlockSpec((tm, d), row), pl.BlockSpec((1, d), fixed),
                   pl.BlockSpec((1, d), fixed)],
        semantics=("arbitrary",),
    )(z, dy, ln_g)


CONV_ROWS = 32


def _mixer_fwd(proj, conv_w, conv_b, cln_g, cln_b, sln_g, sln_b, sg_wm, sg_bb, name, exch=()):
    t = proj.shape[0]
    tm = _tile(t, 256, CHUNK)
    hb = tm // HALO
    nc = tm // CHUNK
    ch = CONV_CH

    def body(av_ref, ag_ref, bu_ref, bv_ref, hv_ref, hg_ref, cw_ref, cb_ref, lg_ref, lb_ref, sg_ref, sb_ref,
             w_ref, bb_ref, y_ref, yt_ref, c_ref, ext):
        i = pl.program_id(0)
        halo = hv_ref[...] * _sigmoid(hg_ref[...])
        ext[0:HALO, :] = jnp.where(i > 0, halo, 0.0)
        ext[HALO:HALO + tm, :] = av_ref[...] * _sigmoid(ag_ref[...])
        for r in range(0, tm, CONV_ROWS):
            acc = jnp.zeros((CONV_ROWS, ch), F32) + cb_ref[...]
            for k in range(CONV_TAPS):
                lo = r + k + HALO - (CONV_TAPS - 1)
                acc = acc + cw_ref[k:k + 1, :] * ext[lo:lo + CONV_ROWS, :]
            c_ref[r:r + CONV_ROWS, :] = acc
        a = _ln(c_ref[...], lg_ref[...], lb_ref[...])
        ya = a * _sigmoid(a)
        y_ref[:, 0:ch] = ya.astype(BF16)
        yt_ref[0:ch, :] = ya.T.astype(BF16)
        for h in range(HEADS):
            sl = slice(h * HEAD_DIM, (h + 1) * HEAD_DIM)
            u, _ = _gelu_and_grad(bu_ref[:, sl])
            v, _ = _gelu_and_grad(bv_ref[:, sl])
            vn = _ln(v, sg_ref[h:h + 1, :], sb_ref[h:h + 1, :])
            vn3 = vn.astype(BF16).reshape(nc, CHUNK, HEAD_DIM)
            wb = jnp.broadcast_to(w_ref[h][None], (nc, CHUNK, CHUNK))
            mixed = jnp.einsum("cts,csd->ctd", wb, vn3, preferred_element_type=F32) + bb_ref[h][None]
            yb = u * mixed.reshape(tm, HEAD_DIM)
            y_ref[:, ch + h * HEAD_DIM:ch + (h + 1) * HEAD_DIM] = yb.astype(BF16)
            yt_ref[ch + h * HEAD_DIM:ch + (h + 1) * HEAD_DIM, :] = yb.T.astype(BF16)

    col = lambda cidx: (lambda i: (i, cidx))
    prev = lambda cidx: (lambda i: (jnp.maximum(i * hb - 1, 0), cidx))
    fix2 = lambda i: (0, 0)
    fix3 = lambda i: (0, 0, 0)
    return _call(
        body, exch, name=name, grid=(t // tm,),
        out_shape=[jax.ShapeDtypeStruct((t, 2 * ch), BF16), jax.ShapeDtypeStruct((2 * ch, t), BF16),
                   jax.ShapeDtypeStruct((t, ch), F32)],
        in_specs=[pl.BlockSpec((tm, ch), col(0)), pl.BlockSpec((tm, ch), col(1)), pl.BlockSpec((tm, ch), col(2)),
                  pl.BlockSpec((tm, ch), col(3)), pl.BlockSpec((HALO, ch), prev(0)), pl.BlockSpec((HALO, ch), prev(1)),
                  pl.BlockSpec((CONV_TAPS, ch), fix2), pl.BlockSpec((1, ch), fix2), pl.BlockSpec((1, ch), fix2),
                  pl.BlockSpec((1, ch), fix2), pl.BlockSpec((HEADS, HEAD_DIM), fix2), pl.BlockSpec((HEADS, HEAD_DIM), fix2),
                  pl.BlockSpec((HEADS, CHUNK, CHUNK), fix3), pl.BlockSpec((HEADS, CHUNK, HEAD_DIM), fix3)],
        out_specs=[pl.BlockSpec((tm, 2 * ch), lambda i: (i, 0)), pl.BlockSpec((2 * ch, tm), lambda i: (0, i)),
                   pl.BlockSpec((tm, ch), lambda i: (i, 0))],
        scratch_shapes=[pltpu.VMEM((HALO + tm, ch), F32)],
        semantics=("parallel",),
    )(proj, proj, proj, proj, proj, proj, conv_w, conv_b, cln_g, cln_b, sln_g, sln_b, sg_wm, sg_bb)


def _mixer_bwd(proj, conv_c, dy, conv_w, cln_g, cln_b, sln_g, sln_b, sg_wm, sg_wmt, sg_bb, name, exch=()):
    t = proj.shape[0]
    tm = _tile(t, 256, CHUNK)
    hb = tm // HALO
    nc = tm // CHUNK
    nt = t // tm
    ch = CONV_CH
    last_halo = t // HALO - 1

    def body(av_ref, ag_ref, bu_ref, bv_ref, hv_ref, hg_ref, c_ref, cn_ref, dya_ref, dyan_ref, dyb_ref,
             cw_ref, lg_ref, lb_ref, sg_ref, sb_ref, w_ref, wt_ref, bb_ref,
             dp_ref, dcw_ref, dcb_ref, dlg_ref, dlb_ref, dsg_ref, dsb_ref, dw_ref, dbs_ref,
             ext_h, ext_dc, acc_cw):
        i = pl.program_id(0)

        @pl.when(i == 0)
        def _():
            acc_cw[...] = jnp.zeros_like(acc_cw)
            for ref in (dcb_ref, dlg_ref, dlb_ref, dsg_ref, dsb_ref, dw_ref, dbs_ref):
                ref[...] = jnp.zeros_like(ref)

        lg = lg_ref[...]
        lb = lb_ref[...]

        def conv_ln_bwd(c, dya):
            xh, rstd = _ln_stats(c)
            a = xh * lg + lb
            da = dya * _silu_grad(a)
            return _ln_bwd(da * lg, xh, rstd), da, xh

        dc, da, xh = conv_ln_bwd(c_ref[...], dya_ref[...])
        dlg_ref[...] += jnp.sum(da * xh, axis=0, keepdims=True)
        dlb_ref[...] += jnp.sum(da, axis=0, keepdims=True)
        dcb_ref[...] += jnp.sum(dc, axis=0, keepdims=True)
        dcn, _, _ = conv_ln_bwd(cn_ref[...], dyan_ref[...])
        ext_dc[0:tm, :] = dc
        ext_dc[tm:tm + HALO, :] = jnp.where(i < nt - 1, dcn, 0.0)
        sig_g = _sigmoid(ag_ref[...])
        halo = hv_ref[...] * _sigmoid(hg_ref[...])
        ext_h[0:HALO, :] = jnp.where(i > 0, halo, 0.0)
        ext_h[HALO:HALO + tm, :] = av_ref[...] * sig_g
        for r in range(0, tm, CONV_ROWS):
            dcr = ext_dc[r:r + CONV_ROWS, :]
            acc = jnp.zeros((CONV_ROWS, ch), F32)
            for k in range(CONV_TAPS):
                lo = r + k + HALO - (CONV_TAPS - 1)
                prod = dcr * ext_h[lo:lo + CONV_ROWS, :]
                acc_cw[k] += jnp.sum(prod.reshape(CONV_ROWS // 8, 8, ch), axis=0)
                hi = r + (CONV_TAPS - 1) - k
                acc = acc + cw_ref[k:k + 1, :] * ext_dc[hi:hi + CONV_ROWS, :]
            sg_r = sig_g[r:r + CONV_ROWS, :]
            av_r = av_ref[r:r + CONV_ROWS, :]
            dp_ref[r:r + CONV_ROWS, 0:ch] = (acc * sg_r).astype(BF16)
            dp_ref[r:r + CONV_ROWS, ch:2 * ch] = (acc * av_r * sg_r * (1.0 - sg_r)).astype(BF16)

        @pl.when(i == nt - 1)
        def _():
            dcw_ref[...] = jnp.sum(acc_cw[...], axis=1)

        tril = (lax.broadcasted_iota(jnp.int32, (CHUNK, CHUNK), 0)
                >= lax.broadcasted_iota(jnp.int32, (CHUNK, CHUNK), 1)).astype(F32)
        for h in range(HEADS):
            sl = slice(h * HEAD_DIM, (h + 1) * HEAD_DIM)
            u, du_dx = _gelu_and_grad(bu_ref[:, sl])
            v, dv_dx = _gelu_and_grad(bv_ref[:, sl])
            xhv, rstdv = _ln_stats(v)
            gh = sg_ref[h:h + 1, :]
            vn3 = (xhv * gh + sb_ref[h:h + 1, :]).astype(BF16).reshape(nc, CHUNK, HEAD_DIM)
            wb = jnp.broadcast_to(w_ref[h][None], (nc, CHUNK, CHUNK))
            mixed = jnp.einsum("cts,csd->ctd", wb, vn3, preferred_element_type=F32) + bb_ref[h][None]
            dyb = dyb_ref[:, sl]
            d_u = dyb * mixed.reshape(tm, HEAD_DIM)
            dm = dyb * u
            dm3 = dm.reshape(nc, CHUNK, HEAD_DIM)
            dbs_ref[h:h + 1, :] += jnp.sum(jnp.sum(dm3, axis=0).T, axis=0, keepdims=True)
            dm3b = dm3.astype(BF16)
            dw_h = jnp.sum(jnp.einsum("ctd,csd->cts", dm3b, vn3, preferred_element_type=F32), axis=0)
            dw_ref[h] += dw_h * tril
            wtb = jnp.broadcast_to(wt_ref[h][None], (nc, CHUNK, CHUNK))
            d_vn = jnp.einsum("cst,ctd->csd", wtb, dm3b, preferred_element_type=F32).reshape(tm, HEAD_DIM)
            dsg_ref[h:h + 1, :] += jnp.sum(d_vn * xhv, axis=0, keepdims=True)
            dsb_ref[h:h + 1, :] += jnp.sum(d_vn, axis=0, keepdims=True)
            dv = _ln_bwd(d_vn * gh, xhv, rstdv)
            dp_ref[:, 2 * ch + h * HEAD_DIM:2 * ch + (h + 1) * HEAD_DIM] = (d_u * du_dx).astype(BF16)
            dp_ref[:, 3 * ch + h * HEAD_DIM:3 * ch + (h + 1) * HEAD_DIM] = (dv * dv_dx).astype(BF16)

    col = lambda cidx: (lambda i: (i, cidx))
    prev = lambda cidx: (lambda i: (jnp.maximum(i * hb - 1, 0), cidx))
    nxt = lambda i: (jnp.minimum((i + 1) * hb, last_halo), 0)
    fix2 = lambda i: (0, 0)
    fix3 = lambda i: (0, 0, 0)
    out_shape = [jax.ShapeDtypeStruct((t, 4 * ch), BF16), jax.ShapeDtypeStruct((CONV_TAPS, ch), F32),
                 jax.ShapeDtypeStruct((1, ch), F32), jax.ShapeDtypeStruct((1, ch), F32), jax.ShapeDtypeStruct((1, ch), F32),
                 jax.ShapeDtypeStruct((HEADS, HEAD_DIM), F32), jax.ShapeDtypeStruct((HEADS, HEAD_DIM), F32),
                 jax.ShapeDtypeStruct((HEADS, CHUNK, CHUNK), F32), jax.ShapeDtypeStruct((HEADS, CHUNK), F32)]
    out_specs = [pl.BlockSpec((tm, 4 * ch), lambda i: (i, 0)), pl.BlockSpec((CONV_TAPS, ch), fix2),
                 pl.BlockSpec((1, ch), fix2), pl.BlockSpec((1, ch), fix2), pl.BlockSpec((1, ch), fix2),
                 pl.BlockSpec((HEADS, HEAD_DIM), fix2), pl.BlockSpec((HEADS, HEAD_DIM), fix2),
                 pl.BlockSpec((HEADS, CHUNK, CHUNK), fix3), pl.BlockSpec((HEADS, CHUNK), fix2)]
    in_specs = [pl.BlockSpec((tm, ch), col(0)), pl.BlockSpec((tm, ch), col(1)), pl.BlockSpec((tm, ch), col(2)),
                pl.BlockSpec((tm, ch), col(3)), pl.BlockSpec((HALO, ch), prev(0)), pl.BlockSpec((HALO, ch), prev(1)),
                pl.BlockSpec((tm, ch), col(0)), pl.BlockSpec((HALO, ch), nxt),
                pl.BlockSpec((tm, ch), col(0)), pl.BlockSpec((HALO, ch), nxt), pl.BlockSpec((tm, ch), col(1)),
                pl.BlockSpec((CONV_TAPS, ch), fix2), pl.BlockSpec((1, ch), fix2), pl.BlockSpec((1, ch), fix2),
                pl.BlockSpec((HEADS, HEAD_DIM), fix2), pl.BlockSpec((HEADS, HEAD_DIM), fix2),
                pl.BlockSpec((HEADS, CHUNK, CHUNK), fix3), pl.BlockSpec((HEADS, CHUNK, CHUNK), fix3),
                pl.BlockSpec((HEADS, CHUNK, HEAD_DIM), fix3)]
    return _call(
        body, exch, name=name, grid=(nt,), out_shape=out_shape, in_specs=in_specs, out_specs=out_specs,
        scratch_shapes=[pltpu.VMEM((HALO + tm, ch), F32), pltpu.VMEM((tm + HALO, ch), F32),
                        pltpu.VMEM((CONV_TAPS, 8, ch), F32)],
        semantics=("arbitrary",),
    )(proj, proj, proj, proj, proj, proj, conv_c, conv_c, dy, dy, dy,
      conv_w, cln_g, cln_b, sln_g, sln_b, sg_wm, sg_wmt, sg_bb)


def _pair_sum(parts, from_sibling, c, name):
    _, r, cc = parts.shape
    tr = _tile(r, max(16, (1 << 20) // (2 * cc)), 16)

    def body(c_ref, p_ref, s_ref, o_ref):
        o_ref[...] = (p_ref[...].astype(F32) + s_ref[...].astype(F32)).astype(BF16)

    grid_spec = pltpu.PrefetchScalarGridSpec(
        num_scalar_prefetch=1, grid=(4, r // tr),
        in_specs=[pl.BlockSpec((1, tr, cc), lambda j, i, c_ref: (2 * j + c_ref[0], i, 0)),
                  pl.BlockSpec((1, tr, cc), lambda j, i, c_ref: (j, i, 0))],
        out_specs=pl.BlockSpec((1, tr, cc), lambda j, i, c_ref: (j, i, 0)))
    return pl.pallas_call(
        body, name=name, grid_spec=grid_spec, out_shape=jax.ShapeDtypeStruct((4, r, cc), BF16),
        compiler_params=_cparams("parallel", "parallel"),
    )(c, parts, from_sibling)


def _adamw_math(w, g, m, v):
    m = ADAM_B1 * m + (1.0 - ADAM_B1) * g
    v = ADAM_B2 * v + (1.0 - ADAM_B2) * (g * g)
    m_hat = m / (1.0 - ADAM_B1 ** ADAM_STEP)
    v_hat = v / (1.0 - ADAM_B2 ** ADAM_STEP)
    delta = -ADAM_LR * (m_hat / (jnp.sqrt(v_hat) + ADAM_EPS) + ADAM_WD * w)
    return delta, m, v


def _adamw_sharded(w, m, v, chip_parts, from_chips, chip, name):
    r, cc = w.shape
    tr = _tile(r, max(16, (1 << 19) // (4 * cc) * 2), 16)

    def body(j_ref, w_ref, m_ref, v_ref, q_ref, o_ref, g_out, d_out, m_out, v_out):
        g = q_ref[0].astype(F32)
        for k in range(3):
            g = g + o_ref[k].astype(F32)
        d, mm, vv = _adamw_math(w_ref[...], g, m_ref[...], v_ref[...])
        g_out[...] = g
        d_out[...] = d
        m_out[...] = mm
        v_out[...] = vv

    row = lambda i, j_ref: (i, 0)
    grid_spec = pltpu.PrefetchScalarGridSpec(
        num_scalar_prefetch=1, grid=(r // tr,),
        in_specs=[pl.BlockSpec((tr, cc), row), pl.BlockSpec((tr, cc), row), pl.BlockSpec((tr, cc), row),
                  pl.BlockSpec((1, tr, cc), lambda i, j_ref: (j_ref[0], i, 0)),
                  pl.BlockSpec((3, tr, cc), lambda i, j_ref: (0, i, 0))],
        out_specs=[pl.BlockSpec((tr, cc), row)] * 4)
    return pl.pallas_call(
        body, name=name, grid_spec=grid_spec, out_shape=[jax.ShapeDtypeStruct((r, cc), F32)] * 4,
        compiler_params=_cparams("parallel"),
    )(chip, w, m, v, chip_parts, from_chips)


def _adamw_small(w, g, m, v, name):
    r, cc = w.shape

    def body(w_ref, g_ref, m_ref, v_ref, d_out, m_out, v_out):
        d, mm, vv = _adamw_math(w_ref[...], g_ref[...], m_ref[...], v_ref[...])
        d_out[...] = d
        m_out[...] = mm
        v_out[...] = vv

    full = pl.BlockSpec((r, cc), lambda i: (0, 0))
    return pl.pallas_call(
        body, name=name, grid=(1,), out_shape=[jax.ShapeDtypeStruct((r, cc), F32)] * 3,
        in_specs=[full] * 4, out_specs=[full] * 3, compiler_params=_cparams("arbitrary"),
    )(w, g, m, v)


SMALL = ("ln1_g", "ln1_b", "conv_b", "conv_ln_g", "conv_ln_b", "sg_ln_g", "sg_ln_b", "sg_w", "sg_b",
         "ln2_g", "ln2_b", "ln3_g", "ln3_b")
ORDER = ("ffn1_w_gate_up", "ffn1_w_down", "ln1_g", "ln1_b", "mix_w_in", "conv_w", "conv_b", "conv_ln_g", "conv_ln_b",
         "sg_ln_g", "sg_ln_b", "sg_w", "sg_b", "mix_w_out", "ln2_g", "ln2_b", "ffn2_w_gate_up", "ffn2_w_down",
         "ln3_g", "ln3_b")


def _rows128(a):
    return a.reshape(-1, 128)


def kernel(x, ffn1_w_gate_up, ffn1_w_down, ln1_g, ln1_b, mix_w_in, conv_w, conv_b, conv_ln_g, conv_ln_b, sg_ln_g, sg_ln_b, sg_w, sg_b, mix_w_out, ln2_g, ln2_b, ffn2_w_gate_up, ffn2_w_down, ln3_g, ln3_b, loss_target, m_ffn1_w_gate_up, m_ffn1_w_down, m_ln1_g, m_ln1_b, m_mix_w_in, m_conv_w, m_conv_b, m_conv_ln_g, m_conv_ln_b, m_sg_ln_g, m_sg_ln_b, m_sg_w, m_sg_b, m_mix_w_out, m_ln2_g, m_ln2_b, m_ffn2_w_gate_up, m_ffn2_w_down, m_ln3_g, m_ln3_b, v_ffn1_w_gate_up, v_ffn1_w_down, v_ln1_g, v_ln1_b, v_mix_w_in, v_conv_w, v_conv_b, v_conv_ln_g, v_conv_ln_b, v_sg_ln_g, v_sg_ln_b, v_sg_w, v_sg_b, v_mix_w_out, v_ln2_g, v_ln2_b, v_ffn2_w_gate_up, v_ffn2_w_down, v_ln3_g, v_ln3_b):
    args = dict(locals())
    w = {n: args[n][0] for n in ORDER}
    mom = {n: args["m_" + n][0] for n in ORDER}
    var = {n: args["v_" + n][0] for n in ORDER}
    x0 = x[0]
    target = loss_target[0]
    t, d = x0.shape
    my_x, my_y, my_c = lax.axis_index("x"), lax.axis_index("y"), lax.axis_index("c")
    my_chip = (2 * my_x + my_y).astype(jnp.int32).reshape(1)
    my_core = my_c.astype(jnp.int32).reshape(1)
    me = 4 * my_x + 2 * my_y + my_c

    big = ("ffn1_w_gate_up", "ffn1_w_down", "mix_w_in", "mix_w_out", "ffn2_w_gate_up", "ffn2_w_down")
    sh = {n: w[n].astype(BF16) for n in big}
    f2s = sh["ffn2_w_gate_up"].shape[1]
    (x0t, x0b), _ = _transpose_bf16(x0, "x0_transpose", with_copy=True)
    order = jnp.stack([4 * p[0] + 2 * p[1] + p[2] for p in _visit_order(my_x, my_y, my_c)]).astype(jnp.int32)
    gu1, (wgu1, wd1, conv_w_all) = _gather_and_gate_up(
        x0b, [sh["ffn1_w_gate_up"], sh["ffn1_w_down"], w["conv_w"]], order, "ffn1_gate_up_fwd")
    wd1 = wd1.reshape(-1, d)
    conv_w_full = jnp.transpose(conv_w_all, (1, 0, 2)).reshape(CONV_TAPS, CONV_CH)
    tril = jnp.tril(jnp.ones((CHUNK, CHUNK), F32))
    sg_wm = w["sg_w"] * tril
    sg_wm_b = sg_wm.astype(BF16)
    sg_wmt_b = jnp.swapaxes(sg_wm, 1, 2).astype(BF16)
    sg_bb = jnp.broadcast_to(w["sg_b"][:, :, None], (HEADS, CHUNK, HEAD_DIM))
    row = lambda a: a.reshape(1, -1)

    (h1t, z1, x1), ((g_in, g_out),) = _ffn_down_fwd(
        gu1, x0, wd1, row(w["ln1_g"]), row(w["ln1_b"]), "ffn1_down_fwd",
        exch=[_gather_first([sh["mix_w_in"], sh["mix_w_out"]], [False, False])])
    x1t, ((w_in, w_out),) = _transpose_bf16(
        x1, "x1_transpose", exch=[_gather_forward([g_in, g_out], [False, False], [None, None])])
    w_out = w_out.reshape(-1, d)
    top, bottom = (0, d // 2), (d // 2, d // 2)
    gu2 = [sh["ffn2_w_gate_up"]]
    proj, ((g_gu2,),) = _mix_in_proj(x1, w_in, "mix_in_fwd", exch=[_gather_first(gu2, [True], rows=top)])
    (y, yt, conv_c), ((g_gu2,),) = _mixer_fwd(
        proj, conv_w_full, row(w["conv_b"]), row(w["conv_ln_g"]), row(w["conv_ln_b"]),
        w["sg_ln_g"], w["sg_ln_b"], sg_wm_b, sg_bb, "mixer_fwd",
        exch=[_both(_gather_first(gu2, [True], rows=bottom, into=[g_gu2]),
                    _gather_forward([g_gu2], [True], [f2s], rows=top))])
    (z2, x2, x2t), ((wgu2,), (g_d2,)) = _mix_out_fwd(
        y, w_out, x1, row(w["ln2_g"]), row(w["ln2_b"]), "mix_out_fwd",
        exch=[_gather_forward([g_gu2], [True], [f2s], rows=bottom), _gather_first([sh["ffn2_w_down"]], [False])])
    (wd2,) = _exchange_alone(_gather_forward([g_d2], [False], [None]), "ffn2_down_gather_forward")
    wd2 = wd2.reshape(-1, d)
    (g2, u2, h2t, z3), _ = _ffn_fwd(x2, wgu2, wd2, row(w["ln3_g"]), row(w["ln3_b"]), "ffn2_fwd", with_ln=False)

    f = wd1.shape[0]
    dn = _tile(d, 1024, 128)
    grads = {}
    pair = lambda p, s, label: _pair_sum(p, s, my_core, "pair_sum_" + label)
    dz3, do2, grads["ln3_g"], grads["ln3_b"], loss_tile = _loss_ln_bwd(
        z3, target, row(w["ln3_g"]), row(w["ln3_b"]), 0.5, "loss_ln3_bwd")
    p_d2, _ = _weight_grad(h2t, do2, dn, 512, "ffn2_dw_down")
    p_d2 = p_d2.reshape(N_DEV, f // N_DEV, d)
    (dg2, du2, dx2), ((s_d2,),) = _ffn_bwd(dz3, do2, g2, u2, wgu2, wd2, "ffn2_bwd", exch=[_rs_sibling([p_d2])])
    q_d2 = pair(p_d2, s_d2, "ffn2_down")
    p_gu2, ((r_d2,),) = _weight_grad(x2t, dg2, f2s, 512, "ffn2_dw_gate", blocks=N_DEV, exch=[_rs_chips([q_d2])])
    p_gu2, _ = _weight_grad(x2t, du2, f2s, 512, "ffn2_dw_up", blocks=N_DEV, block_offset=4, into=p_gu2)
    (dz2, dz2b, grads["ln2_g"], grads["ln2_b"]), ((s_gu2,),) = _ln_bwd_call(
        z2, dx2, row(w["ln2_g"]), 1.0, "ln2_bwd", exch=[_rs_sibling([p_gu2])])
    q_gu2 = pair(p_gu2, s_gu2, "ffn2_gate_up")
    dy = _mix_out_bwd(dz2b, w_out, "mix_out_bwd")
    p_out, _ = _weight_grad(yt, dz2b, dn, 512, "mix_out_dw")
    p_out = p_out.reshape(N_DEV, -1, d)
    (dproj, grads["conv_w"], grads["conv_b"], grads["conv_ln_g"], grads["conv_ln_b"], grads["sg_ln_g"],
     grads["sg_ln_b"], grads["sg_w"], grads["sg_b"]), ((r_gu2,),) = _mixer_bwd(
        proj, conv_c, dy, conv_w_full, row(w["conv_ln_g"]), row(w["conv_ln_b"]), w["sg_ln_g"], w["sg_ln_b"],
        sg_wm_b, sg_wmt_b, sg_bb, "mixer_bwd", exch=[_rs_chips([q_gu2])])
    dx1, ((s_out,),) = _mix_in_bwd(dproj, w_in, dz2, "mix_in_bwd", exch=[_rs_sibling([p_out])])
    p_in, _ = _weight_grad(x1t, dproj, w_in.shape[2], 512, "mix_in_dw", blocks=N_DEV)
    (dz1, do1, grads["ln1_g"], grads["ln1_b"]), ((s_in,),) = _ln_bwd_call(
        z1, dx1, row(w["ln1_g"]), 0.5, "ln1_bwd", exch=[_rs_sibling([p_in])])
    q_out = pair(p_out, s_out, "mix_out")
    q_in = pair(p_in, s_in, "mix_in")
    small_parts = [_rows128(grads[n]) for n in SMALL]
    packed = jnp.concatenate(small_parts + [_rows128(grads["conv_w"]), loss_tile], axis=0)
    p_d1, ((r_in,),) = _weight_grad(h1t, do1, dn, 512, "ffn1_dw_down", exch=[_rs_chips([q_in])])
    p_d1 = p_d1.reshape(N_DEV, f // N_DEV, d)
    (dg1, du1), ((s_d1,), (r_out,), (small_all,)) = _ffn_bwd_act(
        do1, gu1, wd1, "ffn1_bwd_act",
        exch=[_rs_sibling([p_d1]), _rs_chips([q_out]), _small_gather(packed)])
    q_d1 = pair(p_d1, s_d1, "ffn1_down")
    p_gu1, ((r_d1,),) = _weight_grad(x0t, dg1, f2s, 512, "ffn1_dw_gate", blocks=N_DEV, exch=[_rs_chips([q_d1])])
    p_gu1, _ = _weight_grad(x0t, du1, f2s, 512, "ffn1_dw_up", blocks=N_DEV, block_offset=4, into=p_gu1)
    (s_gu1,) = _exchange_alone(_rs_sibling([p_gu1]), "ffn1_gate_up_sibling_exchange")
    q_gu1 = pair(p_gu1, s_gu1, "ffn1_gate_up")
    (grad_x,), ((r_gu1,),) = _ffn_bwd_dx(dz1, dg1, du1, wgu1, "ffn1_bwd_dx", exch=[_rs_chips([q_gu1])])

    chip_parts = [q_gu1, q_d1, q_in, q_out, q_gu2, q_d2]
    from_chips = [r_gu1, r_d1, r_in, r_out, r_gu2, r_d2]
    out = {}
    for k, n in enumerate(big):
        out[n] = _adamw_sharded(w[n], mom[n], var[n], chip_parts[k], from_chips[k], my_chip, "adamw_" + n)

    cw_rows = CONV_TAPS * CONV_CH // 128
    total = _sum_over_devices(small_all)
    offs = [0]
    for p in small_parts:
        offs.append(offs[-1] + p.shape[0])
    n_small = offs[-1]
    loss = total[n_small + cw_rows, 0]
    g_conv_w = lax.dynamic_slice_in_dim(total[n_small:n_small + cw_rows].reshape(CONV_TAPS, CONV_CH),
                                        me * (CONV_CH // N_DEV), CONV_CH // N_DEV, axis=1)
    pad8 = lambda a: jnp.pad(a, ((0, -a.shape[0] % 8), (0, 0)))
    pack = lambda tree, cw: jnp.concatenate([_rows128(tree[n]) for n in SMALL] + [pad8(cw)], axis=0)
    g_pack = jnp.concatenate([total[:n_small], pad8(g_conv_w)], axis=0)
    d_pack, m_pack, v_pack = _adamw_small(pack(w, w["conv_w"]), g_pack, pack(mom, mom["conv_w"]),
                                          pack(var, var["conv_w"]), "adamw_small")
    for k, n in enumerate(SMALL):
        sl = slice(offs[k], offs[k + 1])
        shp = w[n].shape
        out[n] = (total[sl].reshape(shp), d_pack[sl].reshape(shp), m_pack[sl].reshape(shp), v_pack[sl].reshape(shp))
    sl = slice(n_small, n_small + CONV_TAPS)
    out["conv_w"] = (g_conv_w, d_pack[sl], m_pack[sl], v_pack[sl])

    lead = lambda a: a[None]
    res = [loss, grad_x[None]]
    for kind in range(4):
        res += [lead(out[n][kind]) for n in ORDER]
    return tuple(res)
```

```python
import functools
import math

import jax
import jax.numpy as jnp
from jax import lax
from jax.experimental import pallas as pl
from jax.experimental.pallas import tpu as pltpu

F32, BF16 = jnp.float32, jnp.bfloat16
MESH = pl.DeviceIdType.MESH
ANY = pl.BlockSpec(memory_space=pl.ANY)

N_DEV = 8
LN_EPS = 1e-5
ALPHA = 2.0 ** 0.25
CONV_CH = 1024
CONV_TAPS = 31
HALO = 32
HEADS = 8
HEAD_DIM = 128
CHUNK = 128
ADAM_LR, ADAM_B1, ADAM_B2, ADAM_EPS, ADAM_WD, ADAM_STEP = 0.001, 0.9, 0.999, 1e-08, 0.01, 10
V7X_VMEM_LIMIT = 56 * 2 ** 20


def _cparams(*sem):
    return pltpu.CompilerParams(dimension_semantics=sem, vmem_limit_bytes=V7X_VMEM_LIMIT)


def _tile(n, pref, mult):
    best = None
    for t in range(mult, min(n, pref) + 1, mult):
        if n % t == 0:
            best = t
    return best if best is not None else n


def _dot(a, b):
    return jnp.dot(a, b, preferred_element_type=F32)


def _dot_nt(a, b):
    return lax.dot_general(a, b, (((1,), (1,)), ((), ())), preferred_element_type=F32)


def _sigmoid(x):
    return 1.0 / (1.0 + jnp.exp(-x))


def _ln_stats(z):
    mu = jnp.mean(z, axis=-1, keepdims=True)
    zc = z - mu
    var = jnp.mean(zc * zc, axis=-1, keepdims=True)
    rstd = lax.rsqrt(var + LN_EPS)
    return zc * rstd, rstd


def _ln(z, g, b):
    xh, _ = _ln_stats(z)
    return xh * g + b


def _ln_bwd(dxh, xh, rstd):
    m1 = jnp.mean(dxh, axis=-1, keepdims=True)
    m2 = jnp.mean(dxh * xh, axis=-1, keepdims=True)
    return rstd * (dxh - m1 - xh * m2)


_GK = math.sqrt(2.0 / math.pi)
_GA = 0.044715


def _gelu_and_grad(x):
    x2 = x * x
    t = jnp.tanh(_GK * (x + _GA * x * x2))
    y = 0.5 * x * (1.0 + t)
    dy = 0.5 * (1.0 + t) + 0.5 * x * (1.0 - t * t) * (_GK * (1.0 + 3.0 * _GA * x2))
    return y, dy


def _silu_grad(a):
    s = _sigmoid(a)
    return s * (1.0 + a * (1.0 - s))


def _place():
    return lax.axis_index("x"), lax.axis_index("y"), lax.axis_index("c")


def _other_chips(x, y):
    return [(1 - x, y), (x, 1 - y), (1 - x, 1 - y)]


def _visit_order(x, y, c):
    chips = _other_chips(x, y)
    return [(x, y, c), (x, y, 1 - c), (*chips[0], c), (*chips[1], c), (*chips[0], 1 - c), (*chips[1], 1 - c),
            (*chips[2], c), (*chips[2], 1 - c)]


def _gather_and_gate_up(xb, shards, relayed, order, name):
    n = len(shards)
    N_COPIES = 10
    t, d = xb.shape
    cols = shards[0].shape[1]
    tm = _tile(t, 512, 128)
    ni = t // tm
    col_major = [True] + [False] * (n - 1)

    def body(order_ref, x_ref, *refs):
        srcs, gu_ref, dsts = refs[:n], refs[n], refs[n + 1:2 * n + 1]
        wbuf, send_sems, recv_sems, local_sems, load_sem = refs[2 * n + 1:]
        b, i = pl.program_id(0), pl.program_id(1)
        x, y, c = _place()
        me, sib = (x, y, c), (x, y, 1 - c)
        chips = _other_chips(x, y)

        near_x, near_y, far = chips

        def slot(w, p, band=None):
            half = shards[w].shape[0] // 2
            rows = None if band is None else (band * half, half)
            return _block_slot(dsts[w], col_major[w], shards[w].shape[1], p, rows)

        def copy(w, s, block, to, band=None, from_src=False):
            return pltpu.make_async_remote_copy(
                src_ref=srcs[w] if from_src else slot(w, block, band), dst_ref=slot(w, block, band),
                send_sem=send_sems.at[N_COPIES * w + s], recv_sem=recv_sems.at[N_COPIES * w + s],
                device_id=to, device_id_type=MESH)

        def own(w):
            return pltpu.make_async_copy(srcs[w], slot(w, me), local_sems.at[w])

        def sends(w):
            out = [copy(w, 0, me, sib, from_src=True), copy(w, 1, me, (*near_x, c), from_src=True),
                   copy(w, 2, me, (*near_y, c), from_src=True)]
            if not relayed[w]:
                out.append(copy(w, 3, me, (*far, c), from_src=True))
            return out

        def passed_on(w):
            out = [copy(w, 4, (*near_x, c), sib), copy(w, 5, (*near_y, c), sib)]
            if relayed[w]:
                out += [copy(w, 6, (*far, c), sib, band=0), copy(w, 9, (*far, c), sib, band=1),
                        copy(w, 7, (*near_x, c), (*near_y, c), band=0), copy(w, 8, (*near_y, c), (*near_x, c), band=1)]
            else:
                out.append(copy(w, 6, (*far, c), sib))
            return out

        def start_sends(w):
            own(w).start()
            for cp in sends(w):
                cp.start()

        def got_near_x(w):
            copy(w, 1, (*near_x, c), me).wait_recv()
            copy(w, 4, (*near_x, c), sib).start()
            if relayed[w]:
                copy(w, 7, (*near_x, c), (*near_y, c), band=0).start()

        def got_near_y(w):
            copy(w, 2, (*near_y, c), me).wait_recv()
            copy(w, 5, (*near_y, c), sib).start()
            if relayed[w]:
                copy(w, 8, (*near_y, c), (*near_x, c), band=1).start()

        def got_far(w):
            if relayed[w]:
                copy(w, 7, (*far, c), me, band=0).wait_recv()
                copy(w, 6, (*far, c), sib, band=0).start()
                copy(w, 8, (*far, c), me, band=1).wait_recv()
                copy(w, 9, (*far, c), sib, band=1).start()
            else:
                copy(w, 3, (*far, c), me).wait_recv()
                copy(w, 6, (*far, c), sib).start()

        def got_from_sibling(w, which):
            if which == 0:
                copy(w, 0, sib, me).wait_recv()
            elif which == 3 and relayed[w]:
                copy(w, 6, (*far, 1 - c), me, band=0).wait_recv()
                copy(w, 9, (*far, 1 - c), me, band=1).wait_recv()
            else:
                copy(w, 3 + which, (*chips[which - 1], 1 - c), me).wait_recv()

        others = range(1, n)

        def arrive(k):
            if k == 0:
                own(0).wait()
            elif k == 1:
                got_from_sibling(0, 0)
            elif k == 2:
                got_near_x(0)
                for w in others:
                    start_sends(w)
            elif k == 3:
                got_near_y(0)
            elif k in (4, 5):
                got_from_sibling(0, k - 3)
            elif k == 6:
                got_far(0)
                for w in others:
                    got_near_x(w)
                    got_near_y(w)
            else:
                got_from_sibling(0, 3)
                for w in others:
                    got_far(w)

        @pl.when((b == 0) & (i == 0))
        def _():
            start_sends(0)

        for k in range(N_DEV):
            @pl.when((b == k) & (i == 0))
            def _(k=k):
                arrive(k)
                at = pl.multiple_of(order_ref[k] * cols, 128)
                load = pltpu.make_async_copy(dsts[0].at[:, pl.ds(at, cols)], wbuf, load_sem.at[0])
                load.start()
                load.wait()

        gu_ref[...] = _dot(x_ref[...], wbuf[...]).astype(BF16)

        @pl.when((b == N_DEV - 1) & (i == ni - 1))
        def _():
            for w in others:
                for which in range(4):
                    got_from_sibling(w, which)
                own(w).wait()
            for w in range(n):
                for cp in sends(w) + passed_on(w):
                    cp.wait_send()

    grid_spec = pltpu.PrefetchScalarGridSpec(
        num_scalar_prefetch=1, grid=(N_DEV, ni),
        in_specs=[pl.BlockSpec((tm, d), lambda b, i, o: (i, 0))] + [ANY] * n,
        out_specs=[pl.BlockSpec((tm, cols), lambda b, i, o: (i, o[b]))] + [ANY] * n,
        scratch_shapes=[pltpu.VMEM((d, cols), BF16), pltpu.SemaphoreType.DMA((N_COPIES * n,)),
                        pltpu.SemaphoreType.DMA((N_COPIES * n,)), pltpu.SemaphoreType.DMA((n,)),
                        pltpu.SemaphoreType.DMA((1,))])
    res = pl.pallas_call(
        body, name=name, grid_spec=grid_spec,
        out_shape=[jax.ShapeDtypeStruct((t, N_DEV * cols), BF16)]
        + [_gathered_shape(s, cm) for s, cm in zip(shards, col_major)],
        compiler_params=_cparams("arbitrary", "arbitrary"),
    )(order, xb, *shards)
    return res[0], res[1:]


class _Exchange:
    def __init__(self, ins, io, new, n_sems, n_local, make):
        self.ins, self.io, self.new = list(ins), list(io), list(new)
        self.n_sems, self.n_local, self.make = n_sems, n_local, make


def _block_slot(ref, col_major, cols, place, rows=None):
    k = 4 * place[0] + 2 * place[1] + place[2]
    band = slice(None) if rows is None else pl.ds(rows[0], rows[1])
    if col_major:
        return ref.at[band, pl.ds(pl.multiple_of(k * cols, 128), cols)]
    return ref.at[k] if rows is None else ref.at[k, band]


def _gathered_shape(s, col_major):
    return jax.ShapeDtypeStruct((s.shape[0], N_DEV * s.shape[1]) if col_major else (N_DEV,) + s.shape, s.dtype)


def _gather_first(shards, col_major, rows=None, into=None):
    n = len(shards)
    new = [] if into is not None else [_gathered_shape(s, cm) for s, cm in zip(shards, col_major)]

    def make(in_refs, io_refs, new_refs, send_sems, recv_sems, local_sems, base=0, local_base=0):
        x, y, c = _place()
        targets = [(x, y, 1 - c)] + [(*chip, c) for chip in _other_chips(x, y)]
        gathered = io_refs if into is not None else new_refs
        copies = []
        for w in range(n):
            src = in_refs[w] if rows is None else in_refs[w].at[pl.ds(rows[0], rows[1])]
            slot = _block_slot(gathered[w], col_major[w], shards[w].shape[1], (x, y, c), rows)
            copies.append(pltpu.make_async_copy(src, slot, local_sems.at[local_base + w]))
            for s, to in enumerate(targets):
                copies.append(pltpu.make_async_remote_copy(
                    src_ref=src, dst_ref=slot, send_sem=send_sems.at[base + 4 * w + s],
                    recv_sem=recv_sems.at[base + 4 * w + s], device_id=to, device_id_type=MESH))
        return copies

    return _Exchange(shards, into or [], new, 4 * n, n, make)


def _gather_forward(gathered, col_major, cols, rows=None):
    n = len(gathered)

    def make(in_refs, io_refs, new_refs, send_sems, recv_sems, local_sems, base=0, local_base=0):
        x, y, c = _place()
        copies = []
        for w in range(n):
            for j, chip in enumerate(_other_chips(x, y)):
                slot = _block_slot(io_refs[w], col_major[w], cols[w], (*chip, c), rows)
                copies.append(pltpu.make_async_remote_copy(
                    src_ref=slot, dst_ref=slot, send_sem=send_sems.at[base + 3 * w + j],
                    recv_sem=recv_sems.at[base + 3 * w + j], device_id=(x, y, 1 - c), device_id_type=MESH))
        return copies

    return _Exchange([], gathered, [], 3 * n, 0, make)


def _both(a, b):
    def make(in_refs, io_refs, new_refs, send_sems, recv_sems, local_sems):
        na = len(a.ins)
        return (a.make(in_refs[:na], io_refs, [], send_sems, recv_sems, local_sems, 0, 0)
                + b.make(in_refs[na:], io_refs, [], send_sems, recv_sems, local_sems, a.n_sems, a.n_local))

    return _Exchange(a.ins + b.ins, a.io, [], a.n_sems + b.n_sems, a.n_local + b.n_local, make)


def _rs_sibling(parts):
    n = len(parts)

    def make(in_refs, io_refs, new_refs, send_sems, recv_sems, local_sems):
        x, y, c = _place()
        copies = []
        for w in range(n):
            for j in range(4):
                copies.append(pltpu.make_async_remote_copy(
                    src_ref=in_refs[w].at[2 * j + (1 - c)], dst_ref=new_refs[w].at[j],
                    send_sem=send_sems.at[4 * w + j], recv_sem=recv_sems.at[4 * w + j],
                    device_id=(x, y, 1 - c), device_id_type=MESH))
        return copies

    return _Exchange(parts, [], [jax.ShapeDtypeStruct((4,) + p.shape[1:], p.dtype) for p in parts], 4 * n, 0, make)


def _rs_chips(chip_parts):
    n = len(chip_parts)

    def make(in_refs, io_refs, new_refs, send_sems, recv_sems, local_sems):
        x, y, c = _place()
        copies = []
        for w in range(n):
            for rel, (px, py) in enumerate(_other_chips(x, y)):
                copies.append(pltpu.make_async_remote_copy(
                    src_ref=in_refs[w].at[2 * px + py], dst_ref=new_refs[w].at[rel],
                    send_sem=send_sems.at[3 * w + rel], recv_sem=recv_sems.at[3 * w + rel],
                    device_id=(px, py, c), device_id_type=MESH))
        return copies

    return _Exchange(chip_parts, [], [jax.ShapeDtypeStruct((3,) + p.shape[1:], p.dtype) for p in chip_parts],
                     3 * n, 0, make)


def _call(body, exch, *, name, grid, in_specs, out_specs, out_shape, scratch_shapes=(), semantics,
          input_output_aliases=None):
    exch = list(exch)
    in_specs, out_specs, out_shape = list(in_specs), list(out_specs), list(out_shape)
    scratch_shapes = list(scratch_shapes)
    if not exch:
        fn = pl.pallas_call(body, name=name, grid=grid, in_specs=in_specs, out_specs=out_specs, out_shape=out_shape,
                            scratch_shapes=scratch_shapes, input_output_aliases=input_output_aliases or {},
                            compiler_params=_cparams(*semantics))
        return lambda *args: (fn(*args), [])
    n_in, n_out, n_scr = len(in_specs), len(out_specs), len(scratch_shapes)
    aliases = dict(input_output_aliases or {})
    all_in, all_out_specs, all_out_shape, all_scr = list(in_specs), list(out_specs), list(out_shape), list(scratch_shapes)
    extra_args = []
    for ex in exch:
        for k, a in enumerate(ex.io):
            aliases[len(all_in) + len(ex.ins) + k] = len(all_out_specs) + k
        all_in += [ANY] * (len(ex.ins) + len(ex.io))
        extra_args += ex.ins + ex.io
        all_out_specs += [ANY] * (len(ex.io) + len(ex.new))
        all_out_shape += [jax.ShapeDtypeStruct(a.shape, a.dtype) for a in ex.io] + ex.new
        all_scr += [pltpu.SemaphoreType.DMA((ex.n_sems,)), pltpu.SemaphoreType.DMA((ex.n_sems,)),
                    pltpu.SemaphoreType.DMA((max(ex.n_local, 1),))]

    def wrapped(*refs):
        pos = n_in
        ex_in = []
        for ex in exch:
            k = len(ex.ins) + len(ex.io)
            ex_in.append(refs[pos:pos + k])
            pos += k
        outs = refs[pos:pos + n_out]
        pos += n_out
        ex_out = []
        for ex in exch:
            k = len(ex.io) + len(ex.new)
            ex_out.append(refs[pos:pos + k])
            pos += k
        scr = refs[pos:pos + n_scr]
        pos += n_scr
        sems = [refs[pos + 3 * k:pos + 3 * k + 3] for k in range(len(exch))]
        first = functools.reduce(jnp.logical_and, [pl.program_id(a) == 0 for a in range(len(grid))])
        last = functools.reduce(jnp.logical_and, [pl.program_id(a) == g - 1 for a, g in enumerate(grid)])

        def copies():
            out = []
            for ex, ei, eo, es in zip(exch, ex_in, ex_out, sems):
                out += ex.make(ei[:len(ex.ins)], eo[:len(ex.io)], eo[len(ex.io):], *es)
            return out

        @pl.when(first)
        def _():
            for cp in copies():
                cp.start()

        body(*refs[:n_in], *outs, *scr)

        @pl.when(last)
        def _():
            for cp in copies():
                cp.wait()

    fn = pl.pallas_call(wrapped, name=name, grid=grid, in_specs=all_in, out_specs=all_out_specs,
                        out_shape=all_out_shape, scratch_shapes=all_scr, input_output_aliases=aliases,
                        compiler_params=_cparams(*(["arbitrary"] * len(grid))))

    def run(*args):
        res = fn(*args, *extra_args)
        outs, pos, ex_res = res[:n_out], n_out, []
        for ex in exch:
            k = len(ex.io) + len(ex.new)
            ex_res.append(list(res[pos:pos + k]))
            pos += k
        return outs, ex_res

    return run


def _exchange_alone(ex, name):
    def body():
        pass

    _, res = _call(body, [ex], name=name, grid=(1,), in_specs=[], out_specs=[], out_shape=[], semantics=("arbitrary",))()
    return res[0]


def _small_gather(part):
    def make(in_refs, io_refs, new_refs, send_sems, recv_sems, local_sems):
        x, y, c = _place()
        slot = new_refs[0].at[4 * x + 2 * y + c]
        copies = [pltpu.make_async_copy(in_refs[0], slot, local_sems.at[0])]
        for d in range(1, N_DEV):
            peer = (1 - x if d & 4 else x, 1 - y if d & 2 else y, 1 - c if d & 1 else c)
            copies.append(pltpu.make_async_remote_copy(
                src_ref=in_refs[0], dst_ref=slot, send_sem=send_sems.at[d - 1], recv_sem=recv_sems.at[d - 1],
                device_id=peer, device_id_type=MESH))
        return copies

    return _Exchange([part], [], [jax.ShapeDtypeStruct((N_DEV,) + part.shape, part.dtype)], N_DEV - 1, 1, make)


def _sum_over_devices(parts):
    _, rows, lanes = parts.shape

    def body(p_ref, o_ref):
        acc = p_ref[0]
        for k in range(1, N_DEV):
            acc = acc + p_ref[k]
        o_ref[...] = acc

    return pl.pallas_call(
        body, name="small_grads_sum", grid=(1,), out_shape=jax.ShapeDtypeStruct((rows, lanes), F32),
        in_specs=[pl.BlockSpec((N_DEV, rows, lanes), lambda i: (0, 0, 0))],
        out_specs=pl.BlockSpec((rows, lanes), lambda i: (0, 0)),
        compiler_params=_cparams("arbitrary"),
    )(parts)


def _transpose_bf16(a, name, exch=(), with_copy=False):
    r, c = a.shape
    tr, tc = _tile(r, 512, 128), _tile(c, 512, 128)

    def body(a_ref, o_ref, *copy_ref):
        v = a_ref[...].astype(F32)
        o_ref[...] = v.T.astype(BF16)
        if with_copy:
            copy_ref[0][...] = v.astype(BF16)

    outs, ex = _call(
        body, exch, name=name, grid=(r // tr, c // tc),
        out_shape=[jax.ShapeDtypeStruct((c, r), BF16)] + [jax.ShapeDtypeStruct((r, c), BF16)] * with_copy,
        in_specs=[pl.BlockSpec((tr, tc), lambda i, j: (i, j))],
        out_specs=[pl.BlockSpec((tc, tr), lambda i, j: (j, i))] + [pl.BlockSpec((tr, tc), lambda i, j: (i, j))] * with_copy,
        semantics=("parallel", "parallel"),
    )(a)
    return (outs if with_copy else outs[0]), ex


def _ffn_fwd(x, wgu, wd, ln_g, ln_b, name, exch=(), with_ln=True):
    t, d = x.shape
    f = wd.shape[0]
    tm, tf = _tile(t, 512, 128), _tile(f, 512, 128)
    nf = f // tf

    def body(x_ref, wg_ref, wu_ref, wd_ref, g_ref, b_ref, go_ref, uo_ref, ht_ref, z_ref, *rest):
        xn_ref = rest[0] if with_ln else None
        xb, acc = rest[-2:]
        j = pl.program_id(1)

        @pl.when(j == 0)
        def _():
            xb[...] = x_ref[...].astype(BF16)
            acc[...] = jnp.zeros_like(acc)

        g = _dot(xb[...], wg_ref[...])
        u = _dot(xb[...], wu_ref[...])
        h = g * _sigmoid(g) * u
        go_ref[...] = g.astype(BF16)
        uo_ref[...] = u.astype(BF16)
        ht_ref[...] = h.T.astype(BF16)
        acc[...] += _dot(h.astype(BF16), wd_ref[...])

        @pl.when(j == nf - 1)
        def _():
            z = ALPHA * x_ref[...] + 0.5 * acc[...]
            z_ref[...] = z
            if with_ln:
                xn_ref[...] = _ln(z, g_ref[...], b_ref[...])

    row = lambda i, j: (i, 0)
    n_td = 2 if with_ln else 1
    return _call(
        body, exch, name=name, grid=(t // tm, nf),
        out_shape=[jax.ShapeDtypeStruct((t, f), BF16), jax.ShapeDtypeStruct((t, f), BF16),
                   jax.ShapeDtypeStruct((f, t), BF16)] + [jax.ShapeDtypeStruct((t, d), F32)] * n_td,
        in_specs=[pl.BlockSpec((tm, d), row),
                  pl.BlockSpec((d, tf), lambda i, j: (0, j)),
                  pl.BlockSpec((d, tf), lambda i, j: (0, j + nf)),
                  pl.BlockSpec((tf, d), lambda i, j: (j, 0)),
                  pl.BlockSpec((1, d), lambda i, j: (0, 0)),
                  pl.BlockSpec((1, d), lambda i, j: (0, 0))],
        out_specs=[pl.BlockSpec((tm, tf), lambda i, j: (i, j)), pl.BlockSpec((tm, tf), lambda i, j: (i, j)),
                   pl.BlockSpec((tf, tm), lambda i, j: (j, i))] + [pl.BlockSpec((tm, d), row)] * n_td,
        scratch_shapes=[pltpu.VMEM((tm, d), BF16), pltpu.VMEM((tm, d), F32)],
        semantics=("parallel", "arbitrary"),
    )(x, wgu, wgu, wd, ln_g, ln_b)


def _ffn_down_fwd(gu, x, wd, ln_g, ln_b, name, exch=()):
    t, d = x.shape
    f = wd.shape[0]
    tm, tf = _tile(t, 512, 128), _tile(f, 512, 128)
    nf = f // tf

    def body(g_ref, u_ref, wd_ref, x_ref, lg_ref, lb_ref, ht_ref, z_ref, xn_ref, acc):
        j = pl.program_id(1)

        @pl.when(j == 0)
        def _():
            acc[...] = jnp.zeros_like(acc)

        g = g_ref[...].astype(F32)
        h = g * _sigmoid(g) * u_ref[...].astype(F32)
        ht_ref[...] = h.T.astype(BF16)
        acc[...] += _dot(h.astype(BF16), wd_ref[...])

        @pl.when(j == nf - 1)
        def _():
            z = ALPHA * x_ref[...] + 0.5 * acc[...]
            z_ref[...] = z
            xn_ref[...] = _ln(z, lg_ref[...], lb_ref[...])

    row = lambda i, j: (i, 0)
    fixed = lambda i, j: (0, 0)
    return _call(
        body, exch, name=name, grid=(t // tm, nf),
        out_shape=[jax.ShapeDtypeStruct((f, t), BF16), jax.ShapeDtypeStruct((t, d), F32),
                   jax.ShapeDtypeStruct((t, d), F32)],
        in_specs=[pl.BlockSpec((tm, tf), lambda i, j: (i, j)), pl.BlockSpec((tm, tf), lambda i, j: (i, j + nf)),
                  pl.BlockSpec((tf, d), lambda i, j: (j, 0)), pl.BlockSpec((tm, d), row),
                  pl.BlockSpec((1, d), fixed), pl.BlockSpec((1, d), fixed)],
        out_specs=[pl.BlockSpec((tf, tm), lambda i, j: (j, i)), pl.BlockSpec((tm, d), row), pl.BlockSpec((tm, d), row)],
        scratch_shapes=[pltpu.VMEM((tm, d), F32)],
        semantics=("parallel", "arbitrary"),
    )(gu, gu, wd, x, ln_g, ln_b)


def _ffn_act_grads(dh, g_ref, u_ref):
    gg = g_ref[...].astype(F32)
    uu = u_ref[...].astype(F32)
    s = _sigmoid(gg)
    du = (dh * (gg * s)).astype(BF16)
    dg = (dh * uu * (s * (1.0 + gg * (1.0 - s)))).astype(BF16)
    return dg, du


def _ffn_bwd(dz, do, g, u, wgu, wd, name, exch=()):
    t, d = dz.shape
    f = wd.shape[0]
    tm, tf = _tile(t, 512, 128), _tile(f, 512, 128)
    nf = f // tf

    def body(dz_ref, do_ref, g_ref, u_ref, wg_ref, wu_ref, wd_ref, dg_ref, du_ref, dx_ref, acc):
        j = pl.program_id(1)

        @pl.when(j == 0)
        def _():
            acc[...] = jnp.zeros_like(acc)

        dg, du = _ffn_act_grads(_dot_nt(do_ref[...], wd_ref[...]), g_ref, u_ref)
        dg_ref[...] = dg
        du_ref[...] = du
        acc[...] += _dot_nt(dg, wg_ref[...]) + _dot_nt(du, wu_ref[...])

        @pl.when(j == nf - 1)
        def _():
            dx_ref[...] = ALPHA * dz_ref[...] + acc[...]

    row = lambda i, j: (i, 0)
    tile = lambda i, j: (i, j)
    return _call(
        body, exch, name=name, grid=(t // tm, nf),
        out_shape=[jax.ShapeDtypeStruct((t, f), BF16), jax.ShapeDtypeStruct((t, f), BF16),
                   jax.ShapeDtypeStruct((t, d), F32)],
        in_specs=[pl.BlockSpec((tm, d), row), pl.BlockSpec((tm, d), row),
                  pl.BlockSpec((tm, tf), tile), pl.BlockSpec((tm, tf), tile),
                  pl.BlockSpec((d, tf), lambda i, j: (0, j)),
                  pl.BlockSpec((d, tf), lambda i, j: (0, j + nf)),
                  pl.BlockSpec((tf, d), lambda i, j: (j, 0))],
        out_specs=[pl.BlockSpec((tm, tf), tile), pl.BlockSpec((tm, tf), tile), pl.BlockSpec((tm, d), row)],
        scratch_shapes=[pltpu.VMEM((tm, d), F32)],
        semantics=("parallel", "arbitrary"),
    )(dz, do, g, u, wgu, wgu, wd)


def _ffn_bwd_act(do, gu, wd, name, exch=()):
    t, d = do.shape
    f = wd.shape[0]
    tm, tf = _tile(t, 512, 128), _tile(f, 512, 128)
    nf = f // tf

    def body(do_ref, g_ref, u_ref, wd_ref, dg_ref, du_ref):
        dg, du = _ffn_act_grads(_dot_nt(do_ref[...], wd_ref[...]), g_ref, u_ref)
        dg_ref[...] = dg
        du_ref[...] = du

    tile = lambda i, j: (i, j)
    return _call(
        body, exch, name=name, grid=(t // tm, f // tf),
        out_shape=[jax.ShapeDtypeStruct((t, f), BF16), jax.ShapeDtypeStruct((t, f), BF16)],
        in_specs=[pl.BlockSpec((tm, d), lambda i, j: (i, 0)), pl.BlockSpec((tm, tf), tile),
                  pl.BlockSpec((tm, tf), lambda i, j: (i, j + nf)), pl.BlockSpec((tf, d), lambda i, j: (j, 0))],
        out_specs=[pl.BlockSpec((tm, tf), tile), pl.BlockSpec((tm, tf), tile)],
        semantics=("parallel", "parallel"),
    )(do, gu, gu, wd)


def _ffn_bwd_dx(dz, dg, du, wgu, name, exch=()):
    t, d = dz.shape
    f = dg.shape[1]
    tm, tf = _tile(t, 512, 128), _tile(f, 512, 128)
    nf = f // tf

    def body(dz_ref, dg_ref, du_ref, wg_ref, wu_ref, dx_ref, acc):
        j = pl.program_id(1)

        @pl.when(j == 0)
        def _():
            acc[...] = jnp.zeros_like(acc)

        acc[...] += _dot_nt(dg_ref[...], wg_ref[...]) + _dot_nt(du_ref[...], wu_ref[...])

        @pl.when(j == nf - 1)
        def _():
            dx_ref[...] = ALPHA * dz_ref[...] + acc[...]

    row = lambda i, j: (i, 0)
    tile = lambda i, j: (i, j)
    return _call(
        body, exch, name=name, grid=(t // tm, nf), out_shape=[jax.ShapeDtypeStruct((t, d), F32)],
        in_specs=[pl.BlockSpec((tm, d), row), pl.BlockSpec((tm, tf), tile), pl.BlockSpec((tm, tf), tile),
                  pl.BlockSpec((d, tf), lambda i, j: (0, j)), pl.BlockSpec((d, tf), lambda i, j: (0, j + nf))],
        out_specs=[pl.BlockSpec((tm, d), row)],
        scratch_shapes=[pltpu.VMEM((tm, d), F32)],
        semantics=("parallel", "arbitrary"),
    )(dz, dg, du, wgu, wgu)


def _weight_grad(at, b, tn, tmm, name, blocks=None, block_offset=0, into=None, exch=()):
    m, t = at.shape
    nn = b.shape[1]
    tmm = _tile(m, tmm, 16)
    assert nn % tn == 0

    def body(*refs):
        at_ref, b_ref, o_ref = refs[0], refs[1], refs[-1]
        r = _dot(at_ref[...], b_ref[...]).astype(BF16)
        if blocks is None:
            o_ref[...] = r
        else:
            o_ref[0] = r

    in_specs = [pl.BlockSpec((tmm, t), lambda n, i: (i, 0)), pl.BlockSpec((t, tn), lambda n, i: (0, n))]
    args = [at, b]
    aliases = {}
    if into is not None:
        in_specs.append(ANY)
        args.append(into)
        aliases = {2: 0}
    if blocks is None:
        out_shape = jax.ShapeDtypeStruct((m, nn), BF16)
        out_spec = pl.BlockSpec((tmm, tn), lambda n, i: (i, n))
    else:
        out_shape = jax.ShapeDtypeStruct((blocks, m, tn), BF16)
        out_spec = pl.BlockSpec((1, tmm, tn), lambda n, i: (n + block_offset, i, 0))
    (out,), ex = _call(
        body, exch, name=name, grid=(nn // tn, m // tmm), out_shape=[out_shape],
        in_specs=in_specs, out_specs=[out_spec], input_output_aliases=aliases,
        semantics=("parallel", "parallel"),
    )(*args)
    return out, ex


def _mix_in_proj(x, w_in, name, exch=()):
    t, d = x.shape
    nb, _, cb = w_in.shape
    tm = _tile(t, 512, 128)

    def body(x_ref, w_ref, o_ref, xb):
        @pl.when(pl.program_id(1) == 0)
        def _():
            xb[...] = x_ref[...].astype(BF16)

        o_ref[...] = _dot(xb[...], w_ref[0])

    (out,), ex = _call(
        body, exch, name=name, grid=(t // tm, nb), out_shape=[jax.ShapeDtypeStruct((t, nb * cb), F32)],
        in_specs=[pl.BlockSpec((tm, d), lambda i, k: (i, 0)), pl.BlockSpec((1, d, cb), lambda i, k: (k, 0, 0))],
        out_specs=[pl.BlockSpec((tm, cb), lambda i, k: (i, k))],
        scratch_shapes=[pltpu.VMEM((tm, d), BF16)],
        semantics=("parallel", "arbitrary"),
    )(x, w_in)
    return out, ex


def _mix_in_bwd(dproj, w_in, dz, name, exch=()):
    t, d = dz.shape
    nb, _, cb = w_in.shape
    tm = _tile(t, 512, 128)

    def body(dp_ref, w_ref, dz_ref, dx_ref, acc):
        k = pl.program_id(1)

        @pl.when(k == 0)
        def _():
            acc[...] = jnp.zeros_like(acc)

        acc[...] += _dot_nt(dp_ref[...], w_ref[0])

        @pl.when(k == nb - 1)
        def _():
            dx_ref[...] = ALPHA * dz_ref[...] + acc[...]

    (out,), ex = _call(
        body, exch, name=name, grid=(t // tm, nb), out_shape=[jax.ShapeDtypeStruct((t, d), F32)],
        in_specs=[pl.BlockSpec((tm, cb), lambda i, k: (i, k)), pl.BlockSpec((1, d, cb), lambda i, k: (k, 0, 0)),
                  pl.BlockSpec((tm, d), lambda i, k: (i, 0))],
        out_specs=[pl.BlockSpec((tm, d), lambda i, k: (i, 0))],
        scratch_shapes=[pltpu.VMEM((tm, d), F32)],
        semantics=("parallel", "arbitrary"),
    )(dproj, w_in, dz)
    return out, ex


def _mix_out_fwd(y, w_out, x, ln_g, ln_b, name, exch=()):
    t, d = x.shape
    kk = y.shape[1]
    tm = _tile(t, 256, 128)

    def body(y_ref, w_ref, x_ref, g_ref, b_ref, z_ref, xn_ref, xnt_ref):
        z = ALPHA * x_ref[...] + _dot(y_ref[...], w_ref[...])
        z_ref[...] = z
        xn = _ln(z, g_ref[...], b_ref[...])
        xn_ref[...] = xn
        xnt_ref[...] = xn.T.astype(BF16)

    row = lambda i: (i, 0)
    fixed = lambda i: (0, 0)
    return _call(
        body, exch, name=name, grid=(t // tm,),
        out_shape=[jax.ShapeDtypeStruct((t, d), F32), jax.ShapeDtypeStruct((t, d), F32),
                   jax.ShapeDtypeStruct((d, t), BF16)],
        in_specs=[pl.BlockSpec((tm, kk), row), pl.BlockSpec((kk, d), fixed), pl.BlockSpec((tm, d), row),
                  pl.BlockSpec((1, d), fixed), pl.BlockSpec((1, d), fixed)],
        out_specs=[pl.BlockSpec((tm, d), row), pl.BlockSpec((tm, d), row), pl.BlockSpec((d, tm), lambda i: (0, i))],
        semantics=("parallel",),
    )(y, w_out, x, ln_g, ln_b)


def _mix_out_bwd(dzb, w_out, name):
    t, d = dzb.shape
    kk = w_out.shape[0]
    tm = _tile(t, 256, 128)

    def body(dz_ref, w_ref, dy_ref):
        dy_ref[...] = _dot_nt(dz_ref[...], w_ref[...])

    return pl.pallas_call(
        body, name=name, grid=(t // tm,), out_shape=jax.ShapeDtypeStruct((t, kk), F32),
        in_specs=[pl.BlockSpec((tm, d), lambda i: (i, 0)), pl.BlockSpec((kk, d), lambda i: (0, 0))],
        out_specs=pl.BlockSpec((tm, kk), lambda i: (i, 0)),
        compiler_params=_cparams("parallel"),
    )(dzb, w_out)


def _loss_ln_bwd(z, target, ln_g, ln_b, bf16_scale, name):
    t, d = z.shape
    tm = _tile(t, 512, 8)

    def body(z_ref, t_ref, g_ref, b_ref, dz_ref, dzb_ref, dg_ref, db_ref, loss_ref):
        @pl.when(pl.program_id(0) == 0)
        def _():
            dg_ref[...] = jnp.zeros_like(dg_ref)
            db_ref[...] = jnp.zeros_like(db_ref)
            loss_ref[...] = jnp.zeros_like(loss_ref)

        xh, rstd = _ln_stats(z_ref[...])
        e = xh * g_ref[...] + b_ref[...] - t_ref[...]
        loss_ref[...] += 0.5 * jnp.sum(jnp.sum(e * e, axis=-1, keepdims=True) * (1.0 / d), axis=0, keepdims=True)
        dy = e * (1.0 / d)
        dz = _ln_bwd(dy * g_ref[...], xh, rstd)
        dz_ref[...] = dz
        dzb_ref[...] = (bf16_scale * dz).astype(BF16)
        dg_ref[...] += jnp.sum(dy * xh, axis=0, keepdims=True)
        db_ref[...] += jnp.sum(dy, axis=0, keepdims=True)

    row = lambda i: (i, 0)
    fixed = lambda i: (0, 0)
    return pl.pallas_call(
        body, name=name, grid=(t // tm,),
        out_shape=[jax.ShapeDtypeStruct((t, d), F32), jax.ShapeDtypeStruct((t, d), BF16),
                   jax.ShapeDtypeStruct((1, d), F32), jax.ShapeDtypeStruct((1, d), F32),
                   jax.ShapeDtypeStruct((8, 128), F32)],
        in_specs=[pl.BlockSpec((tm, d), row), pl.BlockSpec((tm, d), row), pl.BlockSpec((1, d), fixed),
                  pl.BlockSpec((1, d), fixed)],
        out_specs=[pl.BlockSpec((tm, d), row), pl.BlockSpec((tm, d), row), pl.BlockSpec((1, d), fixed),
                   pl.BlockSpec((1, d), fixed), pl.BlockSpec((8, 128), fixed)],
        compiler_params=_cparams("arbitrary"),
    )(z, target, ln_g, ln_b)


def _ln_bwd_call(z, dy, ln_g, bf16_scale, name, exch=()):
    t, d = z.shape
    tm = _tile(t, 512, 8)

    def body(z_ref, dy_ref, g_ref, dz_ref, dzb_ref, dg_ref, db_ref):
        @pl.when(pl.program_id(0) == 0)
        def _():
            dg_ref[...] = jnp.zeros_like(dg_ref)
            db_ref[...] = jnp.zeros_like(db_ref)

        xh, rstd = _ln_stats(z_ref[...])
        dy = dy_ref[...]
        dz = _ln_bwd(dy * g_ref[...], xh, rstd)
        dz_ref[...] = dz
        dzb_ref[...] = (bf16_scale * dz).astype(BF16)
        dg_ref[...] += jnp.sum(dy * xh, axis=0, keepdims=True)
        db_ref[...] += jnp.sum(dy, axis=0, keepdims=True)

    row = lambda i: (i, 0)
    fixed = lambda i: (0, 0)
    return _call(
        body, exch, name=name, grid=(t // tm,),
        out_shape=[jax.ShapeDtypeStruct((t, d), F32), jax.ShapeDtypeStruct((t, d), BF16),
                   jax.ShapeDtypeStruct((1, d), F32), jax.ShapeDtypeStruct((1, d), F32)],
        in_specs=[pl.BlockSpec((tm, d), row), pl.BlockSpec((tm, d), row), pl.BlockSpec((1, d), fixed)],
        out_specs=[pl.BlockSpec((tm, d), row), pl.BlockSpec((tm, d), row), pl.BlockSpec((1, d), fixed),
                   pl.BlockSpec((1, d), fixed)],
        semantics=("arbitrary",),
    )(z, dy, ln_g)


CONV_ROWS = 32


def _mixer_fwd(proj, conv_w, conv_b, cln_g, cln_b, sln_g, sln_b, sg_wm, sg_bb, name, exch=()):
    t = proj.shape[0]
    tm = _tile(t, 256, CHUNK)
    hb = tm // HALO
    nc = tm // CHUNK
    ch = CONV_CH

    def body(av_ref, ag_ref, bu_ref, bv_ref, hv_ref, hg_ref, cw_ref, cb_ref, lg_ref, lb_ref, sg_ref, sb_ref,
             w_ref, bb_ref, y_ref, yt_ref, c_ref, ext):
        i = pl.program_id(0)
        halo = hv_ref[...] * _sigmoid(hg_ref[...])
        ext[0:HALO, :] = jnp.where(i > 0, halo, 0.0)
        ext[HALO:HALO + tm, :] = av_ref[...] * _sigmoid(ag_ref[...])
        for r in range(0, tm, CONV_ROWS):
            acc = jnp.zeros((CONV_ROWS, ch), F32) + cb_ref[...]
            for k in range(CONV_TAPS):
                lo = r + k + HALO - (CONV_TAPS - 1)
                acc = acc + cw_ref[k:k + 1, :] * ext[lo:lo + CONV_ROWS, :]
            c_ref[r:r + CONV_ROWS, :] = acc
        a = _ln(c_ref[...], lg_ref[...], lb_ref[...])
        ya = a * _sigmoid(a)
        y_ref[:, 0:ch] = ya.astype(BF16)
        yt_ref[0:ch, :] = ya.T.astype(BF16)
        for h in range(HEADS):
            sl = slice(h * HEAD_DIM, (h + 1) * HEAD_DIM)
            u, _ = _gelu_and_grad(bu_ref[:, sl])
            v, _ = _gelu_and_grad(bv_ref[:, sl])
            vn = _ln(v, sg_ref[h:h + 1, :], sb_ref[h:h + 1, :])
            vn3 = vn.astype(BF16).reshape(nc, CHUNK, HEAD_DIM)
            wb = jnp.broadcast_to(w_ref[h][None], (nc, CHUNK, CHUNK))
            mixed = jnp.einsum("cts,csd->ctd", wb, vn3, preferred_element_type=F32) + bb_ref[h][None]
            yb = u * mixed.reshape(tm, HEAD_DIM)
            y_ref[:, ch + h * HEAD_DIM:ch + (h + 1) * HEAD_DIM] = yb.astype(BF16)
            yt_ref[ch + h * HEAD_DIM:ch + (h + 1) * HEAD_DIM, :] = yb.T.astype(BF16)

    col = lambda cidx: (lambda i: (i, cidx))
    prev = lambda cidx: (lambda i: (jnp.maximum(i * hb - 1, 0), cidx))
    fix2 = lambda i: (0, 0)
    fix3 = lambda i: (0, 0, 0)
    return _call(
        body, exch, name=name, grid=(t // tm,),
        out_shape=[jax.ShapeDtypeStruct((t, 2 * ch), BF16), jax.ShapeDtypeStruct((2 * ch, t), BF16),
                   jax.ShapeDtypeStruct((t, ch), F32)],
        in_specs=[pl.BlockSpec((tm, ch), col(0)), pl.BlockSpec((tm, ch), col(1)), pl.BlockSpec((tm, ch), col(2)),
                  pl.BlockSpec((tm, ch), col(3)), pl.BlockSpec((HALO, ch), prev(0)), pl.BlockSpec((HALO, ch), prev(1)),
                  pl.BlockSpec((CONV_TAPS, ch), fix2), pl.BlockSpec((1, ch), fix2), pl.BlockSpec((1, ch), fix2),
                  pl.BlockSpec((1, ch), fix2), pl.BlockSpec((HEADS, HEAD_DIM), fix2), pl.BlockSpec((HEADS, HEAD_DIM), fix2),
                  pl.BlockSpec((HEADS, CHUNK, CHUNK), fix3), pl.BlockSpec((HEADS, CHUNK, HEAD_DIM), fix3)],
        out_specs=[pl.BlockSpec((tm, 2 * ch), lambda i: (i, 0)), pl.BlockSpec((2 * ch, tm), lambda i: (0, i)),
                   pl.BlockSpec((tm, ch), lambda i: (i, 0))],
        scratch_shapes=[pltpu.VMEM((HALO + tm, ch), F32)],
        semantics=("parallel",),
    )(proj, proj, proj, proj, proj, proj, conv_w, conv_b, cln_g, cln_b, sln_g, sln_b, sg_wm, sg_bb)


def _mixer_bwd(proj, conv_c, dy, conv_w, cln_g, cln_b, sln_g, sln_b, sg_wm, sg_wmt, sg_bb, name, exch=()):
    t = proj.shape[0]
    tm = _tile(t, 256, CHUNK)
    hb = tm // HALO
    nc = tm // CHUNK
    nt = t // tm
    ch = CONV_CH
    last_halo = t // HALO - 1

    def body(av_ref, ag_ref, bu_ref, bv_ref, hv_ref, hg_ref, c_ref, cn_ref, dya_ref, dyan_ref, dyb_ref,
             cw_ref, lg_ref, lb_ref, sg_ref, sb_ref, w_ref, wt_ref, bb_ref,
             dp_ref, dcw_ref, dcb_ref, dlg_ref, dlb_ref, dsg_ref, dsb_ref, dw_ref, dbs_ref,
             ext_h, ext_dc, acc_cw):
        i = pl.program_id(0)

        @pl.when(i == 0)
        def _():
            acc_cw[...] = jnp.zeros_like(acc_cw)
            for ref in (dcb_ref, dlg_ref, dlb_ref, dsg_ref, dsb_ref, dw_ref, dbs_ref):
                ref[...] = jnp.zeros_like(ref)

        lg = lg_ref[...]
        lb = lb_ref[...]

        def conv_ln_bwd(c, dya):
            xh, rstd = _ln_stats(c)
            a = xh * lg + lb
            da = dya * _silu_grad(a)
            return _ln_bwd(da * lg, xh, rstd), da, xh

        dc, da, xh = conv_ln_bwd(c_ref[...], dya_ref[...])
        dlg_ref[...] += jnp.sum(da * xh, axis=0, keepdims=True)
        dlb_ref[...] += jnp.sum(da, axis=0, keepdims=True)
        dcb_ref[...] += jnp.sum(dc, axis=0, keepdims=True)
        dcn, _, _ = conv_ln_bwd(cn_ref[...], dyan_ref[...])
        ext_dc[0:tm, :] = dc
        ext_dc[tm:tm + HALO, :] = jnp.where(i < nt - 1, dcn, 0.0)
        sig_g = _sigmoid(ag_ref[...])
        halo = hv_ref[...] * _sigmoid(hg_ref[...])
        ext_h[0:HALO, :] = jnp.where(i > 0, halo, 0.0)
        ext_h[HALO:HALO + tm, :] = av_ref[...] * sig_g
        for r in range(0, tm, CONV_ROWS):
            dcr = ext_dc[r:r + CONV_ROWS, :]
            acc = jnp.zeros((CONV_ROWS, ch), F32)
            for k in range(CONV_TAPS):
                lo = r + k + HALO - (CONV_TAPS - 1)
                prod = dcr * ext_h[lo:lo + CONV_ROWS, :]
                acc_cw[k] += jnp.sum(prod.reshape(CONV_ROWS // 8, 8, ch), axis=0)
                hi = r + (CONV_TAPS - 1) - k
                acc = acc + cw_ref[k:k + 1, :] * ext_dc[hi:hi + CONV_ROWS, :]
            sg_r = sig_g[r:r + CONV_ROWS, :]
            av_r = av_ref[r:r + CONV_ROWS, :]
            dp_ref[r:r + CONV_ROWS, 0:ch] = (acc * sg_r).astype(BF16)
            dp_ref[r:r + CONV_ROWS, ch:2 * ch] = (acc * av_r * sg_r * (1.0 - sg_r)).astype(BF16)

        @pl.when(i == nt - 1)
        def _():
            dcw_ref[...] = jnp.sum(acc_cw[...], axis=1)

        tril = (lax.broadcasted_iota(jnp.int32, (CHUNK, CHUNK), 0)
                >= lax.broadcasted_iota(jnp.int32, (CHUNK, CHUNK), 1)).astype(F32)
        for h in range(HEADS):
            sl = slice(h * HEAD_DIM, (h + 1) * HEAD_DIM)
            u, du_dx = _gelu_and_grad(bu_ref[:, sl])
            v, dv_dx = _gelu_and_grad(bv_ref[:, sl])
            xhv, rstdv = _ln_stats(v)
            gh = sg_ref[h:h + 1, :]
            vn3 = (xhv * gh + sb_ref[h:h + 1, :]).astype(BF16).reshape(nc, CHUNK, HEAD_DIM)
            wb = jnp.broadcast_to(w_ref[h][None], (nc, CHUNK, CHUNK))
            mixed = jnp.einsum("cts,csd->ctd", wb, vn3, preferred_element_type=F32) + bb_ref[h][None]
            dyb = dyb_ref[:, sl]
            d_u = dyb * mixed.reshape(tm, HEAD_DIM)
            dm = dyb * u
            dm3 = dm.reshape(nc, CHUNK, HEAD_DIM)
            dbs_ref[h:h + 1, :] += jnp.sum(jnp.sum(dm3, axis=0).T, axis=0, keepdims=True)
            dm3b = dm3.astype(BF16)
            dw_h = jnp.sum(jnp.einsum("ctd,csd->cts", dm3b, vn3, preferred_element_type=F32), axis=0)
            dw_ref[h] += dw_h * tril
            wtb = jnp.broadcast_to(wt_ref[h][None], (nc, CHUNK, CHUNK))
            d_vn = jnp.einsum("cst,ctd->csd", wtb, dm3b, preferred_element_type=F32).reshape(tm, HEAD_DIM)
            dsg_ref[h:h + 1, :] += jnp.sum(d_vn * xhv, axis=0, keepdims=True)
            dsb_ref[h:h + 1, :] += jnp.sum(d_vn, axis=0, keepdims=True)
            dv = _ln_bwd(d_vn * gh, xhv, rstdv)
            dp_ref[:, 2 * ch + h * HEAD_DIM:2 * ch + (h + 1) * HEAD_DIM] = (d_u * du_dx).astype(BF16)
            dp_ref[:, 3 * ch + h * HEAD_DIM:3 * ch + (h + 1) * HEAD_DIM] = (dv * dv_dx).astype(BF16)

    col = lambda cidx: (lambda i: (i, cidx))
    prev = lambda cidx: (lambda i: (jnp.maximum(i * hb - 1, 0), cidx))
    nxt = lambda i: (jnp.minimum((i + 1) * hb, last_halo), 0)
    fix2 = lambda i: (0, 0)
    fix3 = lambda i: (0, 0, 0)
    out_shape = [jax.ShapeDtypeStruct((t, 4 * ch), BF16), jax.ShapeDtypeStruct((CONV_TAPS, ch), F32),
                 jax.ShapeDtypeStruct((1, ch), F32), jax.ShapeDtypeStruct((1, ch), F32), jax.ShapeDtypeStruct((1, ch), F32),
                 jax.ShapeDtypeStruct((HEADS, HEAD_DIM), F32), jax.ShapeDtypeStruct((HEADS, HEAD_DIM), F32),
                 jax.ShapeDtypeStruct((HEADS, CHUNK, CHUNK), F32), jax.ShapeDtypeStruct((HEADS, CHUNK), F32)]
    out_specs = [pl.BlockSpec((tm, 4 * ch), lambda i: (i, 0)), pl.BlockSpec((CONV_TAPS, ch), fix2),
                 pl.BlockSpec((1, ch), fix2), pl.BlockSpec((1, ch), fix2), pl.BlockSpec((1, ch), fix2),
                 pl.BlockSpec((HEADS, HEAD_DIM), fix2), pl.BlockSpec((HEADS, HEAD_DIM), fix2),
                 pl.BlockSpec((HEADS, CHUNK, CHUNK), fix3), pl.BlockSpec((HEADS, CHUNK), fix2)]
    in_specs = [pl.BlockSpec((tm, ch), col(0)), pl.BlockSpec((tm, ch), col(1)), pl.BlockSpec((tm, ch), col(2)),
                pl.BlockSpec((tm, ch), col(3)), pl.BlockSpec((HALO, ch), prev(0)), pl.BlockSpec((HALO, ch), prev(1)),
                pl.BlockSpec((tm, ch), col(0)), pl.BlockSpec((HALO, ch), nxt),
                pl.BlockSpec((tm, ch), col(0)), pl.BlockSpec((HALO, ch), nxt), pl.BlockSpec((tm, ch), col(1)),
                pl.BlockSpec((CONV_TAPS, ch), fix2), pl.BlockSpec((1, ch), fix2), pl.BlockSpec((1, ch), fix2),
                pl.BlockSpec((HEADS, HEAD_DIM), fix2), pl.BlockSpec((HEADS, HEAD_DIM), fix2),
                pl.BlockSpec((HEADS, CHUNK, CHUNK), fix3), pl.BlockSpec((HEADS, CHUNK, CHUNK), fix3),
                pl.BlockSpec((HEADS, CHUNK, HEAD_DIM), fix3)]
    return _call(
        body, exch, name=name, grid=(nt,), out_shape=out_shape, in_specs=in_specs, out_specs=out_specs,
        scratch_shapes=[pltpu.VMEM((HALO + tm, ch), F32), pltpu.VMEM((tm + HALO, ch), F32),
                        pltpu.VMEM((CONV_TAPS, 8, ch), F32)],
        semantics=("arbitrary",),
    )(proj, proj, proj, proj, proj, proj, conv_c, conv_c, dy, dy, dy,
      conv_w, cln_g, cln_b, sln_g, sln_b, sg_wm, sg_wmt, sg_bb)


def _pair_sum(parts, from_sibling, c, name):
    _, r, cc = parts.shape
    tr = _tile(r, max(16, (1 << 20) // (2 * cc)), 16)

    def body(c_ref, p_ref, s_ref, o_ref):
        o_ref[...] = (p_ref[...].astype(F32) + s_ref[...].astype(F32)).astype(BF16)

    grid_spec = pltpu.PrefetchScalarGridSpec(
        num_scalar_prefetch=1, grid=(4, r // tr),
        in_specs=[pl.BlockSpec((1, tr, cc), lambda j, i, c_ref: (2 * j + c_ref[0], i, 0)),
                  pl.BlockSpec((1, tr, cc), lambda j, i, c_ref: (j, i, 0))],
        out_specs=pl.BlockSpec((1, tr, cc), lambda j, i, c_ref: (j, i, 0)))
    return pl.pallas_call(
        body, name=name, grid_spec=grid_spec, out_shape=jax.ShapeDtypeStruct((4, r, cc), BF16),
        compiler_params=_cparams("parallel", "parallel"),
    )(c, parts, from_sibling)


def _adamw_math(w, g, m, v):
    m = ADAM_B1 * m + (1.0 - ADAM_B1) * g
    v = ADAM_B2 * v + (1.0 - ADAM_B2) * (g * g)
    m_hat = m / (1.0 - ADAM_B1 ** ADAM_STEP)
    v_hat = v / (1.0 - ADAM_B2 ** ADAM_STEP)
    delta = -ADAM_LR * (m_hat / (jnp.sqrt(v_hat) + ADAM_EPS) + ADAM_WD * w)
    return delta, m, v


def _adamw_sharded(w, m, v, chip_parts, from_chips, chip, name):
    r, cc = w.shape
    tr = _tile(r, max(16, (1 << 19) // (4 * cc) * 2), 16)

    def body(j_ref, w_ref, m_ref, v_ref, q_ref, o_ref, g_out, d_out, m_out, v_out):
        g = q_ref[0].astype(F32)
        for k in range(3):
            g = g + o_ref[k].astype(F32)
        d, mm, vv = _adamw_math(w_ref[...], g, m_ref[...], v_ref[...])
        g_out[...] = g
        d_out[...] = d
        m_out[...] = mm
        v_out[...] = vv

    row = lambda i, j_ref: (i, 0)
    grid_spec = pltpu.PrefetchScalarGridSpec(
        num_scalar_prefetch=1, grid=(r // tr,),
        in_specs=[pl.BlockSpec((tr, cc), row), pl.BlockSpec((tr, cc), row), pl.BlockSpec((tr, cc), row),
                  pl.BlockSpec((1, tr, cc), lambda i, j_ref: (j_ref[0], i, 0)),
                  pl.BlockSpec((3, tr, cc), lambda i, j_ref: (0, i, 0))],
        out_specs=[pl.BlockSpec((tr, cc), row)] * 4)
    return pl.pallas_call(
        body, name=name, grid_spec=grid_spec, out_shape=[jax.ShapeDtypeStruct((r, cc), F32)] * 4,
        compiler_params=_cparams("parallel"),
    )(chip, w, m, v, chip_parts, from_chips)


def _adamw_small(w, g, m, v, name):
    r, cc = w.shape

    def body(w_ref, g_ref, m_ref, v_ref, d_out, m_out, v_out):
        d, mm, vv = _adamw_math(w_ref[...], g_ref[...], m_ref[...], v_ref[...])
        d_out[...] = d
        m_out[...] = mm
        v_out[...] = vv

    full = pl.BlockSpec((r, cc), lambda i: (0, 0))
    return pl.pallas_call(
        body, name=name, grid=(1,), out_shape=[jax.ShapeDtypeStruct((r, cc), F32)] * 3,
        in_specs=[full] * 4, out_specs=[full] * 3, compiler_params=_cparams("arbitrary"),
    )(w, g, m, v)


SMALL = ("ln1_g", "ln1_b", "conv_b", "conv_ln_g", "conv_ln_b", "sg_ln_g", "sg_ln_b", "sg_w", "sg_b",
         "ln2_g", "ln2_b", "ln3_g", "ln3_b")
ORDER = ("ffn1_w_gate_up", "ffn1_w_down", "ln1_g", "ln1_b", "mix_w_in", "conv_w", "conv_b", "conv_ln_g", "conv_ln_b",
         "sg_ln_g", "sg_ln_b", "sg_w", "sg_b", "mix_w_out", "ln2_g", "ln2_b", "ffn2_w_gate_up", "ffn2_w_down",
         "ln3_g", "ln3_b")


def _rows128(a):
    return a.reshape(-1, 128)


def kernel(x, ffn1_w_gate_up, ffn1_w_down, ln1_g, ln1_b, mix_w_in, conv_w, conv_b, conv_ln_g, conv_ln_b, sg_ln_g, sg_ln_b, sg_w, sg_b, mix_w_out, ln2_g, ln2_b, ffn2_w_gate_up, ffn2_w_down, ln3_g, ln3_b, loss_target, m_ffn1_w_gate_up, m_ffn1_w_down, m_ln1_g, m_ln1_b, m_mix_w_in, m_conv_w, m_conv_b, m_conv_ln_g, m_conv_ln_b, m_sg_ln_g, m_sg_ln_b, m_sg_w, m_sg_b, m_mix_w_out, m_ln2_g, m_ln2_b, m_ffn2_w_gate_up, m_ffn2_w_down, m_ln3_g, m_ln3_b, v_ffn1_w_gate_up, v_ffn1_w_down, v_ln1_g, v_ln1_b, v_mix_w_in, v_conv_w, v_conv_b, v_conv_ln_g, v_conv_ln_b, v_sg_ln_g, v_sg_ln_b, v_sg_w, v_sg_b, v_mix_w_out, v_ln2_g, v_ln2_b, v_ffn2_w_gate_up, v_ffn2_w_down, v_ln3_g, v_ln3_b):
    args = dict(locals())
    w = {n: args[n][0] for n in ORDER}
    mom = {n: args["m_" + n][0] for n in ORDER}
    var = {n: args["v_" + n][0] for n in ORDER}
    x0 = x[0]
    target = loss_target[0]
    t, d = x0.shape
    my_x, my_y, my_c = lax.axis_index("x"), lax.axis_index("y"), lax.axis_index("c")
    my_chip = (2 * my_x + my_y).astype(jnp.int32).reshape(1)
    my_core = my_c.astype(jnp.int32).reshape(1)
    me = 4 * my_x + 2 * my_y + my_c

    big = ("ffn1_w_gate_up", "ffn1_w_down", "mix_w_in", "mix_w_out", "ffn2_w_gate_up", "ffn2_w_down")
    sh = {n: w[n].astype(BF16) for n in big}
    f2s = sh["ffn2_w_gate_up"].shape[1]
    (x0t, x0b), _ = _transpose_bf16(x0, "x0_transpose", with_copy=True)
    order = jnp.stack([4 * p[0] + 2 * p[1] + p[2] for p in _visit_order(my_x, my_y, my_c)]).astype(jnp.int32)
    gu1, (wgu1, wd1, conv_w_all) = _gather_and_gate_up(
        x0b, [sh["ffn1_w_gate_up"], sh["ffn1_w_down"], w["conv_w"]], [True, True, False], order, "ffn1_gate_up_fwd")
    wd1 = wd1.reshape(-1, d)
    conv_w_full = jnp.transpose(conv_w_all, (1, 0, 2)).reshape(CONV_TAPS, CONV_CH)
    tril = jnp.tril(jnp.ones((CHUNK, CHUNK), F32))
    sg_wm = w["sg_w"] * tril
    sg_wm_b = sg_wm.astype(BF16)
    sg_wmt_b = jnp.swapaxes(sg_wm, 1, 2).astype(BF16)
    sg_bb = jnp.broadcast_to(w["sg_b"][:, :, None], (HEADS, CHUNK, HEAD_DIM))
    row = lambda a: a.reshape(1, -1)

    (h1t, z1, x1), ((g_in, g_out),) = _ffn_down_fwd(
        gu1, x0, wd1, row(w["ln1_g"]), row(w["ln1_b"]), "ffn1_down_fwd",
        exch=[_gather_first([sh["mix_w_in"], sh["mix_w_out"]], [False, False])])
    x1t, ((w_in, w_out),) = _transpose_bf16(
        x1, "x1_transpose", exch=[_gather_forward([g_in, g_out], [False, False], [None, None])])
    w_out = w_out.reshape(-1, d)
    top, bottom = (0, d // 2), (d // 2, d // 2)
    gu2 = [sh["ffn2_w_gate_up"]]
    proj, ((g_gu2,),) = _mix_in_proj(x1, w_in, "mix_in_fwd", exch=[_gather_first(gu2, [True], rows=top)])
    (y, yt, conv_c), ((g_gu2,),) = _mixer_fwd(
        proj, conv_w_full, row(w["conv_b"]), row(w["conv_ln_g"]), row(w["conv_ln_b"]),
        w["sg_ln_g"], w["sg_ln_b"], sg_wm_b, sg_bb, "mixer_fwd",
        exch=[_both(_gather_first(gu2, [True], rows=bottom, into=[g_gu2]),
                    _gather_forward([g_gu2], [True], [f2s], rows=top))])
    (z2, x2, x2t), ((wgu2,), (g_d2,)) = _mix_out_fwd(
        y, w_out, x1, row(w["ln2_g"]), row(w["ln2_b"]), "mix_out_fwd",
        exch=[_gather_forward([g_gu2], [True], [f2s], rows=bottom), _gather_first([sh["ffn2_w_down"]], [False])])
    (wd2,) = _exchange_alone(_gather_forward([g_d2], [False], [None]), "ffn2_down_gather_forward")
    wd2 = wd2.reshape(-1, d)
    (g2, u2, h2t, z3), _ = _ffn_fwd(x2, wgu2, wd2, row(w["ln3_g"]), row(w["ln3_b"]), "ffn2_fwd", with_ln=False)

    f = wd1.shape[0]
    dn = _tile(d, 1024, 128)
    grads = {}
    pair = lambda p, s, label: _pair_sum(p, s, my_core, "pair_sum_" + label)
    dz3, do2, grads["ln3_g"], grads["ln3_b"], loss_tile = _loss_ln_bwd(
        z3, target, row(w["ln3_g"]), row(w["ln3_b"]), 0.5, "loss_ln3_bwd")
    p_d2, _ = _weight_grad(h2t, do2, dn, 512, "ffn2_dw_down")
    p_d2 = p_d2.reshape(N_DEV, f // N_DEV, d)
    (dg2, du2, dx2), ((s_d2,),) = _ffn_bwd(dz3, do2, g2, u2, wgu2, wd2, "ffn2_bwd", exch=[_rs_sibling([p_d2])])
    q_d2 = pair(p_d2, s_d2, "ffn2_down")
    p_gu2, ((r_d2,),) = _weight_grad(x2t, dg2, f2s, 512, "ffn2_dw_gate", blocks=N_DEV, exch=[_rs_chips([q_d2])])
    p_gu2, _ = _weight_grad(x2t, du2, f2s, 512, "ffn2_dw_up", blocks=N_DEV, block_offset=4, into=p_gu2)
    (dz2, dz2b, grads["ln2_g"], grads["ln2_b"]), ((s_gu2,),) = _ln_bwd_call(
        z2, dx2, row(w["ln2_g"]), 1.0, "ln2_bwd", exch=[_rs_sibling([p_gu2])])
    q_gu2 = pair(p_gu2, s_gu2, "ffn2_gate_up")
    dy = _mix_out_bwd(dz2b, w_out, "mix_out_bwd")
    p_out, _ = _weight_grad(yt, dz2b, dn, 512, "mix_out_dw")
    p_out = p_out.reshape(N_DEV, -1, d)
    (dproj, grads["conv_w"], grads["conv_b"], grads["conv_ln_g"], grads["conv_ln_b"], grads["sg_ln_g"],
     grads["sg_ln_b"], grads["sg_w"], grads["sg_b"]), ((r_gu2,),) = _mixer_bwd(
        proj, conv_c, dy, conv_w_full, row(w["conv_ln_g"]), row(w["conv_ln_b"]), w["sg_ln_g"], w["sg_ln_b"],
        sg_wm_b, sg_wmt_b, sg_bb, "mixer_bwd", exch=[_rs_chips([q_gu2])])
    dx1, ((s_out,),) = _mix_in_bwd(dproj, w_in, dz2, "mix_in_bwd", exch=[_rs_sibling([p_out])])
    p_in, _ = _weight_grad(x1t, dproj, w_in.shape[2], 512, "mix_in_dw", blocks=N_DEV)
    (dz1, do1, grads["ln1_g"], grads["ln1_b"]), ((s_in,),) = _ln_bwd_call(
        z1, dx1, row(w["ln1_g"]), 0.5, "ln1_bwd", exch=[_rs_sibling([p_in])])
    q_out = pair(p_out, s_out, "mix_out")
    q_in = pair(p_in, s_in, "mix_in")
    small_parts = [_rows128(grads[n]) for n in SMALL]
    packed = jnp.concatenate(small_parts + [_rows128(grads["conv_w"]), loss_tile], axis=0)
    p_d1, ((r_in,),) = _weight_grad(h1t, do1, dn, 512, "ffn1_dw_down", exch=[_rs_chips([q_in])])
    p_d1 = p_d1.reshape(N_DEV, f // N_DEV, d)
    (dg1, du1), ((s_d1,), (r_out,), (small_all,)) = _ffn_bwd_act(
        do1, gu1, wd1, "ffn1_bwd_act",
        exch=[_rs_sibling([p_d1]), _rs_chips([q_out]), _small_gather(packed)])
    q_d1 = pair(p_d1, s_d1, "ffn1_down")
    p_gu1, ((r_d1,),) = _weight_grad(x0t, dg1, f2s, 512, "ffn1_dw_gate", blocks=N_DEV, exch=[_rs_chips([q_d1])])
    p_gu1, _ = _weight_grad(x0t, du1, f2s, 512, "ffn1_dw_up", blocks=N_DEV, block_offset=4, into=p_gu1)
    (s_gu1,) = _exchange_alone(_rs_sibling([p_gu1]), "ffn1_gate_up_sibling_exchange")
    q_gu1 = pair(p_gu1, s_gu1, "ffn1_gate_up")
    (grad_x,), ((r_gu1,),) = _ffn_bwd_dx(dz1, dg1, du1, wgu1, "ffn1_bwd_dx", exch=[_rs_chips([q_gu1])])

    chip_parts = [q_gu1, q_d1, q_in, q_out, q_gu2, q_d2]
    from_chips = [r_gu1, r_d1, r_in, r_out, r_gu2, r_d2]
    out = {}
    for k, n in enumerate(big):
        out[n] = _adamw_sharded(w[n], mom[n], var[n], chip_parts[k], from_chips[k], my_chip, "adamw_" + n)

    cw_rows = CONV_TAPS * CONV_CH // 128
    total = _sum_over_devices(small_all)
    offs = [0]
    for p in small_parts:
        offs.append(offs[-1] + p.shape[0])
    n_small = offs[-1]
    loss = total[n_small + cw_rows, 0]
    g_conv_w = lax.dynamic_slice_in_dim(total[n_small:n_small + cw_rows].reshape(CONV_TAPS, CONV_CH),
                                        me * (CONV_CH // N_DEV), CONV_CH // N_DEV, axis=1)
    pad8 = lambda a: jnp.pad(a, ((0, -a.shape[0] % 8), (0, 0)))
    pack = lambda tree, cw: jnp.concatenate([_rows128(tree[n]) for n in SMALL] + [pad8(cw)], axis=0)
    g_pack = jnp.concatenate([total[:n_small], pad8(g_conv_w)], axis=0)
    d_pack, m_pack, v_pack = _adamw_small(pack(w, w["conv_w"]), g_pack, pack(mom, mom["conv_w"]),
                                          pack(var, var["conv_w"]), "adamw_small")
    for k, n in enumerate(SMALL):
        sl = slice(offs[k], offs[k + 1])
        shp = w[n].shape
        out[n] = (total[sl].reshape(shp), d_pack[sl].reshape(shp), m_pack[sl].reshape(shp), v_pack[sl].reshape(shp))
    sl = slice(n_small, n_small + CONV_TAPS)
    out["conv_w"] = (g_conv_w, d_pack[sl], m_pack[sl], v_pack[sl])

    lead = lambda a: a[None]
    res = [loss, grad_x[None]]
    for kind in range(4):
        res += [lead(out[n][kind]) for n in ORDER]
    return tuple(res)
```

```python
import functools
import math

import jax
import jax.numpy as jnp
from jax import lax
from jax.experimental import pallas as pl
from jax.experimental.pallas import tpu as pltpu

F32, BF16 = jnp.float32, jnp.bfloat16
MESH = pl.DeviceIdType.MESH
ANY = pl.BlockSpec(memory_space=pl.ANY)

N_DEV = 8
LN_EPS = 1e-5
ALPHA = 2.0 ** 0.25
CONV_CH = 1024
CONV_TAPS = 31
HALO = 32
HEADS = 8
HEAD_DIM = 128
CHUNK = 128
ADAM_LR, ADAM_B1, ADAM_B2, ADAM_EPS, ADAM_WD, ADAM_STEP = 0.001, 0.9, 0.999, 1e-08, 0.01, 10
V7X_VMEM_LIMIT = 56 * 2 ** 20


def _cparams(*sem):
    return pltpu.CompilerParams(dimension_semantics=sem, vmem_limit_bytes=V7X_VMEM_LIMIT)


def _tile(n, pref, mult):
    best = None
    for t in range(mult, min(n, pref) + 1, mult):
        if n % t == 0:
            best = t
    return best if best is not None else n


def _dot(a, b):
    return jnp.dot(a, b, preferred_element_type=F32)


def _dot_nt(a, b):
    return lax.dot_general(a, b, (((1,), (1,)), ((), ())), preferred_element_type=F32)


def _sigmoid(x):
    return 1.0 / (1.0 + jnp.exp(-x))


def _ln_stats(z):
    mu = jnp.mean(z, axis=-1, keepdims=True)
    zc = z - mu
    var = jnp.mean(zc * zc, axis=-1, keepdims=True)
    rstd = lax.rsqrt(var + LN_EPS)
    return zc * rstd, rstd


def _ln(z, g, b):
    xh, _ = _ln_stats(z)
    return xh * g + b


def _ln_bwd(dxh, xh, rstd):
    m1 = jnp.mean(dxh, axis=-1, keepdims=True)
    m2 = jnp.mean(dxh * xh, axis=-1, keepdims=True)
    return rstd * (dxh - m1 - xh * m2)


_GK = math.sqrt(2.0 / math.pi)
_GA = 0.044715


def _gelu_and_grad(x):
    x2 = x * x
    t = jnp.tanh(_GK * (x + _GA * x * x2))
    y = 0.5 * x * (1.0 + t)
    dy = 0.5 * (1.0 + t) + 0.5 * x * (1.0 - t * t) * (_GK * (1.0 + 3.0 * _GA * x2))
    return y, dy


def _silu_grad(a):
    s = _sigmoid(a)
    return s * (1.0 + a * (1.0 - s))


def _place():
    return lax.axis_index("x"), lax.axis_index("y"), lax.axis_index("c")


def _other_chips(x, y):
    return [(1 - x, y), (x, 1 - y), (1 - x, 1 - y)]


def _visit_order(x, y, c):
    chips = _other_chips(x, y)
    return [(x, y, c), (x, y, 1 - c), (*chips[0], c), (*chips[1], c), (*chips[0], 1 - c), (*chips[1], 1 - c),
            (*chips[2], c), (*chips[2], 1 - c)]


def _gather_and_gate_up(xb, shards, relayed, order, name):
    n = len(shards)
    N_COPIES = 10
    t, d = xb.shape
    cols = shards[0].shape[1]
    tm = _tile(t, 512, 128)
    ni = t // tm
    col_major = [True] + [False] * (n - 1)

    def body(order_ref, x_ref, *refs):
        srcs, gu_ref, dsts = refs[:n], refs[n], refs[n + 1:2 * n + 1]
        wbuf, send_sems, recv_sems, local_sems, load_sem = refs[2 * n + 1:]
        b, i = pl.program_id(0), pl.program_id(1)
        x, y, c = _place()
        me, sib = (x, y, c), (x, y, 1 - c)
        chips = _other_chips(x, y)

        near_x, near_y, far = chips

        def slot(w, p, band=None):
            half = shards[w].shape[0] // 2
            rows = None if band is None else (band * half, half)
            return _block_slot(dsts[w], col_major[w], shards[w].shape[1], p, rows)

        def copy(w, s, block, to, band=None, from_src=False):
            return pltpu.make_async_remote_copy(
                src_ref=srcs[w] if from_src else slot(w, block, band), dst_ref=slot(w, block, band),
                send_sem=send_sems.at[N_COPIES * w + s], recv_sem=recv_sems.at[N_COPIES * w + s],
                device_id=to, device_id_type=MESH)

        def own(w):
            return pltpu.make_async_copy(srcs[w], slot(w, me), local_sems.at[w])

        def sends(w):
            out = [copy(w, 0, me, sib, from_src=True), copy(w, 1, me, (*near_x, c), from_src=True),
                   copy(w, 2, me, (*near_y, c), from_src=True)]
            if not relayed[w]:
                out.append(copy(w, 3, me, (*far, c), from_src=True))
            return out

        def passed_on(w):
            out = [copy(w, 4, (*near_x, c), sib), copy(w, 5, (*near_y, c), sib)]
            if relayed[w]:
                out += [copy(w, 6, (*far, c), sib, band=0), copy(w, 9, (*far, c), sib, band=1),
                        copy(w, 7, (*near_x, c), (*near_y, c), band=0), copy(w, 8, (*near_y, c), (*near_x, c), band=1)]
            else:
                out.append(copy(w, 6, (*far, c), sib))
            return out

        def start_sends(w):
            own(w).start()
            for cp in sends(w):
                cp.start()

        def got_near_x(w):
            copy(w, 1, (*near_x, c), me).wait_recv()
            copy(w, 4, (*near_x, c), sib).start()
            if relayed[w]:
                copy(w, 7, (*near_x, c), (*near_y, c), band=0).start()

        def got_near_y(w):
            copy(w, 2, (*near_y, c), me).wait_recv()
            copy(w, 5, (*near_y, c), sib).start()
            if relayed[w]:
                copy(w, 8, (*near_y, c), (*near_x, c), band=1).start()

        def got_far(w):
            if relayed[w]:
                copy(w, 7, (*far, c), me, band=0).wait_recv()
                copy(w, 6, (*far, c), sib, band=0).start()
                copy(w, 8, (*far, c), me, band=1).wait_recv()
                copy(w, 9, (*far, c), sib, band=1).start()
            else:
                copy(w, 3, (*far, c), me).wait_recv()
                copy(w, 6, (*far, c), sib).start()

        def got_from_sibling(w, which):
            if which == 0:
                copy(w, 0, sib, me).wait_recv()
            elif which == 3 and relayed[w]:
                copy(w, 6, (*far, 1 - c), me, band=0).wait_recv()
                copy(w, 9, (*far, 1 - c), me, band=1).wait_recv()
            else:
                copy(w, 3 + which, (*chips[which - 1], 1 - c), me).wait_recv()

        others = range(1, n)

        def arrive(k):
            if k == 0:
                own(0).wait()
            elif k == 1:
                got_from_sibling(0, 0)
            elif k == 2:
                got_near_x(0)
                for w in others:
                    start_sends(w)
            elif k == 3:
                got_near_y(0)
            elif k in (4, 5):
                got_from_sibling(0, k - 3)
            elif k == 6:
                got_far(0)
                for w in others:
                    got_near_x(w)
                    got_near_y(w)
            else:
                got_from_sibling(0, 3)
                for w in others:
                    got_far(w)

        @pl.when((b == 0) & (i == 0))
        def _():
            start_sends(0)

        for k in range(N_DEV):
            @pl.when((b == k) & (i == 0))
            def _(k=k):
                arrive(k)
                at = pl.multiple_of(order_ref[k] * cols, 128)
                load = pltpu.make_async_copy(dsts[0].at[:, pl.ds(at, cols)], wbuf, load_sem.at[0])
                load.start()
                load.wait()

        gu_ref[...] = _dot(x_ref[...], wbuf[...]).astype(BF16)

        @pl.when((b == N_DEV - 1) & (i == ni - 1))
        def _():
            for w in others:
                for which in range(4):
                    got_from_sibling(w, which)
                own(w).wait()
            for w in range(n):
                for cp in sends(w) + passed_on(w):
                    cp.wait_send()

    grid_spec = pltpu.PrefetchScalarGridSpec(
        num_scalar_prefetch=1, grid=(N_DEV, ni),
        in_specs=[pl.BlockSpec((tm, d), lambda b, i, o: (i, 0))] + [ANY] * n,
        out_specs=[pl.BlockSpec((tm, cols), lambda b, i, o: (i, o[b]))] + [ANY] * n,
        scratch_shapes=[pltpu.VMEM((d, cols), BF16), pltpu.SemaphoreType.DMA((N_COPIES * n,)),
                        pltpu.SemaphoreType.DMA((N_COPIES * n,)), pltpu.SemaphoreType.DMA((n,)),
                        pltpu.SemaphoreType.DMA((1,))])
    res = pl.pallas_call(
        body, name=name, grid_spec=grid_spec,
        out_shape=[jax.ShapeDtypeStruct((t, N_DEV * cols), BF16)]
        + [_gathered_shape(s, cm) for s, cm in zip(shards, col_major)],
        compiler_params=_cparams("arbitrary", "arbitrary"),
    )(order, xb, *shards)
    return res[0], res[1:]


class _Exchange:
    def __init__(self, ins, io, new, n_sems, n_local, make):
        self.ins, self.io, self.new = list(ins), list(io), list(new)
        self.n_sems, self.n_local, self.make = n_sems, n_local, make


def _block_slot(ref, col_major, cols, place, rows=None):
    k = 4 * place[0] + 2 * place[1] + place[2]
    band = slice(None) if rows is None else pl.ds(rows[0], rows[1])
    if col_major:
        return ref.at[band, pl.ds(pl.multiple_of(k * cols, 128), cols)]
    return ref.at[k] if rows is None else ref.at[k, band]


def _gathered_shape(s, col_major):
    return jax.ShapeDtypeStruct((s.shape[0], N_DEV * s.shape[1]) if col_major else (N_DEV,) + s.shape, s.dtype)


def _gather_first(shards, col_major, rows=None, into=None):
    n = len(shards)
    new = [] if into is not None else [_gathered_shape(s, cm) for s, cm in zip(shards, col_major)]

    def make(in_refs, io_refs, new_refs, send_sems, recv_sems, local_sems, base=0, local_base=0):
        x, y, c = _place()
        targets = [(x, y, 1 - c)] + [(*chip, c) for chip in _other_chips(x, y)]
        gathered = io_refs if into is not None else new_refs
        copies = []
        for w in range(n):
            src = in_refs[w] if rows is None else in_refs[w].at[pl.ds(rows[0], rows[1])]
            slot = _block_slot(gathered[w], col_major[w], shards[w].shape[1], (x, y, c), rows)
            copies.append(pltpu.make_async_copy(src, slot, local_sems.at[local_base + w]))
            for s, to in enumerate(targets):
                copies.append(pltpu.make_async_remote_copy(
                    src_ref=src, dst_ref=slot, send_sem=send_sems.at[base + 4 * w + s],
                    recv_sem=recv_sems.at[base + 4 * w + s], device_id=to, device_id_type=MESH))
        return copies

    return _Exchange(shards, into or [], new, 4 * n, n, make)


def _gather_forward(gathered, col_major, cols, rows=None):
    n = len(gathered)

    def make(in_refs, io_refs, new_refs, send_sems, recv_sems, local_sems, base=0, local_base=0):
        x, y, c = _place()
        copies = []
        for w in range(n):
            for j, chip in enumerate(_other_chips(x, y)):
                slot = _block_slot(io_refs[w], col_major[w], cols[w], (*chip, c), rows)
                copies.append(pltpu.make_async_remote_copy(
                    src_ref=slot, dst_ref=slot, send_sem=send_sems.at[base + 3 * w + j],
                    recv_sem=recv_sems.at[base + 3 * w + j], device_id=(x, y, 1 - c), device_id_type=MESH))
        return copies

    return _Exchange([], gathered, [], 3 * n, 0, make)


def _both(a, b):
    def make(in_refs, io_refs, new_refs, send_sems, recv_sems, local_sems):
        na = len(a.ins)
        return (a.make(in_refs[:na], io_refs, [], send_sems, recv_sems, local_sems, 0, 0)
                + b.make(in_refs[na:], io_refs, [], send_sems, recv_sems, local_sems, a.n_sems, a.n_local))

    return _Exchange(a.ins + b.ins, a.io, [], a.n_sems + b.n_sems, a.n_local + b.n_local, make)


def _rs_sibling(parts):
    n = len(parts)

    def make(in_refs, io_refs, new_refs, send_sems, recv_sems, local_sems):
        x, y, c = _place()
        copies = []
        for w in range(n):
            for j in range(4):
                copies.append(pltpu.make_async_remote_copy(
                    src_ref=in_refs[w].at[2 * j + (1 - c)], dst_ref=new_refs[w].at[j],
                    send_sem=send_sems.at[4 * w + j], recv_sem=recv_sems.at[4 * w + j],
                    device_id=(x, y, 1 - c), device_id_type=MESH))
        return copies

    return _Exchange(parts, [], [jax.ShapeDtypeStruct((4,) + p.shape[1:], p.dtype) for p in parts], 4 * n, 0, make)


def _rs_chips(chip_parts, rows=None, into=None):
    n = len(chip_parts)
    band = slice(None) if rows is None else pl.ds(rows[0], rows[1])
    new = [] if into is not None else [jax.ShapeDtypeStruct((3,) + p.shape[1:], p.dtype) for p in chip_parts]

    def make(in_refs, io_refs, new_refs, send_sems, recv_sems, local_sems):
        x, y, c = _place()
        landing = io_refs if into is not None else new_refs
        copies = []
        for w in range(n):
            for rel, (px, py) in enumerate(_other_chips(x, y)):
                copies.append(pltpu.make_async_remote_copy(
                    src_ref=in_refs[w].at[2 * px + py, band], dst_ref=landing[w].at[rel, band],
                    send_sem=send_sems.at[3 * w + rel], recv_sem=recv_sems.at[3 * w + rel],
                    device_id=(px, py, c), device_id_type=MESH))
        return copies

    return _Exchange(chip_parts, into or [], new, 3 * n, 0, make)


def _call(body, exch, *, name, grid, in_specs, out_specs, out_shape, scratch_shapes=(), semantics,
          input_output_aliases=None):
    exch = list(exch)
    in_specs, out_specs, out_shape = list(in_specs), list(out_specs), list(out_shape)
    scratch_shapes = list(scratch_shapes)
    if not exch:
        fn = pl.pallas_call(body, name=name, grid=grid, in_specs=in_specs, out_specs=out_specs, out_shape=out_shape,
                            scratch_shapes=scratch_shapes, input_output_aliases=input_output_aliases or {},
                            compiler_params=_cparams(*semantics))
        return lambda *args: (fn(*args), [])
    n_in, n_out, n_scr = len(in_specs), len(out_specs), len(scratch_shapes)
    aliases = dict(input_output_aliases or {})
    all_in, all_out_specs, all_out_shape, all_scr = list(in_specs), list(out_specs), list(out_shape), list(scratch_shapes)
    extra_args = []
    for ex in exch:
        for k, a in enumerate(ex.io):
            aliases[len(all_in) + len(ex.ins) + k] = len(all_out_specs) + k
        all_in += [ANY] * (len(ex.ins) + len(ex.io))
        extra_args += ex.ins + ex.io
        all_out_specs += [ANY] * (len(ex.io) + len(ex.new))
        all_out_shape += [jax.ShapeDtypeStruct(a.shape, a.dtype) for a in ex.io] + ex.new
        all_scr += [pltpu.SemaphoreType.DMA((ex.n_sems,)), pltpu.SemaphoreType.DMA((ex.n_sems,)),
                    pltpu.SemaphoreType.DMA((max(ex.n_local, 1),))]

    def wrapped(*refs):
        pos = n_in
        ex_in = []
        for ex in exch:
            k = len(ex.ins) + len(ex.io)
            ex_in.append(refs[pos:pos + k])
            pos += k
        outs = refs[pos:pos + n_out]
        pos += n_out
        ex_out = []
        for ex in exch:
            k = len(ex.io) + len(ex.new)
            ex_out.append(refs[pos:pos + k])
            pos += k
        scr = refs[pos:pos + n_scr]
        pos += n_scr
        sems = [refs[pos + 3 * k:pos + 3 * k + 3] for k in range(len(exch))]
        first = functools.reduce(jnp.logical_and, [pl.program_id(a) == 0 for a in range(len(grid))])
        last = functools.reduce(jnp.logical_and, [pl.program_id(a) == g - 1 for a, g in enumerate(grid)])

        def copies():
            out = []
            for ex, ei, eo, es in zip(exch, ex_in, ex_out, sems):
                out += ex.make(ei[:len(ex.ins)], eo[:len(ex.io)], eo[len(ex.io):], *es)
            return out

        @pl.when(first)
        def _():
            for cp in copies():
                cp.start()

        body(*refs[:n_in], *outs, *scr)

        @pl.when(last)
        def _():
            for cp in copies():
                cp.wait()

    fn = pl.pallas_call(wrapped, name=name, grid=grid, in_specs=all_in, out_specs=all_out_specs,
                        out_shape=all_out_shape, scratch_shapes=all_scr, input_output_aliases=aliases,
                        compiler_params=_cparams(*(["arbitrary"] * len(grid))))

    def run(*args):
        res = fn(*args, *extra_args)
        outs, pos, ex_res = res[:n_out], n_out, []
        for ex in exch:
            k = len(ex.io) + len(ex.new)
            ex_res.append(list(res[pos:pos + k]))
            pos += k
        return outs, ex_res

    return run


def _exchange_alone(ex, name):
    def body():
        pass

    _, res = _call(body, [ex], name=name, grid=(1,), in_specs=[], out_specs=[], out_shape=[], semantics=("arbitrary",))()
    return res[0]


def _small_gather(part):
    def make(in_refs, io_refs, new_refs, send_sems, recv_sems, local_sems):
        x, y, c = _place()
        slot = new_refs[0].at[4 * x + 2 * y + c]
        copies = [pltpu.make_async_copy(in_refs[0], slot, local_sems.at[0])]
        for d in range(1, N_DEV):
            peer = (1 - x if d & 4 else x, 1 - y if d & 2 else y, 1 - c if d & 1 else c)
            copies.append(pltpu.make_async_remote_copy(
                src_ref=in_refs[0], dst_ref=slot, send_sem=send_sems.at[d - 1], recv_sem=recv_sems.at[d - 1],
                device_id=peer, device_id_type=MESH))
        return copies

    return _Exchange([part], [], [jax.ShapeDtypeStruct((N_DEV,) + part.shape, part.dtype)], N_DEV - 1, 1, make)


def _sum_over_devices(parts):
    _, rows, lanes = parts.shape

    def body(p_ref, o_ref):
        acc = p_ref[0]
        for k in range(1, N_DEV):
            acc = acc + p_ref[k]
        o_ref[...] = acc

    return pl.pallas_call(
        body, name="small_grads_sum", grid=(1,), out_shape=jax.ShapeDtypeStruct((rows, lanes), F32),
        in_specs=[pl.BlockSpec((N_DEV, rows, lanes), lambda i: (0, 0, 0))],
        out_specs=pl.BlockSpec((rows, lanes), lambda i: (0, 0)),
        compiler_params=_cparams("arbitrary"),
    )(parts)


def _transpose_bf16(a, name, exch=(), with_copy=False):
    r, c = a.shape
    tr, tc = _tile(r, 512, 128), _tile(c, 512, 128)

    def body(a_ref, o_ref, *copy_ref):
        v = a_ref[...].astype(F32)
        o_ref[...] = v.T.astype(BF16)
        if with_copy:
            copy_ref[0][...] = v.astype(BF16)

    outs, ex = _call(
        body, exch, name=name, grid=(r // tr, c // tc),
        out_shape=[jax.ShapeDtypeStruct((c, r), BF16)] + [jax.ShapeDtypeStruct((r, c), BF16)] * with_copy,
        in_specs=[pl.BlockSpec((tr, tc), lambda i, j: (i, j))],
        out_specs=[pl.BlockSpec((tc, tr), lambda i, j: (j, i))] + [pl.BlockSpec((tr, tc), lambda i, j: (i, j))] * with_copy,
        semantics=("parallel", "parallel"),
    )(a)
    return (outs if with_copy else outs[0]), ex


def _ffn_fwd(x, wgu, wd, ln_g, ln_b, name, exch=(), with_ln=True):
    t, d = x.shape
    f = wd.shape[0]
    tm, tf = _tile(t, 512, 128), _tile(f, 512, 128)
    nf = f // tf

    def body(x_ref, wg_ref, wu_ref, wd_ref, g_ref, b_ref, go_ref, uo_ref, ht_ref, z_ref, *rest):
        xn_ref = rest[0] if with_ln else None
        xb, acc = rest[-2:]
        j = pl.program_id(1)

        @pl.when(j == 0)
        def _():
            xb[...] = x_ref[...].astype(BF16)
            acc[...] = jnp.zeros_like(acc)

        g = _dot(xb[...], wg_ref[...])
        u = _dot(xb[...], wu_ref[...])
        h = g * _sigmoid(g) * u
        go_ref[...] = g.astype(BF16)
        uo_ref[...] = u.astype(BF16)
        ht_ref[...] = h.T.astype(BF16)
        acc[...] += _dot(h.astype(BF16), wd_ref[...])

        @pl.when(j == nf - 1)
        def _():
            z = ALPHA * x_ref[...] + 0.5 * acc[...]
            z_ref[...] = z
            if with_ln:
                xn_ref[...] = _ln(z, g_ref[...], b_ref[...])

    row = lambda i, j: (i, 0)
    n_td = 2 if with_ln else 1
    return _call(
        body, exch, name=name, grid=(t // tm, nf),
        out_shape=[jax.ShapeDtypeStruct((t, f), BF16), jax.ShapeDtypeStruct((t, f), BF16),
                   jax.ShapeDtypeStruct((f, t), BF16)] + [jax.ShapeDtypeStruct((t, d), F32)] * n_td,
        in_specs=[pl.BlockSpec((tm, d), row),
                  pl.BlockSpec((d, tf), lambda i, j: (0, j)),
                  pl.BlockSpec((d, tf), lambda i, j: (0, j + nf)),
                  pl.BlockSpec((tf, d), lambda i, j: (j, 0)),
                  pl.BlockSpec((1, d), lambda i, j: (0, 0)),
                  pl.BlockSpec((1, d), lambda i, j: (0, 0))],
        out_specs=[pl.BlockSpec((tm, tf), lambda i, j: (i, j)), pl.BlockSpec((tm, tf), lambda i, j: (i, j)),
                   pl.BlockSpec((tf, tm), lambda i, j: (j, i))] + [pl.BlockSpec((tm, d), row)] * n_td,
        scratch_shapes=[pltpu.VMEM((tm, d), BF16), pltpu.VMEM((tm, d), F32)],
        semantics=("parallel", "arbitrary"),
    )(x, wgu, wgu, wd, ln_g, ln_b)


def _ffn_down_fwd(gu, x, wd, ln_g, ln_b, name, exch=()):
    t, d = x.shape
    f = wd.shape[0]
    tm, tf = _tile(t, 512, 128), _tile(f, 512, 128)
    nf = f // tf

    def body(g_ref, u_ref, wd_ref, x_ref, lg_ref, lb_ref, ht_ref, z_ref, xn_ref, acc):
        j = pl.program_id(1)

        @pl.when(j == 0)
        def _():
            acc[...] = jnp.zeros_like(acc)

        g = g_ref[...].astype(F32)
        h = g * _sigmoid(g) * u_ref[...].astype(F32)
        ht_ref[...] = h.T.astype(BF16)
        acc[...] += _dot(h.astype(BF16), wd_ref[...])

        @pl.when(j == nf - 1)
        def _():
            z = ALPHA * x_ref[...] + 0.5 * acc[...]
            z_ref[...] = z
            xn_ref[...] = _ln(z, lg_ref[...], lb_ref[...])

    row = lambda i, j: (i, 0)
    fixed = lambda i, j: (0, 0)
    return _call(
        body, exch, name=name, grid=(t // tm, nf),
        out_shape=[jax.ShapeDtypeStruct((f, t), BF16), jax.ShapeDtypeStruct((t, d), F32),
                   jax.ShapeDtypeStruct((t, d), F32)],
        in_specs=[pl.BlockSpec((tm, tf), lambda i, j: (i, j)), pl.BlockSpec((tm, tf), lambda i, j: (i, j + nf)),
                  pl.BlockSpec((tf, d), lambda i, j: (j, 0)), pl.BlockSpec((tm, d), row),
                  pl.BlockSpec((1, d), fixed), pl.BlockSpec((1, d), fixed)],
        out_specs=[pl.BlockSpec((tf, tm), lambda i, j: (j, i)), pl.BlockSpec((tm, d), row), pl.BlockSpec((tm, d), row)],
        scratch_shapes=[pltpu.VMEM((tm, d), F32)],
        semantics=("parallel", "arbitrary"),
    )(gu, gu, wd, x, ln_g, ln_b)


def _ffn_act_grads(dh, g_ref, u_ref):
    gg = g_ref[...].astype(F32)
    uu = u_ref[...].astype(F32)
    s = _sigmoid(gg)
    du = (dh * (gg * s)).astype(BF16)
    dg = (dh * uu * (s * (1.0 + gg * (1.0 - s)))).astype(BF16)
    return dg, du


def _ffn_bwd(dz, do, g, u, wgu, wd, name, exch=()):
    t, d = dz.shape
    f = wd.shape[0]
    tm, tf = _tile(t, 512, 128), _tile(f, 512, 128)
    nf = f // tf

    def body(dz_ref, do_ref, g_ref, u_ref, wg_ref, wu_ref, wd_ref, dg_ref, du_ref, dx_ref, acc):
        j = pl.program_id(1)

        @pl.when(j == 0)
        def _():
            acc[...] = jnp.zeros_like(acc)

        dg, du = _ffn_act_grads(_dot_nt(do_ref[...], wd_ref[...]), g_ref, u_ref)
        dg_ref[...] = dg
        du_ref[...] = du
        acc[...] += _dot_nt(dg, wg_ref[...]) + _dot_nt(du, wu_ref[...])

        @pl.when(j == nf - 1)
        def _():
            dx_ref[...] = ALPHA * dz_ref[...] + acc[...]

    row = lambda i, j: (i, 0)
    tile = lambda i, j: (i, j)
    return _call(
        body, exch, name=name, grid=(t // tm, nf),
        out_shape=[jax.ShapeDtypeStruct((t, f), BF16), jax.ShapeDtypeStruct((t, f), BF16),
                   jax.ShapeDtypeStruct((t, d), F32)],
        in_specs=[pl.BlockSpec((tm, d), row), pl.BlockSpec((tm, d), row),
                  pl.BlockSpec((tm, tf), tile), pl.BlockSpec((tm, tf), tile),
                  pl.BlockSpec((d, tf), lambda i, j: (0, j)),
                  pl.BlockSpec((d, tf), lambda i, j: (0, j + nf)),
                  pl.BlockSpec((tf, d), lambda i, j: (j, 0))],
        out_specs=[pl.BlockSpec((tm, tf), tile), pl.BlockSpec((tm, tf), tile), pl.BlockSpec((tm, d), row)],
        scratch_shapes=[pltpu.VMEM((tm, d), F32)],
        semantics=("parallel", "arbitrary"),
    )(dz, do, g, u, wgu, wgu, wd)


def _ffn_bwd_act(do, gu, wd, name, exch=()):
    t, d = do.shape
    f = wd.shape[0]
    tm, tf = _tile(t, 512, 128), _tile(f, 512, 128)
    nf = f // tf

    def body(do_ref, g_ref, u_ref, wd_ref, dg_ref, du_ref):
        dg, du = _ffn_act_grads(_dot_nt(do_ref[...], wd_ref[...]), g_ref, u_ref)
        dg_ref[...] = dg
        du_ref[...] = du

    tile = lambda i, j: (i, j)
    return _call(
        body, exch, name=name, grid=(t // tm, f // tf),
        out_shape=[jax.ShapeDtypeStruct((t, f), BF16), jax.ShapeDtypeStruct((t, f), BF16)],
        in_specs=[pl.BlockSpec((tm, d), lambda i, j: (i, 0)), pl.BlockSpec((tm, tf), tile),
                  pl.BlockSpec((tm, tf), lambda i, j: (i, j + nf)), pl.BlockSpec((tf, d), lambda i, j: (j, 0))],
        out_specs=[pl.BlockSpec((tm, tf), tile), pl.BlockSpec((tm, tf), tile)],
        semantics=("parallel", "parallel"),
    )(do, gu, gu, wd)


def _ffn_bwd_dx(dz, dg, du, wgu, name, exch=()):
    t, d = dz.shape
    f = dg.shape[1]
    tm, tf = _tile(t, 512, 128), _tile(f, 512, 128)
    nf = f // tf

    def body(dz_ref, dg_ref, du_ref, wg_ref, wu_ref, dx_ref, acc):
        j = pl.program_id(1)

        @pl.when(j == 0)
        def _():
            acc[...] = jnp.zeros_like(acc)

        acc[...] += _dot_nt(dg_ref[...], wg_ref[...]) + _dot_nt(du_ref[...], wu_ref[...])

        @pl.when(j == nf - 1)
        def _():
            dx_ref[...] = ALPHA * dz_ref[...] + acc[...]

    row = lambda i, j: (i, 0)
    tile = lambda i, j: (i, j)
    return _call(
        body, exch, name=name, grid=(t // tm, nf), out_shape=[jax.ShapeDtypeStruct((t, d), F32)],
        in_specs=[pl.BlockSpec((tm, d), row), pl.BlockSpec((tm, tf), tile), pl.BlockSpec((tm, tf), tile),
                  pl.BlockSpec((d, tf), lambda i, j: (0, j)), pl.BlockSpec((d, tf), lambda i, j: (0, j + nf))],
        out_specs=[pl.BlockSpec((tm, d), row)],
        scratch_shapes=[pltpu.VMEM((tm, d), F32)],
        semantics=("parallel", "arbitrary"),
    )(dz, dg, du, wgu, wgu)


def _weight_grad(at, b, tn, tmm, name, blocks=None, block_offset=0, into=None, exch=()):
    m, t = at.shape
    nn = b.shape[1]
    tmm = _tile(m, tmm, 16)
    assert nn % tn == 0

    def body(*refs):
        at_ref, b_ref, o_ref = refs[0], refs[1], refs[-1]
        r = _dot(at_ref[...], b_ref[...]).astype(BF16)
        if blocks is None:
            o_ref[...] = r
        else:
            o_ref[0] = r

    in_specs = [pl.BlockSpec((tmm, t), lambda n, i: (i, 0)), pl.BlockSpec((t, tn), lambda n, i: (0, n))]
    args = [at, b]
    aliases = {}
    if into is not None:
        in_specs.append(ANY)
        args.append(into)
        aliases = {2: 0}
    if blocks is None:
        out_shape = jax.ShapeDtypeStruct((m, nn), BF16)
        out_spec = pl.BlockSpec((tmm, tn), lambda n, i: (i, n))
    else:
        out_shape = jax.ShapeDtypeStruct((blocks, m, tn), BF16)
        out_spec = pl.BlockSpec((1, tmm, tn), lambda n, i: (n + block_offset, i, 0))
    (out,), ex = _call(
        body, exch, name=name, grid=(nn // tn, m // tmm), out_shape=[out_shape],
        in_specs=in_specs, out_specs=[out_spec], input_output_aliases=aliases,
        semantics=("parallel", "parallel"),
    )(*args)
    return out, ex


def _mix_in_proj(x, w_in, name, exch=()):
    t, d = x.shape
    nb, _, cb = w_in.shape
    tm = _tile(t, 512, 128)

    def body(x_ref, w_ref, o_ref, xb):
        @pl.when(pl.program_id(1) == 0)
        def _():
            xb[...] = x_ref[...].astype(BF16)

        o_ref[...] = _dot(xb[...], w_ref[0])

    (out,), ex = _call(
        body, exch, name=name, grid=(t // tm, nb), out_shape=[jax.ShapeDtypeStruct((t, nb * cb), F32)],
        in_specs=[pl.BlockSpec((tm, d), lambda i, k: (i, 0)), pl.BlockSpec((1, d, cb), lambda i, k: (k, 0, 0))],
        out_specs=[pl.BlockSpec((tm, cb), lambda i, k: (i, k))],
        scratch_shapes=[pltpu.VMEM((tm, d), BF16)],
        semantics=("parallel", "arbitrary"),
    )(x, w_in)
    return out, ex


def _mix_in_bwd(dproj, w_in, dz, name, exch=()):
    t, d = dz.shape
    nb, _, cb = w_in.shape
    tm = _tile(t, 512, 128)

    def body(dp_ref, w_ref, dz_ref, dx_ref, acc):
        k = pl.program_id(1)

        @pl.when(k == 0)
        def _():
            acc[...] = jnp.zeros_like(acc)

        acc[...] += _dot_nt(dp_ref[...], w_ref[0])

        @pl.when(k == nb - 1)
        def _():
            dx_ref[...] = ALPHA * dz_ref[...] + acc[...]

    (out,), ex = _call(
        body, exch, name=name, grid=(t // tm, nb), out_shape=[jax.ShapeDtypeStruct((t, d), F32)],
        in_specs=[pl.BlockSpec((tm, cb), lambda i, k: (i, k)), pl.BlockSpec((1, d, cb), lambda i, k: (k, 0, 0)),
                  pl.BlockSpec((tm, d), lambda i, k: (i, 0))],
        out_specs=[pl.BlockSpec((tm, d), lambda i, k: (i, 0))],
        scratch_shapes=[pltpu.VMEM((tm, d), F32)],
        semantics=("parallel", "arbitrary"),
    )(dproj, w_in, dz)
    return out, ex


def _mix_out_fwd(y, w_out, x, ln_g, ln_b, name, exch=()):
    t, d = x.shape
    kk = y.shape[1]
    tm = _tile(t, 256, 128)

    def body(y_ref, w_ref, x_ref, g_ref, b_ref, z_ref, xn_ref, xnt_ref):
        z = ALPHA * x_ref[...] + _dot(y_ref[...], w_ref[...])
        z_ref[...] = z
        xn = _ln(z, g_ref[...], b_ref[...])
        xn_ref[...] = xn
        xnt_ref[...] = xn.T.astype(BF16)

    row = lambda i: (i, 0)
    fixed = lambda i: (0, 0)
    return _call(
        body, exch, name=name, grid=(t // tm,),
        out_shape=[jax.ShapeDtypeStruct((t, d), F32), jax.ShapeDtypeStruct((t, d), F32),
                   jax.ShapeDtypeStruct((d, t), BF16)],
        in_specs=[pl.BlockSpec((tm, kk), row), pl.BlockSpec((kk, d), fixed), pl.BlockSpec((tm, d), row),
                  pl.BlockSpec((1, d), fixed), pl.BlockSpec((1, d), fixed)],
        out_specs=[pl.BlockSpec((tm, d), row), pl.BlockSpec((tm, d), row), pl.BlockSpec((d, tm), lambda i: (0, i))],
        semantics=("parallel",),
    )(y, w_out, x, ln_g, ln_b)


def _mix_out_bwd(dzb, w_out, name):
    t, d = dzb.shape
    kk = w_out.shape[0]
    tm = _tile(t, 256, 128)

    def body(dz_ref, w_ref, dy_ref):
        dy_ref[...] = _dot_nt(dz_ref[...], w_ref[...])

    return pl.pallas_call(
        body, name=name, grid=(t // tm,), out_shape=jax.ShapeDtypeStruct((t, kk), F32),
        in_specs=[pl.BlockSpec((tm, d), lambda i: (i, 0)), pl.BlockSpec((kk, d), lambda i: (0, 0))],
        out_specs=pl.BlockSpec((tm, kk), lambda i: (i, 0)),
        compiler_params=_cparams("parallel"),
    )(dzb, w_out)


def _loss_ln_bwd(z, target, ln_g, ln_b, bf16_scale, name):
    t, d = z.shape
    tm = _tile(t, 512, 8)

    def body(z_ref, t_ref, g_ref, b_ref, dz_ref, dzb_ref, dg_ref, db_ref, loss_ref):
        @pl.when(pl.program_id(0) == 0)
        def _():
            dg_ref[...] = jnp.zeros_like(dg_ref)
            db_ref[...] = jnp.zeros_like(db_ref)
            loss_ref[...] = jnp.zeros_like(loss_ref)

        xh, rstd = _ln_stats(z_ref[...])
        e = xh * g_ref[...] + b_ref[...] - t_ref[...]
        loss_ref[...] += 0.5 * jnp.sum(jnp.sum(e * e, axis=-1, keepdims=True) * (1.0 / d), axis=0, keepdims=True)
        dy = e * (1.0 / d)
        dz = _ln_bwd(dy * g_ref[...], xh, rstd)
        dz_ref[...] = dz
        dzb_ref[...] = (bf16_scale * dz).astype(BF16)
        dg_ref[...] += jnp.sum(dy * xh, axis=0, keepdims=True)
        db_ref[...] += jnp.sum(dy, axis=0, keepdims=True)

    row = lambda i: (i, 0)
    fixed = lambda i: (0, 0)
    return pl.pallas_call(
        body, name=name, grid=(t // tm,),
        out_shape=[jax.ShapeDtypeStruct((t, d), F32), jax.ShapeDtypeStruct((t, d), BF16),
                   jax.ShapeDtypeStruct((1, d), F32), jax.ShapeDtypeStruct((1, d), F32),
                   jax.ShapeDtypeStruct((8, 128), F32)],
        in_specs=[pl.BlockSpec((tm, d), row), pl.BlockSpec((tm, d), row), pl.BlockSpec((1, d), fixed),
                  pl.BlockSpec((1, d), fixed)],
        out_specs=[pl.BlockSpec((tm, d), row), pl.BlockSpec((tm, d), row), pl.BlockSpec((1, d), fixed),
                   pl.BlockSpec((1, d), fixed), pl.BlockSpec((8, 128), fixed)],
        compiler_params=_cparams("arbitrary"),
    )(z, target, ln_g, ln_b)


def _ln_bwd_call(z, dy, ln_g, bf16_scale, name, exch=()):
    t, d = z.shape
    tm = _tile(t, 512, 8)

    def body(z_ref, dy_ref, g_ref, dz_ref, dzb_ref, dg_ref, db_ref):
        @pl.when(pl.program_id(0) == 0)
        def _():
            dg_ref[...] = jnp.zeros_like(dg_ref)
            db_ref[...] = jnp.zeros_like(db_ref)

        xh, rstd = _ln_stats(z_ref[...])
        dy = dy_ref[...]
        dz = _ln_bwd(dy * g_ref[...], xh, rstd)
        dz_ref[...] = dz
        dzb_ref[...] = (bf16_scale * dz).astype(BF16)
        dg_ref[...] += jnp.sum(dy * xh, axis=0, keepdims=True)
        db_ref[...] += jnp.sum(dy, axis=0, keepdims=True)

    row = lambda i: (i, 0)
    fixed = lambda i: (0, 0)
    return _call(
        body, exch, name=name, grid=(t // tm,),
        out_shape=[jax.ShapeDtypeStruct((t, d), F32), jax.ShapeDtypeStruct((t, d), BF16),
                   jax.ShapeDtypeStruct((1, d), F32), jax.ShapeDtypeStruct((1, d), F32)],
        in_specs=[pl.BlockSpec((tm, d), row), pl.BlockSpec((tm, d), row), pl.BlockSpec((1, d), fixed)],
        out_specs=[pl.BlockSpec((tm, d), row), pl.BlockSpec((tm, d), row), pl.BlockSpec((1, d), fixed),
                   pl.BlockSpec((1, d), fixed)],
        semantics=("arbitrary",),
    )(z, dy, ln_g)


CONV_ROWS = 32
SUBLANES = 8


def _fill_shifted(ext, shifted):
    rows = ext.shape[0] - SUBLANES
    for s in range(1, SUBLANES):
        for r in range(0, rows, CONV_ROWS):
            n = min(CONV_ROWS, rows - r)
            shifted[s - 1, r:r + n, :] = ext[r + s:r + s + n, :]


def _window(ext, shifted, lo, n):
    s = lo % SUBLANES
    return ext[lo:lo + n, :] if s == 0 else shifted[s - 1, lo - s:lo - s + n, :]


def _mixer_fwd(proj, conv_w, conv_b, cln_g, cln_b, sln_g, sln_b, sg_wm, sg_bb, name, exch=()):
    t = proj.shape[0]
    tm = _tile(t, 256, CHUNK)
    hb = tm // HALO
    nc = tm // CHUNK
    ch = CONV_CH

    def body(av_ref, ag_ref, bu_ref, bv_ref, hv_ref, hg_ref, cw_ref, cb_ref, lg_ref, lb_ref, sg_ref, sb_ref,
             w_ref, bb_ref, y_ref, yt_ref, c_ref, ext, ext_s):
        i = pl.program_id(0)
        halo = hv_ref[...] * _sigmoid(hg_ref[...])
        ext[0:HALO, :] = jnp.where(i > 0, halo, 0.0)
        ext[HALO:HALO + tm, :] = av_ref[...] * _sigmoid(ag_ref[...])
        _fill_shifted(ext, ext_s)
        for r in range(0, tm, CONV_ROWS):
            acc = jnp.zeros((CONV_ROWS, ch), F32) + cb_ref[...]
            for k in range(CONV_TAPS):
                lo = r + k + HALO - (CONV_TAPS - 1)
                acc = acc + cw_ref[k:k + 1, :] * _window(ext, ext_s, lo, CONV_ROWS)
            c_ref[r:r + CONV_ROWS, :] = acc
        a = _ln(c_ref[...], lg_ref[...], lb_ref[...])
        ya = a * _sigmoid(a)
        y_ref[:, 0:ch] = ya.astype(BF16)
        yt_ref[0:ch, :] = ya.T.astype(BF16)
        for h in range(HEADS):
            sl = slice(h * HEAD_DIM, (h + 1) * HEAD_DIM)
            u, _ = _gelu_and_grad(bu_ref[:, sl])
            v, _ = _gelu_and_grad(bv_ref[:, sl])
            vn = _ln(v, sg_ref[h:h + 1, :], sb_ref[h:h + 1, :])
            vn3 = vn.astype(BF16).reshape(nc, CHUNK, HEAD_DIM)
            wb = jnp.broadcast_to(w_ref[h][None], (nc, CHUNK, CHUNK))
            mixed = jnp.einsum("cts,csd->ctd", wb, vn3, preferred_element_type=F32) + bb_ref[h][None]
            yb = u * mixed.reshape(tm, HEAD_DIM)
            y_ref[:, ch + h * HEAD_DIM:ch + (h + 1) * HEAD_DIM] = yb.astype(BF16)
            yt_ref[ch + h * HEAD_DIM:ch + (h + 1) * HEAD_DIM, :] = yb.T.astype(BF16)

    col = lambda cidx: (lambda i: (i, cidx))
    prev = lambda cidx: (lambda i: (jnp.maximum(i * hb - 1, 0), cidx))
    fix2 = lambda i: (0, 0)
    fix3 = lambda i: (0, 0, 0)
    return _call(
        body, exch, name=name, grid=(t // tm,),
        out_shape=[jax.ShapeDtypeStruct((t, 2 * ch), BF16), jax.ShapeDtypeStruct((2 * ch, t), BF16),
                   jax.ShapeDtypeStruct((t, ch), F32)],
        in_specs=[pl.BlockSpec((tm, ch), col(0)), pl.BlockSpec((tm, ch), col(1)), pl.BlockSpec((tm, ch), col(2)),
                  pl.BlockSpec((tm, ch), col(3)), pl.BlockSpec((HALO, ch), prev(0)), pl.BlockSpec((HALO, ch), prev(1)),
                  pl.BlockSpec((CONV_TAPS, ch), fix2), pl.BlockSpec((1, ch), fix2), pl.BlockSpec((1, ch), fix2),
                  pl.BlockSpec((1, ch), fix2), pl.BlockSpec((HEADS, HEAD_DIM), fix2), pl.BlockSpec((HEADS, HEAD_DIM), fix2),
                  pl.BlockSpec((HEADS, CHUNK, CHUNK), fix3), pl.BlockSpec((HEADS, CHUNK, HEAD_DIM), fix3)],
        out_specs=[pl.BlockSpec((tm, 2 * ch), lambda i: (i, 0)), pl.BlockSpec((2 * ch, tm), lambda i: (0, i)),
                   pl.BlockSpec((tm, ch), lambda i: (i, 0))],
        scratch_shapes=[pltpu.VMEM((HALO + tm, ch), F32), pltpu.VMEM((SUBLANES - 1, HALO + tm, ch), F32)],
        semantics=("parallel",),
    )(proj, proj, proj, proj, proj, proj, conv_w, conv_b, cln_g, cln_b, sln_g, sln_b, sg_wm, sg_bb)


def _mixer_bwd(proj, conv_c, dy, conv_w, cln_g, cln_b, sln_g, sln_b, sg_wm, sg_wmt, sg_bb, name, exch=()):
    t = proj.shape[0]
    tm = _tile(t, 256, CHUNK)
    hb = tm // HALO
    nc = tm // CHUNK
    nt = t // tm
    ch = CONV_CH
    last_halo = t // HALO - 1

    def body(av_ref, ag_ref, bu_ref, bv_ref, hv_ref, hg_ref, c_ref, cn_ref, dya_ref, dyan_ref, dyb_ref,
             cw_ref, lg_ref, lb_ref, sg_ref, sb_ref, w_ref, wt_ref, bb_ref,
             dp_ref, dcw_ref, dcb_ref, dlg_ref, dlb_ref, dsg_ref, dsb_ref, dw_ref, dbs_ref,
             ext_h, ext_dc, ext_hs, ext_dcs, acc_cw):
        i = pl.program_id(0)

        @pl.when(i == 0)
        def _():
            acc_cw[...] = jnp.zeros_like(acc_cw)
            for ref in (dcb_ref, dlg_ref, dlb_ref, dsg_ref, dsb_ref, dw_ref, dbs_ref):
                ref[...] = jnp.zeros_like(ref)

        lg = lg_ref[...]
        lb = lb_ref[...]

        def conv_ln_bwd(c, dya):
            xh, rstd = _ln_stats(c)
            a = xh * lg + lb
            da = dya * _silu_grad(a)
            return _ln_bwd(da * lg, xh, rstd), da, xh

        fold = lambda v: jnp.sum(v.reshape(CONV_ROWS // SUBLANES, SUBLANES, ch), axis=0)
        s_lg = s_lb = s_cb = jnp.zeros((SUBLANES, ch), F32)
        for r in range(0, tm, CONV_ROWS):
            dc, da, xh = conv_ln_bwd(c_ref[r:r + CONV_ROWS, :], dya_ref[r:r + CONV_ROWS, :])
            ext_dc[r:r + CONV_ROWS, :] = dc
            s_lg, s_lb, s_cb = s_lg + fold(da * xh), s_lb + fold(da), s_cb + fold(dc)
        dlg_ref[...] += jnp.sum(s_lg, axis=0, keepdims=True)
        dlb_ref[...] += jnp.sum(s_lb, axis=0, keepdims=True)
        dcb_ref[...] += jnp.sum(s_cb, axis=0, keepdims=True)
        dcn, _, _ = conv_ln_bwd(cn_ref[...], dyan_ref[...])
        ext_dc[tm:tm + HALO, :] = jnp.where(i < nt - 1, dcn, 0.0)
        halo = hv_ref[...] * _sigmoid(hg_ref[...])
        ext_h[0:HALO, :] = jnp.where(i > 0, halo, 0.0)
        ext_h[HALO:HALO + tm, :] = av_ref[...] * _sigmoid(ag_ref[...])
        _fill_shifted(ext_h, ext_hs)
        _fill_shifted(ext_dc, ext_dcs)
        for r in range(0, tm, CONV_ROWS):
            dcr = ext_dc[r:r + CONV_ROWS, :]
            acc = jnp.zeros((CONV_ROWS, ch), F32)
            for k in range(CONV_TAPS):
                lo = r + k + HALO - (CONV_TAPS - 1)
                prod = dcr * _window(ext_h, ext_hs, lo, CONV_ROWS)
                acc_cw[k] += jnp.sum(prod.reshape(CONV_ROWS // 8, 8, ch), axis=0)
                hi = r + (CONV_TAPS - 1) - k
                acc = acc + cw_ref[k:k + 1, :] * _window(ext_dc, ext_dcs, hi, CONV_ROWS)
            sg_r = _sigmoid(ag_ref[r:r + CONV_ROWS, :])
            av_r = av_ref[r:r + CONV_ROWS, :]
            dp_ref[r:r + CONV_ROWS, 0:ch] = (acc * sg_r).astype(BF16)
            dp_ref[r:r + CONV_ROWS, ch:2 * ch] = (acc * av_r * sg_r * (1.0 - sg_r)).astype(BF16)

        @pl.when(i == nt - 1)
        def _():
            dcw_ref[...] = jnp.sum(acc_cw[...], axis=1)

        tril = (lax.broadcasted_iota(jnp.int32, (CHUNK, CHUNK), 0)
                >= lax.broadcasted_iota(jnp.int32, (CHUNK, CHUNK), 1)).astype(F32)
        for h in range(HEADS):
            sl = slice(h * HEAD_DIM, (h + 1) * HEAD_DIM)
            u, du_dx = _gelu_and_grad(bu_ref[:, sl])
            v, dv_dx = _gelu_and_grad(bv_ref[:, sl])
            xhv, rstdv = _ln_stats(v)
            gh = sg_ref[h:h + 1, :]
            vn3 = (xhv * gh + sb_ref[h:h + 1, :]).astype(BF16).reshape(nc, CHUNK, HEAD_DIM)
            wb = jnp.broadcast_to(w_ref[h][None], (nc, CHUNK, CHUNK))
            mixed = jnp.einsum("cts,csd->ctd", wb, vn3, preferred_element_type=F32) + bb_ref[h][None]
            dyb = dyb_ref[:, sl]
            d_u = dyb * mixed.reshape(tm, HEAD_DIM)
            dm = dyb * u
            dm3 = dm.reshape(nc, CHUNK, HEAD_DIM)
            dbs_ref[h:h + 1, :] += jnp.sum(jnp.sum(dm3, axis=0).T, axis=0, keepdims=True)
            dm3b = dm3.astype(BF16)
            dw_h = jnp.sum(jnp.einsum("ctd,csd->cts", dm3b, vn3, preferred_element_type=F32), axis=0)
            dw_ref[h] += dw_h * tril
            wtb = jnp.broadcast_to(wt_ref[h][None], (nc, CHUNK, CHUNK))
            d_vn = jnp.einsum("cst,ctd->csd", wtb, dm3b, preferred_element_type=F32).reshape(tm, HEAD_DIM)
            dsg_ref[h:h + 1, :] += jnp.sum(d_vn * xhv, axis=0, keepdims=True)
            dsb_ref[h:h + 1, :] += jnp.sum(d_vn, axis=0, keepdims=True)
            dv = _ln_bwd(d_vn * gh, xhv, rstdv)
            dp_ref[:, 2 * ch + h * HEAD_DIM:2 * ch + (h + 1) * HEAD_DIM] = (d_u * du_dx).astype(BF16)
            dp_ref[:, 3 * ch + h * HEAD_DIM:3 * ch + (h + 1) * HEAD_DIM] = (dv * dv_dx).astype(BF16)

    col = lambda cidx: (lambda i: (i, cidx))
    prev = lambda cidx: (lambda i: (jnp.maximum(i * hb - 1, 0), cidx))
    nxt = lambda i: (jnp.minimum((i + 1) * hb, last_halo), 0)
    fix2 = lambda i: (0, 0)
    fix3 = lambda i: (0, 0, 0)
    out_shape = [jax.ShapeDtypeStruct((t, 4 * ch), BF16), jax.ShapeDtypeStruct((CONV_TAPS, ch), F32),
                 jax.ShapeDtypeStruct((1, ch), F32), jax.ShapeDtypeStruct((1, ch), F32), jax.ShapeDtypeStruct((1, ch), F32),
                 jax.ShapeDtypeStruct((HEADS, HEAD_DIM), F32), jax.ShapeDtypeStruct((HEADS, HEAD_DIM), F32),
                 jax.ShapeDtypeStruct((HEADS, CHUNK, CHUNK), F32), jax.ShapeDtypeStruct((HEADS, CHUNK), F32)]
    out_specs = [pl.BlockSpec((tm, 4 * ch), lambda i: (i, 0)), pl.BlockSpec((CONV_TAPS, ch), fix2),
                 pl.BlockSpec((1, ch), fix2), pl.BlockSpec((1, ch), fix2), pl.BlockSpec((1, ch), fix2),
                 pl.BlockSpec((HEADS, HEAD_DIM), fix2), pl.BlockSpec((HEADS, HEAD_DIM), fix2),
                 pl.BlockSpec((HEADS, CHUNK, CHUNK), fix3), pl.BlockSpec((HEADS, CHUNK), fix2)]
    in_specs = [pl.BlockSpec((tm, ch), col(0)), pl.BlockSpec((tm, ch), col(1)), pl.BlockSpec((tm, ch), col(2)),
                pl.BlockSpec((tm, ch), col(3)), pl.BlockSpec((HALO, ch), prev(0)), pl.BlockSpec((HALO, ch), prev(1)),
                pl.BlockSpec((tm, ch), col(0)), pl.BlockSpec((HALO, ch), nxt),
                pl.BlockSpec((tm, ch), col(0)), pl.BlockSpec((HALO, ch), nxt), pl.BlockSpec((tm, ch), col(1)),
                pl.BlockSpec((CONV_TAPS, ch), fix2), pl.BlockSpec((1, ch), fix2), pl.BlockSpec((1, ch), fix2),
                pl.BlockSpec((HEADS, HEAD_DIM), fix2), pl.BlockSpec((HEADS, HEAD_DIM), fix2),
                pl.BlockSpec((HEADS, CHUNK, CHUNK), fix3), pl.BlockSpec((HEADS, CHUNK, CHUNK), fix3),
                pl.BlockSpec((HEADS, CHUNK, HEAD_DIM), fix3)]
    return _call(
        body, exch, name=name, grid=(nt,), out_shape=out_shape, in_specs=in_specs, out_specs=out_specs,
        scratch_shapes=[pltpu.VMEM((HALO + tm, ch), F32), pltpu.VMEM((tm + HALO, ch), F32),
                        pltpu.VMEM((SUBLANES - 1, HALO + tm, ch), F32), pltpu.VMEM((SUBLANES - 1, tm + HALO, ch), F32),
                        pltpu.VMEM((CONV_TAPS, 8, ch), F32)],
        semantics=("arbitrary",),
    )(proj, proj, proj, proj, proj, proj, conv_c, conv_c, dy, dy, dy,
      conv_w, cln_g, cln_b, sln_g, sln_b, sg_wm, sg_wmt, sg_bb)


def _pair_sum(parts, from_sibling, c, name):
    _, r, cc = parts.shape
    tr = _tile(r, max(16, (1 << 20) // (2 * cc)), 16)

    def body(c_ref, p_ref, s_ref, o_ref):
        o_ref[...] = (p_ref[...].astype(F32) + s_ref[...].astype(F32)).astype(BF16)

    grid_spec = pltpu.PrefetchScalarGridSpec(
        num_scalar_prefetch=1, grid=(4, r // tr),
        in_specs=[pl.BlockSpec((1, tr, cc), lambda j, i, c_ref: (2 * j + c_ref[0], i, 0)),
                  pl.BlockSpec((1, tr, cc), lambda j, i, c_ref: (j, i, 0))],
        out_specs=pl.BlockSpec((1, tr, cc), lambda j, i, c_ref: (j, i, 0)))
    return pl.pallas_call(
        body, name=name, grid_spec=grid_spec, out_shape=jax.ShapeDtypeStruct((4, r, cc), BF16),
        compiler_params=_cparams("parallel", "parallel"),
    )(c, parts, from_sibling)


def _adamw_math(w, g, m, v):
    m = ADAM_B1 * m + (1.0 - ADAM_B1) * g
    v = ADAM_B2 * v + (1.0 - ADAM_B2) * (g * g)
    m_hat = m / (1.0 - ADAM_B1 ** ADAM_STEP)
    v_hat = v / (1.0 - ADAM_B2 ** ADAM_STEP)
    delta = -ADAM_LR * (m_hat / (jnp.sqrt(v_hat) + ADAM_EPS) + ADAM_WD * w)
    return delta, m, v


def _adamw_sharded(w, m, v, chip_parts, from_chips, chip, name):
    r, cc = w.shape
    tr = _tile(r, max(16, (1 << 19) // (4 * cc) * 2), 16)

    def body(j_ref, w_ref, m_ref, v_ref, q_ref, o_ref, g_out, d_out, m_out, v_out):
        g = q_ref[0].astype(F32)
        for k in range(3):
            g = g + o_ref[k].astype(F32)
        d, mm, vv = _adamw_math(w_ref[...], g, m_ref[...], v_ref[...])
        g_out[...] = g
        d_out[...] = d
        m_out[...] = mm
        v_out[...] = vv

    row = lambda i, j_ref: (i, 0)
    grid_spec = pltpu.PrefetchScalarGridSpec(
        num_scalar_prefetch=1, grid=(r // tr,),
        in_specs=[pl.BlockSpec((tr, cc), row), pl.BlockSpec((tr, cc), row), pl.BlockSpec((tr, cc), row),
                  pl.BlockSpec((1, tr, cc), lambda i, j_ref: (j_ref[0], i, 0)),
                  pl.BlockSpec((3, tr, cc), lambda i, j_ref: (0, i, 0))],
        out_specs=[pl.BlockSpec((tr, cc), row)] * 4)
    return pl.pallas_call(
        body, name=name, grid_spec=grid_spec, out_shape=[jax.ShapeDtypeStruct((r, cc), F32)] * 4,
        compiler_params=_cparams("parallel"),
    )(chip, w, m, v, chip_parts, from_chips)


def _adamw_small(w, g, m, v, name):
    r, cc = w.shape

    def body(w_ref, g_ref, m_ref, v_ref, d_out, m_out, v_out):
        d, mm, vv = _adamw_math(w_ref[...], g_ref[...], m_ref[...], v_ref[...])
        d_out[...] = d
        m_out[...] = mm
        v_out[...] = vv

    full = pl.BlockSpec((r, cc), lambda i: (0, 0))
    return pl.pallas_call(
        body, name=name, grid=(1,), out_shape=[jax.ShapeDtypeStruct((r, cc), F32)] * 3,
        in_specs=[full] * 4, out_specs=[full] * 3, compiler_params=_cparams("arbitrary"),
    )(w, g, m, v)


SMALL = ("ln1_g", "ln1_b", "conv_b", "conv_ln_g", "conv_ln_b", "sg_ln_g", "sg_ln_b", "sg_w", "sg_b",
         "ln2_g", "ln2_b", "ln3_g", "ln3_b")
ORDER = ("ffn1_w_gate_up", "ffn1_w_down", "ln1_g", "ln1_b", "mix_w_in", "conv_w", "conv_b", "conv_ln_g", "conv_ln_b",
         "sg_ln_g", "sg_ln_b", "sg_w", "sg_b", "mix_w_out", "ln2_g", "ln2_b", "ffn2_w_gate_up", "ffn2_w_down",
         "ln3_g", "ln3_b")


def _rows128(a):
    return a.reshape(-1, 128)


def kernel(x, ffn1_w_gate_up, ffn1_w_down, ln1_g, ln1_b, mix_w_in, conv_w, conv_b, conv_ln_g, conv_ln_b, sg_ln_g, sg_ln_b, sg_w, sg_b, mix_w_out, ln2_g, ln2_b, ffn2_w_gate_up, ffn2_w_down, ln3_g, ln3_b, loss_target, m_ffn1_w_gate_up, m_ffn1_w_down, m_ln1_g, m_ln1_b, m_mix_w_in, m_conv_w, m_conv_b, m_conv_ln_g, m_conv_ln_b, m_sg_ln_g, m_sg_ln_b, m_sg_w, m_sg_b, m_mix_w_out, m_ln2_g, m_ln2_b, m_ffn2_w_gate_up, m_ffn2_w_down, m_ln3_g, m_ln3_b, v_ffn1_w_gate_up, v_ffn1_w_down, v_ln1_g, v_ln1_b, v_mix_w_in, v_conv_w, v_conv_b, v_conv_ln_g, v_conv_ln_b, v_sg_ln_g, v_sg_ln_b, v_sg_w, v_sg_b, v_mix_w_out, v_ln2_g, v_ln2_b, v_ffn2_w_gate_up, v_ffn2_w_down, v_ln3_g, v_ln3_b):
    args = dict(locals())
    w = {n: args[n][0] for n in ORDER}
    mom = {n: args["m_" + n][0] for n in ORDER}
    var = {n: args["v_" + n][0] for n in ORDER}
    x0 = x[0]
    target = loss_target[0]
    t, d = x0.shape
    my_x, my_y, my_c = lax.axis_index("x"), lax.axis_index("y"), lax.axis_index("c")
    my_chip = (2 * my_x + my_y).astype(jnp.int32).reshape(1)
    my_core = my_c.astype(jnp.int32).reshape(1)
    me = 4 * my_x + 2 * my_y + my_c

    big = ("ffn1_w_gate_up", "ffn1_w_down", "mix_w_in", "mix_w_out", "ffn2_w_gate_up", "ffn2_w_down")
    sh = {n: w[n].astype(BF16) for n in big}
    f2s = sh["ffn2_w_gate_up"].shape[1]
    (x0t, x0b), _ = _transpose_bf16(x0, "x0_transpose", with_copy=True)
    order = jnp.stack([4 * p[0] + 2 * p[1] + p[2] for p in _visit_order(my_x, my_y, my_c)]).astype(jnp.int32)
    gu1, (wgu1, wd1, conv_w_all) = _gather_and_gate_up(
        x0b, [sh["ffn1_w_gate_up"], sh["ffn1_w_down"], w["conv_w"]], [True, False, False], order, "ffn1_gate_up_fwd")
    wd1 = wd1.reshape(-1, d)
    conv_w_full = jnp.transpose(conv_w_all, (1, 0, 2)).reshape(CONV_TAPS, CONV_CH)
    tril = jnp.tril(jnp.ones((CHUNK, CHUNK), F32))
    sg_wm = w["sg_w"] * tril
    sg_wm_b = sg_wm.astype(BF16)
    sg_wmt_b = jnp.swapaxes(sg_wm, 1, 2).astype(BF16)
    sg_bb = jnp.broadcast_to(w["sg_b"][:, :, None], (HEADS, CHUNK, HEAD_DIM))
    row = lambda a: a.reshape(1, -1)

    (h1t, z1, x1), ((g_in, g_out),) = _ffn_down_fwd(
        gu1, x0, wd1, row(w["ln1_g"]), row(w["ln1_b"]), "ffn1_down_fwd",
        exch=[_gather_first([sh["mix_w_in"], sh["mix_w_out"]], [False, False])])
    x1t, ((w_in, w_out),) = _transpose_bf16(
        x1, "x1_transpose", exch=[_gather_forward([g_in, g_out], [False, False], [None, None])])
    w_out = w_out.reshape(-1, d)
    top, bottom = (0, d // 2), (d // 2, d // 2)
    gu2 = [sh["ffn2_w_gate_up"]]
    proj, ((g_gu2,),) = _mix_in_proj(x1, w_in, "mix_in_fwd", exch=[_gather_first(gu2, [True], rows=top)])
    (y, yt, conv_c), ((g_gu2,),) = _mixer_fwd(
        proj, conv_w_full, row(w["conv_b"]), row(w["conv_ln_g"]), row(w["conv_ln_b"]),
        w["sg_ln_g"], w["sg_ln_b"], sg_wm_b, sg_bb, "mixer_fwd",
        exch=[_both(_gather_first(gu2, [True], rows=bottom, into=[g_gu2]),
                    _gather_forward([g_gu2], [True], [f2s], rows=top))])
    (z2, x2, x2t), ((wgu2,), (g_d2,)) = _mix_out_fwd(
        y, w_out, x1, row(w["ln2_g"]), row(w["ln2_b"]), "mix_out_fwd",
        exch=[_gather_forward([g_gu2], [True], [f2s], rows=bottom), _gather_first([sh["ffn2_w_down"]], [False])])
    (wd2,) = _exchange_alone(_gather_forward([g_d2], [False], [None]), "ffn2_down_gather_forward")
    wd2 = wd2.reshape(-1, d)
    (g2, u2, h2t, z3), _ = _ffn_fwd(x2, wgu2, wd2, row(w["ln3_g"]), row(w["ln3_b"]), "ffn2_fwd", with_ln=False)

    f = wd1.shape[0]
    dn = _tile(d, 1024, 128)
    grads = {}
    pair = lambda p, s, label: _pair_sum(p, s, my_core, "pair_sum_" + label)
    dz3, do2, grads["ln3_g"], grads["ln3_b"], loss_tile = _loss_ln_bwd(
        z3, target, row(w["ln3_g"]), row(w["ln3_b"]), 0.5, "loss_ln3_bwd")
    p_d2, _ = _weight_grad(h2t, do2, dn, 512, "ffn2_dw_down")
    p_d2 = p_d2.reshape(N_DEV, f // N_DEV, d)
    (dg2, du2, dx2), ((s_d2,),) = _ffn_bwd(dz3, do2, g2, u2, wgu2, wd2, "ffn2_bwd", exch=[_rs_sibling([p_d2])])
    q_d2 = pair(p_d2, s_d2, "ffn2_down")
    p_gu2, ((r_d2,),) = _weight_grad(x2t, dg2, f2s, 512, "ffn2_dw_gate", blocks=N_DEV, exch=[_rs_chips([q_d2])])
    p_gu2, _ = _weight_grad(x2t, du2, f2s, 512, "ffn2_dw_up", blocks=N_DEV, block_offset=4, into=p_gu2)
    (dz2, dz2b, grads["ln2_g"], grads["ln2_b"]), ((s_gu2,),) = _ln_bwd_call(
        z2, dx2, row(w["ln2_g"]), 1.0, "ln2_bwd", exch=[_rs_sibling([p_gu2])])
    q_gu2 = pair(p_gu2, s_gu2, "ffn2_gate_up")
    dy = _mix_out_bwd(dz2b, w_out, "mix_out_bwd")
    p_out, _ = _weight_grad(yt, dz2b, dn, 512, "mix_out_dw")
    p_out = p_out.reshape(N_DEV, -1, d)
    (dproj, grads["conv_w"], grads["conv_b"], grads["conv_ln_g"], grads["conv_ln_b"], grads["sg_ln_g"],
     grads["sg_ln_b"], grads["sg_w"], grads["sg_b"]), ((r_gu2,),) = _mixer_bwd(
        proj, conv_c, dy, conv_w_full, row(w["conv_ln_g"]), row(w["conv_ln_b"]), w["sg_ln_g"], w["sg_ln_b"],
        sg_wm_b, sg_wmt_b, sg_bb, "mixer_bwd", exch=[_rs_chips([q_gu2], rows=top)])
    dx1, ((s_out,), (r_gu2,)) = _mix_in_bwd(
        dproj, w_in, dz2, "mix_in_bwd", exch=[_rs_sibling([p_out]), _rs_chips([q_gu2], rows=bottom, into=[r_gu2])])
    p_in, _ = _weight_grad(x1t, dproj, w_in.shape[2], 512, "mix_in_dw", blocks=N_DEV)
    (dz1, do1, grads["ln1_g"], grads["ln1_b"]), ((s_in,),) = _ln_bwd_call(
        z1, dx1, row(w["ln1_g"]), 0.5, "ln1_bwd", exch=[_rs_sibling([p_in])])
    q_out = pair(p_out, s_out, "mix_out")
    q_in = pair(p_in, s_in, "mix_in")
    small_parts = [_rows128(grads[n]) for n in SMALL]
    packed = jnp.concatenate(small_parts + [_rows128(grads["conv_w"]), loss_tile], axis=0)
    p_d1, ((r_in,),) = _weight_grad(h1t, do1, dn, 512, "ffn1_dw_down", exch=[_rs_chips([q_in])])
    p_d1 = p_d1.reshape(N_DEV, f // N_DEV, d)
    (dg1, du1), ((s_d1,), (r_out,), (small_all,)) = _ffn_bwd_act(
        do1, gu1, wd1, "ffn1_bwd_act",
        exch=[_rs_sibling([p_d1]), _rs_chips([q_out]), _small_gather(packed)])
    q_d1 = pair(p_d1, s_d1, "ffn1_down")
    p_gu1, ((r_d1,),) = _weight_grad(x0t, dg1, f2s, 512, "ffn1_dw_gate", blocks=N_DEV, exch=[_rs_chips([q_d1])])
    p_gu1, _ = _weight_grad(x0t, du1, f2s, 512, "ffn1_dw_up", blocks=N_DEV, block_offset=4, into=p_gu1)
    (s_gu1,) = _exchange_alone(_rs_sibling([p_gu1]), "ffn1_gate_up_sibling_exchange")
    q_gu1 = pair(p_gu1, s_gu1, "ffn1_gate_up")
    (grad_x,), ((r_gu1,),) = _ffn_bwd_dx(dz1, dg1, du1, wgu1, "ffn1_bwd_dx", exch=[_rs_chips([q_gu1])])

    chip_parts = [q_gu1, q_d1, q_in, q_out, q_gu2, q_d2]
    from_chips = [r_gu1, r_d1, r_in, r_out, r_gu2, r_d2]
    out = {}
    for k, n in enumerate(big):
        out[n] = _adamw_sharded(w[n], mom[n], var[n], chip_parts[k], from_chips[k], my_chip, "adamw_" + n)

    cw_rows = CONV_TAPS * CONV_CH // 128
    total = _sum_over_devices(small_all)
    offs = [0]
    for p in small_parts:
        offs.append(offs[-1] + p.shape[0])
    n_small = offs[-1]
    loss = total[n_small + cw_rows, 0]
    g_conv_w = lax.dynamic_slice_in_dim(total[n_small:n_small + cw_rows].reshape(CONV_TAPS, CONV_CH),
                                        me * (CONV_CH // N_DEV), CONV_CH // N_DEV, axis=1)
    pad8 = lambda a: jnp.pad(a, ((0, -a.shape[0] % 8), (0, 0)))
    pack = lambda tree, cw: jnp.concatenate([_rows128(tree[n]) for n in SMALL] + [pad8(cw)], axis=0)
    g_pack = jnp.concatenate([total[:n_small], pad8(g_conv_w)], axis=0)
    d_pack, m_pack, v_pack = _adamw_small(pack(w, w["conv_w"]), g_pack, pack(mom, mom["conv_w"]),
                                          pack(var, var["conv_w"]), "adamw_small")
    for k, n in enumerate(SMALL):
        sl = slice(offs[k], offs[k + 1])
        shp = w[n].shape
        out[n] = (total[sl].reshape(shp), d_pack[sl].reshape(shp), m_pack[sl].reshape(shp), v_pack[sl].reshape(shp))
    sl = slice(n_small, n_small + CONV_TAPS)
    out["conv_w"] = (g_conv_w, d_pack[sl], m_pack[sl], v_pack[sl])

    lead = lambda a: a[None]
    res = [loss, grad_x[None]]
    for kind in range(4):
        res += [lead(out[n][kind]) for n in ORDER]
    return tuple(res)
```

```python
import functools
import math

import jax
import jax.numpy as jnp
from jax import lax
from jax.experimental import pallas as pl
from jax.experimental.pallas import tpu as pltpu

F32, BF16 = jnp.float32, jnp.bfloat16
MESH = pl.DeviceIdType.MESH
ANY = pl.BlockSpec(memory_space=pl.ANY)

N_DEV = 8
LN_EPS = 1e-5
ALPHA = 2.0 ** 0.25
CONV_CH = 1024
CONV_TAPS = 31
HALO = 32
HEADS = 8
HEAD_DIM = 128
CHUNK = 128
ADAM_LR, ADAM_B1, ADAM_B2, ADAM_EPS, ADAM_WD, ADAM_STEP = 0.001, 0.9, 0.999, 1e-08, 0.01, 10
V7X_VMEM_LIMIT = 56 * 2 ** 20


def _cparams(*sem):
    return pltpu.CompilerParams(dimension_semantics=sem, vmem_limit_bytes=V7X_VMEM_LIMIT)


def _tile(n, pref, mult):
    best = None
    for t in range(mult, min(n, pref) + 1, mult):
        if n % t == 0:
            best = t
    return best if best is not None else n


def _dot(a, b):
    return jnp.dot(a, b, preferred_element_type=F32)


def _dot_nt(a, b):
    return lax.dot_general(a, b, (((1,), (1,)), ((), ())), preferred_element_type=F32)


def _sigmoid(x):
    return 1.0 / (1.0 + jnp.exp(-x))


def _ln_stats(z):
    mu = jnp.mean(z, axis=-1, keepdims=True)
    zc = z - mu
    var = jnp.mean(zc * zc, axis=-1, keepdims=True)
    rstd = lax.rsqrt(var + LN_EPS)
    return zc * rstd, rstd


def _ln(z, g, b):
    xh, _ = _ln_stats(z)
    return xh * g + b


def _ln_bwd(dxh, xh, rstd):
    m1 = jnp.mean(dxh, axis=-1, keepdims=True)
    m2 = jnp.mean(dxh * xh, axis=-1, keepdims=True)
    return rstd * (dxh - m1 - xh * m2)


_GK = math.sqrt(2.0 / math.pi)
_GA = 0.044715


def _gelu_and_grad(x):
    x2 = x * x
    t = jnp.tanh(_GK * (x + _GA * x * x2))
    y = 0.5 * x * (1.0 + t)
    dy = 0.5 * (1.0 + t) + 0.5 * x * (1.0 - t * t) * (_GK * (1.0 + 3.0 * _GA * x2))
    return y, dy


def _silu_grad(a):
    s = _sigmoid(a)
    return s * (1.0 + a * (1.0 - s))


def _place():
    return lax.axis_index("x"), lax.axis_index("y"), lax.axis_index("c")


def _other_chips(x, y):
    return [(1 - x, y), (x, 1 - y), (1 - x, 1 - y)]


def _visit_order(x, y, c):
    chips = _other_chips(x, y)
    return [(x, y, c), (x, y, 1 - c), (*chips[0], c), (*chips[1], c), (*chips[0], 1 - c), (*chips[1], 1 - c),
            (*chips[2], c), (*chips[2], 1 - c)]


def _gather_and_gate_up(xb, shards, relayed, order, name):
    n = len(shards)
    N_COPIES = 10
    t, d = xb.shape
    cols = shards[0].shape[1]
    tm = _tile(t, 512, 128)
    ni = t // tm
    col_major = [True] + [False] * (n - 1)

    def body(order_ref, x_ref, *refs):
        srcs, gu_ref, dsts = refs[:n], refs[n], refs[n + 1:2 * n + 1]
        wbuf, send_sems, recv_sems, local_sems, load_sem = refs[2 * n + 1:]
        b, i = pl.program_id(0), pl.program_id(1)
        x, y, c = _place()
        me, sib = (x, y, c), (x, y, 1 - c)
        chips = _other_chips(x, y)

        near_x, near_y, far = chips

        def slot(w, p, band=None):
            half = shards[w].shape[0] // 2
            rows = None if band is None else (band * half, half)
            return _block_slot(dsts[w], col_major[w], shards[w].shape[1], p, rows)

        def copy(w, s, block, to, band=None, from_src=False):
            return pltpu.make_async_remote_copy(
                src_ref=srcs[w] if from_src else slot(w, block, band), dst_ref=slot(w, block, band),
                send_sem=send_sems.at[N_COPIES * w + s], recv_sem=recv_sems.at[N_COPIES * w + s],
                device_id=to, device_id_type=MESH)

        def own(w):
            return pltpu.make_async_copy(srcs[w], slot(w, me), local_sems.at[w])

        def sends(w):
            out = [copy(w, 0, me, sib, from_src=True), copy(w, 1, me, (*near_x, c), from_src=True),
                   copy(w, 2, me, (*near_y, c), from_src=True)]
            if not relayed[w]:
                out.append(copy(w, 3, me, (*far, c), from_src=True))
            return out

        def passed_on(w):
            out = [copy(w, 4, (*near_x, c), sib), copy(w, 5, (*near_y, c), sib)]
            if relayed[w]:
                out += [copy(w, 6, (*far, c), sib, band=0), copy(w, 9, (*far, c), sib, band=1),
                        copy(w, 7, (*near_x, c), (*near_y, c), band=0), copy(w, 8, (*near_y, c), (*near_x, c), band=1)]
            else:
                out.append(copy(w, 6, (*far, c), sib))
            return out

        def start_sends(w):
            own(w).start()
            for cp in sends(w):
                cp.start()

        def got_near_x(w):
            copy(w, 1, (*near_x, c), me).wait_recv()
            copy(w, 4, (*near_x, c), sib).start()
            if relayed[w]:
                copy(w, 7, (*near_x, c), (*near_y, c), band=0).start()

        def got_near_y(w):
            copy(w, 2, (*near_y, c), me).wait_recv()
            copy(w, 5, (*near_y, c), sib).start()
            if relayed[w]:
                copy(w, 8, (*near_y, c), (*near_x, c), band=1).start()

        def got_far(w):
            if relayed[w]:
                copy(w, 7, (*far, c), me, band=0).wait_recv()
                copy(w, 6, (*far, c), sib, band=0).start()
                copy(w, 8, (*far, c), me, band=1).wait_recv()
                copy(w, 9, (*far, c), sib, band=1).start()
            else:
                copy(w, 3, (*far, c), me).wait_recv()
                copy(w, 6, (*far, c), sib).start()

        def got_from_sibling(w, which):
            if which == 0:
                copy(w, 0, sib, me).wait_recv()
            elif which == 3 and relayed[w]:
                copy(w, 6, (*far, 1 - c), me, band=0).wait_recv()
                copy(w, 9, (*far, 1 - c), me, band=1).wait_recv()
            else:
                copy(w, 3 + which, (*chips[which - 1], 1 - c), me).wait_recv()

        others = range(1, n)

        def arrive(k):
            if k == 0:
                own(0).wait()
            elif k == 1:
                got_from_sibling(0, 0)
            elif k == 2:
                got_near_x(0)
                for w in others:
                    start_sends(w)
            elif k == 3:
                got_near_y(0)
            elif k in (4, 5):
                got_from_sibling(0, k - 3)
            elif k == 6:
                got_far(0)
                for w in others:
                    got_near_x(w)
                    got_near_y(w)
            else:
                got_from_sibling(0, 3)
                for w in others:
                    got_far(w)

        @pl.when((b == 0) & (i == 0))
        def _():
            start_sends(0)

        for k in range(N_DEV):
            @pl.when((b == k) & (i == 0))
            def _(k=k):
                arrive(k)
                at = pl.multiple_of(order_ref[k] * cols, 128)
                load = pltpu.make_async_copy(dsts[0].at[:, pl.ds(at, cols)], wbuf, load_sem.at[0])
                load.start()
                load.wait()

        gu_ref[...] = _dot(x_ref[...], wbuf[...]).astype(BF16)

        @pl.when((b == N_DEV - 1) & (i == ni - 1))
        def _():
            for w in others:
                for which in range(4):
                    got_from_sibling(w, which)
                own(w).wait()
            for w in range(n):
                for cp in sends(w) + passed_on(w):
                    cp.wait_send()

    grid_spec = pltpu.PrefetchScalarGridSpec(
        num_scalar_prefetch=1, grid=(N_DEV, ni),
        in_specs=[pl.BlockSpec((tm, d), lambda b, i, o: (i, 0))] + [ANY] * n,
        out_specs=[pl.BlockSpec((tm, cols), lambda b, i, o: (i, o[b]))] + [ANY] * n,
        scratch_shapes=[pltpu.VMEM((d, cols), BF16), pltpu.SemaphoreType.DMA((N_COPIES * n,)),
                        pltpu.SemaphoreType.DMA((N_COPIES * n,)), pltpu.SemaphoreType.DMA((n,)),
                        pltpu.SemaphoreType.DMA((1,))])
    res = pl.pallas_call(
        body, name=name, grid_spec=grid_spec,
        out_shape=[jax.ShapeDtypeStruct((t, N_DEV * cols), BF16)]
        + [_gathered_shape(s, cm) for s, cm in zip(shards, col_major)],
        compiler_params=_cparams("arbitrary", "arbitrary"),
    )(order, xb, *shards)
    return res[0], res[1:]


class _Exchange:
    def __init__(self, ins, io, new, n_sems, n_local, make):
        self.ins, self.io, self.new = list(ins), list(io), list(new)
        self.n_sems, self.n_local, self.make = n_sems, n_local, make


def _block_slot(ref, col_major, cols, place, rows=None):
    k = 4 * place[0] + 2 * place[1] + place[2]
    band = slice(None) if rows is None else pl.ds(rows[0], rows[1])
    if col_major:
        return ref.at[band, pl.ds(pl.multiple_of(k * cols, 128), cols)]
    return ref.at[k] if rows is None else ref.at[k, band]


def _gathered_shape(s, col_major):
    return jax.ShapeDtypeStruct((s.shape[0], N_DEV * s.shape[1]) if col_major else (N_DEV,) + s.shape, s.dtype)


def _gather_first(shards, col_major, rows=None, into=None):
    n = len(shards)
    new = [] if into is not None else [_gathered_shape(s, cm) for s, cm in zip(shards, col_major)]

    def make(in_refs, io_refs, new_refs, send_sems, recv_sems, local_sems, base=0, local_base=0):
        x, y, c = _place()
        targets = [(x, y, 1 - c)] + [(*chip, c) for chip in _other_chips(x, y)]
        gathered = io_refs if into is not None else new_refs
        copies = []
        for w in range(n):
            src = in_refs[w] if rows is None else in_refs[w].at[pl.ds(rows[0], rows[1])]
            slot = _block_slot(gathered[w], col_major[w], shards[w].shape[1], (x, y, c), rows)
            copies.append(pltpu.make_async_copy(src, slot, local_sems.at[local_base + w]))
            for s, to in enumerate(targets):
                copies.append(pltpu.make_async_remote_copy(
                    src_ref=src, dst_ref=slot, send_sem=send_sems.at[base + 4 * w + s],
                    recv_sem=recv_sems.at[base + 4 * w + s], device_id=to, device_id_type=MESH))
        return copies

    return _Exchange(shards, into or [], new, 4 * n, n, make)


def _gather_forward(gathered, col_major, cols, rows=None):
    n = len(gathered)

    def make(in_refs, io_refs, new_refs, send_sems, recv_sems, local_sems, base=0, local_base=0):
        x, y, c = _place()
        copies = []
        for w in range(n):
            for j, chip in enumerate(_other_chips(x, y)):
                slot = _block_slot(io_refs[w], col_major[w], cols[w], (*chip, c), rows)
                copies.append(pltpu.make_async_remote_copy(
                    src_ref=slot, dst_ref=slot, send_sem=send_sems.at[base + 3 * w + j],
                    recv_sem=recv_sems.at[base + 3 * w + j], device_id=(x, y, 1 - c), device_id_type=MESH))
        return copies

    return _Exchange([], gathered, [], 3 * n, 0, make)


def _both(a, b):
    def make(in_refs, io_refs, new_refs, send_sems, recv_sems, local_sems):
        na = len(a.ins)
        return (a.make(in_refs[:na], io_refs, [], send_sems, recv_sems, local_sems, 0, 0)
                + b.make(in_refs[na:], io_refs, [], send_sems, recv_sems, local_sems, a.n_sems, a.n_local))

    return _Exchange(a.ins + b.ins, a.io, [], a.n_sems + b.n_sems, a.n_local + b.n_local, make)


def _rs_sibling(parts):
    n = len(parts)

    def make(in_refs, io_refs, new_refs, send_sems, recv_sems, local_sems):
        x, y, c = _place()
        copies = []
        for w in range(n):
            for j in range(4):
                copies.append(pltpu.make_async_remote_copy(
                    src_ref=in_refs[w].at[2 * j + (1 - c)], dst_ref=new_refs[w].at[j],
                    send_sem=send_sems.at[4 * w + j], recv_sem=recv_sems.at[4 * w + j],
                    device_id=(x, y, 1 - c), device_id_type=MESH))
        return copies

    return _Exchange(parts, [], [jax.ShapeDtypeStruct((4,) + p.shape[1:], p.dtype) for p in parts], 4 * n, 0, make)


def _rs_chips(chip_parts, rows=None, into=None):
    n = len(chip_parts)
    band = slice(None) if rows is None else pl.ds(rows[0], rows[1])
    new = [] if into is not None else [jax.ShapeDtypeStruct((3,) + p.shape[1:], p.dtype) for p in chip_parts]

    def make(in_refs, io_refs, new_refs, send_sems, recv_sems, local_sems):
        x, y, c = _place()
        landing = io_refs if into is not None else new_refs
        copies = []
        for w in range(n):
            for rel, (px, py) in enumerate(_other_chips(x, y)):
                copies.append(pltpu.make_async_remote_copy(
                    src_ref=in_refs[w].at[2 * px + py, band], dst_ref=landing[w].at[rel, band],
                    send_sem=send_sems.at[3 * w + rel], recv_sem=recv_sems.at[3 * w + rel],
                    device_id=(px, py, c), device_id_type=MESH))
        return copies

    return _Exchange(chip_parts, into or [], new, 3 * n, 0, make)


def _call(body, exch, *, name, grid, in_specs, out_specs, out_shape, scratch_shapes=(), semantics,
          input_output_aliases=None):
    exch = list(exch)
    in_specs, out_specs, out_shape = list(in_specs), list(out_specs), list(out_shape)
    scratch_shapes = list(scratch_shapes)
    if not exch:
        fn = pl.pallas_call(body, name=name, grid=grid, in_specs=in_specs, out_specs=out_specs, out_shape=out_shape,
                            scratch_shapes=scratch_shapes, input_output_aliases=input_output_aliases or {},
                            compiler_params=_cparams(*semantics))
        return lambda *args: (fn(*args), [])
    n_in, n_out, n_scr = len(in_specs), len(out_specs), len(scratch_shapes)
    aliases = dict(input_output_aliases or {})
    all_in, all_out_specs, all_out_shape, all_scr = list(in_specs), list(out_specs), list(out_shape), list(scratch_shapes)
    extra_args = []
    for ex in exch:
        for k, a in enumerate(ex.io):
            aliases[len(all_in) + len(ex.ins) + k] = len(all_out_specs) + k
        all_in += [ANY] * (len(ex.ins) + len(ex.io))
        extra_args += ex.ins + ex.io
        all_out_specs += [ANY] * (len(ex.io) + len(ex.new))
        all_out_shape += [jax.ShapeDtypeStruct(a.shape, a.dtype) for a in ex.io] + ex.new
        all_scr += [pltpu.SemaphoreType.DMA((ex.n_sems,)), pltpu.SemaphoreType.DMA((ex.n_sems,)),
                    pltpu.SemaphoreType.DMA((max(ex.n_local, 1),))]

    def wrapped(*refs):
        pos = n_in
        ex_in = []
        for ex in exch:
            k = len(ex.ins) + len(ex.io)
            ex_in.append(refs[pos:pos + k])
            pos += k
        outs = refs[pos:pos + n_out]
        pos += n_out
        ex_out = []
        for ex in exch:
            k = len(ex.io) + len(ex.new)
            ex_out.append(refs[pos:pos + k])
            pos += k
        scr = refs[pos:pos + n_scr]
        pos += n_scr
        sems = [refs[pos + 3 * k:pos + 3 * k + 3] for k in range(len(exch))]
        first = functools.reduce(jnp.logical_and, [pl.program_id(a) == 0 for a in range(len(grid))])
        last = functools.reduce(jnp.logical_and, [pl.program_id(a) == g - 1 for a, g in enumerate(grid)])

        def copies():
            out = []
            for ex, ei, eo, es in zip(exch, ex_in, ex_out, sems):
                out += ex.make(ei[:len(ex.ins)], eo[:len(ex.io)], eo[len(ex.io):], *es)
            return out

        @pl.when(first)
        def _():
            for cp in copies():
                cp.start()

        body(*refs[:n_in], *outs, *scr)

        @pl.when(last)
        def _():
            for cp in copies():
                cp.wait()

    fn = pl.pallas_call(wrapped, name=name, grid=grid, in_specs=all_in, out_specs=all_out_specs,
                        out_shape=all_out_shape, scratch_shapes=all_scr, input_output_aliases=aliases,
                        compiler_params=_cparams(*(["arbitrary"] * len(grid))))

    def run(*args):
        res = fn(*args, *extra_args)
        outs, pos, ex_res = res[:n_out], n_out, []
        for ex in exch:
            k = len(ex.io) + len(ex.new)
            ex_res.append(list(res[pos:pos + k]))
            pos += k
        return outs, ex_res

    return run


def _exchange_alone(ex, name):
    def body():
        pass

    _, res = _call(body, [ex], name=name, grid=(1,), in_specs=[], out_specs=[], out_shape=[], semantics=("arbitrary",))()
    return res[0]


def _small_gather(part):
    def make(in_refs, io_refs, new_refs, send_sems, recv_sems, local_sems):
        x, y, c = _place()
        slot = new_refs[0].at[4 * x + 2 * y + c]
        copies = [pltpu.make_async_copy(in_refs[0], slot, local_sems.at[0])]
        for d in range(1, N_DEV):
            peer = (1 - x if d & 4 else x, 1 - y if d & 2 else y, 1 - c if d & 1 else c)
            copies.append(pltpu.make_async_remote_copy(
                src_ref=in_refs[0], dst_ref=slot, send_sem=send_sems.at[d - 1], recv_sem=recv_sems.at[d - 1],
                device_id=peer, device_id_type=MESH))
        return copies

    return _Exchange([part], [], [jax.ShapeDtypeStruct((N_DEV,) + part.shape, part.dtype)], N_DEV - 1, 1, make)


def _sum_over_devices(parts):
    _, rows, lanes = parts.shape

    def body(p_ref, o_ref):
        acc = p_ref[0]
        for k in range(1, N_DEV):
            acc = acc + p_ref[k]
        o_ref[...] = acc

    return pl.pallas_call(
        body, name="small_grads_sum", grid=(1,), out_shape=jax.ShapeDtypeStruct((rows, lanes), F32),
        in_specs=[pl.BlockSpec((N_DEV, rows, lanes), lambda i: (0, 0, 0))],
        out_specs=pl.BlockSpec((rows, lanes), lambda i: (0, 0)),
        compiler_params=_cparams("arbitrary"),
    )(parts)


def _transpose_bf16(a, name, exch=(), with_copy=False):
    r, c = a.shape
    tr, tc = _tile(r, 512, 128), _tile(c, 512, 128)

    def body(a_ref, o_ref, *copy_ref):
        v = a_ref[...].astype(F32)
        o_ref[...] = v.T.astype(BF16)
        if with_copy:
            copy_ref[0][...] = v.astype(BF16)

    outs, ex = _call(
        body, exch, name=name, grid=(r // tr, c // tc),
        out_shape=[jax.ShapeDtypeStruct((c, r), BF16)] + [jax.ShapeDtypeStruct((r, c), BF16)] * with_copy,
        in_specs=[pl.BlockSpec((tr, tc), lambda i, j: (i, j))],
        out_specs=[pl.BlockSpec((tc, tr), lambda i, j: (j, i))] + [pl.BlockSpec((tr, tc), lambda i, j: (i, j))] * with_copy,
        semantics=("parallel", "parallel"),
    )(a)
    return (outs if with_copy else outs[0]), ex


def _ffn_fwd(x, wgu, wd, ln_g, ln_b, name, exch=(), with_ln=True):
    t, d = x.shape
    f = wd.shape[0]
    tm, tf = _tile(t, 512, 128), _tile(f, 512, 128)
    nf = f // tf

    def body(x_ref, wg_ref, wu_ref, wd_ref, g_ref, b_ref, go_ref, uo_ref, ht_ref, z_ref, *rest):
        xn_ref = rest[0] if with_ln else None
        xb, acc = rest[-2:]
        j = pl.program_id(1)

        @pl.when(j == 0)
        def _():
            xb[...] = x_ref[...].astype(BF16)
            acc[...] = jnp.zeros_like(acc)

        g = _dot(xb[...], wg_ref[...])
        u = _dot(xb[...], wu_ref[...])
        h = g * _sigmoid(g) * u
        go_ref[...] = g.astype(BF16)
        uo_ref[...] = u.astype(BF16)
        ht_ref[...] = h.T.astype(BF16)
        acc[...] += _dot(h.astype(BF16), wd_ref[...])

        @pl.when(j == nf - 1)
        def _():
            z = ALPHA * x_ref[...] + 0.5 * acc[...]
            z_ref[...] = z
            if with_ln:
                xn_ref[...] = _ln(z, g_ref[...], b_ref[...])

    row = lambda i, j: (i, 0)
    n_td = 2 if with_ln else 1
    return _call(
        body, exch, name=name, grid=(t // tm, nf),
        out_shape=[jax.ShapeDtypeStruct((t, f), BF16), jax.ShapeDtypeStruct((t, f), BF16),
                   jax.ShapeDtypeStruct((f, t), BF16)] + [jax.ShapeDtypeStruct((t, d), F32)] * n_td,
        in_specs=[pl.BlockSpec((tm, d), row),
                  pl.BlockSpec((d, tf), lambda i, j: (0, j)),
                  pl.BlockSpec((d, tf), lambda i, j: (0, j + nf)),
                  pl.BlockSpec((tf, d), lambda i, j: (j, 0)),
                  pl.BlockSpec((1, d), lambda i, j: (0, 0)),
                  pl.BlockSpec((1, d), lambda i, j: (0, 0))],
        out_specs=[pl.BlockSpec((tm, tf), lambda i, j: (i, j)), pl.BlockSpec((tm, tf), lambda i, j: (i, j)),
                   pl.BlockSpec((tf, tm), lambda i, j: (j, i))] + [pl.BlockSpec((tm, d), row)] * n_td,
        scratch_shapes=[pltpu.VMEM((tm, d), BF16), pltpu.VMEM((tm, d), F32)],
        semantics=("parallel", "arbitrary"),
    )(x, wgu, wgu, wd, ln_g, ln_b)


def _ffn_down_fwd(gu, x, wd, ln_g, ln_b, name, exch=()):
    t, d = x.shape
    f = wd.shape[0]
    tm, tf = _tile(t, 512, 128), _tile(f, 512, 128)
    nf = f // tf

    def body(g_ref, u_ref, wd_ref, x_ref, lg_ref, lb_ref, ht_ref, z_ref, xn_ref, acc):
        j = pl.program_id(1)

        @pl.when(j == 0)
        def _():
            acc[...] = jnp.zeros_like(acc)

        g = g_ref[...].astype(F32)
        h = g * _sigmoid(g) * u_ref[...].astype(F32)
        ht_ref[...] = h.T.astype(BF16)
        acc[...] += _dot(h.astype(BF16), wd_ref[...])

        @pl.when(j == nf - 1)
        def _():
            z = ALPHA * x_ref[...] + 0.5 * acc[...]
            z_ref[...] = z
            xn_ref[...] = _ln(z, lg_ref[...], lb_ref[...])

    row = lambda i, j: (i, 0)
    fixed = lambda i, j: (0, 0)
    return _call(
        body, exch, name=name, grid=(t // tm, nf),
        out_shape=[jax.ShapeDtypeStruct((f, t), BF16), jax.ShapeDtypeStruct((t, d), F32),
                   jax.ShapeDtypeStruct((t, d), F32)],
        in_specs=[pl.BlockSpec((tm, tf), lambda i, j: (i, j)), pl.BlockSpec((tm, tf), lambda i, j: (i, j + nf)),
                  pl.BlockSpec((tf, d), lambda i, j: (j, 0)), pl.BlockSpec((tm, d), row),
                  pl.BlockSpec((1, d), fixed), pl.BlockSpec((1, d), fixed)],
        out_specs=[pl.BlockSpec((tf, tm), lambda i, j: (j, i)), pl.BlockSpec((tm, d), row), pl.BlockSpec((tm, d), row)],
        scratch_shapes=[pltpu.VMEM((tm, d), F32)],
        semantics=("parallel", "arbitrary"),
    )(gu, gu, wd, x, ln_g, ln_b)


def _ffn_act_grads(dh, g_ref, u_ref):
    gg = g_ref[...].astype(F32)
    uu = u_ref[...].astype(F32)
    s = _sigmoid(gg)
    du = (dh * (gg * s)).astype(BF16)
    dg = (dh * uu * (s * (1.0 + gg * (1.0 - s)))).astype(BF16)
    return dg, du


def _ffn_bwd(dz, do, g, u, wgu, wd, name, exch=()):
    t, d = dz.shape
    f = wd.shape[0]
    tm, tf = _tile(t, 512, 128), _tile(f, 512, 128)
    nf = f // tf

    def body(dz_ref, do_ref, g_ref, u_ref, wg_ref, wu_ref, wd_ref, dg_ref, du_ref, dx_ref, acc):
        j = pl.program_id(1)

        @pl.when(j == 0)
        def _():
            acc[...] = jnp.zeros_like(acc)

        dg, du = _ffn_act_grads(_dot_nt(do_ref[...], wd_ref[...]), g_ref, u_ref)
        dg_ref[...] = dg
        du_ref[...] = du
        acc[...] += _dot_nt(dg, wg_ref[...]) + _dot_nt(du, wu_ref[...])

        @pl.when(j == nf - 1)
        def _():
            dx_ref[...] = ALPHA * dz_ref[...] + acc[...]

    row = lambda i, j: (i, 0)
    tile = lambda i, j: (i, j)
    return _call(
        body, exch, name=name, grid=(t // tm, nf),
        out_shape=[jax.ShapeDtypeStruct((t, f), BF16), jax.ShapeDtypeStruct((t, f), BF16),
                   jax.ShapeDtypeStruct((t, d), F32)],
        in_specs=[pl.BlockSpec((tm, d), row), pl.BlockSpec((tm, d), row),
                  pl.BlockSpec((tm, tf), tile), pl.BlockSpec((tm, tf), tile),
                  pl.BlockSpec((d, tf), lambda i, j: (0, j)),
                  pl.BlockSpec((d, tf), lambda i, j: (0, j + nf)),
                  pl.BlockSpec((tf, d), lambda i, j: (j, 0))],
        out_specs=[pl.BlockSpec((tm, tf), tile), pl.BlockSpec((tm, tf), tile), pl.BlockSpec((tm, d), row)],
        scratch_shapes=[pltpu.VMEM((tm, d), F32)],
        semantics=("parallel", "arbitrary"),
    )(dz, do, g, u, wgu, wgu, wd)


def _ffn_bwd_act(do, gu, wd, name, exch=()):
    t, d = do.shape
    f = wd.shape[0]
    tm, tf = _tile(t, 512, 128), _tile(f, 512, 128)
    nf = f // tf

    def body(do_ref, g_ref, u_ref, wd_ref, dg_ref, du_ref):
        dg, du = _ffn_act_grads(_dot_nt(do_ref[...], wd_ref[...]), g_ref, u_ref)
        dg_ref[...] = dg
        du_ref[...] = du

    tile = lambda i, j: (i, j)
    return _call(
        body, exch, name=name, grid=(t // tm, f // tf),
        out_shape=[jax.ShapeDtypeStruct((t, f), BF16), jax.ShapeDtypeStruct((t, f), BF16)],
        in_specs=[pl.BlockSpec((tm, d), lambda i, j: (i, 0)), pl.BlockSpec((tm, tf), tile),
                  pl.BlockSpec((tm, tf), lambda i, j: (i, j + nf)), pl.BlockSpec((tf, d), lambda i, j: (j, 0))],
        out_specs=[pl.BlockSpec((tm, tf), tile), pl.BlockSpec((tm, tf), tile)],
        semantics=("parallel", "parallel"),
    )(do, gu, gu, wd)


def _ffn_bwd_dx(dz, dg, du, wgu, name, exch=()):
    t, d = dz.shape
    f = dg.shape[1]
    tm, tn = _tile(t, 512, 128), _tile(d, 256, 128)

    def body(dz_ref, dg_ref, du_ref, wg_ref, wu_ref, dx_ref):
        dx_ref[...] = ALPHA * dz_ref[...] + _dot_nt(dg_ref[...], wg_ref[...]) + _dot_nt(du_ref[...], wu_ref[...])

    row = lambda i, n: (i, 0)
    tile = lambda i, n: (i, n)
    return _call(
        body, exch, name=name, grid=(t // tm, d // tn), out_shape=[jax.ShapeDtypeStruct((t, d), F32)],
        in_specs=[pl.BlockSpec((tm, tn), tile), pl.BlockSpec((tm, f), row), pl.BlockSpec((tm, f), row),
                  pl.BlockSpec((tn, f), lambda i, n: (n, 0)), pl.BlockSpec((tn, f), lambda i, n: (n, 1))],
        out_specs=[pl.BlockSpec((tm, tn), tile)],
        semantics=("parallel", "arbitrary"),
    )(dz, dg, du, wgu, wgu)


def _weight_grad(at, b, tn, tmm, name, blocks=None, block_offset=0, into=None, exch=()):
    m, t = at.shape
    nn = b.shape[1]
    tmm = _tile(m, tmm, 16)
    assert nn % tn == 0

    def body(*refs):
        at_ref, b_ref, o_ref = refs[0], refs[1], refs[-1]
        r = _dot(at_ref[...], b_ref[...]).astype(BF16)
        if blocks is None:
            o_ref[...] = r
        else:
            o_ref[0] = r

    in_specs = [pl.BlockSpec((tmm, t), lambda n, i: (i, 0)), pl.BlockSpec((t, tn), lambda n, i: (0, n))]
    args = [at, b]
    aliases = {}
    if into is not None:
        in_specs.append(ANY)
        args.append(into)
        aliases = {2: 0}
    if blocks is None:
        out_shape = jax.ShapeDtypeStruct((m, nn), BF16)
        out_spec = pl.BlockSpec((tmm, tn), lambda n, i: (i, n))
    else:
        out_shape = jax.ShapeDtypeStruct((blocks, m, tn), BF16)
        out_spec = pl.BlockSpec((1, tmm, tn), lambda n, i: (n + block_offset, i, 0))
    (out,), ex = _call(
        body, exch, name=name, grid=(nn // tn, m // tmm), out_shape=[out_shape],
        in_specs=in_specs, out_specs=[out_spec], input_output_aliases=aliases,
        semantics=("parallel", "parallel"),
    )(*args)
    return out, ex


def _mix_in_proj(x, w_in, name, exch=()):
    t, d = x.shape
    n_out = w_in.shape[1]
    tm, cb = _tile(t, 512, 128), _tile(n_out, 512, 128)

    def body(x_ref, w_ref, o_ref, xb):
        @pl.when(pl.program_id(1) == 0)
        def _():
            xb[...] = x_ref[...].astype(BF16)

        o_ref[...] = _dot(xb[...], w_ref[...])

    (out,), ex = _call(
        body, exch, name=name, grid=(t // tm, n_out // cb), out_shape=[jax.ShapeDtypeStruct((t, n_out), F32)],
        in_specs=[pl.BlockSpec((tm, d), lambda i, k: (i, 0)), pl.BlockSpec((d, cb), lambda i, k: (0, k))],
        out_specs=[pl.BlockSpec((tm, cb), lambda i, k: (i, k))],
        scratch_shapes=[pltpu.VMEM((tm, d), BF16)],
        semantics=("parallel", "arbitrary"),
    )(x, w_in)
    return out, ex


def _mix_in_bwd(dproj, w_in, dz, name, exch=()):
    t, d = dz.shape
    kk = w_in.shape[1]
    tm, tn = _tile(t, 512, 128), _tile(d, 256, 128)

    def body(dp_ref, w_ref, dz_ref, dx_ref):
        dx_ref[...] = ALPHA * dz_ref[...] + _dot_nt(dp_ref[...], w_ref[...])

    (out,), ex = _call(
        body, exch, name=name, grid=(t // tm, d // tn), out_shape=[jax.ShapeDtypeStruct((t, d), F32)],
        in_specs=[pl.BlockSpec((tm, kk), lambda i, n: (i, 0)), pl.BlockSpec((tn, kk), lambda i, n: (n, 0)),
                  pl.BlockSpec((tm, tn), lambda i, n: (i, n))],
        out_specs=[pl.BlockSpec((tm, tn), lambda i, n: (i, n))],
        semantics=("parallel", "arbitrary"),
    )(dproj, w_in, dz)
    return out, ex


def _mix_out_fwd(y, w_out, x, ln_g, ln_b, name, exch=()):
    t, d = x.shape
    kk = y.shape[1]
    tm = _tile(t, 256, 128)

    def body(y_ref, w_ref, x_ref, g_ref, b_ref, z_ref, xn_ref, xnt_ref):
        z = ALPHA * x_ref[...] + _dot(y_ref[...], w_ref[...])
        z_ref[...] = z
        xn = _ln(z, g_ref[...], b_ref[...])
        xn_ref[...] = xn
        xnt_ref[...] = xn.T.astype(BF16)

    row = lambda i: (i, 0)
    fixed = lambda i: (0, 0)
    return _call(
        body, exch, name=name, grid=(t // tm,),
        out_shape=[jax.ShapeDtypeStruct((t, d), F32), jax.ShapeDtypeStruct((t, d), F32),
                   jax.ShapeDtypeStruct((d, t), BF16)],
        in_specs=[pl.BlockSpec((tm, kk), row), pl.BlockSpec((kk, d), fixed), pl.BlockSpec((tm, d), row),
                  pl.BlockSpec((1, d), fixed), pl.BlockSpec((1, d), fixed)],
        out_specs=[pl.BlockSpec((tm, d), row), pl.BlockSpec((tm, d), row), pl.BlockSpec((d, tm), lambda i: (0, i))],
        semantics=("parallel",),
    )(y, w_out, x, ln_g, ln_b)


def _mix_out_bwd(dzb, w_out, name):
    t, d = dzb.shape
    kk = w_out.shape[0]
    tm = _tile(t, 256, 128)

    def body(dz_ref, w_ref, dy_ref):
        dy_ref[...] = _dot_nt(dz_ref[...], w_ref[...])

    return pl.pallas_call(
        body, name=name, grid=(t // tm,), out_shape=jax.ShapeDtypeStruct((t, kk), F32),
        in_specs=[pl.BlockSpec((tm, d), lambda i: (i, 0)), pl.BlockSpec((kk, d), lambda i: (0, 0))],
        out_specs=pl.BlockSpec((tm, kk), lambda i: (i, 0)),
        compiler_params=_cparams("parallel"),
    )(dzb, w_out)


def _loss_ln_bwd(z, target, ln_g, ln_b, bf16_scale, name):
    t, d = z.shape
    tm = _tile(t, 512, 8)

    def body(z_ref, t_ref, g_ref, b_ref, dz_ref, dzb_ref, dg_ref, db_ref, loss_ref):
        @pl.when(pl.program_id(0) == 0)
        def _():
            dg_ref[...] = jnp.zeros_like(dg_ref)
            db_ref[...] = jnp.zeros_like(db_ref)
            loss_ref[...] = jnp.zeros_like(loss_ref)

        xh, rstd = _ln_stats(z_ref[...])
        e = xh * g_ref[...] + b_ref[...] - t_ref[...]
        loss_ref[...] += 0.5 * jnp.sum(jnp.sum(e * e, axis=-1, keepdims=True) * (1.0 / d), axis=0, keepdims=True)
        dy = e * (1.0 / d)
        dz = _ln_bwd(dy * g_ref[...], xh, rstd)
        dz_ref[...] = dz
        dzb_ref[...] = (bf16_scale * dz).astype(BF16)
        dg_ref[...] += jnp.sum(dy * xh, axis=0, keepdims=True)
        db_ref[...] += jnp.sum(dy, axis=0, keepdims=True)

    row = lambda i: (i, 0)
    fixed = lambda i: (0, 0)
    return pl.pallas_call(
        body, name=name, grid=(t // tm,),
        out_shape=[jax.ShapeDtypeStruct((t, d), F32), jax.ShapeDtypeStruct((t, d), BF16),
                   jax.ShapeDtypeStruct((1, d), F32), jax.ShapeDtypeStruct((1, d), F32),
                   jax.ShapeDtypeStruct((8, 128), F32)],
        in_specs=[pl.BlockSpec((tm, d), row), pl.BlockSpec((tm, d), row), pl.BlockSpec((1, d), fixed),
                  pl.BlockSpec((1, d), fixed)],
        out_specs=[pl.BlockSpec((tm, d), row), pl.BlockSpec((tm, d), row), pl.BlockSpec((1, d), fixed),
                   pl.BlockSpec((1, d), fixed), pl.BlockSpec((8, 128), fixed)],
        compiler_params=_cparams("arbitrary"),
    )(z, target, ln_g, ln_b)


def _ln_bwd_call(z, dy, ln_g, bf16_scale, name, exch=()):
    t, d = z.shape
    tm = _tile(t, 512, 8)

    def body(z_ref, dy_ref, g_ref, dz_ref, dzb_ref, dg_ref, db_ref):
        @pl.when(pl.program_id(0) == 0)
        def _():
            dg_ref[...] = jnp.zeros_like(dg_ref)
            db_ref[...] = jnp.zeros_like(db_ref)

        xh, rstd = _ln_stats(z_ref[...])
        dy = dy_ref[...]
        dz = _ln_bwd(dy * g_ref[...], xh, rstd)
        dz_ref[...] = dz
        dzb_ref[...] = (bf16_scale * dz).astype(BF16)
        dg_ref[...] += jnp.sum(dy * xh, axis=0, keepdims=True)
        db_ref[...] += jnp.sum(dy, axis=0, keepdims=True)

    row = lambda i: (i, 0)
    fixed = lambda i: (0, 0)
    return _call(
        body, exch, name=name, grid=(t // tm,),
        out_shape=[jax.ShapeDtypeStruct((t, d), F32), jax.ShapeDtypeStruct((t, d), BF16),
                   jax.ShapeDtypeStruct((1, d), F32), jax.ShapeDtypeStruct((1, d), F32)],
        in_specs=[pl.BlockSpec((tm, d), row), pl.BlockSpec((tm, d), row), pl.BlockSpec((1, d), fixed)],
        out_specs=[pl.BlockSpec((tm, d), row), pl.BlockSpec((tm, d), row), pl.BlockSpec((1, d), fixed),
                   pl.BlockSpec((1, d), fixed)],
        semantics=("arbitrary",),
    )(z, dy, ln_g)


CONV_ROWS = 32
SUBLANES = 8


def _fill_shifted(ext, shifted):
    rows = ext.shape[0] - SUBLANES
    for s in range(1, SUBLANES):
        for r in range(0, rows, CONV_ROWS):
            n = min(CONV_ROWS, rows - r)
            shifted[s - 1, r:r + n, :] = ext[r + s:r + s + n, :]


def _window(ext, shifted, lo, n):
    s = lo % SUBLANES
    return ext[lo:lo + n, :] if s == 0 else shifted[s - 1, lo - s:lo - s + n, :]


def _mixer_fwd(proj, conv_w, conv_b, cln_g, cln_b, sln_g, sln_b, sg_wm, sg_bb, name, exch=()):
    t = proj.shape[0]
    tm = _tile(t, 256, CHUNK)
    hb = tm // HALO
    nc = tm // CHUNK
    ch = CONV_CH

    def body(av_ref, ag_ref, bu_ref, bv_ref, hv_ref, hg_ref, cw_ref, cb_ref, lg_ref, lb_ref, sg_ref, sb_ref,
             w_ref, bb_ref, y_ref, yt_ref, c_ref, ext, ext_s):
        i = pl.program_id(0)
        halo = hv_ref[...] * _sigmoid(hg_ref[...])
        ext[0:HALO, :] = jnp.where(i > 0, halo, 0.0)
        ext[HALO:HALO + tm, :] = av_ref[...] * _sigmoid(ag_ref[...])
        _fill_shifted(ext, ext_s)
        for r in range(0, tm, CONV_ROWS):
            acc = jnp.zeros((CONV_ROWS, ch), F32) + cb_ref[...]
            for k in range(CONV_TAPS):
                lo = r + k + HALO - (CONV_TAPS - 1)
                acc = acc + cw_ref[k:k + 1, :] * _window(ext, ext_s, lo, CONV_ROWS)
            c_ref[r:r + CONV_ROWS, :] = acc
        a = _ln(c_ref[...], lg_ref[...], lb_ref[...])
        ya = a * _sigmoid(a)
        y_ref[:, 0:ch] = ya.astype(BF16)
        yt_ref[0:ch, :] = ya.T.astype(BF16)
        for h in range(HEADS):
            sl = slice(h * HEAD_DIM, (h + 1) * HEAD_DIM)
            u, _ = _gelu_and_grad(bu_ref[:, sl])
            v, _ = _gelu_and_grad(bv_ref[:, sl])
            vn = _ln(v, sg_ref[h:h + 1, :], sb_ref[h:h + 1, :])
            vn3 = vn.astype(BF16).reshape(nc, CHUNK, HEAD_DIM)
            wb = jnp.broadcast_to(w_ref[h][None], (nc, CHUNK, CHUNK))
            mixed = jnp.einsum("cts,csd->ctd", wb, vn3, preferred_element_type=F32) + bb_ref[h][None]
            yb = u * mixed.reshape(tm, HEAD_DIM)
            y_ref[:, ch + h * HEAD_DIM:ch + (h + 1) * HEAD_DIM] = yb.astype(BF16)
            yt_ref[ch + h * HEAD_DIM:ch + (h + 1) * HEAD_DIM, :] = yb.T.astype(BF16)

    col = lambda cidx: (lambda i: (i, cidx))
    prev = lambda cidx: (lambda i: (jnp.maximum(i * hb - 1, 0), cidx))
    fix2 = lambda i: (0, 0)
    fix3 = lambda i: (0, 0, 0)
    return _call(
        body, exch, name=name, grid=(t // tm,),
        out_shape=[jax.ShapeDtypeStruct((t, 2 * ch), BF16), jax.ShapeDtypeStruct((2 * ch, t), BF16),
                   jax.ShapeDtypeStruct((t, ch), F32)],
        in_specs=[pl.BlockSpec((tm, ch), col(0)), pl.BlockSpec((tm, ch), col(1)), pl.BlockSpec((tm, ch), col(2)),
                  pl.BlockSpec((tm, ch), col(3)), pl.BlockSpec((HALO, ch), prev(0)), pl.BlockSpec((HALO, ch), prev(1)),
                  pl.BlockSpec((CONV_TAPS, ch), fix2), pl.BlockSpec((1, ch), fix2), pl.BlockSpec((1, ch), fix2),
                  pl.BlockSpec((1, ch), fix2), pl.BlockSpec((HEADS, HEAD_DIM), fix2), pl.BlockSpec((HEADS, HEAD_DIM), fix2),
                  pl.BlockSpec((HEADS, CHUNK, CHUNK), fix3), pl.BlockSpec((HEADS, CHUNK, HEAD_DIM), fix3)],
        out_specs=[pl.BlockSpec((tm, 2 * ch), lambda i: (i, 0)), pl.BlockSpec((2 * ch, tm), lambda i: (0, i)),
                   pl.BlockSpec((tm, ch), lambda i: (i, 0))],
        scratch_shapes=[pltpu.VMEM((HALO + tm, ch), F32), pltpu.VMEM((SUBLANES - 1, HALO + tm, ch), F32)],
        semantics=("parallel",),
    )(proj, proj, proj, proj, proj, proj, conv_w, conv_b, cln_g, cln_b, sln_g, sln_b, sg_wm, sg_bb)


def _mixer_bwd(proj, conv_c, dy, conv_w, cln_g, cln_b, sln_g, sln_b, sg_wm, sg_wmt, sg_bb, name, exch=()):
    t = proj.shape[0]
    tm = _tile(t, 256, CHUNK)
    hb = tm // HALO
    nc = tm // CHUNK
    nt = t // tm
    ch = CONV_CH
    last_halo = t // HALO - 1

    def body(av_ref, ag_ref, bu_ref, bv_ref, hv_ref, hg_ref, c_ref, cn_ref, dya_ref, dyan_ref, dyb_ref,
             cw_ref, lg_ref, lb_ref, sg_ref, sb_ref, w_ref, wt_ref, bb_ref,
             dp_ref, dcw_ref, dcb_ref, dlg_ref, dlb_ref, dsg_ref, dsb_ref, dw_ref, dbs_ref,
             ext_h, ext_dc, ext_hs, ext_dcs, acc_cw):
        i = pl.program_id(0)

        @pl.when(i == 0)
        def _():
            acc_cw[...] = jnp.zeros_like(acc_cw)
            for ref in (dcb_ref, dlg_ref, dlb_ref, dsg_ref, dsb_ref, dw_ref, dbs_ref):
                ref[...] = jnp.zeros_like(ref)

        lg = lg_ref[...]
        lb = lb_ref[...]

        def conv_ln_bwd(c, dya):
            xh, rstd = _ln_stats(c)
            a = xh * lg + lb
            da = dya * _silu_grad(a)
            return _ln_bwd(da * lg, xh, rstd), da, xh

        fold = lambda v: jnp.sum(v.reshape(CONV_ROWS // SUBLANES, SUBLANES, ch), axis=0)
        s_lg = s_lb = s_cb = jnp.zeros((SUBLANES, ch), F32)
        for r in range(0, tm, CONV_ROWS):
            dc, da, xh = conv_ln_bwd(c_ref[r:r + CONV_ROWS, :], dya_ref[r:r + CONV_ROWS, :])
            ext_dc[r:r + CONV_ROWS, :] = dc
            s_lg, s_lb, s_cb = s_lg + fold(da * xh), s_lb + fold(da), s_cb + fold(dc)
        dlg_ref[...] += jnp.sum(s_lg, axis=0, keepdims=True)
        dlb_ref[...] += jnp.sum(s_lb, axis=0, keepdims=True)
        dcb_ref[...] += jnp.sum(s_cb, axis=0, keepdims=True)
        dcn, _, _ = conv_ln_bwd(cn_ref[...], dyan_ref[...])
        ext_dc[tm:tm + HALO, :] = jnp.where(i < nt - 1, dcn, 0.0)
        halo = hv_ref[...] * _sigmoid(hg_ref[...])
        ext_h[0:HALO, :] = jnp.where(i > 0, halo, 0.0)
        ext_h[HALO:HALO + tm, :] = av_ref[...] * _sigmoid(ag_ref[...])
        _fill_shifted(ext_h, ext_hs)
        _fill_shifted(ext_dc, ext_dcs)
        for r in range(0, tm, CONV_ROWS):
            dcr = ext_dc[r:r + CONV_ROWS, :]
            acc = jnp.zeros((CONV_ROWS, ch), F32)
            for k in range(CONV_TAPS):
                lo = r + k + HALO - (CONV_TAPS - 1)
                prod = dcr * _window(ext_h, ext_hs, lo, CONV_ROWS)
                acc_cw[k] += jnp.sum(prod.reshape(CONV_ROWS // 8, 8, ch), axis=0)
                hi = r + (CONV_TAPS - 1) - k
                acc = acc + cw_ref[k:k + 1, :] * _window(ext_dc, ext_dcs, hi, CONV_ROWS)
            sg_r = _sigmoid(ag_ref[r:r + CONV_ROWS, :])
            av_r = av_ref[r:r + CONV_ROWS, :]
            dp_ref[r:r + CONV_ROWS, 0:ch] = (acc * sg_r).astype(BF16)
            dp_ref[r:r + CONV_ROWS, ch:2 * ch] = (acc * av_r * sg_r * (1.0 - sg_r)).astype(BF16)

        @pl.when(i == nt - 1)
        def _():
            dcw_ref[...] = jnp.sum(acc_cw[...], axis=1)

        tril = (lax.broadcasted_iota(jnp.int32, (CHUNK, CHUNK), 0)
                >= lax.broadcasted_iota(jnp.int32, (CHUNK, CHUNK), 1)).astype(F32)
        for h in range(HEADS):
            sl = slice(h * HEAD_DIM, (h + 1) * HEAD_DIM)
            u, du_dx = _gelu_and_grad(bu_ref[:, sl])
            v, dv_dx = _gelu_and_grad(bv_ref[:, sl])
            xhv, rstdv = _ln_stats(v)
            gh = sg_ref[h:h + 1, :]
            vn3 = (xhv * gh + sb_ref[h:h + 1, :]).astype(BF16).reshape(nc, CHUNK, HEAD_DIM)
            wb = jnp.broadcast_to(w_ref[h][None], (nc, CHUNK, CHUNK))
            mixed = jnp.einsum("cts,csd->ctd", wb, vn3, preferred_element_type=F32) + bb_ref[h][None]
            dyb = dyb_ref[:, sl]
            d_u = dyb * mixed.reshape(tm, HEAD_DIM)
            dm = dyb * u
            dm3 = dm.reshape(nc, CHUNK, HEAD_DIM)
            dbs_ref[h:h + 1, :] += jnp.sum(jnp.sum(dm3, axis=0).T, axis=0, keepdims=True)
            dm3b = dm3.astype(BF16)
            dw_h = jnp.sum(jnp.einsum("ctd,csd->cts", dm3b, vn3, preferred_element_type=F32), axis=0)
            dw_ref[h] += dw_h * tril
            wtb = jnp.broadcast_to(wt_ref[h][None], (nc, CHUNK, CHUNK))
            d_vn = jnp.einsum("cst,ctd->csd", wtb, dm3b, preferred_element_type=F32).reshape(tm, HEAD_DIM)
            dsg_ref[h:h + 1, :] += jnp.sum(d_vn * xhv, axis=0, keepdims=True)
            dsb_ref[h:h + 1, :] += jnp.sum(d_vn, axis=0, keepdims=True)
            dv = _ln_bwd(d_vn * gh, xhv, rstdv)
            dp_ref[:, 2 * ch + h * HEAD_DIM:2 * ch + (h + 1) * HEAD_DIM] = (d_u * du_dx).astype(BF16)
            dp_ref[:, 3 * ch + h * HEAD_DIM:3 * ch + (h + 1) * HEAD_DIM] = (dv * dv_dx).astype(BF16)

    col = lambda cidx: (lambda i: (i, cidx))
    prev = lambda cidx: (lambda i: (jnp.maximum(i * hb - 1, 0), cidx))
    nxt = lambda i: (jnp.minimum((i + 1) * hb, last_halo), 0)
    fix2 = lambda i: (0, 0)
    fix3 = lambda i: (0, 0, 0)
    out_shape = [jax.ShapeDtypeStruct((t, 4 * ch), BF16), jax.ShapeDtypeStruct((CONV_TAPS, ch), F32),
                 jax.ShapeDtypeStruct((1, ch), F32), jax.ShapeDtypeStruct((1, ch), F32), jax.ShapeDtypeStruct((1, ch), F32),
                 jax.ShapeDtypeStruct((HEADS, HEAD_DIM), F32), jax.ShapeDtypeStruct((HEADS, HEAD_DIM), F32),
                 jax.ShapeDtypeStruct((HEADS, CHUNK, CHUNK), F32), jax.ShapeDtypeStruct((HEADS, CHUNK), F32)]
    out_specs = [pl.BlockSpec((tm, 4 * ch), lambda i: (i, 0)), pl.BlockSpec((CONV_TAPS, ch), fix2),
                 pl.BlockSpec((1, ch), fix2), pl.BlockSpec((1, ch), fix2), pl.BlockSpec((1, ch), fix2),
                 pl.BlockSpec((HEADS, HEAD_DIM), fix2), pl.BlockSpec((HEADS, HEAD_DIM), fix2),
                 pl.BlockSpec((HEADS, CHUNK, CHUNK), fix3), pl.BlockSpec((HEADS, CHUNK), fix2)]
    in_specs = [pl.BlockSpec((tm, ch), col(0)), pl.BlockSpec((tm, ch), col(1)), pl.BlockSpec((tm, ch), col(2)),
                pl.BlockSpec((tm, ch), col(3)), pl.BlockSpec((HALO, ch), prev(0)), pl.BlockSpec((HALO, ch), prev(1)),
                pl.BlockSpec((tm, ch), col(0)), pl.BlockSpec((HALO, ch), nxt),
                pl.BlockSpec((tm, ch), col(0)), pl.BlockSpec((HALO, ch), nxt), pl.BlockSpec((tm, ch), col(1)),
                pl.BlockSpec((CONV_TAPS, ch), fix2), pl.BlockSpec((1, ch), fix2), pl.BlockSpec((1, ch), fix2),
                pl.BlockSpec((HEADS, HEAD_DIM), fix2), pl.BlockSpec((HEADS, HEAD_DIM), fix2),
                pl.BlockSpec((HEADS, CHUNK, CHUNK), fix3), pl.BlockSpec((HEADS, CHUNK, CHUNK), fix3),
                pl.BlockSpec((HEADS, CHUNK, HEAD_DIM), fix3)]
    return _call(
        body, exch, name=name, grid=(nt,), out_shape=out_shape, in_specs=in_specs, out_specs=out_specs,
        scratch_shapes=[pltpu.VMEM((HALO + tm, ch), F32), pltpu.VMEM((tm + HALO, ch), F32),
                        pltpu.VMEM((SUBLANES - 1, HALO + tm, ch), F32), pltpu.VMEM((SUBLANES - 1, tm + HALO, ch), F32),
                        pltpu.VMEM((CONV_TAPS, 8, ch), F32)],
        semantics=("arbitrary",),
    )(proj, proj, proj, proj, proj, proj, conv_c, conv_c, dy, dy, dy,
      conv_w, cln_g, cln_b, sln_g, sln_b, sg_wm, sg_wmt, sg_bb)


def _pair_sum(parts, from_sibling, c, name):
    _, r, cc = parts.shape
    tr = _tile(r, max(16, (1 << 20) // (2 * cc)), 16)

    def body(c_ref, p_ref, s_ref, o_ref):
        o_ref[...] = (p_ref[...].astype(F32) + s_ref[...].astype(F32)).astype(BF16)

    grid_spec = pltpu.PrefetchScalarGridSpec(
        num_scalar_prefetch=1, grid=(4, r // tr),
        in_specs=[pl.BlockSpec((1, tr, cc), lambda j, i, c_ref: (2 * j + c_ref[0], i, 0)),
                  pl.BlockSpec((1, tr, cc), lambda j, i, c_ref: (j, i, 0))],
        out_specs=pl.BlockSpec((1, tr, cc), lambda j, i, c_ref: (j, i, 0)))
    return pl.pallas_call(
        body, name=name, grid_spec=grid_spec, out_shape=jax.ShapeDtypeStruct((4, r, cc), BF16),
        compiler_params=_cparams("parallel", "parallel"),
    )(c, parts, from_sibling)


def _adamw_math(w, g, m, v):
    m = ADAM_B1 * m + (1.0 - ADAM_B1) * g
    v = ADAM_B2 * v + (1.0 - ADAM_B2) * (g * g)
    m_hat = m / (1.0 - ADAM_B1 ** ADAM_STEP)
    v_hat = v / (1.0 - ADAM_B2 ** ADAM_STEP)
    delta = -ADAM_LR * (m_hat / (jnp.sqrt(v_hat) + ADAM_EPS) + ADAM_WD * w)
    return delta, m, v


def _adamw_sharded(w, m, v, chip_parts, from_chips, chip, name):
    r, cc = w.shape
    tr = _tile(r, max(16, (1 << 19) // (4 * cc) * 2), 16)

    def body(j_ref, w_ref, m_ref, v_ref, q_ref, o_ref, g_out, d_out, m_out, v_out):
        g = q_ref[0].astype(F32)
        for k in range(3):
            g = g + o_ref[k].astype(F32)
        d, mm, vv = _adamw_math(w_ref[...], g, m_ref[...], v_ref[...])
        g_out[...] = g
        d_out[...] = d
        m_out[...] = mm
        v_out[...] = vv

    row = lambda i, j_ref: (i, 0)
    grid_spec = pltpu.PrefetchScalarGridSpec(
        num_scalar_prefetch=1, grid=(r // tr,),
        in_specs=[pl.BlockSpec((tr, cc), row), pl.BlockSpec((tr, cc), row), pl.BlockSpec((tr, cc), row),
                  pl.BlockSpec((1, tr, cc), lambda i, j_ref: (j_ref[0], i, 0)),
                  pl.BlockSpec((3, tr, cc), lambda i, j_ref: (0, i, 0))],
        out_specs=[pl.BlockSpec((tr, cc), row)] * 4)
    return pl.pallas_call(
        body, name=name, grid_spec=grid_spec, out_shape=[jax.ShapeDtypeStruct((r, cc), F32)] * 4,
        compiler_params=_cparams("parallel"),
    )(chip, w, m, v, chip_parts, from_chips)


def _adamw_small(w, g, m, v, name):
    r, cc = w.shape

    def body(w_ref, g_ref, m_ref, v_ref, d_out, m_out, v_out):
        d, mm, vv = _adamw_math(w_ref[...], g_ref[...], m_ref[...], v_ref[...])
        d_out[...] = d
        m_out[...] = mm
        v_out[...] = vv

    full = pl.BlockSpec((r, cc), lambda i: (0, 0))
    return pl.pallas_call(
        body, name=name, grid=(1,), out_shape=[jax.ShapeDtypeStruct((r, cc), F32)] * 3,
        in_specs=[full] * 4, out_specs=[full] * 3, compiler_params=_cparams("arbitrary"),
    )(w, g, m, v)


SMALL = ("ln1_g", "ln1_b", "conv_b", "conv_ln_g", "conv_ln_b", "sg_ln_g", "sg_ln_b", "sg_w", "sg_b",
         "ln2_g", "ln2_b", "ln3_g", "ln3_b")
ORDER = ("ffn1_w_gate_up", "ffn1_w_down", "ln1_g", "ln1_b", "mix_w_in", "conv_w", "conv_b", "conv_ln_g", "conv_ln_b",
         "sg_ln_g", "sg_ln_b", "sg_w", "sg_b", "mix_w_out", "ln2_g", "ln2_b", "ffn2_w_gate_up", "ffn2_w_down",
         "ln3_g", "ln3_b")


def _rows128(a):
    return a.reshape(-1, 128)


def kernel(x, ffn1_w_gate_up, ffn1_w_down, ln1_g, ln1_b, mix_w_in, conv_w, conv_b, conv_ln_g, conv_ln_b, sg_ln_g, sg_ln_b, sg_w, sg_b, mix_w_out, ln2_g, ln2_b, ffn2_w_gate_up, ffn2_w_down, ln3_g, ln3_b, loss_target, m_ffn1_w_gate_up, m_ffn1_w_down, m_ln1_g, m_ln1_b, m_mix_w_in, m_conv_w, m_conv_b, m_conv_ln_g, m_conv_ln_b, m_sg_ln_g, m_sg_ln_b, m_sg_w, m_sg_b, m_mix_w_out, m_ln2_g, m_ln2_b, m_ffn2_w_gate_up, m_ffn2_w_down, m_ln3_g, m_ln3_b, v_ffn1_w_gate_up, v_ffn1_w_down, v_ln1_g, v_ln1_b, v_mix_w_in, v_conv_w, v_conv_b, v_conv_ln_g, v_conv_ln_b, v_sg_ln_g, v_sg_ln_b, v_sg_w, v_sg_b, v_mix_w_out, v_ln2_g, v_ln2_b, v_ffn2_w_gate_up, v_ffn2_w_down, v_ln3_g, v_ln3_b):
    args = dict(locals())
    w = {n: args[n][0] for n in ORDER}
    mom = {n: args["m_" + n][0] for n in ORDER}
    var = {n: args["v_" + n][0] for n in ORDER}
    x0 = x[0]
    target = loss_target[0]
    t, d = x0.shape
    my_x, my_y, my_c = lax.axis_index("x"), lax.axis_index("y"), lax.axis_index("c")
    my_chip = (2 * my_x + my_y).astype(jnp.int32).reshape(1)
    my_core = my_c.astype(jnp.int32).reshape(1)
    me = 4 * my_x + 2 * my_y + my_c

    big = ("ffn1_w_gate_up", "ffn1_w_down", "mix_w_in", "mix_w_out", "ffn2_w_gate_up", "ffn2_w_down")
    sh = {n: w[n].astype(BF16) for n in big}
    f2s = sh["ffn2_w_gate_up"].shape[1]
    (x0t, x0b), _ = _transpose_bf16(x0, "x0_transpose", with_copy=True)
    order = jnp.stack([4 * p[0] + 2 * p[1] + p[2] for p in _visit_order(my_x, my_y, my_c)]).astype(jnp.int32)
    gu1, (wgu1, wd1, conv_w_all) = _gather_and_gate_up(
        x0b, [sh["ffn1_w_gate_up"], sh["ffn1_w_down"], w["conv_w"]], [True, True, False], order, "ffn1_gate_up_fwd")
    wd1 = wd1.reshape(-1, d)
    conv_w_full = jnp.transpose(conv_w_all, (1, 0, 2)).reshape(CONV_TAPS, CONV_CH)
    tril = jnp.tril(jnp.ones((CHUNK, CHUNK), F32))
    sg_wm = w["sg_w"] * tril
    sg_wm_b = sg_wm.astype(BF16)
    sg_wmt_b = jnp.swapaxes(sg_wm, 1, 2).astype(BF16)
    sg_bb = jnp.broadcast_to(w["sg_b"][:, :, None], (HEADS, CHUNK, HEAD_DIM))
    row = lambda a: a.reshape(1, -1)

    (h1t, z1, x1), ((g_in, g_out),) = _ffn_down_fwd(
        gu1, x0, wd1, row(w["ln1_g"]), row(w["ln1_b"]), "ffn1_down_fwd",
        exch=[_gather_first([sh["mix_w_in"], sh["mix_w_out"]], [True, False])])
    in_cols = sh["mix_w_in"].shape[1]
    x1t, ((w_in, w_out),) = _transpose_bf16(
        x1, "x1_transpose", exch=[_gather_forward([g_in, g_out], [True, False], [in_cols, None])])
    w_out = w_out.reshape(-1, d)
    top, bottom = (0, d // 2), (d // 2, d // 2)
    gu2 = [sh["ffn2_w_gate_up"]]
    proj, ((g_gu2,),) = _mix_in_proj(x1, w_in, "mix_in_fwd", exch=[_gather_first(gu2, [True], rows=top)])
    (y, yt, conv_c), ((g_gu2,),) = _mixer_fwd(
        proj, conv_w_full, row(w["conv_b"]), row(w["conv_ln_g"]), row(w["conv_ln_b"]),
        w["sg_ln_g"], w["sg_ln_b"], sg_wm_b, sg_bb, "mixer_fwd",
        exch=[_both(_gather_first(gu2, [True], rows=bottom, into=[g_gu2]),
                    _gather_forward([g_gu2], [True], [f2s], rows=top))])
    (z2, x2, x2t), ((wgu2,), (g_d2,)) = _mix_out_fwd(
        y, w_out, x1, row(w["ln2_g"]), row(w["ln2_b"]), "mix_out_fwd",
        exch=[_gather_forward([g_gu2], [True], [f2s], rows=bottom), _gather_first([sh["ffn2_w_down"]], [False])])
    (wd2,) = _exchange_alone(_gather_forward([g_d2], [False], [None]), "ffn2_down_gather_forward")
    wd2 = wd2.reshape(-1, d)
    (g2, u2, h2t, z3), _ = _ffn_fwd(x2, wgu2, wd2, row(w["ln3_g"]), row(w["ln3_b"]), "ffn2_fwd", with_ln=False)

    f = wd1.shape[0]
    dn = _tile(d, 1024, 128)
    grads = {}
    pair = lambda p, s, label: _pair_sum(p, s, my_core, "pair_sum_" + label)
    dz3, do2, grads["ln3_g"], grads["ln3_b"], loss_tile = _loss_ln_bwd(
        z3, target, row(w["ln3_g"]), row(w["ln3_b"]), 0.5, "loss_ln3_bwd")
    p_d2, _ = _weight_grad(h2t, do2, dn, 512, "ffn2_dw_down")
    p_d2 = p_d2.reshape(N_DEV, f // N_DEV, d)
    (dg2, du2, dx2), ((s_d2,),) = _ffn_bwd(dz3, do2, g2, u2, wgu2, wd2, "ffn2_bwd", exch=[_rs_sibling([p_d2])])
    q_d2 = pair(p_d2, s_d2, "ffn2_down")
    p_gu2, ((r_d2,),) = _weight_grad(x2t, dg2, f2s, 512, "ffn2_dw_gate", blocks=N_DEV, exch=[_rs_chips([q_d2])])
    p_gu2, _ = _weight_grad(x2t, du2, f2s, 512, "ffn2_dw_up", blocks=N_DEV, block_offset=4, into=p_gu2)
    (dz2, dz2b, grads["ln2_g"], grads["ln2_b"]), ((s_gu2,),) = _ln_bwd_call(
        z2, dx2, row(w["ln2_g"]), 1.0, "ln2_bwd", exch=[_rs_sibling([p_gu2])])
    q_gu2 = pair(p_gu2, s_gu2, "ffn2_gate_up")
    dy = _mix_out_bwd(dz2b, w_out, "mix_out_bwd")
    p_out, _ = _weight_grad(yt, dz2b, dn, 512, "mix_out_dw")
    p_out = p_out.reshape(N_DEV, -1, d)
    (dproj, grads["conv_w"], grads["conv_b"], grads["conv_ln_g"], grads["conv_ln_b"], grads["sg_ln_g"],
     grads["sg_ln_b"], grads["sg_w"], grads["sg_b"]), ((r_gu2,),) = _mixer_bwd(
        proj, conv_c, dy, conv_w_full, row(w["conv_ln_g"]), row(w["conv_ln_b"]), w["sg_ln_g"], w["sg_ln_b"],
        sg_wm_b, sg_wmt_b, sg_bb, "mixer_bwd", exch=[_rs_chips([q_gu2], rows=top)])
    dx1, ((s_out,), (r_gu2,)) = _mix_in_bwd(
        dproj, w_in, dz2, "mix_in_bwd", exch=[_rs_sibling([p_out]), _rs_chips([q_gu2], rows=bottom, into=[r_gu2])])
    p_in, _ = _weight_grad(x1t, dproj, in_cols, 512, "mix_in_dw", blocks=N_DEV)
    (dz1, do1, grads["ln1_g"], grads["ln1_b"]), ((s_in,),) = _ln_bwd_call(
        z1, dx1, row(w["ln1_g"]), 0.5, "ln1_bwd", exch=[_rs_sibling([p_in])])
    q_out = pair(p_out, s_out, "mix_out")
    q_in = pair(p_in, s_in, "mix_in")
    small_parts = [_rows128(grads[n]) for n in SMALL]
    packed = jnp.concatenate(small_parts + [_rows128(grads["conv_w"]), loss_tile], axis=0)
    p_d1, ((r_in,),) = _weight_grad(h1t, do1, dn, 512, "ffn1_dw_down", exch=[_rs_chips([q_in])])
    p_d1 = p_d1.reshape(N_DEV, f // N_DEV, d)
    (dg1, du1), ((s_d1,), (r_out,), (small_all,)) = _ffn_bwd_act(
        do1, gu1, wd1, "ffn1_bwd_act",
        exch=[_rs_sibling([p_d1]), _rs_chips([q_out]), _small_gather(packed)])
    q_d1 = pair(p_d1, s_d1, "ffn1_down")
    p_gu1, ((r_d1,),) = _weight_grad(x0t, dg1, f2s, 512, "ffn1_dw_gate", blocks=N_DEV, exch=[_rs_chips([q_d1])])
    p_gu1, _ = _weight_grad(x0t, du1, f2s, 512, "ffn1_dw_up", blocks=N_DEV, block_offset=4, into=p_gu1)
    (s_gu1,) = _exchange_alone(_rs_sibling([p_gu1]), "ffn1_gate_up_sibling_exchange")
    q_gu1 = pair(p_gu1, s_gu1, "ffn1_gate_up")
    (grad_x,), ((r_gu1,),) = _ffn_bwd_dx(dz1, dg1, du1, wgu1, "ffn1_bwd_dx", exch=[_rs_chips([q_gu1])])

    chip_parts = [q_gu1, q_d1, q_in, q_out, q_gu2, q_d2]
    from_chips = [r_gu1, r_d1, r_in, r_out, r_gu2, r_d2]
    out = {}
    for k, n in enumerate(big):
        out[n] = _adamw_sharded(w[n], mom[n], var[n], chip_parts[k], from_chips[k], my_chip, "adamw_" + n)

    cw_rows = CONV_TAPS * CONV_CH // 128
    total = _sum_over_devices(small_all)
    offs = [0]
    for p in small_parts:
        offs.append(offs[-1] + p.shape[0])
    n_small = offs[-1]
    loss = total[n_small + cw_rows, 0]
    g_conv_w = lax.dynamic_slice_in_dim(total[n_small:n_small + cw_rows].reshape(CONV_TAPS, CONV_CH),
                                        me * (CONV_CH // N_DEV), CONV_CH // N_DEV, axis=1)
    pad8 = lambda a: jnp.pad(a, ((0, -a.shape[0] % 8), (0, 0)))
    pack = lambda tree, cw: jnp.concatenate([_rows128(tree[n]) for n in SMALL] + [pad8(cw)], axis=0)
    g_pack = jnp.concatenate([total[:n_small], pad8(g_conv_w)], axis=0)
    d_pack, m_pack, v_pack = _adamw_small(pack(w, w["conv_w"]), g_pack, pack(mom, mom["conv_w"]),
                                          pack(var, var["conv_w"]), "adamw_small")
    for k, n in enumerate(SMALL):
        sl = slice(offs[k], offs[k + 1])
        shp = w[n].shape
        out[n] = (total[sl].reshape(shp), d_pack[sl].reshape(shp), m_pack[sl].reshape(shp), v_pack[sl].reshape(shp))
    sl = slice(n_small, n_small + CONV_TAPS)
    out["conv_w"] = (g_conv_w, d_pack[sl], m_pack[sl], v_pack[sl])

    lead = lambda a: a[None]
    res = [loss, grad_x[None]]
    for kind in range(4):
        res += [lead(out[n][kind]) for n in ORDER]
    return tuple(res)
```

```python
import functools
import math

import jax
import jax.numpy as jnp
from jax import lax
from jax.experimental import pallas as pl
from jax.experimental.pallas import tpu as pltpu

F32, BF16 = jnp.float32, jnp.bfloat16
MESH = pl.DeviceIdType.MESH
ANY = pl.BlockSpec(memory_space=pl.ANY)

N_DEV = 8
LN_EPS = 1e-5
ALPHA = 2.0 ** 0.25
CONV_CH = 1024
CONV_TAPS = 31
HALO = 32
HEADS = 8
HEAD_DIM = 128
CHUNK = 128
ADAM_LR, ADAM_B1, ADAM_B2, ADAM_EPS, ADAM_WD, ADAM_STEP = 0.001, 0.9, 0.999, 1e-08, 0.01, 10
V7X_VMEM_LIMIT = 56 * 2 ** 20

def _cparams(*sem):
    return pltpu.CompilerParams(dimension_semantics=sem, vmem_limit_bytes=V7X_VMEM_LIMIT)


def _tile(n, pref, mult):
    best = None
    for t in range(mult, min(n, pref) + 1, mult):
        if n % t == 0:
            best = t
    return best if best is not None else n


def _dot(a, b):
    return jnp.dot(a, b, preferred_element_type=F32)


def _dot_nt(a, b):
    return lax.dot_general(a, b, (((1,), (1,)), ((), ())), preferred_element_type=F32)


def _sigmoid(x):
    return 1.0 / (1.0 + jnp.exp(-x))


def _ln_stats(z):
    mu = jnp.mean(z, axis=-1, keepdims=True)
    zc = z - mu
    var = jnp.mean(zc * zc, axis=-1, keepdims=True)
    rstd = lax.rsqrt(var + LN_EPS)
    return zc * rstd, rstd


def _ln(z, g, b):
    xh, _ = _ln_stats(z)
    return xh * g + b


def _ln_bwd(dxh, xh, rstd):
    m1 = jnp.mean(dxh, axis=-1, keepdims=True)
    m2 = jnp.mean(dxh * xh, axis=-1, keepdims=True)
    return rstd * (dxh - m1 - xh * m2)


_GK = math.sqrt(2.0 / math.pi)
_GA = 0.044715


def _gelu_and_grad(x):
    x2 = x * x
    t = jnp.tanh(_GK * (x + _GA * x * x2))
    y = 0.5 * x * (1.0 + t)
    dy = 0.5 * (1.0 + t) + 0.5 * x * (1.0 - t * t) * (_GK * (1.0 + 3.0 * _GA * x2))
    return y, dy


def _silu_grad(a):
    s = _sigmoid(a)
    return s * (1.0 + a * (1.0 - s))


def _place():
    return lax.axis_index("x"), lax.axis_index("y"), lax.axis_index("c")


def _other_chips(x, y):
    return [(1 - x, y), (x, 1 - y), (1 - x, 1 - y)]


def _visit_order(x, y, c):
    chips = _other_chips(x, y)
    return [(x, y, c), (x, y, 1 - c), (*chips[0], c), (*chips[1], c), (*chips[0], 1 - c), (*chips[1], 1 - c),
            (*chips[2], c), (*chips[2], 1 - c)]


def _gather_and_gate_up(xb, shards, relayed, order, name):
    n = len(shards)
    N_COPIES = 10
    t, d = xb.shape
    cols = shards[0].shape[1]
    tm = _tile(t, 512, 128)
    ni = t // tm
    col_major = [True] + [False] * (n - 1)

    def body(order_ref, x_ref, *refs):
        srcs, gu_ref, dsts = refs[:n], refs[n], refs[n + 1:2 * n + 1]
        wbuf, send_sems, recv_sems, local_sems, load_sem = refs[2 * n + 1:]
        b, i = pl.program_id(0), pl.program_id(1)
        x, y, c = _place()
        me, sib = (x, y, c), (x, y, 1 - c)
        chips = _other_chips(x, y)

        near_x, near_y, far = chips

        def slot(w, p, band=None):
            half = shards[w].shape[0] // 2
            rows = None if band is None else (band * half, half)
            return _block_slot(dsts[w], col_major[w], shards[w].shape[1], p, rows)

        def copy(w, s, block, to, band=None, from_src=False):
            return pltpu.make_async_remote_copy(
                src_ref=srcs[w] if from_src else slot(w, block, band), dst_ref=slot(w, block, band),
                send_sem=send_sems.at[N_COPIES * w + s], recv_sem=recv_sems.at[N_COPIES * w + s],
                device_id=to, device_id_type=MESH)

        def own(w):
            return pltpu.make_async_copy(srcs[w], slot(w, me), local_sems.at[w])

        def sends(w):
            out = [copy(w, 0, me, sib, from_src=True), copy(w, 1, me, (*near_x, c), from_src=True),
                   copy(w, 2, me, (*near_y, c), from_src=True)]
            if not relayed[w]:
                out.append(copy(w, 3, me, (*far, c), from_src=True))
            return out

        def passed_on(w):
            out = [copy(w, 4, (*near_x, c), sib), copy(w, 5, (*near_y, c), sib)]
            if relayed[w]:
                out += [copy(w, 6, (*far, c), sib, band=0), copy(w, 9, (*far, c), sib, band=1),
                        copy(w, 7, (*near_x, c), (*near_y, c), band=0), copy(w, 8, (*near_y, c), (*near_x, c), band=1)]
            else:
                out.append(copy(w, 6, (*far, c), sib))
            return out

        def start_sends(w):
            own(w).start()
            for cp in sends(w):
                cp.start()

        def got_near_x(w):
            copy(w, 1, (*near_x, c), me).wait_recv()
            copy(w, 4, (*near_x, c), sib).start()
            if relayed[w]:
                copy(w, 7, (*near_x, c), (*near_y, c), band=0).start()

        def got_near_y(w):
            copy(w, 2, (*near_y, c), me).wait_recv()
            copy(w, 5, (*near_y, c), sib).start()
            if relayed[w]:
                copy(w, 8, (*near_y, c), (*near_x, c), band=1).start()

        def got_far(w):
            if relayed[w]:
                copy(w, 7, (*far, c), me, band=0).wait_recv()
                copy(w, 6, (*far, c), sib, band=0).start()
                copy(w, 8, (*far, c), me, band=1).wait_recv()
                copy(w, 9, (*far, c), sib, band=1).start()
            else:
                copy(w, 3, (*far, c), me).wait_recv()
                copy(w, 6, (*far, c), sib).start()

        def got_from_sibling(w, which):
            if which == 0:
                copy(w, 0, sib, me).wait_recv()
            elif which == 3 and relayed[w]:
                copy(w, 6, (*far, 1 - c), me, band=0).wait_recv()
                copy(w, 9, (*far, 1 - c), me, band=1).wait_recv()
            else:
                copy(w, 3 + which, (*chips[which - 1], 1 - c), me).wait_recv()

        others = range(1, n)

        def arrive(k):
            if k == 0:
                own(0).wait()
            elif k == 1:
                got_from_sibling(0, 0)
            elif k == 2:
                got_near_x(0)
                for w in others:
                    start_sends(w)
            elif k == 3:
                got_near_y(0)
            elif k in (4, 5):
                got_from_sibling(0, k - 3)
            elif k == 6:
                got_far(0)
                for w in others:
                    got_near_x(w)
                    got_near_y(w)
            else:
                got_from_sibling(0, 3)
                for w in others:
                    got_far(w)

        @pl.when((b == 0) & (i == 0))
        def _():
            start_sends(0)

        for k in range(N_DEV):
            @pl.when((b == k) & (i == 0))
            def _(k=k):
                arrive(k)
                at = pl.multiple_of(order_ref[k] * cols, 128)
                load = pltpu.make_async_copy(dsts[0].at[:, pl.ds(at, cols)], wbuf, load_sem.at[0])
                load.start()
                load.wait()

        gu_ref[...] = _dot(x_ref[...], wbuf[...]).astype(BF16)

        @pl.when((b == N_DEV - 1) & (i == ni - 1))
        def _():
            for w in others:
                for which in range(4):
                    got_from_sibling(w, which)
                own(w).wait()
            for w in range(n):
                for cp in sends(w) + passed_on(w):
                    cp.wait_send()

    grid_spec = pltpu.PrefetchScalarGridSpec(
        num_scalar_prefetch=1, grid=(N_DEV, ni),
        in_specs=[pl.BlockSpec((tm, d), lambda b, i, o: (i, 0))] + [ANY] * n,
        out_specs=[pl.BlockSpec((tm, cols), lambda b, i, o: (i, o[b]))] + [ANY] * n,
        scratch_shapes=[pltpu.VMEM((d, cols), BF16), pltpu.SemaphoreType.DMA((N_COPIES * n,)),
                        pltpu.SemaphoreType.DMA((N_COPIES * n,)), pltpu.SemaphoreType.DMA((n,)),
                        pltpu.SemaphoreType.DMA((1,))])
    res = pl.pallas_call(
        body, name=name, grid_spec=grid_spec,
        out_shape=[jax.ShapeDtypeStruct((t, N_DEV * cols), BF16)]
        + [_gathered_shape(s, cm) for s, cm in zip(shards, col_major)],
        compiler_params=_cparams("arbitrary", "arbitrary"),
    )(order, xb, *shards)
    return res[0], res[1:]


class _Exchange:
    def __init__(self, ins, io, new, n_sems, n_local, make):
        self.ins, self.io, self.new = list(ins), list(io), list(new)
        self.n_sems, self.n_local, self.make = n_sems, n_local, make


def _block_slot(ref, col_major, cols, place, rows=None):
    k = 4 * place[0] + 2 * place[1] + place[2]
    band = slice(None) if rows is None else pl.ds(rows[0], rows[1])
    if col_major:
        return ref.at[band, pl.ds(pl.multiple_of(k * cols, 128), cols)]
    return ref.at[k] if rows is None else ref.at[k, band]


def _gathered_shape(s, col_major):
    return jax.ShapeDtypeStruct((s.shape[0], N_DEV * s.shape[1]) if col_major else (N_DEV,) + s.shape, s.dtype)


def _gather_first(shards, col_major, rows=None, into=None):
    n = len(shards)
    new = [] if into is not None else [_gathered_shape(s, cm) for s, cm in zip(shards, col_major)]

    def make(in_refs, io_refs, new_refs, send_sems, recv_sems, local_sems, base=0, local_base=0):
        x, y, c = _place()
        targets = [(x, y, 1 - c)] + [(*chip, c) for chip in _other_chips(x, y)]
        gathered = io_refs if into is not None else new_refs
        copies = []
        for w in range(n):
            src = in_refs[w] if rows is None else in_refs[w].at[pl.ds(rows[0], rows[1])]
            slot = _block_slot(gathered[w], col_major[w], shards[w].shape[1], (x, y, c), rows)
            copies.append(pltpu.make_async_copy(src, slot, local_sems.at[local_base + w]))
            for s, to in enumerate(targets):
                copies.append(pltpu.make_async_remote_copy(
                    src_ref=src, dst_ref=slot, send_sem=send_sems.at[base + 4 * w + s],
                    recv_sem=recv_sems.at[base + 4 * w + s], device_id=to, device_id_type=MESH))
        return copies

    return _Exchange(shards, into or [], new, 4 * n, n, make)


def _gather_forward(gathered, col_major, cols, rows=None):
    n = len(gathered)

    def make(in_refs, io_refs, new_refs, send_sems, recv_sems, local_sems, base=0, local_base=0):
        x, y, c = _place()
        copies = []
        for w in range(n):
            for j, chip in enumerate(_other_chips(x, y)):
                slot = _block_slot(io_refs[w], col_major[w], cols[w], (*chip, c), rows)
                copies.append(pltpu.make_async_remote_copy(
                    src_ref=slot, dst_ref=slot, send_sem=send_sems.at[base + 3 * w + j],
                    recv_sem=recv_sems.at[base + 3 * w + j], device_id=(x, y, 1 - c), device_id_type=MESH))
        return copies

    return _Exchange([], gathered, [], 3 * n, 0, make)


def _both(a, b):
    def make(in_refs, io_refs, new_refs, send_sems, recv_sems, local_sems):
        na = len(a.ins)
        return (a.make(in_refs[:na], io_refs, [], send_sems, recv_sems, local_sems, 0, 0)
                + b.make(in_refs[na:], io_refs, [], send_sems, recv_sems, local_sems, a.n_sems, a.n_local))

    return _Exchange(a.ins + b.ins, a.io, [], a.n_sems + b.n_sems, a.n_local + b.n_local, make)


def _rs_sibling(parts):
    n = len(parts)

    def make(in_refs, io_refs, new_refs, send_sems, recv_sems, local_sems):
        x, y, c = _place()
        copies = []
        for w in range(n):
            for j in range(4):
                copies.append(pltpu.make_async_remote_copy(
                    src_ref=in_refs[w].at[2 * j + (1 - c)], dst_ref=new_refs[w].at[j],
                    send_sem=send_sems.at[4 * w + j], recv_sem=recv_sems.at[4 * w + j],
                    device_id=(x, y, 1 - c), device_id_type=MESH))
        return copies

    return _Exchange(parts, [], [jax.ShapeDtypeStruct((4,) + p.shape[1:], p.dtype) for p in parts], 4 * n, 0, make)


def _rs_chips(chip_parts, rows=None, into=None):
    n = len(chip_parts)
    band = slice(None) if rows is None else pl.ds(rows[0], rows[1])
    new = [] if into is not None else [jax.ShapeDtypeStruct((3,) + p.shape[1:], p.dtype) for p in chip_parts]

    def make(in_refs, io_refs, new_refs, send_sems, recv_sems, local_sems):
        x, y, c = _place()
        landing = io_refs if into is not None else new_refs
        copies = []
        for w in range(n):
            for rel, (px, py) in enumerate(_other_chips(x, y)):
                copies.append(pltpu.make_async_remote_copy(
                    src_ref=in_refs[w].at[2 * px + py, band], dst_ref=landing[w].at[rel, band],
                    send_sem=send_sems.at[3 * w + rel], recv_sem=recv_sems.at[3 * w + rel],
                    device_id=(px, py, c), device_id_type=MESH))
        return copies

    return _Exchange(chip_parts, into or [], new, 3 * n, 0, make)


class _Side:
    def __init__(self, ins, in_blocks, out_shapes, out_blocks, n_tiles, fn):
        self.ins, self.in_blocks, self.out_shapes, self.out_blocks = list(ins), in_blocks, list(out_shapes), out_blocks
        self.n_tiles, self.fn = n_tiles, fn


def _call(body, exch, *, name, grid, in_specs, out_specs, out_shape, scratch_shapes=(), semantics,
          input_output_aliases=None):
    exch = list(exch)
    in_specs, out_specs, out_shape = list(in_specs), list(out_specs), list(out_shape)
    scratch_shapes = list(scratch_shapes)
    if not exch:
        fn = pl.pallas_call(body, name=name, grid=grid, in_specs=in_specs, out_specs=out_specs, out_shape=out_shape,
                            scratch_shapes=scratch_shapes, input_output_aliases=input_output_aliases or {},
                            compiler_params=_cparams(*semantics))
        return lambda *args: (fn(*args), [])
    n_in, n_out, n_scr = len(in_specs), len(out_specs), len(scratch_shapes)
    aliases = dict(input_output_aliases or {})
    all_in, all_out_specs, all_out_shape, all_scr = list(in_specs), list(out_specs), list(out_shape), list(scratch_shapes)
    extra_args = []

    def step(idx):
        s = idx[0]
        for a in range(1, len(grid)):
            s = s * grid[a] + idx[a]
        return s

    def tile_spec(shape, where, n_tiles):
        return pl.BlockSpec(shape, lambda *idx: where(jnp.minimum(step(idx), n_tiles - 1)))

    for ex in exch:
        if isinstance(ex, _Side):
            all_in += [tile_spec(shape, where, ex.n_tiles) for shape, where in ex.in_blocks]
            extra_args += ex.ins
            all_out_specs += [tile_spec(shape, where, ex.n_tiles) for shape, where in ex.out_blocks]
            all_out_shape += ex.out_shapes
            continue
        for k, a in enumerate(ex.io):
            aliases[len(all_in) + len(ex.ins) + k] = len(all_out_specs) + k
        all_in += [ANY] * (len(ex.ins) + len(ex.io))
        extra_args += ex.ins + ex.io
        all_out_specs += [ANY] * (len(ex.io) + len(ex.new))
        all_out_shape += [jax.ShapeDtypeStruct(a.shape, a.dtype) for a in ex.io] + ex.new
        all_scr += [pltpu.SemaphoreType.DMA((ex.n_sems,)), pltpu.SemaphoreType.DMA((ex.n_sems,)),
                    pltpu.SemaphoreType.DMA((max(ex.n_local, 1),))]

    n_ins = [len(ex.ins) if isinstance(ex, _Side) else len(ex.ins) + len(ex.io) for ex in exch]
    n_outs = [len(ex.out_shapes) if isinstance(ex, _Side) else len(ex.io) + len(ex.new) for ex in exch]

    def wrapped(*refs):
        pos = n_in
        ex_in = []
        for k in n_ins:
            ex_in.append(refs[pos:pos + k])
            pos += k
        outs = refs[pos:pos + n_out]
        pos += n_out
        ex_out = []
        for k in n_outs:
            ex_out.append(refs[pos:pos + k])
            pos += k
        scr = refs[pos:pos + n_scr]
        pos += n_scr
        idx = [pl.program_id(a) for a in range(len(grid))]
        first = functools.reduce(jnp.logical_and, [i == 0 for i in idx])
        last = functools.reduce(jnp.logical_and, [i == g - 1 for i, g in zip(idx, grid)])

        def copies():
            out, at = [], pos
            for ex, ei, eo in zip(exch, ex_in, ex_out):
                if not isinstance(ex, _Side):
                    out += ex.make(ei[:len(ex.ins)], eo[:len(ex.io)], eo[len(ex.io):], *refs[at:at + 3])
                    at += 3
            return out

        @pl.when(first)
        def _():
            for cp in copies():
                cp.start()

        body(*refs[:n_in], *outs, *scr)
        for ex, ei, eo in zip(exch, ex_in, ex_out):
            if isinstance(ex, _Side):
                pl.when(step(idx) < ex.n_tiles)(functools.partial(ex.fn, ei, eo))

        @pl.when(last)
        def _():
            for cp in copies():
                cp.wait()

    fn = pl.pallas_call(wrapped, name=name, grid=grid, in_specs=all_in, out_specs=all_out_specs,
                        out_shape=all_out_shape, scratch_shapes=all_scr, input_output_aliases=aliases,
                        compiler_params=_cparams(*(["arbitrary"] * len(grid))))

    def run(*args):
        res = fn(*args, *extra_args)
        outs, pos, ex_res = res[:n_out], n_out, []
        for k in n_outs:
            ex_res.append(list(res[pos:pos + k]))
            pos += k
        return outs, ex_res

    return run


def _exchange_alone(ex, name):
    def body():
        pass

    _, res = _call(body, [ex], name=name, grid=(1,), in_specs=[], out_specs=[], out_shape=[], semantics=("arbitrary",))()
    return res[0]


def _small_gather(part):
    def make(in_refs, io_refs, new_refs, send_sems, recv_sems, local_sems):
        x, y, c = _place()
        slot = new_refs[0].at[4 * x + 2 * y + c]
        copies = [pltpu.make_async_copy(in_refs[0], slot, local_sems.at[0])]
        for d in range(1, N_DEV):
            peer = (1 - x if d & 4 else x, 1 - y if d & 2 else y, 1 - c if d & 1 else c)
            copies.append(pltpu.make_async_remote_copy(
                src_ref=in_refs[0], dst_ref=slot, send_sem=send_sems.at[d - 1], recv_sem=recv_sems.at[d - 1],
                device_id=peer, device_id_type=MESH))
        return copies

    return _Exchange([part], [], [jax.ShapeDtypeStruct((N_DEV,) + part.shape, part.dtype)], N_DEV - 1, 1, make)


def _sum_over_devices(parts):
    _, rows, lanes = parts.shape

    def body(p_ref, o_ref):
        acc = p_ref[0]
        for k in range(1, N_DEV):
            acc = acc + p_ref[k]
        o_ref[...] = acc

    return pl.pallas_call(
        body, name="small_grads_sum", grid=(1,), out_shape=jax.ShapeDtypeStruct((rows, lanes), F32),
        in_specs=[pl.BlockSpec((N_DEV, rows, lanes), lambda i: (0, 0, 0))],
        out_specs=pl.BlockSpec((rows, lanes), lambda i: (0, 0)),
        compiler_params=_cparams("arbitrary"),
    )(parts)


def _transpose_bf16(a, name, exch=(), with_copy=False):
    r, c = a.shape
    tr, tc = _tile(r, 512, 128), _tile(c, 512, 128)

    def body(a_ref, o_ref, *copy_ref):
        v = a_ref[...].astype(F32)
        o_ref[...] = v.T.astype(BF16)
        if with_copy:
            copy_ref[0][...] = v.astype(BF16)

    outs, ex = _call(
        body, exch, name=name, grid=(r // tr, c // tc),
        out_shape=[jax.ShapeDtypeStruct((c, r), BF16)] + [jax.ShapeDtypeStruct((r, c), BF16)] * with_copy,
        in_specs=[pl.BlockSpec((tr, tc), lambda i, j: (i, j))],
        out_specs=[pl.BlockSpec((tc, tr), lambda i, j: (j, i))] + [pl.BlockSpec((tr, tc), lambda i, j: (i, j))] * with_copy,
        semantics=("parallel", "parallel"),
    )(a)
    return (outs if with_copy else outs[0]), ex


def _ffn_fwd(x, wgu, wd, ln_g, ln_b, name, exch=(), with_ln=True):
    t, d = x.shape
    f = wd.shape[0]
    tm, tf = _tile(t, 512, 128), _tile(f, 512, 128)
    nf = f // tf

    def body(x_ref, wg_ref, wu_ref, wd_ref, g_ref, b_ref, go_ref, uo_ref, ht_ref, z_ref, *rest):
        xn_ref = rest[0] if with_ln else None
        xb, acc = rest[-2:]
        j = pl.program_id(1)

        @pl.when(j == 0)
        def _():
            xb[...] = x_ref[...].astype(BF16)
            acc[...] = jnp.zeros_like(acc)

        g = _dot(xb[...], wg_ref[...])
        u = _dot(xb[...], wu_ref[...])
        h = g * _sigmoid(g) * u
        go_ref[...] = g.astype(BF16)
        uo_ref[...] = u.astype(BF16)
        ht_ref[...] = h.T.astype(BF16)
        acc[...] += _dot(h.astype(BF16), wd_ref[...])

        @pl.when(j == nf - 1)
        def _():
            z = ALPHA * x_ref[...] + 0.5 * acc[...]
            z_ref[...] = z
            if with_ln:
                xn_ref[...] = _ln(z, g_ref[...], b_ref[...])

    row = lambda i, j: (i, 0)
    n_td = 2 if with_ln else 1
    return _call(
        body, exch, name=name, grid=(t // tm, nf),
        out_shape=[jax.ShapeDtypeStruct((t, f), BF16), jax.ShapeDtypeStruct((t, f), BF16),
                   jax.ShapeDtypeStruct((f, t), BF16)] + [jax.ShapeDtypeStruct((t, d), F32)] * n_td,
        in_specs=[pl.BlockSpec((tm, d), row),
                  pl.BlockSpec((d, tf), lambda i, j: (0, j)),
                  pl.BlockSpec((d, tf), lambda i, j: (0, j + nf)),
                  pl.BlockSpec((tf, d), lambda i, j: (j, 0)),
                  pl.BlockSpec((1, d), lambda i, j: (0, 0)),
                  pl.BlockSpec((1, d), lambda i, j: (0, 0))],
        out_specs=[pl.BlockSpec((tm, tf), lambda i, j: (i, j)), pl.BlockSpec((tm, tf), lambda i, j: (i, j)),
                   pl.BlockSpec((tf, tm), lambda i, j: (j, i))] + [pl.BlockSpec((tm, d), row)] * n_td,
        scratch_shapes=[pltpu.VMEM((tm, d), BF16), pltpu.VMEM((tm, d), F32)],
        semantics=("parallel", "arbitrary"),
    )(x, wgu, wgu, wd, ln_g, ln_b)


def _ffn_down_fwd(gu, x, wd, ln_g, ln_b, name, exch=()):
    t, d = x.shape
    f = wd.shape[0]
    tm, tf = _tile(t, 512, 128), _tile(f, 512, 128)
    nf = f // tf

    def body(g_ref, u_ref, wd_ref, x_ref, lg_ref, lb_ref, ht_ref, z_ref, xn_ref, acc):
        j = pl.program_id(1)

        @pl.when(j == 0)
        def _():
            acc[...] = jnp.zeros_like(acc)

        g = g_ref[...].astype(F32)
        h = g * _sigmoid(g) * u_ref[...].astype(F32)
        ht_ref[...] = h.T.astype(BF16)
        acc[...] += _dot(h.astype(BF16), wd_ref[...])

        @pl.when(j == nf - 1)
        def _():
            z = ALPHA * x_ref[...] + 0.5 * acc[...]
            z_ref[...] = z
            xn_ref[...] = _ln(z, lg_ref[...], lb_ref[...])

    row = lambda i, j: (i, 0)
    fixed = lambda i, j: (0, 0)
    return _call(
        body, exch, name=name, grid=(t // tm, nf),
        out_shape=[jax.ShapeDtypeStruct((f, t), BF16), jax.ShapeDtypeStruct((t, d), F32),
                   jax.ShapeDtypeStruct((t, d), F32)],
        in_specs=[pl.BlockSpec((tm, tf), lambda i, j: (i, j)), pl.BlockSpec((tm, tf), lambda i, j: (i, j + nf)),
                  pl.BlockSpec((tf, d), lambda i, j: (j, 0)), pl.BlockSpec((tm, d), row),
                  pl.BlockSpec((1, d), fixed), pl.BlockSpec((1, d), fixed)],
        out_specs=[pl.BlockSpec((tf, tm), lambda i, j: (j, i)), pl.BlockSpec((tm, d), row), pl.BlockSpec((tm, d), row)],
        scratch_shapes=[pltpu.VMEM((tm, d), F32)],
        semantics=("parallel", "arbitrary"),
    )(gu, gu, wd, x, ln_g, ln_b)


def _ffn_act_grads(dh, g_ref, u_ref):
    gg = g_ref[...].astype(F32)
    uu = u_ref[...].astype(F32)
    s = _sigmoid(gg)
    du = (dh * (gg * s)).astype(BF16)
    dg = (dh * uu * (s * (1.0 + gg * (1.0 - s)))).astype(BF16)
    return dg, du


def _ffn_bwd(dz, do, g, u, wgu, wd, name, exch=()):
    t, d = dz.shape
    f = wd.shape[0]
    tm, tf = _tile(t, 512, 128), _tile(f, 512, 128)
    nf = f // tf

    def body(dz_ref, do_ref, g_ref, u_ref, wg_ref, wu_ref, wd_ref, dg_ref, du_ref, dx_ref, acc):
        j = pl.program_id(1)

        @pl.when(j == 0)
        def _():
            acc[...] = jnp.zeros_like(acc)

        dg, du = _ffn_act_grads(_dot_nt(do_ref[...], wd_ref[...]), g_ref, u_ref)
        dg_ref[...] = dg
        du_ref[...] = du
        acc[...] += _dot_nt(dg, wg_ref[...]) + _dot_nt(du, wu_ref[...])

        @pl.when(j == nf - 1)
        def _():
            dx_ref[...] = ALPHA * dz_ref[...] + acc[...]

    row = lambda i, j: (i, 0)
    tile = lambda i, j: (i, j)
    return _call(
        body, exch, name=name, grid=(t // tm, nf),
        out_shape=[jax.ShapeDtypeStruct((t, f), BF16), jax.ShapeDtypeStruct((t, f), BF16),
                   jax.ShapeDtypeStruct((t, d), F32)],
        in_specs=[pl.BlockSpec((tm, d), row), pl.BlockSpec((tm, d), row),
                  pl.BlockSpec((tm, tf), tile), pl.BlockSpec((tm, tf), tile),
                  pl.BlockSpec((d, tf), lambda i, j: (0, j)),
                  pl.BlockSpec((d, tf), lambda i, j: (0, j + nf)),
                  pl.BlockSpec((tf, d), lambda i, j: (j, 0))],
        out_specs=[pl.BlockSpec((tm, tf), tile), pl.BlockSpec((tm, tf), tile), pl.BlockSpec((tm, d), row)],
        scratch_shapes=[pltpu.VMEM((tm, d), F32)],
        semantics=("parallel", "arbitrary"),
    )(dz, do, g, u, wgu, wgu, wd)


def _ffn_bwd_act(do, gu, wd, name, exch=()):
    t, d = do.shape
    f = wd.shape[0]
    tm, tf = _tile(t, 512, 128), _tile(f, 512, 128)
    nf = f // tf

    def body(do_ref, g_ref, u_ref, wd_ref, dg_ref, du_ref):
        dg, du = _ffn_act_grads(_dot_nt(do_ref[...], wd_ref[...]), g_ref, u_ref)
        dg_ref[...] = dg
        du_ref[...] = du

    tile = lambda i, j: (i, j)
    return _call(
        body, exch, name=name, grid=(t // tm, f // tf),
        out_shape=[jax.ShapeDtypeStruct((t, f), BF16), jax.ShapeDtypeStruct((t, f), BF16)],
        in_specs=[pl.BlockSpec((tm, d), lambda i, j: (i, 0)), pl.BlockSpec((tm, tf), tile),
                  pl.BlockSpec((tm, tf), lambda i, j: (i, j + nf)), pl.BlockSpec((tf, d), lambda i, j: (j, 0))],
        out_specs=[pl.BlockSpec((tm, tf), tile), pl.BlockSpec((tm, tf), tile)],
        semantics=("parallel", "parallel"),
    )(do, gu, gu, wd)


def _ffn_bwd_dx(dz, dg, du, wgu, name, exch=()):
    t, d = dz.shape
    f = dg.shape[1]
    tm, tn = _tile(t, 512, 128), _tile(d, 256, 128)

    def body(dz_ref, dg_ref, du_ref, wg_ref, wu_ref, dx_ref):
        dx_ref[...] = ALPHA * dz_ref[...] + _dot_nt(dg_ref[...], wg_ref[...]) + _dot_nt(du_ref[...], wu_ref[...])

    row = lambda i, n: (i, 0)
    tile = lambda i, n: (i, n)
    return _call(
        body, exch, name=name, grid=(t // tm, d // tn), out_shape=[jax.ShapeDtypeStruct((t, d), F32)],
        in_specs=[pl.BlockSpec((tm, tn), tile), pl.BlockSpec((tm, f), row), pl.BlockSpec((tm, f), row),
                  pl.BlockSpec((tn, f), lambda i, n: (n, 0)), pl.BlockSpec((tn, f), lambda i, n: (n, 1))],
        out_specs=[pl.BlockSpec((tm, tn), tile)],
        semantics=("parallel", "arbitrary"),
    )(dz, dg, du, wgu, wgu)


def _weight_grad(at, b, tn, tmm, name, blocks=None, block_offset=0, into=None, exch=()):
    m, t = at.shape
    nn = b.shape[1]
    tmm = _tile(m, tmm, 16)
    assert nn % tn == 0

    def body(*refs):
        at_ref, b_ref, o_ref = refs[0], refs[1], refs[-1]
        r = _dot(at_ref[...], b_ref[...]).astype(BF16)
        if blocks is None:
            o_ref[...] = r
        else:
            o_ref[0] = r

    in_specs = [pl.BlockSpec((tmm, t), lambda n, i: (i, 0)), pl.BlockSpec((t, tn), lambda n, i: (0, n))]
    args = [at, b]
    aliases = {}
    if into is not None:
        in_specs.append(ANY)
        args.append(into)
        aliases = {2: 0}
    if blocks is None:
        out_shape = jax.ShapeDtypeStruct((m, nn), BF16)
        out_spec = pl.BlockSpec((tmm, tn), lambda n, i: (i, n))
    else:
        out_shape = jax.ShapeDtypeStruct((blocks, m, tn), BF16)
        out_spec = pl.BlockSpec((1, tmm, tn), lambda n, i: (n + block_offset, i, 0))
    (out,), ex = _call(
        body, exch, name=name, grid=(nn // tn, m // tmm), out_shape=[out_shape],
        in_specs=in_specs, out_specs=[out_spec], input_output_aliases=aliases,
        semantics=("parallel", "parallel"),
    )(*args)
    return out, ex


def _mix_in_proj(x, w_in, name, exch=()):
    t, d = x.shape
    n_out = w_in.shape[1]
    tm, cb = _tile(t, 512, 128), _tile(n_out, 512, 128)

    def body(x_ref, w_ref, o_ref, xb):
        @pl.when(pl.program_id(1) == 0)
        def _():
            xb[...] = x_ref[...].astype(BF16)

        o_ref[...] = _dot(xb[...], w_ref[...])

    (out,), ex = _call(
        body, exch, name=name, grid=(t // tm, n_out // cb), out_shape=[jax.ShapeDtypeStruct((t, n_out), F32)],
        in_specs=[pl.BlockSpec((tm, d), lambda i, k: (i, 0)), pl.BlockSpec((d, cb), lambda i, k: (0, k))],
        out_specs=[pl.BlockSpec((tm, cb), lambda i, k: (i, k))],
        scratch_shapes=[pltpu.VMEM((tm, d), BF16)],
        semantics=("parallel", "arbitrary"),
    )(x, w_in)
    return out, ex


def _mix_in_bwd(dproj, w_in, dz, name, exch=()):
    t, d = dz.shape
    kk = w_in.shape[1]
    tm, tn = _tile(t, 512, 128), _tile(d, 256, 128)

    def body(dp_ref, w_ref, dz_ref, dx_ref):
        dx_ref[...] = ALPHA * dz_ref[...] + _dot_nt(dp_ref[...], w_ref[...])

    (out,), ex = _call(
        body, exch, name=name, grid=(t // tm, d // tn), out_shape=[jax.ShapeDtypeStruct((t, d), F32)],
        in_specs=[pl.BlockSpec((tm, kk), lambda i, n: (i, 0)), pl.BlockSpec((tn, kk), lambda i, n: (n, 0)),
                  pl.BlockSpec((tm, tn), lambda i, n: (i, n))],
        out_specs=[pl.BlockSpec((tm, tn), lambda i, n: (i, n))],
        semantics=("parallel", "arbitrary"),
    )(dproj, w_in, dz)
    return out, ex


def _mix_out_fwd(y, w_out, x, ln_g, ln_b, name, exch=()):
    t, d = x.shape
    kk = y.shape[1]
    tm = _tile(t, 256, 128)

    def body(y_ref, w_ref, x_ref, g_ref, b_ref, z_ref, xn_ref, xnt_ref):
        z = ALPHA * x_ref[...] + _dot(y_ref[...], w_ref[...])
        z_ref[...] = z
        xn = _ln(z, g_ref[...], b_ref[...])
        xn_ref[...] = xn
        xnt_ref[...] = xn.T.astype(BF16)

    row = lambda i: (i, 0)
    fixed = lambda i: (0, 0)
    return _call(
        body, exch, name=name, grid=(t // tm,),
        out_shape=[jax.ShapeDtypeStruct((t, d), F32), jax.ShapeDtypeStruct((t, d), F32),
                   jax.ShapeDtypeStruct((d, t), BF16)],
        in_specs=[pl.BlockSpec((tm, kk), row), pl.BlockSpec((kk, d), fixed), pl.BlockSpec((tm, d), row),
                  pl.BlockSpec((1, d), fixed), pl.BlockSpec((1, d), fixed)],
        out_specs=[pl.BlockSpec((tm, d), row), pl.BlockSpec((tm, d), row), pl.BlockSpec((d, tm), lambda i: (0, i))],
        semantics=("parallel",),
    )(y, w_out, x, ln_g, ln_b)


def _mix_out_bwd(dzb, w_out, name):
    t, d = dzb.shape
    kk = w_out.shape[0]
    tm = _tile(t, 256, 128)

    def body(dz_ref, w_ref, dy_ref):
        dy_ref[...] = _dot_nt(dz_ref[...], w_ref[...])

    return pl.pallas_call(
        body, name=name, grid=(t // tm,), out_shape=jax.ShapeDtypeStruct((t, kk), F32),
        in_specs=[pl.BlockSpec((tm, d), lambda i: (i, 0)), pl.BlockSpec((kk, d), lambda i: (0, 0))],
        out_specs=pl.BlockSpec((tm, kk), lambda i: (i, 0)),
        compiler_params=_cparams("parallel"),
    )(dzb, w_out)


def _loss_ln_bwd(z, target, ln_g, ln_b, bf16_scale, name):
    t, d = z.shape
    tm = _tile(t, 512, 8)

    def body(z_ref, t_ref, g_ref, b_ref, dz_ref, dzb_ref, dg_ref, db_ref, loss_ref):
        @pl.when(pl.program_id(0) == 0)
        def _():
            dg_ref[...] = jnp.zeros_like(dg_ref)
            db_ref[...] = jnp.zeros_like(db_ref)
            loss_ref[...] = jnp.zeros_like(loss_ref)

        xh, rstd = _ln_stats(z_ref[...])
        e = xh * g_ref[...] + b_ref[...] - t_ref[...]
        loss_ref[...] += 0.5 * jnp.sum(jnp.sum(e * e, axis=-1, keepdims=True) * (1.0 / d), axis=0, keepdims=True)
        dy = e * (1.0 / d)
        dz = _ln_bwd(dy * g_ref[...], xh, rstd)
        dz_ref[...] = dz
        dzb_ref[...] = (bf16_scale * dz).astype(BF16)
        dg_ref[...] += jnp.sum(dy * xh, axis=0, keepdims=True)
        db_ref[...] += jnp.sum(dy, axis=0, keepdims=True)

    row = lambda i: (i, 0)
    fixed = lambda i: (0, 0)
    return pl.pallas_call(
        body, name=name, grid=(t // tm,),
        out_shape=[jax.ShapeDtypeStruct((t, d), F32), jax.ShapeDtypeStruct((t, d), BF16),
                   jax.ShapeDtypeStruct((1, d), F32), jax.ShapeDtypeStruct((1, d), F32),
                   jax.ShapeDtypeStruct((8, 128), F32)],
        in_specs=[pl.BlockSpec((tm, d), row), pl.BlockSpec((tm, d), row), pl.BlockSpec((1, d), fixed),
                  pl.BlockSpec((1, d), fixed)],
        out_specs=[pl.BlockSpec((tm, d), row), pl.BlockSpec((tm, d), row), pl.BlockSpec((1, d), fixed),
                   pl.BlockSpec((1, d), fixed), pl.BlockSpec((8, 128), fixed)],
        compiler_params=_cparams("arbitrary"),
    )(z, target, ln_g, ln_b)


def _ln_bwd_call(z, dy, ln_g, bf16_scale, name, exch=()):
    t, d = z.shape
    tm = _tile(t, 512, 8)

    def body(z_ref, dy_ref, g_ref, dz_ref, dzb_ref, dg_ref, db_ref):
        @pl.when(pl.program_id(0) == 0)
        def _():
            dg_ref[...] = jnp.zeros_like(dg_ref)
            db_ref[...] = jnp.zeros_like(db_ref)

        xh, rstd = _ln_stats(z_ref[...])
        dy = dy_ref[...]
        dz = _ln_bwd(dy * g_ref[...], xh, rstd)
        dz_ref[...] = dz
        dzb_ref[...] = (bf16_scale * dz).astype(BF16)
        dg_ref[...] += jnp.sum(dy * xh, axis=0, keepdims=True)
        db_ref[...] += jnp.sum(dy, axis=0, keepdims=True)

    row = lambda i: (i, 0)
    fixed = lambda i: (0, 0)
    return _call(
        body, exch, name=name, grid=(t // tm,),
        out_shape=[jax.ShapeDtypeStruct((t, d), F32), jax.ShapeDtypeStruct((t, d), BF16),
                   jax.ShapeDtypeStruct((1, d), F32), jax.ShapeDtypeStruct((1, d), F32)],
        in_specs=[pl.BlockSpec((tm, d), row), pl.BlockSpec((tm, d), row), pl.BlockSpec((1, d), fixed)],
        out_specs=[pl.BlockSpec((tm, d), row), pl.BlockSpec((tm, d), row), pl.BlockSpec((1, d), fixed),
                   pl.BlockSpec((1, d), fixed)],
        semantics=("arbitrary",),
    )(z, dy, ln_g)


CONV_ROWS = 32
SUBLANES = 8


def _fill_shifted(ext, shifted):
    rows = ext.shape[0] - SUBLANES
    for s in range(1, SUBLANES):
        for r in range(0, rows, CONV_ROWS):
            n = min(CONV_ROWS, rows - r)
            shifted[s - 1, r:r + n, :] = ext[r + s:r + s + n, :]


def _window(ext, shifted, lo, n):
    s = lo % SUBLANES
    return ext[lo:lo + n, :] if s == 0 else shifted[s - 1, lo - s:lo - s + n, :]


def _mixer_fwd(proj, conv_w, conv_b, cln_g, cln_b, sln_g, sln_b, sg_wm, sg_bb, name, exch=()):
    t = proj.shape[0]
    tm = _tile(t, 256, CHUNK)
    hb = tm // HALO
    nc = tm // CHUNK
    ch = CONV_CH

    def body(av_ref, ag_ref, bu_ref, bv_ref, hv_ref, hg_ref, cw_ref, cb_ref, lg_ref, lb_ref, sg_ref, sb_ref,
             w_ref, bb_ref, y_ref, yt_ref, c_ref, ext, ext_s):
        i = pl.program_id(0)
        halo = hv_ref[...] * _sigmoid(hg_ref[...])
        ext[0:HALO, :] = jnp.where(i > 0, halo, 0.0)
        ext[HALO:HALO + tm, :] = av_ref[...] * _sigmoid(ag_ref[...])
        _fill_shifted(ext, ext_s)
        for r in range(0, tm, CONV_ROWS):
            acc = jnp.zeros((CONV_ROWS, ch), F32) + cb_ref[...]
            for k in range(CONV_TAPS):
                lo = r + k + HALO - (CONV_TAPS - 1)
                acc = acc + cw_ref[k:k + 1, :] * _window(ext, ext_s, lo, CONV_ROWS)
            c_ref[r:r + CONV_ROWS, :] = acc
        a = _ln(c_ref[...], lg_ref[...], lb_ref[...])
        ya = a * _sigmoid(a)
        y_ref[:, 0:ch] = ya.astype(BF16)
        yt_ref[0:ch, :] = ya.T.astype(BF16)
        for h in range(HEADS):
            sl = slice(h * HEAD_DIM, (h + 1) * HEAD_DIM)
            u, _ = _gelu_and_grad(bu_ref[:, sl])
            v, _ = _gelu_and_grad(bv_ref[:, sl])
            vn = _ln(v, sg_ref[h:h + 1, :], sb_ref[h:h + 1, :])
            vn3 = vn.astype(BF16).reshape(nc, CHUNK, HEAD_DIM)
            wb = jnp.broadcast_to(w_ref[h][None], (nc, CHUNK, CHUNK))
            mixed = jnp.einsum("cts,csd->ctd", wb, vn3, preferred_element_type=F32) + bb_ref[h][None]
            yb = u * mixed.reshape(tm, HEAD_DIM)
            y_ref[:, ch + h * HEAD_DIM:ch + (h + 1) * HEAD_DIM] = yb.astype(BF16)
            yt_ref[ch + h * HEAD_DIM:ch + (h + 1) * HEAD_DIM, :] = yb.T.astype(BF16)

    col = lambda cidx: (lambda i: (i, cidx))
    prev = lambda cidx: (lambda i: (jnp.maximum(i * hb - 1, 0), cidx))
    fix2 = lambda i: (0, 0)
    fix3 = lambda i: (0, 0, 0)
    return _call(
        body, exch, name=name, grid=(t // tm,),
        out_shape=[jax.ShapeDtypeStruct((t, 2 * ch), BF16), jax.ShapeDtypeStruct((2 * ch, t), BF16),
                   jax.ShapeDtypeStruct((t, ch), F32)],
        in_specs=[pl.BlockSpec((tm, ch), col(0)), pl.BlockSpec((tm, ch), col(1)), pl.BlockSpec((tm, ch), col(2)),
                  pl.BlockSpec((tm, ch), col(3)), pl.BlockSpec((HALO, ch), prev(0)), pl.BlockSpec((HALO, ch), prev(1)),
                  pl.BlockSpec((CONV_TAPS, ch), fix2), pl.BlockSpec((1, ch), fix2), pl.BlockSpec((1, ch), fix2),
                  pl.BlockSpec((1, ch), fix2), pl.BlockSpec((HEADS, HEAD_DIM), fix2), pl.BlockSpec((HEADS, HEAD_DIM), fix2),
                  pl.BlockSpec((HEADS, CHUNK, CHUNK), fix3), pl.BlockSpec((HEADS, CHUNK, HEAD_DIM), fix3)],
        out_specs=[pl.BlockSpec((tm, 2 * ch), lambda i: (i, 0)), pl.BlockSpec((2 * ch, tm), lambda i: (0, i)),
                   pl.BlockSpec((tm, ch), lambda i: (i, 0))],
        scratch_shapes=[pltpu.VMEM((HALO + tm, ch), F32), pltpu.VMEM((SUBLANES - 1, HALO + tm, ch), F32)],
        semantics=("parallel",),
    )(proj, proj, proj, proj, proj, proj, conv_w, conv_b, cln_g, cln_b, sln_g, sln_b, sg_wm, sg_bb)


def _mixer_bwd(proj, conv_c, dy, conv_w, cln_g, cln_b, sln_g, sln_b, sg_wm, sg_wmt, sg_bb, name, exch=()):
    t = proj.shape[0]
    tm = _tile(t, 256, CHUNK)
    hb = tm // HALO
    nc = tm // CHUNK
    nt = t // tm
    ch = CONV_CH
    last_halo = t // HALO - 1

    def body(av_ref, ag_ref, bu_ref, bv_ref, hv_ref, hg_ref, c_ref, cn_ref, dya_ref, dyan_ref, dyb_ref,
             cw_ref, lg_ref, lb_ref, sg_ref, sb_ref, w_ref, wt_ref, bb_ref,
             dp_ref, dcw_ref, dcb_ref, dlg_ref, dlb_ref, dsg_ref, dsb_ref, dw_ref, dbs_ref,
             ext_h, ext_dc, ext_hs, ext_dcs, acc_cw):
        i = pl.program_id(0)

        @pl.when(i == 0)
        def _():
            acc_cw[...] = jnp.zeros_like(acc_cw)
            for ref in (dcb_ref, dlg_ref, dlb_ref, dsg_ref, dsb_ref, dw_ref, dbs_ref):
                ref[...] = jnp.zeros_like(ref)

        lg = lg_ref[...]
        lb = lb_ref[...]

        def conv_ln_bwd(c, dya):
            xh, rstd = _ln_stats(c)
            a = xh * lg + lb
            da = dya * _silu_grad(a)
            return _ln_bwd(da * lg, xh, rstd), da, xh

        fold = lambda v: jnp.sum(v.reshape(CONV_ROWS // SUBLANES, SUBLANES, ch), axis=0)
        s_lg = s_lb = s_cb = jnp.zeros((SUBLANES, ch), F32)
        for r in range(0, tm, CONV_ROWS):
            dc, da, xh = conv_ln_bwd(c_ref[r:r + CONV_ROWS, :], dya_ref[r:r + CONV_ROWS, :])
            ext_dc[r:r + CONV_ROWS, :] = dc
            s_lg, s_lb, s_cb = s_lg + fold(da * xh), s_lb + fold(da), s_cb + fold(dc)
        dlg_ref[...] += jnp.sum(s_lg, axis=0, keepdims=True)
        dlb_ref[...] += jnp.sum(s_lb, axis=0, keepdims=True)
        dcb_ref[...] += jnp.sum(s_cb, axis=0, keepdims=True)
        dcn, _, _ = conv_ln_bwd(cn_ref[...], dyan_ref[...])
        ext_dc[tm:tm + HALO, :] = jnp.where(i < nt - 1, dcn, 0.0)
        halo = hv_ref[...] * _sigmoid(hg_ref[...])
        ext_h[0:HALO, :] = jnp.where(i > 0, halo, 0.0)
        ext_h[HALO:HALO + tm, :] = av_ref[...] * _sigmoid(ag_ref[...])
        _fill_shifted(ext_h, ext_hs)
        _fill_shifted(ext_dc, ext_dcs)
        for r in range(0, tm, CONV_ROWS):
            dcr = ext_dc[r:r + CONV_ROWS, :]
            acc = jnp.zeros((CONV_ROWS, ch), F32)
            for k in range(CONV_TAPS):
                lo = r + k + HALO - (CONV_TAPS - 1)
                prod = dcr * _window(ext_h, ext_hs, lo, CONV_ROWS)
                acc_cw[k] += jnp.sum(prod.reshape(CONV_ROWS // 8, 8, ch), axis=0)
                hi = r + (CONV_TAPS - 1) - k
                acc = acc + cw_ref[k:k + 1, :] * _window(ext_dc, ext_dcs, hi, CONV_ROWS)
            sg_r = _sigmoid(ag_ref[r:r + CONV_ROWS, :])
            av_r = av_ref[r:r + CONV_ROWS, :]
            dp_ref[r:r + CONV_ROWS, 0:ch] = (acc * sg_r).astype(BF16)
            dp_ref[r:r + CONV_ROWS, ch:2 * ch] = (acc * av_r * sg_r * (1.0 - sg_r)).astype(BF16)

        @pl.when(i == nt - 1)
        def _():
            dcw_ref[...] = jnp.sum(acc_cw[...], axis=1)

        tril = (lax.broadcasted_iota(jnp.int32, (CHUNK, CHUNK), 0)
                >= lax.broadcasted_iota(jnp.int32, (CHUNK, CHUNK), 1)).astype(F32)
        for h in range(HEADS):
            sl = slice(h * HEAD_DIM, (h + 1) * HEAD_DIM)
            u, du_dx = _gelu_and_grad(bu_ref[:, sl])
            v, dv_dx = _gelu_and_grad(bv_ref[:, sl])
            xhv, rstdv = _ln_stats(v)
            gh = sg_ref[h:h + 1, :]
            vn3 = (xhv * gh + sb_ref[h:h + 1, :]).astype(BF16).reshape(nc, CHUNK, HEAD_DIM)
            wb = jnp.broadcast_to(w_ref[h][None], (nc, CHUNK, CHUNK))
            mixed = jnp.einsum("cts,csd->ctd", wb, vn3, preferred_element_type=F32) + bb_ref[h][None]
            dyb = dyb_ref[:, sl]
            d_u = dyb * mixed.reshape(tm, HEAD_DIM)
            dm = dyb * u
            dm3 = dm.reshape(nc, CHUNK, HEAD_DIM)
            dbs_ref[h:h + 1, :] += jnp.sum(jnp.sum(dm3, axis=0).T, axis=0, keepdims=True)
            dm3b = dm3.astype(BF16)
            dw_h = jnp.sum(jnp.einsum("ctd,csd->cts", dm3b, vn3, preferred_element_type=F32), axis=0)
            dw_ref[h] += dw_h * tril
            wtb = jnp.broadcast_to(wt_ref[h][None], (nc, CHUNK, CHUNK))
            d_vn = jnp.einsum("cst,ctd->csd", wtb, dm3b, preferred_element_type=F32).reshape(tm, HEAD_DIM)
            dsg_ref[h:h + 1, :] += jnp.sum(d_vn * xhv, axis=0, keepdims=True)
            dsb_ref[h:h + 1, :] += jnp.sum(d_vn, axis=0, keepdims=True)
            dv = _ln_bwd(d_vn * gh, xhv, rstdv)
            dp_ref[:, 2 * ch + h * HEAD_DIM:2 * ch + (h + 1) * HEAD_DIM] = (d_u * du_dx).astype(BF16)
            dp_ref[:, 3 * ch + h * HEAD_DIM:3 * ch + (h + 1) * HEAD_DIM] = (dv * dv_dx).astype(BF16)

    col = lambda cidx: (lambda i: (i, cidx))
    prev = lambda cidx: (lambda i: (jnp.maximum(i * hb - 1, 0), cidx))
    nxt = lambda i: (jnp.minimum((i + 1) * hb, last_halo), 0)
    fix2 = lambda i: (0, 0)
    fix3 = lambda i: (0, 0, 0)
    out_shape = [jax.ShapeDtypeStruct((t, 4 * ch), BF16), jax.ShapeDtypeStruct((CONV_TAPS, ch), F32),
                 jax.ShapeDtypeStruct((1, ch), F32), jax.ShapeDtypeStruct((1, ch), F32), jax.ShapeDtypeStruct((1, ch), F32),
                 jax.ShapeDtypeStruct((HEADS, HEAD_DIM), F32), jax.ShapeDtypeStruct((HEADS, HEAD_DIM), F32),
                 jax.ShapeDtypeStruct((HEADS, CHUNK, CHUNK), F32), jax.ShapeDtypeStruct((HEADS, CHUNK), F32)]
    out_specs = [pl.BlockSpec((tm, 4 * ch), lambda i: (i, 0)), pl.BlockSpec((CONV_TAPS, ch), fix2),
                 pl.BlockSpec((1, ch), fix2), pl.BlockSpec((1, ch), fix2), pl.BlockSpec((1, ch), fix2),
                 pl.BlockSpec((HEADS, HEAD_DIM), fix2), pl.BlockSpec((HEADS, HEAD_DIM), fix2),
                 pl.BlockSpec((HEADS, CHUNK, CHUNK), fix3), pl.BlockSpec((HEADS, CHUNK), fix2)]
    in_specs = [pl.BlockSpec((tm, ch), col(0)), pl.BlockSpec((tm, ch), col(1)), pl.BlockSpec((tm, ch), col(2)),
                pl.BlockSpec((tm, ch), col(3)), pl.BlockSpec((HALO, ch), prev(0)), pl.BlockSpec((HALO, ch), prev(1)),
                pl.BlockSpec((tm, ch), col(0)), pl.BlockSpec((HALO, ch), nxt),
                pl.BlockSpec((tm, ch), col(0)), pl.BlockSpec((HALO, ch), nxt), pl.BlockSpec((tm, ch), col(1)),
                pl.BlockSpec((CONV_TAPS, ch), fix2), pl.BlockSpec((1, ch), fix2), pl.BlockSpec((1, ch), fix2),
                pl.BlockSpec((HEADS, HEAD_DIM), fix2), pl.BlockSpec((HEADS, HEAD_DIM), fix2),
                pl.BlockSpec((HEADS, CHUNK, CHUNK), fix3), pl.BlockSpec((HEADS, CHUNK, CHUNK), fix3),
                pl.BlockSpec((HEADS, CHUNK, HEAD_DIM), fix3)]
    return _call(
        body, exch, name=name, grid=(nt,), out_shape=out_shape, in_specs=in_specs, out_specs=out_specs,
        scratch_shapes=[pltpu.VMEM((HALO + tm, ch), F32), pltpu.VMEM((tm + HALO, ch), F32),
                        pltpu.VMEM((SUBLANES - 1, HALO + tm, ch), F32), pltpu.VMEM((SUBLANES - 1, tm + HALO, ch), F32),
                        pltpu.VMEM((CONV_TAPS, 8, ch), F32)],
        semantics=("arbitrary",),
    )(proj, proj, proj, proj, proj, proj, conv_c, conv_c, dy, dy, dy,
      conv_w, cln_g, cln_b, sln_g, sln_b, sg_wm, sg_wmt, sg_bb)


def _pair_sum(parts, from_sibling, core_chip, name):
    _, r, cc = parts.shape
    tr = _tile(r, max(16, (1 << 20) // (2 * cc)), 16)

    def body(cc_ref, p_ref, s_ref, o_ref, own_ref):
        q = (p_ref[...].astype(F32) + s_ref[...].astype(F32)).astype(BF16)
        o_ref[...] = q

        @pl.when(pl.program_id(1) == cc_ref[1])
        def _():
            own_ref[...] = q[0]

    grid_spec = pltpu.PrefetchScalarGridSpec(
        num_scalar_prefetch=1, grid=(r // tr, 4),
        in_specs=[pl.BlockSpec((1, tr, cc), lambda i, j, cc_ref: (2 * j + cc_ref[0], i, 0)),
                  pl.BlockSpec((1, tr, cc), lambda i, j, cc_ref: (j, i, 0))],
        out_specs=[pl.BlockSpec((1, tr, cc), lambda i, j, cc_ref: (j, i, 0)),
                   pl.BlockSpec((tr, cc), lambda i, j, cc_ref: (i, 0))])
    return pl.pallas_call(
        body, name=name, grid_spec=grid_spec,
        out_shape=[jax.ShapeDtypeStruct((4, r, cc), BF16), jax.ShapeDtypeStruct((r, cc), BF16)],
        compiler_params=_cparams("parallel", "arbitrary"),
    )(core_chip, parts, from_sibling)


def _adamw_math(w, g, m, v):
    m = ADAM_B1 * m + (1.0 - ADAM_B1) * g
    v = ADAM_B2 * v + (1.0 - ADAM_B2) * (g * g)
    m_hat = m / (1.0 - ADAM_B1 ** ADAM_STEP)
    v_hat = v / (1.0 - ADAM_B2 ** ADAM_STEP)
    delta = -ADAM_LR * (m_hat / (jnp.sqrt(v_hat) + ADAM_EPS) + ADAM_WD * w)
    return delta, m, v


def _adamw_tile(in_refs, out_refs):
    w_ref, m_ref, v_ref, q_ref, o_ref = in_refs
    g = q_ref[...].astype(F32)
    for k in range(3):
        g = g + o_ref[k].astype(F32)
    d, mm, vv = _adamw_math(w_ref[...], g, m_ref[...], v_ref[...])
    for ref, val in zip(out_refs, (g, d, mm, vv)):
        ref[...] = val


def _adamw_side(w, m, v, chip_part, from_chips, max_tiles):
    r, cc = w.shape
    n = max(k for k in range(1, max_tiles + 1) if r % k == 0 and (r // k) % 16 == 0)
    tr = r // n
    row = ((tr, cc), lambda s: (s, 0))
    return _Side([w, m, v, chip_part, from_chips], [row, row, row, row, ((3, tr, cc), lambda s: (0, s, 0))],
                 [jax.ShapeDtypeStruct((r, cc), F32)] * 4, [row] * 4, n, _adamw_tile)


def _adamw_sharded(w, m, v, chip_part, from_chips, name):
    r, cc = w.shape
    tr = _tile(r, max(16, (1 << 19) // (4 * cc) * 2), 16)

    def body(*refs):
        _adamw_tile(refs[:5], refs[5:])

    row = pl.BlockSpec((tr, cc), lambda i: (i, 0))
    return pl.pallas_call(
        body, name=name, grid=(r // tr,), out_shape=[jax.ShapeDtypeStruct((r, cc), F32)] * 4,
        in_specs=[row, row, row, row, pl.BlockSpec((3, tr, cc), lambda i: (0, i, 0))], out_specs=[row] * 4,
        compiler_params=_cparams("parallel"),
    )(w, m, v, chip_part, from_chips)


def _adamw_small(w, g, m, v, name):
    r, cc = w.shape

    def body(w_ref, g_ref, m_ref, v_ref, d_out, m_out, v_out):
        d, mm, vv = _adamw_math(w_ref[...], g_ref[...], m_ref[...], v_ref[...])
        d_out[...] = d
        m_out[...] = mm
        v_out[...] = vv

    full = pl.BlockSpec((r, cc), lambda i: (0, 0))
    return pl.pallas_call(
        body, name=name, grid=(1,), out_shape=[jax.ShapeDtypeStruct((r, cc), F32)] * 3,
        in_specs=[full] * 4, out_specs=[full] * 3, compiler_params=_cparams("arbitrary"),
    )(w, g, m, v)


SMALL = ("ln1_g", "ln1_b", "conv_b", "conv_ln_g", "conv_ln_b", "sg_ln_g", "sg_ln_b", "sg_w", "sg_b",
         "ln2_g", "ln2_b", "ln3_g", "ln3_b")
ORDER = ("ffn1_w_gate_up", "ffn1_w_down", "ln1_g", "ln1_b", "mix_w_in", "conv_w", "conv_b", "conv_ln_g", "conv_ln_b",
         "sg_ln_g", "sg_ln_b", "sg_w", "sg_b", "mix_w_out", "ln2_g", "ln2_b", "ffn2_w_gate_up", "ffn2_w_down",
         "ln3_g", "ln3_b")


def _rows128(a):
    return a.reshape(-1, 128)


def kernel(x, ffn1_w_gate_up, ffn1_w_down, ln1_g, ln1_b, mix_w_in, conv_w, conv_b, conv_ln_g, conv_ln_b, sg_ln_g, sg_ln_b, sg_w, sg_b, mix_w_out, ln2_g, ln2_b, ffn2_w_gate_up, ffn2_w_down, ln3_g, ln3_b, loss_target, m_ffn1_w_gate_up, m_ffn1_w_down, m_ln1_g, m_ln1_b, m_mix_w_in, m_conv_w, m_conv_b, m_conv_ln_g, m_conv_ln_b, m_sg_ln_g, m_sg_ln_b, m_sg_w, m_sg_b, m_mix_w_out, m_ln2_g, m_ln2_b, m_ffn2_w_gate_up, m_ffn2_w_down, m_ln3_g, m_ln3_b, v_ffn1_w_gate_up, v_ffn1_w_down, v_ln1_g, v_ln1_b, v_mix_w_in, v_conv_w, v_conv_b, v_conv_ln_g, v_conv_ln_b, v_sg_ln_g, v_sg_ln_b, v_sg_w, v_sg_b, v_mix_w_out, v_ln2_g, v_ln2_b, v_ffn2_w_gate_up, v_ffn2_w_down, v_ln3_g, v_ln3_b):
    args = dict(locals())
    w = {n: args[n][0] for n in ORDER}
    mom = {n: args["m_" + n][0] for n in ORDER}
    var = {n: args["v_" + n][0] for n in ORDER}
    x0 = x[0]
    target = loss_target[0]
    t, d = x0.shape
    my_x, my_y, my_c = lax.axis_index("x"), lax.axis_index("y"), lax.axis_index("c")
    my_chip = (2 * my_x + my_y).astype(jnp.int32).reshape(1)
    my_core = my_c.astype(jnp.int32).reshape(1)
    me = 4 * my_x + 2 * my_y + my_c

    big = ("ffn1_w_gate_up", "ffn1_w_down", "mix_w_in", "mix_w_out", "ffn2_w_gate_up", "ffn2_w_down")
    sh = {n: w[n].astype(BF16) for n in big}
    f2s = sh["ffn2_w_gate_up"].shape[1]
    (x0t, x0b), _ = _transpose_bf16(x0, "x0_transpose", with_copy=True)
    order = jnp.stack([4 * p[0] + 2 * p[1] + p[2] for p in _visit_order(my_x, my_y, my_c)]).astype(jnp.int32)
    gu1, (wgu1, wd1, conv_w_all) = _gather_and_gate_up(
        x0b, [sh["ffn1_w_gate_up"], sh["ffn1_w_down"], w["conv_w"]], [True, True, False], order, "ffn1_gate_up_fwd")
    wd1 = wd1.reshape(-1, d)
    conv_w_full = jnp.transpose(conv_w_all, (1, 0, 2)).reshape(CONV_TAPS, CONV_CH)
    tril = jnp.tril(jnp.ones((CHUNK, CHUNK), F32))
    sg_wm = w["sg_w"] * tril
    sg_wm_b = sg_wm.astype(BF16)
    sg_wmt_b = jnp.swapaxes(sg_wm, 1, 2).astype(BF16)
    sg_bb = jnp.broadcast_to(w["sg_b"][:, :, None], (HEADS, CHUNK, HEAD_DIM))
    row = lambda a: a.reshape(1, -1)

    (h1t, z1, x1), ((g_in, g_out),) = _ffn_down_fwd(
        gu1, x0, wd1, row(w["ln1_g"]), row(w["ln1_b"]), "ffn1_down_fwd",
        exch=[_gather_first([sh["mix_w_in"], sh["mix_w_out"]], [True, False])])
    in_cols = sh["mix_w_in"].shape[1]
    x1t, ((w_in, w_out),) = _transpose_bf16(
        x1, "x1_transpose", exch=[_gather_forward([g_in, g_out], [True, False], [in_cols, None])])
    w_out = w_out.reshape(-1, d)
    top, bottom = (0, d // 2), (d // 2, d // 2)
    gu2 = [sh["ffn2_w_gate_up"]]
    proj, ((g_gu2,),) = _mix_in_proj(x1, w_in, "mix_in_fwd", exch=[_gather_first(gu2, [True], rows=top)])
    (y, yt, conv_c), ((g_gu2,),) = _mixer_fwd(
        proj, conv_w_full, row(w["conv_b"]), row(w["conv_ln_g"]), row(w["conv_ln_b"]),
        w["sg_ln_g"], w["sg_ln_b"], sg_wm_b, sg_bb, "mixer_fwd",
        exch=[_both(_gather_first(gu2, [True], rows=bottom, into=[g_gu2]),
                    _gather_forward([g_gu2], [True], [f2s], rows=top))])
    (z2, x2, x2t), ((wgu2,), (g_d2,)) = _mix_out_fwd(
        y, w_out, x1, row(w["ln2_g"]), row(w["ln2_b"]), "mix_out_fwd",
        exch=[_gather_forward([g_gu2], [True], [f2s], rows=bottom), _gather_first([sh["ffn2_w_down"]], [False])])
    (wd2,) = _exchange_alone(_gather_forward([g_d2], [False], [None]), "ffn2_down_gather_forward")
    wd2 = wd2.reshape(-1, d)
    (g2, u2, h2t, z3), _ = _ffn_fwd(x2, wgu2, wd2, row(w["ln3_g"]), row(w["ln3_b"]), "ffn2_fwd", with_ln=False)

    f = wd1.shape[0]
    dn = _tile(d, 1024, 128)
    grads = {}
    core_chip = jnp.concatenate([my_core, my_chip])
    pair = lambda p, s, label: _pair_sum(p, s, core_chip, "pair_sum_" + label)
    adamw = lambda n, own, got, steps: _adamw_side(w[n], mom[n], var[n], own, got, steps)
    m_tiles = d // _tile(d, 512, 16)
    out = {}
    dz3, do2, grads["ln3_g"], grads["ln3_b"], loss_tile = _loss_ln_bwd(
        z3, target, row(w["ln3_g"]), row(w["ln3_b"]), 0.5, "loss_ln3_bwd")
    p_d2, _ = _weight_grad(h2t, do2, dn, 512, "ffn2_dw_down")
    p_d2 = p_d2.reshape(N_DEV, f // N_DEV, d)
    (dg2, du2, dx2), ((s_d2,),) = _ffn_bwd(dz3, do2, g2, u2, wgu2, wd2, "ffn2_bwd", exch=[_rs_sibling([p_d2])])
    q_d2, own_d2 = pair(p_d2, s_d2, "ffn2_down")
    p_gu2, ((r_d2,),) = _weight_grad(x2t, dg2, f2s, 512, "ffn2_dw_gate", blocks=N_DEV, exch=[_rs_chips([q_d2])])
    p_gu2, _ = _weight_grad(x2t, du2, f2s, 512, "ffn2_dw_up", blocks=N_DEV, block_offset=4, into=p_gu2)
    (dz2, dz2b, grads["ln2_g"], grads["ln2_b"]), ((s_gu2,),) = _ln_bwd_call(
        z2, dx2, row(w["ln2_g"]), 1.0, "ln2_bwd", exch=[_rs_sibling([p_gu2])])
    q_gu2, own_gu2 = pair(p_gu2, s_gu2, "ffn2_gate_up")
    dy = _mix_out_bwd(dz2b, w_out, "mix_out_bwd")
    p_out, _ = _weight_grad(yt, dz2b, dn, 512, "mix_out_dw")
    p_out = p_out.reshape(N_DEV, -1, d)
    (dproj, grads["conv_w"], grads["conv_b"], grads["conv_ln_g"], grads["conv_ln_b"], grads["sg_ln_g"],
     grads["sg_ln_b"], grads["sg_w"], grads["sg_b"]), ((r_gu2,),) = _mixer_bwd(
        proj, conv_c, dy, conv_w_full, row(w["conv_ln_g"]), row(w["conv_ln_b"]), w["sg_ln_g"], w["sg_ln_b"],
        sg_wm_b, sg_wmt_b, sg_bb, "mixer_bwd", exch=[_rs_chips([q_gu2], rows=top)])
    dx1, ((s_out,), (r_gu2,)) = _mix_in_bwd(
        dproj, w_in, dz2, "mix_in_bwd", exch=[_rs_sibling([p_out]), _rs_chips([q_gu2], rows=bottom, into=[r_gu2])])
    p_in, (out["ffn2_w_gate_up"], out["ffn2_w_down"]) = _weight_grad(
        x1t, dproj, in_cols, 512, "mix_in_dw", blocks=N_DEV,
        exch=[adamw("ffn2_w_gate_up", own_gu2, r_gu2, N_DEV * m_tiles), adamw("ffn2_w_down", own_d2, r_d2, N_DEV * m_tiles)])
    (dz1, do1, grads["ln1_g"], grads["ln1_b"]), ((s_in,),) = _ln_bwd_call(
        z1, dx1, row(w["ln1_g"]), 0.5, "ln1_bwd", exch=[_rs_sibling([p_in])])
    q_out, own_out = pair(p_out, s_out, "mix_out")
    q_in, own_in = pair(p_in, s_in, "mix_in")
    small_parts = [_rows128(grads[n]) for n in SMALL]
    packed = jnp.concatenate(small_parts + [_rows128(grads["conv_w"]), loss_tile], axis=0)
    p_d1, ((r_in,),) = _weight_grad(h1t, do1, dn, 512, "ffn1_dw_down", exch=[_rs_chips([q_in])])
    p_d1 = p_d1.reshape(N_DEV, f // N_DEV, d)
    (dg1, du1), ((s_d1,), (r_out,), (small_all,)) = _ffn_bwd_act(
        do1, gu1, wd1, "ffn1_bwd_act",
        exch=[_rs_sibling([p_d1]), _rs_chips([q_out]), _small_gather(packed)])
    q_d1, own_d1 = pair(p_d1, s_d1, "ffn1_down")
    p_gu1, ((r_d1,),) = _weight_grad(x0t, dg1, f2s, 512, "ffn1_dw_gate", blocks=N_DEV, exch=[_rs_chips([q_d1])])
    p_gu1, (out["mix_w_in"], out["mix_w_out"]) = _weight_grad(
        x0t, du1, f2s, 512, "ffn1_dw_up", blocks=N_DEV, block_offset=4, into=p_gu1,
        exch=[adamw("mix_w_in", own_in, r_in, 4 * m_tiles), adamw("mix_w_out", own_out, r_out, 4 * m_tiles)])
    (s_gu1,) = _exchange_alone(_rs_sibling([p_gu1]), "ffn1_gate_up_sibling_exchange")
    q_gu1, own_gu1 = pair(p_gu1, s_gu1, "ffn1_gate_up")
    (grad_x,), ((r_gu1,),) = _ffn_bwd_dx(dz1, dg1, du1, wgu1, "ffn1_bwd_dx", exch=[_rs_chips([q_gu1])])
    for n, own, got in (("ffn1_w_down", own_d1, r_d1), ("ffn1_w_gate_up", own_gu1, r_gu1)):
        out[n] = _adamw_sharded(w[n], mom[n], var[n], own, got, "adamw_" + n)

    cw_rows = CONV_TAPS * CONV_CH // 128
    total = _sum_over_devices(small_all)
    offs = [0]
    for p in small_parts:
        offs.append(offs[-1] + p.shape[0])
    n_small = offs[-1]
    loss = total[n_small + cw_rows, 0]
    g_conv_w = lax.dynamic_slice_in_dim(total[n_small:n_small + cw_rows].reshape(CONV_TAPS, CONV_CH),
                                        me * (CONV_CH // N_DEV), CONV_CH // N_DEV, axis=1)
    pad8 = lambda a: jnp.pad(a, ((0, -a.shape[0] % 8), (0, 0)))
    pack = lambda tree, cw: jnp.concatenate([_rows128(tree[n]) for n in SMALL] + [pad8(cw)], axis=0)
    g_pack = jnp.concatenate([total[:n_small], pad8(g_conv_w)], axis=0)
    d_pack, m_pack, v_pack = _adamw_small(pack(w, w["conv_w"]), g_pack, pack(mom, mom["conv_w"]),
                                          pack(var, var["conv_w"]), "adamw_small")
    for k, n in enumerate(SMALL):
        sl = slice(offs[k], offs[k + 1])
        shp = w[n].shape
        out[n] = (total[sl].reshape(shp), d_pack[sl].reshape(shp), m_pack[sl].reshape(shp), v_pack[sl].reshape(shp))
    sl = slice(n_small, n_small + CONV_TAPS)
    out["conv_w"] = (g_conv_w, d_pack[sl], m_pack[sl], v_pack[sl])

    lead = lambda a: a[None]
    res = [loss, grad_x[None]]
    for kind in range(4):
        res += [lead(out[n][kind]) for n in ORDER]
    return tuple(res)
```

```python
import functools
import math

import jax
import jax.numpy as jnp
from jax import lax
from jax.experimental import pallas as pl
from jax.experimental.pallas import tpu as pltpu

F32, BF16 = jnp.float32, jnp.bfloat16
MESH = pl.DeviceIdType.MESH
ANY = pl.BlockSpec(memory_space=pl.ANY)

N_DEV = 8
LN_EPS = 1e-5
ALPHA = 2.0 ** 0.25
CONV_CH = 1024
CONV_TAPS = 31
HALO = 32
HEADS = 8
HEAD_DIM = 128
CHUNK = 128
ADAM_LR, ADAM_B1, ADAM_B2, ADAM_EPS, ADAM_WD, ADAM_STEP = 0.001, 0.9, 0.999, 1e-08, 0.01, 10
V7X_VMEM_LIMIT = 56 * 2 ** 20

def _cparams(*sem):
    return pltpu.CompilerParams(dimension_semantics=sem, vmem_limit_bytes=V7X_VMEM_LIMIT)


def _tile(n, pref, mult):
    best = None
    for t in range(mult, min(n, pref) + 1, mult):
        if n % t == 0:
            best = t
    return best if best is not None else n


def _dot(a, b):
    return jnp.dot(a, b, preferred_element_type=F32)


def _dot_nt(a, b):
    return lax.dot_general(a, b, (((1,), (1,)), ((), ())), preferred_element_type=F32)


def _sigmoid(x):
    return 1.0 / (1.0 + jnp.exp(-x))


def _ln_stats(z):
    mu = jnp.mean(z, axis=-1, keepdims=True)
    zc = z - mu
    var = jnp.mean(zc * zc, axis=-1, keepdims=True)
    rstd = lax.rsqrt(var + LN_EPS)
    return zc * rstd, rstd


def _ln(z, g, b):
    xh, _ = _ln_stats(z)
    return xh * g + b


def _ln_bwd(dxh, xh, rstd):
    m1 = jnp.mean(dxh, axis=-1, keepdims=True)
    m2 = jnp.mean(dxh * xh, axis=-1, keepdims=True)
    return rstd * (dxh - m1 - xh * m2)


_GK = math.sqrt(2.0 / math.pi)
_GA = 0.044715


def _gelu_and_grad(x):
    x2 = x * x
    t = jnp.tanh(_GK * (x + _GA * x * x2))
    y = 0.5 * x * (1.0 + t)
    dy = 0.5 * (1.0 + t) + 0.5 * x * (1.0 - t * t) * (_GK * (1.0 + 3.0 * _GA * x2))
    return y, dy


def _silu_grad(a):
    s = _sigmoid(a)
    return s * (1.0 + a * (1.0 - s))


def _place():
    return lax.axis_index("x"), lax.axis_index("y"), lax.axis_index("c")


def _other_chips(x, y):
    return [(1 - x, y), (x, 1 - y), (1 - x, 1 - y)]


def _visit_order(x, y, c):
    chips = _other_chips(x, y)
    return [(x, y, c), (x, y, 1 - c), (*chips[0], c), (*chips[1], c), (*chips[0], 1 - c), (*chips[1], 1 - c),
            (*chips[2], c), (*chips[2], 1 - c)]


def _gather_and_gate_up(xb, shards, relayed, order, name):
    n = len(shards)
    N_COPIES = 10
    t, d = xb.shape
    cols = shards[0].shape[1]
    tm = _tile(t, 512, 128)
    ni = t // tm
    col_major = [True] + [False] * (n - 1)

    def body(order_ref, x_ref, *refs):
        srcs, gu_ref, dsts = refs[:n], refs[n], refs[n + 1:2 * n + 1]
        wbuf, send_sems, recv_sems, local_sems, load_sem = refs[2 * n + 1:]
        b, i = pl.program_id(0), pl.program_id(1)
        x, y, c = _place()
        me, sib = (x, y, c), (x, y, 1 - c)
        chips = _other_chips(x, y)

        near_x, near_y, far = chips

        def slot(w, p, band=None):
            half = shards[w].shape[0] // 2
            rows = None if band is None else (band * half, half)
            return _block_slot(dsts[w], col_major[w], shards[w].shape[1], p, rows)

        def copy(w, s, block, to, band=None, from_src=False):
            return pltpu.make_async_remote_copy(
                src_ref=srcs[w] if from_src else slot(w, block, band), dst_ref=slot(w, block, band),
                send_sem=send_sems.at[N_COPIES * w + s], recv_sem=recv_sems.at[N_COPIES * w + s],
                device_id=to, device_id_type=MESH)

        def own(w):
            return pltpu.make_async_copy(srcs[w], slot(w, me), local_sems.at[w])

        def sends(w):
            out = [copy(w, 0, me, sib, from_src=True), copy(w, 1, me, (*near_x, c), from_src=True),
                   copy(w, 2, me, (*near_y, c), from_src=True)]
            if not relayed[w]:
                out.append(copy(w, 3, me, (*far, c), from_src=True))
            return out

        def passed_on(w):
            out = [copy(w, 4, (*near_x, c), sib), copy(w, 5, (*near_y, c), sib)]
            if relayed[w]:
                out += [copy(w, 6, (*far, c), sib, band=0), copy(w, 9, (*far, c), sib, band=1),
                        copy(w, 7, (*near_x, c), (*near_y, c), band=0), copy(w, 8, (*near_y, c), (*near_x, c), band=1)]
            else:
                out.append(copy(w, 6, (*far, c), sib))
            return out

        def start_sends(w):
            own(w).start()
            for cp in sends(w):
                cp.start()

        def got_near_x(w):
            copy(w, 1, (*near_x, c), me).wait_recv()
            copy(w, 4, (*near_x, c), sib).start()
            if relayed[w]:
                copy(w, 7, (*near_x, c), (*near_y, c), band=0).start()

        def got_near_y(w):
            copy(w, 2, (*near_y, c), me).wait_recv()
            copy(w, 5, (*near_y, c), sib).start()
            if relayed[w]:
                copy(w, 8, (*near_y, c), (*near_x, c), band=1).start()

        def got_far(w):
            if relayed[w]:
                copy(w, 7, (*far, c), me, band=0).wait_recv()
                copy(w, 6, (*far, c), sib, band=0).start()
                copy(w, 8, (*far, c), me, band=1).wait_recv()
                copy(w, 9, (*far, c), sib, band=1).start()
            else:
                copy(w, 3, (*far, c), me).wait_recv()
                copy(w, 6, (*far, c), sib).start()

        def got_from_sibling(w, which):
            if which == 0:
                copy(w, 0, sib, me).wait_recv()
            elif which == 3 and relayed[w]:
                copy(w, 6, (*far, 1 - c), me, band=0).wait_recv()
                copy(w, 9, (*far, 1 - c), me, band=1).wait_recv()
            else:
                copy(w, 3 + which, (*chips[which - 1], 1 - c), me).wait_recv()

        others = range(1, n)

        def arrive(k):
            if k == 0:
                own(0).wait()
            elif k == 1:
                got_from_sibling(0, 0)
            elif k == 2:
                got_near_x(0)
            elif k == 3:
                got_near_y(0)
            elif k in (4, 5):
                got_from_sibling(0, k - 3)
            elif k == 6:
                got_far(0)
                for w in others:
                    start_sends(w)
            else:
                got_from_sibling(0, 3)

        @pl.when((b == 0) & (i == 0))
        def _():
            start_sends(0)

        for k in range(N_DEV):
            @pl.when((b == k) & (i == 0))
            def _(k=k):
                arrive(k)
                at = pl.multiple_of(order_ref[k] * cols, 128)
                load = pltpu.make_async_copy(dsts[0].at[:, pl.ds(at, cols)], wbuf, load_sem.at[0])
                load.start()
                load.wait()

        gu_ref[...] = _dot(x_ref[...], wbuf[...]).astype(BF16)

        @pl.when((b == N_DEV - 1) & (i == ni - 1))
        def _():
            for w in others:
                got_near_x(w)
                got_near_y(w)
                got_far(w)
            for w in others:
                for which in range(4):
                    got_from_sibling(w, which)
                own(w).wait()
            for w in range(n):
                for cp in sends(w) + passed_on(w):
                    cp.wait_send()

    grid_spec = pltpu.PrefetchScalarGridSpec(
        num_scalar_prefetch=1, grid=(N_DEV, ni),
        in_specs=[pl.BlockSpec((tm, d), lambda b, i, o: (i, 0))] + [ANY] * n,
        out_specs=[pl.BlockSpec((tm, cols), lambda b, i, o: (i, o[b]))] + [ANY] * n,
        scratch_shapes=[pltpu.VMEM((d, cols), BF16), pltpu.SemaphoreType.DMA((N_COPIES * n,)),
                        pltpu.SemaphoreType.DMA((N_COPIES * n,)), pltpu.SemaphoreType.DMA((n,)),
                        pltpu.SemaphoreType.DMA((1,))])
    res = pl.pallas_call(
        body, name=name, grid_spec=grid_spec,
        out_shape=[jax.ShapeDtypeStruct((t, N_DEV * cols), BF16)]
        + [_gathered_shape(s, cm) for s, cm in zip(shards, col_major)],
        compiler_params=_cparams("arbitrary", "arbitrary"),
    )(order, xb, *shards)
    return res[0], res[1:]


class _Exchange:
    def __init__(self, ins, io, new, n_sems, n_local, make):
        self.ins, self.io, self.new = list(ins), list(io), list(new)
        self.n_sems, self.n_local, self.make = n_sems, n_local, make


def _block_slot(ref, col_major, cols, place, rows=None):
    k = 4 * place[0] + 2 * place[1] + place[2]
    band = slice(None) if rows is None else pl.ds(rows[0], rows[1])
    if col_major:
        return ref.at[band, pl.ds(pl.multiple_of(k * cols, 128), cols)]
    return ref.at[k] if rows is None else ref.at[k, band]


def _gathered_shape(s, col_major):
    return jax.ShapeDtypeStruct((s.shape[0], N_DEV * s.shape[1]) if col_major else (N_DEV,) + s.shape, s.dtype)


def _gather_first(shards, col_major, rows=None, into=None):
    n = len(shards)
    new = [] if into is not None else [_gathered_shape(s, cm) for s, cm in zip(shards, col_major)]

    def make(in_refs, io_refs, new_refs, send_sems, recv_sems, local_sems, base=0, local_base=0):
        x, y, c = _place()
        targets = [(x, y, 1 - c)] + [(*chip, c) for chip in _other_chips(x, y)]
        gathered = io_refs if into is not None else new_refs
        copies = []
        for w in range(n):
            src = in_refs[w] if rows is None else in_refs[w].at[pl.ds(rows[0], rows[1])]
            slot = _block_slot(gathered[w], col_major[w], shards[w].shape[1], (x, y, c), rows)
            copies.append(pltpu.make_async_copy(src, slot, local_sems.at[local_base + w]))
            for s, to in enumerate(targets):
                copies.append(pltpu.make_async_remote_copy(
                    src_ref=src, dst_ref=slot, send_sem=send_sems.at[base + 4 * w + s],
                    recv_sem=recv_sems.at[base + 4 * w + s], device_id=to, device_id_type=MESH))
        return copies

    return _Exchange(shards, into or [], new, 4 * n, n, make)


def _gather_forward(gathered, col_major, cols, rows=None):
    n = len(gathered)

    def make(in_refs, io_refs, new_refs, send_sems, recv_sems, local_sems, base=0, local_base=0):
        x, y, c = _place()
        copies = []
        for w in range(n):
            for j, chip in enumerate(_other_chips(x, y)):
                slot = _block_slot(io_refs[w], col_major[w], cols[w], (*chip, c), rows)
                copies.append(pltpu.make_async_remote_copy(
                    src_ref=slot, dst_ref=slot, send_sem=send_sems.at[base + 3 * w + j],
                    recv_sem=recv_sems.at[base + 3 * w + j], device_id=(x, y, 1 - c), device_id_type=MESH))
        return copies

    return _Exchange([], gathered, [], 3 * n, 0, make)


def _both(a, b):
    def make(in_refs, io_refs, new_refs, send_sems, recv_sems, local_sems):
        na = len(a.ins)
        return (a.make(in_refs[:na], io_refs, [], send_sems, recv_sems, local_sems, 0, 0)
                + b.make(in_refs[na:], io_refs, [], send_sems, recv_sems, local_sems, a.n_sems, a.n_local))

    return _Exchange(a.ins + b.ins, a.io, [], a.n_sems + b.n_sems, a.n_local + b.n_local, make)


def _rs_sibling(parts):
    n = len(parts)

    def make(in_refs, io_refs, new_refs, send_sems, recv_sems, local_sems):
        x, y, c = _place()
        copies = []
        for w in range(n):
            for j in range(4):
                copies.append(pltpu.make_async_remote_copy(
                    src_ref=in_refs[w].at[2 * j + (1 - c)], dst_ref=new_refs[w].at[j],
                    send_sem=send_sems.at[4 * w + j], recv_sem=recv_sems.at[4 * w + j],
                    device_id=(x, y, 1 - c), device_id_type=MESH))
        return copies

    return _Exchange(parts, [], [jax.ShapeDtypeStruct((4,) + p.shape[1:], p.dtype) for p in parts], 4 * n, 0, make)


def _rs_chips(chip_parts, rows=None, into=None):
    n = len(chip_parts)
    band = slice(None) if rows is None else pl.ds(rows[0], rows[1])
    new = [] if into is not None else [jax.ShapeDtypeStruct((3,) + p.shape[1:], p.dtype) for p in chip_parts]

    def make(in_refs, io_refs, new_refs, send_sems, recv_sems, local_sems):
        x, y, c = _place()
        landing = io_refs if into is not None else new_refs
        copies = []
        for w in range(n):
            for rel, (px, py) in enumerate(_other_chips(x, y)):
                copies.append(pltpu.make_async_remote_copy(
                    src_ref=in_refs[w].at[2 * px + py, band], dst_ref=landing[w].at[rel, band],
                    send_sem=send_sems.at[3 * w + rel], recv_sem=recv_sems.at[3 * w + rel],
                    device_id=(px, py, c), device_id_type=MESH))
        return copies

    return _Exchange(chip_parts, into or [], new, 3 * n, 0, make)


class _Side:
    def __init__(self, ins, in_blocks, out_shapes, out_blocks, n_tiles, fn):
        self.ins, self.in_blocks, self.out_shapes, self.out_blocks = list(ins), in_blocks, list(out_shapes), out_blocks
        self.n_tiles, self.fn = n_tiles, fn


def _call(body, exch, *, name, grid, in_specs, out_specs, out_shape, scratch_shapes=(), semantics,
          input_output_aliases=None):
    exch = list(exch)
    in_specs, out_specs, out_shape = list(in_specs), list(out_specs), list(out_shape)
    scratch_shapes = list(scratch_shapes)
    if not exch:
        fn = pl.pallas_call(body, name=name, grid=grid, in_specs=in_specs, out_specs=out_specs, out_shape=out_shape,
                            scratch_shapes=scratch_shapes, input_output_aliases=input_output_aliases or {},
                            compiler_params=_cparams(*semantics))
        return lambda *args: (fn(*args), [])
    n_in, n_out, n_scr = len(in_specs), len(out_specs), len(scratch_shapes)
    aliases = dict(input_output_aliases or {})
    all_in, all_out_specs, all_out_shape, all_scr = list(in_specs), list(out_specs), list(out_shape), list(scratch_shapes)
    extra_args = []

    def step(idx):
        s = idx[0]
        for a in range(1, len(grid)):
            s = s * grid[a] + idx[a]
        return s

    def tile_spec(shape, where, n_tiles):
        return pl.BlockSpec(shape, lambda *idx: where(jnp.minimum(step(idx), n_tiles - 1)))

    for ex in exch:
        if isinstance(ex, _Side):
            all_in += [tile_spec(shape, where, ex.n_tiles) for shape, where in ex.in_blocks]
            extra_args += ex.ins
            all_out_specs += [tile_spec(shape, where, ex.n_tiles) for shape, where in ex.out_blocks]
            all_out_shape += ex.out_shapes
            continue
        for k, a in enumerate(ex.io):
            aliases[len(all_in) + len(ex.ins) + k] = len(all_out_specs) + k
        all_in += [ANY] * (len(ex.ins) + len(ex.io))
        extra_args += ex.ins + ex.io
        all_out_specs += [ANY] * (len(ex.io) + len(ex.new))
        all_out_shape += [jax.ShapeDtypeStruct(a.shape, a.dtype) for a in ex.io] + ex.new
        all_scr += [pltpu.SemaphoreType.DMA((ex.n_sems,)), pltpu.SemaphoreType.DMA((ex.n_sems,)),
                    pltpu.SemaphoreType.DMA((max(ex.n_local, 1),))]

    n_ins = [len(ex.ins) if isinstance(ex, _Side) else len(ex.ins) + len(ex.io) for ex in exch]
    n_outs = [len(ex.out_shapes) if isinstance(ex, _Side) else len(ex.io) + len(ex.new) for ex in exch]

    def wrapped(*refs):
        pos = n_in
        ex_in = []
        for k in n_ins:
            ex_in.append(refs[pos:pos + k])
            pos += k
        outs = refs[pos:pos + n_out]
        pos += n_out
        ex_out = []
        for k in n_outs:
            ex_out.append(refs[pos:pos + k])
            pos += k
        scr = refs[pos:pos + n_scr]
        pos += n_scr
        idx = [pl.program_id(a) for a in range(len(grid))]
        first = functools.reduce(jnp.logical_and, [i == 0 for i in idx])
        last = functools.reduce(jnp.logical_and, [i == g - 1 for i, g in zip(idx, grid)])

        def copies():
            out, at = [], pos
            for ex, ei, eo in zip(exch, ex_in, ex_out):
                if not isinstance(ex, _Side):
                    out += ex.make(ei[:len(ex.ins)], eo[:len(ex.io)], eo[len(ex.io):], *refs[at:at + 3])
                    at += 3
            return out

        @pl.when(first)
        def _():
            for cp in copies():
                cp.start()

        body(*refs[:n_in], *outs, *scr)
        for ex, ei, eo in zip(exch, ex_in, ex_out):
            if isinstance(ex, _Side):
                pl.when(step(idx) < ex.n_tiles)(functools.partial(ex.fn, ei, eo))

        @pl.when(last)
        def _():
            for cp in copies():
                cp.wait()

    fn = pl.pallas_call(wrapped, name=name, grid=grid, in_specs=all_in, out_specs=all_out_specs,
                        out_shape=all_out_shape, scratch_shapes=all_scr, input_output_aliases=aliases,
                        compiler_params=_cparams(*(["arbitrary"] * len(grid))))

    def run(*args):
        res = fn(*args, *extra_args)
        outs, pos, ex_res = res[:n_out], n_out, []
        for k in n_outs:
            ex_res.append(list(res[pos:pos + k]))
            pos += k
        return outs, ex_res

    return run


def _exchange_alone(ex, name):
    def body():
        pass

    _, res = _call(body, [ex], name=name, grid=(1,), in_specs=[], out_specs=[], out_shape=[], semantics=("arbitrary",))()
    return res[0]


def _small_gather(part):
    def make(in_refs, io_refs, new_refs, send_sems, recv_sems, local_sems):
        x, y, c = _place()
        slot = new_refs[0].at[4 * x + 2 * y + c]
        copies = [pltpu.make_async_copy(in_refs[0], slot, local_sems.at[0])]
        for d in range(1, N_DEV):
            peer = (1 - x if d & 4 else x, 1 - y if d & 2 else y, 1 - c if d & 1 else c)
            copies.append(pltpu.make_async_remote_copy(
                src_ref=in_refs[0], dst_ref=slot, send_sem=send_sems.at[d - 1], recv_sem=recv_sems.at[d - 1],
                device_id=peer, device_id_type=MESH))
        return copies

    return _Exchange([part], [], [jax.ShapeDtypeStruct((N_DEV,) + part.shape, part.dtype)], N_DEV - 1, 1, make)


def _sum_over_devices(parts):
    _, rows, lanes = parts.shape

    def body(p_ref, o_ref):
        acc = p_ref[0]
        for k in range(1, N_DEV):
            acc = acc + p_ref[k]
        o_ref[...] = acc

    return pl.pallas_call(
        body, name="small_grads_sum", grid=(1,), out_shape=jax.ShapeDtypeStruct((rows, lanes), F32),
        in_specs=[pl.BlockSpec((N_DEV, rows, lanes), lambda i: (0, 0, 0))],
        out_specs=pl.BlockSpec((rows, lanes), lambda i: (0, 0)),
        compiler_params=_cparams("arbitrary"),
    )(parts)


def _transpose_bf16(a, name, exch=(), with_copy=False):
    r, c = a.shape
    tr, tc = _tile(r, 512, 128), _tile(c, 512, 128)

    def body(a_ref, o_ref, *copy_ref):
        v = a_ref[...].astype(F32)
        o_ref[...] = v.T.astype(BF16)
        if with_copy:
            copy_ref[0][...] = v.astype(BF16)

    outs, ex = _call(
        body, exch, name=name, grid=(r // tr, c // tc),
        out_shape=[jax.ShapeDtypeStruct((c, r), BF16)] + [jax.ShapeDtypeStruct((r, c), BF16)] * with_copy,
        in_specs=[pl.BlockSpec((tr, tc), lambda i, j: (i, j))],
        out_specs=[pl.BlockSpec((tc, tr), lambda i, j: (j, i))] + [pl.BlockSpec((tr, tc), lambda i, j: (i, j))] * with_copy,
        semantics=("parallel", "parallel"),
    )(a)
    return (outs if with_copy else outs[0]), ex


def _ffn_fwd(x, wgu, wd, ln_g, ln_b, name, exch=(), with_ln=True):
    t, d = x.shape
    f = wd.shape[0]
    tm, tf = _tile(t, 512, 128), _tile(f, 512, 128)
    nf = f // tf

    def body(x_ref, wg_ref, wu_ref, wd_ref, g_ref, b_ref, go_ref, uo_ref, ht_ref, z_ref, *rest):
        xn_ref = rest[0] if with_ln else None
        xb, acc = rest[-2:]
        j = pl.program_id(1)

        @pl.when(j == 0)
        def _():
            xb[...] = x_ref[...].astype(BF16)
            acc[...] = jnp.zeros_like(acc)

        g = _dot(xb[...], wg_ref[...])
        u = _dot(xb[...], wu_ref[...])
        h = g * _sigmoid(g) * u
        go_ref[...] = g.astype(BF16)
        uo_ref[...] = u.astype(BF16)
        ht_ref[...] = h.T.astype(BF16)
        acc[...] += _dot(h.astype(BF16), wd_ref[...])

        @pl.when(j == nf - 1)
        def _():
            z = ALPHA * x_ref[...] + 0.5 * acc[...]
            z_ref[...] = z
            if with_ln:
                xn_ref[...] = _ln(z, g_ref[...], b_ref[...])

    row = lambda i, j: (i, 0)
    n_td = 2 if with_ln else 1
    return _call(
        body, exch, name=name, grid=(t // tm, nf),
        out_shape=[jax.ShapeDtypeStruct((t, f), BF16), jax.ShapeDtypeStruct((t, f), BF16),
                   jax.ShapeDtypeStruct((f, t), BF16)] + [jax.ShapeDtypeStruct((t, d), F32)] * n_td,
        in_specs=[pl.BlockSpec((tm, d), row),
                  pl.BlockSpec((d, tf), lambda i, j: (0, j)),
                  pl.BlockSpec((d, tf), lambda i, j: (0, j + nf)),
                  pl.BlockSpec((tf, d), lambda i, j: (j, 0)),
                  pl.BlockSpec((1, d), lambda i, j: (0, 0)),
                  pl.BlockSpec((1, d), lambda i, j: (0, 0))],
        out_specs=[pl.BlockSpec((tm, tf), lambda i, j: (i, j)), pl.BlockSpec((tm, tf), lambda i, j: (i, j)),
                   pl.BlockSpec((tf, tm), lambda i, j: (j, i))] + [pl.BlockSpec((tm, d), row)] * n_td,
        scratch_shapes=[pltpu.VMEM((tm, d), BF16), pltpu.VMEM((tm, d), F32)],
        semantics=("parallel", "arbitrary"),
    )(x, wgu, wgu, wd, ln_g, ln_b)


def _ffn_down_fwd(gu, x, wd, ln_g, ln_b, name, exch=()):
    t, d = x.shape
    f = wd.shape[0]
    tm, tf = _tile(t, 512, 128), _tile(f, 512, 128)
    nf = f // tf

    def body(g_ref, u_ref, wd_ref, x_ref, lg_ref, lb_ref, ht_ref, z_ref, xn_ref, acc):
        j = pl.program_id(1)

        @pl.when(j == 0)
        def _():
            acc[...] = jnp.zeros_like(acc)

        g = g_ref[...].astype(F32)
        h = g * _sigmoid(g) * u_ref[...].astype(F32)
        ht_ref[...] = h.T.astype(BF16)
        acc[...] += _dot(h.astype(BF16), wd_ref[...])

        @pl.when(j == nf - 1)
        def _():
            z = ALPHA * x_ref[...] + 0.5 * acc[...]
            z_ref[...] = z
            xn_ref[...] = _ln(z, lg_ref[...], lb_ref[...])

    row = lambda i, j: (i, 0)
    fixed = lambda i, j: (0, 0)
    return _call(
        body, exch, name=name, grid=(t // tm, nf),
        out_shape=[jax.ShapeDtypeStruct((f, t), BF16), jax.ShapeDtypeStruct((t, d), F32),
                   jax.ShapeDtypeStruct((t, d), F32)],
        in_specs=[pl.BlockSpec((tm, tf), lambda i, j: (i, j)), pl.BlockSpec((tm, tf), lambda i, j: (i, j + nf)),
                  pl.BlockSpec((tf, d), lambda i, j: (j, 0)), pl.BlockSpec((tm, d), row),
                  pl.BlockSpec((1, d), fixed), pl.BlockSpec((1, d), fixed)],
        out_specs=[pl.BlockSpec((tf, tm), lambda i, j: (j, i)), pl.BlockSpec((tm, d), row), pl.BlockSpec((tm, d), row)],
        scratch_shapes=[pltpu.VMEM((tm, d), F32)],
        semantics=("parallel", "arbitrary"),
    )(gu, gu, wd, x, ln_g, ln_b)


def _ffn_act_grads(dh, g_ref, u_ref):
    gg = g_ref[...].astype(F32)
    uu = u_ref[...].astype(F32)
    s = _sigmoid(gg)
    du = (dh * (gg * s)).astype(BF16)
    dg = (dh * uu * (s * (1.0 + gg * (1.0 - s)))).astype(BF16)
    return dg, du


def _ffn_bwd(dz, do, g, u, wgu, wd, name, exch=()):
    t, d = dz.shape
    f = wd.shape[0]
    tm, tf = _tile(t, 512, 128), _tile(f, 512, 128)
    nf = f // tf

    def body(dz_ref, do_ref, g_ref, u_ref, wg_ref, wu_ref, wd_ref, dg_ref, du_ref, dx_ref, acc):
        j = pl.program_id(1)

        @pl.when(j == 0)
        def _():
            acc[...] = jnp.zeros_like(acc)

        dg, du = _ffn_act_grads(_dot_nt(do_ref[...], wd_ref[...]), g_ref, u_ref)
        dg_ref[...] = dg
        du_ref[...] = du
        acc[...] += _dot_nt(dg, wg_ref[...]) + _dot_nt(du, wu_ref[...])

        @pl.when(j == nf - 1)
        def _():
            dx_ref[...] = ALPHA * dz_ref[...] + acc[...]

    row = lambda i, j: (i, 0)
    tile = lambda i, j: (i, j)
    return _call(
        body, exch, name=name, grid=(t // tm, nf),
        out_shape=[jax.ShapeDtypeStruct((t, f), BF16), jax.ShapeDtypeStruct((t, f), BF16),
                   jax.ShapeDtypeStruct((t, d), F32)],
        in_specs=[pl.BlockSpec((tm, d), row), pl.BlockSpec((tm, d), row),
                  pl.BlockSpec((tm, tf), tile), pl.BlockSpec((tm, tf), tile),
                  pl.BlockSpec((d, tf), lambda i, j: (0, j)),
                  pl.BlockSpec((d, tf), lambda i, j: (0, j + nf)),
                  pl.BlockSpec((tf, d), lambda i, j: (j, 0))],
        out_specs=[pl.BlockSpec((tm, tf), tile), pl.BlockSpec((tm, tf), tile), pl.BlockSpec((tm, d), row)],
        scratch_shapes=[pltpu.VMEM((tm, d), F32)],
        semantics=("parallel", "arbitrary"),
    )(dz, do, g, u, wgu, wgu, wd)


def _ffn_bwd_act(do, gu, wd, name, exch=()):
    t, d = do.shape
    f = wd.shape[0]
    tm, tf = _tile(t, 512, 128), _tile(f, 512, 128)
    nf = f // tf

    def body(do_ref, g_ref, u_ref, wd_ref, dg_ref, du_ref):
        dg, du = _ffn_act_grads(_dot_nt(do_ref[...], wd_ref[...]), g_ref, u_ref)
        dg_ref[...] = dg
        du_ref[...] = du

    tile = lambda i, j: (i, j)
    return _call(
        body, exch, name=name, grid=(t // tm, f // tf),
        out_shape=[jax.ShapeDtypeStruct((t, f), BF16), jax.ShapeDtypeStruct((t, f), BF16)],
        in_specs=[pl.BlockSpec((tm, d), lambda i, j: (i, 0)), pl.BlockSpec((tm, tf), tile),
                  pl.BlockSpec((tm, tf), lambda i, j: (i, j + nf)), pl.BlockSpec((tf, d), lambda i, j: (j, 0))],
        out_specs=[pl.BlockSpec((tm, tf), tile), pl.BlockSpec((tm, tf), tile)],
        semantics=("parallel", "parallel"),
    )(do, gu, gu, wd)


def _ffn_bwd_dx(dz, dg, du, wgu, name, exch=()):
    t, d = dz.shape
    f = dg.shape[1]
    tm, tn = _tile(t, 512, 128), _tile(d, 256, 128)

    def body(dz_ref, dg_ref, du_ref, wg_ref, wu_ref, dx_ref):
        dx_ref[...] = ALPHA * dz_ref[...] + _dot_nt(dg_ref[...], wg_ref[...]) + _dot_nt(du_ref[...], wu_ref[...])

    row = lambda i, n: (i, 0)
    tile = lambda i, n: (i, n)
    return _call(
        body, exch, name=name, grid=(t // tm, d // tn), out_shape=[jax.ShapeDtypeStruct((t, d), F32)],
        in_specs=[pl.BlockSpec((tm, tn), tile), pl.BlockSpec((tm, f), row), pl.BlockSpec((tm, f), row),
                  pl.BlockSpec((tn, f), lambda i, n: (n, 0)), pl.BlockSpec((tn, f), lambda i, n: (n, 1))],
        out_specs=[pl.BlockSpec((tm, tn), tile)],
        semantics=("parallel", "arbitrary"),
    )(dz, dg, du, wgu, wgu)


def _weight_grad(at, b, tn, tmm, name, blocks=None, block_offset=0, into=None, exch=()):
    m, t = at.shape
    nn = b.shape[1]
    tmm = _tile(m, tmm, 16)
    assert nn % tn == 0

    def body(*refs):
        at_ref, b_ref, o_ref = refs[0], refs[1], refs[-1]
        r = _dot(at_ref[...], b_ref[...]).astype(BF16)
        if blocks is None:
            o_ref[...] = r
        else:
            o_ref[0] = r

    in_specs = [pl.BlockSpec((tmm, t), lambda n, i: (i, 0)), pl.BlockSpec((t, tn), lambda n, i: (0, n))]
    args = [at, b]
    aliases = {}
    if into is not None:
        in_specs.append(ANY)
        args.append(into)
        aliases = {2: 0}
    if blocks is None:
        out_shape = jax.ShapeDtypeStruct((m, nn), BF16)
        out_spec = pl.BlockSpec((tmm, tn), lambda n, i: (i, n))
    else:
        out_shape = jax.ShapeDtypeStruct((blocks, m, tn), BF16)
        out_spec = pl.BlockSpec((1, tmm, tn), lambda n, i: (n + block_offset, i, 0))
    (out,), ex = _call(
        body, exch, name=name, grid=(nn // tn, m // tmm), out_shape=[out_shape],
        in_specs=in_specs, out_specs=[out_spec], input_output_aliases=aliases,
        semantics=("parallel", "parallel"),
    )(*args)
    return out, ex


def _mix_in_proj(x, w_in, name, exch=()):
    t, d = x.shape
    n_out = w_in.shape[1]
    tm, cb = _tile(t, 512, 128), _tile(n_out, 512, 128)

    def body(x_ref, w_ref, o_ref, xb):
        @pl.when(pl.program_id(1) == 0)
        def _():
            xb[...] = x_ref[...].astype(BF16)

        o_ref[...] = _dot(xb[...], w_ref[...])

    (out,), ex = _call(
        body, exch, name=name, grid=(t // tm, n_out // cb), out_shape=[jax.ShapeDtypeStruct((t, n_out), F32)],
        in_specs=[pl.BlockSpec((tm, d), lambda i, k: (i, 0)), pl.BlockSpec((d, cb), lambda i, k: (0, k))],
        out_specs=[pl.BlockSpec((tm, cb), lambda i, k: (i, k))],
        scratch_shapes=[pltpu.VMEM((tm, d), BF16)],
        semantics=("parallel", "arbitrary"),
    )(x, w_in)
    return out, ex


def _mix_in_bwd(dproj, w_in, dz, name, exch=()):
    t, d = dz.shape
    kk = w_in.shape[1]
    tm, tn = _tile(t, 512, 128), _tile(d, 256, 128)

    def body(dp_ref, w_ref, dz_ref, dx_ref):
        dx_ref[...] = ALPHA * dz_ref[...] + _dot_nt(dp_ref[...], w_ref[...])

    (out,), ex = _call(
        body, exch, name=name, grid=(t // tm, d // tn), out_shape=[jax.ShapeDtypeStruct((t, d), F32)],
        in_specs=[pl.BlockSpec((tm, kk), lambda i, n: (i, 0)), pl.BlockSpec((tn, kk), lambda i, n: (n, 0)),
                  pl.BlockSpec((tm, tn), lambda i, n: (i, n))],
        out_specs=[pl.BlockSpec((tm, tn), lambda i, n: (i, n))],
        semantics=("parallel", "arbitrary"),
    )(dproj, w_in, dz)
    return out, ex


def _mix_out_fwd(y, w_out, x, ln_g, ln_b, name, exch=()):
    t, d = x.shape
    kk = y.shape[1]
    tm = _tile(t, 256, 128)

    def body(y_ref, w_ref, x_ref, g_ref, b_ref, z_ref, xn_ref, xnt_ref):
        z = ALPHA * x_ref[...] + _dot(y_ref[...], w_ref[...])
        z_ref[...] = z
        xn = _ln(z, g_ref[...], b_ref[...])
        xn_ref[...] = xn
        xnt_ref[...] = xn.T.astype(BF16)

    row = lambda i: (i, 0)
    fixed = lambda i: (0, 0)
    return _call(
        body, exch, name=name, grid=(t // tm,),
        out_shape=[jax.ShapeDtypeStruct((t, d), F32), jax.ShapeDtypeStruct((t, d), F32),
                   jax.ShapeDtypeStruct((d, t), BF16)],
        in_specs=[pl.BlockSpec((tm, kk), row), pl.BlockSpec((kk, d), fixed), pl.BlockSpec((tm, d), row),
                  pl.BlockSpec((1, d), fixed), pl.BlockSpec((1, d), fixed)],
        out_specs=[pl.BlockSpec((tm, d), row), pl.BlockSpec((tm, d), row), pl.BlockSpec((d, tm), lambda i: (0, i))],
        semantics=("parallel",),
    )(y, w_out, x, ln_g, ln_b)


def _mix_out_bwd(dzb, w_out, name):
    t, d = dzb.shape
    kk = w_out.shape[0]
    tm = _tile(t, 256, 128)

    def body(dz_ref, w_ref, dy_ref):
        dy_ref[...] = _dot_nt(dz_ref[...], w_ref[...])

    return pl.pallas_call(
        body, name=name, grid=(t // tm,), out_shape=jax.ShapeDtypeStruct((t, kk), F32),
        in_specs=[pl.BlockSpec((tm, d), lambda i: (i, 0)), pl.BlockSpec((kk, d), lambda i: (0, 0))],
        out_specs=pl.BlockSpec((tm, kk), lambda i: (i, 0)),
        compiler_params=_cparams("parallel"),
    )(dzb, w_out)


def _loss_ln_bwd(z, target, ln_g, ln_b, bf16_scale, name):
    t, d = z.shape
    tm = _tile(t, 512, 8)

    def body(z_ref, t_ref, g_ref, b_ref, dz_ref, dzb_ref, dg_ref, db_ref, loss_ref):
        @pl.when(pl.program_id(0) == 0)
        def _():
            dg_ref[...] = jnp.zeros_like(dg_ref)
            db_ref[...] = jnp.zeros_like(db_ref)
            loss_ref[...] = jnp.zeros_like(loss_ref)

        xh, rstd = _ln_stats(z_ref[...])
        e = xh * g_ref[...] + b_ref[...] - t_ref[...]
        loss_ref[...] += 0.5 * jnp.sum(jnp.sum(e * e, axis=-1, keepdims=True) * (1.0 / d), axis=0, keepdims=True)
        dy = e * (1.0 / d)
        dz = _ln_bwd(dy * g_ref[...], xh, rstd)
        dz_ref[...] = dz
        dzb_ref[...] = (bf16_scale * dz).astype(BF16)
        dg_ref[...] += jnp.sum(dy * xh, axis=0, keepdims=True)
        db_ref[...] += jnp.sum(dy, axis=0, keepdims=True)

    row = lambda i: (i, 0)
    fixed = lambda i: (0, 0)
    return pl.pallas_call(
        body, name=name, grid=(t // tm,),
        out_shape=[jax.ShapeDtypeStruct((t, d), F32), jax.ShapeDtypeStruct((t, d), BF16),
                   jax.ShapeDtypeStruct((1, d), F32), jax.ShapeDtypeStruct((1, d), F32),
                   jax.ShapeDtypeStruct((8, 128), F32)],
        in_specs=[pl.BlockSpec((tm, d), row), pl.BlockSpec((tm, d), row), pl.BlockSpec((1, d), fixed),
                  pl.BlockSpec((1, d), fixed)],
        out_specs=[pl.BlockSpec((tm, d), row), pl.BlockSpec((tm, d), row), pl.BlockSpec((1, d), fixed),
                   pl.BlockSpec((1, d), fixed), pl.BlockSpec((8, 128), fixed)],
        compiler_params=_cparams("arbitrary"),
    )(z, target, ln_g, ln_b)


def _ln_bwd_call(z, dy, ln_g, bf16_scale, name, exch=()):
    t, d = z.shape
    tm = _tile(t, 512, 8)

    def body(z_ref, dy_ref, g_ref, dz_ref, dzb_ref, dg_ref, db_ref):
        @pl.when(pl.program_id(0) == 0)
        def _():
            dg_ref[...] = jnp.zeros_like(dg_ref)
            db_ref[...] = jnp.zeros_like(db_ref)

        xh, rstd = _ln_stats(z_ref[...])
        dy = dy_ref[...]
        dz = _ln_bwd(dy * g_ref[...], xh, rstd)
        dz_ref[...] = dz
        dzb_ref[...] = (bf16_scale * dz).astype(BF16)
        dg_ref[...] += jnp.sum(dy * xh, axis=0, keepdims=True)
        db_ref[...] += jnp.sum(dy, axis=0, keepdims=True)

    row = lambda i: (i, 0)
    fixed = lambda i: (0, 0)
    return _call(
        body, exch, name=name, grid=(t // tm,),
        out_shape=[jax.ShapeDtypeStruct((t, d), F32), jax.ShapeDtypeStruct((t, d), BF16),
                   jax.ShapeDtypeStruct((1, d), F32), jax.ShapeDtypeStruct((1, d), F32)],
        in_specs=[pl.BlockSpec((tm, d), row), pl.BlockSpec((tm, d), row), pl.BlockSpec((1, d), fixed)],
        out_specs=[pl.BlockSpec((tm, d), row), pl.BlockSpec((tm, d), row), pl.BlockSpec((1, d), fixed),
                   pl.BlockSpec((1, d), fixed)],
        semantics=("arbitrary",),
    )(z, dy, ln_g)


CONV_ROWS = 32
SUBLANES = 8


def _fill_shifted(ext, shifted):
    rows = ext.shape[0] - SUBLANES
    for s in range(1, SUBLANES):
        for r in range(0, rows, CONV_ROWS):
            n = min(CONV_ROWS, rows - r)
            shifted[s - 1, r:r + n, :] = ext[r + s:r + s + n, :]


def _window(ext, shifted, lo, n):
    s = lo % SUBLANES
    return ext[lo:lo + n, :] if s == 0 else shifted[s - 1, lo - s:lo - s + n, :]


def _mixer_fwd(proj, conv_w, conv_b, cln_g, cln_b, sln_g, sln_b, sg_wm, sg_bb, name, exch=()):
    t = proj.shape[0]
    tm = _tile(t, 256, CHUNK)
    hb = tm // HALO
    nc = tm // CHUNK
    ch = CONV_CH

    def body(av_ref, ag_ref, bu_ref, bv_ref, hv_ref, hg_ref, cw_ref, cb_ref, lg_ref, lb_ref, sg_ref, sb_ref,
             w_ref, bb_ref, y_ref, yt_ref, c_ref, ext, ext_s):
        i = pl.program_id(0)
        halo = hv_ref[...] * _sigmoid(hg_ref[...])
        ext[0:HALO, :] = jnp.where(i > 0, halo, 0.0)
        ext[HALO:HALO + tm, :] = av_ref[...] * _sigmoid(ag_ref[...])
        _fill_shifted(ext, ext_s)
        for r in range(0, tm, CONV_ROWS):
            acc = jnp.zeros((CONV_ROWS, ch), F32) + cb_ref[...]
            for k in range(CONV_TAPS):
                lo = r + k + HALO - (CONV_TAPS - 1)
                acc = acc + cw_ref[k:k + 1, :] * _window(ext, ext_s, lo, CONV_ROWS)
            c_ref[r:r + CONV_ROWS, :] = acc
        a = _ln(c_ref[...], lg_ref[...], lb_ref[...])
        ya = a * _sigmoid(a)
        y_ref[:, 0:ch] = ya.astype(BF16)
        yt_ref[0:ch, :] = ya.T.astype(BF16)
        for h in range(HEADS):
            sl = slice(h * HEAD_DIM, (h + 1) * HEAD_DIM)
            u, _ = _gelu_and_grad(bu_ref[:, sl])
            v, _ = _gelu_and_grad(bv_ref[:, sl])
            vn = _ln(v, sg_ref[h:h + 1, :], sb_ref[h:h + 1, :])
            vn3 = vn.astype(BF16).reshape(nc, CHUNK, HEAD_DIM)
            wb = jnp.broadcast_to(w_ref[h][None], (nc, CHUNK, CHUNK))
            mixed = jnp.einsum("cts,csd->ctd", wb, vn3, preferred_element_type=F32) + bb_ref[h][None]
            yb = u * mixed.reshape(tm, HEAD_DIM)
            y_ref[:, ch + h * HEAD_DIM:ch + (h + 1) * HEAD_DIM] = yb.astype(BF16)
            yt_ref[ch + h * HEAD_DIM:ch + (h + 1) * HEAD_DIM, :] = yb.T.astype(BF16)

    col = lambda cidx: (lambda i: (i, cidx))
    prev = lambda cidx: (lambda i: (jnp.maximum(i * hb - 1, 0), cidx))
    fix2 = lambda i: (0, 0)
    fix3 = lambda i: (0, 0, 0)
    return _call(
        body, exch, name=name, grid=(t // tm,),
        out_shape=[jax.ShapeDtypeStruct((t, 2 * ch), BF16), jax.ShapeDtypeStruct((2 * ch, t), BF16),
                   jax.ShapeDtypeStruct((t, ch), F32)],
        in_specs=[pl.BlockSpec((tm, ch), col(0)), pl.BlockSpec((tm, ch), col(1)), pl.BlockSpec((tm, ch), col(2)),
                  pl.BlockSpec((tm, ch), col(3)), pl.BlockSpec((HALO, ch), prev(0)), pl.BlockSpec((HALO, ch), prev(1)),
                  pl.BlockSpec((CONV_TAPS, ch), fix2), pl.BlockSpec((1, ch), fix2), pl.BlockSpec((1, ch), fix2),
                  pl.BlockSpec((1, ch), fix2), pl.BlockSpec((HEADS, HEAD_DIM), fix2), pl.BlockSpec((HEADS, HEAD_DIM), fix2),
                  pl.BlockSpec((HEADS, CHUNK, CHUNK), fix3), pl.BlockSpec((HEADS, CHUNK, HEAD_DIM), fix3)],
        out_specs=[pl.BlockSpec((tm, 2 * ch), lambda i: (i, 0)), pl.BlockSpec((2 * ch, tm), lambda i: (0, i)),
                   pl.BlockSpec((tm, ch), lambda i: (i, 0))],
        scratch_shapes=[pltpu.VMEM((HALO + tm, ch), F32), pltpu.VMEM((SUBLANES - 1, HALO + tm, ch), F32)],
        semantics=("parallel",),
    )(proj, proj, proj, proj, proj, proj, conv_w, conv_b, cln_g, cln_b, sln_g, sln_b, sg_wm, sg_bb)


def _mixer_bwd(proj, conv_c, dy, conv_w, cln_g, cln_b, sln_g, sln_b, sg_wm, sg_wmt, sg_bb, name, exch=()):
    t = proj.shape[0]
    tm = _tile(t, 256, CHUNK)
    hb = tm // HALO
    nc = tm // CHUNK
    nt = t // tm
    ch = CONV_CH
    last_halo = t // HALO - 1

    def body(av_ref, ag_ref, bu_ref, bv_ref, hv_ref, hg_ref, c_ref, cn_ref, dya_ref, dyan_ref, dyb_ref,
             cw_ref, lg_ref, lb_ref, sg_ref, sb_ref, w_ref, wt_ref, bb_ref,
             dp_ref, dcw_ref, dcb_ref, dlg_ref, dlb_ref, dsg_ref, dsb_ref, dw_ref, dbs_ref,
             ext_h, ext_dc, ext_hs, ext_dcs, acc_cw):
        i = pl.program_id(0)

        @pl.when(i == 0)
        def _():
            acc_cw[...] = jnp.zeros_like(acc_cw)
            for ref in (dcb_ref, dlg_ref, dlb_ref, dsg_ref, dsb_ref, dw_ref, dbs_ref):
                ref[...] = jnp.zeros_like(ref)

        lg = lg_ref[...]
        lb = lb_ref[...]

        def conv_ln_bwd(c, dya):
            xh, rstd = _ln_stats(c)
            a = xh * lg + lb
            da = dya * _silu_grad(a)
            return _ln_bwd(da * lg, xh, rstd), da, xh

        fold = lambda v: jnp.sum(v.reshape(CONV_ROWS // SUBLANES, SUBLANES, ch), axis=0)
        s_lg = s_lb = s_cb = jnp.zeros((SUBLANES, ch), F32)
        for r in range(0, tm, CONV_ROWS):
            dc, da, xh = conv_ln_bwd(c_ref[r:r + CONV_ROWS, :], dya_ref[r:r + CONV_ROWS, :])
            ext_dc[r:r + CONV_ROWS, :] = dc
            s_lg, s_lb, s_cb = s_lg + fold(da * xh), s_lb + fold(da), s_cb + fold(dc)
        dlg_ref[...] += jnp.sum(s_lg, axis=0, keepdims=True)
        dlb_ref[...] += jnp.sum(s_lb, axis=0, keepdims=True)
        dcb_ref[...] += jnp.sum(s_cb, axis=0, keepdims=True)
        dcn, _, _ = conv_ln_bwd(cn_ref[...], dyan_ref[...])
        ext_dc[tm:tm + HALO, :] = jnp.where(i < nt - 1, dcn, 0.0)
        halo = hv_ref[...] * _sigmoid(hg_ref[...])
        ext_h[0:HALO, :] = jnp.where(i > 0, halo, 0.0)
        ext_h[HALO:HALO + tm, :] = av_ref[...] * _sigmoid(ag_ref[...])
        _fill_shifted(ext_h, ext_hs)
        _fill_shifted(ext_dc, ext_dcs)
        for r in range(0, tm, CONV_ROWS):
            dcr = ext_dc[r:r + CONV_ROWS, :]
            acc = jnp.zeros((CONV_ROWS, ch), F32)
            for k in range(CONV_TAPS):
                lo = r + k + HALO - (CONV_TAPS - 1)
                prod = dcr * _window(ext_h, ext_hs, lo, CONV_ROWS)
                acc_cw[k] += jnp.sum(prod.reshape(CONV_ROWS // 8, 8, ch), axis=0)
                hi = r + (CONV_TAPS - 1) - k
                acc = acc + cw_ref[k:k + 1, :] * _window(ext_dc, ext_dcs, hi, CONV_ROWS)
            sg_r = _sigmoid(ag_ref[r:r + CONV_ROWS, :])
            av_r = av_ref[r:r + CONV_ROWS, :]
            dp_ref[r:r + CONV_ROWS, 0:ch] = (acc * sg_r).astype(BF16)
            dp_ref[r:r + CONV_ROWS, ch:2 * ch] = (acc * av_r * sg_r * (1.0 - sg_r)).astype(BF16)

        @pl.when(i == nt - 1)
        def _():
            dcw_ref[...] = jnp.sum(acc_cw[...], axis=1)

        tril = (lax.broadcasted_iota(jnp.int32, (CHUNK, CHUNK), 0)
                >= lax.broadcasted_iota(jnp.int32, (CHUNK, CHUNK), 1)).astype(F32)
        for h in range(HEADS):
            sl = slice(h * HEAD_DIM, (h + 1) * HEAD_DIM)
            u, du_dx = _gelu_and_grad(bu_ref[:, sl])
            v, dv_dx = _gelu_and_grad(bv_ref[:, sl])
            xhv, rstdv = _ln_stats(v)
            gh = sg_ref[h:h + 1, :]
            vn3 = (xhv * gh + sb_ref[h:h + 1, :]).astype(BF16).reshape(nc, CHUNK, HEAD_DIM)
            wb = jnp.broadcast_to(w_ref[h][None], (nc, CHUNK, CHUNK))
            mixed = jnp.einsum("cts,csd->ctd", wb, vn3, preferred_element_type=F32) + bb_ref[h][None]
            dyb = dyb_ref[:, sl]
            d_u = dyb * mixed.reshape(tm, HEAD_DIM)
            dm = dyb * u
            dm3 = dm.reshape(nc, CHUNK, HEAD_DIM)
            dbs_ref[h:h + 1, :] += jnp.sum(jnp.sum(dm3, axis=0).T, axis=0, keepdims=True)
            dm3b = dm3.astype(BF16)
            dw_h = jnp.sum(jnp.einsum("ctd,csd->cts", dm3b, vn3, preferred_element_type=F32), axis=0)
            dw_ref[h] += dw_h * tril
            wtb = jnp.broadcast_to(wt_ref[h][None], (nc, CHUNK, CHUNK))
            d_vn = jnp.einsum("cst,ctd->csd", wtb, dm3b, preferred_element_type=F32).reshape(tm, HEAD_DIM)
            dsg_ref[h:h + 1, :] += jnp.sum(d_vn * xhv, axis=0, keepdims=True)
            dsb_ref[h:h + 1, :] += jnp.sum(d_vn, axis=0, keepdims=True)
            dv = _ln_bwd(d_vn * gh, xhv, rstdv)
            dp_ref[:, 2 * ch + h * HEAD_DIM:2 * ch + (h + 1) * HEAD_DIM] = (d_u * du_dx).astype(BF16)
            dp_ref[:, 3 * ch + h * HEAD_DIM:3 * ch + (h + 1) * HEAD_DIM] = (dv * dv_dx).astype(BF16)

    col = lambda cidx: (lambda i: (i, cidx))
    prev = lambda cidx: (lambda i: (jnp.maximum(i * hb - 1, 0), cidx))
    nxt = lambda i: (jnp.minimum((i + 1) * hb, last_halo), 0)
    fix2 = lambda i: (0, 0)
    fix3 = lambda i: (0, 0, 0)
    out_shape = [jax.ShapeDtypeStruct((t, 4 * ch), BF16), jax.ShapeDtypeStruct((CONV_TAPS, ch), F32),
                 jax.ShapeDtypeStruct((1, ch), F32), jax.ShapeDtypeStruct((1, ch), F32), jax.ShapeDtypeStruct((1, ch), F32),
                 jax.ShapeDtypeStruct((HEADS, HEAD_DIM), F32), jax.ShapeDtypeStruct((HEADS, HEAD_DIM), F32),
                 jax.ShapeDtypeStruct((HEADS, CHUNK, CHUNK), F32), jax.ShapeDtypeStruct((HEADS, CHUNK), F32)]
    out_specs = [pl.BlockSpec((tm, 4 * ch), lambda i: (i, 0)), pl.BlockSpec((CONV_TAPS, ch), fix2),
                 pl.BlockSpec((1, ch), fix2), pl.BlockSpec((1, ch), fix2), pl.BlockSpec((1, ch), fix2),
                 pl.BlockSpec((HEADS, HEAD_DIM), fix2), pl.BlockSpec((HEADS, HEAD_DIM), fix2),
                 pl.BlockSpec((HEADS, CHUNK, CHUNK), fix3), pl.BlockSpec((HEADS, CHUNK), fix2)]
    in_specs = [pl.BlockSpec((tm, ch), col(0)), pl.BlockSpec((tm, ch), col(1)), pl.BlockSpec((tm, ch), col(2)),
                pl.BlockSpec((tm, ch), col(3)), pl.BlockSpec((HALO, ch), prev(0)), pl.BlockSpec((HALO, ch), prev(1)),
                pl.BlockSpec((tm, ch), col(0)), pl.BlockSpec((HALO, ch), nxt),
                pl.BlockSpec((tm, ch), col(0)), pl.BlockSpec((HALO, ch), nxt), pl.BlockSpec((tm, ch), col(1)),
                pl.BlockSpec((CONV_TAPS, ch), fix2), pl.BlockSpec((1, ch), fix2), pl.BlockSpec((1, ch), fix2),
                pl.BlockSpec((HEADS, HEAD_DIM), fix2), pl.BlockSpec((HEADS, HEAD_DIM), fix2),
                pl.BlockSpec((HEADS, CHUNK, CHUNK), fix3), pl.BlockSpec((HEADS, CHUNK, CHUNK), fix3),
                pl.BlockSpec((HEADS, CHUNK, HEAD_DIM), fix3)]
    return _call(
        body, exch, name=name, grid=(nt,), out_shape=out_shape, in_specs=in_specs, out_specs=out_specs,
        scratch_shapes=[pltpu.VMEM((HALO + tm, ch), F32), pltpu.VMEM((tm + HALO, ch), F32),
                        pltpu.VMEM((SUBLANES - 1, HALO + tm, ch), F32), pltpu.VMEM((SUBLANES - 1, tm + HALO, ch), F32),
                        pltpu.VMEM((CONV_TAPS, 8, ch), F32)],
        semantics=("arbitrary",),
    )(proj, proj, proj, proj, proj, proj, conv_c, conv_c, dy, dy, dy,
      conv_w, cln_g, cln_b, sln_g, sln_b, sg_wm, sg_wmt, sg_bb)


def _pair_sum(parts, from_sibling, core_chip, name):
    _, r, cc = parts.shape
    tr = _tile(r, max(16, (1 << 20) // (2 * cc)), 16)

    def body(cc_ref, p_ref, s_ref, o_ref, own_ref):
        q = (p_ref[...].astype(F32) + s_ref[...].astype(F32)).astype(BF16)
        o_ref[...] = q

        @pl.when(pl.program_id(1) == cc_ref[1])
        def _():
            own_ref[...] = q[0]

    grid_spec = pltpu.PrefetchScalarGridSpec(
        num_scalar_prefetch=1, grid=(r // tr, 4),
        in_specs=[pl.BlockSpec((1, tr, cc), lambda i, j, cc_ref: (2 * j + cc_ref[0], i, 0)),
                  pl.BlockSpec((1, tr, cc), lambda i, j, cc_ref: (j, i, 0))],
        out_specs=[pl.BlockSpec((1, tr, cc), lambda i, j, cc_ref: (j, i, 0)),
                   pl.BlockSpec((tr, cc), lambda i, j, cc_ref: (i, 0))])
    return pl.pallas_call(
        body, name=name, grid_spec=grid_spec,
        out_shape=[jax.ShapeDtypeStruct((4, r, cc), BF16), jax.ShapeDtypeStruct((r, cc), BF16)],
        compiler_params=_cparams("parallel", "arbitrary"),
    )(core_chip, parts, from_sibling)


def _adamw_math(w, g, m, v):
    m = ADAM_B1 * m + (1.0 - ADAM_B1) * g
    v = ADAM_B2 * v + (1.0 - ADAM_B2) * (g * g)
    m_hat = m / (1.0 - ADAM_B1 ** ADAM_STEP)
    v_hat = v / (1.0 - ADAM_B2 ** ADAM_STEP)
    delta = -ADAM_LR * (m_hat / (jnp.sqrt(v_hat) + ADAM_EPS) + ADAM_WD * w)
    return delta, m, v


def _adamw_tile(in_refs, out_refs):
    w_ref, m_ref, v_ref, q_ref, o_ref = in_refs
    g = q_ref[...].astype(F32)
    for k in range(3):
        g = g + o_ref[k].astype(F32)
    d, mm, vv = _adamw_math(w_ref[...], g, m_ref[...], v_ref[...])
    for ref, val in zip(out_refs, (g, d, mm, vv)):
        ref[...] = val


def _adamw_side(w, m, v, chip_part, from_chips, max_tiles):
    r, cc = w.shape
    n = max(k for k in range(1, max_tiles + 1) if r % k == 0 and (r // k) % 16 == 0)
    tr = r // n
    row = ((tr, cc), lambda s: (s, 0))
    return _Side([w, m, v, chip_part, from_chips], [row, row, row, row, ((3, tr, cc), lambda s: (0, s, 0))],
                 [jax.ShapeDtypeStruct((r, cc), F32)] * 4, [row] * 4, n, _adamw_tile)


def _adamw_sharded(w, m, v, chip_part, from_chips, name):
    r, cc = w.shape
    tr = _tile(r, max(16, (1 << 19) // (4 * cc) * 2), 16)

    def body(*refs):
        _adamw_tile(refs[:5], refs[5:])

    row = pl.BlockSpec((tr, cc), lambda i: (i, 0))
    return pl.pallas_call(
        body, name=name, grid=(r // tr,), out_shape=[jax.ShapeDtypeStruct((r, cc), F32)] * 4,
        in_specs=[row, row, row, row, pl.BlockSpec((3, tr, cc), lambda i: (0, i, 0))], out_specs=[row] * 4,
        compiler_params=_cparams("parallel"),
    )(w, m, v, chip_part, from_chips)


def _adamw_small(w, g, m, v, name):
    r, cc = w.shape

    def body(w_ref, g_ref, m_ref, v_ref, d_out, m_out, v_out):
        d, mm, vv = _adamw_math(w_ref[...], g_ref[...], m_ref[...], v_ref[...])
        d_out[...] = d
        m_out[...] = mm
        v_out[...] = vv

    full = pl.BlockSpec((r, cc), lambda i: (0, 0))
    return pl.pallas_call(
        body, name=name, grid=(1,), out_shape=[jax.ShapeDtypeStruct((r, cc), F32)] * 3,
        in_specs=[full] * 4, out_specs=[full] * 3, compiler_params=_cparams("arbitrary"),
    )(w, g, m, v)


SMALL = ("ln1_g", "ln1_b", "conv_b", "conv_ln_g", "conv_ln_b", "sg_ln_g", "sg_ln_b", "sg_w", "sg_b",
         "ln2_g", "ln2_b", "ln3_g", "ln3_b")
ORDER = ("ffn1_w_gate_up", "ffn1_w_down", "ln1_g", "ln1_b", "mix_w_in", "conv_w", "conv_b", "conv_ln_g", "conv_ln_b",
         "sg_ln_g", "sg_ln_b", "sg_w", "sg_b", "mix_w_out", "ln2_g", "ln2_b", "ffn2_w_gate_up", "ffn2_w_down",
         "ln3_g", "ln3_b")


def _rows128(a):
    return a.reshape(-1, 128)


def kernel(x, ffn1_w_gate_up, ffn1_w_down, ln1_g, ln1_b, mix_w_in, conv_w, conv_b, conv_ln_g, conv_ln_b, sg_ln_g, sg_ln_b, sg_w, sg_b, mix_w_out, ln2_g, ln2_b, ffn2_w_gate_up, ffn2_w_down, ln3_g, ln3_b, loss_target, m_ffn1_w_gate_up, m_ffn1_w_down, m_ln1_g, m_ln1_b, m_mix_w_in, m_conv_w, m_conv_b, m_conv_ln_g, m_conv_ln_b, m_sg_ln_g, m_sg_ln_b, m_sg_w, m_sg_b, m_mix_w_out, m_ln2_g, m_ln2_b, m_ffn2_w_gate_up, m_ffn2_w_down, m_ln3_g, m_ln3_b, v_ffn1_w_gate_up, v_ffn1_w_down, v_ln1_g, v_ln1_b, v_mix_w_in, v_conv_w, v_conv_b, v_conv_ln_g, v_conv_ln_b, v_sg_ln_g, v_sg_ln_b, v_sg_w, v_sg_b, v_mix_w_out, v_ln2_g, v_ln2_b, v_ffn2_w_gate_up, v_ffn2_w_down, v_ln3_g, v_ln3_b):
    args = dict(locals())
    w = {n: args[n][0] for n in ORDER}
    mom = {n: args["m_" + n][0] for n in ORDER}
    var = {n: args["v_" + n][0] for n in ORDER}
    x0 = x[0]
    target = loss_target[0]
    t, d = x0.shape
    my_x, my_y, my_c = lax.axis_index("x"), lax.axis_index("y"), lax.axis_index("c")
    my_chip = (2 * my_x + my_y).astype(jnp.int32).reshape(1)
    my_core = my_c.astype(jnp.int32).reshape(1)
    me = 4 * my_x + 2 * my_y + my_c

    big = ("ffn1_w_gate_up", "ffn1_w_down", "mix_w_in", "mix_w_out", "ffn2_w_gate_up", "ffn2_w_down")
    sh = {n: w[n].astype(BF16) for n in big}
    f2s = sh["ffn2_w_gate_up"].shape[1]
    (x0t, x0b), _ = _transpose_bf16(x0, "x0_transpose", with_copy=True)
    order = jnp.stack([4 * p[0] + 2 * p[1] + p[2] for p in _visit_order(my_x, my_y, my_c)]).astype(jnp.int32)
    gu1, (wgu1, wd1, conv_w_all) = _gather_and_gate_up(
        x0b, [sh["ffn1_w_gate_up"], sh["ffn1_w_down"], w["conv_w"]], [True, False, False], order, "ffn1_gate_up_fwd")
    wd1 = wd1.reshape(-1, d)
    conv_w_full = jnp.transpose(conv_w_all, (1, 0, 2)).reshape(CONV_TAPS, CONV_CH)
    tril = jnp.tril(jnp.ones((CHUNK, CHUNK), F32))
    sg_wm = w["sg_w"] * tril
    sg_wm_b = sg_wm.astype(BF16)
    sg_wmt_b = jnp.swapaxes(sg_wm, 1, 2).astype(BF16)
    sg_bb = jnp.broadcast_to(w["sg_b"][:, :, None], (HEADS, CHUNK, HEAD_DIM))
    row = lambda a: a.reshape(1, -1)

    d2 = [sh["ffn2_w_down"]]
    d2_first = d2[0].shape[0] * 3 // 5 // 16 * 16
    d2_top, d2_bottom = (0, d2_first), (d2_first, d2[0].shape[0] - d2_first)
    (h1t, z1, x1), ((g_in, g_out), (g_d2,)) = _ffn_down_fwd(
        gu1, x0, wd1, row(w["ln1_g"]), row(w["ln1_b"]), "ffn1_down_fwd",
        exch=[_gather_first([sh["mix_w_in"], sh["mix_w_out"]], [True, False]),
              _gather_first(d2, [False], rows=d2_top)])
    in_cols = sh["mix_w_in"].shape[1]
    x1t, ((w_in, w_out), (g_d2,)) = _transpose_bf16(
        x1, "x1_transpose", exch=[_gather_forward([g_in, g_out], [True, False], [in_cols, None]),
                                  _gather_forward([g_d2], [False], [None], rows=d2_top)])
    w_out = w_out.reshape(-1, d)
    top, bottom = (0, d // 2), (d // 2, d // 2)
    gu2 = [sh["ffn2_w_gate_up"]]
    proj, ((g_gu2,),) = _mix_in_proj(x1, w_in, "mix_in_fwd", exch=[_gather_first(gu2, [True], rows=top)])
    (y, yt, conv_c), ((g_gu2,),) = _mixer_fwd(
        proj, conv_w_full, row(w["conv_b"]), row(w["conv_ln_g"]), row(w["conv_ln_b"]),
        w["sg_ln_g"], w["sg_ln_b"], sg_wm_b, sg_bb, "mixer_fwd",
        exch=[_both(_gather_first(gu2, [True], rows=bottom, into=[g_gu2]),
                    _gather_forward([g_gu2], [True], [f2s], rows=top))])
    (z2, x2, x2t), ((wgu2,), (g_d2,)) = _mix_out_fwd(
        y, w_out, x1, row(w["ln2_g"]), row(w["ln2_b"]), "mix_out_fwd",
        exch=[_gather_forward([g_gu2], [True], [f2s], rows=bottom),
              _gather_first(d2, [False], rows=d2_bottom, into=[g_d2])])
    (wd2,) = _exchange_alone(_gather_forward([g_d2], [False], [None], rows=d2_bottom), "ffn2_down_gather_forward")
    wd2 = wd2.reshape(-1, d)
    (g2, u2, h2t, z3), _ = _ffn_fwd(x2, wgu2, wd2, row(w["ln3_g"]), row(w["ln3_b"]), "ffn2_fwd", with_ln=False)

    f = wd1.shape[0]
    dn = _tile(d, 1024, 128)
    grads = {}
    core_chip = jnp.concatenate([my_core, my_chip])
    pair = lambda p, s, label: _pair_sum(p, s, core_chip, "pair_sum_" + label)
    adamw = lambda n, own, got, steps: _adamw_side(w[n], mom[n], var[n], own, got, steps)
    m_tiles = d // _tile(d, 512, 16)
    out = {}
    dz3, do2, grads["ln3_g"], grads["ln3_b"], loss_tile = _loss_ln_bwd(
        z3, target, row(w["ln3_g"]), row(w["ln3_b"]), 0.5, "loss_ln3_bwd")
    p_d2, _ = _weight_grad(h2t, do2, dn, 512, "ffn2_dw_down")
    p_d2 = p_d2.reshape(N_DEV, f // N_DEV, d)
    (dg2, du2, dx2), ((s_d2,),) = _ffn_bwd(dz3, do2, g2, u2, wgu2, wd2, "ffn2_bwd", exch=[_rs_sibling([p_d2])])
    q_d2, own_d2 = pair(p_d2, s_d2, "ffn2_down")
    p_gu2, ((r_d2,),) = _weight_grad(x2t, dg2, f2s, 512, "ffn2_dw_gate", blocks=N_DEV, exch=[_rs_chips([q_d2])])
    p_gu2, _ = _weight_grad(x2t, du2, f2s, 512, "ffn2_dw_up", blocks=N_DEV, block_offset=4, into=p_gu2)
    (dz2, dz2b, grads["ln2_g"], grads["ln2_b"]), ((s_gu2,),) = _ln_bwd_call(
        z2, dx2, row(w["ln2_g"]), 1.0, "ln2_bwd", exch=[_rs_sibling([p_gu2])])
    q_gu2, own_gu2 = pair(p_gu2, s_gu2, "ffn2_gate_up")
    dy = _mix_out_bwd(dz2b, w_out, "mix_out_bwd")
    p_out, _ = _weight_grad(yt, dz2b, dn, 512, "mix_out_dw")
    p_out = p_out.reshape(N_DEV, -1, d)
    (dproj, grads["conv_w"], grads["conv_b"], grads["conv_ln_g"], grads["conv_ln_b"], grads["sg_ln_g"],
     grads["sg_ln_b"], grads["sg_w"], grads["sg_b"]), ((r_gu2,),) = _mixer_bwd(
        proj, conv_c, dy, conv_w_full, row(w["conv_ln_g"]), row(w["conv_ln_b"]), w["sg_ln_g"], w["sg_ln_b"],
        sg_wm_b, sg_wmt_b, sg_bb, "mixer_bwd", exch=[_rs_chips([q_gu2], rows=top)])
    dx1, ((s_out,), (r_gu2,)) = _mix_in_bwd(
        dproj, w_in, dz2, "mix_in_bwd", exch=[_rs_sibling([p_out]), _rs_chips([q_gu2], rows=bottom, into=[r_gu2])])
    p_in, (out["ffn2_w_gate_up"], out["ffn2_w_down"]) = _weight_grad(
        x1t, dproj, in_cols, 512, "mix_in_dw", blocks=N_DEV,
        exch=[adamw("ffn2_w_gate_up", own_gu2, r_gu2, N_DEV * m_tiles), adamw("ffn2_w_down", own_d2, r_d2, N_DEV * m_tiles)])
    (dz1, do1, grads["ln1_g"], grads["ln1_b"]), ((s_in,),) = _ln_bwd_call(
        z1, dx1, row(w["ln1_g"]), 0.5, "ln1_bwd", exch=[_rs_sibling([p_in])])
    q_out, own_out = pair(p_out, s_out, "mix_out")
    q_in, own_in = pair(p_in, s_in, "mix_in")
    small_parts = [_rows128(grads[n]) for n in SMALL]
    packed = jnp.concatenate(small_parts + [_rows128(grads["conv_w"]), loss_tile], axis=0)
    p_d1, ((r_in,),) = _weight_grad(h1t, do1, dn, 512, "ffn1_dw_down", exch=[_rs_chips([q_in])])
    p_d1 = p_d1.reshape(N_DEV, f // N_DEV, d)
    (dg1, du1), ((s_d1,), (r_out,), (small_all,)) = _ffn_bwd_act(
        do1, gu1, wd1, "ffn1_bwd_act",
        exch=[_rs_sibling([p_d1]), _rs_chips([q_out]), _small_gather(packed)])
    q_d1, own_d1 = pair(p_d1, s_d1, "ffn1_down")
    p_gu1, ((r_d1,),) = _weight_grad(x0t, dg1, f2s, 512, "ffn1_dw_gate", blocks=N_DEV, exch=[_rs_chips([q_d1])])
    p_gu1, (out["mix_w_in"], out["mix_w_out"]) = _weight_grad(
        x0t, du1, f2s, 512, "ffn1_dw_up", blocks=N_DEV, block_offset=4, into=p_gu1,
        exch=[adamw("mix_w_in", own_in, r_in, 4 * m_tiles), adamw("mix_w_out", own_out, r_out, 4 * m_tiles)])
    (s_gu1,) = _exchange_alone(_rs_sibling([p_gu1]), "ffn1_gate_up_sibling_exchange")
    q_gu1, own_gu1 = pair(p_gu1, s_gu1, "ffn1_gate_up")
    (grad_x,), ((r_gu1,),) = _ffn_bwd_dx(dz1, dg1, du1, wgu1, "ffn1_bwd_dx", exch=[_rs_chips([q_gu1])])
    for n, own, got in (("ffn1_w_down", own_d1, r_d1), ("ffn1_w_gate_up", own_gu1, r_gu1)):
        out[n] = _adamw_sharded(w[n], mom[n], var[n], own, got, "adamw_" + n)

    cw_rows = CONV_TAPS * CONV_CH // 128
    total = _sum_over_devices(small_all)
    offs = [0]
    for p in small_parts:
        offs.append(offs[-1] + p.shape[0])
    n_small = offs[-1]
    loss = total[n_small + cw_rows, 0]
    g_conv_w = lax.dynamic_slice_in_dim(total[n_small:n_small + cw_rows].reshape(CONV_TAPS, CONV_CH),
                                        me * (CONV_CH // N_DEV), CONV_CH // N_DEV, axis=1)
    pad8 = lambda a: jnp.pad(a, ((0, -a.shape[0] % 8), (0, 0)))
    pack = lambda tree, cw: jnp.concatenate([_rows128(tree[n]) for n in SMALL] + [pad8(cw)], axis=0)
    g_pack = jnp.concatenate([total[:n_small], pad8(g_conv_w)], axis=0)
    d_pack, m_pack, v_pack = _adamw_small(pack(w, w["conv_w"]), g_pack, pack(mom, mom["conv_w"]),
                                          pack(var, var["conv_w"]), "adamw_small")
    for k, n in enumerate(SMALL):
        sl = slice(offs[k], offs[k + 1])
        shp = w[n].shape
        out[n] = (total[sl].reshape(shp), d_pack[sl].reshape(shp), m_pack[sl].reshape(shp), v_pack[sl].reshape(shp))
    sl = slice(n_small, n_small + CONV_TAPS)
    out["conv_w"] = (g_conv_w, d_pack[sl], m_pack[sl], v_pack[sl])

    lead = lambda a: a[None]
    res = [loss, grad_x[None]]
    for kind in range(4):
        res += [lead(out[n][kind]) for n in ORDER]
    return tuple(res)
```

```python
import functools
import math

import jax
import jax.numpy as jnp
from jax import lax
from jax.experimental import pallas as pl
from jax.experimental.pallas import tpu as pltpu

F32, BF16 = jnp.float32, jnp.bfloat16
MESH = pl.DeviceIdType.MESH
ANY = pl.BlockSpec(memory_space=pl.ANY)

N_DEV = 8
LN_EPS = 1e-5
ALPHA = 2.0 ** 0.25
CONV_CH = 1024
CONV_TAPS = 31
HALO = 32
HEADS = 8
HEAD_DIM = 128
CHUNK = 128
ADAM_LR, ADAM_B1, ADAM_B2, ADAM_EPS, ADAM_WD, ADAM_STEP = 0.001, 0.9, 0.999, 1e-08, 0.01, 10
V7X_VMEM_LIMIT = 56 * 2 ** 20

def _cparams(*sem):
    return pltpu.CompilerParams(dimension_semantics=sem, vmem_limit_bytes=V7X_VMEM_LIMIT)


def _tile(n, pref, mult):
    best = None
    for t in range(mult, min(n, pref) + 1, mult):
        if n % t == 0:
            best = t
    return best if best is not None else n


def _dot(a, b):
    return jnp.dot(a, b, preferred_element_type=F32)


def _dot_nt(a, b):
    return lax.dot_general(a, b, (((1,), (1,)), ((), ())), preferred_element_type=F32)


def _sigmoid(x):
    return 1.0 / (1.0 + jnp.exp(-x))


def _ln_stats(z):
    mu = jnp.mean(z, axis=-1, keepdims=True)
    zc = z - mu
    var = jnp.mean(zc * zc, axis=-1, keepdims=True)
    rstd = lax.rsqrt(var + LN_EPS)
    return zc * rstd, rstd


def _ln(z, g, b):
    xh, _ = _ln_stats(z)
    return xh * g + b


def _ln_bwd(dxh, xh, rstd):
    m1 = jnp.mean(dxh, axis=-1, keepdims=True)
    m2 = jnp.mean(dxh * xh, axis=-1, keepdims=True)
    return rstd * (dxh - m1 - xh * m2)


_GK = math.sqrt(2.0 / math.pi)
_GA = 0.044715


def _gelu_and_grad(x):
    x2 = x * x
    t = jnp.tanh(_GK * (x + _GA * x * x2))
    y = 0.5 * x * (1.0 + t)
    dy = 0.5 * (1.0 + t) + 0.5 * x * (1.0 - t * t) * (_GK * (1.0 + 3.0 * _GA * x2))
    return y, dy


def _silu_grad(a):
    s = _sigmoid(a)
    return s * (1.0 + a * (1.0 - s))


def _place():
    return lax.axis_index("x"), lax.axis_index("y"), lax.axis_index("c")


def _other_chips(x, y):
    return [(1 - x, y), (x, 1 - y), (1 - x, 1 - y)]


def _visit_order(x, y, c):
    chips = _other_chips(x, y)
    return [(x, y, c), (x, y, 1 - c), (*chips[0], c), (*chips[1], c), (*chips[0], 1 - c), (*chips[1], 1 - c),
            (*chips[2], c), (*chips[2], 1 - c)]


def _gather_and_gate_up(xb, shards, relayed, order, name):
    n = len(shards)
    N_COPIES = 10
    t, d = xb.shape
    cols = shards[0].shape[1]
    tm = _tile(t, 512, 128)
    ni = t // tm
    col_major = [True] + [False] * (n - 1)

    def body(order_ref, x_ref, *refs):
        srcs, gu_ref, xt_ref, dsts = refs[:n], refs[n], refs[n + 1], refs[n + 2:2 * n + 2]
        wbuf, send_sems, recv_sems, local_sems, load_sem = refs[2 * n + 2:]
        b, i = pl.program_id(0), pl.program_id(1)
        x, y, c = _place()
        me, sib = (x, y, c), (x, y, 1 - c)
        chips = _other_chips(x, y)

        near_x, near_y, far = chips

        def slot(w, p, band=None):
            half = shards[w].shape[0] // 2
            rows = None if band is None else (band * half, half)
            return _block_slot(dsts[w], col_major[w], shards[w].shape[1], p, rows)

        def copy(w, s, block, to, band=None, from_src=False):
            return pltpu.make_async_remote_copy(
                src_ref=srcs[w] if from_src else slot(w, block, band), dst_ref=slot(w, block, band),
                send_sem=send_sems.at[N_COPIES * w + s], recv_sem=recv_sems.at[N_COPIES * w + s],
                device_id=to, device_id_type=MESH)

        def own(w):
            return pltpu.make_async_copy(srcs[w], slot(w, me), local_sems.at[w])

        def sends(w):
            out = [copy(w, 0, me, sib, from_src=True), copy(w, 1, me, (*near_x, c), from_src=True),
                   copy(w, 2, me, (*near_y, c), from_src=True)]
            if not relayed[w]:
                out.append(copy(w, 3, me, (*far, c), from_src=True))
            return out

        def passed_on(w):
            out = [copy(w, 4, (*near_x, c), sib), copy(w, 5, (*near_y, c), sib)]
            if relayed[w]:
                out += [copy(w, 6, (*far, c), sib, band=0), copy(w, 9, (*far, c), sib, band=1),
                        copy(w, 7, (*near_x, c), (*near_y, c), band=0), copy(w, 8, (*near_y, c), (*near_x, c), band=1)]
            else:
                out.append(copy(w, 6, (*far, c), sib))
            return out

        def start_sends(w):
            own(w).start()
            for cp in sends(w):
                cp.start()

        def got_near_x(w):
            copy(w, 1, (*near_x, c), me).wait_recv()
            copy(w, 4, (*near_x, c), sib).start()
            if relayed[w]:
                copy(w, 7, (*near_x, c), (*near_y, c), band=0).start()

        def got_near_y(w):
            copy(w, 2, (*near_y, c), me).wait_recv()
            copy(w, 5, (*near_y, c), sib).start()
            if relayed[w]:
                copy(w, 8, (*near_y, c), (*near_x, c), band=1).start()

        def got_far(w):
            if relayed[w]:
                copy(w, 7, (*far, c), me, band=0).wait_recv()
                copy(w, 6, (*far, c), sib, band=0).start()
                copy(w, 8, (*far, c), me, band=1).wait_recv()
                copy(w, 9, (*far, c), sib, band=1).start()
            else:
                copy(w, 3, (*far, c), me).wait_recv()
                copy(w, 6, (*far, c), sib).start()

        def got_from_sibling(w, which):
            if which == 0:
                copy(w, 0, sib, me).wait_recv()
            elif which == 3 and relayed[w]:
                copy(w, 6, (*far, 1 - c), me, band=0).wait_recv()
                copy(w, 9, (*far, 1 - c), me, band=1).wait_recv()
            else:
                copy(w, 3 + which, (*chips[which - 1], 1 - c), me).wait_recv()

        others = range(1, n)

        def arrive(k):
            if k == 0:
                own(0).wait()
            elif k == 1:
                got_from_sibling(0, 0)
            elif k == 2:
                got_near_x(0)
                for w in others:
                    start_sends(w)
            elif k == 3:
                got_near_y(0)
            elif k in (4, 5):
                got_from_sibling(0, k - 3)
            elif k == 6:
                got_far(0)
                for w in others:
                    got_near_x(w)
                    got_near_y(w)
            else:
                got_from_sibling(0, 3)
                for w in others:
                    got_far(w)

        @pl.when((b == 0) & (i == 0))
        def _():
            start_sends(0)

        for k in range(N_DEV):
            @pl.when((b == k) & (i == 0))
            def _(k=k):
                arrive(k)
                at = pl.multiple_of(order_ref[k] * cols, 128)
                load = pltpu.make_async_copy(dsts[0].at[:, pl.ds(at, cols)], wbuf, load_sem.at[0])
                load.start()
                load.wait()

        gu_ref[...] = _dot(x_ref[...].astype(BF16), wbuf[...]).astype(BF16)

        @pl.when(b == 0)
        def _():
            xt_ref[...] = x_ref[...].T.astype(BF16)

        @pl.when((b == N_DEV - 1) & (i == ni - 1))
        def _():
            for w in others:
                for which in range(4):
                    got_from_sibling(w, which)
                own(w).wait()
            for w in range(n):
                for cp in sends(w) + passed_on(w):
                    cp.wait_send()

    grid_spec = pltpu.PrefetchScalarGridSpec(
        num_scalar_prefetch=1, grid=(N_DEV, ni),
        in_specs=[pl.BlockSpec((tm, d), lambda b, i, o: (i, 0))] + [ANY] * n,
        out_specs=[pl.BlockSpec((tm, cols), lambda b, i, o: (i, o[b])),
                   pl.BlockSpec((d, tm), lambda b, i, o: (0, jnp.where(b == 0, i, ni - 1)))] + [ANY] * n,
        scratch_shapes=[pltpu.VMEM((d, cols), BF16), pltpu.SemaphoreType.DMA((N_COPIES * n,)),
                        pltpu.SemaphoreType.DMA((N_COPIES * n,)), pltpu.SemaphoreType.DMA((n,)),
                        pltpu.SemaphoreType.DMA((1,))])
    res = pl.pallas_call(
        body, name=name, grid_spec=grid_spec,
        out_shape=[jax.ShapeDtypeStruct((t, N_DEV * cols), BF16), jax.ShapeDtypeStruct((d, t), BF16)]
        + [_gathered_shape(s, cm) for s, cm in zip(shards, col_major)],
        compiler_params=_cparams("arbitrary", "arbitrary"),
    )(order, xb, *shards)
    return res[0], res[1], res[2:]


class _Exchange:
    def __init__(self, ins, io, new, n_sems, n_local, make):
        self.ins, self.io, self.new = list(ins), list(io), list(new)
        self.n_sems, self.n_local, self.make = n_sems, n_local, make


def _block_slot(ref, col_major, cols, place, rows=None):
    k = 4 * place[0] + 2 * place[1] + place[2]
    band = slice(None) if rows is None else pl.ds(rows[0], rows[1])
    if col_major:
        return ref.at[band, pl.ds(pl.multiple_of(k * cols, 128), cols)]
    return ref.at[k] if rows is None else ref.at[k, band]


def _gathered_shape(s, col_major):
    return jax.ShapeDtypeStruct((s.shape[0], N_DEV * s.shape[1]) if col_major else (N_DEV,) + s.shape, s.dtype)


def _gather_first(shards, col_major, rows=None, into=None):
    n = len(shards)
    new = [] if into is not None else [_gathered_shape(s, cm) for s, cm in zip(shards, col_major)]

    def make(in_refs, io_refs, new_refs, send_sems, recv_sems, local_sems, base=0, local_base=0):
        x, y, c = _place()
        targets = [(x, y, 1 - c)] + [(*chip, c) for chip in _other_chips(x, y)]
        gathered = io_refs if into is not None else new_refs
        copies = []
        for w in range(n):
            src = in_refs[w] if rows is None else in_refs[w].at[pl.ds(rows[0], rows[1])]
            slot = _block_slot(gathered[w], col_major[w], shards[w].shape[1], (x, y, c), rows)
            copies.append(pltpu.make_async_copy(src, slot, local_sems.at[local_base + w]))
            for s, to in enumerate(targets):
                copies.append(pltpu.make_async_remote_copy(
                    src_ref=src, dst_ref=slot, send_sem=send_sems.at[base + 4 * w + s],
                    recv_sem=recv_sems.at[base + 4 * w + s], device_id=to, device_id_type=MESH))
        return copies

    return _Exchange(shards, into or [], new, 4 * n, n, make)


def _gather_forward(gathered, col_major, cols, rows=None):
    n = len(gathered)

    def make(in_refs, io_refs, new_refs, send_sems, recv_sems, local_sems, base=0, local_base=0):
        x, y, c = _place()
        copies = []
        for w in range(n):
            for j, chip in enumerate(_other_chips(x, y)):
                slot = _block_slot(io_refs[w], col_major[w], cols[w], (*chip, c), rows)
                copies.append(pltpu.make_async_remote_copy(
                    src_ref=slot, dst_ref=slot, send_sem=send_sems.at[base + 3 * w + j],
                    recv_sem=recv_sems.at[base + 3 * w + j], device_id=(x, y, 1 - c), device_id_type=MESH))
        return copies

    return _Exchange([], gathered, [], 3 * n, 0, make)


def _both(a, b):
    def make(in_refs, io_refs, new_refs, send_sems, recv_sems, local_sems):
        na = len(a.ins)
        return (a.make(in_refs[:na], io_refs, [], send_sems, recv_sems, local_sems, 0, 0)
                + b.make(in_refs[na:], io_refs, [], send_sems, recv_sems, local_sems, a.n_sems, a.n_local))

    return _Exchange(a.ins + b.ins, a.io, [], a.n_sems + b.n_sems, a.n_local + b.n_local, make)


def _rs_sibling(parts):
    n = len(parts)

    def make(in_refs, io_refs, new_refs, send_sems, recv_sems, local_sems):
        x, y, c = _place()
        copies = []
        for w in range(n):
            for j in range(4):
                copies.append(pltpu.make_async_remote_copy(
                    src_ref=in_refs[w].at[2 * j + (1 - c)], dst_ref=new_refs[w].at[j],
                    send_sem=send_sems.at[4 * w + j], recv_sem=recv_sems.at[4 * w + j],
                    device_id=(x, y, 1 - c), device_id_type=MESH))
        return copies

    return _Exchange(parts, [], [jax.ShapeDtypeStruct((4,) + p.shape[1:], p.dtype) for p in parts], 4 * n, 0, make)


def _rs_chips(chip_parts, rows=None, into=None):
    n = len(chip_parts)
    band = slice(None) if rows is None else pl.ds(rows[0], rows[1])
    new = [] if into is not None else [jax.ShapeDtypeStruct((3,) + p.shape[1:], p.dtype) for p in chip_parts]

    def make(in_refs, io_refs, new_refs, send_sems, recv_sems, local_sems):
        x, y, c = _place()
        landing = io_refs if into is not None else new_refs
        copies = []
        for w in range(n):
            for rel, (px, py) in enumerate(_other_chips(x, y)):
                copies.append(pltpu.make_async_remote_copy(
                    src_ref=in_refs[w].at[2 * px + py, band], dst_ref=landing[w].at[rel, band],
                    send_sem=send_sems.at[3 * w + rel], recv_sem=recv_sems.at[3 * w + rel],
                    device_id=(px, py, c), device_id_type=MESH))
        return copies

    return _Exchange(chip_parts, into or [], new, 3 * n, 0, make)


class _Side:
    def __init__(self, ins, in_blocks, out_shapes, out_blocks, n_tiles, fn):
        self.ins, self.in_blocks, self.out_shapes, self.out_blocks = list(ins), in_blocks, list(out_shapes), out_blocks
        self.n_tiles, self.fn = n_tiles, fn


def _call(body, exch, *, name, grid, in_specs, out_specs, out_shape, scratch_shapes=(), semantics,
          input_output_aliases=None):
    exch = list(exch)
    in_specs, out_specs, out_shape = list(in_specs), list(out_specs), list(out_shape)
    scratch_shapes = list(scratch_shapes)
    if not exch:
        fn = pl.pallas_call(body, name=name, grid=grid, in_specs=in_specs, out_specs=out_specs, out_shape=out_shape,
                            scratch_shapes=scratch_shapes, input_output_aliases=input_output_aliases or {},
                            compiler_params=_cparams(*semantics))
        return lambda *args: (fn(*args), [])
    n_in, n_out, n_scr = len(in_specs), len(out_specs), len(scratch_shapes)
    aliases = dict(input_output_aliases or {})
    all_in, all_out_specs, all_out_shape, all_scr = list(in_specs), list(out_specs), list(out_shape), list(scratch_shapes)
    extra_args = []

    def step(idx):
        s = idx[0]
        for a in range(1, len(grid)):
            s = s * grid[a] + idx[a]
        return s

    def tile_spec(shape, where, n_tiles):
        return pl.BlockSpec(shape, lambda *idx: where(jnp.minimum(step(idx), n_tiles - 1)))

    for ex in exch:
        if isinstance(ex, _Side):
            all_in += [tile_spec(shape, where, ex.n_tiles) for shape, where in ex.in_blocks]
            extra_args += ex.ins
            all_out_specs += [tile_spec(shape, where, ex.n_tiles) for shape, where in ex.out_blocks]
            all_out_shape += ex.out_shapes
            continue
        for k, a in enumerate(ex.io):
            aliases[len(all_in) + len(ex.ins) + k] = len(all_out_specs) + k
        all_in += [ANY] * (len(ex.ins) + len(ex.io))
        extra_args += ex.ins + ex.io
        all_out_specs += [ANY] * (len(ex.io) + len(ex.new))
        all_out_shape += [jax.ShapeDtypeStruct(a.shape, a.dtype) for a in ex.io] + ex.new
        all_scr += [pltpu.SemaphoreType.DMA((ex.n_sems,)), pltpu.SemaphoreType.DMA((ex.n_sems,)),
                    pltpu.SemaphoreType.DMA((max(ex.n_local, 1),))]

    n_ins = [len(ex.ins) if isinstance(ex, _Side) else len(ex.ins) + len(ex.io) for ex in exch]
    n_outs = [len(ex.out_shapes) if isinstance(ex, _Side) else len(ex.io) + len(ex.new) for ex in exch]

    def wrapped(*refs):
        pos = n_in
        ex_in = []
        for k in n_ins:
            ex_in.append(refs[pos:pos + k])
            pos += k
        outs = refs[pos:pos + n_out]
        pos += n_out
        ex_out = []
        for k in n_outs:
            ex_out.append(refs[pos:pos + k])
            pos += k
        scr = refs[pos:pos + n_scr]
        pos += n_scr
        idx = [pl.program_id(a) for a in range(len(grid))]
        first = functools.reduce(jnp.logical_and, [i == 0 for i in idx])
        last = functools.reduce(jnp.logical_and, [i == g - 1 for i, g in zip(idx, grid)])

        def copies():
            out, at = [], pos
            for ex, ei, eo in zip(exch, ex_in, ex_out):
                if not isinstance(ex, _Side):
                    out += ex.make(ei[:len(ex.ins)], eo[:len(ex.io)], eo[len(ex.io):], *refs[at:at + 3])
                    at += 3
            return out

        @pl.when(first)
        def _():
            for cp in copies():
                cp.start()

        body(*refs[:n_in], *outs, *scr)
        for ex, ei, eo in zip(exch, ex_in, ex_out):
            if isinstance(ex, _Side):
                pl.when(step(idx) < ex.n_tiles)(functools.partial(ex.fn, ei, eo))

        @pl.when(last)
        def _():
            for cp in copies():
                cp.wait()

    fn = pl.pallas_call(wrapped, name=name, grid=grid, in_specs=all_in, out_specs=all_out_specs,
                        out_shape=all_out_shape, scratch_shapes=all_scr, input_output_aliases=aliases,
                        compiler_params=_cparams(*(["arbitrary"] * len(grid))))

    def run(*args):
        res = fn(*args, *extra_args)
        outs, pos, ex_res = res[:n_out], n_out, []
        for k in n_outs:
            ex_res.append(list(res[pos:pos + k]))
            pos += k
        return outs, ex_res

    return run


def _exchange_alone(ex, name):
    def body():
        pass

    _, res = _call(body, [ex], name=name, grid=(1,), in_specs=[], out_specs=[], out_shape=[], semantics=("arbitrary",))()
    return res[0]


def _small_gather(part):
    def make(in_refs, io_refs, new_refs, send_sems, recv_sems, local_sems):
        x, y, c = _place()
        slot = new_refs[0].at[4 * x + 2 * y + c]
        copies = [pltpu.make_async_copy(in_refs[0], slot, local_sems.at[0])]
        for d in range(1, N_DEV):
            peer = (1 - x if d & 4 else x, 1 - y if d & 2 else y, 1 - c if d & 1 else c)
            copies.append(pltpu.make_async_remote_copy(
                src_ref=in_refs[0], dst_ref=slot, send_sem=send_sems.at[d - 1], recv_sem=recv_sems.at[d - 1],
                device_id=peer, device_id_type=MESH))
        return copies

    return _Exchange([part], [], [jax.ShapeDtypeStruct((N_DEV,) + part.shape, part.dtype)], N_DEV - 1, 1, make)


def _sum_over_devices(parts):
    _, rows, lanes = parts.shape

    def body(p_ref, o_ref):
        acc = p_ref[0]
        for k in range(1, N_DEV):
            acc = acc + p_ref[k]
        o_ref[...] = acc

    return pl.pallas_call(
        body, name="small_grads_sum", grid=(1,), out_shape=jax.ShapeDtypeStruct((rows, lanes), F32),
        in_specs=[pl.BlockSpec((N_DEV, rows, lanes), lambda i: (0, 0, 0))],
        out_specs=pl.BlockSpec((rows, lanes), lambda i: (0, 0)),
        compiler_params=_cparams("arbitrary"),
    )(parts)


def _transpose_bf16(a, name, exch=(), with_copy=False):
    r, c = a.shape
    tr, tc = _tile(r, 512, 128), _tile(c, 512, 128)

    def body(a_ref, o_ref, *copy_ref):
        v = a_ref[...].astype(F32)
        o_ref[...] = v.T.astype(BF16)
        if with_copy:
            copy_ref[0][...] = v.astype(BF16)

    outs, ex = _call(
        body, exch, name=name, grid=(r // tr, c // tc),
        out_shape=[jax.ShapeDtypeStruct((c, r), BF16)] + [jax.ShapeDtypeStruct((r, c), BF16)] * with_copy,
        in_specs=[pl.BlockSpec((tr, tc), lambda i, j: (i, j))],
        out_specs=[pl.BlockSpec((tc, tr), lambda i, j: (j, i))] + [pl.BlockSpec((tr, tc), lambda i, j: (i, j))] * with_copy,
        semantics=("parallel", "parallel"),
    )(a)
    return (outs if with_copy else outs[0]), ex


def _ffn_fwd(x, wgu, wd, ln_g, ln_b, name, exch=(), with_ln=True):
    t, d = x.shape
    f = wd.shape[0]
    tm, tf = _tile(t, 512, 128), _tile(f, 512, 128)
    nf = f // tf

    def body(x_ref, wg_ref, wu_ref, wd_ref, g_ref, b_ref, go_ref, uo_ref, ht_ref, z_ref, *rest):
        xn_ref = rest[0] if with_ln else None
        xb, acc = rest[-2:]
        j = pl.program_id(1)

        @pl.when(j == 0)
        def _():
            xb[...] = x_ref[...].astype(BF16)
            acc[...] = jnp.zeros_like(acc)

        g = _dot(xb[...], wg_ref[...])
        u = _dot(xb[...], wu_ref[...])
        h = g * _sigmoid(g) * u
        go_ref[...] = g.astype(BF16)
        uo_ref[...] = u.astype(BF16)
        ht_ref[...] = h.T.astype(BF16)
        acc[...] += _dot(h.astype(BF16), wd_ref[...])

        @pl.when(j == nf - 1)
        def _():
            z = ALPHA * x_ref[...] + 0.5 * acc[...]
            z_ref[...] = z
            if with_ln:
                xn_ref[...] = _ln(z, g_ref[...], b_ref[...])

    row = lambda i, j: (i, 0)
    n_td = 2 if with_ln else 1
    return _call(
        body, exch, name=name, grid=(t // tm, nf),
        out_shape=[jax.ShapeDtypeStruct((t, f), BF16), jax.ShapeDtypeStruct((t, f), BF16),
                   jax.ShapeDtypeStruct((f, t), BF16)] + [jax.ShapeDtypeStruct((t, d), F32)] * n_td,
        in_specs=[pl.BlockSpec((tm, d), row),
                  pl.BlockSpec((d, tf), lambda i, j: (0, j)),
                  pl.BlockSpec((d, tf), lambda i, j: (0, j + nf)),
                  pl.BlockSpec((tf, d), lambda i, j: (j, 0)),
                  pl.BlockSpec((1, d), lambda i, j: (0, 0)),
                  pl.BlockSpec((1, d), lambda i, j: (0, 0))],
        out_specs=[pl.BlockSpec((tm, tf), lambda i, j: (i, j)), pl.BlockSpec((tm, tf), lambda i, j: (i, j)),
                   pl.BlockSpec((tf, tm), lambda i, j: (j, i))] + [pl.BlockSpec((tm, d), row)] * n_td,
        scratch_shapes=[pltpu.VMEM((tm, d), BF16), pltpu.VMEM((tm, d), F32)],
        semantics=("parallel", "arbitrary"),
    )(x, wgu, wgu, wd, ln_g, ln_b)


def _ffn_down_fwd(gu, x, wd, ln_g, ln_b, name, exch=()):
    t, d = x.shape
    f = wd.shape[0]
    tm, tf = _tile(t, 512, 128), _tile(f, 512, 128)
    nf = f // tf

    def body(g_ref, u_ref, wd_ref, x_ref, lg_ref, lb_ref, ht_ref, z_ref, xn_ref, acc):
        j = pl.program_id(1)

        @pl.when(j == 0)
        def _():
            acc[...] = jnp.zeros_like(acc)

        g = g_ref[...].astype(F32)
        h = g * _sigmoid(g) * u_ref[...].astype(F32)
        ht_ref[...] = h.T.astype(BF16)
        acc[...] += _dot(h.astype(BF16), wd_ref[...])

        @pl.when(j == nf - 1)
        def _():
            z = ALPHA * x_ref[...] + 0.5 * acc[...]
            z_ref[...] = z
            xn_ref[...] = _ln(z, lg_ref[...], lb_ref[...])

    row = lambda i, j: (i, 0)
    fixed = lambda i, j: (0, 0)
    return _call(
        body, exch, name=name, grid=(t // tm, nf),
        out_shape=[jax.ShapeDtypeStruct((f, t), BF16), jax.ShapeDtypeStruct((t, d), F32),
                   jax.ShapeDtypeStruct((t, d), F32)],
        in_specs=[pl.BlockSpec((tm, tf), lambda i, j: (i, j)), pl.BlockSpec((tm, tf), lambda i, j: (i, j + nf)),
                  pl.BlockSpec((tf, d), lambda i, j: (j, 0)), pl.BlockSpec((tm, d), row),
                  pl.BlockSpec((1, d), fixed), pl.BlockSpec((1, d), fixed)],
        out_specs=[pl.BlockSpec((tf, tm), lambda i, j: (j, i)), pl.BlockSpec((tm, d), row), pl.BlockSpec((tm, d), row)],
        scratch_shapes=[pltpu.VMEM((tm, d), F32)],
        semantics=("parallel", "arbitrary"),
    )(gu, gu, wd, x, ln_g, ln_b)


def _ffn_act_grads(dh, g_ref, u_ref):
    gg = g_ref[...].astype(F32)
    uu = u_ref[...].astype(F32)
    s = _sigmoid(gg)
    du = (dh * (gg * s)).astype(BF16)
    dg = (dh * uu * (s * (1.0 + gg * (1.0 - s)))).astype(BF16)
    return dg, du


def _ffn_bwd(dz, do, g, u, wgu, wd, name, exch=()):
    t, d = dz.shape
    f = wd.shape[0]
    tm, tf = _tile(t, 512, 128), _tile(f, 512, 128)
    nf = f // tf

    def body(dz_ref, do_ref, g_ref, u_ref, wg_ref, wu_ref, wd_ref, dg_ref, du_ref, dx_ref, acc):
        j = pl.program_id(1)

        @pl.when(j == 0)
        def _():
            acc[...] = jnp.zeros_like(acc)

        dg, du = _ffn_act_grads(_dot_nt(do_ref[...], wd_ref[...]), g_ref, u_ref)
        dg_ref[...] = dg
        du_ref[...] = du
        acc[...] += _dot_nt(dg, wg_ref[...]) + _dot_nt(du, wu_ref[...])

        @pl.when(j == nf - 1)
        def _():
            dx_ref[...] = ALPHA * dz_ref[...] + acc[...]

    row = lambda i, j: (i, 0)
    tile = lambda i, j: (i, j)
    return _call(
        body, exch, name=name, grid=(t // tm, nf),
        out_shape=[jax.ShapeDtypeStruct((t, f), BF16), jax.ShapeDtypeStruct((t, f), BF16),
                   jax.ShapeDtypeStruct((t, d), F32)],
        in_specs=[pl.BlockSpec((tm, d), row), pl.BlockSpec((tm, d), row),
                  pl.BlockSpec((tm, tf), tile), pl.BlockSpec((tm, tf), tile),
                  pl.BlockSpec((d, tf), lambda i, j: (0, j)),
                  pl.BlockSpec((d, tf), lambda i, j: (0, j + nf)),
                  pl.BlockSpec((tf, d), lambda i, j: (j, 0))],
        out_specs=[pl.BlockSpec((tm, tf), tile), pl.BlockSpec((tm, tf), tile), pl.BlockSpec((tm, d), row)],
        scratch_shapes=[pltpu.VMEM((tm, d), F32)],
        semantics=("parallel", "arbitrary"),
    )(dz, do, g, u, wgu, wgu, wd)


def _ffn_bwd_act(do, gu, wd, name, exch=()):
    t, d = do.shape
    f = wd.shape[0]
    tm, tf = _tile(t, 512, 128), _tile(f, 512, 128)
    nf = f // tf

    def body(do_ref, g_ref, u_ref, wd_ref, dg_ref, du_ref):
        dg, du = _ffn_act_grads(_dot_nt(do_ref[...], wd_ref[...]), g_ref, u_ref)
        dg_ref[...] = dg
        du_ref[...] = du

    tile = lambda i, j: (i, j)
    return _call(
        body, exch, name=name, grid=(t // tm, f // tf),
        out_shape=[jax.ShapeDtypeStruct((t, f), BF16), jax.ShapeDtypeStruct((t, f), BF16)],
        in_specs=[pl.BlockSpec((tm, d), lambda i, j: (i, 0)), pl.BlockSpec((tm, tf), tile),
                  pl.BlockSpec((tm, tf), lambda i, j: (i, j + nf)), pl.BlockSpec((tf, d), lambda i, j: (j, 0))],
        out_specs=[pl.BlockSpec((tm, tf), tile), pl.BlockSpec((tm, tf), tile)],
        semantics=("parallel", "parallel"),
    )(do, gu, gu, wd)


def _ffn_bwd_dx(dz, dg, du, wgu, name, exch=()):
    t, d = dz.shape
    f = dg.shape[1]
    tm, tn = _tile(t, 512, 128), _tile(d, 256, 128)

    def body(dz_ref, dg_ref, du_ref, wg_ref, wu_ref, dx_ref):
        dx_ref[...] = ALPHA * dz_ref[...] + _dot_nt(dg_ref[...], wg_ref[...]) + _dot_nt(du_ref[...], wu_ref[...])

    row = lambda i, n: (i, 0)
    tile = lambda i, n: (i, n)
    return _call(
        body, exch, name=name, grid=(t // tm, d // tn), out_shape=[jax.ShapeDtypeStruct((t, d), F32)],
        in_specs=[pl.BlockSpec((tm, tn), tile), pl.BlockSpec((tm, f), row), pl.BlockSpec((tm, f), row),
                  pl.BlockSpec((tn, f), lambda i, n: (n, 0)), pl.BlockSpec((tn, f), lambda i, n: (n, 1))],
        out_specs=[pl.BlockSpec((tm, tn), tile)],
        semantics=("parallel", "arbitrary"),
    )(dz, dg, du, wgu, wgu)


def _weight_grad(at, b, tn, tmm, name, blocks=None, block_offset=0, into=None, exch=()):
    m, t = at.shape
    nn = b.shape[1]
    tmm = _tile(m, tmm, 16)
    assert nn % tn == 0

    def body(*refs):
        at_ref, b_ref, o_ref = refs[0], refs[1], refs[-1]
        r = _dot(at_ref[...], b_ref[...]).astype(BF16)
        if blocks is None:
            o_ref[...] = r
        else:
            o_ref[0] = r

    in_specs = [pl.BlockSpec((tmm, t), lambda n, i: (i, 0)), pl.BlockSpec((t, tn), lambda n, i: (0, n))]
    args = [at, b]
    aliases = {}
    if into is not None:
        in_specs.append(ANY)
        args.append(into)
        aliases = {2: 0}
    if blocks is None:
        out_shape = jax.ShapeDtypeStruct((m, nn), BF16)
        out_spec = pl.BlockSpec((tmm, tn), lambda n, i: (i, n))
    else:
        out_shape = jax.ShapeDtypeStruct((blocks, m, tn), BF16)
        out_spec = pl.BlockSpec((1, tmm, tn), lambda n, i: (n + block_offset, i, 0))
    (out,), ex = _call(
        body, exch, name=name, grid=(nn // tn, m // tmm), out_shape=[out_shape],
        in_specs=in_specs, out_specs=[out_spec], input_output_aliases=aliases,
        semantics=("parallel", "parallel"),
    )(*args)
    return out, ex


def _mix_in_proj(x, w_in, name, exch=()):
    t, d = x.shape
    n_out = w_in.shape[1]
    tm, cb = _tile(t, 512, 128), _tile(n_out, 512, 128)

    def body(x_ref, w_ref, o_ref, xb):
        @pl.when(pl.program_id(1) == 0)
        def _():
            xb[...] = x_ref[...].astype(BF16)

        o_ref[...] = _dot(xb[...], w_ref[...])

    (out,), ex = _call(
        body, exch, name=name, grid=(t // tm, n_out // cb), out_shape=[jax.ShapeDtypeStruct((t, n_out), F32)],
        in_specs=[pl.BlockSpec((tm, d), lambda i, k: (i, 0)), pl.BlockSpec((d, cb), lambda i, k: (0, k))],
        out_specs=[pl.BlockSpec((tm, cb), lambda i, k: (i, k))],
        scratch_shapes=[pltpu.VMEM((tm, d), BF16)],
        semantics=("parallel", "arbitrary"),
    )(x, w_in)
    return out, ex


def _mix_in_bwd(dproj, w_in, dz, name, exch=()):
    t, d = dz.shape
    kk = w_in.shape[1]
    tm, tn = _tile(t, 512, 128), _tile(d, 256, 128)

    def body(dp_ref, w_ref, dz_ref, dx_ref):
        dx_ref[...] = ALPHA * dz_ref[...] + _dot_nt(dp_ref[...], w_ref[...])

    (out,), ex = _call(
        body, exch, name=name, grid=(t // tm, d // tn), out_shape=[jax.ShapeDtypeStruct((t, d), F32)],
        in_specs=[pl.BlockSpec((tm, kk), lambda i, n: (i, 0)), pl.BlockSpec((tn, kk), lambda i, n: (n, 0)),
                  pl.BlockSpec((tm, tn), lambda i, n: (i, n))],
        out_specs=[pl.BlockSpec((tm, tn), lambda i, n: (i, n))],
        semantics=("parallel", "arbitrary"),
    )(dproj, w_in, dz)
    return out, ex


def _mix_out_fwd(y, w_out, x, ln_g, ln_b, name, exch=()):
    t, d = x.shape
    kk = y.shape[1]
    tm = _tile(t, 256, 128)

    def body(y_ref, w_ref, x_ref, g_ref, b_ref, z_ref, xn_ref, xnt_ref):
        z = ALPHA * x_ref[...] + _dot(y_ref[...], w_ref[...])
        z_ref[...] = z
        xn = _ln(z, g_ref[...], b_ref[...])
        xn_ref[...] = xn
        xnt_ref[...] = xn.T.astype(BF16)

    row = lambda i: (i, 0)
    fixed = lambda i: (0, 0)
    return _call(
        body, exch, name=name, grid=(t // tm,),
        out_shape=[jax.ShapeDtypeStruct((t, d), F32), jax.ShapeDtypeStruct((t, d), F32),
                   jax.ShapeDtypeStruct((d, t), BF16)],
        in_specs=[pl.BlockSpec((tm, kk), row), pl.BlockSpec((kk, d), fixed), pl.BlockSpec((tm, d), row),
                  pl.BlockSpec((1, d), fixed), pl.BlockSpec((1, d), fixed)],
        out_specs=[pl.BlockSpec((tm, d), row), pl.BlockSpec((tm, d), row), pl.BlockSpec((d, tm), lambda i: (0, i))],
        semantics=("parallel",),
    )(y, w_out, x, ln_g, ln_b)


def _mix_out_bwd(dzb, w_out, name):
    t, d = dzb.shape
    kk = w_out.shape[0]
    tm = _tile(t, 256, 128)

    def body(dz_ref, w_ref, dy_ref):
        dy_ref[...] = _dot_nt(dz_ref[...], w_ref[...])

    return pl.pallas_call(
        body, name=name, grid=(t // tm,), out_shape=jax.ShapeDtypeStruct((t, kk), F32),
        in_specs=[pl.BlockSpec((tm, d), lambda i: (i, 0)), pl.BlockSpec((kk, d), lambda i: (0, 0))],
        out_specs=pl.BlockSpec((tm, kk), lambda i: (i, 0)),
        compiler_params=_cparams("parallel"),
    )(dzb, w_out)


def _loss_ln_bwd(z, target, ln_g, ln_b, bf16_scale, name):
    t, d = z.shape
    tm = _tile(t, 512, 8)

    def body(z_ref, t_ref, g_ref, b_ref, dz_ref, dzb_ref, dg_ref, db_ref, loss_ref):
        @pl.when(pl.program_id(0) == 0)
        def _():
            dg_ref[...] = jnp.zeros_like(dg_ref)
            db_ref[...] = jnp.zeros_like(db_ref)
            loss_ref[...] = jnp.zeros_like(loss_ref)

        xh, rstd = _ln_stats(z_ref[...])
        e = xh * g_ref[...] + b_ref[...] - t_ref[...]
        loss_ref[...] += 0.5 * jnp.sum(jnp.sum(e * e, axis=-1, keepdims=True) * (1.0 / d), axis=0, keepdims=True)
        dy = e * (1.0 / d)
        dz = _ln_bwd(dy * g_ref[...], xh, rstd)
        dz_ref[...] = dz
        dzb_ref[...] = (bf16_scale * dz).astype(BF16)
        dg_ref[...] += jnp.sum(dy * xh, axis=0, keepdims=True)
        db_ref[...] += jnp.sum(dy, axis=0, keepdims=True)

    row = lambda i: (i, 0)
    fixed = lambda i: (0, 0)
    return pl.pallas_call(
        body, name=name, grid=(t // tm,),
        out_shape=[jax.ShapeDtypeStruct((t, d), F32), jax.ShapeDtypeStruct((t, d), BF16),
                   jax.ShapeDtypeStruct((1, d), F32), jax.ShapeDtypeStruct((1, d), F32),
                   jax.ShapeDtypeStruct((8, 128), F32)],
        in_specs=[pl.BlockSpec((tm, d), row), pl.BlockSpec((tm, d), row), pl.BlockSpec((1, d), fixed),
                  pl.BlockSpec((1, d), fixed)],
        out_specs=[pl.BlockSpec((tm, d), row), pl.BlockSpec((tm, d), row), pl.BlockSpec((1, d), fixed),
                   pl.BlockSpec((1, d), fixed), pl.BlockSpec((8, 128), fixed)],
        compiler_params=_cparams("arbitrary"),
    )(z, target, ln_g, ln_b)


def _ln_bwd_call(z, dy, ln_g, bf16_scale, name, exch=()):
    t, d = z.shape
    tm = _tile(t, 512, 8)

    def body(z_ref, dy_ref, g_ref, dz_ref, dzb_ref, dg_ref, db_ref):
        @pl.when(pl.program_id(0) == 0)
        def _():
            dg_ref[...] = jnp.zeros_like(dg_ref)
            db_ref[...] = jnp.zeros_like(db_ref)

        xh, rstd = _ln_stats(z_ref[...])
        dy = dy_ref[...]
        dz = _ln_bwd(dy * g_ref[...], xh, rstd)
        dz_ref[...] = dz
        dzb_ref[...] = (bf16_scale * dz).astype(BF16)
        dg_ref[...] += jnp.sum(dy * xh, axis=0, keepdims=True)
        db_ref[...] += jnp.sum(dy, axis=0, keepdims=True)

    row = lambda i: (i, 0)
    fixed = lambda i: (0, 0)
    return _call(
        body, exch, name=name, grid=(t // tm,),
        out_shape=[jax.ShapeDtypeStruct((t, d), F32), jax.ShapeDtypeStruct((t, d), BF16),
                   jax.ShapeDtypeStruct((1, d), F32), jax.ShapeDtypeStruct((1, d), F32)],
        in_specs=[pl.BlockSpec((tm, d), row), pl.BlockSpec((tm, d), row), pl.BlockSpec((1, d), fixed)],
        out_specs=[pl.BlockSpec((tm, d), row), pl.BlockSpec((tm, d), row), pl.BlockSpec((1, d), fixed),
                   pl.BlockSpec((1, d), fixed)],
        semantics=("arbitrary",),
    )(z, dy, ln_g)


CONV_ROWS = 32
SUBLANES = 8


def _fill_shifted(ext, shifted):
    rows = ext.shape[0] - SUBLANES
    for s in range(1, SUBLANES):
        for r in range(0, rows, CONV_ROWS):
            n = min(CONV_ROWS, rows - r)
            shifted[s - 1, r:r + n, :] = ext[r + s:r + s + n, :]


def _window(ext, shifted, lo, n):
    s = lo % SUBLANES
    return ext[lo:lo + n, :] if s == 0 else shifted[s - 1, lo - s:lo - s + n, :]


def _mixer_fwd(proj, conv_w, conv_b, cln_g, cln_b, sln_g, sln_b, sg_wm, sg_bb, name, exch=()):
    t = proj.shape[0]
    tm = _tile(t, 256, CHUNK)
    hb = tm // HALO
    nc = tm // CHUNK
    ch = CONV_CH

    def body(av_ref, ag_ref, bu_ref, bv_ref, hv_ref, hg_ref, cw_ref, cb_ref, lg_ref, lb_ref, sg_ref, sb_ref,
             w_ref, bb_ref, y_ref, yt_ref, c_ref, ext, ext_s):
        i = pl.program_id(0)
        halo = hv_ref[...] * _sigmoid(hg_ref[...])
        ext[0:HALO, :] = jnp.where(i > 0, halo, 0.0)
        ext[HALO:HALO + tm, :] = av_ref[...] * _sigmoid(ag_ref[...])
        _fill_shifted(ext, ext_s)
        for r in range(0, tm, CONV_ROWS):
            acc = jnp.zeros((CONV_ROWS, ch), F32) + cb_ref[...]
            for k in range(CONV_TAPS):
                lo = r + k + HALO - (CONV_TAPS - 1)
                acc = acc + cw_ref[k:k + 1, :] * _window(ext, ext_s, lo, CONV_ROWS)
            c_ref[r:r + CONV_ROWS, :] = acc
        a = _ln(c_ref[...], lg_ref[...], lb_ref[...])
        ya = a * _sigmoid(a)
        y_ref[:, 0:ch] = ya.astype(BF16)
        yt_ref[0:ch, :] = ya.T.astype(BF16)
        for h in range(HEADS):
            sl = slice(h * HEAD_DIM, (h + 1) * HEAD_DIM)
            u, _ = _gelu_and_grad(bu_ref[:, sl])
            v, _ = _gelu_and_grad(bv_ref[:, sl])
            vn = _ln(v, sg_ref[h:h + 1, :], sb_ref[h:h + 1, :])
            vn3 = vn.astype(BF16).reshape(nc, CHUNK, HEAD_DIM)
            wb = jnp.broadcast_to(w_ref[h][None], (nc, CHUNK, CHUNK))
            mixed = jnp.einsum("cts,csd->ctd", wb, vn3, preferred_element_type=F32) + bb_ref[h][None]
            yb = u * mixed.reshape(tm, HEAD_DIM)
            y_ref[:, ch + h * HEAD_DIM:ch + (h + 1) * HEAD_DIM] = yb.astype(BF16)
            yt_ref[ch + h * HEAD_DIM:ch + (h + 1) * HEAD_DIM, :] = yb.T.astype(BF16)

    col = lambda cidx: (lambda i: (i, cidx))
    prev = lambda cidx: (lambda i: (jnp.maximum(i * hb - 1, 0), cidx))
    fix2 = lambda i: (0, 0)
    fix3 = lambda i: (0, 0, 0)
    return _call(
        body, exch, name=name, grid=(t // tm,),
        out_shape=[jax.ShapeDtypeStruct((t, 2 * ch), BF16), jax.ShapeDtypeStruct((2 * ch, t), BF16),
                   jax.ShapeDtypeStruct((t, ch), F32)],
        in_specs=[pl.BlockSpec((tm, ch), col(0)), pl.BlockSpec((tm, ch), col(1)), pl.BlockSpec((tm, ch), col(2)),
                  pl.BlockSpec((tm, ch), col(3)), pl.BlockSpec((HALO, ch), prev(0)), pl.BlockSpec((HALO, ch), prev(1)),
                  pl.BlockSpec((CONV_TAPS, ch), fix2), pl.BlockSpec((1, ch), fix2), pl.BlockSpec((1, ch), fix2),
                  pl.BlockSpec((1, ch), fix2), pl.BlockSpec((HEADS, HEAD_DIM), fix2), pl.BlockSpec((HEADS, HEAD_DIM), fix2),
                  pl.BlockSpec((HEADS, CHUNK, CHUNK), fix3), pl.BlockSpec((HEADS, CHUNK, HEAD_DIM), fix3)],
        out_specs=[pl.BlockSpec((tm, 2 * ch), lambda i: (i, 0)), pl.BlockSpec((2 * ch, tm), lambda i: (0, i)),
                   pl.BlockSpec((tm, ch), lambda i: (i, 0))],
        scratch_shapes=[pltpu.VMEM((HALO + tm, ch), F32), pltpu.VMEM((SUBLANES - 1, HALO + tm, ch), F32)],
        semantics=("parallel",),
    )(proj, proj, proj, proj, proj, proj, conv_w, conv_b, cln_g, cln_b, sln_g, sln_b, sg_wm, sg_bb)


def _mixer_bwd(proj, conv_c, dy, conv_w, cln_g, cln_b, sln_g, sln_b, sg_wm, sg_wmt, sg_bb, name, exch=()):
    t = proj.shape[0]
    tm = _tile(t, 256, CHUNK)
    hb = tm // HALO
    nc = tm // CHUNK
    nt = t // tm
    ch = CONV_CH
    last_halo = t // HALO - 1

    def body(av_ref, ag_ref, bu_ref, bv_ref, hv_ref, hg_ref, c_ref, cn_ref, dya_ref, dyan_ref, dyb_ref,
             cw_ref, lg_ref, lb_ref, sg_ref, sb_ref, w_ref, wt_ref, bb_ref,
             dp_ref, dcw_ref, dcb_ref, dlg_ref, dlb_ref, dsg_ref, dsb_ref, dw_ref, dbs_ref,
             ext_h, ext_dc, ext_hs, ext_dcs, acc_cw):
        i = pl.program_id(0)

        @pl.when(i == 0)
        def _():
            acc_cw[...] = jnp.zeros_like(acc_cw)
            for ref in (dcb_ref, dlg_ref, dlb_ref, dsg_ref, dsb_ref, dw_ref, dbs_ref):
                ref[...] = jnp.zeros_like(ref)

        lg = lg_ref[...]
        lb = lb_ref[...]

        def conv_ln_bwd(c, dya):
            xh, rstd = _ln_stats(c)
            a = xh * lg + lb
            da = dya * _silu_grad(a)
            return _ln_bwd(da * lg, xh, rstd), da, xh

        fold = lambda v: jnp.sum(v.reshape(CONV_ROWS // SUBLANES, SUBLANES, ch), axis=0)
        s_lg = s_lb = s_cb = jnp.zeros((SUBLANES, ch), F32)
        for r in range(0, tm, CONV_ROWS):
            dc, da, xh = conv_ln_bwd(c_ref[r:r + CONV_ROWS, :], dya_ref[r:r + CONV_ROWS, :])
            ext_dc[r:r + CONV_ROWS, :] = dc
            s_lg, s_lb, s_cb = s_lg + fold(da * xh), s_lb + fold(da), s_cb + fold(dc)
        dlg_ref[...] += jnp.sum(s_lg, axis=0, keepdims=True)
        dlb_ref[...] += jnp.sum(s_lb, axis=0, keepdims=True)
        dcb_ref[...] += jnp.sum(s_cb, axis=0, keepdims=True)
        dcn, _, _ = conv_ln_bwd(cn_ref[...], dyan_ref[...])
        ext_dc[tm:tm + HALO, :] = jnp.where(i < nt - 1, dcn, 0.0)
        halo = hv_ref[...] * _sigmoid(hg_ref[...])
        ext_h[0:HALO, :] = jnp.where(i > 0, halo, 0.0)
        ext_h[HALO:HALO + tm, :] = av_ref[...] * _sigmoid(ag_ref[...])
        _fill_shifted(ext_h, ext_hs)
        _fill_shifted(ext_dc, ext_dcs)
        for r in range(0, tm, CONV_ROWS):
            dcr = ext_dc[r:r + CONV_ROWS, :]
            acc = jnp.zeros((CONV_ROWS, ch), F32)
            for k in range(CONV_TAPS):
                lo = r + k + HALO - (CONV_TAPS - 1)
                prod = dcr * _window(ext_h, ext_hs, lo, CONV_ROWS)
                acc_cw[k] += jnp.sum(prod.reshape(CONV_ROWS // 8, 8, ch), axis=0)
                hi = r + (CONV_TAPS - 1) - k
                acc = acc + cw_ref[k:k + 1, :] * _window(ext_dc, ext_dcs, hi, CONV_ROWS)
            sg_r = _sigmoid(ag_ref[r:r + CONV_ROWS, :])
            av_r = av_ref[r:r + CONV_ROWS, :]
            dp_ref[r:r + CONV_ROWS, 0:ch] = (acc * sg_r).astype(BF16)
            dp_ref[r:r + CONV_ROWS, ch:2 * ch] = (acc * av_r * sg_r * (1.0 - sg_r)).astype(BF16)

        @pl.when(i == nt - 1)
        def _():
            dcw_ref[...] = jnp.sum(acc_cw[...], axis=1)

        tril = (lax.broadcasted_iota(jnp.int32, (CHUNK, CHUNK), 0)
                >= lax.broadcasted_iota(jnp.int32, (CHUNK, CHUNK), 1)).astype(F32)
        for h in range(HEADS):
            sl = slice(h * HEAD_DIM, (h + 1) * HEAD_DIM)
            u, du_dx = _gelu_and_grad(bu_ref[:, sl])
            v, dv_dx = _gelu_and_grad(bv_ref[:, sl])
            xhv, rstdv = _ln_stats(v)
            gh = sg_ref[h:h + 1, :]
            vn3 = (xhv * gh + sb_ref[h:h + 1, :]).astype(BF16).reshape(nc, CHUNK, HEAD_DIM)
            wb = jnp.broadcast_to(w_ref[h][None], (nc, CHUNK, CHUNK))
            mixed = jnp.einsum("cts,csd->ctd", wb, vn3, preferred_element_type=F32) + bb_ref[h][None]
            dyb = dyb_ref[:, sl]
            d_u = dyb * mixed.reshape(tm, HEAD_DIM)
            dm = dyb * u
            dm3 = dm.reshape(nc, CHUNK, HEAD_DIM)
            dbs_ref[h:h + 1, :] += jnp.sum(jnp.sum(dm3, axis=0).T, axis=0, keepdims=True)
            dm3b = dm3.astype(BF16)
            dw_h = jnp.sum(jnp.einsum("ctd,csd->cts", dm3b, vn3, preferred_element_type=F32), axis=0)
            dw_ref[h] += dw_h * tril
            wtb = jnp.broadcast_to(wt_ref[h][None], (nc, CHUNK, CHUNK))
            d_vn = jnp.einsum("cst,ctd->csd", wtb, dm3b, preferred_element_type=F32).reshape(tm, HEAD_DIM)
            dsg_ref[h:h + 1, :] += jnp.sum(d_vn * xhv, axis=0, keepdims=True)
            dsb_ref[h:h + 1, :] += jnp.sum(d_vn, axis=0, keepdims=True)
            dv = _ln_bwd(d_vn * gh, xhv, rstdv)
            dp_ref[:, 2 * ch + h * HEAD_DIM:2 * ch + (h + 1) * HEAD_DIM] = (d_u * du_dx).astype(BF16)
            dp_ref[:, 3 * ch + h * HEAD_DIM:3 * ch + (h + 1) * HEAD_DIM] = (dv * dv_dx).astype(BF16)

    col = lambda cidx: (lambda i: (i, cidx))
    prev = lambda cidx: (lambda i: (jnp.maximum(i * hb - 1, 0), cidx))
    nxt = lambda i: (jnp.minimum((i + 1) * hb, last_halo), 0)
    fix2 = lambda i: (0, 0)
    fix3 = lambda i: (0, 0, 0)
    out_shape = [jax.ShapeDtypeStruct((t, 4 * ch), BF16), jax.ShapeDtypeStruct((CONV_TAPS, ch), F32),
                 jax.ShapeDtypeStruct((1, ch), F32), jax.ShapeDtypeStruct((1, ch), F32), jax.ShapeDtypeStruct((1, ch), F32),
                 jax.ShapeDtypeStruct((HEADS, HEAD_DIM), F32), jax.ShapeDtypeStruct((HEADS, HEAD_DIM), F32),
                 jax.ShapeDtypeStruct((HEADS, CHUNK, CHUNK), F32), jax.ShapeDtypeStruct((HEADS, CHUNK), F32)]
    out_specs = [pl.BlockSpec((tm, 4 * ch), lambda i: (i, 0)), pl.BlockSpec((CONV_TAPS, ch), fix2),
                 pl.BlockSpec((1, ch), fix2), pl.BlockSpec((1, ch), fix2), pl.BlockSpec((1, ch), fix2),
                 pl.BlockSpec((HEADS, HEAD_DIM), fix2), pl.BlockSpec((HEADS, HEAD_DIM), fix2),
                 pl.BlockSpec((HEADS, CHUNK, CHUNK), fix3), pl.BlockSpec((HEADS, CHUNK), fix2)]
    in_specs = [pl.BlockSpec((tm, ch), col(0)), pl.BlockSpec((tm, ch), col(1)), pl.BlockSpec((tm, ch), col(2)),
                pl.BlockSpec((tm, ch), col(3)), pl.BlockSpec((HALO, ch), prev(0)), pl.BlockSpec((HALO, ch), prev(1)),
                pl.BlockSpec((tm, ch), col(0)), pl.BlockSpec((HALO, ch), nxt),
                pl.BlockSpec((tm, ch), col(0)), pl.BlockSpec((HALO, ch), nxt), pl.BlockSpec((tm, ch), col(1)),
                pl.BlockSpec((CONV_TAPS, ch), fix2), pl.BlockSpec((1, ch), fix2), pl.BlockSpec((1, ch), fix2),
                pl.BlockSpec((HEADS, HEAD_DIM), fix2), pl.BlockSpec((HEADS, HEAD_DIM), fix2),
                pl.BlockSpec((HEADS, CHUNK, CHUNK), fix3), pl.BlockSpec((HEADS, CHUNK, CHUNK), fix3),
                pl.BlockSpec((HEADS, CHUNK, HEAD_DIM), fix3)]
    return _call(
        body, exch, name=name, grid=(nt,), out_shape=out_shape, in_specs=in_specs, out_specs=out_specs,
        scratch_shapes=[pltpu.VMEM((HALO + tm, ch), F32), pltpu.VMEM((tm + HALO, ch), F32),
                        pltpu.VMEM((SUBLANES - 1, HALO + tm, ch), F32), pltpu.VMEM((SUBLANES - 1, tm + HALO, ch), F32),
                        pltpu.VMEM((CONV_TAPS, 8, ch), F32)],
        semantics=("arbitrary",),
    )(proj, proj, proj, proj, proj, proj, conv_c, conv_c, dy, dy, dy,
      conv_w, cln_g, cln_b, sln_g, sln_b, sg_wm, sg_wmt, sg_bb)


def _pair_sum(parts, from_sibling, core_chip, name):
    _, r, cc = parts.shape
    tr = _tile(r, max(16, (1 << 20) // (2 * cc)), 16)

    def body(cc_ref, p_ref, s_ref, o_ref, own_ref):
        q = (p_ref[...].astype(F32) + s_ref[...].astype(F32)).astype(BF16)
        o_ref[...] = q

        @pl.when(pl.program_id(1) == cc_ref[1])
        def _():
            own_ref[...] = q[0]

    grid_spec = pltpu.PrefetchScalarGridSpec(
        num_scalar_prefetch=1, grid=(r // tr, 4),
        in_specs=[pl.BlockSpec((1, tr, cc), lambda i, j, cc_ref: (2 * j + cc_ref[0], i, 0)),
                  pl.BlockSpec((1, tr, cc), lambda i, j, cc_ref: (j, i, 0))],
        out_specs=[pl.BlockSpec((1, tr, cc), lambda i, j, cc_ref: (j, i, 0)),
                   pl.BlockSpec((tr, cc), lambda i, j, cc_ref: (i, 0))])
    return pl.pallas_call(
        body, name=name, grid_spec=grid_spec,
        out_shape=[jax.ShapeDtypeStruct((4, r, cc), BF16), jax.ShapeDtypeStruct((r, cc), BF16)],
        compiler_params=_cparams("parallel", "arbitrary"),
    )(core_chip, parts, from_sibling)


def _adamw_math(w, g, m, v):
    m = ADAM_B1 * m + (1.0 - ADAM_B1) * g
    v = ADAM_B2 * v + (1.0 - ADAM_B2) * (g * g)
    m_hat = m / (1.0 - ADAM_B1 ** ADAM_STEP)
    v_hat = v / (1.0 - ADAM_B2 ** ADAM_STEP)
    delta = -ADAM_LR * (m_hat / (jnp.sqrt(v_hat) + ADAM_EPS) + ADAM_WD * w)
    return delta, m, v


def _adamw_tile(in_refs, out_refs):
    w_ref, m_ref, v_ref, q_ref, o_ref = in_refs
    g = q_ref[...].astype(F32)
    for k in range(3):
        g = g + o_ref[k].astype(F32)
    d, mm, vv = _adamw_math(w_ref[...], g, m_ref[...], v_ref[...])
    for ref, val in zip(out_refs, (g, d, mm, vv)):
        ref[...] = val


def _adamw_side(w, m, v, chip_part, from_chips, max_tiles):
    r, cc = w.shape
    n = max(k for k in range(1, max_tiles + 1) if r % k == 0 and (r // k) % 16 == 0)
    tr = r // n
    row = ((tr, cc), lambda s: (s, 0))
    return _Side([w, m, v, chip_part, from_chips], [row, row, row, row, ((3, tr, cc), lambda s: (0, s, 0))],
                 [jax.ShapeDtypeStruct((r, cc), F32)] * 4, [row] * 4, n, _adamw_tile)


def _adamw_sharded(w, m, v, chip_part, from_chips, name):
    r, cc = w.shape
    tr = _tile(r, max(16, (1 << 19) // (4 * cc) * 2), 16)

    def body(*refs):
        _adamw_tile(refs[:5], refs[5:])

    row = pl.BlockSpec((tr, cc), lambda i: (i, 0))
    return pl.pallas_call(
        body, name=name, grid=(r // tr,), out_shape=[jax.ShapeDtypeStruct((r, cc), F32)] * 4,
        in_specs=[row, row, row, row, pl.BlockSpec((3, tr, cc), lambda i: (0, i, 0))], out_specs=[row] * 4,
        compiler_params=_cparams("parallel"),
    )(w, m, v, chip_part, from_chips)


def _adamw_small(w, g, m, v, name):
    r, cc = w.shape

    def body(w_ref, g_ref, m_ref, v_ref, d_out, m_out, v_out):
        d, mm, vv = _adamw_math(w_ref[...], g_ref[...], m_ref[...], v_ref[...])
        d_out[...] = d
        m_out[...] = mm
        v_out[...] = vv

    full = pl.BlockSpec((r, cc), lambda i: (0, 0))
    return pl.pallas_call(
        body, name=name, grid=(1,), out_shape=[jax.ShapeDtypeStruct((r, cc), F32)] * 3,
        in_specs=[full] * 4, out_specs=[full] * 3, compiler_params=_cparams("arbitrary"),
    )(w, g, m, v)


SMALL = ("ln1_g", "ln1_b", "conv_b", "conv_ln_g", "conv_ln_b", "sg_ln_g", "sg_ln_b", "sg_w", "sg_b",
         "ln2_g", "ln2_b", "ln3_g", "ln3_b")
ORDER = ("ffn1_w_gate_up", "ffn1_w_down", "ln1_g", "ln1_b", "mix_w_in", "conv_w", "conv_b", "conv_ln_g", "conv_ln_b",
         "sg_ln_g", "sg_ln_b", "sg_w", "sg_b", "mix_w_out", "ln2_g", "ln2_b", "ffn2_w_gate_up", "ffn2_w_down",
         "ln3_g", "ln3_b")


def _rows128(a):
    return a.reshape(-1, 128)


def kernel(x, ffn1_w_gate_up, ffn1_w_down, ln1_g, ln1_b, mix_w_in, conv_w, conv_b, conv_ln_g, conv_ln_b, sg_ln_g, sg_ln_b, sg_w, sg_b, mix_w_out, ln2_g, ln2_b, ffn2_w_gate_up, ffn2_w_down, ln3_g, ln3_b, loss_target, m_ffn1_w_gate_up, m_ffn1_w_down, m_ln1_g, m_ln1_b, m_mix_w_in, m_conv_w, m_conv_b, m_conv_ln_g, m_conv_ln_b, m_sg_ln_g, m_sg_ln_b, m_sg_w, m_sg_b, m_mix_w_out, m_ln2_g, m_ln2_b, m_ffn2_w_gate_up, m_ffn2_w_down, m_ln3_g, m_ln3_b, v_ffn1_w_gate_up, v_ffn1_w_down, v_ln1_g, v_ln1_b, v_mix_w_in, v_conv_w, v_conv_b, v_conv_ln_g, v_conv_ln_b, v_sg_ln_g, v_sg_ln_b, v_sg_w, v_sg_b, v_mix_w_out, v_ln2_g, v_ln2_b, v_ffn2_w_gate_up, v_ffn2_w_down, v_ln3_g, v_ln3_b):
    args = dict(locals())
    w = {n: args[n][0] for n in ORDER}
    mom = {n: args["m_" + n][0] for n in ORDER}
    var = {n: args["v_" + n][0] for n in ORDER}
    x0 = x[0]
    target = loss_target[0]
    t, d = x0.shape
    my_x, my_y, my_c = lax.axis_index("x"), lax.axis_index("y"), lax.axis_index("c")
    my_chip = (2 * my_x + my_y).astype(jnp.int32).reshape(1)
    my_core = my_c.astype(jnp.int32).reshape(1)
    me = 4 * my_x + 2 * my_y + my_c

    big = ("ffn1_w_gate_up", "ffn1_w_down", "mix_w_in", "mix_w_out", "ffn2_w_gate_up", "ffn2_w_down")
    sh = {n: w[n].astype(BF16) for n in big}
    f2s = sh["ffn2_w_gate_up"].shape[1]
    order = jnp.stack([4 * p[0] + 2 * p[1] + p[2] for p in _visit_order(my_x, my_y, my_c)]).astype(jnp.int32)
    gu1, x0t, (wgu1, wd1, conv_w_all) = _gather_and_gate_up(
        x0, [sh["ffn1_w_gate_up"], sh["ffn1_w_down"], w["conv_w"]], [True, True, False], order, "ffn1_gate_up_fwd")
    wd1 = wd1.reshape(-1, d)
    conv_w_full = jnp.transpose(conv_w_all, (1, 0, 2)).reshape(CONV_TAPS, CONV_CH)
    tril = jnp.tril(jnp.ones((CHUNK, CHUNK), F32))
    sg_wm = w["sg_w"] * tril
    sg_wm_b = sg_wm.astype(BF16)
    sg_wmt_b = jnp.swapaxes(sg_wm, 1, 2).astype(BF16)
    sg_bb = jnp.broadcast_to(w["sg_b"][:, :, None], (HEADS, CHUNK, HEAD_DIM))
    row = lambda a: a.reshape(1, -1)

    d2 = [sh["ffn2_w_down"]]
    d2_first = d2[0].shape[0] * 3 // 5 // 16 * 16
    d2_top, d2_bottom = (0, d2_first), (d2_first, d2[0].shape[0] - d2_first)
    (h1t, z1, x1), ((g_in, g_out), (g_d2,)) = _ffn_down_fwd(
        gu1, x0, wd1, row(w["ln1_g"]), row(w["ln1_b"]), "ffn1_down_fwd",
        exch=[_gather_first([sh["mix_w_in"], sh["mix_w_out"]], [True, False]),
              _gather_first(d2, [False], rows=d2_top)])
    in_cols = sh["mix_w_in"].shape[1]
    x1t, ((w_in, w_out), (g_d2,)) = _transpose_bf16(
        x1, "x1_transpose", exch=[_gather_forward([g_in, g_out], [True, False], [in_cols, None]),
                                  _gather_forward([g_d2], [False], [None], rows=d2_top)])
    w_out = w_out.reshape(-1, d)
    top, bottom = (0, d // 2), (d // 2, d // 2)
    gu2 = [sh["ffn2_w_gate_up"]]
    proj, ((g_gu2,),) = _mix_in_proj(x1, w_in, "mix_in_fwd", exch=[_gather_first(gu2, [True], rows=top)])
    (y, yt, conv_c), ((g_gu2,),) = _mixer_fwd(
        proj, conv_w_full, row(w["conv_b"]), row(w["conv_ln_g"]), row(w["conv_ln_b"]),
        w["sg_ln_g"], w["sg_ln_b"], sg_wm_b, sg_bb, "mixer_fwd",
        exch=[_both(_gather_first(gu2, [True], rows=bottom, into=[g_gu2]),
                    _gather_forward([g_gu2], [True], [f2s], rows=top))])
    (z2, x2, x2t), ((wgu2,), (g_d2,)) = _mix_out_fwd(
        y, w_out, x1, row(w["ln2_g"]), row(w["ln2_b"]), "mix_out_fwd",
        exch=[_gather_forward([g_gu2], [True], [f2s], rows=bottom),
              _gather_first(d2, [False], rows=d2_bottom, into=[g_d2])])
    (wd2,) = _exchange_alone(_gather_forward([g_d2], [False], [None], rows=d2_bottom), "ffn2_down_gather_forward")
    wd2 = wd2.reshape(-1, d)
    (g2, u2, h2t, z3), _ = _ffn_fwd(x2, wgu2, wd2, row(w["ln3_g"]), row(w["ln3_b"]), "ffn2_fwd", with_ln=False)

    f = wd1.shape[0]
    dn = _tile(d, 1024, 128)
    grads = {}
    core_chip = jnp.concatenate([my_core, my_chip])
    pair = lambda p, s, label: _pair_sum(p, s, core_chip, "pair_sum_" + label)
    adamw = lambda n, own, got, steps: _adamw_side(w[n], mom[n], var[n], own, got, steps)
    m_tiles = d // _tile(d, 512, 16)
    out = {}
    dz3, do2, grads["ln3_g"], grads["ln3_b"], loss_tile = _loss_ln_bwd(
        z3, target, row(w["ln3_g"]), row(w["ln3_b"]), 0.5, "loss_ln3_bwd")
    p_d2, _ = _weight_grad(h2t, do2, dn, 512, "ffn2_dw_down")
    p_d2 = p_d2.reshape(N_DEV, f // N_DEV, d)
    (dg2, du2, dx2), ((s_d2,),) = _ffn_bwd(dz3, do2, g2, u2, wgu2, wd2, "ffn2_bwd", exch=[_rs_sibling([p_d2])])
    q_d2, own_d2 = pair(p_d2, s_d2, "ffn2_down")
    p_gu2, ((r_d2,),) = _weight_grad(x2t, dg2, f2s, 512, "ffn2_dw_gate", blocks=N_DEV, exch=[_rs_chips([q_d2])])
    p_gu2, _ = _weight_grad(x2t, du2, f2s, 512, "ffn2_dw_up", blocks=N_DEV, block_offset=4, into=p_gu2)
    (dz2, dz2b, grads["ln2_g"], grads["ln2_b"]), ((s_gu2,),) = _ln_bwd_call(
        z2, dx2, row(w["ln2_g"]), 1.0, "ln2_bwd", exch=[_rs_sibling([p_gu2])])
    q_gu2, own_gu2 = pair(p_gu2, s_gu2, "ffn2_gate_up")
    dy = _mix_out_bwd(dz2b, w_out, "mix_out_bwd")
    p_out, _ = _weight_grad(yt, dz2b, dn, 512, "mix_out_dw")
    p_out = p_out.reshape(N_DEV, -1, d)
    (dproj, grads["conv_w"], grads["conv_b"], grads["conv_ln_g"], grads["conv_ln_b"], grads["sg_ln_g"],
     grads["sg_ln_b"], grads["sg_w"], grads["sg_b"]), ((r_gu2,),) = _mixer_bwd(
        proj, conv_c, dy, conv_w_full, row(w["conv_ln_g"]), row(w["conv_ln_b"]), w["sg_ln_g"], w["sg_ln_b"],
        sg_wm_b, sg_wmt_b, sg_bb, "mixer_bwd", exch=[_rs_chips([q_gu2], rows=top)])
    dx1, ((s_out,), (r_gu2,)) = _mix_in_bwd(
        dproj, w_in, dz2, "mix_in_bwd", exch=[_rs_sibling([p_out]), _rs_chips([q_gu2], rows=bottom, into=[r_gu2])])
    p_in, (out["ffn2_w_gate_up"], out["ffn2_w_down"]) = _weight_grad(
        x1t, dproj, in_cols, 512, "mix_in_dw", blocks=N_DEV,
        exch=[adamw("ffn2_w_gate_up", own_gu2, r_gu2, N_DEV * m_tiles), adamw("ffn2_w_down", own_d2, r_d2, N_DEV * m_tiles)])
    (dz1, do1, grads["ln1_g"], grads["ln1_b"]), ((s_in,),) = _ln_bwd_call(
        z1, dx1, row(w["ln1_g"]), 0.5, "ln1_bwd", exch=[_rs_sibling([p_in])])
    q_out, own_out = pair(p_out, s_out, "mix_out")
    q_in, own_in = pair(p_in, s_in, "mix_in")
    small_parts = [_rows128(grads[n]) for n in SMALL]
    packed = jnp.concatenate(small_parts + [_rows128(grads["conv_w"]), loss_tile], axis=0)
    p_d1, ((r_in,),) = _weight_grad(h1t, do1, dn, 512, "ffn1_dw_down", exch=[_rs_chips([q_in])])
    p_d1 = p_d1.reshape(N_DEV, f // N_DEV, d)
    (dg1, du1), ((s_d1,), (r_out,), (small_all,)) = _ffn_bwd_act(
        do1, gu1, wd1, "ffn1_bwd_act",
        exch=[_rs_sibling([p_d1]), _rs_chips([q_out]), _small_gather(packed)])
    q_d1, own_d1 = pair(p_d1, s_d1, "ffn1_down")
    p_gu1, ((r_d1,),) = _weight_grad(x0t, dg1, f2s, 512, "ffn1_dw_gate", blocks=N_DEV, exch=[_rs_chips([q_d1])])
    p_gu1, (out["mix_w_in"], out["mix_w_out"]) = _weight_grad(
        x0t, du1, f2s, 512, "ffn1_dw_up", blocks=N_DEV, block_offset=4, into=p_gu1,
        exch=[adamw("mix_w_in", own_in, r_in, 4 * m_tiles), adamw("mix_w_out", own_out, r_out, 4 * m_tiles)])
    (s_gu1,) = _exchange_alone(_rs_sibling([p_gu1]), "ffn1_gate_up_sibling_exchange")
    q_gu1, own_gu1 = pair(p_gu1, s_gu1, "ffn1_gate_up")
    (grad_x,), ((r_gu1,),) = _ffn_bwd_dx(dz1, dg1, du1, wgu1, "ffn1_bwd_dx", exch=[_rs_chips([q_gu1])])
    for n, own, got in (("ffn1_w_down", own_d1, r_d1), ("ffn1_w_gate_up", own_gu1, r_gu1)):
        out[n] = _adamw_sharded(w[n], mom[n], var[n], own, got, "adamw_" + n)

    cw_rows = CONV_TAPS * CONV_CH // 128
    total = _sum_over_devices(small_all)
    offs = [0]
    for p in small_parts:
        offs.append(offs[-1] + p.shape[0])
    n_small = offs[-1]
    loss = total[n_small + cw_rows, 0]
    g_conv_w = lax.dynamic_slice_in_dim(total[n_small:n_small + cw_rows].reshape(CONV_TAPS, CONV_CH),
                                        me * (CONV_CH // N_DEV), CONV_CH // N_DEV, axis=1)
    pad8 = lambda a: jnp.pad(a, ((0, -a.shape[0] % 8), (0, 0)))
    pack = lambda tree, cw: jnp.concatenate([_rows128(tree[n]) for n in SMALL] + [pad8(cw)], axis=0)
    g_pack = jnp.concatenate([total[:n_small], pad8(g_conv_w)], axis=0)
    d_pack, m_pack, v_pack = _adamw_small(pack(w, w["conv_w"]), g_pack, pack(mom, mom["conv_w"]),
                                          pack(var, var["conv_w"]), "adamw_small")
    for k, n in enumerate(SMALL):
        sl = slice(offs[k], offs[k + 1])
        shp = w[n].shape
        out[n] = (total[sl].reshape(shp), d_pack[sl].reshape(shp), m_pack[sl].reshape(shp), v_pack[sl].reshape(shp))
    sl = slice(n_small, n_small + CONV_TAPS)
    out["conv_w"] = (g_conv_w, d_pack[sl], m_pack[sl], v_pack[sl])

    lead = lambda a: a[None]
    res = [loss, grad_x[None]]
    for kind in range(4):
        res += [lead(out[n][kind]) for n in ORDER]
    return tuple(res)
```

```python
import functools
import math

import jax
import jax.numpy as jnp
from jax import lax
from jax.experimental import pallas as pl
from jax.experimental.pallas import tpu as pltpu

F32, BF16 = jnp.float32, jnp.bfloat16
MESH = pl.DeviceIdType.MESH
ANY = pl.BlockSpec(memory_space=pl.ANY)

N_DEV = 8
LN_EPS = 1e-5
ALPHA = 2.0 ** 0.25
CONV_CH = 1024
CONV_TAPS = 31
HALO = 32
HEADS = 8
HEAD_DIM = 128
CHUNK = 128
ADAM_LR, ADAM_B1, ADAM_B2, ADAM_EPS, ADAM_WD, ADAM_STEP = 0.001, 0.9, 0.999, 1e-08, 0.01, 10
V7X_VMEM_LIMIT = 56 * 2 ** 20

def _cparams(*sem):
    return pltpu.CompilerParams(dimension_semantics=sem, vmem_limit_bytes=V7X_VMEM_LIMIT)


def _tile(n, pref, mult):
    best = None
    for t in range(mult, min(n, pref) + 1, mult):
        if n % t == 0:
            best = t
    return best if best is not None else n


def _dot(a, b):
    return jnp.dot(a, b, preferred_element_type=F32)


def _dot_nt(a, b):
    return lax.dot_general(a, b, (((1,), (1,)), ((), ())), preferred_element_type=F32)


def _sigmoid(x):
    return 1.0 / (1.0 + jnp.exp(-x))


def _ln_stats(z):
    mu = jnp.mean(z, axis=-1, keepdims=True)
    zc = z - mu
    var = jnp.mean(zc * zc, axis=-1, keepdims=True)
    rstd = lax.rsqrt(var + LN_EPS)
    return zc * rstd, rstd


def _ln(z, g, b):
    xh, _ = _ln_stats(z)
    return xh * g + b


def _ln_bwd(dxh, xh, rstd):
    m1 = jnp.mean(dxh, axis=-1, keepdims=True)
    m2 = jnp.mean(dxh * xh, axis=-1, keepdims=True)
    return rstd * (dxh - m1 - xh * m2)


_GK = math.sqrt(2.0 / math.pi)
_GA = 0.044715


def _gelu_and_grad(x):
    x2 = x * x
    t = jnp.tanh(_GK * (x + _GA * x * x2))
    y = 0.5 * x * (1.0 + t)
    dy = 0.5 * (1.0 + t) + 0.5 * x * (1.0 - t * t) * (_GK * (1.0 + 3.0 * _GA * x2))
    return y, dy


def _silu_grad(a):
    s = _sigmoid(a)
    return s * (1.0 + a * (1.0 - s))


def _place():
    return lax.axis_index("x"), lax.axis_index("y"), lax.axis_index("c")


def _other_chips(x, y):
    return [(1 - x, y), (x, 1 - y), (1 - x, 1 - y)]


def _visit_order(x, y, c):
    chips = _other_chips(x, y)
    return [(x, y, c), (x, y, 1 - c), (*chips[0], c), (*chips[1], c), (*chips[0], 1 - c), (*chips[1], 1 - c),
            (*chips[2], c), (*chips[2], 1 - c)]


def _gather_and_gate_up(xb, shards, relayed, order, name):
    n = len(shards)
    N_COPIES = 10
    t, d = xb.shape
    cols = shards[0].shape[1]
    tm = _tile(t, 512, 128)
    ni = t // tm
    col_major = [True] + [False] * (n - 1)

    def body(order_ref, x_ref, *refs):
        srcs, gu_ref, xt_ref, dsts = refs[:n], refs[n], refs[n + 1], refs[n + 2:2 * n + 2]
        wbuf, send_sems, recv_sems, local_sems, load_sem = refs[2 * n + 2:]
        b, i = pl.program_id(0), pl.program_id(1)
        x, y, c = _place()
        me, sib = (x, y, c), (x, y, 1 - c)
        chips = _other_chips(x, y)

        near_x, near_y, far = chips

        def slot(w, p, band=None):
            half = shards[w].shape[0] // 2
            rows = None if band is None else (band * half, half)
            return _block_slot(dsts[w], col_major[w], shards[w].shape[1], p, rows)

        def copy(w, s, block, to, band=None, from_src=False):
            return pltpu.make_async_remote_copy(
                src_ref=srcs[w] if from_src else slot(w, block, band), dst_ref=slot(w, block, band),
                send_sem=send_sems.at[N_COPIES * w + s], recv_sem=recv_sems.at[N_COPIES * w + s],
                device_id=to, device_id_type=MESH)

        def own(w):
            return pltpu.make_async_copy(srcs[w], slot(w, me), local_sems.at[w])

        def sends(w):
            out = [copy(w, 0, me, sib, from_src=True), copy(w, 1, me, (*near_x, c), from_src=True),
                   copy(w, 2, me, (*near_y, c), from_src=True)]
            if not relayed[w]:
                out.append(copy(w, 3, me, (*far, c), from_src=True))
            return out

        def passed_on(w):
            out = [copy(w, 4, (*near_x, c), sib), copy(w, 5, (*near_y, c), sib)]
            if relayed[w]:
                out += [copy(w, 6, (*far, c), sib, band=0), copy(w, 9, (*far, c), sib, band=1),
                        copy(w, 7, (*near_x, c), (*near_y, c), band=0), copy(w, 8, (*near_y, c), (*near_x, c), band=1)]
            else:
                out.append(copy(w, 6, (*far, c), sib))
            return out

        def start_sends(w):
            own(w).start()
            for cp in sends(w):
                cp.start()

        def got_near_x(w):
            copy(w, 1, (*near_x, c), me).wait_recv()
            copy(w, 4, (*near_x, c), sib).start()
            if relayed[w]:
                copy(w, 7, (*near_x, c), (*near_y, c), band=0).start()

        def got_near_y(w):
            copy(w, 2, (*near_y, c), me).wait_recv()
            copy(w, 5, (*near_y, c), sib).start()
            if relayed[w]:
                copy(w, 8, (*near_y, c), (*near_x, c), band=1).start()

        def got_far(w):
            if relayed[w]:
                copy(w, 7, (*far, c), me, band=0).wait_recv()
                copy(w, 6, (*far, c), sib, band=0).start()
                copy(w, 8, (*far, c), me, band=1).wait_recv()
                copy(w, 9, (*far, c), sib, band=1).start()
            else:
                copy(w, 3, (*far, c), me).wait_recv()
                copy(w, 6, (*far, c), sib).start()

        def got_from_sibling(w, which):
            if which == 0:
                copy(w, 0, sib, me).wait_recv()
            elif which == 3 and relayed[w]:
                copy(w, 6, (*far, 1 - c), me, band=0).wait_recv()
                copy(w, 9, (*far, 1 - c), me, band=1).wait_recv()
            else:
                copy(w, 3 + which, (*chips[which - 1], 1 - c), me).wait_recv()

        others = range(1, n)

        def arrive(k):
            if k == 0:
                own(0).wait()
            elif k == 1:
                got_from_sibling(0, 0)
            elif k == 2:
                got_near_x(0)
                for w in others:
                    start_sends(w)
            elif k == 3:
                got_near_y(0)
            elif k in (4, 5):
                got_from_sibling(0, k - 3)
            elif k == 6:
                got_far(0)
                for w in others:
                    got_near_x(w)
                    got_near_y(w)
            else:
                got_from_sibling(0, 3)
                for w in others:
                    got_far(w)

        @pl.when((b == 0) & (i == 0))
        def _():
            start_sends(0)

        for k in range(N_DEV):
            @pl.when((b == k) & (i == 0))
            def _(k=k):
                arrive(k)
                at = pl.multiple_of(order_ref[k] * cols, 128)
                load = pltpu.make_async_copy(dsts[0].at[:, pl.ds(at, cols)], wbuf, load_sem.at[0])
                load.start()
                load.wait()

        gu_ref[...] = _dot(x_ref[...].astype(BF16), wbuf[...]).astype(BF16)

        @pl.when(b == 0)
        def _():
            xt_ref[...] = x_ref[...].T.astype(BF16)

        @pl.when((b == N_DEV - 1) & (i == ni - 1))
        def _():
            for w in others:
                for which in range(4):
                    got_from_sibling(w, which)
                own(w).wait()
            for w in range(n):
                for cp in sends(w) + passed_on(w):
                    cp.wait_send()

    grid_spec = pltpu.PrefetchScalarGridSpec(
        num_scalar_prefetch=1, grid=(N_DEV, ni),
        in_specs=[pl.BlockSpec((tm, d), lambda b, i, o: (i, 0))] + [ANY] * n,
        out_specs=[pl.BlockSpec((tm, cols), lambda b, i, o: (i, o[b])),
                   pl.BlockSpec((d, tm), lambda b, i, o: (0, jnp.where(b == 0, i, ni - 1)))] + [ANY] * n,
        scratch_shapes=[pltpu.VMEM((d, cols), BF16), pltpu.SemaphoreType.DMA((N_COPIES * n,)),
                        pltpu.SemaphoreType.DMA((N_COPIES * n,)), pltpu.SemaphoreType.DMA((n,)),
                        pltpu.SemaphoreType.DMA((1,))])
    res = pl.pallas_call(
        body, name=name, grid_spec=grid_spec,
        out_shape=[jax.ShapeDtypeStruct((t, N_DEV * cols), BF16), jax.ShapeDtypeStruct((d, t), BF16)]
        + [_gathered_shape(s, cm) for s, cm in zip(shards, col_major)],
        compiler_params=_cparams("arbitrary", "arbitrary"),
    )(order, xb, *shards)
    return res[0], res[1], res[2:]


class _Exchange:
    def __init__(self, ins, io, new, n_sems, n_local, make):
        self.ins, self.io, self.new = list(ins), list(io), list(new)
        self.n_sems, self.n_local, self.make = n_sems, n_local, make


def _block_slot(ref, col_major, cols, place, rows=None):
    k = 4 * place[0] + 2 * place[1] + place[2]
    band = slice(None) if rows is None else pl.ds(rows[0], rows[1])
    if col_major:
        return ref.at[band, pl.ds(pl.multiple_of(k * cols, 128), cols)]
    return ref.at[k] if rows is None else ref.at[k, band]


def _gathered_shape(s, col_major):
    return jax.ShapeDtypeStruct((s.shape[0], N_DEV * s.shape[1]) if col_major else (N_DEV,) + s.shape, s.dtype)


def _gather_first(shards, col_major, rows=None, into=None):
    n = len(shards)
    new = [] if into is not None else [_gathered_shape(s, cm) for s, cm in zip(shards, col_major)]

    def make(in_refs, io_refs, new_refs, send_sems, recv_sems, local_sems, base=0, local_base=0):
        x, y, c = _place()
        targets = [(x, y, 1 - c)] + [(*chip, c) for chip in _other_chips(x, y)]
        gathered = io_refs if into is not None else new_refs
        copies = []
        for w in range(n):
            src = in_refs[w] if rows is None else in_refs[w].at[pl.ds(rows[0], rows[1])]
            slot = _block_slot(gathered[w], col_major[w], shards[w].shape[1], (x, y, c), rows)
            copies.append(pltpu.make_async_copy(src, slot, local_sems.at[local_base + w]))
            for s, to in enumerate(targets):
                copies.append(pltpu.make_async_remote_copy(
                    src_ref=src, dst_ref=slot, send_sem=send_sems.at[base + 4 * w + s],
                    recv_sem=recv_sems.at[base + 4 * w + s], device_id=to, device_id_type=MESH))
        return copies

    return _Exchange(shards, into or [], new, 4 * n, n, make)


def _gather_forward(gathered, col_major, cols, rows=None):
    n = len(gathered)

    def make(in_refs, io_refs, new_refs, send_sems, recv_sems, local_sems, base=0, local_base=0):
        x, y, c = _place()
        copies = []
        for w in range(n):
            for j, chip in enumerate(_other_chips(x, y)):
                slot = _block_slot(io_refs[w], col_major[w], cols[w], (*chip, c), rows)
                copies.append(pltpu.make_async_remote_copy(
                    src_ref=slot, dst_ref=slot, send_sem=send_sems.at[base + 3 * w + j],
                    recv_sem=recv_sems.at[base + 3 * w + j], device_id=(x, y, 1 - c), device_id_type=MESH))
        return copies

    return _Exchange([], gathered, [], 3 * n, 0, make)


def _both(a, b):
    def make(in_refs, io_refs, new_refs, send_sems, recv_sems, local_sems):
        na = len(a.ins)
        return (a.make(in_refs[:na], io_refs, [], send_sems, recv_sems, local_sems, 0, 0)
                + b.make(in_refs[na:], io_refs, [], send_sems, recv_sems, local_sems, a.n_sems, a.n_local))

    return _Exchange(a.ins + b.ins, a.io, [], a.n_sems + b.n_sems, a.n_local + b.n_local, make)


def _rs_sibling(parts):
    n = len(parts)

    def make(in_refs, io_refs, new_refs, send_sems, recv_sems, local_sems):
        x, y, c = _place()
        copies = []
        for w in range(n):
            for j in range(4):
                copies.append(pltpu.make_async_remote_copy(
                    src_ref=in_refs[w].at[2 * j + (1 - c)], dst_ref=new_refs[w].at[j],
                    send_sem=send_sems.at[4 * w + j], recv_sem=recv_sems.at[4 * w + j],
                    device_id=(x, y, 1 - c), device_id_type=MESH))
        return copies

    return _Exchange(parts, [], [jax.ShapeDtypeStruct((4,) + p.shape[1:], p.dtype) for p in parts], 4 * n, 0, make)


def _rs_chips(chip_parts, rows=None, into=None):
    n = len(chip_parts)
    band = slice(None) if rows is None else pl.ds(rows[0], rows[1])
    new = [] if into is not None else [jax.ShapeDtypeStruct((3,) + p.shape[1:], p.dtype) for p in chip_parts]

    def make(in_refs, io_refs, new_refs, send_sems, recv_sems, local_sems):
        x, y, c = _place()
        landing = io_refs if into is not None else new_refs
        copies = []
        for w in range(n):
            for rel, (px, py) in enumerate(_other_chips(x, y)):
                copies.append(pltpu.make_async_remote_copy(
                    src_ref=in_refs[w].at[2 * px + py, band], dst_ref=landing[w].at[rel, band],
                    send_sem=send_sems.at[3 * w + rel], recv_sem=recv_sems.at[3 * w + rel],
                    device_id=(px, py, c), device_id_type=MESH))
        return copies

    return _Exchange(chip_parts, into or [], new, 3 * n, 0, make)


class _Side:
    def __init__(self, ins, in_blocks, out_shapes, out_blocks, n_tiles, fn):
        self.ins, self.in_blocks, self.out_shapes, self.out_blocks = list(ins), in_blocks, list(out_shapes), out_blocks
        self.n_tiles, self.fn = n_tiles, fn


def _call(body, exch, *, name, grid, in_specs, out_specs, out_shape, scratch_shapes=(), semantics,
          input_output_aliases=None):
    exch = list(exch)
    in_specs, out_specs, out_shape = list(in_specs), list(out_specs), list(out_shape)
    scratch_shapes = list(scratch_shapes)
    if not exch:
        fn = pl.pallas_call(body, name=name, grid=grid, in_specs=in_specs, out_specs=out_specs, out_shape=out_shape,
                            scratch_shapes=scratch_shapes, input_output_aliases=input_output_aliases or {},
                            compiler_params=_cparams(*semantics))
        return lambda *args: (fn(*args), [])
    n_in, n_out, n_scr = len(in_specs), len(out_specs), len(scratch_shapes)
    aliases = dict(input_output_aliases or {})
    all_in, all_out_specs, all_out_shape, all_scr = list(in_specs), list(out_specs), list(out_shape), list(scratch_shapes)
    extra_args = []

    def step(idx):
        s = idx[0]
        for a in range(1, len(grid)):
            s = s * grid[a] + idx[a]
        return s

    def tile_spec(shape, where, n_tiles):
        return pl.BlockSpec(shape, lambda *idx: where(jnp.minimum(step(idx), n_tiles - 1)))

    for ex in exch:
        if isinstance(ex, _Side):
            all_in += [tile_spec(shape, where, ex.n_tiles) for shape, where in ex.in_blocks]
            extra_args += ex.ins
            all_out_specs += [tile_spec(shape, where, ex.n_tiles) for shape, where in ex.out_blocks]
            all_out_shape += ex.out_shapes
            continue
        for k, a in enumerate(ex.io):
            aliases[len(all_in) + len(ex.ins) + k] = len(all_out_specs) + k
        all_in += [ANY] * (len(ex.ins) + len(ex.io))
        extra_args += ex.ins + ex.io
        all_out_specs += [ANY] * (len(ex.io) + len(ex.new))
        all_out_shape += [jax.ShapeDtypeStruct(a.shape, a.dtype) for a in ex.io] + ex.new
        all_scr += [pltpu.SemaphoreType.DMA((ex.n_sems,)), pltpu.SemaphoreType.DMA((ex.n_sems,)),
                    pltpu.SemaphoreType.DMA((max(ex.n_local, 1),))]

    n_ins = [len(ex.ins) if isinstance(ex, _Side) else len(ex.ins) + len(ex.io) for ex in exch]
    n_outs = [len(ex.out_shapes) if isinstance(ex, _Side) else len(ex.io) + len(ex.new) for ex in exch]

    def wrapped(*refs):
        pos = n_in
        ex_in = []
        for k in n_ins:
            ex_in.append(refs[pos:pos + k])
            pos += k
        outs = refs[pos:pos + n_out]
        pos += n_out
        ex_out = []
        for k in n_outs:
            ex_out.append(refs[pos:pos + k])
            pos += k
        scr = refs[pos:pos + n_scr]
        pos += n_scr
        idx = [pl.program_id(a) for a in range(len(grid))]
        first = functools.reduce(jnp.logical_and, [i == 0 for i in idx])
        last = functools.reduce(jnp.logical_and, [i == g - 1 for i, g in zip(idx, grid)])

        def copies():
            out, at = [], pos
            for ex, ei, eo in zip(exch, ex_in, ex_out):
                if not isinstance(ex, _Side):
                    out += ex.make(ei[:len(ex.ins)], eo[:len(ex.io)], eo[len(ex.io):], *refs[at:at + 3])
                    at += 3
            return out

        @pl.when(first)
        def _():
            for cp in copies():
                cp.start()

        body(*refs[:n_in], *outs, *scr)
        for ex, ei, eo in zip(exch, ex_in, ex_out):
            if isinstance(ex, _Side):
                pl.when(step(idx) < ex.n_tiles)(functools.partial(ex.fn, ei, eo))

        @pl.when(last)
        def _():
            for cp in copies():
                cp.wait()

    fn = pl.pallas_call(wrapped, name=name, grid=grid, in_specs=all_in, out_specs=all_out_specs,
                        out_shape=all_out_shape, scratch_shapes=all_scr, input_output_aliases=aliases,
                        compiler_params=_cparams(*(["arbitrary"] * len(grid))))

    def run(*args):
        res = fn(*args, *extra_args)
        outs, pos, ex_res = res[:n_out], n_out, []
        for k in n_outs:
            ex_res.append(list(res[pos:pos + k]))
            pos += k
        return outs, ex_res

    return run


def _exchange_alone(ex, name):
    def body():
        pass

    _, res = _call(body, [ex], name=name, grid=(1,), in_specs=[], out_specs=[], out_shape=[], semantics=("arbitrary",))()
    return res[0]


def _small_gather(part):
    def make(in_refs, io_refs, new_refs, send_sems, recv_sems, local_sems):
        x, y, c = _place()
        slot = new_refs[0].at[4 * x + 2 * y + c]
        copies = [pltpu.make_async_copy(in_refs[0], slot, local_sems.at[0])]
        for d in range(1, N_DEV):
            peer = (1 - x if d & 4 else x, 1 - y if d & 2 else y, 1 - c if d & 1 else c)
            copies.append(pltpu.make_async_remote_copy(
                src_ref=in_refs[0], dst_ref=slot, send_sem=send_sems.at[d - 1], recv_sem=recv_sems.at[d - 1],
                device_id=peer, device_id_type=MESH))
        return copies

    return _Exchange([part], [], [jax.ShapeDtypeStruct((N_DEV,) + part.shape, part.dtype)], N_DEV - 1, 1, make)


def _sum_over_devices(parts):
    _, rows, lanes = parts.shape

    def body(p_ref, o_ref):
        acc = p_ref[0]
        for k in range(1, N_DEV):
            acc = acc + p_ref[k]
        o_ref[...] = acc

    return pl.pallas_call(
        body, name="small_grads_sum", grid=(1,), out_shape=jax.ShapeDtypeStruct((rows, lanes), F32),
        in_specs=[pl.BlockSpec((N_DEV, rows, lanes), lambda i: (0, 0, 0))],
        out_specs=pl.BlockSpec((rows, lanes), lambda i: (0, 0)),
        compiler_params=_cparams("arbitrary"),
    )(parts)


def _transpose_bf16(a, name, exch=(), with_copy=False):
    r, c = a.shape
    tr, tc = _tile(r, 512, 128), _tile(c, 512, 128)

    def body(a_ref, o_ref, *copy_ref):
        v = a_ref[...].astype(F32)
        o_ref[...] = v.T.astype(BF16)
        if with_copy:
            copy_ref[0][...] = v.astype(BF16)

    outs, ex = _call(
        body, exch, name=name, grid=(r // tr, c // tc),
        out_shape=[jax.ShapeDtypeStruct((c, r), BF16)] + [jax.ShapeDtypeStruct((r, c), BF16)] * with_copy,
        in_specs=[pl.BlockSpec((tr, tc), lambda i, j: (i, j))],
        out_specs=[pl.BlockSpec((tc, tr), lambda i, j: (j, i))] + [pl.BlockSpec((tr, tc), lambda i, j: (i, j))] * with_copy,
        semantics=("parallel", "parallel"),
    )(a)
    return (outs if with_copy else outs[0]), ex


def _ffn_fwd(x, wgu, wd, ln_g, ln_b, name, exch=(), with_ln=True):
    t, d = x.shape
    f = wd.shape[0]
    tm, tf = _tile(t, 512, 128), _tile(f, 512, 128)
    nf = f // tf

    def body(x_ref, wg_ref, wu_ref, wd_ref, g_ref, b_ref, go_ref, uo_ref, ht_ref, z_ref, *rest):
        xn_ref = rest[0] if with_ln else None
        xb, acc = rest[-2:]
        j = pl.program_id(1)

        @pl.when(j == 0)
        def _():
            xb[...] = x_ref[...].astype(BF16)
            acc[...] = jnp.zeros_like(acc)

        g = _dot(xb[...], wg_ref[...])
        u = _dot(xb[...], wu_ref[...])
        h = g * _sigmoid(g) * u
        go_ref[...] = g.astype(BF16)
        uo_ref[...] = u.astype(BF16)
        ht_ref[...] = h.T.astype(BF16)
        acc[...] += _dot(h.astype(BF16), wd_ref[...])

        @pl.when(j == nf - 1)
        def _():
            z = ALPHA * x_ref[...] + 0.5 * acc[...]
            z_ref[...] = z
            if with_ln:
                xn_ref[...] = _ln(z, g_ref[...], b_ref[...])

    row = lambda i, j: (i, 0)
    n_td = 2 if with_ln else 1
    return _call(
        body, exch, name=name, grid=(t // tm, nf),
        out_shape=[jax.ShapeDtypeStruct((t, f), BF16), jax.ShapeDtypeStruct((t, f), BF16),
                   jax.ShapeDtypeStruct((f, t), BF16)] + [jax.ShapeDtypeStruct((t, d), F32)] * n_td,
        in_specs=[pl.BlockSpec((tm, d), row),
                  pl.BlockSpec((d, tf), lambda i, j: (0, j)),
                  pl.BlockSpec((d, tf), lambda i, j: (0, j + nf)),
                  pl.BlockSpec((tf, d), lambda i, j: (j, 0)),
                  pl.BlockSpec((1, d), lambda i, j: (0, 0)),
                  pl.BlockSpec((1, d), lambda i, j: (0, 0))],
        out_specs=[pl.BlockSpec((tm, tf), lambda i, j: (i, j)), pl.BlockSpec((tm, tf), lambda i, j: (i, j)),
                   pl.BlockSpec((tf, tm), lambda i, j: (j, i))] + [pl.BlockSpec((tm, d), row)] * n_td,
        scratch_shapes=[pltpu.VMEM((tm, d), BF16), pltpu.VMEM((tm, d), F32)],
        semantics=("parallel", "arbitrary"),
    )(x, wgu, wgu, wd, ln_g, ln_b)


def _ffn_down_fwd(gu, x, wd, ln_g, ln_b, name, exch=()):
    t, d = x.shape
    f = wd.shape[0]
    tm, tf = _tile(t, 512, 128), _tile(f, 512, 128)
    nf = f // tf

    def body(g_ref, u_ref, wd_ref, x_ref, lg_ref, lb_ref, ht_ref, z_ref, xn_ref, acc):
        j = pl.program_id(1)

        @pl.when(j == 0)
        def _():
            acc[...] = jnp.zeros_like(acc)

        g = g_ref[...].astype(F32)
        h = g * _sigmoid(g) * u_ref[...].astype(F32)
        ht_ref[...] = h.T.astype(BF16)
        acc[...] += _dot(h.astype(BF16), wd_ref[...])

        @pl.when(j == nf - 1)
        def _():
            z = ALPHA * x_ref[...] + 0.5 * acc[...]
            z_ref[...] = z
            xn_ref[...] = _ln(z, lg_ref[...], lb_ref[...])

    row = lambda i, j: (i, 0)
    fixed = lambda i, j: (0, 0)
    return _call(
        body, exch, name=name, grid=(t // tm, nf),
        out_shape=[jax.ShapeDtypeStruct((f, t), BF16), jax.ShapeDtypeStruct((t, d), F32),
                   jax.ShapeDtypeStruct((t, d), F32)],
        in_specs=[pl.BlockSpec((tm, tf), lambda i, j: (i, j)), pl.BlockSpec((tm, tf), lambda i, j: (i, j + nf)),
                  pl.BlockSpec((tf, d), lambda i, j: (j, 0)), pl.BlockSpec((tm, d), row),
                  pl.BlockSpec((1, d), fixed), pl.BlockSpec((1, d), fixed)],
        out_specs=[pl.BlockSpec((tf, tm), lambda i, j: (j, i)), pl.BlockSpec((tm, d), row), pl.BlockSpec((tm, d), row)],
        scratch_shapes=[pltpu.VMEM((tm, d), F32)],
        semantics=("parallel", "arbitrary"),
    )(gu, gu, wd, x, ln_g, ln_b)


def _ffn_act_grads(dh, g_ref, u_ref):
    gg = g_ref[...].astype(F32)
    uu = u_ref[...].astype(F32)
    s = _sigmoid(gg)
    du = (dh * (gg * s)).astype(BF16)
    dg = (dh * uu * (s * (1.0 + gg * (1.0 - s)))).astype(BF16)
    return dg, du


def _ffn_bwd(dz, do, g, u, wgu, wd, name, exch=()):
    t, d = dz.shape
    f = wd.shape[0]
    tm, tf = _tile(t, 512, 128), _tile(f, 512, 128)
    nf = f // tf

    def body(dz_ref, do_ref, g_ref, u_ref, wg_ref, wu_ref, wd_ref, dg_ref, du_ref, dx_ref, acc):
        j = pl.program_id(1)

        @pl.when(j == 0)
        def _():
            acc[...] = jnp.zeros_like(acc)

        dg, du = _ffn_act_grads(_dot_nt(do_ref[...], wd_ref[...]), g_ref, u_ref)
        dg_ref[...] = dg
        du_ref[...] = du
        acc[...] += _dot_nt(dg, wg_ref[...]) + _dot_nt(du, wu_ref[...])

        @pl.when(j == nf - 1)
        def _():
            dx_ref[...] = ALPHA * dz_ref[...] + acc[...]

    row = lambda i, j: (i, 0)
    tile = lambda i, j: (i, j)
    return _call(
        body, exch, name=name, grid=(t // tm, nf),
        out_shape=[jax.ShapeDtypeStruct((t, f), BF16), jax.ShapeDtypeStruct((t, f), BF16),
                   jax.ShapeDtypeStruct((t, d), F32)],
        in_specs=[pl.BlockSpec((tm, d), row), pl.BlockSpec((tm, d), row),
                  pl.BlockSpec((tm, tf), tile), pl.BlockSpec((tm, tf), tile),
                  pl.BlockSpec((d, tf), lambda i, j: (0, j)),
                  pl.BlockSpec((d, tf), lambda i, j: (0, j + nf)),
                  pl.BlockSpec((tf, d), lambda i, j: (j, 0))],
        out_specs=[pl.BlockSpec((tm, tf), tile), pl.BlockSpec((tm, tf), tile), pl.BlockSpec((tm, d), row)],
        scratch_shapes=[pltpu.VMEM((tm, d), F32)],
        semantics=("parallel", "arbitrary"),
    )(dz, do, g, u, wgu, wgu, wd)


def _ffn_bwd_act(do, gu, wd, name, exch=()):
    t, d = do.shape
    f = wd.shape[0]
    tm, tf = _tile(t, 512, 128), _tile(f, 512, 128)
    nf = f // tf

    def body(do_ref, g_ref, u_ref, wd_ref, dg_ref, du_ref):
        dg, du = _ffn_act_grads(_dot_nt(do_ref[...], wd_ref[...]), g_ref, u_ref)
        dg_ref[...] = dg
        du_ref[...] = du

    tile = lambda i, j: (i, j)
    return _call(
        body, exch, name=name, grid=(t // tm, f // tf),
        out_shape=[jax.ShapeDtypeStruct((t, f), BF16), jax.ShapeDtypeStruct((t, f), BF16)],
        in_specs=[pl.BlockSpec((tm, d), lambda i, j: (i, 0)), pl.BlockSpec((tm, tf), tile),
                  pl.BlockSpec((tm, tf), lambda i, j: (i, j + nf)), pl.BlockSpec((tf, d), lambda i, j: (j, 0))],
        out_specs=[pl.BlockSpec((tm, tf), tile), pl.BlockSpec((tm, tf), tile)],
        semantics=("parallel", "parallel"),
    )(do, gu, gu, wd)


def _ffn_bwd_dx(dz, dg, du, wgu, name, exch=()):
    t, d = dz.shape
    f = dg.shape[1]
    tm, tn = _tile(t, 512, 128), _tile(d, 256, 128)

    def body(dz_ref, dg_ref, du_ref, wg_ref, wu_ref, dx_ref):
        dx_ref[...] = ALPHA * dz_ref[...] + _dot_nt(dg_ref[...], wg_ref[...]) + _dot_nt(du_ref[...], wu_ref[...])

    row = lambda i, n: (i, 0)
    tile = lambda i, n: (i, n)
    return _call(
        body, exch, name=name, grid=(t // tm, d // tn), out_shape=[jax.ShapeDtypeStruct((t, d), F32)],
        in_specs=[pl.BlockSpec((tm, tn), tile), pl.BlockSpec((tm, f), row), pl.BlockSpec((tm, f), row),
                  pl.BlockSpec((tn, f), lambda i, n: (n, 0)), pl.BlockSpec((tn, f), lambda i, n: (n, 1))],
        out_specs=[pl.BlockSpec((tm, tn), tile)],
        semantics=("parallel", "arbitrary"),
    )(dz, dg, du, wgu, wgu)


def _weight_grad(at, b, tn, tmm, name, blocks=None, block_offset=0, into=None, exch=()):
    m, t = at.shape
    nn = b.shape[1]
    tmm = _tile(m, tmm, 16)
    assert nn % tn == 0

    def body(*refs):
        at_ref, b_ref, o_ref = refs[0], refs[1], refs[-1]
        r = _dot(at_ref[...], b_ref[...]).astype(BF16)
        if blocks is None:
            o_ref[...] = r
        else:
            o_ref[0] = r

    in_specs = [pl.BlockSpec((tmm, t), lambda n, i: (i, 0)), pl.BlockSpec((t, tn), lambda n, i: (0, n))]
    args = [at, b]
    aliases = {}
    if into is not None:
        in_specs.append(ANY)
        args.append(into)
        aliases = {2: 0}
    if blocks is None:
        out_shape = jax.ShapeDtypeStruct((m, nn), BF16)
        out_spec = pl.BlockSpec((tmm, tn), lambda n, i: (i, n))
    else:
        out_shape = jax.ShapeDtypeStruct((blocks, m, tn), BF16)
        out_spec = pl.BlockSpec((1, tmm, tn), lambda n, i: (n + block_offset, i, 0))
    (out,), ex = _call(
        body, exch, name=name, grid=(nn // tn, m // tmm), out_shape=[out_shape],
        in_specs=in_specs, out_specs=[out_spec], input_output_aliases=aliases,
        semantics=("parallel", "parallel"),
    )(*args)
    return out, ex


def _mix_in_proj(x, w_in, name, exch=()):
    t, d = x.shape
    n_out = w_in.shape[1]
    tm, cb = _tile(t, 512, 128), _tile(n_out, 512, 128)

    def body(x_ref, w_ref, o_ref, xb):
        @pl.when(pl.program_id(1) == 0)
        def _():
            xb[...] = x_ref[...].astype(BF16)

        o_ref[...] = _dot(xb[...], w_ref[...])

    (out,), ex = _call(
        body, exch, name=name, grid=(t // tm, n_out // cb), out_shape=[jax.ShapeDtypeStruct((t, n_out), F32)],
        in_specs=[pl.BlockSpec((tm, d), lambda i, k: (i, 0)), pl.BlockSpec((d, cb), lambda i, k: (0, k))],
        out_specs=[pl.BlockSpec((tm, cb), lambda i, k: (i, k))],
        scratch_shapes=[pltpu.VMEM((tm, d), BF16)],
        semantics=("parallel", "arbitrary"),
    )(x, w_in)
    return out, ex


def _mix_in_bwd(dproj, w_in, dz, name, exch=()):
    t, d = dz.shape
    kk = w_in.shape[1]
    tm, tn = _tile(t, 512, 128), _tile(d, 256, 128)

    def body(dp_ref, w_ref, dz_ref, dx_ref):
        dx_ref[...] = ALPHA * dz_ref[...] + _dot_nt(dp_ref[...], w_ref[...])

    (out,), ex = _call(
        body, exch, name=name, grid=(t // tm, d // tn), out_shape=[jax.ShapeDtypeStruct((t, d), F32)],
        in_specs=[pl.BlockSpec((tm, kk), lambda i, n: (i, 0)), pl.BlockSpec((tn, kk), lambda i, n: (n, 0)),
                  pl.BlockSpec((tm, tn), lambda i, n: (i, n))],
        out_specs=[pl.BlockSpec((tm, tn), lambda i, n: (i, n))],
        semantics=("parallel", "arbitrary"),
    )(dproj, w_in, dz)
    return out, ex


def _mix_out_fwd(y, w_out, x, ln_g, ln_b, name, exch=()):
    t, d = x.shape
    kk = y.shape[1]
    tm = _tile(t, 256, 128)

    def body(y_ref, w_ref, x_ref, g_ref, b_ref, z_ref, xn_ref, xnt_ref):
        z = ALPHA * x_ref[...] + _dot(y_ref[...], w_ref[...])
        z_ref[...] = z
        xn = _ln(z, g_ref[...], b_ref[...])
        xn_ref[...] = xn
        xnt_ref[...] = xn.T.astype(BF16)

    row = lambda i: (i, 0)
    fixed = lambda i: (0, 0)
    return _call(
        body, exch, name=name, grid=(t // tm,),
        out_shape=[jax.ShapeDtypeStruct((t, d), F32), jax.ShapeDtypeStruct((t, d), F32),
                   jax.ShapeDtypeStruct((d, t), BF16)],
        in_specs=[pl.BlockSpec((tm, kk), row), pl.BlockSpec((kk, d), fixed), pl.BlockSpec((tm, d), row),
                  pl.BlockSpec((1, d), fixed), pl.BlockSpec((1, d), fixed)],
        out_specs=[pl.BlockSpec((tm, d), row), pl.BlockSpec((tm, d), row), pl.BlockSpec((d, tm), lambda i: (0, i))],
        semantics=("parallel",),
    )(y, w_out, x, ln_g, ln_b)


def _mix_out_bwd(dzb, w_out, name):
    t, d = dzb.shape
    kk = w_out.shape[0]
    tm = _tile(t, 256, 128)

    def body(dz_ref, w_ref, dy_ref):
        dy_ref[...] = _dot_nt(dz_ref[...], w_ref[...])

    return pl.pallas_call(
        body, name=name, grid=(t // tm,), out_shape=jax.ShapeDtypeStruct((t, kk), F32),
        in_specs=[pl.BlockSpec((tm, d), lambda i: (i, 0)), pl.BlockSpec((kk, d), lambda i: (0, 0))],
        out_specs=pl.BlockSpec((tm, kk), lambda i: (i, 0)),
        compiler_params=_cparams("parallel"),
    )(dzb, w_out)


def _loss_ln_bwd(z, target, ln_g, ln_b, bf16_scale, name):
    t, d = z.shape
    tm = _tile(t, 512, 8)

    def body(z_ref, t_ref, g_ref, b_ref, dz_ref, dzb_ref, dg_ref, db_ref, loss_ref):
        @pl.when(pl.program_id(0) == 0)
        def _():
            dg_ref[...] = jnp.zeros_like(dg_ref)
            db_ref[...] = jnp.zeros_like(db_ref)
            loss_ref[...] = jnp.zeros_like(loss_ref)

        xh, rstd = _ln_stats(z_ref[...])
        e = xh * g_ref[...] + b_ref[...] - t_ref[...]
        loss_ref[...] += 0.5 * jnp.sum(jnp.sum(e * e, axis=-1, keepdims=True) * (1.0 / d), axis=0, keepdims=True)
        dy = e * (1.0 / d)
        dz = _ln_bwd(dy * g_ref[...], xh, rstd)
        dz_ref[...] = dz
        dzb_ref[...] = (bf16_scale * dz).astype(BF16)
        dg_ref[...] += jnp.sum(dy * xh, axis=0, keepdims=True)
        db_ref[...] += jnp.sum(dy, axis=0, keepdims=True)

    row = lambda i: (i, 0)
    fixed = lambda i: (0, 0)
    return pl.pallas_call(
        body, name=name, grid=(t // tm,),
        out_shape=[jax.ShapeDtypeStruct((t, d), F32), jax.ShapeDtypeStruct((t, d), BF16),
                   jax.ShapeDtypeStruct((1, d), F32), jax.ShapeDtypeStruct((1, d), F32),
                   jax.ShapeDtypeStruct((8, 128), F32)],
        in_specs=[pl.BlockSpec((tm, d), row), pl.BlockSpec((tm, d), row), pl.BlockSpec((1, d), fixed),
                  pl.BlockSpec((1, d), fixed)],
        out_specs=[pl.BlockSpec((tm, d), row), pl.BlockSpec((tm, d), row), pl.BlockSpec((1, d), fixed),
                   pl.BlockSpec((1, d), fixed), pl.BlockSpec((8, 128), fixed)],
        compiler_params=_cparams("arbitrary"),
    )(z, target, ln_g, ln_b)


def _ln_bwd_call(z, dy, ln_g, bf16_scale, name, exch=()):
    t, d = z.shape
    tm = _tile(t, 512, 8)

    def body(z_ref, dy_ref, g_ref, dz_ref, dzb_ref, dg_ref, db_ref):
        @pl.when(pl.program_id(0) == 0)
        def _():
            dg_ref[...] = jnp.zeros_like(dg_ref)
            db_ref[...] = jnp.zeros_like(db_ref)

        xh, rstd = _ln_stats(z_ref[...])
        dy = dy_ref[...]
        dz = _ln_bwd(dy * g_ref[...], xh, rstd)
        dz_ref[...] = dz
        dzb_ref[...] = (bf16_scale * dz).astype(BF16)
        dg_ref[...] += jnp.sum(dy * xh, axis=0, keepdims=True)
        db_ref[...] += jnp.sum(dy, axis=0, keepdims=True)

    row = lambda i: (i, 0)
    fixed = lambda i: (0, 0)
    return _call(
        body, exch, name=name, grid=(t // tm,),
        out_shape=[jax.ShapeDtypeStruct((t, d), F32), jax.ShapeDtypeStruct((t, d), BF16),
                   jax.ShapeDtypeStruct((1, d), F32), jax.ShapeDtypeStruct((1, d), F32)],
        in_specs=[pl.BlockSpec((tm, d), row), pl.BlockSpec((tm, d), row), pl.BlockSpec((1, d), fixed)],
        out_specs=[pl.BlockSpec((tm, d), row), pl.BlockSpec((tm, d), row), pl.BlockSpec((1, d), fixed),
                   pl.BlockSpec((1, d), fixed)],
        semantics=("arbitrary",),
    )(z, dy, ln_g)


CONV_ROWS = 32
SUBLANES = 8


def _fill_shifted(ext, shifted):
    rows = ext.shape[0] - SUBLANES
    for s in range(1, SUBLANES):
        for r in range(0, rows, CONV_ROWS):
            n = min(CONV_ROWS, rows - r)
            shifted[s - 1, r:r + n, :] = ext[r + s:r + s + n, :]


def _window(ext, shifted, lo, n):
    s = lo % SUBLANES
    return ext[lo:lo + n, :] if s == 0 else shifted[s - 1, lo - s:lo - s + n, :]


def _mixer_fwd(proj, conv_w, conv_b, cln_g, cln_b, sln_g, sln_b, sg_wm, sg_bb, name, exch=()):
    t = proj.shape[0]
    tm = _tile(t, 256, CHUNK)
    hb = tm // HALO
    nc = tm // CHUNK
    ch = CONV_CH

    def body(av_ref, ag_ref, bu_ref, bv_ref, hv_ref, hg_ref, cw_ref, cb_ref, lg_ref, lb_ref, sg_ref, sb_ref,
             w_ref, bb_ref, y_ref, yt_ref, c_ref, ext, ext_s):
        i = pl.program_id(0)
        halo = hv_ref[...] * _sigmoid(hg_ref[...])
        ext[0:HALO, :] = jnp.where(i > 0, halo, 0.0)
        ext[HALO:HALO + tm, :] = av_ref[...] * _sigmoid(ag_ref[...])
        _fill_shifted(ext, ext_s)
        for r in range(0, tm, CONV_ROWS):
            acc = jnp.zeros((CONV_ROWS, ch), F32) + cb_ref[...]
            for k in range(CONV_TAPS):
                lo = r + k + HALO - (CONV_TAPS - 1)
                acc = acc + cw_ref[k:k + 1, :] * _window(ext, ext_s, lo, CONV_ROWS)
            c_ref[r:r + CONV_ROWS, :] = acc
        a = _ln(c_ref[...], lg_ref[...], lb_ref[...])
        ya = a * _sigmoid(a)
        y_ref[:, 0:ch] = ya.astype(BF16)
        yt_ref[0:ch, :] = ya.T.astype(BF16)
        for h in range(HEADS):
            sl = slice(h * HEAD_DIM, (h + 1) * HEAD_DIM)
            u, _ = _gelu_and_grad(bu_ref[:, sl])
            v, _ = _gelu_and_grad(bv_ref[:, sl])
            vn = _ln(v, sg_ref[h:h + 1, :], sb_ref[h:h + 1, :])
            vn3 = vn.astype(BF16).reshape(nc, CHUNK, HEAD_DIM)
            wb = jnp.broadcast_to(w_ref[h][None], (nc, CHUNK, CHUNK))
            mixed = jnp.einsum("cts,csd->ctd", wb, vn3, preferred_element_type=F32) + bb_ref[h][None]
            yb = u * mixed.reshape(tm, HEAD_DIM)
            y_ref[:, ch + h * HEAD_DIM:ch + (h + 1) * HEAD_DIM] = yb.astype(BF16)
            yt_ref[ch + h * HEAD_DIM:ch + (h + 1) * HEAD_DIM, :] = yb.T.astype(BF16)

    col = lambda cidx: (lambda i: (i, cidx))
    prev = lambda cidx: (lambda i: (jnp.maximum(i * hb - 1, 0), cidx))
    fix2 = lambda i: (0, 0)
    fix3 = lambda i: (0, 0, 0)
    return _call(
        body, exch, name=name, grid=(t // tm,),
        out_shape=[jax.ShapeDtypeStruct((t, 2 * ch), BF16), jax.ShapeDtypeStruct((2 * ch, t), BF16),
                   jax.ShapeDtypeStruct((t, ch), F32)],
        in_specs=[pl.BlockSpec((tm, ch), col(0)), pl.BlockSpec((tm, ch), col(1)), pl.BlockSpec((tm, ch), col(2)),
                  pl.BlockSpec((tm, ch), col(3)), pl.BlockSpec((HALO, ch), prev(0)), pl.BlockSpec((HALO, ch), prev(1)),
                  pl.BlockSpec((CONV_TAPS, ch), fix2), pl.BlockSpec((1, ch), fix2), pl.BlockSpec((1, ch), fix2),
                  pl.BlockSpec((1, ch), fix2), pl.BlockSpec((HEADS, HEAD_DIM), fix2), pl.BlockSpec((HEADS, HEAD_DIM), fix2),
                  pl.BlockSpec((HEADS, CHUNK, CHUNK), fix3), pl.BlockSpec((HEADS, CHUNK, HEAD_DIM), fix3)],
        out_specs=[pl.BlockSpec((tm, 2 * ch), lambda i: (i, 0)), pl.BlockSpec((2 * ch, tm), lambda i: (0, i)),
                   pl.BlockSpec((tm, ch), lambda i: (i, 0))],
        scratch_shapes=[pltpu.VMEM((HALO + tm, ch), F32), pltpu.VMEM((SUBLANES - 1, HALO + tm, ch), F32)],
        semantics=("parallel",),
    )(proj, proj, proj, proj, proj, proj, conv_w, conv_b, cln_g, cln_b, sln_g, sln_b, sg_wm, sg_bb)


def _mixer_bwd(proj, conv_c, dy, conv_w, cln_g, cln_b, sln_g, sln_b, sg_wm, sg_wmt, sg_bb, name, exch=()):
    t = proj.shape[0]
    tm = _tile(t, 256, CHUNK)
    hb = tm // HALO
    nc = tm // CHUNK
    nt = t // tm
    ch = CONV_CH
    last_halo = t // HALO - 1

    def body(av_ref, ag_ref, bu_ref, bv_ref, hv_ref, hg_ref, c_ref, cn_ref, dya_ref, dyan_ref, dyb_ref,
             cw_ref, lg_ref, lb_ref, sg_ref, sb_ref, w_ref, wt_ref, bb_ref,
             dp_ref, dcw_ref, dcb_ref, dlg_ref, dlb_ref, dsg_ref, dsb_ref, dw_ref, dbs_ref,
             ext_h, ext_dc, ext_hs, ext_dcs, acc_cw):
        i = pl.program_id(0)

        @pl.when(i == 0)
        def _():
            acc_cw[...] = jnp.zeros_like(acc_cw)
            for ref in (dcb_ref, dlg_ref, dlb_ref, dsg_ref, dsb_ref, dw_ref, dbs_ref):
                ref[...] = jnp.zeros_like(ref)

        lg = lg_ref[...]
        lb = lb_ref[...]

        def conv_ln_bwd(c, dya):
            xh, rstd = _ln_stats(c)
            a = xh * lg + lb
            da = dya * _silu_grad(a)
            return _ln_bwd(da * lg, xh, rstd), da, xh

        fold = lambda v: jnp.sum(v.reshape(CONV_ROWS // SUBLANES, SUBLANES, ch), axis=0)
        s_lg = s_lb = s_cb = jnp.zeros((SUBLANES, ch), F32)
        for r in range(0, tm, CONV_ROWS):
            dc, da, xh = conv_ln_bwd(c_ref[r:r + CONV_ROWS, :], dya_ref[r:r + CONV_ROWS, :])
            ext_dc[r:r + CONV_ROWS, :] = dc
            s_lg, s_lb, s_cb = s_lg + fold(da * xh), s_lb + fold(da), s_cb + fold(dc)
        dlg_ref[...] += jnp.sum(s_lg, axis=0, keepdims=True)
        dlb_ref[...] += jnp.sum(s_lb, axis=0, keepdims=True)
        dcb_ref[...] += jnp.sum(s_cb, axis=0, keepdims=True)
        dcn, _, _ = conv_ln_bwd(cn_ref[...], dyan_ref[...])
        ext_dc[tm:tm + HALO, :] = jnp.where(i < nt - 1, dcn, 0.0)
        halo = hv_ref[...] * _sigmoid(hg_ref[...])
        ext_h[0:HALO, :] = jnp.where(i > 0, halo, 0.0)
        ext_h[HALO:HALO + tm, :] = av_ref[...] * _sigmoid(ag_ref[...])
        _fill_shifted(ext_h, ext_hs)
        _fill_shifted(ext_dc, ext_dcs)
        for r in range(0, tm, CONV_ROWS):
            dcr = ext_dc[r:r + CONV_ROWS, :]
            acc = jnp.zeros((CONV_ROWS, ch), F32)
            for k in range(CONV_TAPS):
                lo = r + k + HALO - (CONV_TAPS - 1)
                prod = dcr * _window(ext_h, ext_hs, lo, CONV_ROWS)
                acc_cw[k] += jnp.sum(prod.reshape(CONV_ROWS // 8, 8, ch), axis=0)
                hi = r + (CONV_TAPS - 1) - k
                acc = acc + cw_ref[k:k + 1, :] * _window(ext_dc, ext_dcs, hi, CONV_ROWS)
            sg_r = _sigmoid(ag_ref[r:r + CONV_ROWS, :])
            av_r = av_ref[r:r + CONV_ROWS, :]
            dp_ref[r:r + CONV_ROWS, 0:ch] = (acc * sg_r).astype(BF16)
            dp_ref[r:r + CONV_ROWS, ch:2 * ch] = (acc * av_r * sg_r * (1.0 - sg_r)).astype(BF16)

        @pl.when(i == nt - 1)
        def _():
            dcw_ref[...] = jnp.sum(acc_cw[...], axis=1)

        tril = (lax.broadcasted_iota(jnp.int32, (CHUNK, CHUNK), 0)
                >= lax.broadcasted_iota(jnp.int32, (CHUNK, CHUNK), 1)).astype(F32)
        for h in range(HEADS):
            sl = slice(h * HEAD_DIM, (h + 1) * HEAD_DIM)
            u, du_dx = _gelu_and_grad(bu_ref[:, sl])
            v, dv_dx = _gelu_and_grad(bv_ref[:, sl])
            xhv, rstdv = _ln_stats(v)
            gh = sg_ref[h:h + 1, :]
            vn3 = (xhv * gh + sb_ref[h:h + 1, :]).astype(BF16).reshape(nc, CHUNK, HEAD_DIM)
            wb = jnp.broadcast_to(w_ref[h][None], (nc, CHUNK, CHUNK))
            mixed = jnp.einsum("cts,csd->ctd", wb, vn3, preferred_element_type=F32) + bb_ref[h][None]
            dyb = dyb_ref[:, sl]
            d_u = dyb * mixed.reshape(tm, HEAD_DIM)
            dm = dyb * u
            dm3 = dm.reshape(nc, CHUNK, HEAD_DIM)
            dbs_ref[h:h + 1, :] += jnp.sum(jnp.sum(dm3, axis=0).T, axis=0, keepdims=True)
            dm3b = dm3.astype(BF16)
            dw_h = jnp.sum(jnp.einsum("ctd,csd->cts", dm3b, vn3, preferred_element_type=F32), axis=0)
            dw_ref[h] += dw_h * tril
            wtb = jnp.broadcast_to(wt_ref[h][None], (nc, CHUNK, CHUNK))
            d_vn = jnp.einsum("cst,ctd->csd", wtb, dm3b, preferred_element_type=F32).reshape(tm, HEAD_DIM)
            dsg_ref[h:h + 1, :] += jnp.sum(d_vn * xhv, axis=0, keepdims=True)
            dsb_ref[h:h + 1, :] += jnp.sum(d_vn, axis=0, keepdims=True)
            dv = _ln_bwd(d_vn * gh, xhv, rstdv)
            dp_ref[:, 2 * ch + h * HEAD_DIM:2 * ch + (h + 1) * HEAD_DIM] = (d_u * du_dx).astype(BF16)
            dp_ref[:, 3 * ch + h * HEAD_DIM:3 * ch + (h + 1) * HEAD_DIM] = (dv * dv_dx).astype(BF16)

    col = lambda cidx: (lambda i: (i, cidx))
    prev = lambda cidx: (lambda i: (jnp.maximum(i * hb - 1, 0), cidx))
    nxt = lambda i: (jnp.minimum((i + 1) * hb, last_halo), 0)
    fix2 = lambda i: (0, 0)
    fix3 = lambda i: (0, 0, 0)
    out_shape = [jax.ShapeDtypeStruct((t, 4 * ch), BF16), jax.ShapeDtypeStruct((CONV_TAPS, ch), F32),
                 jax.ShapeDtypeStruct((1, ch), F32), jax.ShapeDtypeStruct((1, ch), F32), jax.ShapeDtypeStruct((1, ch), F32),
                 jax.ShapeDtypeStruct((HEADS, HEAD_DIM), F32), jax.ShapeDtypeStruct((HEADS, HEAD_DIM), F32),
                 jax.ShapeDtypeStruct((HEADS, CHUNK, CHUNK), F32), jax.ShapeDtypeStruct((HEADS, CHUNK), F32)]
    out_specs = [pl.BlockSpec((tm, 4 * ch), lambda i: (i, 0)), pl.BlockSpec((CONV_TAPS, ch), fix2),
                 pl.BlockSpec((1, ch), fix2), pl.BlockSpec((1, ch), fix2), pl.BlockSpec((1, ch), fix2),
                 pl.BlockSpec((HEADS, HEAD_DIM), fix2), pl.BlockSpec((HEADS, HEAD_DIM), fix2),
                 pl.BlockSpec((HEADS, CHUNK, CHUNK), fix3), pl.BlockSpec((HEADS, CHUNK), fix2)]
    in_specs = [pl.BlockSpec((tm, ch), col(0)), pl.BlockSpec((tm, ch), col(1)), pl.BlockSpec((tm, ch), col(2)),
                pl.BlockSpec((tm, ch), col(3)), pl.BlockSpec((HALO, ch), prev(0)), pl.BlockSpec((HALO, ch), prev(1)),
                pl.BlockSpec((tm, ch), col(0)), pl.BlockSpec((HALO, ch), nxt),
                pl.BlockSpec((tm, ch), col(0)), pl.BlockSpec((HALO, ch), nxt), pl.BlockSpec((tm, ch), col(1)),
                pl.BlockSpec((CONV_TAPS, ch), fix2), pl.BlockSpec((1, ch), fix2), pl.BlockSpec((1, ch), fix2),
                pl.BlockSpec((HEADS, HEAD_DIM), fix2), pl.BlockSpec((HEADS, HEAD_DIM), fix2),
                pl.BlockSpec((HEADS, CHUNK, CHUNK), fix3), pl.BlockSpec((HEADS, CHUNK, CHUNK), fix3),
                pl.BlockSpec((HEADS, CHUNK, HEAD_DIM), fix3)]
    return _call(
        body, exch, name=name, grid=(nt,), out_shape=out_shape, in_specs=in_specs, out_specs=out_specs,
        scratch_shapes=[pltpu.VMEM((HALO + tm, ch), F32), pltpu.VMEM((tm + HALO, ch), F32),
                        pltpu.VMEM((SUBLANES - 1, HALO + tm, ch), F32), pltpu.VMEM((SUBLANES - 1, tm + HALO, ch), F32),
                        pltpu.VMEM((CONV_TAPS, 8, ch), F32)],
        semantics=("arbitrary",),
    )(proj, proj, proj, proj, proj, proj, conv_c, conv_c, dy, dy, dy,
      conv_w, cln_g, cln_b, sln_g, sln_b, sg_wm, sg_wmt, sg_bb)


def _pair_sum(parts, from_sibling, core_chip, name):
    _, r, cc = parts.shape
    tr = _tile(r, max(16, (1 << 20) // (2 * cc)), 16)

    def body(cc_ref, p_ref, s_ref, o_ref, own_ref):
        q = (p_ref[...].astype(F32) + s_ref[...].astype(F32)).astype(BF16)
        o_ref[...] = q

        @pl.when(pl.program_id(1) == cc_ref[1])
        def _():
            own_ref[...] = q[0]

    grid_spec = pltpu.PrefetchScalarGridSpec(
        num_scalar_prefetch=1, grid=(r // tr, 4),
        in_specs=[pl.BlockSpec((1, tr, cc), lambda i, j, cc_ref: (2 * j + cc_ref[0], i, 0)),
                  pl.BlockSpec((1, tr, cc), lambda i, j, cc_ref: (j, i, 0))],
        out_specs=[pl.BlockSpec((1, tr, cc), lambda i, j, cc_ref: (j, i, 0)),
                   pl.BlockSpec((tr, cc), lambda i, j, cc_ref: (i, 0))])
    return pl.pallas_call(
        body, name=name, grid_spec=grid_spec,
        out_shape=[jax.ShapeDtypeStruct((4, r, cc), BF16), jax.ShapeDtypeStruct((r, cc), BF16)],
        compiler_params=_cparams("parallel", "arbitrary"),
    )(core_chip, parts, from_sibling)


def _adamw_math(w, g, m, v):
    m = ADAM_B1 * m + (1.0 - ADAM_B1) * g
    v = ADAM_B2 * v + (1.0 - ADAM_B2) * (g * g)
    m_hat = m / (1.0 - ADAM_B1 ** ADAM_STEP)
    v_hat = v / (1.0 - ADAM_B2 ** ADAM_STEP)
    delta = -ADAM_LR * (m_hat / (jnp.sqrt(v_hat) + ADAM_EPS) + ADAM_WD * w)
    return delta, m, v


def _adamw_tile(in_refs, out_refs):
    w_ref, m_ref, v_ref, q_ref, o_ref = in_refs
    g = q_ref[...].astype(F32)
    for k in range(3):
        g = g + o_ref[k].astype(F32)
    d, mm, vv = _adamw_math(w_ref[...], g, m_ref[...], v_ref[...])
    for ref, val in zip(out_refs, (g, d, mm, vv)):
        ref[...] = val


def _adamw_side(w, m, v, chip_part, from_chips, max_tiles):
    r, cc = w.shape
    n = max(k for k in range(1, max_tiles + 1) if r % k == 0 and (r // k) % 16 == 0)
    tr = r // n
    row = ((tr, cc), lambda s: (s, 0))
    return _Side([w, m, v, chip_part, from_chips], [row, row, row, row, ((3, tr, cc), lambda s: (0, s, 0))],
                 [jax.ShapeDtypeStruct((r, cc), F32)] * 4, [row] * 4, n, _adamw_tile)


def _adamw_sharded(w, m, v, chip_part, from_chips, name):
    r, cc = w.shape
    tr = _tile(r, max(16, (1 << 19) // (4 * cc) * 2), 16)

    def body(*refs):
        _adamw_tile(refs[:5], refs[5:])

    row = pl.BlockSpec((tr, cc), lambda i: (i, 0))
    return pl.pallas_call(
        body, name=name, grid=(r // tr,), out_shape=[jax.ShapeDtypeStruct((r, cc), F32)] * 4,
        in_specs=[row, row, row, row, pl.BlockSpec((3, tr, cc), lambda i: (0, i, 0))], out_specs=[row] * 4,
        compiler_params=_cparams("parallel"),
    )(w, m, v, chip_part, from_chips)


def _adamw_small(w, g, m, v, name):
    r, cc = w.shape

    def body(w_ref, g_ref, m_ref, v_ref, d_out, m_out, v_out):
        d, mm, vv = _adamw_math(w_ref[...], g_ref[...], m_ref[...], v_ref[...])
        d_out[...] = d
        m_out[...] = mm
        v_out[...] = vv

    full = pl.BlockSpec((r, cc), lambda i: (0, 0))
    return pl.pallas_call(
        body, name=name, grid=(1,), out_shape=[jax.ShapeDtypeStruct((r, cc), F32)] * 3,
        in_specs=[full] * 4, out_specs=[full] * 3, compiler_params=_cparams("arbitrary"),
    )(w, g, m, v)


SMALL = ("ln1_g", "ln1_b", "conv_b", "conv_ln_g", "conv_ln_b", "sg_ln_g", "sg_ln_b", "sg_w", "sg_b",
         "ln2_g", "ln2_b", "ln3_g", "ln3_b")
ORDER = ("ffn1_w_gate_up", "ffn1_w_down", "ln1_g", "ln1_b", "mix_w_in", "conv_w", "conv_b", "conv_ln_g", "conv_ln_b",
         "sg_ln_g", "sg_ln_b", "sg_w", "sg_b", "mix_w_out", "ln2_g", "ln2_b", "ffn2_w_gate_up", "ffn2_w_down",
         "ln3_g", "ln3_b")


def _rows128(a):
    return a.reshape(-1, 128)


def kernel(x, ffn1_w_gate_up, ffn1_w_down, ln1_g, ln1_b, mix_w_in, conv_w, conv_b, conv_ln_g, conv_ln_b, sg_ln_g, sg_ln_b, sg_w, sg_b, mix_w_out, ln2_g, ln2_b, ffn2_w_gate_up, ffn2_w_down, ln3_g, ln3_b, loss_target, m_ffn1_w_gate_up, m_ffn1_w_down, m_ln1_g, m_ln1_b, m_mix_w_in, m_conv_w, m_conv_b, m_conv_ln_g, m_conv_ln_b, m_sg_ln_g, m_sg_ln_b, m_sg_w, m_sg_b, m_mix_w_out, m_ln2_g, m_ln2_b, m_ffn2_w_gate_up, m_ffn2_w_down, m_ln3_g, m_ln3_b, v_ffn1_w_gate_up, v_ffn1_w_down, v_ln1_g, v_ln1_b, v_mix_w_in, v_conv_w, v_conv_b, v_conv_ln_g, v_conv_ln_b, v_sg_ln_g, v_sg_ln_b, v_sg_w, v_sg_b, v_mix_w_out, v_ln2_g, v_ln2_b, v_ffn2_w_gate_up, v_ffn2_w_down, v_ln3_g, v_ln3_b):
    args = dict(locals())
    w = {n: args[n][0] for n in ORDER}
    mom = {n: args["m_" + n][0] for n in ORDER}
    var = {n: args["v_" + n][0] for n in ORDER}
    x0 = x[0]
    target = loss_target[0]
    t, d = x0.shape
    my_x, my_y, my_c = lax.axis_index("x"), lax.axis_index("y"), lax.axis_index("c")
    my_chip = (2 * my_x + my_y).astype(jnp.int32).reshape(1)
    my_core = my_c.astype(jnp.int32).reshape(1)
    me = 4 * my_x + 2 * my_y + my_c

    big = ("ffn1_w_gate_up", "ffn1_w_down", "mix_w_in", "mix_w_out", "ffn2_w_gate_up", "ffn2_w_down")
    sh = {n: w[n].astype(BF16) for n in big}
    f2s = sh["ffn2_w_gate_up"].shape[1]
    order = jnp.stack([4 * p[0] + 2 * p[1] + p[2] for p in _visit_order(my_x, my_y, my_c)]).astype(jnp.int32)
    gu1, x0t, (wgu1, wd1, conv_w_all) = _gather_and_gate_up(
        x0, [sh["ffn1_w_gate_up"], sh["ffn1_w_down"], w["conv_w"]], [True, True, False], order, "ffn1_gate_up_fwd")
    wd1 = wd1.reshape(-1, d)
    conv_w_full = jnp.transpose(conv_w_all, (1, 0, 2)).reshape(CONV_TAPS, CONV_CH)
    tril = jnp.tril(jnp.ones((CHUNK, CHUNK), F32))
    sg_wm = w["sg_w"] * tril
    sg_wm_b = sg_wm.astype(BF16)
    sg_wmt_b = jnp.swapaxes(sg_wm, 1, 2).astype(BF16)
    sg_bb = jnp.broadcast_to(w["sg_b"][:, :, None], (HEADS, CHUNK, HEAD_DIM))
    row = lambda a: a.reshape(1, -1)

    d2 = [sh["ffn2_w_down"]]
    d2_first = d2[0].shape[0] * 3 // 5 // 16 * 16
    d2_top, d2_bottom = (0, d2_first), (d2_first, d2[0].shape[0] - d2_first)
    (h1t, z1, x1), ((g_in, g_out), (g_d2,)) = _ffn_down_fwd(
        gu1, x0, wd1, row(w["ln1_g"]), row(w["ln1_b"]), "ffn1_down_fwd",
        exch=[_gather_first([sh["mix_w_in"], sh["mix_w_out"]], [True, False]),
              _gather_first(d2, [False], rows=d2_top)])
    in_cols = sh["mix_w_in"].shape[1]
    x1t, ((w_in, w_out), (g_d2,)) = _transpose_bf16(
        x1, "x1_transpose", exch=[_gather_forward([g_in, g_out], [True, False], [in_cols, None]),
                                  _gather_forward([g_d2], [False], [None], rows=d2_top)])
    w_out = w_out.reshape(-1, d)
    top, bottom = (0, d // 2), (d // 2, d // 2)
    gu2 = [sh["ffn2_w_gate_up"]]
    proj, ((g_gu2,),) = _mix_in_proj(x1, w_in, "mix_in_fwd", exch=[_gather_first(gu2, [True], rows=top)])
    (y, yt, conv_c), ((g_gu2,),) = _mixer_fwd(
        proj, conv_w_full, row(w["conv_b"]), row(w["conv_ln_g"]), row(w["conv_ln_b"]),
        w["sg_ln_g"], w["sg_ln_b"], sg_wm_b, sg_bb, "mixer_fwd",
        exch=[_both(_gather_first(gu2, [True], rows=bottom, into=[g_gu2]),
                    _gather_forward([g_gu2], [True], [f2s], rows=top))])
    (z2, x2, x2t), ((wgu2,), (g_d2,)) = _mix_out_fwd(
        y, w_out, x1, row(w["ln2_g"]), row(w["ln2_b"]), "mix_out_fwd",
        exch=[_gather_forward([g_gu2], [True], [f2s], rows=bottom),
              _gather_first(d2, [False], rows=d2_bottom, into=[g_d2])])
    (wd2,) = _exchange_alone(_gather_forward([g_d2], [False], [None], rows=d2_bottom), "ffn2_down_gather_forward")
    wd2 = wd2.reshape(-1, d)
    (g2, u2, h2t, z3), _ = _ffn_fwd(x2, wgu2, wd2, row(w["ln3_g"]), row(w["ln3_b"]), "ffn2_fwd", with_ln=False)

    f = wd1.shape[0]
    dn = _tile(d, 1024, 128)
    grads = {}
    core_chip = jnp.concatenate([my_core, my_chip])
    pair = lambda p, s, label: _pair_sum(p, s, core_chip, "pair_sum_" + label)
    adamw = lambda n, own, got, steps: _adamw_side(w[n], mom[n], var[n], own, got, steps)
    m_tiles = d // _tile(d, 512, 16)
    gu_first = d * 2 // 3 // 16 * 16
    out = {}
    dz3, do2, grads["ln3_g"], grads["ln3_b"], loss_tile = _loss_ln_bwd(
        z3, target, row(w["ln3_g"]), row(w["ln3_b"]), 0.5, "loss_ln3_bwd")
    p_d2, _ = _weight_grad(h2t, do2, dn, 512, "ffn2_dw_down")
    p_d2 = p_d2.reshape(N_DEV, f // N_DEV, d)
    (dg2, du2, dx2), ((s_d2,),) = _ffn_bwd(dz3, do2, g2, u2, wgu2, wd2, "ffn2_bwd", exch=[_rs_sibling([p_d2])])
    q_d2, own_d2 = pair(p_d2, s_d2, "ffn2_down")
    d_rows = q_d2.shape[1]
    d_half = d_rows // 2 // 16 * 16
    p_gu2, ((r_d2,),) = _weight_grad(x2t, dg2, f2s, 512, "ffn2_dw_gate", blocks=N_DEV,
                                     exch=[_rs_chips([q_d2], rows=(0, d_half))])
    p_gu2, ((r_d2,),) = _weight_grad(x2t, du2, f2s, 512, "ffn2_dw_up", blocks=N_DEV, block_offset=4, into=p_gu2,
                                     exch=[_rs_chips([q_d2], rows=(d_half, d_rows - d_half), into=[r_d2])])
    (dz2, dz2b, grads["ln2_g"], grads["ln2_b"]), ((s_gu2,),) = _ln_bwd_call(
        z2, dx2, row(w["ln2_g"]), 1.0, "ln2_bwd", exch=[_rs_sibling([p_gu2])])
    q_gu2, own_gu2 = pair(p_gu2, s_gu2, "ffn2_gate_up")
    dy = _mix_out_bwd(dz2b, w_out, "mix_out_bwd")
    p_out, _ = _weight_grad(yt, dz2b, dn, 512, "mix_out_dw")
    p_out = p_out.reshape(N_DEV, -1, d)
    (dproj, grads["conv_w"], grads["conv_b"], grads["conv_ln_g"], grads["conv_ln_b"], grads["sg_ln_g"],
     grads["sg_ln_b"], grads["sg_w"], grads["sg_b"]), ((r_gu2,),) = _mixer_bwd(
        proj, conv_c, dy, conv_w_full, row(w["conv_ln_g"]), row(w["conv_ln_b"]), w["sg_ln_g"], w["sg_ln_b"],
        sg_wm_b, sg_wmt_b, sg_bb, "mixer_bwd", exch=[_rs_chips([q_gu2], rows=(0, gu_first))])
    dx1, ((s_out,), (r_gu2,)) = _mix_in_bwd(
        dproj, w_in, dz2, "mix_in_bwd",
        exch=[_rs_sibling([p_out]), _rs_chips([q_gu2], rows=(gu_first, d - gu_first), into=[r_gu2])])
    p_in, (out["ffn2_w_gate_up"], out["ffn2_w_down"]) = _weight_grad(
        x1t, dproj, in_cols, 512, "mix_in_dw", blocks=N_DEV,
        exch=[adamw("ffn2_w_gate_up", own_gu2, r_gu2, N_DEV * m_tiles), adamw("ffn2_w_down", own_d2, r_d2, N_DEV * m_tiles)])
    (dz1, do1, grads["ln1_g"], grads["ln1_b"]), ((s_in,),) = _ln_bwd_call(
        z1, dx1, row(w["ln1_g"]), 0.5, "ln1_bwd", exch=[_rs_sibling([p_in])])
    q_out, own_out = pair(p_out, s_out, "mix_out")
    q_in, own_in = pair(p_in, s_in, "mix_in")
    small_parts = [_rows128(grads[n]) for n in SMALL]
    packed = jnp.concatenate(small_parts + [_rows128(grads["conv_w"]), loss_tile], axis=0)
    p_d1, ((r_in,),) = _weight_grad(h1t, do1, dn, 512, "ffn1_dw_down", exch=[_rs_chips([q_in])])
    p_d1 = p_d1.reshape(N_DEV, f // N_DEV, d)
    (dg1, du1), ((s_d1,), (r_out,), (small_all,)) = _ffn_bwd_act(
        do1, gu1, wd1, "ffn1_bwd_act",
        exch=[_rs_sibling([p_d1]), _rs_chips([q_out]), _small_gather(packed)])
    q_d1, own_d1 = pair(p_d1, s_d1, "ffn1_down")
    p_gu1, ((r_d1,),) = _weight_grad(x0t, dg1, f2s, 512, "ffn1_dw_gate", blocks=N_DEV, exch=[_rs_chips([q_d1])])
    p_gu1, (out["mix_w_in"], out["mix_w_out"]) = _weight_grad(
        x0t, du1, f2s, 512, "ffn1_dw_up", blocks=N_DEV, block_offset=4, into=p_gu1,
        exch=[adamw("mix_w_in", own_in, r_in, 4 * m_tiles), adamw("mix_w_out", own_out, r_out, 4 * m_tiles)])
    (s_gu1,) = _exchange_alone(_rs_sibling([p_gu1]), "ffn1_gate_up_sibling_exchange")
    q_gu1, own_gu1 = pair(p_gu1, s_gu1, "ffn1_gate_up")
    (grad_x,), ((r_gu1,),) = _ffn_bwd_dx(dz1, dg1, du1, wgu1, "ffn1_bwd_dx", exch=[_rs_chips([q_gu1])])
    for n, own, got in (("ffn1_w_down", own_d1, r_d1), ("ffn1_w_gate_up", own_gu1, r_gu1)):
        out[n] = _adamw_sharded(w[n], mom[n], var[n], own, got, "adamw_" + n)

    cw_rows = CONV_TAPS * CONV_CH // 128
    total = _sum_over_devices(small_all)
    offs = [0]
    for p in small_parts:
        offs.append(offs[-1] + p.shape[0])
    n_small = offs[-1]
    loss = total[n_small + cw_rows, 0]
    g_conv_w = lax.dynamic_slice_in_dim(total[n_small:n_small + cw_rows].reshape(CONV_TAPS, CONV_CH),
                                        me * (CONV_CH // N_DEV), CONV_CH // N_DEV, axis=1)
    pad8 = lambda a: jnp.pad(a, ((0, -a.shape[0] % 8), (0, 0)))
    pack = lambda tree, cw: jnp.concatenate([_rows128(tree[n]) for n in SMALL] + [pad8(cw)], axis=0)
    g_pack = jnp.concatenate([total[:n_small], pad8(g_conv_w)], axis=0)
    d_pack, m_pack, v_pack = _adamw_small(pack(w, w["conv_w"]), g_pack, pack(mom, mom["conv_w"]),
                                          pack(var, var["conv_w"]), "adamw_small")
    for k, n in enumerate(SMALL):
        sl = slice(offs[k], offs[k + 1])
        shp = w[n].shape
        out[n] = (total[sl].reshape(shp), d_pack[sl].reshape(shp), m_pack[sl].reshape(shp), v_pack[sl].reshape(shp))
    sl = slice(n_small, n_small + CONV_TAPS)
    out["conv_w"] = (g_conv_w, d_pack[sl], m_pack[sl], v_pack[sl])

    lead = lambda a: a[None]
    res = [loss, grad_x[None]]
    for kind in range(4):
        res += [lead(out[n][kind]) for n in ORDER]
    return tuple(res)
```

```python
import functools
import math

import jax
import jax.numpy as jnp
from jax import lax
from jax.experimental import pallas as pl
from jax.experimental.pallas import tpu as pltpu

F32, BF16 = jnp.float32, jnp.bfloat16
MESH = pl.DeviceIdType.MESH
ANY = pl.BlockSpec(memory_space=pl.ANY)

N_DEV = 8
LN_EPS = 1e-5
ALPHA = 2.0 ** 0.25
CONV_CH = 1024
CONV_TAPS = 31
HALO = 32
HEADS = 8
HEAD_DIM = 128
CHUNK = 128
ADAM_LR, ADAM_B1, ADAM_B2, ADAM_EPS, ADAM_WD, ADAM_STEP = 0.001, 0.9, 0.999, 1e-08, 0.01, 10
V7X_VMEM_LIMIT = 62 * 2 ** 20
EPILOGUE_ROWS = 128

def _cparams(*sem):
    return pltpu.CompilerParams(dimension_semantics=sem, vmem_limit_bytes=V7X_VMEM_LIMIT)


def _tile(n, pref, mult):
    best = None
    for t in range(mult, min(n, pref) + 1, mult):
        if n % t == 0:
            best = t
    return best if best is not None else n


def _dot(a, b):
    return jnp.dot(a, b, preferred_element_type=F32)


def _dot_nt(a, b):
    return lax.dot_general(a, b, (((1,), (1,)), ((), ())), preferred_element_type=F32)


def _sigmoid(x):
    return 1.0 / (1.0 + jnp.exp(-x))


def _ln_stats(z):
    mu = jnp.mean(z, axis=-1, keepdims=True)
    zc = z - mu
    var = jnp.mean(zc * zc, axis=-1, keepdims=True)
    rstd = lax.rsqrt(var + LN_EPS)
    return zc * rstd, rstd


def _ln(z, g, b):
    xh, _ = _ln_stats(z)
    return xh * g + b


def _ln_bwd(dxh, xh, rstd):
    m1 = jnp.mean(dxh, axis=-1, keepdims=True)
    m2 = jnp.mean(dxh * xh, axis=-1, keepdims=True)
    return rstd * (dxh - m1 - xh * m2)


_GK = math.sqrt(2.0 / math.pi)
_GA = 0.044715


def _gelu_and_grad(x):
    x2 = x * x
    t = jnp.tanh(_GK * (x + _GA * x * x2))
    y = 0.5 * x * (1.0 + t)
    dy = 0.5 * (1.0 + t) + 0.5 * x * (1.0 - t * t) * (_GK * (1.0 + 3.0 * _GA * x2))
    return y, dy


def _silu_grad(a):
    s = _sigmoid(a)
    return s * (1.0 + a * (1.0 - s))


def _place():
    return lax.axis_index("x"), lax.axis_index("y"), lax.axis_index("c")


def _other_chips(x, y):
    return [(1 - x, y), (x, 1 - y), (1 - x, 1 - y)]


def _visit_order(x, y, c):
    chips = _other_chips(x, y)
    return [(x, y, c), (x, y, 1 - c), (*chips[0], c), (*chips[1], c), (*chips[0], 1 - c), (*chips[1], 1 - c),
            (*chips[2], c), (*chips[2], 1 - c)]


def _gather_and_gate_up(xb, shards, relayed, order, name):
    n = len(shards)
    N_COPIES = 10
    t, d = xb.shape
    cols = shards[0].shape[1]
    tm = _tile(t, 512, 128)
    ni = t // tm
    col_major = [True] + [False] * (n - 1)

    def body(order_ref, x_ref, *refs):
        srcs, gu_ref, xt_ref, dsts = refs[:n], refs[n], refs[n + 1], refs[n + 2:2 * n + 2]
        wbuf, send_sems, recv_sems, local_sems, load_sem = refs[2 * n + 2:]
        b, i = pl.program_id(0), pl.program_id(1)
        x, y, c = _place()
        me, sib = (x, y, c), (x, y, 1 - c)
        chips = _other_chips(x, y)

        near_x, near_y, far = chips

        def slot(w, p, band=None):
            half = shards[w].shape[0] // 2
            rows = None if band is None else (band * half, half)
            return _block_slot(dsts[w], col_major[w], shards[w].shape[1], p, rows)

        def copy(w, s, block, to, band=None, from_src=False):
            return pltpu.make_async_remote_copy(
                src_ref=srcs[w] if from_src else slot(w, block, band), dst_ref=slot(w, block, band),
                send_sem=send_sems.at[N_COPIES * w + s], recv_sem=recv_sems.at[N_COPIES * w + s],
                device_id=to, device_id_type=MESH)

        def own(w):
            return pltpu.make_async_copy(srcs[w], slot(w, me), local_sems.at[w])

        def sends(w):
            out = [copy(w, 0, me, sib, from_src=True), copy(w, 1, me, (*near_x, c), from_src=True),
                   copy(w, 2, me, (*near_y, c), from_src=True)]
            if not relayed[w]:
                out.append(copy(w, 3, me, (*far, c), from_src=True))
            return out

        def passed_on(w):
            out = [copy(w, 4, (*near_x, c), sib), copy(w, 5, (*near_y, c), sib)]
            if relayed[w]:
                out += [copy(w, 6, (*far, c), sib, band=0), copy(w, 9, (*far, c), sib, band=1),
                        copy(w, 7, (*near_x, c), (*near_y, c), band=0), copy(w, 8, (*near_y, c), (*near_x, c), band=1)]
            else:
                out.append(copy(w, 6, (*far, c), sib))
            return out

        def start_sends(w):
            own(w).start()
            for cp in sends(w):
                cp.start()

        def got_near_x(w):
            copy(w, 1, (*near_x, c), me).wait_recv()
            copy(w, 4, (*near_x, c), sib).start()
            if relayed[w]:
                copy(w, 7, (*near_x, c), (*near_y, c), band=0).start()

        def got_near_y(w):
            copy(w, 2, (*near_y, c), me).wait_recv()
            copy(w, 5, (*near_y, c), sib).start()
            if relayed[w]:
                copy(w, 8, (*near_y, c), (*near_x, c), band=1).start()

        def got_far(w):
            if relayed[w]:
                copy(w, 7, (*far, c), me, band=0).wait_recv()
                copy(w, 6, (*far, c), sib, band=0).start()
                copy(w, 8, (*far, c), me, band=1).wait_recv()
                copy(w, 9, (*far, c), sib, band=1).start()
            else:
                copy(w, 3, (*far, c), me).wait_recv()
                copy(w, 6, (*far, c), sib).start()

        def got_from_sibling(w, which):
            if which == 0:
                copy(w, 0, sib, me).wait_recv()
            elif which == 3 and relayed[w]:
                copy(w, 6, (*far, 1 - c), me, band=0).wait_recv()
                copy(w, 9, (*far, 1 - c), me, band=1).wait_recv()
            else:
                copy(w, 3 + which, (*chips[which - 1], 1 - c), me).wait_recv()

        others = range(1, n)

        def arrive(k):
            if k == 0:
                own(0).wait()
            elif k == 1:
                got_from_sibling(0, 0)
            elif k == 2:
                got_near_x(0)
                for w in others:
                    start_sends(w)
            elif k == 3:
                got_near_y(0)
            elif k in (4, 5):
                got_from_sibling(0, k - 3)
            elif k == 6:
                got_far(0)
                for w in others:
                    got_near_x(w)
                    got_near_y(w)
            else:
                got_from_sibling(0, 3)
                for w in others:
                    got_far(w)

        @pl.when((b == 0) & (i == 0))
        def _():
            start_sends(0)

        for k in range(N_DEV):
            @pl.when((b == k) & (i == 0))
            def _(k=k):
                arrive(k)
                at = pl.multiple_of(order_ref[k] * cols, 128)
                load = pltpu.make_async_copy(dsts[0].at[:, pl.ds(at, cols)], wbuf, load_sem.at[0])
                load.start()
                load.wait()

        gu_ref[...] = _dot(x_ref[...].astype(BF16), wbuf[...]).astype(BF16)

        @pl.when(b == 0)
        def _():
            xt_ref[...] = x_ref[...].T.astype(BF16)

        @pl.when((b == N_DEV - 1) & (i == ni - 1))
        def _():
            for w in others:
                for which in range(4):
                    got_from_sibling(w, which)
                own(w).wait()
            for w in range(n):
                for cp in sends(w) + passed_on(w):
                    cp.wait_send()

    grid_spec = pltpu.PrefetchScalarGridSpec(
        num_scalar_prefetch=1, grid=(N_DEV, ni),
        in_specs=[pl.BlockSpec((tm, d), lambda b, i, o: (i, 0))] + [ANY] * n,
        out_specs=[pl.BlockSpec((tm, cols), lambda b, i, o: (i, o[b])),
                   pl.BlockSpec((d, tm), lambda b, i, o: (0, jnp.where(b == 0, i, ni - 1)))] + [ANY] * n,
        scratch_shapes=[pltpu.VMEM((d, cols), BF16), pltpu.SemaphoreType.DMA((N_COPIES * n,)),
                        pltpu.SemaphoreType.DMA((N_COPIES * n,)), pltpu.SemaphoreType.DMA((n,)),
                        pltpu.SemaphoreType.DMA((1,))])
    res = pl.pallas_call(
        body, name=name, grid_spec=grid_spec,
        out_shape=[jax.ShapeDtypeStruct((t, N_DEV * cols), BF16), jax.ShapeDtypeStruct((d, t), BF16)]
        + [_gathered_shape(s, cm) for s, cm in zip(shards, col_major)],
        compiler_params=_cparams("arbitrary", "arbitrary"),
    )(order, xb, *shards)
    return res[0], res[1], res[2:]


class _Exchange:
    def __init__(self, ins, io, new, n_sems, n_local, make):
        self.ins, self.io, self.new = list(ins), list(io), list(new)
        self.n_sems, self.n_local, self.make = n_sems, n_local, make


def _block_slot(ref, col_major, cols, place, rows=None):
    k = 4 * place[0] + 2 * place[1] + place[2]
    band = slice(None) if rows is None else pl.ds(rows[0], rows[1])
    if col_major:
        return ref.at[band, pl.ds(pl.multiple_of(k * cols, 128), cols)]
    return ref.at[k] if rows is None else ref.at[k, band]


def _gathered_shape(s, col_major):
    return jax.ShapeDtypeStruct((s.shape[0], N_DEV * s.shape[1]) if col_major else (N_DEV,) + s.shape, s.dtype)


def _gather_first(shards, col_major, rows=None, into=None):
    n = len(shards)
    new = [] if into is not None else [_gathered_shape(s, cm) for s, cm in zip(shards, col_major)]

    def make(in_refs, io_refs, new_refs, send_sems, recv_sems, local_sems, base=0, local_base=0):
        x, y, c = _place()
        targets = [(x, y, 1 - c)] + [(*chip, c) for chip in _other_chips(x, y)]
        gathered = io_refs if into is not None else new_refs
        copies = []
        for w in range(n):
            src = in_refs[w] if rows is None else in_refs[w].at[pl.ds(rows[0], rows[1])]
            slot = _block_slot(gathered[w], col_major[w], shards[w].shape[1], (x, y, c), rows)
            copies.append(pltpu.make_async_copy(src, slot, local_sems.at[local_base + w]))
            for s, to in enumerate(targets):
                copies.append(pltpu.make_async_remote_copy(
                    src_ref=src, dst_ref=slot, send_sem=send_sems.at[base + 4 * w + s],
                    recv_sem=recv_sems.at[base + 4 * w + s], device_id=to, device_id_type=MESH))
        return copies

    return _Exchange(shards, into or [], new, 4 * n, n, make)


def _gather_forward(gathered, col_major, cols, rows=None):
    n = len(gathered)

    def make(in_refs, io_refs, new_refs, send_sems, recv_sems, local_sems, base=0, local_base=0):
        x, y, c = _place()
        copies = []
        for w in range(n):
            for j, chip in enumerate(_other_chips(x, y)):
                slot = _block_slot(io_refs[w], col_major[w], cols[w], (*chip, c), rows)
                copies.append(pltpu.make_async_remote_copy(
                    src_ref=slot, dst_ref=slot, send_sem=send_sems.at[base + 3 * w + j],
                    recv_sem=recv_sems.at[base + 3 * w + j], device_id=(x, y, 1 - c), device_id_type=MESH))
        return copies

    return _Exchange([], gathered, [], 3 * n, 0, make)


def _both(a, b):
    def make(in_refs, io_refs, new_refs, send_sems, recv_sems, local_sems):
        na = len(a.ins)
        return (a.make(in_refs[:na], io_refs, [], send_sems, recv_sems, local_sems, 0, 0)
                + b.make(in_refs[na:], io_refs, [], send_sems, recv_sems, local_sems, a.n_sems, a.n_local))

    return _Exchange(a.ins + b.ins, a.io, [], a.n_sems + b.n_sems, a.n_local + b.n_local, make)


def _rs_sibling(parts):
    n = len(parts)

    def make(in_refs, io_refs, new_refs, send_sems, recv_sems, local_sems):
        x, y, c = _place()
        copies = []
        for w in range(n):
            for j in range(4):
                copies.append(pltpu.make_async_remote_copy(
                    src_ref=in_refs[w].at[2 * j + (1 - c)], dst_ref=new_refs[w].at[j],
                    send_sem=send_sems.at[4 * w + j], recv_sem=recv_sems.at[4 * w + j],
                    device_id=(x, y, 1 - c), device_id_type=MESH))
        return copies

    return _Exchange(parts, [], [jax.ShapeDtypeStruct((4,) + p.shape[1:], p.dtype) for p in parts], 4 * n, 0, make)


def _rs_chips(chip_parts, rows=None, into=None):
    n = len(chip_parts)
    band = slice(None) if rows is None else pl.ds(rows[0], rows[1])
    new = [] if into is not None else [jax.ShapeDtypeStruct((3,) + p.shape[1:], p.dtype) for p in chip_parts]

    def make(in_refs, io_refs, new_refs, send_sems, recv_sems, local_sems):
        x, y, c = _place()
        landing = io_refs if into is not None else new_refs
        copies = []
        for w in range(n):
            for rel, (px, py) in enumerate(_other_chips(x, y)):
                copies.append(pltpu.make_async_remote_copy(
                    src_ref=in_refs[w].at[2 * px + py, band], dst_ref=landing[w].at[rel, band],
                    send_sem=send_sems.at[3 * w + rel], recv_sem=recv_sems.at[3 * w + rel],
                    device_id=(px, py, c), device_id_type=MESH))
        return copies

    return _Exchange(chip_parts, into or [], new, 3 * n, 0, make)


class _Side:
    def __init__(self, ins, in_blocks, out_shapes, out_blocks, n_tiles, fn):
        self.ins, self.in_blocks, self.out_shapes, self.out_blocks = list(ins), in_blocks, list(out_shapes), out_blocks
        self.n_tiles, self.fn = n_tiles, fn


def _call(body, exch, *, name, grid, in_specs, out_specs, out_shape, scratch_shapes=(), semantics,
          input_output_aliases=None):
    exch = list(exch)
    in_specs, out_specs, out_shape = list(in_specs), list(out_specs), list(out_shape)
    scratch_shapes = list(scratch_shapes)
    if not exch:
        fn = pl.pallas_call(body, name=name, grid=grid, in_specs=in_specs, out_specs=out_specs, out_shape=out_shape,
                            scratch_shapes=scratch_shapes, input_output_aliases=input_output_aliases or {},
                            compiler_params=_cparams(*semantics))
        return lambda *args: (fn(*args), [])
    n_in, n_out, n_scr = len(in_specs), len(out_specs), len(scratch_shapes)
    aliases = dict(input_output_aliases or {})
    all_in, all_out_specs, all_out_shape, all_scr = list(in_specs), list(out_specs), list(out_shape), list(scratch_shapes)
    extra_args = []

    def step(idx):
        s = idx[0]
        for a in range(1, len(grid)):
            s = s * grid[a] + idx[a]
        return s

    def tile_spec(shape, where, n_tiles):
        return pl.BlockSpec(shape, lambda *idx: where(jnp.minimum(step(idx), n_tiles - 1)))

    for ex in exch:
        if isinstance(ex, _Side):
            all_in += [tile_spec(shape, where, ex.n_tiles) for shape, where in ex.in_blocks]
            extra_args += ex.ins
            all_out_specs += [tile_spec(shape, where, ex.n_tiles) for shape, where in ex.out_blocks]
            all_out_shape += ex.out_shapes
            continue
        for k, a in enumerate(ex.io):
            aliases[len(all_in) + len(ex.ins) + k] = len(all_out_specs) + k
        all_in += [ANY] * (len(ex.ins) + len(ex.io))
        extra_args += ex.ins + ex.io
        all_out_specs += [ANY] * (len(ex.io) + len(ex.new))
        all_out_shape += [jax.ShapeDtypeStruct(a.shape, a.dtype) for a in ex.io] + ex.new
        all_scr += [pltpu.SemaphoreType.DMA((ex.n_sems,)), pltpu.SemaphoreType.DMA((ex.n_sems,)),
                    pltpu.SemaphoreType.DMA((max(ex.n_local, 1),))]

    n_ins = [len(ex.ins) if isinstance(ex, _Side) else len(ex.ins) + len(ex.io) for ex in exch]
    n_outs = [len(ex.out_shapes) if isinstance(ex, _Side) else len(ex.io) + len(ex.new) for ex in exch]

    def wrapped(*refs):
        pos = n_in
        ex_in = []
        for k in n_ins:
            ex_in.append(refs[pos:pos + k])
            pos += k
        outs = refs[pos:pos + n_out]
        pos += n_out
        ex_out = []
        for k in n_outs:
            ex_out.append(refs[pos:pos + k])
            pos += k
        scr = refs[pos:pos + n_scr]
        pos += n_scr
        idx = [pl.program_id(a) for a in range(len(grid))]
        first = functools.reduce(jnp.logical_and, [i == 0 for i in idx])
        last = functools.reduce(jnp.logical_and, [i == g - 1 for i, g in zip(idx, grid)])

        def copies():
            out, at = [], pos
            for ex, ei, eo in zip(exch, ex_in, ex_out):
                if not isinstance(ex, _Side):
                    out += ex.make(ei[:len(ex.ins)], eo[:len(ex.io)], eo[len(ex.io):], *refs[at:at + 3])
                    at += 3
            return out

        @pl.when(first)
        def _():
            for cp in copies():
                cp.start()

        body(*refs[:n_in], *outs, *scr)
        for ex, ei, eo in zip(exch, ex_in, ex_out):
            if isinstance(ex, _Side):
                pl.when(step(idx) < ex.n_tiles)(functools.partial(ex.fn, ei, eo))

        @pl.when(last)
        def _():
            for cp in copies():
                cp.wait()

    fn = pl.pallas_call(wrapped, name=name, grid=grid, in_specs=all_in, out_specs=all_out_specs,
                        out_shape=all_out_shape, scratch_shapes=all_scr, input_output_aliases=aliases,
                        compiler_params=_cparams(*(["arbitrary"] * len(grid))))

    def run(*args):
        res = fn(*args, *extra_args)
        outs, pos, ex_res = res[:n_out], n_out, []
        for k in n_outs:
            ex_res.append(list(res[pos:pos + k]))
            pos += k
        return outs, ex_res

    return run


def _exchange_alone(ex, name):
    def body():
        pass

    _, res = _call(body, [ex], name=name, grid=(1,), in_specs=[], out_specs=[], out_shape=[], semantics=("arbitrary",))()
    return res[0]


def _small_gather(part):
    def make(in_refs, io_refs, new_refs, send_sems, recv_sems, local_sems):
        x, y, c = _place()
        slot = new_refs[0].at[4 * x + 2 * y + c]
        copies = [pltpu.make_async_copy(in_refs[0], slot, local_sems.at[0])]
        for d in range(1, N_DEV):
            peer = (1 - x if d & 4 else x, 1 - y if d & 2 else y, 1 - c if d & 1 else c)
            copies.append(pltpu.make_async_remote_copy(
                src_ref=in_refs[0], dst_ref=slot, send_sem=send_sems.at[d - 1], recv_sem=recv_sems.at[d - 1],
                device_id=peer, device_id_type=MESH))
        return copies

    return _Exchange([part], [], [jax.ShapeDtypeStruct((N_DEV,) + part.shape, part.dtype)], N_DEV - 1, 1, make)


def _sum_over_devices(parts):
    _, rows, lanes = parts.shape

    def body(p_ref, o_ref):
        acc = p_ref[0]
        for k in range(1, N_DEV):
            acc = acc + p_ref[k]
        o_ref[...] = acc

    return pl.pallas_call(
        body, name="small_grads_sum", grid=(1,), out_shape=jax.ShapeDtypeStruct((rows, lanes), F32),
        in_specs=[pl.BlockSpec((N_DEV, rows, lanes), lambda i: (0, 0, 0))],
        out_specs=pl.BlockSpec((rows, lanes), lambda i: (0, 0)),
        compiler_params=_cparams("arbitrary"),
    )(parts)


def _transpose_bf16(a, name, exch=(), with_copy=False):
    r, c = a.shape
    tr, tc = _tile(r, 512, 128), _tile(c, 512, 128)

    def body(a_ref, o_ref, *copy_ref):
        v = a_ref[...].astype(F32)
        o_ref[...] = v.T.astype(BF16)
        if with_copy:
            copy_ref[0][...] = v.astype(BF16)

    outs, ex = _call(
        body, exch, name=name, grid=(r // tr, c // tc),
        out_shape=[jax.ShapeDtypeStruct((c, r), BF16)] + [jax.ShapeDtypeStruct((r, c), BF16)] * with_copy,
        in_specs=[pl.BlockSpec((tr, tc), lambda i, j: (i, j))],
        out_specs=[pl.BlockSpec((tc, tr), lambda i, j: (j, i))] + [pl.BlockSpec((tr, tc), lambda i, j: (i, j))] * with_copy,
        semantics=("parallel", "parallel"),
    )(a)
    return (outs if with_copy else outs[0]), ex


def _ffn_fwd_loss(x, wgu, wd, ln_g, ln_b, target, name, exch=()):
    t, d = x.shape
    f = wd.shape[0]
    tm, tf = _tile(t, 512, 128), _tile(f, 512, 128)
    nf = f // tf

    def body(x_ref, wg_ref, wu_ref, wd_ref, lg_ref, lb_ref, t_ref,
             go_ref, uo_ref, ht_ref, dz_ref, dzb_ref, dlg_ref, dlb_ref, loss_ref, xb, acc):
        i, j = pl.program_id(0), pl.program_id(1)

        @pl.when(j == 0)
        def _():
            xb[...] = x_ref[...].astype(BF16)
            acc[...] = jnp.zeros_like(acc)

        @pl.when((i == 0) & (j == 0))
        def _():
            dlg_ref[...] = jnp.zeros_like(dlg_ref)
            dlb_ref[...] = jnp.zeros_like(dlb_ref)
            loss_ref[...] = jnp.zeros_like(loss_ref)

        g = _dot(xb[...], wg_ref[...])
        u = _dot(xb[...], wu_ref[...])
        h = g * _sigmoid(g) * u
        go_ref[...] = g.astype(BF16)
        uo_ref[...] = u.astype(BF16)
        ht_ref[...] = h.T.astype(BF16)
        acc[...] += _dot(h.astype(BF16), wd_ref[...])

        @pl.when(j == nf - 1)
        def _():
            for r in range(0, tm, EPILOGUE_ROWS):
                rows = slice(r, r + EPILOGUE_ROWS)
                xh, rstd = _ln_stats(ALPHA * x_ref[rows, :] + 0.5 * acc[rows, :])
                e = xh * lg_ref[...] + lb_ref[...] - t_ref[rows, :]
                loss_ref[...] += 0.5 * jnp.sum(jnp.sum(e * e, axis=-1, keepdims=True) * (1.0 / d), axis=0,
                                               keepdims=True)
                dy = e * (1.0 / d)
                dz = _ln_bwd(dy * lg_ref[...], xh, rstd)
                dz_ref[rows, :] = dz
                dzb_ref[rows, :] = (0.5 * dz).astype(BF16)
                dlg_ref[...] += jnp.sum(dy * xh, axis=0, keepdims=True)
                dlb_ref[...] += jnp.sum(dy, axis=0, keepdims=True)

    row = lambda i, j: (i, 0)
    fixed = lambda i, j: (0, 0)
    return _call(
        body, exch, name=name, grid=(t // tm, nf),
        out_shape=[jax.ShapeDtypeStruct((t, f), BF16), jax.ShapeDtypeStruct((t, f), BF16),
                   jax.ShapeDtypeStruct((f, t), BF16), jax.ShapeDtypeStruct((t, d), F32),
                   jax.ShapeDtypeStruct((t, d), BF16), jax.ShapeDtypeStruct((1, d), F32),
                   jax.ShapeDtypeStruct((1, d), F32), jax.ShapeDtypeStruct((8, 128), F32)],
        in_specs=[pl.BlockSpec((tm, d), row),
                  pl.BlockSpec((d, tf), lambda i, j: (0, j)),
                  pl.BlockSpec((d, tf), lambda i, j: (0, j + nf)),
                  pl.BlockSpec((tf, d), lambda i, j: (j, 0)),
                  pl.BlockSpec((1, d), fixed), pl.BlockSpec((1, d), fixed), pl.BlockSpec((tm, d), row)],
        out_specs=[pl.BlockSpec((tm, tf), lambda i, j: (i, j)), pl.BlockSpec((tm, tf), lambda i, j: (i, j)),
                   pl.BlockSpec((tf, tm), lambda i, j: (j, i)), pl.BlockSpec((tm, d), row), pl.BlockSpec((tm, d), row),
                   pl.BlockSpec((1, d), fixed), pl.BlockSpec((1, d), fixed), pl.BlockSpec((8, 128), fixed)],
        scratch_shapes=[pltpu.VMEM((tm, d), BF16), pltpu.VMEM((tm, d), F32)],
        semantics=("arbitrary", "arbitrary"),
    )(x, wgu, wgu, wd, ln_g, ln_b, target)


def _ffn_down_fwd(gu, x, wd, ln_g, ln_b, name, exch=()):
    t, d = x.shape
    f = wd.shape[0]
    tm, tf = _tile(t, 512, 128), _tile(f, 512, 128)
    nf = f // tf

    def body(g_ref, u_ref, wd_ref, x_ref, lg_ref, lb_ref, ht_ref, z_ref, xn_ref, acc):
        j = pl.program_id(1)

        @pl.when(j == 0)
        def _():
            acc[...] = jnp.zeros_like(acc)

        g = g_ref[...].astype(F32)
        h = g * _sigmoid(g) * u_ref[...].astype(F32)
        ht_ref[...] = h.T.astype(BF16)
        acc[...] += _dot(h.astype(BF16), wd_ref[...])

        @pl.when(j == nf - 1)
        def _():
            z = ALPHA * x_ref[...] + 0.5 * acc[...]
            z_ref[...] = z
            xn_ref[...] = _ln(z, lg_ref[...], lb_ref[...])

    row = lambda i, j: (i, 0)
    fixed = lambda i, j: (0, 0)
    return _call(
        body, exch, name=name, grid=(t // tm, nf),
        out_shape=[jax.ShapeDtypeStruct((f, t), BF16), jax.ShapeDtypeStruct((t, d), F32),
                   jax.ShapeDtypeStruct((t, d), F32)],
        in_specs=[pl.BlockSpec((tm, tf), lambda i, j: (i, j)), pl.BlockSpec((tm, tf), lambda i, j: (i, j + nf)),
                  pl.BlockSpec((tf, d), lambda i, j: (j, 0)), pl.BlockSpec((tm, d), row),
                  pl.BlockSpec((1, d), fixed), pl.BlockSpec((1, d), fixed)],
        out_specs=[pl.BlockSpec((tf, tm), lambda i, j: (j, i)), pl.BlockSpec((tm, d), row), pl.BlockSpec((tm, d), row)],
        scratch_shapes=[pltpu.VMEM((tm, d), F32)],
        semantics=("parallel", "arbitrary"),
    )(gu, gu, wd, x, ln_g, ln_b)


def _ffn_act_grads(dh, g_ref, u_ref):
    gg = g_ref[...].astype(F32)
    uu = u_ref[...].astype(F32)
    s = _sigmoid(gg)
    du = (dh * (gg * s)).astype(BF16)
    dg = (dh * uu * (s * (1.0 + gg * (1.0 - s)))).astype(BF16)
    return dg, du


def _ffn_bwd(dz, do, g, u, wgu, wd, name, exch=()):
    t, d = dz.shape
    f = wd.shape[0]
    tm, tf = _tile(t, 512, 128), _tile(f, 512, 128)
    nf = f // tf

    def body(dz_ref, do_ref, g_ref, u_ref, wg_ref, wu_ref, wd_ref, dg_ref, du_ref, dx_ref, acc):
        j = pl.program_id(1)

        @pl.when(j == 0)
        def _():
            acc[...] = jnp.zeros_like(acc)

        dg, du = _ffn_act_grads(_dot_nt(do_ref[...], wd_ref[...]), g_ref, u_ref)
        dg_ref[...] = dg
        du_ref[...] = du
        acc[...] += _dot_nt(dg, wg_ref[...]) + _dot_nt(du, wu_ref[...])

        @pl.when(j == nf - 1)
        def _():
            dx_ref[...] = ALPHA * dz_ref[...] + acc[...]

    row = lambda i, j: (i, 0)
    tile = lambda i, j: (i, j)
    return _call(
        body, exch, name=name, grid=(t // tm, nf),
        out_shape=[jax.ShapeDtypeStruct((t, f), BF16), jax.ShapeDtypeStruct((t, f), BF16),
                   jax.ShapeDtypeStruct((t, d), F32)],
        in_specs=[pl.BlockSpec((tm, d), row), pl.BlockSpec((tm, d), row),
                  pl.BlockSpec((tm, tf), tile), pl.BlockSpec((tm, tf), tile),
                  pl.BlockSpec((d, tf), lambda i, j: (0, j)),
                  pl.BlockSpec((d, tf), lambda i, j: (0, j + nf)),
                  pl.BlockSpec((tf, d), lambda i, j: (j, 0))],
        out_specs=[pl.BlockSpec((tm, tf), tile), pl.BlockSpec((tm, tf), tile), pl.BlockSpec((tm, d), row)],
        scratch_shapes=[pltpu.VMEM((tm, d), F32)],
        semantics=("parallel", "arbitrary"),
    )(dz, do, g, u, wgu, wgu, wd)


def _ffn_bwd_act(do, gu, wd, name, exch=()):
    t, d = do.shape
    f = wd.shape[0]
    tm, tf = _tile(t, 512, 128), _tile(f, 512, 128)
    nf = f // tf

    def body(do_ref, g_ref, u_ref, wd_ref, dg_ref, du_ref):
        dg, du = _ffn_act_grads(_dot_nt(do_ref[...], wd_ref[...]), g_ref, u_ref)
        dg_ref[...] = dg
        du_ref[...] = du

    tile = lambda i, j: (i, j)
    return _call(
        body, exch, name=name, grid=(t // tm, f // tf),
        out_shape=[jax.ShapeDtypeStruct((t, f), BF16), jax.ShapeDtypeStruct((t, f), BF16)],
        in_specs=[pl.BlockSpec((tm, d), lambda i, j: (i, 0)), pl.BlockSpec((tm, tf), tile),
                  pl.BlockSpec((tm, tf), lambda i, j: (i, j + nf)), pl.BlockSpec((tf, d), lambda i, j: (j, 0))],
        out_specs=[pl.BlockSpec((tm, tf), tile), pl.BlockSpec((tm, tf), tile)],
        semantics=("parallel", "parallel"),
    )(do, gu, gu, wd)


def _ffn_bwd_dx(dz, dg, du, wgu, name, exch=()):
    t, d = dz.shape
    f = dg.shape[1]
    tm, tn = _tile(t, 512, 128), _tile(d, 256, 128)

    def body(dz_ref, dg_ref, du_ref, wg_ref, wu_ref, dx_ref):
        dx_ref[...] = ALPHA * dz_ref[...] + _dot_nt(dg_ref[...], wg_ref[...]) + _dot_nt(du_ref[...], wu_ref[...])

    row = lambda i, n: (i, 0)
    tile = lambda i, n: (i, n)
    return _call(
        body, exch, name=name, grid=(t // tm, d // tn), out_shape=[jax.ShapeDtypeStruct((t, d), F32)],
        in_specs=[pl.BlockSpec((tm, tn), tile), pl.BlockSpec((tm, f), row), pl.BlockSpec((tm, f), row),
                  pl.BlockSpec((tn, f), lambda i, n: (n, 0)), pl.BlockSpec((tn, f), lambda i, n: (n, 1))],
        out_specs=[pl.BlockSpec((tm, tn), tile)],
        semantics=("parallel", "arbitrary"),
    )(dz, dg, du, wgu, wgu)


def _weight_grad(at, b, tn, tmm, name, blocks=None, block_offset=0, into=None, exch=()):
    m, t = at.shape
    nn = b.shape[1]
    tmm = _tile(m, tmm, 16)
    assert nn % tn == 0

    def body(*refs):
        at_ref, b_ref, o_ref = refs[0], refs[1], refs[-1]
        r = _dot(at_ref[...], b_ref[...]).astype(BF16)
        if blocks is None:
            o_ref[...] = r
        else:
            o_ref[0] = r

    in_specs = [pl.BlockSpec((tmm, t), lambda n, i: (i, 0)), pl.BlockSpec((t, tn), lambda n, i: (0, n))]
    args = [at, b]
    aliases = {}
    if into is not None:
        in_specs.append(ANY)
        args.append(into)
        aliases = {2: 0}
    if blocks is None:
        out_shape = jax.ShapeDtypeStruct((m, nn), BF16)
        out_spec = pl.BlockSpec((tmm, tn), lambda n, i: (i, n))
    else:
        out_shape = jax.ShapeDtypeStruct((blocks, m, tn), BF16)
        out_spec = pl.BlockSpec((1, tmm, tn), lambda n, i: (n + block_offset, i, 0))
    (out,), ex = _call(
        body, exch, name=name, grid=(nn // tn, m // tmm), out_shape=[out_shape],
        in_specs=in_specs, out_specs=[out_spec], input_output_aliases=aliases,
        semantics=("parallel", "parallel"),
    )(*args)
    return out, ex


def _mix_in_proj(x, w_in, name, exch=()):
    t, d = x.shape
    n_out = w_in.shape[1]
    tm, cb = _tile(t, 512, 128), _tile(n_out, 512, 128)

    def body(x_ref, w_ref, o_ref, xb):
        @pl.when(pl.program_id(1) == 0)
        def _():
            xb[...] = x_ref[...].astype(BF16)

        o_ref[...] = _dot(xb[...], w_ref[...])

    (out,), ex = _call(
        body, exch, name=name, grid=(t // tm, n_out // cb), out_shape=[jax.ShapeDtypeStruct((t, n_out), F32)],
        in_specs=[pl.BlockSpec((tm, d), lambda i, k: (i, 0)), pl.BlockSpec((d, cb), lambda i, k: (0, k))],
        out_specs=[pl.BlockSpec((tm, cb), lambda i, k: (i, k))],
        scratch_shapes=[pltpu.VMEM((tm, d), BF16)],
        semantics=("parallel", "arbitrary"),
    )(x, w_in)
    return out, ex


def _mix_in_bwd(dproj, w_in, dz, name, exch=()):
    t, d = dz.shape
    kk = w_in.shape[1]
    tm, tn = _tile(t, 512, 128), _tile(d, 256, 128)

    def body(dp_ref, w_ref, dz_ref, dx_ref):
        dx_ref[...] = ALPHA * dz_ref[...] + _dot_nt(dp_ref[...], w_ref[...])

    (out,), ex = _call(
        body, exch, name=name, grid=(t // tm, d // tn), out_shape=[jax.ShapeDtypeStruct((t, d), F32)],
        in_specs=[pl.BlockSpec((tm, kk), lambda i, n: (i, 0)), pl.BlockSpec((tn, kk), lambda i, n: (n, 0)),
                  pl.BlockSpec((tm, tn), lambda i, n: (i, n))],
        out_specs=[pl.BlockSpec((tm, tn), lambda i, n: (i, n))],
        semantics=("parallel", "arbitrary"),
    )(dproj, w_in, dz)
    return out, ex


def _mix_out_fwd(y, w_out, x, ln_g, ln_b, name, exch=()):
    t, d = x.shape
    kk = y.shape[1]
    tm = _tile(t, 256, 128)

    def body(y_ref, w_ref, x_ref, g_ref, b_ref, z_ref, xn_ref, xnt_ref):
        z = ALPHA * x_ref[...] + _dot(y_ref[...], w_ref[...])
        z_ref[...] = z
        xn = _ln(z, g_ref[...], b_ref[...])
        xn_ref[...] = xn
        xnt_ref[...] = xn.T.astype(BF16)

    row = lambda i: (i, 0)
    fixed = lambda i: (0, 0)
    return _call(
        body, exch, name=name, grid=(t // tm,),
        out_shape=[jax.ShapeDtypeStruct((t, d), F32), jax.ShapeDtypeStruct((t, d), F32),
                   jax.ShapeDtypeStruct((d, t), BF16)],
        in_specs=[pl.BlockSpec((tm, kk), row), pl.BlockSpec((kk, d), fixed), pl.BlockSpec((tm, d), row),
                  pl.BlockSpec((1, d), fixed), pl.BlockSpec((1, d), fixed)],
        out_specs=[pl.BlockSpec((tm, d), row), pl.BlockSpec((tm, d), row), pl.BlockSpec((d, tm), lambda i: (0, i))],
        semantics=("parallel",),
    )(y, w_out, x, ln_g, ln_b)


def _mix_out_bwd(dzb, w_out, name, exch=()):
    t, d = dzb.shape
    kk = w_out.shape[0]
    tm = _tile(t, 256, 128)

    def body(dz_ref, w_ref, dy_ref):
        dy_ref[...] = _dot_nt(dz_ref[...], w_ref[...])

    (out,), ex = _call(
        body, exch, name=name, grid=(t // tm,), out_shape=[jax.ShapeDtypeStruct((t, kk), F32)],
        in_specs=[pl.BlockSpec((tm, d), lambda i: (i, 0)), pl.BlockSpec((kk, d), lambda i: (0, 0))],
        out_specs=[pl.BlockSpec((tm, kk), lambda i: (i, 0))],
        semantics=("parallel",),
    )(dzb, w_out)
    return out, ex


def _loss_ln_bwd(z, target, ln_g, ln_b, bf16_scale, name):
    t, d = z.shape
    tm = _tile(t, 512, 8)

    def body(z_ref, t_ref, g_ref, b_ref, dz_ref, dzb_ref, dg_ref, db_ref, loss_ref):
        @pl.when(pl.program_id(0) == 0)
        def _():
            dg_ref[...] = jnp.zeros_like(dg_ref)
            db_ref[...] = jnp.zeros_like(db_ref)
            loss_ref[...] = jnp.zeros_like(loss_ref)

        xh, rstd = _ln_stats(z_ref[...])
        e = xh * g_ref[...] + b_ref[...] - t_ref[...]
        loss_ref[...] += 0.5 * jnp.sum(jnp.sum(e * e, axis=-1, keepdims=True) * (1.0 / d), axis=0, keepdims=True)
        dy = e * (1.0 / d)
        dz = _ln_bwd(dy * g_ref[...], xh, rstd)
        dz_ref[...] = dz
        dzb_ref[...] = (bf16_scale * dz).astype(BF16)
        dg_ref[...] += jnp.sum(dy * xh, axis=0, keepdims=True)
        db_ref[...] += jnp.sum(dy, axis=0, keepdims=True)

    row = lambda i: (i, 0)
    fixed = lambda i: (0, 0)
    return pl.pallas_call(
        body, name=name, grid=(t // tm,),
        out_shape=[jax.ShapeDtypeStruct((t, d), F32), jax.ShapeDtypeStruct((t, d), BF16),
                   jax.ShapeDtypeStruct((1, d), F32), jax.ShapeDtypeStruct((1, d), F32),
                   jax.ShapeDtypeStruct((8, 128), F32)],
        in_specs=[pl.BlockSpec((tm, d), row), pl.BlockSpec((tm, d), row), pl.BlockSpec((1, d), fixed),
                  pl.BlockSpec((1, d), fixed)],
        out_specs=[pl.BlockSpec((tm, d), row), pl.BlockSpec((tm, d), row), pl.BlockSpec((1, d), fixed),
                   pl.BlockSpec((1, d), fixed), pl.BlockSpec((8, 128), fixed)],
        compiler_params=_cparams("arbitrary"),
    )(z, target, ln_g, ln_b)


def _ln_bwd_call(z, dy, ln_g, bf16_scale, name, exch=()):
    t, d = z.shape
    tm = _tile(t, 512, 8)

    def body(z_ref, dy_ref, g_ref, dz_ref, dzb_ref, dg_ref, db_ref):
        @pl.when(pl.program_id(0) == 0)
        def _():
            dg_ref[...] = jnp.zeros_like(dg_ref)
            db_ref[...] = jnp.zeros_like(db_ref)

        xh, rstd = _ln_stats(z_ref[...])
        dy = dy_ref[...]
        dz = _ln_bwd(dy * g_ref[...], xh, rstd)
        dz_ref[...] = dz
        dzb_ref[...] = (bf16_scale * dz).astype(BF16)
        dg_ref[...] += jnp.sum(dy * xh, axis=0, keepdims=True)
        db_ref[...] += jnp.sum(dy, axis=0, keepdims=True)

    row = lambda i: (i, 0)
    fixed = lambda i: (0, 0)
    return _call(
        body, exch, name=name, grid=(t // tm,),
        out_shape=[jax.ShapeDtypeStruct((t, d), F32), jax.ShapeDtypeStruct((t, d), BF16),
                   jax.ShapeDtypeStruct((1, d), F32), jax.ShapeDtypeStruct((1, d), F32)],
        in_specs=[pl.BlockSpec((tm, d), row), pl.BlockSpec((tm, d), row), pl.BlockSpec((1, d), fixed)],
        out_specs=[pl.BlockSpec((tm, d), row), pl.BlockSpec((tm, d), row), pl.BlockSpec((1, d), fixed),
                   pl.BlockSpec((1, d), fixed)],
        semantics=("arbitrary",),
    )(z, dy, ln_g)


CONV_ROWS = 32
SUBLANES = 8


def _fill_shifted(ext, shifted):
    rows = ext.shape[0] - SUBLANES
    for s in range(1, SUBLANES):
        for r in range(0, rows, CONV_ROWS):
            n = min(CONV_ROWS, rows - r)
            shifted[s - 1, r:r + n, :] = ext[r + s:r + s + n, :]


def _window(ext, shifted, lo, n):
    s = lo % SUBLANES
    return ext[lo:lo + n, :] if s == 0 else shifted[s - 1, lo - s:lo - s + n, :]


def _mixer_fwd(proj, conv_w, conv_b, cln_g, cln_b, sln_g, sln_b, sg_wm, sg_bb, name, exch=()):
    t = proj.shape[0]
    tm = _tile(t, 256, CHUNK)
    hb = tm // HALO
    nc = tm // CHUNK
    ch = CONV_CH

    def body(av_ref, ag_ref, bu_ref, bv_ref, hv_ref, hg_ref, cw_ref, cb_ref, lg_ref, lb_ref, sg_ref, sb_ref,
             w_ref, bb_ref, y_ref, yt_ref, c_ref, ext, ext_s):
        i = pl.program_id(0)
        halo = hv_ref[...] * _sigmoid(hg_ref[...])
        ext[0:HALO, :] = jnp.where(i > 0, halo, 0.0)
        ext[HALO:HALO + tm, :] = av_ref[...] * _sigmoid(ag_ref[...])
        _fill_shifted(ext, ext_s)
        for r in range(0, tm, CONV_ROWS):
            acc = jnp.zeros((CONV_ROWS, ch), F32) + cb_ref[...]
            for k in range(CONV_TAPS):
                lo = r + k + HALO - (CONV_TAPS - 1)
                acc = acc + cw_ref[k:k + 1, :] * _window(ext, ext_s, lo, CONV_ROWS)
            c_ref[r:r + CONV_ROWS, :] = acc
        a = _ln(c_ref[...], lg_ref[...], lb_ref[...])
        ya = a * _sigmoid(a)
        y_ref[:, 0:ch] = ya.astype(BF16)
        yt_ref[0:ch, :] = ya.T.astype(BF16)
        for h in range(HEADS):
            sl = slice(h * HEAD_DIM, (h + 1) * HEAD_DIM)
            u, _ = _gelu_and_grad(bu_ref[:, sl])
            v, _ = _gelu_and_grad(bv_ref[:, sl])
            vn = _ln(v, sg_ref[h:h + 1, :], sb_ref[h:h + 1, :])
            vn3 = vn.astype(BF16).reshape(nc, CHUNK, HEAD_DIM)
            wb = jnp.broadcast_to(w_ref[h][None], (nc, CHUNK, CHUNK))
            mixed = jnp.einsum("cts,csd->ctd", wb, vn3, preferred_element_type=F32) + bb_ref[h][None]
            yb = u * mixed.reshape(tm, HEAD_DIM)
            y_ref[:, ch + h * HEAD_DIM:ch + (h + 1) * HEAD_DIM] = yb.astype(BF16)
            yt_ref[ch + h * HEAD_DIM:ch + (h + 1) * HEAD_DIM, :] = yb.T.astype(BF16)

    col = lambda cidx: (lambda i: (i, cidx))
    prev = lambda cidx: (lambda i: (jnp.maximum(i * hb - 1, 0), cidx))
    fix2 = lambda i: (0, 0)
    fix3 = lambda i: (0, 0, 0)
    return _call(
        body, exch, name=name, grid=(t // tm,),
        out_shape=[jax.ShapeDtypeStruct((t, 2 * ch), BF16), jax.ShapeDtypeStruct((2 * ch, t), BF16),
                   jax.ShapeDtypeStruct((t, ch), F32)],
        in_specs=[pl.BlockSpec((tm, ch), col(0)), pl.BlockSpec((tm, ch), col(1)), pl.BlockSpec((tm, ch), col(2)),
                  pl.BlockSpec((tm, ch), col(3)), pl.BlockSpec((HALO, ch), prev(0)), pl.BlockSpec((HALO, ch), prev(1)),
                  pl.BlockSpec((CONV_TAPS, ch), fix2), pl.BlockSpec((1, ch), fix2), pl.BlockSpec((1, ch), fix2),
                  pl.BlockSpec((1, ch), fix2), pl.BlockSpec((HEADS, HEAD_DIM), fix2), pl.BlockSpec((HEADS, HEAD_DIM), fix2),
                  pl.BlockSpec((HEADS, CHUNK, CHUNK), fix3), pl.BlockSpec((HEADS, CHUNK, HEAD_DIM), fix3)],
        out_specs=[pl.BlockSpec((tm, 2 * ch), lambda i: (i, 0)), pl.BlockSpec((2 * ch, tm), lambda i: (0, i)),
                   pl.BlockSpec((tm, ch), lambda i: (i, 0))],
        scratch_shapes=[pltpu.VMEM((HALO + tm, ch), F32), pltpu.VMEM((SUBLANES - 1, HALO + tm, ch), F32)],
        semantics=("parallel",),
    )(proj, proj, proj, proj, proj, proj, conv_w, conv_b, cln_g, cln_b, sln_g, sln_b, sg_wm, sg_bb)


def _mixer_bwd(proj, conv_c, dy, conv_w, cln_g, cln_b, sln_g, sln_b, sg_wm, sg_wmt, sg_bb, name, exch=()):
    t = proj.shape[0]
    tm = _tile(t, 256, CHUNK)
    hb = tm // HALO
    nc = tm // CHUNK
    nt = t // tm
    ch = CONV_CH
    last_halo = t // HALO - 1

    def body(av_ref, ag_ref, bu_ref, bv_ref, hv_ref, hg_ref, c_ref, cn_ref, dya_ref, dyan_ref, dyb_ref,
             cw_ref, lg_ref, lb_ref, sg_ref, sb_ref, w_ref, wt_ref, bb_ref,
             dp_ref, dcw_ref, dcb_ref, dlg_ref, dlb_ref, dsg_ref, dsb_ref, dw_ref, dbs_ref,
             ext_h, ext_dc, ext_hs, ext_dcs, acc_cw):
        i = pl.program_id(0)

        @pl.when(i == 0)
        def _():
            acc_cw[...] = jnp.zeros_like(acc_cw)
            for ref in (dcb_ref, dlg_ref, dlb_ref, dsg_ref, dsb_ref, dw_ref, dbs_ref):
                ref[...] = jnp.zeros_like(ref)

        lg = lg_ref[...]
        lb = lb_ref[...]

        def conv_ln_bwd(c, dya):
            xh, rstd = _ln_stats(c)
            a = xh * lg + lb
            da = dya * _silu_grad(a)
            return _ln_bwd(da * lg, xh, rstd), da, xh

        fold = lambda v: jnp.sum(v.reshape(CONV_ROWS // SUBLANES, SUBLANES, ch), axis=0)
        s_lg = s_lb = s_cb = jnp.zeros((SUBLANES, ch), F32)
        for r in range(0, tm, CONV_ROWS):
            dc, da, xh = conv_ln_bwd(c_ref[r:r + CONV_ROWS, :], dya_ref[r:r + CONV_ROWS, :])
            ext_dc[r:r + CONV_ROWS, :] = dc
            s_lg, s_lb, s_cb = s_lg + fold(da * xh), s_lb + fold(da), s_cb + fold(dc)
        dlg_ref[...] += jnp.sum(s_lg, axis=0, keepdims=True)
        dlb_ref[...] += jnp.sum(s_lb, axis=0, keepdims=True)
        dcb_ref[...] += jnp.sum(s_cb, axis=0, keepdims=True)
        dcn, _, _ = conv_ln_bwd(cn_ref[...], dyan_ref[...])
        ext_dc[tm:tm + HALO, :] = jnp.where(i < nt - 1, dcn, 0.0)
        halo = hv_ref[...] * _sigmoid(hg_ref[...])
        ext_h[0:HALO, :] = jnp.where(i > 0, halo, 0.0)
        ext_h[HALO:HALO + tm, :] = av_ref[...] * _sigmoid(ag_ref[...])
        _fill_shifted(ext_h, ext_hs)
        _fill_shifted(ext_dc, ext_dcs)
        for r in range(0, tm, CONV_ROWS):
            dcr = ext_dc[r:r + CONV_ROWS, :]
            acc = jnp.zeros((CONV_ROWS, ch), F32)
            for k in range(CONV_TAPS):
                lo = r + k + HALO - (CONV_TAPS - 1)
                prod = dcr * _window(ext_h, ext_hs, lo, CONV_ROWS)
                acc_cw[k] += jnp.sum(prod.reshape(CONV_ROWS // 8, 8, ch), axis=0)
                hi = r + (CONV_TAPS - 1) - k
                acc = acc + cw_ref[k:k + 1, :] * _window(ext_dc, ext_dcs, hi, CONV_ROWS)
            sg_r = _sigmoid(ag_ref[r:r + CONV_ROWS, :])
            av_r = av_ref[r:r + CONV_ROWS, :]
            dp_ref[r:r + CONV_ROWS, 0:ch] = (acc * sg_r).astype(BF16)
            dp_ref[r:r + CONV_ROWS, ch:2 * ch] = (acc * av_r * sg_r * (1.0 - sg_r)).astype(BF16)

        @pl.when(i == nt - 1)
        def _():
            dcw_ref[...] = jnp.sum(acc_cw[...], axis=1)

        tril = (lax.broadcasted_iota(jnp.int32, (CHUNK, CHUNK), 0)
                >= lax.broadcasted_iota(jnp.int32, (CHUNK, CHUNK), 1)).astype(F32)
        for h in range(HEADS):
            sl = slice(h * HEAD_DIM, (h + 1) * HEAD_DIM)
            u, du_dx = _gelu_and_grad(bu_ref[:, sl])
            v, dv_dx = _gelu_and_grad(bv_ref[:, sl])
            xhv, rstdv = _ln_stats(v)
            gh = sg_ref[h:h + 1, :]
            vn3 = (xhv * gh + sb_ref[h:h + 1, :]).astype(BF16).reshape(nc, CHUNK, HEAD_DIM)
            wb = jnp.broadcast_to(w_ref[h][None], (nc, CHUNK, CHUNK))
            mixed = jnp.einsum("cts,csd->ctd", wb, vn3, preferred_element_type=F32) + bb_ref[h][None]
            dyb = dyb_ref[:, sl]
            d_u = dyb * mixed.reshape(tm, HEAD_DIM)
            dm = dyb * u
            dm3 = dm.reshape(nc, CHUNK, HEAD_DIM)
            dbs_ref[h:h + 1, :] += jnp.sum(jnp.sum(dm3, axis=0).T, axis=0, keepdims=True)
            dm3b = dm3.astype(BF16)
            dw_h = jnp.sum(jnp.einsum("ctd,csd->cts", dm3b, vn3, preferred_element_type=F32), axis=0)
            dw_ref[h] += dw_h * tril
            wtb = jnp.broadcast_to(wt_ref[h][None], (nc, CHUNK, CHUNK))
            d_vn = jnp.einsum("cst,ctd->csd", wtb, dm3b, preferred_element_type=F32).reshape(tm, HEAD_DIM)
            dsg_ref[h:h + 1, :] += jnp.sum(d_vn * xhv, axis=0, keepdims=True)
            dsb_ref[h:h + 1, :] += jnp.sum(d_vn, axis=0, keepdims=True)
            dv = _ln_bwd(d_vn * gh, xhv, rstdv)
            dp_ref[:, 2 * ch + h * HEAD_DIM:2 * ch + (h + 1) * HEAD_DIM] = (d_u * du_dx).astype(BF16)
            dp_ref[:, 3 * ch + h * HEAD_DIM:3 * ch + (h + 1) * HEAD_DIM] = (dv * dv_dx).astype(BF16)

    col = lambda cidx: (lambda i: (i, cidx))
    prev = lambda cidx: (lambda i: (jnp.maximum(i * hb - 1, 0), cidx))
    nxt = lambda i: (jnp.minimum((i + 1) * hb, last_halo), 0)
    fix2 = lambda i: (0, 0)
    fix3 = lambda i: (0, 0, 0)
    out_shape = [jax.ShapeDtypeStruct((t, 4 * ch), BF16), jax.ShapeDtypeStruct((CONV_TAPS, ch), F32),
                 jax.ShapeDtypeStruct((1, ch), F32), jax.ShapeDtypeStruct((1, ch), F32), jax.ShapeDtypeStruct((1, ch), F32),
                 jax.ShapeDtypeStruct((HEADS, HEAD_DIM), F32), jax.ShapeDtypeStruct((HEADS, HEAD_DIM), F32),
                 jax.ShapeDtypeStruct((HEADS, CHUNK, CHUNK), F32), jax.ShapeDtypeStruct((HEADS, CHUNK), F32)]
    out_specs = [pl.BlockSpec((tm, 4 * ch), lambda i: (i, 0)), pl.BlockSpec((CONV_TAPS, ch), fix2),
                 pl.BlockSpec((1, ch), fix2), pl.BlockSpec((1, ch), fix2), pl.BlockSpec((1, ch), fix2),
                 pl.BlockSpec((HEADS, HEAD_DIM), fix2), pl.BlockSpec((HEADS, HEAD_DIM), fix2),
                 pl.BlockSpec((HEADS, CHUNK, CHUNK), fix3), pl.BlockSpec((HEADS, CHUNK), fix2)]
    in_specs = [pl.BlockSpec((tm, ch), col(0)), pl.BlockSpec((tm, ch), col(1)), pl.BlockSpec((tm, ch), col(2)),
                pl.BlockSpec((tm, ch), col(3)), pl.BlockSpec((HALO, ch), prev(0)), pl.BlockSpec((HALO, ch), prev(1)),
                pl.BlockSpec((tm, ch), col(0)), pl.BlockSpec((HALO, ch), nxt),
                pl.BlockSpec((tm, ch), col(0)), pl.BlockSpec((HALO, ch), nxt), pl.BlockSpec((tm, ch), col(1)),
                pl.BlockSpec((CONV_TAPS, ch), fix2), pl.BlockSpec((1, ch), fix2), pl.BlockSpec((1, ch), fix2),
                pl.BlockSpec((HEADS, HEAD_DIM), fix2), pl.BlockSpec((HEADS, HEAD_DIM), fix2),
                pl.BlockSpec((HEADS, CHUNK, CHUNK), fix3), pl.BlockSpec((HEADS, CHUNK, CHUNK), fix3),
                pl.BlockSpec((HEADS, CHUNK, HEAD_DIM), fix3)]
    return _call(
        body, exch, name=name, grid=(nt,), out_shape=out_shape, in_specs=in_specs, out_specs=out_specs,
        scratch_shapes=[pltpu.VMEM((HALO + tm, ch), F32), pltpu.VMEM((tm + HALO, ch), F32),
                        pltpu.VMEM((SUBLANES - 1, HALO + tm, ch), F32), pltpu.VMEM((SUBLANES - 1, tm + HALO, ch), F32),
                        pltpu.VMEM((CONV_TAPS, 8, ch), F32)],
        semantics=("arbitrary",),
    )(proj, proj, proj, proj, proj, proj, conv_c, conv_c, dy, dy, dy,
      conv_w, cln_g, cln_b, sln_g, sln_b, sg_wm, sg_wmt, sg_bb)


def _pair_sum(parts, from_sibling, core_chip, name):
    _, r, cc = parts.shape
    tr = _tile(r, max(16, (1 << 20) // (2 * cc)), 16)

    def body(cc_ref, p_ref, s_ref, o_ref, own_ref):
        q = (p_ref[...].astype(F32) + s_ref[...].astype(F32)).astype(BF16)
        o_ref[...] = q

        @pl.when(pl.program_id(1) == cc_ref[1])
        def _():
            own_ref[...] = q[0]

    grid_spec = pltpu.PrefetchScalarGridSpec(
        num_scalar_prefetch=1, grid=(r // tr, 4),
        in_specs=[pl.BlockSpec((1, tr, cc), lambda i, j, cc_ref: (2 * j + cc_ref[0], i, 0)),
                  pl.BlockSpec((1, tr, cc), lambda i, j, cc_ref: (j, i, 0))],
        out_specs=[pl.BlockSpec((1, tr, cc), lambda i, j, cc_ref: (j, i, 0)),
                   pl.BlockSpec((tr, cc), lambda i, j, cc_ref: (i, 0))])
    return pl.pallas_call(
        body, name=name, grid_spec=grid_spec,
        out_shape=[jax.ShapeDtypeStruct((4, r, cc), BF16), jax.ShapeDtypeStruct((r, cc), BF16)],
        compiler_params=_cparams("parallel", "arbitrary"),
    )(core_chip, parts, from_sibling)


def _adamw_math(w, g, m, v):
    m = ADAM_B1 * m + (1.0 - ADAM_B1) * g
    v = ADAM_B2 * v + (1.0 - ADAM_B2) * (g * g)
    m_hat = m / (1.0 - ADAM_B1 ** ADAM_STEP)
    v_hat = v / (1.0 - ADAM_B2 ** ADAM_STEP)
    delta = -ADAM_LR * (m_hat / (jnp.sqrt(v_hat) + ADAM_EPS) + ADAM_WD * w)
    return delta, m, v


def _adamw_tile(in_refs, out_refs):
    w_ref, m_ref, v_ref, q_ref, o_ref = in_refs
    g = q_ref[...].astype(F32)
    for k in range(3):
        g = g + o_ref[k].astype(F32)
    d, mm, vv = _adamw_math(w_ref[...], g, m_ref[...], v_ref[...])
    for ref, val in zip(out_refs, (g, d, mm, vv)):
        ref[...] = val


def _adamw_side(w, m, v, chip_part, from_chips, max_tiles):
    r, cc = w.shape
    n = max(k for k in range(1, max_tiles + 1) if r % k == 0 and (r // k) % 16 == 0)
    tr = r // n
    row = ((tr, cc), lambda s: (s, 0))
    return _Side([w, m, v, chip_part, from_chips], [row, row, row, row, ((3, tr, cc), lambda s: (0, s, 0))],
                 [jax.ShapeDtypeStruct((r, cc), F32)] * 4, [row] * 4, n, _adamw_tile)


def _adamw_sharded(w, m, v, chip_part, from_chips, name):
    r, cc = w.shape
    tr = _tile(r, max(16, (1 << 19) // (4 * cc) * 2), 16)

    def body(*refs):
        _adamw_tile(refs[:5], refs[5:])

    row = pl.BlockSpec((tr, cc), lambda i: (i, 0))
    return pl.pallas_call(
        body, name=name, grid=(r // tr,), out_shape=[jax.ShapeDtypeStruct((r, cc), F32)] * 4,
        in_specs=[row, row, row, row, pl.BlockSpec((3, tr, cc), lambda i: (0, i, 0))], out_specs=[row] * 4,
        compiler_params=_cparams("parallel"),
    )(w, m, v, chip_part, from_chips)


def _adamw_small(w, g, m, v, name):
    r, cc = w.shape

    def body(w_ref, g_ref, m_ref, v_ref, d_out, m_out, v_out):
        d, mm, vv = _adamw_math(w_ref[...], g_ref[...], m_ref[...], v_ref[...])
        d_out[...] = d
        m_out[...] = mm
        v_out[...] = vv

    full = pl.BlockSpec((r, cc), lambda i: (0, 0))
    return pl.pallas_call(
        body, name=name, grid=(1,), out_shape=[jax.ShapeDtypeStruct((r, cc), F32)] * 3,
        in_specs=[full] * 4, out_specs=[full] * 3, compiler_params=_cparams("arbitrary"),
    )(w, g, m, v)


SMALL = ("ln1_g", "ln1_b", "conv_b", "conv_ln_g", "conv_ln_b", "sg_ln_g", "sg_ln_b", "sg_w", "sg_b",
         "ln2_g", "ln2_b", "ln3_g", "ln3_b")
ORDER = ("ffn1_w_gate_up", "ffn1_w_down", "ln1_g", "ln1_b", "mix_w_in", "conv_w", "conv_b", "conv_ln_g", "conv_ln_b",
         "sg_ln_g", "sg_ln_b", "sg_w", "sg_b", "mix_w_out", "ln2_g", "ln2_b", "ffn2_w_gate_up", "ffn2_w_down",
         "ln3_g", "ln3_b")


def _rows128(a):
    return a.reshape(-1, 128)


def kernel(x, ffn1_w_gate_up, ffn1_w_down, ln1_g, ln1_b, mix_w_in, conv_w, conv_b, conv_ln_g, conv_ln_b, sg_ln_g, sg_ln_b, sg_w, sg_b, mix_w_out, ln2_g, ln2_b, ffn2_w_gate_up, ffn2_w_down, ln3_g, ln3_b, loss_target, m_ffn1_w_gate_up, m_ffn1_w_down, m_ln1_g, m_ln1_b, m_mix_w_in, m_conv_w, m_conv_b, m_conv_ln_g, m_conv_ln_b, m_sg_ln_g, m_sg_ln_b, m_sg_w, m_sg_b, m_mix_w_out, m_ln2_g, m_ln2_b, m_ffn2_w_gate_up, m_ffn2_w_down, m_ln3_g, m_ln3_b, v_ffn1_w_gate_up, v_ffn1_w_down, v_ln1_g, v_ln1_b, v_mix_w_in, v_conv_w, v_conv_b, v_conv_ln_g, v_conv_ln_b, v_sg_ln_g, v_sg_ln_b, v_sg_w, v_sg_b, v_mix_w_out, v_ln2_g, v_ln2_b, v_ffn2_w_gate_up, v_ffn2_w_down, v_ln3_g, v_ln3_b):
    args = dict(locals())
    w = {n: args[n][0] for n in ORDER}
    mom = {n: args["m_" + n][0] for n in ORDER}
    var = {n: args["v_" + n][0] for n in ORDER}
    x0 = x[0]
    target = loss_target[0]
    t, d = x0.shape
    my_x, my_y, my_c = lax.axis_index("x"), lax.axis_index("y"), lax.axis_index("c")
    my_chip = (2 * my_x + my_y).astype(jnp.int32).reshape(1)
    my_core = my_c.astype(jnp.int32).reshape(1)
    me = 4 * my_x + 2 * my_y + my_c

    big = ("ffn1_w_gate_up", "ffn1_w_down", "mix_w_in", "mix_w_out", "ffn2_w_gate_up", "ffn2_w_down")
    sh = {n: w[n].astype(BF16) for n in big}
    f2s = sh["ffn2_w_gate_up"].shape[1]
    order = jnp.stack([4 * p[0] + 2 * p[1] + p[2] for p in _visit_order(my_x, my_y, my_c)]).astype(jnp.int32)
    gu1, x0t, (wgu1, wd1, conv_w_all) = _gather_and_gate_up(
        x0, [sh["ffn1_w_gate_up"], sh["ffn1_w_down"], w["conv_w"]], [True, True, False], order, "ffn1_gate_up_fwd")
    wd1 = wd1.reshape(-1, d)
    conv_w_full = jnp.transpose(conv_w_all, (1, 0, 2)).reshape(CONV_TAPS, CONV_CH)
    tril = jnp.tril(jnp.ones((CHUNK, CHUNK), F32))
    sg_wm = w["sg_w"] * tril
    sg_wm_b = sg_wm.astype(BF16)
    sg_wmt_b = jnp.swapaxes(sg_wm, 1, 2).astype(BF16)
    sg_bb = jnp.broadcast_to(w["sg_b"][:, :, None], (HEADS, CHUNK, HEAD_DIM))
    row = lambda a: a.reshape(1, -1)

    d2 = [sh["ffn2_w_down"]]
    d2_first = d2[0].shape[0] * 3 // 5 // 16 * 16
    d2_top, d2_bottom = (0, d2_first), (d2_first, d2[0].shape[0] - d2_first)
    (h1t, z1, x1), ((g_in, g_out), (g_d2,)) = _ffn_down_fwd(
        gu1, x0, wd1, row(w["ln1_g"]), row(w["ln1_b"]), "ffn1_down_fwd",
        exch=[_gather_first([sh["mix_w_in"], sh["mix_w_out"]], [True, False]),
              _gather_first(d2, [False], rows=d2_top)])
    in_cols = sh["mix_w_in"].shape[1]
    x1t, ((w_in, w_out), (g_d2,)) = _transpose_bf16(
        x1, "x1_transpose", exch=[_gather_forward([g_in, g_out], [True, False], [in_cols, None]),
                                  _gather_forward([g_d2], [False], [None], rows=d2_top)])
    w_out = w_out.reshape(-1, d)
    top, bottom = (0, d // 2), (d // 2, d // 2)
    gu2 = [sh["ffn2_w_gate_up"]]
    proj, ((g_gu2,),) = _mix_in_proj(x1, w_in, "mix_in_fwd", exch=[_gather_first(gu2, [True], rows=top)])
    (y, yt, conv_c), ((g_gu2,),) = _mixer_fwd(
        proj, conv_w_full, row(w["conv_b"]), row(w["conv_ln_g"]), row(w["conv_ln_b"]),
        w["sg_ln_g"], w["sg_ln_b"], sg_wm_b, sg_bb, "mixer_fwd",
        exch=[_both(_gather_first(gu2, [True], rows=bottom, into=[g_gu2]),
                    _gather_forward([g_gu2], [True], [f2s], rows=top))])
    (z2, x2, x2t), ((wgu2,), (g_d2,)) = _mix_out_fwd(
        y, w_out, x1, row(w["ln2_g"]), row(w["ln2_b"]), "mix_out_fwd",
        exch=[_gather_forward([g_gu2], [True], [f2s], rows=bottom),
              _gather_first(d2, [False], rows=d2_bottom, into=[g_d2])])
    (wd2,) = _exchange_alone(_gather_forward([g_d2], [False], [None], rows=d2_bottom), "ffn2_down_gather_forward")
    wd2 = wd2.reshape(-1, d)
    grads = {}
    (g2, u2, h2t, dz3, do2, grads["ln3_g"], grads["ln3_b"], loss_tile), _ = _ffn_fwd_loss(
        x2, wgu2, wd2, row(w["ln3_g"]), row(w["ln3_b"]), target, "ffn2_fwd_loss")

    f = wd1.shape[0]
    dn = _tile(d, 1024, 128)
    core_chip = jnp.concatenate([my_core, my_chip])
    pair = lambda p, s, label: _pair_sum(p, s, core_chip, "pair_sum_" + label)
    adamw = lambda n, own, got, steps: _adamw_side(w[n], mom[n], var[n], own, got, steps)
    m_tiles = d // _tile(d, 512, 16)
    gu_first = d * 2 // 3 // 16 * 16
    out = {}
    p_d2, _ = _weight_grad(h2t, do2, dn, 512, "ffn2_dw_down")
    p_d2 = p_d2.reshape(N_DEV, f // N_DEV, d)
    (dg2, du2, dx2), ((s_d2,),) = _ffn_bwd(dz3, do2, g2, u2, wgu2, wd2, "ffn2_bwd", exch=[_rs_sibling([p_d2])])
    q_d2, own_d2 = pair(p_d2, s_d2, "ffn2_down")
    d_rows = q_d2.shape[1]
    d_half = d_rows // 2 // 16 * 16
    p_gu2, ((r_d2,),) = _weight_grad(x2t, dg2, f2s, 512, "ffn2_dw_gate", blocks=N_DEV,
                                     exch=[_rs_chips([q_d2], rows=(0, d_half))])
    p_gu2, ((r_d2,),) = _weight_grad(x2t, du2, f2s, 512, "ffn2_dw_up", blocks=N_DEV, block_offset=4, into=p_gu2,
                                     exch=[_rs_chips([q_d2], rows=(d_half, d_rows - d_half), into=[r_d2])])
    (dz2, dz2b, grads["ln2_g"], grads["ln2_b"]), ((s_gu2,),) = _ln_bwd_call(
        z2, dx2, row(w["ln2_g"]), 1.0, "ln2_bwd", exch=[_rs_sibling([p_gu2])])
    q_gu2, own_gu2 = pair(p_gu2, s_gu2, "ffn2_gate_up")
    dy, _ = _mix_out_bwd(dz2b, w_out, "mix_out_bwd")
    p_out, _ = _weight_grad(yt, dz2b, dn, 512, "mix_out_dw")
    p_out = p_out.reshape(N_DEV, -1, d)
    (dproj, grads["conv_w"], grads["conv_b"], grads["conv_ln_g"], grads["conv_ln_b"], grads["sg_ln_g"],
     grads["sg_ln_b"], grads["sg_w"], grads["sg_b"]), ((r_gu2,),) = _mixer_bwd(
        proj, conv_c, dy, conv_w_full, row(w["conv_ln_g"]), row(w["conv_ln_b"]), w["sg_ln_g"], w["sg_ln_b"],
        sg_wm_b, sg_wmt_b, sg_bb, "mixer_bwd", exch=[_rs_chips([q_gu2], rows=(0, gu_first))])
    dx1, ((s_out,), (r_gu2,)) = _mix_in_bwd(
        dproj, w_in, dz2, "mix_in_bwd",
        exch=[_rs_sibling([p_out]), _rs_chips([q_gu2], rows=(gu_first, d - gu_first), into=[r_gu2])])
    p_in, (out["ffn2_w_gate_up"], out["ffn2_w_down"]) = _weight_grad(
        x1t, dproj, in_cols, 512, "mix_in_dw", blocks=N_DEV,
        exch=[adamw("ffn2_w_gate_up", own_gu2, r_gu2, N_DEV * m_tiles), adamw("ffn2_w_down", own_d2, r_d2, N_DEV * m_tiles)])
    (dz1, do1, grads["ln1_g"], grads["ln1_b"]), ((s_in,),) = _ln_bwd_call(
        z1, dx1, row(w["ln1_g"]), 0.5, "ln1_bwd", exch=[_rs_sibling([p_in])])
    q_out, own_out = pair(p_out, s_out, "mix_out")
    q_in, own_in = pair(p_in, s_in, "mix_in")
    small_parts = [_rows128(grads[n]) for n in SMALL]
    packed = jnp.concatenate(small_parts + [_rows128(grads["conv_w"]), loss_tile], axis=0)
    p_d1, ((r_in,),) = _weight_grad(h1t, do1, dn, 512, "ffn1_dw_down", exch=[_rs_chips([q_in])])
    p_d1 = p_d1.reshape(N_DEV, f // N_DEV, d)
    (dg1, du1), ((s_d1,), (r_out,), (small_all,)) = _ffn_bwd_act(
        do1, gu1, wd1, "ffn1_bwd_act",
        exch=[_rs_sibling([p_d1]), _rs_chips([q_out]), _small_gather(packed)])
    q_d1, own_d1 = pair(p_d1, s_d1, "ffn1_down")
    p_gu1, ((r_d1,),) = _weight_grad(x0t, dg1, f2s, 512, "ffn1_dw_gate", blocks=N_DEV, exch=[_rs_chips([q_d1])])
    p_gu1, (out["mix_w_in"], out["mix_w_out"]) = _weight_grad(
        x0t, du1, f2s, 512, "ffn1_dw_up", blocks=N_DEV, block_offset=4, into=p_gu1,
        exch=[adamw("mix_w_in", own_in, r_in, 4 * m_tiles), adamw("mix_w_out", own_out, r_out, 4 * m_tiles)])
    (s_gu1,) = _exchange_alone(_rs_sibling([p_gu1]), "ffn1_gate_up_sibling_exchange")
    q_gu1, own_gu1 = pair(p_gu1, s_gu1, "ffn1_gate_up")
    (grad_x,), ((r_gu1,),) = _ffn_bwd_dx(dz1, dg1, du1, wgu1, "ffn1_bwd_dx", exch=[_rs_chips([q_gu1])])
    for n, own, got in (("ffn1_w_down", own_d1, r_d1), ("ffn1_w_gate_up", own_gu1, r_gu1)):
        out[n] = _adamw_sharded(w[n], mom[n], var[n], own, got, "adamw_" + n)

    cw_rows = CONV_TAPS * CONV_CH // 128
    total = _sum_over_devices(small_all)
    offs = [0]
    for p in small_parts:
        offs.append(offs[-1] + p.shape[0])
    n_small = offs[-1]
    loss = total[n_small + cw_rows, 0]
    g_conv_w = lax.dynamic_slice_in_dim(total[n_small:n_small + cw_rows].reshape(CONV_TAPS, CONV_CH),
                                        me * (CONV_CH // N_DEV), CONV_CH // N_DEV, axis=1)
    pad8 = lambda a: jnp.pad(a, ((0, -a.shape[0] % 8), (0, 0)))
    pack = lambda tree, cw: jnp.concatenate([_rows128(tree[n]) for n in SMALL] + [pad8(cw)], axis=0)
    g_pack = jnp.concatenate([total[:n_small], pad8(g_conv_w)], axis=0)
    d_pack, m_pack, v_pack = _adamw_small(pack(w, w["conv_w"]), g_pack, pack(mom, mom["conv_w"]),
                                          pack(var, var["conv_w"]), "adamw_small")
    for k, n in enumerate(SMALL):
        sl = slice(offs[k], offs[k + 1])
        shp = w[n].shape
        out[n] = (total[sl].reshape(shp), d_pack[sl].reshape(shp), m_pack[sl].reshape(shp), v_pack[sl].reshape(shp))
    sl = slice(n_small, n_small + CONV_TAPS)
    out["conv_w"] = (g_conv_w, d_pack[sl], m_pack[sl], v_pack[sl])

    lead = lambda a: a[None]
    res = [loss, grad_x[None]]
    for kind in range(4):
        res += [lead(out[n][kind]) for n in ORDER]
    return tuple(res)
```

```python
import functools
import math

import jax
import jax.numpy as jnp
from jax import lax
from jax.experimental import pallas as pl
from jax.experimental.pallas import tpu as pltpu

F32, BF16 = jnp.float32, jnp.bfloat16
MESH = pl.DeviceIdType.MESH
ANY = pl.BlockSpec(memory_space=pl.ANY)

N_DEV = 8
LN_EPS = 1e-5
ALPHA = 2.0 ** 0.25
CONV_CH = 1024
CONV_TAPS = 31
HALO = 32
HEADS = 8
HEAD_DIM = 128
CHUNK = 128
ADAM_LR, ADAM_B1, ADAM_B2, ADAM_EPS, ADAM_WD, ADAM_STEP = 0.001, 0.9, 0.999, 1e-08, 0.01, 10
V7X_VMEM_LIMIT = 62 * 2 ** 20
EPILOGUE_ROWS = 128

def _cparams(*sem):
    return pltpu.CompilerParams(dimension_semantics=sem, vmem_limit_bytes=V7X_VMEM_LIMIT)


def _tile(n, pref, mult):
    best = None
    for t in range(mult, min(n, pref) + 1, mult):
        if n % t == 0:
            best = t
    return best if best is not None else n


def _dot(a, b):
    return jnp.dot(a, b, preferred_element_type=F32)


def _dot_nt(a, b):
    return lax.dot_general(a, b, (((1,), (1,)), ((), ())), preferred_element_type=F32)


def _sigmoid(x):
    return 1.0 / (1.0 + jnp.exp(-x))


def _ln_stats(z):
    mu = jnp.mean(z, axis=-1, keepdims=True)
    zc = z - mu
    var = jnp.mean(zc * zc, axis=-1, keepdims=True)
    rstd = lax.rsqrt(var + LN_EPS)
    return zc * rstd, rstd


def _ln(z, g, b):
    xh, _ = _ln_stats(z)
    return xh * g + b


def _ln_bwd(dxh, xh, rstd):
    m1 = jnp.mean(dxh, axis=-1, keepdims=True)
    m2 = jnp.mean(dxh * xh, axis=-1, keepdims=True)
    return rstd * (dxh - m1 - xh * m2)


_GK = math.sqrt(2.0 / math.pi)
_GA = 0.044715


def _gelu_and_grad(x):
    x2 = x * x
    t = jnp.tanh(_GK * (x + _GA * x * x2))
    y = 0.5 * x * (1.0 + t)
    dy = 0.5 * (1.0 + t) + 0.5 * x * (1.0 - t * t) * (_GK * (1.0 + 3.0 * _GA * x2))
    return y, dy


def _silu_grad(a):
    s = _sigmoid(a)
    return s * (1.0 + a * (1.0 - s))


def _place():
    return lax.axis_index("x"), lax.axis_index("y"), lax.axis_index("c")


def _other_chips(x, y):
    return [(1 - x, y), (x, 1 - y), (1 - x, 1 - y)]


def _visit_order(x, y, c):
    chips = _other_chips(x, y)
    return [(x, y, c), (x, y, 1 - c), (*chips[0], c), (*chips[1], c), (*chips[0], 1 - c), (*chips[1], 1 - c),
            (*chips[2], c), (*chips[2], 1 - c)]


def _gather_and_gate_up(xb, shards, relayed, order, name):
    n = len(shards)
    N_COPIES = 10
    t, d = xb.shape
    cols = shards[0].shape[1]
    tm = _tile(t, 512, 128)
    ni = t // tm
    col_major = [True] + [False] * (n - 1)

    def body(order_ref, x_ref, *refs):
        srcs, gu_ref, xt_ref, dsts = refs[:n], refs[n], refs[n + 1], refs[n + 2:2 * n + 2]
        wbuf, send_sems, recv_sems, local_sems, load_sem = refs[2 * n + 2:]
        b, i = pl.program_id(0), pl.program_id(1)
        x, y, c = _place()
        me, sib = (x, y, c), (x, y, 1 - c)
        chips = _other_chips(x, y)

        near_x, near_y, far = chips

        def slot(w, p, band=None):
            half = shards[w].shape[0] // 2
            rows = None if band is None else (band * half, half)
            return _block_slot(dsts[w], col_major[w], shards[w].shape[1], p, rows)

        def copy(w, s, block, to, band=None, from_src=False):
            return pltpu.make_async_remote_copy(
                src_ref=srcs[w] if from_src else slot(w, block, band), dst_ref=slot(w, block, band),
                send_sem=send_sems.at[N_COPIES * w + s], recv_sem=recv_sems.at[N_COPIES * w + s],
                device_id=to, device_id_type=MESH)

        def own(w):
            return pltpu.make_async_copy(srcs[w], slot(w, me), local_sems.at[w])

        def sends(w):
            out = [copy(w, 0, me, sib, from_src=True), copy(w, 1, me, (*near_x, c), from_src=True),
                   copy(w, 2, me, (*near_y, c), from_src=True)]
            if not relayed[w]:
                out.append(copy(w, 3, me, (*far, c), from_src=True))
            return out

        def passed_on(w):
            out = [copy(w, 4, (*near_x, c), sib), copy(w, 5, (*near_y, c), sib)]
            if relayed[w]:
                out += [copy(w, 6, (*far, c), sib, band=0), copy(w, 9, (*far, c), sib, band=1),
                        copy(w, 7, (*near_x, c), (*near_y, c), band=0), copy(w, 8, (*near_y, c), (*near_x, c), band=1)]
            else:
                out.append(copy(w, 6, (*far, c), sib))
            return out

        def start_sends(w):
            own(w).start()
            for cp in sends(w):
                cp.start()

        def got_near_x(w):
            copy(w, 1, (*near_x, c), me).wait_recv()
            copy(w, 4, (*near_x, c), sib).start()
            if relayed[w]:
                copy(w, 7, (*near_x, c), (*near_y, c), band=0).start()

        def got_near_y(w):
            copy(w, 2, (*near_y, c), me).wait_recv()
            copy(w, 5, (*near_y, c), sib).start()
            if relayed[w]:
                copy(w, 8, (*near_y, c), (*near_x, c), band=1).start()

        def got_far(w):
            if relayed[w]:
                copy(w, 7, (*far, c), me, band=0).wait_recv()
                copy(w, 6, (*far, c), sib, band=0).start()
                copy(w, 8, (*far, c), me, band=1).wait_recv()
                copy(w, 9, (*far, c), sib, band=1).start()
            else:
                copy(w, 3, (*far, c), me).wait_recv()
                copy(w, 6, (*far, c), sib).start()

        def got_from_sibling(w, which):
            if which == 0:
                copy(w, 0, sib, me).wait_recv()
            elif which == 3 and relayed[w]:
                copy(w, 6, (*far, 1 - c), me, band=0).wait_recv()
                copy(w, 9, (*far, 1 - c), me, band=1).wait_recv()
            else:
                copy(w, 3 + which, (*chips[which - 1], 1 - c), me).wait_recv()

        others = range(1, n)

        def arrive(k):
            if k == 0:
                own(0).wait()
            elif k == 1:
                got_from_sibling(0, 0)
            elif k == 2:
                got_near_x(0)
                for w in others:
                    start_sends(w)
            elif k == 3:
                got_near_y(0)
            elif k in (4, 5):
                got_from_sibling(0, k - 3)
            elif k == 6:
                got_far(0)
                for w in others:
                    got_near_x(w)
                    got_near_y(w)
            else:
                got_from_sibling(0, 3)
                for w in others:
                    got_far(w)

        def load(k):
            at = pl.multiple_of(order_ref[k] * cols, 128)
            return pltpu.make_async_copy(dsts[0].at[:, pl.ds(at, cols)], wbuf.at[k % 2], load_sem.at[k % 2])

        @pl.when((b == 0) & (i == 0))
        def _():
            start_sends(0)
            arrive(0)
            load(0).start()
            load(0).wait()

        early = max(ni - 2, 0)
        for k in range(1, N_DEV):
            @pl.when((b == k - 1) & (i == early))
            def _(k=k):
                arrive(k)
                load(k).start()

            @pl.when((b == k) & (i == 0))
            def _(k=k):
                load(k).wait()

        gu_ref[...] = _dot(x_ref[...].astype(BF16), wbuf[b % 2]).astype(BF16)

        @pl.when(b == 0)
        def _():
            xt_ref[...] = x_ref[...].T.astype(BF16)

        @pl.when((b == N_DEV - 1) & (i == ni - 1))
        def _():
            for w in others:
                for which in range(4):
                    got_from_sibling(w, which)
                own(w).wait()
            for w in range(n):
                for cp in sends(w) + passed_on(w):
                    cp.wait_send()

    grid_spec = pltpu.PrefetchScalarGridSpec(
        num_scalar_prefetch=1, grid=(N_DEV, ni),
        in_specs=[pl.BlockSpec((tm, d), lambda b, i, o: (i, 0))] + [ANY] * n,
        out_specs=[pl.BlockSpec((tm, cols), lambda b, i, o: (i, o[b])),
                   pl.BlockSpec((d, tm), lambda b, i, o: (0, jnp.where(b == 0, i, ni - 1)))] + [ANY] * n,
        scratch_shapes=[pltpu.VMEM((2, d, cols), BF16), pltpu.SemaphoreType.DMA((N_COPIES * n,)),
                        pltpu.SemaphoreType.DMA((N_COPIES * n,)), pltpu.SemaphoreType.DMA((n,)),
                        pltpu.SemaphoreType.DMA((2,))])
    res = pl.pallas_call(
        body, name=name, grid_spec=grid_spec,
        out_shape=[jax.ShapeDtypeStruct((t, N_DEV * cols), BF16), jax.ShapeDtypeStruct((d, t), BF16)]
        + [_gathered_shape(s, cm) for s, cm in zip(shards, col_major)],
        compiler_params=_cparams("arbitrary", "arbitrary"),
    )(order, xb, *shards)
    return res[0], res[1], res[2:]


class _Exchange:
    def __init__(self, ins, io, new, n_sems, n_local, make):
        self.ins, self.io, self.new = list(ins), list(io), list(new)
        self.n_sems, self.n_local, self.make = n_sems, n_local, make


def _block_slot(ref, col_major, cols, place, rows=None):
    k = 4 * place[0] + 2 * place[1] + place[2]
    band = slice(None) if rows is None else pl.ds(rows[0], rows[1])
    if col_major:
        return ref.at[band, pl.ds(pl.multiple_of(k * cols, 128), cols)]
    return ref.at[k] if rows is None else ref.at[k, band]


def _gathered_shape(s, col_major):
    return jax.ShapeDtypeStruct((s.shape[0], N_DEV * s.shape[1]) if col_major else (N_DEV,) + s.shape, s.dtype)


def _gather_first(shards, col_major, rows=None, into=None):
    n = len(shards)
    new = [] if into is not None else [_gathered_shape(s, cm) for s, cm in zip(shards, col_major)]

    def make(in_refs, io_refs, new_refs, send_sems, recv_sems, local_sems, base=0, local_base=0):
        x, y, c = _place()
        targets = [(x, y, 1 - c)] + [(*chip, c) for chip in _other_chips(x, y)]
        gathered = io_refs if into is not None else new_refs
        copies = []
        for w in range(n):
            src = in_refs[w] if rows is None else in_refs[w].at[pl.ds(rows[0], rows[1])]
            slot = _block_slot(gathered[w], col_major[w], shards[w].shape[1], (x, y, c), rows)
            copies.append(pltpu.make_async_copy(src, slot, local_sems.at[local_base + w]))
            for s, to in enumerate(targets):
                copies.append(pltpu.make_async_remote_copy(
                    src_ref=src, dst_ref=slot, send_sem=send_sems.at[base + 4 * w + s],
                    recv_sem=recv_sems.at[base + 4 * w + s], device_id=to, device_id_type=MESH))
        return copies

    return _Exchange(shards, into or [], new, 4 * n, n, make)


def _gather_forward(gathered, col_major, cols, rows=None):
    n = len(gathered)

    def make(in_refs, io_refs, new_refs, send_sems, recv_sems, local_sems, base=0, local_base=0):
        x, y, c = _place()
        copies = []
        for w in range(n):
            for j, chip in enumerate(_other_chips(x, y)):
                slot = _block_slot(io_refs[w], col_major[w], cols[w], (*chip, c), rows)
                copies.append(pltpu.make_async_remote_copy(
                    src_ref=slot, dst_ref=slot, send_sem=send_sems.at[base + 3 * w + j],
                    recv_sem=recv_sems.at[base + 3 * w + j], device_id=(x, y, 1 - c), device_id_type=MESH))
        return copies

    return _Exchange([], gathered, [], 3 * n, 0, make)


def _both(a, b):
    def make(in_refs, io_refs, new_refs, send_sems, recv_sems, local_sems):
        na = len(a.ins)
        return (a.make(in_refs[:na], io_refs, [], send_sems, recv_sems, local_sems, 0, 0)
                + b.make(in_refs[na:], io_refs, [], send_sems, recv_sems, local_sems, a.n_sems, a.n_local))

    return _Exchange(a.ins + b.ins, a.io, [], a.n_sems + b.n_sems, a.n_local + b.n_local, make)


def _rs_sibling(parts):
    n = len(parts)

    def make(in_refs, io_refs, new_refs, send_sems, recv_sems, local_sems):
        x, y, c = _place()
        copies = []
        for w in range(n):
            for j in range(4):
                copies.append(pltpu.make_async_remote_copy(
                    src_ref=in_refs[w].at[2 * j + (1 - c)], dst_ref=new_refs[w].at[j],
                    send_sem=send_sems.at[4 * w + j], recv_sem=recv_sems.at[4 * w + j],
                    device_id=(x, y, 1 - c), device_id_type=MESH))
        return copies

    return _Exchange(parts, [], [jax.ShapeDtypeStruct((4,) + p.shape[1:], p.dtype) for p in parts], 4 * n, 0, make)


def _rs_chips(chip_parts, rows=None, into=None):
    n = len(chip_parts)
    band = slice(None) if rows is None else pl.ds(rows[0], rows[1])
    new = [] if into is not None else [jax.ShapeDtypeStruct((3,) + p.shape[1:], p.dtype) for p in chip_parts]

    def make(in_refs, io_refs, new_refs, send_sems, recv_sems, local_sems):
        x, y, c = _place()
        landing = io_refs if into is not None else new_refs
        copies = []
        for w in range(n):
            for rel, (px, py) in enumerate(_other_chips(x, y)):
                copies.append(pltpu.make_async_remote_copy(
                    src_ref=in_refs[w].at[2 * px + py, band], dst_ref=landing[w].at[rel, band],
                    send_sem=send_sems.at[3 * w + rel], recv_sem=recv_sems.at[3 * w + rel],
                    device_id=(px, py, c), device_id_type=MESH))
        return copies

    return _Exchange(chip_parts, into or [], new, 3 * n, 0, make)


class _Side:
    def __init__(self, ins, in_blocks, out_shapes, out_blocks, n_tiles, fn):
        self.ins, self.in_blocks, self.out_shapes, self.out_blocks = list(ins), in_blocks, list(out_shapes), out_blocks
        self.n_tiles, self.fn = n_tiles, fn


def _call(body, exch, *, name, grid, in_specs, out_specs, out_shape, scratch_shapes=(), semantics,
          input_output_aliases=None):
    exch = list(exch)
    in_specs, out_specs, out_shape = list(in_specs), list(out_specs), list(out_shape)
    scratch_shapes = list(scratch_shapes)
    if not exch:
        fn = pl.pallas_call(body, name=name, grid=grid, in_specs=in_specs, out_specs=out_specs, out_shape=out_shape,
                            scratch_shapes=scratch_shapes, input_output_aliases=input_output_aliases or {},
                            compiler_params=_cparams(*semantics))
        return lambda *args: (fn(*args), [])
    n_in, n_out, n_scr = len(in_specs), len(out_specs), len(scratch_shapes)
    aliases = dict(input_output_aliases or {})
    all_in, all_out_specs, all_out_shape, all_scr = list(in_specs), list(out_specs), list(out_shape), list(scratch_shapes)
    extra_args = []

    def step(idx):
        s = idx[0]
        for a in range(1, len(grid)):
            s = s * grid[a] + idx[a]
        return s

    def tile_spec(shape, where, n_tiles):
        return pl.BlockSpec(shape, lambda *idx: where(jnp.minimum(step(idx), n_tiles - 1)))

    for ex in exch:
        if isinstance(ex, _Side):
            all_in += [tile_spec(shape, where, ex.n_tiles) for shape, where in ex.in_blocks]
            extra_args += ex.ins
            all_out_specs += [tile_spec(shape, where, ex.n_tiles) for shape, where in ex.out_blocks]
            all_out_shape += ex.out_shapes
            continue
        for k, a in enumerate(ex.io):
            aliases[len(all_in) + len(ex.ins) + k] = len(all_out_specs) + k
        all_in += [ANY] * (len(ex.ins) + len(ex.io))
        extra_args += ex.ins + ex.io
        all_out_specs += [ANY] * (len(ex.io) + len(ex.new))
        all_out_shape += [jax.ShapeDtypeStruct(a.shape, a.dtype) for a in ex.io] + ex.new
        all_scr += [pltpu.SemaphoreType.DMA((ex.n_sems,)), pltpu.SemaphoreType.DMA((ex.n_sems,)),
                    pltpu.SemaphoreType.DMA((max(ex.n_local, 1),))]

    n_ins = [len(ex.ins) if isinstance(ex, _Side) else len(ex.ins) + len(ex.io) for ex in exch]
    n_outs = [len(ex.out_shapes) if isinstance(ex, _Side) else len(ex.io) + len(ex.new) for ex in exch]

    def wrapped(*refs):
        pos = n_in
        ex_in = []
        for k in n_ins:
            ex_in.append(refs[pos:pos + k])
            pos += k
        outs = refs[pos:pos + n_out]
        pos += n_out
        ex_out = []
        for k in n_outs:
            ex_out.append(refs[pos:pos + k])
            pos += k
        scr = refs[pos:pos + n_scr]
        pos += n_scr
        idx = [pl.program_id(a) for a in range(len(grid))]
        first = functools.reduce(jnp.logical_and, [i == 0 for i in idx])
        last = functools.reduce(jnp.logical_and, [i == g - 1 for i, g in zip(idx, grid)])

        def copies():
            out, at = [], pos
            for ex, ei, eo in zip(exch, ex_in, ex_out):
                if not isinstance(ex, _Side):
                    out += ex.make(ei[:len(ex.ins)], eo[:len(ex.io)], eo[len(ex.io):], *refs[at:at + 3])
                    at += 3
            return out

        @pl.when(first)
        def _():
            for cp in copies():
                cp.start()

        body(*refs[:n_in], *outs, *scr)
        for ex, ei, eo in zip(exch, ex_in, ex_out):
            if isinstance(ex, _Side):
                pl.when(step(idx) < ex.n_tiles)(functools.partial(ex.fn, ei, eo))

        @pl.when(last)
        def _():
            for cp in copies():
                cp.wait()

    fn = pl.pallas_call(wrapped, name=name, grid=grid, in_specs=all_in, out_specs=all_out_specs,
                        out_shape=all_out_shape, scratch_shapes=all_scr, input_output_aliases=aliases,
                        compiler_params=_cparams(*(["arbitrary"] * len(grid))))

    def run(*args):
        res = fn(*args, *extra_args)
        outs, pos, ex_res = res[:n_out], n_out, []
        for k in n_outs:
            ex_res.append(list(res[pos:pos + k]))
            pos += k
        return outs, ex_res

    return run


def _exchange_alone(ex, name):
    def body():
        pass

    _, res = _call(body, [ex], name=name, grid=(1,), in_specs=[], out_specs=[], out_shape=[], semantics=("arbitrary",))()
    return res[0]


def _small_gather(part):
    def make(in_refs, io_refs, new_refs, send_sems, recv_sems, local_sems):
        x, y, c = _place()
        slot = new_refs[0].at[4 * x + 2 * y + c]
        copies = [pltpu.make_async_copy(in_refs[0], slot, local_sems.at[0])]
        for d in range(1, N_DEV):
            peer = (1 - x if d & 4 else x, 1 - y if d & 2 else y, 1 - c if d & 1 else c)
            copies.append(pltpu.make_async_remote_copy(
                src_ref=in_refs[0], dst_ref=slot, send_sem=send_sems.at[d - 1], recv_sem=recv_sems.at[d - 1],
                device_id=peer, device_id_type=MESH))
        return copies

    return _Exchange([part], [], [jax.ShapeDtypeStruct((N_DEV,) + part.shape, part.dtype)], N_DEV - 1, 1, make)


def _sum_over_devices(parts):
    _, rows, lanes = parts.shape

    def body(p_ref, o_ref):
        acc = p_ref[0]
        for k in range(1, N_DEV):
            acc = acc + p_ref[k]
        o_ref[...] = acc

    return pl.pallas_call(
        body, name="small_grads_sum", grid=(1,), out_shape=jax.ShapeDtypeStruct((rows, lanes), F32),
        in_specs=[pl.BlockSpec((N_DEV, rows, lanes), lambda i: (0, 0, 0))],
        out_specs=pl.BlockSpec((rows, lanes), lambda i: (0, 0)),
        compiler_params=_cparams("arbitrary"),
    )(parts)


def _transpose_bf16(a, name, exch=(), with_copy=False):
    r, c = a.shape
    tr, tc = _tile(r, 512, 128), _tile(c, 512, 128)

    def body(a_ref, o_ref, *copy_ref):
        v = a_ref[...].astype(F32)
        o_ref[...] = v.T.astype(BF16)
        if with_copy:
            copy_ref[0][...] = v.astype(BF16)

    outs, ex = _call(
        body, exch, name=name, grid=(r // tr, c // tc),
        out_shape=[jax.ShapeDtypeStruct((c, r), BF16)] + [jax.ShapeDtypeStruct((r, c), BF16)] * with_copy,
        in_specs=[pl.BlockSpec((tr, tc), lambda i, j: (i, j))],
        out_specs=[pl.BlockSpec((tc, tr), lambda i, j: (j, i))] + [pl.BlockSpec((tr, tc), lambda i, j: (i, j))] * with_copy,
        semantics=("parallel", "parallel"),
    )(a)
    return (outs if with_copy else outs[0]), ex


def _ffn_fwd_loss(x, wgu, wd, ln_g, ln_b, target, name, exch=()):
    t, d = x.shape
    f = wd.shape[0]
    tm, tf = _tile(t, 512, 128), _tile(f, 512, 128)
    nf = f // tf

    def body(x_ref, wg_ref, wu_ref, wd_ref, lg_ref, lb_ref, t_ref,
             go_ref, uo_ref, ht_ref, dz_ref, dzb_ref, dlg_ref, dlb_ref, loss_ref, xb, acc):
        i, j = pl.program_id(0), pl.program_id(1)

        @pl.when(j == 0)
        def _():
            xb[...] = x_ref[...].astype(BF16)
            acc[...] = jnp.zeros_like(acc)

        @pl.when((i == 0) & (j == 0))
        def _():
            dlg_ref[...] = jnp.zeros_like(dlg_ref)
            dlb_ref[...] = jnp.zeros_like(dlb_ref)
            loss_ref[...] = jnp.zeros_like(loss_ref)

        g = _dot(xb[...], wg_ref[...])
        u = _dot(xb[...], wu_ref[...])
        h = g * _sigmoid(g) * u
        go_ref[...] = g.astype(BF16)
        uo_ref[...] = u.astype(BF16)
        ht_ref[...] = h.T.astype(BF16)
        acc[...] += _dot(h.astype(BF16), wd_ref[...])

        @pl.when(j == nf - 1)
        def _():
            for r in range(0, tm, EPILOGUE_ROWS):
                rows = slice(r, r + EPILOGUE_ROWS)
                xh, rstd = _ln_stats(ALPHA * x_ref[rows, :] + 0.5 * acc[rows, :])
                e = xh * lg_ref[...] + lb_ref[...] - t_ref[rows, :]
                loss_ref[...] += 0.5 * jnp.sum(jnp.sum(e * e, axis=-1, keepdims=True) * (1.0 / d), axis=0,
                                               keepdims=True)
                dy = e * (1.0 / d)
                dz = _ln_bwd(dy * lg_ref[...], xh, rstd)
                dz_ref[rows, :] = dz
                dzb_ref[rows, :] = (0.5 * dz).astype(BF16)
                dlg_ref[...] += jnp.sum(dy * xh, axis=0, keepdims=True)
                dlb_ref[...] += jnp.sum(dy, axis=0, keepdims=True)

    row = lambda i, j: (i, 0)
    fixed = lambda i, j: (0, 0)
    return _call(
        body, exch, name=name, grid=(t // tm, nf),
        out_shape=[jax.ShapeDtypeStruct((t, f), BF16), jax.ShapeDtypeStruct((t, f), BF16),
                   jax.ShapeDtypeStruct((f, t), BF16), jax.ShapeDtypeStruct((t, d), F32),
                   jax.ShapeDtypeStruct((t, d), BF16), jax.ShapeDtypeStruct((1, d), F32),
                   jax.ShapeDtypeStruct((1, d), F32), jax.ShapeDtypeStruct((8, 128), F32)],
        in_specs=[pl.BlockSpec((tm, d), row),
                  pl.BlockSpec((d, tf), lambda i, j: (0, j)),
                  pl.BlockSpec((d, tf), lambda i, j: (0, j + nf)),
                  pl.BlockSpec((tf, d), lambda i, j: (j, 0)),
                  pl.BlockSpec((1, d), fixed), pl.BlockSpec((1, d), fixed), pl.BlockSpec((tm, d), row)],
        out_specs=[pl.BlockSpec((tm, tf), lambda i, j: (i, j)), pl.BlockSpec((tm, tf), lambda i, j: (i, j)),
                   pl.BlockSpec((tf, tm), lambda i, j: (j, i)), pl.BlockSpec((tm, d), row), pl.BlockSpec((tm, d), row),
                   pl.BlockSpec((1, d), fixed), pl.BlockSpec((1, d), fixed), pl.BlockSpec((8, 128), fixed)],
        scratch_shapes=[pltpu.VMEM((tm, d), BF16), pltpu.VMEM((tm, d), F32)],
        semantics=("arbitrary", "arbitrary"),
    )(x, wgu, wgu, wd, ln_g, ln_b, target)


def _ffn_down_fwd(gu, x, wd, ln_g, ln_b, name, exch=()):
    t, d = x.shape
    f = wd.shape[0]
    tm, tf = _tile(t, 512, 128), _tile(f, 512, 128)
    nf = f // tf

    def body(g_ref, u_ref, wd_ref, x_ref, lg_ref, lb_ref, ht_ref, z_ref, xn_ref, acc):
        j = pl.program_id(1)

        @pl.when(j == 0)
        def _():
            acc[...] = jnp.zeros_like(acc)

        g = g_ref[...].astype(F32)
        h = g * _sigmoid(g) * u_ref[...].astype(F32)
        ht_ref[...] = h.T.astype(BF16)
        acc[...] += _dot(h.astype(BF16), wd_ref[...])

        @pl.when(j == nf - 1)
        def _():
            z = ALPHA * x_ref[...] + 0.5 * acc[...]
            z_ref[...] = z
            xn_ref[...] = _ln(z, lg_ref[...], lb_ref[...])

    row = lambda i, j: (i, 0)
    fixed = lambda i, j: (0, 0)
    return _call(
        body, exch, name=name, grid=(t // tm, nf),
        out_shape=[jax.ShapeDtypeStruct((f, t), BF16), jax.ShapeDtypeStruct((t, d), F32),
                   jax.ShapeDtypeStruct((t, d), F32)],
        in_specs=[pl.BlockSpec((tm, tf), lambda i, j: (i, j)), pl.BlockSpec((tm, tf), lambda i, j: (i, j + nf)),
                  pl.BlockSpec((tf, d), lambda i, j: (j, 0)), pl.BlockSpec((tm, d), row),
                  pl.BlockSpec((1, d), fixed), pl.BlockSpec((1, d), fixed)],
        out_specs=[pl.BlockSpec((tf, tm), lambda i, j: (j, i)), pl.BlockSpec((tm, d), row), pl.BlockSpec((tm, d), row)],
        scratch_shapes=[pltpu.VMEM((tm, d), F32)],
        semantics=("parallel", "arbitrary"),
    )(gu, gu, wd, x, ln_g, ln_b)


def _ffn_act_grads(dh, g_ref, u_ref):
    gg = g_ref[...].astype(F32)
    uu = u_ref[...].astype(F32)
    s = _sigmoid(gg)
    du = (dh * (gg * s)).astype(BF16)
    dg = (dh * uu * (s * (1.0 + gg * (1.0 - s)))).astype(BF16)
    return dg, du


def _ffn_bwd(dz, do, g, u, wgu, wd, name, exch=()):
    t, d = dz.shape
    f = wd.shape[0]
    tm, tf = _tile(t, 512, 128), _tile(f, 512, 128)
    nf = f // tf

    def body(dz_ref, do_ref, g_ref, u_ref, wg_ref, wu_ref, wd_ref, dg_ref, du_ref, dx_ref, acc):
        j = pl.program_id(1)

        @pl.when(j == 0)
        def _():
            acc[...] = jnp.zeros_like(acc)

        dg, du = _ffn_act_grads(_dot_nt(do_ref[...], wd_ref[...]), g_ref, u_ref)
        dg_ref[...] = dg
        du_ref[...] = du
        acc[...] += _dot_nt(dg, wg_ref[...]) + _dot_nt(du, wu_ref[...])

        @pl.when(j == nf - 1)
        def _():
            dx_ref[...] = ALPHA * dz_ref[...] + acc[...]

    row = lambda i, j: (i, 0)
    tile = lambda i, j: (i, j)
    return _call(
        body, exch, name=name, grid=(t // tm, nf),
        out_shape=[jax.ShapeDtypeStruct((t, f), BF16), jax.ShapeDtypeStruct((t, f), BF16),
                   jax.ShapeDtypeStruct((t, d), F32)],
        in_specs=[pl.BlockSpec((tm, d), row), pl.BlockSpec((tm, d), row),
                  pl.BlockSpec((tm, tf), tile), pl.BlockSpec((tm, tf), tile),
                  pl.BlockSpec((d, tf), lambda i, j: (0, j)),
                  pl.BlockSpec((d, tf), lambda i, j: (0, j + nf)),
                  pl.BlockSpec((tf, d), lambda i, j: (j, 0))],
        out_specs=[pl.BlockSpec((tm, tf), tile), pl.BlockSpec((tm, tf), tile), pl.BlockSpec((tm, d), row)],
        scratch_shapes=[pltpu.VMEM((tm, d), F32)],
        semantics=("parallel", "arbitrary"),
    )(dz, do, g, u, wgu, wgu, wd)


def _ffn_bwd_act(do, gu, wd, name, exch=()):
    t, d = do.shape
    f = wd.shape[0]
    tm, tf = _tile(t, 1024, 128), _tile(f, 512, 128)
    nf = f // tf

    def body(do_ref, g_ref, u_ref, wd_ref, dg_ref, du_ref):
        dg, du = _ffn_act_grads(_dot_nt(do_ref[...], wd_ref[...]), g_ref, u_ref)
        dg_ref[...] = dg
        du_ref[...] = du

    tile = lambda i, j: (i, j)
    return _call(
        body, exch, name=name, grid=(t // tm, f // tf),
        out_shape=[jax.ShapeDtypeStruct((t, f), BF16), jax.ShapeDtypeStruct((t, f), BF16)],
        in_specs=[pl.BlockSpec((tm, d), lambda i, j: (i, 0)), pl.BlockSpec((tm, tf), tile),
                  pl.BlockSpec((tm, tf), lambda i, j: (i, j + nf)), pl.BlockSpec((tf, d), lambda i, j: (j, 0))],
        out_specs=[pl.BlockSpec((tm, tf), tile), pl.BlockSpec((tm, tf), tile)],
        semantics=("parallel", "parallel"),
    )(do, gu, gu, wd)


def _ffn_bwd_dx(dz, dg, du, wgu, name, exch=()):
    t, d = dz.shape
    f = dg.shape[1]
    tm, tn = _tile(t, 512, 128), _tile(d, 256, 128)

    def body(dz_ref, dg_ref, du_ref, wg_ref, wu_ref, dx_ref):
        dx_ref[...] = ALPHA * dz_ref[...] + _dot_nt(dg_ref[...], wg_ref[...]) + _dot_nt(du_ref[...], wu_ref[...])

    row = lambda i, n: (i, 0)
    tile = lambda i, n: (i, n)
    return _call(
        body, exch, name=name, grid=(t // tm, d // tn), out_shape=[jax.ShapeDtypeStruct((t, d), F32)],
        in_specs=[pl.BlockSpec((tm, tn), tile), pl.BlockSpec((tm, f), row), pl.BlockSpec((tm, f), row),
                  pl.BlockSpec((tn, f), lambda i, n: (n, 0)), pl.BlockSpec((tn, f), lambda i, n: (n, 1))],
        out_specs=[pl.BlockSpec((tm, tn), tile)],
        semantics=("parallel", "arbitrary"),
    )(dz, dg, du, wgu, wgu)


def _weight_grad(at, b, tn, tmm, name, blocks=None, block_offset=0, into=None, exch=()):
    m, t = at.shape
    nn = b.shape[1]
    tmm = _tile(m, tmm, 16)
    assert nn % tn == 0

    def body(*refs):
        at_ref, b_ref, o_ref = refs[0], refs[1], refs[-1]
        r = _dot(at_ref[...], b_ref[...]).astype(BF16)
        if blocks is None:
            o_ref[...] = r
        else:
            o_ref[0] = r

    in_specs = [pl.BlockSpec((tmm, t), lambda n, i: (i, 0)), pl.BlockSpec((t, tn), lambda n, i: (0, n))]
    args = [at, b]
    aliases = {}
    if into is not None:
        in_specs.append(ANY)
        args.append(into)
        aliases = {2: 0}
    if blocks is None:
        out_shape = jax.ShapeDtypeStruct((m, nn), BF16)
        out_spec = pl.BlockSpec((tmm, tn), lambda n, i: (i, n))
    else:
        out_shape = jax.ShapeDtypeStruct((blocks, m, tn), BF16)
        out_spec = pl.BlockSpec((1, tmm, tn), lambda n, i: (n + block_offset, i, 0))
    (out,), ex = _call(
        body, exch, name=name, grid=(nn // tn, m // tmm), out_shape=[out_shape],
        in_specs=in_specs, out_specs=[out_spec], input_output_aliases=aliases,
        semantics=("parallel", "parallel"),
    )(*args)
    return out, ex


def _mix_in_proj(x, w_in, name, exch=()):
    t, d = x.shape
    n_out = w_in.shape[1]
    tm, cb = _tile(t, 512, 128), _tile(n_out, 1024, 128)

    def body(x_ref, w_ref, o_ref, xb):
        @pl.when(pl.program_id(1) == 0)
        def _():
            xb[...] = x_ref[...].astype(BF16)

        o_ref[...] = _dot(xb[...], w_ref[...])

    (out,), ex = _call(
        body, exch, name=name, grid=(t // tm, n_out // cb), out_shape=[jax.ShapeDtypeStruct((t, n_out), F32)],
        in_specs=[pl.BlockSpec((tm, d), lambda i, k: (i, 0)), pl.BlockSpec((d, cb), lambda i, k: (0, k))],
        out_specs=[pl.BlockSpec((tm, cb), lambda i, k: (i, k))],
        scratch_shapes=[pltpu.VMEM((tm, d), BF16)],
        semantics=("parallel", "arbitrary"),
    )(x, w_in)
    return out, ex


def _mix_in_bwd(dproj, w_in, dz, name, exch=()):
    t, d = dz.shape
    kk = w_in.shape[1]
    tm, tn = _tile(t, 512, 128), _tile(d, 512, 128)

    def body(dp_ref, w_ref, dz_ref, dx_ref):
        dx_ref[...] = ALPHA * dz_ref[...] + _dot_nt(dp_ref[...], w_ref[...])

    (out,), ex = _call(
        body, exch, name=name, grid=(t // tm, d // tn), out_shape=[jax.ShapeDtypeStruct((t, d), F32)],
        in_specs=[pl.BlockSpec((tm, kk), lambda i, n: (i, 0)), pl.BlockSpec((tn, kk), lambda i, n: (n, 0)),
                  pl.BlockSpec((tm, tn), lambda i, n: (i, n))],
        out_specs=[pl.BlockSpec((tm, tn), lambda i, n: (i, n))],
        semantics=("parallel", "arbitrary"),
    )(dproj, w_in, dz)
    return out, ex


def _mix_out_fwd(y, w_out, x, ln_g, ln_b, name, exch=()):
    t, d = x.shape
    kk = y.shape[1]
    tm = _tile(t, 256, 128)

    def body(y_ref, w_ref, x_ref, g_ref, b_ref, z_ref, xn_ref, xnt_ref):
        z = ALPHA * x_ref[...] + _dot(y_ref[...], w_ref[...])
        z_ref[...] = z
        xn = _ln(z, g_ref[...], b_ref[...])
        xn_ref[...] = xn
        xnt_ref[...] = xn.T.astype(BF16)

    row = lambda i: (i, 0)
    fixed = lambda i: (0, 0)
    return _call(
        body, exch, name=name, grid=(t // tm,),
        out_shape=[jax.ShapeDtypeStruct((t, d), F32), jax.ShapeDtypeStruct((t, d), F32),
                   jax.ShapeDtypeStruct((d, t), BF16)],
        in_specs=[pl.BlockSpec((tm, kk), row), pl.BlockSpec((kk, d), fixed), pl.BlockSpec((tm, d), row),
                  pl.BlockSpec((1, d), fixed), pl.BlockSpec((1, d), fixed)],
        out_specs=[pl.BlockSpec((tm, d), row), pl.BlockSpec((tm, d), row), pl.BlockSpec((d, tm), lambda i: (0, i))],
        semantics=("parallel",),
    )(y, w_out, x, ln_g, ln_b)


def _mix_out_bwd(dzb, w_out, name, exch=()):
    t, d = dzb.shape
    kk = w_out.shape[0]
    tm = _tile(t, 256, 128)

    def body(dz_ref, w_ref, dy_ref):
        dy_ref[...] = _dot_nt(dz_ref[...], w_ref[...])

    (out,), ex = _call(
        body, exch, name=name, grid=(t // tm,), out_shape=[jax.ShapeDtypeStruct((t, kk), F32)],
        in_specs=[pl.BlockSpec((tm, d), lambda i: (i, 0)), pl.BlockSpec((kk, d), lambda i: (0, 0))],
        out_specs=[pl.BlockSpec((tm, kk), lambda i: (i, 0))],
        semantics=("parallel",),
    )(dzb, w_out)
    return out, ex


def _loss_ln_bwd(z, target, ln_g, ln_b, bf16_scale, name):
    t, d = z.shape
    tm = _tile(t, 512, 8)

    def body(z_ref, t_ref, g_ref, b_ref, dz_ref, dzb_ref, dg_ref, db_ref, loss_ref):
        @pl.when(pl.program_id(0) == 0)
        def _():
            dg_ref[...] = jnp.zeros_like(dg_ref)
            db_ref[...] = jnp.zeros_like(db_ref)
            loss_ref[...] = jnp.zeros_like(loss_ref)

        xh, rstd = _ln_stats(z_ref[...])
        e = xh * g_ref[...] + b_ref[...] - t_ref[...]
        loss_ref[...] += 0.5 * jnp.sum(jnp.sum(e * e, axis=-1, keepdims=True) * (1.0 / d), axis=0, keepdims=True)
        dy = e * (1.0 / d)
        dz = _ln_bwd(dy * g_ref[...], xh, rstd)
        dz_ref[...] = dz
        dzb_ref[...] = (bf16_scale * dz).astype(BF16)
        dg_ref[...] += jnp.sum(dy * xh, axis=0, keepdims=True)
        db_ref[...] += jnp.sum(dy, axis=0, keepdims=True)

    row = lambda i: (i, 0)
    fixed = lambda i: (0, 0)
    return pl.pallas_call(
        body, name=name, grid=(t // tm,),
        out_shape=[jax.ShapeDtypeStruct((t, d), F32), jax.ShapeDtypeStruct((t, d), BF16),
                   jax.ShapeDtypeStruct((1, d), F32), jax.ShapeDtypeStruct((1, d), F32),
                   jax.ShapeDtypeStruct((8, 128), F32)],
        in_specs=[pl.BlockSpec((tm, d), row), pl.BlockSpec((tm, d), row), pl.BlockSpec((1, d), fixed),
                  pl.BlockSpec((1, d), fixed)],
        out_specs=[pl.BlockSpec((tm, d), row), pl.BlockSpec((tm, d), row), pl.BlockSpec((1, d), fixed),
                   pl.BlockSpec((1, d), fixed), pl.BlockSpec((8, 128), fixed)],
        compiler_params=_cparams("arbitrary"),
    )(z, target, ln_g, ln_b)


def _ln_bwd_call(z, dy, ln_g, bf16_scale, name, exch=()):
    t, d = z.shape
    tm = _tile(t, 512, 8)

    def body(z_ref, dy_ref, g_ref, dz_ref, dzb_ref, dg_ref, db_ref):
        @pl.when(pl.program_id(0) == 0)
        def _():
            dg_ref[...] = jnp.zeros_like(dg_ref)
            db_ref[...] = jnp.zeros_like(db_ref)

        xh, rstd = _ln_stats(z_ref[...])
        dy = dy_ref[...]
        dz = _ln_bwd(dy * g_ref[...], xh, rstd)
        dz_ref[...] = dz
        dzb_ref[...] = (bf16_scale * dz).astype(BF16)
        dg_ref[...] += jnp.sum(dy * xh, axis=0, keepdims=True)
        db_ref[...] += jnp.sum(dy, axis=0, keepdims=True)

    row = lambda i: (i, 0)
    fixed = lambda i: (0, 0)
    return _call(
        body, exch, name=name, grid=(t // tm,),
        out_shape=[jax.ShapeDtypeStruct((t, d), F32), jax.ShapeDtypeStruct((t, d), BF16),
                   jax.ShapeDtypeStruct((1, d), F32), jax.ShapeDtypeStruct((1, d), F32)],
        in_specs=[pl.BlockSpec((tm, d), row), pl.BlockSpec((tm, d), row), pl.BlockSpec((1, d), fixed)],
        out_specs=[pl.BlockSpec((tm, d), row), pl.BlockSpec((tm, d), row), pl.BlockSpec((1, d), fixed),
                   pl.BlockSpec((1, d), fixed)],
        semantics=("arbitrary",),
    )(z, dy, ln_g)


CONV_ROWS = 32
SUBLANES = 8


def _fill_shifted(ext, shifted):
    rows = ext.shape[0] - SUBLANES
    for s in range(1, SUBLANES):
        for r in range(0, rows, CONV_ROWS):
            n = min(CONV_ROWS, rows - r)
            shifted[s - 1, r:r + n, :] = ext[r + s:r + s + n, :]


def _window(ext, shifted, lo, n):
    s = lo % SUBLANES
    return ext[lo:lo + n, :] if s == 0 else shifted[s - 1, lo - s:lo - s + n, :]


def _mixer_fwd(proj, conv_w, conv_b, cln_g, cln_b, sln_g, sln_b, sg_wm, sg_bb, name, exch=()):
    t = proj.shape[0]
    tm = _tile(t, 256, CHUNK)
    hb = tm // HALO
    nc = tm // CHUNK
    ch = CONV_CH

    def body(av_ref, ag_ref, bu_ref, bv_ref, hv_ref, hg_ref, cw_ref, cb_ref, lg_ref, lb_ref, sg_ref, sb_ref,
             w_ref, bb_ref, y_ref, yt_ref, c_ref, ext, ext_s):
        i = pl.program_id(0)
        halo = hv_ref[...] * _sigmoid(hg_ref[...])
        ext[0:HALO, :] = jnp.where(i > 0, halo, 0.0)
        ext[HALO:HALO + tm, :] = av_ref[...] * _sigmoid(ag_ref[...])
        _fill_shifted(ext, ext_s)
        for r in range(0, tm, CONV_ROWS):
            acc = jnp.zeros((CONV_ROWS, ch), F32) + cb_ref[...]
            for k in range(CONV_TAPS):
                lo = r + k + HALO - (CONV_TAPS - 1)
                acc = acc + cw_ref[k:k + 1, :] * _window(ext, ext_s, lo, CONV_ROWS)
            c_ref[r:r + CONV_ROWS, :] = acc
        a = _ln(c_ref[...], lg_ref[...], lb_ref[...])
        ya = a * _sigmoid(a)
        y_ref[:, 0:ch] = ya.astype(BF16)
        yt_ref[0:ch, :] = ya.T.astype(BF16)
        for h in range(HEADS):
            sl = slice(h * HEAD_DIM, (h + 1) * HEAD_DIM)
            u, _ = _gelu_and_grad(bu_ref[:, sl])
            v, _ = _gelu_and_grad(bv_ref[:, sl])
            vn = _ln(v, sg_ref[h:h + 1, :], sb_ref[h:h + 1, :])
            vn3 = vn.astype(BF16).reshape(nc, CHUNK, HEAD_DIM)
            wb = jnp.broadcast_to(w_ref[h][None], (nc, CHUNK, CHUNK))
            mixed = jnp.einsum("cts,csd->ctd", wb, vn3, preferred_element_type=F32) + bb_ref[h][None]
            yb = u * mixed.reshape(tm, HEAD_DIM)
            y_ref[:, ch + h * HEAD_DIM:ch + (h + 1) * HEAD_DIM] = yb.astype(BF16)
            yt_ref[ch + h * HEAD_DIM:ch + (h + 1) * HEAD_DIM, :] = yb.T.astype(BF16)

    col = lambda cidx: (lambda i: (i, cidx))
    prev = lambda cidx: (lambda i: (jnp.maximum(i * hb - 1, 0), cidx))
    fix2 = lambda i: (0, 0)
    fix3 = lambda i: (0, 0, 0)
    return _call(
        body, exch, name=name, grid=(t // tm,),
        out_shape=[jax.ShapeDtypeStruct((t, 2 * ch), BF16), jax.ShapeDtypeStruct((2 * ch, t), BF16),
                   jax.ShapeDtypeStruct((t, ch), F32)],
        in_specs=[pl.BlockSpec((tm, ch), col(0)), pl.BlockSpec((tm, ch), col(1)), pl.BlockSpec((tm, ch), col(2)),
                  pl.BlockSpec((tm, ch), col(3)), pl.BlockSpec((HALO, ch), prev(0)), pl.BlockSpec((HALO, ch), prev(1)),
                  pl.BlockSpec((CONV_TAPS, ch), fix2), pl.BlockSpec((1, ch), fix2), pl.BlockSpec((1, ch), fix2),
                  pl.BlockSpec((1, ch), fix2), pl.BlockSpec((HEADS, HEAD_DIM), fix2), pl.BlockSpec((HEADS, HEAD_DIM), fix2),
                  pl.BlockSpec((HEADS, CHUNK, CHUNK), fix3), pl.BlockSpec((HEADS, CHUNK, HEAD_DIM), fix3)],
        out_specs=[pl.BlockSpec((tm, 2 * ch), lambda i: (i, 0)), pl.BlockSpec((2 * ch, tm), lambda i: (0, i)),
                   pl.BlockSpec((tm, ch), lambda i: (i, 0))],
        scratch_shapes=[pltpu.VMEM((HALO + tm, ch), F32), pltpu.VMEM((SUBLANES - 1, HALO + tm, ch), F32)],
        semantics=("parallel",),
    )(proj, proj, proj, proj, proj, proj, conv_w, conv_b, cln_g, cln_b, sln_g, sln_b, sg_wm, sg_bb)


def _mixer_bwd(proj, conv_c, dy, conv_w, cln_g, cln_b, sln_g, sln_b, sg_wm, sg_wmt, sg_bb, name, exch=()):
    t = proj.shape[0]
    tm = _tile(t, 256, CHUNK)
    hb = tm // HALO
    nc = tm // CHUNK
    nt = t // tm
    ch = CONV_CH
    last_halo = t // HALO - 1

    def body(av_ref, ag_ref, bu_ref, bv_ref, hv_ref, hg_ref, c_ref, cn_ref, dya_ref, dyan_ref, dyb_ref,
             cw_ref, lg_ref, lb_ref, sg_ref, sb_ref, w_ref, wt_ref, bb_ref,
             dp_ref, dcw_ref, dcb_ref, dlg_ref, dlb_ref, dsg_ref, dsb_ref, dw_ref, dbs_ref,
             ext_h, ext_dc, ext_hs, ext_dcs, acc_cw):
        i = pl.program_id(0)

        @pl.when(i == 0)
        def _():
            acc_cw[...] = jnp.zeros_like(acc_cw)
            for ref in (dcb_ref, dlg_ref, dlb_ref, dsg_ref, dsb_ref, dw_ref, dbs_ref):
                ref[...] = jnp.zeros_like(ref)

        lg = lg_ref[...]
        lb = lb_ref[...]

        def conv_ln_bwd(c, dya):
            xh, rstd = _ln_stats(c)
            a = xh * lg + lb
            da = dya * _silu_grad(a)
            return _ln_bwd(da * lg, xh, rstd), da, xh

        fold = lambda v: jnp.sum(v.reshape(CONV_ROWS // SUBLANES, SUBLANES, ch), axis=0)
        s_lg = s_lb = s_cb = jnp.zeros((SUBLANES, ch), F32)
        for r in range(0, tm, CONV_ROWS):
            dc, da, xh = conv_ln_bwd(c_ref[r:r + CONV_ROWS, :], dya_ref[r:r + CONV_ROWS, :])
            ext_dc[r:r + CONV_ROWS, :] = dc
            s_lg, s_lb, s_cb = s_lg + fold(da * xh), s_lb + fold(da), s_cb + fold(dc)
        dlg_ref[...] += jnp.sum(s_lg, axis=0, keepdims=True)
        dlb_ref[...] += jnp.sum(s_lb, axis=0, keepdims=True)
        dcb_ref[...] += jnp.sum(s_cb, axis=0, keepdims=True)
        dcn, _, _ = conv_ln_bwd(cn_ref[...], dyan_ref[...])
        ext_dc[tm:tm + HALO, :] = jnp.where(i < nt - 1, dcn, 0.0)
        halo = hv_ref[...] * _sigmoid(hg_ref[...])
        ext_h[0:HALO, :] = jnp.where(i > 0, halo, 0.0)
        ext_h[HALO:HALO + tm, :] = av_ref[...] * _sigmoid(ag_ref[...])
        _fill_shifted(ext_h, ext_hs)
        _fill_shifted(ext_dc, ext_dcs)
        for r in range(0, tm, CONV_ROWS):
            dcr = ext_dc[r:r + CONV_ROWS, :]
            acc = jnp.zeros((CONV_ROWS, ch), F32)
            for k in range(CONV_TAPS):
                lo = r + k + HALO - (CONV_TAPS - 1)
                prod = dcr * _window(ext_h, ext_hs, lo, CONV_ROWS)
                acc_cw[k] += jnp.sum(prod.reshape(CONV_ROWS // 8, 8, ch), axis=0)
                hi = r + (CONV_TAPS - 1) - k
                acc = acc + cw_ref[k:k + 1, :] * _window(ext_dc, ext_dcs, hi, CONV_ROWS)
            sg_r = _sigmoid(ag_ref[r:r + CONV_ROWS, :])
            av_r = av_ref[r:r + CONV_ROWS, :]
            dp_ref[r:r + CONV_ROWS, 0:ch] = (acc * sg_r).astype(BF16)
            dp_ref[r:r + CONV_ROWS, ch:2 * ch] = (acc * av_r * sg_r * (1.0 - sg_r)).astype(BF16)

        @pl.when(i == nt - 1)
        def _():
            dcw_ref[...] = jnp.sum(acc_cw[...], axis=1)

        tril = (lax.broadcasted_iota(jnp.int32, (CHUNK, CHUNK), 0)
                >= lax.broadcasted_iota(jnp.int32, (CHUNK, CHUNK), 1)).astype(F32)
        for h in range(HEADS):
            sl = slice(h * HEAD_DIM, (h + 1) * HEAD_DIM)
            u, du_dx = _gelu_and_grad(bu_ref[:, sl])
            v, dv_dx = _gelu_and_grad(bv_ref[:, sl])
            xhv, rstdv = _ln_stats(v)
            gh = sg_ref[h:h + 1, :]
            vn3 = (xhv * gh + sb_ref[h:h + 1, :]).astype(BF16).reshape(nc, CHUNK, HEAD_DIM)
            wb = jnp.broadcast_to(w_ref[h][None], (nc, CHUNK, CHUNK))
            mixed = jnp.einsum("cts,csd->ctd", wb, vn3, preferred_element_type=F32) + bb_ref[h][None]
            dyb = dyb_ref[:, sl]
            d_u = dyb * mixed.reshape(tm, HEAD_DIM)
            dm = dyb * u
            dm3 = dm.reshape(nc, CHUNK, HEAD_DIM)
            dbs_ref[h:h + 1, :] += jnp.sum(jnp.sum(dm3, axis=0).T, axis=0, keepdims=True)
            dm3b = dm3.astype(BF16)
            dw_h = jnp.sum(jnp.einsum("ctd,csd->cts", dm3b, vn3, preferred_element_type=F32), axis=0)
            dw_ref[h] += dw_h * tril
            wtb = jnp.broadcast_to(wt_ref[h][None], (nc, CHUNK, CHUNK))
            d_vn = jnp.einsum("cst,ctd->csd", wtb, dm3b, preferred_element_type=F32).reshape(tm, HEAD_DIM)
            dsg_ref[h:h + 1, :] += jnp.sum(d_vn * xhv, axis=0, keepdims=True)
            dsb_ref[h:h + 1, :] += jnp.sum(d_vn, axis=0, keepdims=True)
            dv = _ln_bwd(d_vn * gh, xhv, rstdv)
            dp_ref[:, 2 * ch + h * HEAD_DIM:2 * ch + (h + 1) * HEAD_DIM] = (d_u * du_dx).astype(BF16)
            dp_ref[:, 3 * ch + h * HEAD_DIM:3 * ch + (h + 1) * HEAD_DIM] = (dv * dv_dx).astype(BF16)

    col = lambda cidx: (lambda i: (i, cidx))
    prev = lambda cidx: (lambda i: (jnp.maximum(i * hb - 1, 0), cidx))
    nxt = lambda i: (jnp.minimum((i + 1) * hb, last_halo), 0)
    fix2 = lambda i: (0, 0)
    fix3 = lambda i: (0, 0, 0)
    out_shape = [jax.ShapeDtypeStruct((t, 4 * ch), BF16), jax.ShapeDtypeStruct((CONV_TAPS, ch), F32),
                 jax.ShapeDtypeStruct((1, ch), F32), jax.ShapeDtypeStruct((1, ch), F32), jax.ShapeDtypeStruct((1, ch), F32),
                 jax.ShapeDtypeStruct((HEADS, HEAD_DIM), F32), jax.ShapeDtypeStruct((HEADS, HEAD_DIM), F32),
                 jax.ShapeDtypeStruct((HEADS, CHUNK, CHUNK), F32), jax.ShapeDtypeStruct((HEADS, CHUNK), F32)]
    out_specs = [pl.BlockSpec((tm, 4 * ch), lambda i: (i, 0)), pl.BlockSpec((CONV_TAPS, ch), fix2),
                 pl.BlockSpec((1, ch), fix2), pl.BlockSpec((1, ch), fix2), pl.BlockSpec((1, ch), fix2),
                 pl.BlockSpec((HEADS, HEAD_DIM), fix2), pl.BlockSpec((HEADS, HEAD_DIM), fix2),
                 pl.BlockSpec((HEADS, CHUNK, CHUNK), fix3), pl.BlockSpec((HEADS, CHUNK), fix2)]
    in_specs = [pl.BlockSpec((tm, ch), col(0)), pl.BlockSpec((tm, ch), col(1)), pl.BlockSpec((tm, ch), col(2)),
                pl.BlockSpec((tm, ch), col(3)), pl.BlockSpec((HALO, ch), prev(0)), pl.BlockSpec((HALO, ch), prev(1)),
                pl.BlockSpec((tm, ch), col(0)), pl.BlockSpec((HALO, ch), nxt),
                pl.BlockSpec((tm, ch), col(0)), pl.BlockSpec((HALO, ch), nxt), pl.BlockSpec((tm, ch), col(1)),
                pl.BlockSpec((CONV_TAPS, ch), fix2), pl.BlockSpec((1, ch), fix2), pl.BlockSpec((1, ch), fix2),
                pl.BlockSpec((HEADS, HEAD_DIM), fix2), pl.BlockSpec((HEADS, HEAD_DIM), fix2),
                pl.BlockSpec((HEADS, CHUNK, CHUNK), fix3), pl.BlockSpec((HEADS, CHUNK, CHUNK), fix3),
                pl.BlockSpec((HEADS, CHUNK, HEAD_DIM), fix3)]
    return _call(
        body, exch, name=name, grid=(nt,), out_shape=out_shape, in_specs=in_specs, out_specs=out_specs,
        scratch_shapes=[pltpu.VMEM((HALO + tm, ch), F32), pltpu.VMEM((tm + HALO, ch), F32),
                        pltpu.VMEM((SUBLANES - 1, HALO + tm, ch), F32), pltpu.VMEM((SUBLANES - 1, tm + HALO, ch), F32),
                        pltpu.VMEM((CONV_TAPS, 8, ch), F32)],
        semantics=("arbitrary",),
    )(proj, proj, proj, proj, proj, proj, conv_c, conv_c, dy, dy, dy,
      conv_w, cln_g, cln_b, sln_g, sln_b, sg_wm, sg_wmt, sg_bb)


def _pair_sum(parts, from_sibling, core_chip, name):
    _, r, cc = parts.shape
    tr = _tile(r, max(16, (1 << 20) // (2 * cc)), 16)

    def body(cc_ref, p_ref, s_ref, o_ref, own_ref):
        q = (p_ref[...].astype(F32) + s_ref[...].astype(F32)).astype(BF16)
        o_ref[...] = q

        @pl.when(pl.program_id(1) == cc_ref[1])
        def _():
            own_ref[...] = q[0]

    grid_spec = pltpu.PrefetchScalarGridSpec(
        num_scalar_prefetch=1, grid=(r // tr, 4),
        in_specs=[pl.BlockSpec((1, tr, cc), lambda i, j, cc_ref: (2 * j + cc_ref[0], i, 0)),
                  pl.BlockSpec((1, tr, cc), lambda i, j, cc_ref: (j, i, 0))],
        out_specs=[pl.BlockSpec((1, tr, cc), lambda i, j, cc_ref: (j, i, 0)),
                   pl.BlockSpec((tr, cc), lambda i, j, cc_ref: (i, 0))])
    return pl.pallas_call(
        body, name=name, grid_spec=grid_spec,
        out_shape=[jax.ShapeDtypeStruct((4, r, cc), BF16), jax.ShapeDtypeStruct((r, cc), BF16)],
        compiler_params=_cparams("parallel", "arbitrary"),
    )(core_chip, parts, from_sibling)


def _adamw_math(w, g, m, v):
    m = ADAM_B1 * m + (1.0 - ADAM_B1) * g
    v = ADAM_B2 * v + (1.0 - ADAM_B2) * (g * g)
    m_hat = m / (1.0 - ADAM_B1 ** ADAM_STEP)
    v_hat = v / (1.0 - ADAM_B2 ** ADAM_STEP)
    delta = -ADAM_LR * (m_hat / (jnp.sqrt(v_hat) + ADAM_EPS) + ADAM_WD * w)
    return delta, m, v


def _adamw_tile(in_refs, out_refs):
    w_ref, m_ref, v_ref, q_ref, o_ref = in_refs
    g = q_ref[...].astype(F32)
    for k in range(3):
        g = g + o_ref[k].astype(F32)
    d, mm, vv = _adamw_math(w_ref[...], g, m_ref[...], v_ref[...])
    for ref, val in zip(out_refs, (g, d, mm, vv)):
        ref[...] = val


def _adamw_side(w, m, v, chip_part, from_chips, max_tiles):
    r, cc = w.shape
    n = max(k for k in range(1, max_tiles + 1) if r % k == 0 and (r // k) % 16 == 0)
    tr = r // n
    row = ((tr, cc), lambda s: (s, 0))
    return _Side([w, m, v, chip_part, from_chips], [row, row, row, row, ((3, tr, cc), lambda s: (0, s, 0))],
                 [jax.ShapeDtypeStruct((r, cc), F32)] * 4, [row] * 4, n, _adamw_tile)


def _adamw_sharded(w, m, v, chip_part, from_chips, name):
    r, cc = w.shape
    tr = _tile(r, max(16, (1 << 19) // (4 * cc) * 2), 16)

    def body(*refs):
        _adamw_tile(refs[:5], refs[5:])

    row = pl.BlockSpec((tr, cc), lambda i: (i, 0))
    return pl.pallas_call(
        body, name=name, grid=(r // tr,), out_shape=[jax.ShapeDtypeStruct((r, cc), F32)] * 4,
        in_specs=[row, row, row, row, pl.BlockSpec((3, tr, cc), lambda i: (0, i, 0))], out_specs=[row] * 4,
        compiler_params=_cparams("parallel"),
    )(w, m, v, chip_part, from_chips)


def _adamw_small(w, g, m, v, name):
    r, cc = w.shape

    def body(w_ref, g_ref, m_ref, v_ref, d_out, m_out, v_out):
        d, mm, vv = _adamw_math(w_ref[...], g_ref[...], m_ref[...], v_ref[...])
        d_out[...] = d
        m_out[...] = mm
        v_out[...] = vv

    full = pl.BlockSpec((r, cc), lambda i: (0, 0))
    return pl.pallas_call(
        body, name=name, grid=(1,), out_shape=[jax.ShapeDtypeStruct((r, cc), F32)] * 3,
        in_specs=[full] * 4, out_specs=[full] * 3, compiler_params=_cparams("arbitrary"),
    )(w, g, m, v)


SMALL = ("ln1_g", "ln1_b", "conv_b", "conv_ln_g", "conv_ln_b", "sg_ln_g", "sg_ln_b", "sg_w", "sg_b",
         "ln2_g", "ln2_b", "ln3_g", "ln3_b")
ORDER = ("ffn1_w_gate_up", "ffn1_w_down", "ln1_g", "ln1_b", "mix_w_in", "conv_w", "conv_b", "conv_ln_g", "conv_ln_b",
         "sg_ln_g", "sg_ln_b", "sg_w", "sg_b", "mix_w_out", "ln2_g", "ln2_b", "ffn2_w_gate_up", "ffn2_w_down",
         "ln3_g", "ln3_b")


def _rows128(a):
    return a.reshape(-1, 128)


def kernel(x, ffn1_w_gate_up, ffn1_w_down, ln1_g, ln1_b, mix_w_in, conv_w, conv_b, conv_ln_g, conv_ln_b, sg_ln_g, sg_ln_b, sg_w, sg_b, mix_w_out, ln2_g, ln2_b, ffn2_w_gate_up, ffn2_w_down, ln3_g, ln3_b, loss_target, m_ffn1_w_gate_up, m_ffn1_w_down, m_ln1_g, m_ln1_b, m_mix_w_in, m_conv_w, m_conv_b, m_conv_ln_g, m_conv_ln_b, m_sg_ln_g, m_sg_ln_b, m_sg_w, m_sg_b, m_mix_w_out, m_ln2_g, m_ln2_b, m_ffn2_w_gate_up, m_ffn2_w_down, m_ln3_g, m_ln3_b, v_ffn1_w_gate_up, v_ffn1_w_down, v_ln1_g, v_ln1_b, v_mix_w_in, v_conv_w, v_conv_b, v_conv_ln_g, v_conv_ln_b, v_sg_ln_g, v_sg_ln_b, v_sg_w, v_sg_b, v_mix_w_out, v_ln2_g, v_ln2_b, v_ffn2_w_gate_up, v_ffn2_w_down, v_ln3_g, v_ln3_b):
    args = dict(locals())
    w = {n: args[n][0] for n in ORDER}
    mom = {n: args["m_" + n][0] for n in ORDER}
    var = {n: args["v_" + n][0] for n in ORDER}
    x0 = x[0]
    target = loss_target[0]
    t, d = x0.shape
    my_x, my_y, my_c = lax.axis_index("x"), lax.axis_index("y"), lax.axis_index("c")
    my_chip = (2 * my_x + my_y).astype(jnp.int32).reshape(1)
    my_core = my_c.astype(jnp.int32).reshape(1)
    me = 4 * my_x + 2 * my_y + my_c

    big = ("ffn1_w_gate_up", "ffn1_w_down", "mix_w_in", "mix_w_out", "ffn2_w_gate_up", "ffn2_w_down")
    sh = {n: w[n].astype(BF16) for n in big}
    f2s = sh["ffn2_w_gate_up"].shape[1]
    order = jnp.stack([4 * p[0] + 2 * p[1] + p[2] for p in _visit_order(my_x, my_y, my_c)]).astype(jnp.int32)
    gu1, x0t, (wgu1, wd1, conv_w_all) = _gather_and_gate_up(
        x0, [sh["ffn1_w_gate_up"], sh["ffn1_w_down"], w["conv_w"]], [True, True, False], order, "ffn1_gate_up_fwd")
    wd1 = wd1.reshape(-1, d)
    conv_w_full = jnp.transpose(conv_w_all, (1, 0, 2)).reshape(CONV_TAPS, CONV_CH)
    tril = jnp.tril(jnp.ones((CHUNK, CHUNK), F32))
    sg_wm = w["sg_w"] * tril
    sg_wm_b = sg_wm.astype(BF16)
    sg_wmt_b = jnp.swapaxes(sg_wm, 1, 2).astype(BF16)
    sg_bb = jnp.broadcast_to(w["sg_b"][:, :, None], (HEADS, CHUNK, HEAD_DIM))
    row = lambda a: a.reshape(1, -1)

    d2 = [sh["ffn2_w_down"]]
    d2_first = d2[0].shape[0] * 3 // 5 // 16 * 16
    d2_top, d2_bottom = (0, d2_first), (d2_first, d2[0].shape[0] - d2_first)
    (h1t, z1, x1), ((g_in, g_out), (g_d2,)) = _ffn_down_fwd(
        gu1, x0, wd1, row(w["ln1_g"]), row(w["ln1_b"]), "ffn1_down_fwd",
        exch=[_gather_first([sh["mix_w_in"], sh["mix_w_out"]], [True, False]),
              _gather_first(d2, [False], rows=d2_top)])
    in_cols = sh["mix_w_in"].shape[1]
    x1t, ((w_in, w_out), (g_d2,)) = _transpose_bf16(
        x1, "x1_transpose", exch=[_gather_forward([g_in, g_out], [True, False], [in_cols, None]),
                                  _gather_forward([g_d2], [False], [None], rows=d2_top)])
    w_out = w_out.reshape(-1, d)
    top, bottom = (0, d // 2), (d // 2, d // 2)
    gu2 = [sh["ffn2_w_gate_up"]]
    proj, ((g_gu2,),) = _mix_in_proj(x1, w_in, "mix_in_fwd", exch=[_gather_first(gu2, [True], rows=top)])
    (y, yt, conv_c), ((g_gu2,),) = _mixer_fwd(
        proj, conv_w_full, row(w["conv_b"]), row(w["conv_ln_g"]), row(w["conv_ln_b"]),
        w["sg_ln_g"], w["sg_ln_b"], sg_wm_b, sg_bb, "mixer_fwd",
        exch=[_both(_gather_first(gu2, [True], rows=bottom, into=[g_gu2]),
                    _gather_forward([g_gu2], [True], [f2s], rows=top))])
    (z2, x2, x2t), ((wgu2,), (g_d2,)) = _mix_out_fwd(
        y, w_out, x1, row(w["ln2_g"]), row(w["ln2_b"]), "mix_out_fwd",
        exch=[_gather_forward([g_gu2], [True], [f2s], rows=bottom),
              _gather_first(d2, [False], rows=d2_bottom, into=[g_d2])])
    (wd2,) = _exchange_alone(_gather_forward([g_d2], [False], [None], rows=d2_bottom), "ffn2_down_gather_forward")
    wd2 = wd2.reshape(-1, d)
    grads = {}
    (g2, u2, h2t, dz3, do2, grads["ln3_g"], grads["ln3_b"], loss_tile), _ = _ffn_fwd_loss(
        x2, wgu2, wd2, row(w["ln3_g"]), row(w["ln3_b"]), target, "ffn2_fwd_loss")

    f = wd1.shape[0]
    dn = _tile(d, 1024, 128)
    core_chip = jnp.concatenate([my_core, my_chip])
    pair = lambda p, s, label: _pair_sum(p, s, core_chip, "pair_sum_" + label)
    adamw = lambda n, own, got, steps: _adamw_side(w[n], mom[n], var[n], own, got, steps)
    m_tiles = d // _tile(d, 512, 16)
    gu_first = d * 2 // 3 // 16 * 16
    out = {}
    p_d2, _ = _weight_grad(h2t, do2, dn, 512, "ffn2_dw_down")
    p_d2 = p_d2.reshape(N_DEV, f // N_DEV, d)
    (dg2, du2, dx2), ((s_d2,),) = _ffn_bwd(dz3, do2, g2, u2, wgu2, wd2, "ffn2_bwd", exch=[_rs_sibling([p_d2])])
    q_d2, own_d2 = pair(p_d2, s_d2, "ffn2_down")
    d_rows = q_d2.shape[1]
    d_half = d_rows // 2 // 16 * 16
    p_gu2, ((r_d2,),) = _weight_grad(x2t, dg2, f2s, 512, "ffn2_dw_gate", blocks=N_DEV,
                                     exch=[_rs_chips([q_d2], rows=(0, d_half))])
    p_gu2, ((r_d2,),) = _weight_grad(x2t, du2, f2s, 512, "ffn2_dw_up", blocks=N_DEV, block_offset=4, into=p_gu2,
                                     exch=[_rs_chips([q_d2], rows=(d_half, d_rows - d_half), into=[r_d2])])
    (dz2, dz2b, grads["ln2_g"], grads["ln2_b"]), ((s_gu2,),) = _ln_bwd_call(
        z2, dx2, row(w["ln2_g"]), 1.0, "ln2_bwd", exch=[_rs_sibling([p_gu2])])
    q_gu2, own_gu2 = pair(p_gu2, s_gu2, "ffn2_gate_up")
    dy, _ = _mix_out_bwd(dz2b, w_out, "mix_out_bwd")
    p_out, _ = _weight_grad(yt, dz2b, dn, 512, "mix_out_dw")
    p_out = p_out.reshape(N_DEV, -1, d)
    (dproj, grads["conv_w"], grads["conv_b"], grads["conv_ln_g"], grads["conv_ln_b"], grads["sg_ln_g"],
     grads["sg_ln_b"], grads["sg_w"], grads["sg_b"]), ((r_gu2,),) = _mixer_bwd(
        proj, conv_c, dy, conv_w_full, row(w["conv_ln_g"]), row(w["conv_ln_b"]), w["sg_ln_g"], w["sg_ln_b"],
        sg_wm_b, sg_wmt_b, sg_bb, "mixer_bwd", exch=[_rs_chips([q_gu2], rows=(0, gu_first))])
    dx1, ((s_out,), (r_gu2,)) = _mix_in_bwd(
        dproj, w_in, dz2, "mix_in_bwd",
        exch=[_rs_sibling([p_out]), _rs_chips([q_gu2], rows=(gu_first, d - gu_first), into=[r_gu2])])
    p_in, (out["ffn2_w_gate_up"], out["ffn2_w_down"]) = _weight_grad(
        x1t, dproj, in_cols, 512, "mix_in_dw", blocks=N_DEV,
        exch=[adamw("ffn2_w_gate_up", own_gu2, r_gu2, N_DEV * m_tiles), adamw("ffn2_w_down", own_d2, r_d2, N_DEV * m_tiles)])
    (dz1, do1, grads["ln1_g"], grads["ln1_b"]), ((s_in,),) = _ln_bwd_call(
        z1, dx1, row(w["ln1_g"]), 0.5, "ln1_bwd", exch=[_rs_sibling([p_in])])
    q_out, own_out = pair(p_out, s_out, "mix_out")
    q_in, own_in = pair(p_in, s_in, "mix_in")
    small_parts = [_rows128(grads[n]) for n in SMALL]
    packed = jnp.concatenate(small_parts + [_rows128(grads["conv_w"]), loss_tile], axis=0)
    p_d1, ((r_in,),) = _weight_grad(h1t, do1, dn, 512, "ffn1_dw_down", exch=[_rs_chips([q_in])])
    p_d1 = p_d1.reshape(N_DEV, f // N_DEV, d)
    (dg1, du1), ((s_d1,), (r_out,), (small_all,)) = _ffn_bwd_act(
        do1, gu1, wd1, "ffn1_bwd_act",
        exch=[_rs_sibling([p_d1]), _rs_chips([q_out]), _small_gather(packed)])
    q_d1, own_d1 = pair(p_d1, s_d1, "ffn1_down")
    p_gu1, ((r_d1,),) = _weight_grad(x0t, dg1, f2s, 512, "ffn1_dw_gate", blocks=N_DEV, exch=[_rs_chips([q_d1])])
    p_gu1, (out["mix_w_in"], out["mix_w_out"]) = _weight_grad(
        x0t, du1, f2s, 512, "ffn1_dw_up", blocks=N_DEV, block_offset=4, into=p_gu1,
        exch=[adamw("mix_w_in", own_in, r_in, 4 * m_tiles), adamw("mix_w_out", own_out, r_out, 4 * m_tiles)])
    (s_gu1,) = _exchange_alone(_rs_sibling([p_gu1]), "ffn1_gate_up_sibling_exchange")
    q_gu1, own_gu1 = pair(p_gu1, s_gu1, "ffn1_gate_up")
    (grad_x,), ((r_gu1,),) = _ffn_bwd_dx(dz1, dg1, du1, wgu1, "ffn1_bwd_dx", exch=[_rs_chips([q_gu1])])
    for n, own, got in (("ffn1_w_down", own_d1, r_d1), ("ffn1_w_gate_up", own_gu1, r_gu1)):
        out[n] = _adamw_sharded(w[n], mom[n], var[n], own, got, "adamw_" + n)

    cw_rows = CONV_TAPS * CONV_CH // 128
    total = _sum_over_devices(small_all)
    offs = [0]
    for p in small_parts:
        offs.append(offs[-1] + p.shape[0])
    n_small = offs[-1]
    loss = total[n_small + cw_rows, 0]
    g_conv_w = lax.dynamic_slice_in_dim(total[n_small:n_small + cw_rows].reshape(CONV_TAPS, CONV_CH),
                                        me * (CONV_CH // N_DEV), CONV_CH // N_DEV, axis=1)
    pad8 = lambda a: jnp.pad(a, ((0, -a.shape[0] % 8), (0, 0)))
    pack = lambda tree, cw: jnp.concatenate([_rows128(tree[n]) for n in SMALL] + [pad8(cw)], axis=0)
    g_pack = jnp.concatenate([total[:n_small], pad8(g_conv_w)], axis=0)
    d_pack, m_pack, v_pack = _adamw_small(pack(w, w["conv_w"]), g_pack, pack(mom, mom["conv_w"]),
                                          pack(var, var["conv_w"]), "adamw_small")
    for k, n in enumerate(SMALL):
        sl = slice(offs[k], offs[k + 1])
        shp = w[n].shape
        out[n] = (total[sl].reshape(shp), d_pack[sl].reshape(shp), m_pack[sl].reshape(shp), v_pack[sl].reshape(shp))
    sl = slice(n_small, n_small + CONV_TAPS)
    out["conv_w"] = (g_conv_w, d_pack[sl], m_pack[sl], v_pack[sl])

    lead = lambda a: a[None]
    res = [loss, grad_x[None]]
    for kind in range(4):
        res += [lead(out[n][kind]) for n in ORDER]
    return tuple(res)
```

```python
import functools
import math

import jax
import jax.numpy as jnp
from jax import lax
from jax.experimental import pallas as pl
from jax.experimental.pallas import tpu as pltpu

F32, BF16 = jnp.float32, jnp.bfloat16
MESH = pl.DeviceIdType.MESH
ANY = pl.BlockSpec(memory_space=pl.ANY)

N_DEV = 8
LN_EPS = 1e-5
ALPHA = 2.0 ** 0.25
CONV_CH = 1024
CONV_TAPS = 31
HALO = 32
HEADS = 8
HEAD_DIM = 128
CHUNK = 128
ADAM_LR, ADAM_B1, ADAM_B2, ADAM_EPS, ADAM_WD, ADAM_STEP = 0.001, 0.9, 0.999, 1e-08, 0.01, 10
V7X_VMEM_LIMIT = 62 * 2 ** 20
EPILOGUE_ROWS = 128

def _cparams(*sem):
    return pltpu.CompilerParams(dimension_semantics=sem, vmem_limit_bytes=V7X_VMEM_LIMIT)


def _tile(n, pref, mult):
    best = None
    for t in range(mult, min(n, pref) + 1, mult):
        if n % t == 0:
            best = t
    return best if best is not None else n


def _dot(a, b):
    return jnp.dot(a, b, preferred_element_type=F32)


def _dot_nt(a, b):
    return lax.dot_general(a, b, (((1,), (1,)), ((), ())), preferred_element_type=F32)


def _sigmoid(x):
    return 1.0 / (1.0 + jnp.exp(-x))


def _ln_stats(z):
    mu = jnp.mean(z, axis=-1, keepdims=True)
    zc = z - mu
    var = jnp.mean(zc * zc, axis=-1, keepdims=True)
    rstd = lax.rsqrt(var + LN_EPS)
    return zc * rstd, rstd


def _ln(z, g, b):
    xh, _ = _ln_stats(z)
    return xh * g + b


def _ln_bwd(dxh, xh, rstd):
    m1 = jnp.mean(dxh, axis=-1, keepdims=True)
    m2 = jnp.mean(dxh * xh, axis=-1, keepdims=True)
    return rstd * (dxh - m1 - xh * m2)


_GK = math.sqrt(2.0 / math.pi)
_GA = 0.044715


def _gelu_and_grad(x):
    x2 = x * x
    t = jnp.tanh(_GK * (x + _GA * x * x2))
    y = 0.5 * x * (1.0 + t)
    dy = 0.5 * (1.0 + t) + 0.5 * x * (1.0 - t * t) * (_GK * (1.0 + 3.0 * _GA * x2))
    return y, dy


def _silu_grad(a):
    s = _sigmoid(a)
    return s * (1.0 + a * (1.0 - s))


def _place():
    return lax.axis_index("x"), lax.axis_index("y"), lax.axis_index("c")


def _other_chips(x, y):
    return [(1 - x, y), (x, 1 - y), (1 - x, 1 - y)]


def _visit_order(x, y, c):
    chips = _other_chips(x, y)
    return [(x, y, c), (x, y, 1 - c), (*chips[0], c), (*chips[1], c), (*chips[0], 1 - c), (*chips[1], 1 - c),
            (*chips[2], c), (*chips[2], 1 - c)]


def _gather_and_gate_up(xb, shards, relayed, order, name):
    n = len(shards)
    N_COPIES = 10
    t, d = xb.shape
    cols = shards[0].shape[1]
    tm = _tile(t, 1024, 128)
    ni = t // tm
    col_major = [True] + [False] * (n - 1)

    def body(order_ref, x_ref, *refs):
        srcs, gu_ref, xt_ref, dsts = refs[:n], refs[n], refs[n + 1], refs[n + 2:2 * n + 2]
        wbuf, send_sems, recv_sems, local_sems, load_sem = refs[2 * n + 2:]
        b, i = pl.program_id(0), pl.program_id(1)
        x, y, c = _place()
        me, sib = (x, y, c), (x, y, 1 - c)
        chips = _other_chips(x, y)

        near_x, near_y, far = chips

        def slot(w, p, band=None):
            half = shards[w].shape[0] // 2
            rows = None if band is None else (band * half, half)
            return _block_slot(dsts[w], col_major[w], shards[w].shape[1], p, rows)

        def copy(w, s, block, to, band=None, from_src=False):
            return pltpu.make_async_remote_copy(
                src_ref=srcs[w] if from_src else slot(w, block, band), dst_ref=slot(w, block, band),
                send_sem=send_sems.at[N_COPIES * w + s], recv_sem=recv_sems.at[N_COPIES * w + s],
                device_id=to, device_id_type=MESH)

        def own(w):
            return pltpu.make_async_copy(srcs[w], slot(w, me), local_sems.at[w])

        def sends(w):
            out = [copy(w, 0, me, sib, from_src=True), copy(w, 1, me, (*near_x, c), from_src=True),
                   copy(w, 2, me, (*near_y, c), from_src=True)]
            if not relayed[w]:
                out.append(copy(w, 3, me, (*far, c), from_src=True))
            return out

        def passed_on(w):
            out = [copy(w, 4, (*near_x, c), sib), copy(w, 5, (*near_y, c), sib)]
            if relayed[w]:
                out += [copy(w, 6, (*far, c), sib, band=0), copy(w, 9, (*far, c), sib, band=1),
                        copy(w, 7, (*near_x, c), (*near_y, c), band=0), copy(w, 8, (*near_y, c), (*near_x, c), band=1)]
            else:
                out.append(copy(w, 6, (*far, c), sib))
            return out

        def start_sends(w):
            own(w).start()
            for cp in sends(w):
                cp.start()

        def got_near_x(w):
            copy(w, 1, (*near_x, c), me).wait_recv()
            copy(w, 4, (*near_x, c), sib).start()
            if relayed[w]:
                copy(w, 7, (*near_x, c), (*near_y, c), band=0).start()

        def got_near_y(w):
            copy(w, 2, (*near_y, c), me).wait_recv()
            copy(w, 5, (*near_y, c), sib).start()
            if relayed[w]:
                copy(w, 8, (*near_y, c), (*near_x, c), band=1).start()

        def got_far(w):
            if relayed[w]:
                copy(w, 7, (*far, c), me, band=0).wait_recv()
                copy(w, 6, (*far, c), sib, band=0).start()
                copy(w, 8, (*far, c), me, band=1).wait_recv()
                copy(w, 9, (*far, c), sib, band=1).start()
            else:
                copy(w, 3, (*far, c), me).wait_recv()
                copy(w, 6, (*far, c), sib).start()

        def got_from_sibling(w, which):
            if which == 0:
                copy(w, 0, sib, me).wait_recv()
            elif which == 3 and relayed[w]:
                copy(w, 6, (*far, 1 - c), me, band=0).wait_recv()
                copy(w, 9, (*far, 1 - c), me, band=1).wait_recv()
            else:
                copy(w, 3 + which, (*chips[which - 1], 1 - c), me).wait_recv()

        others = range(1, n)

        def arrive(k):
            if k == 0:
                own(0).wait()
            elif k == 1:
                got_from_sibling(0, 0)
            elif k == 2:
                got_near_x(0)
                for w in others:
                    start_sends(w)
            elif k == 3:
                got_near_y(0)
            elif k in (4, 5):
                got_from_sibling(0, k - 3)
            elif k == 6:
                got_far(0)
                for w in others:
                    got_near_x(w)
                    got_near_y(w)
            else:
                got_from_sibling(0, 3)
                for w in others:
                    got_far(w)

        def load(k):
            at = pl.multiple_of(order_ref[k] * cols, 128)
            return pltpu.make_async_copy(dsts[0].at[:, pl.ds(at, cols)], wbuf.at[k % 2], load_sem.at[k % 2])

        @pl.when((b == 0) & (i == 0))
        def _():
            start_sends(0)
            arrive(0)
            load(0).start()
            load(0).wait()

        early = ni - 1
        for k in range(1, N_DEV):
            @pl.when((b == k - 1) & (i == early))
            def _(k=k):
                arrive(k)
                load(k).start()

            @pl.when((b == k) & (i == 0))
            def _(k=k):
                load(k).wait()

        gu_ref[...] = _dot(x_ref[...].astype(BF16), wbuf[b % 2]).astype(BF16)

        @pl.when(b == 0)
        def _():
            xt_ref[...] = x_ref[...].T.astype(BF16)

        @pl.when((b == N_DEV - 1) & (i == ni - 1))
        def _():
            for w in others:
                for which in range(4):
                    got_from_sibling(w, which)
                own(w).wait()
            for w in range(n):
                for cp in sends(w) + passed_on(w):
                    cp.wait_send()

    grid_spec = pltpu.PrefetchScalarGridSpec(
        num_scalar_prefetch=1, grid=(N_DEV, ni),
        in_specs=[pl.BlockSpec((tm, d), lambda b, i, o: (i, 0))] + [ANY] * n,
        out_specs=[pl.BlockSpec((tm, cols), lambda b, i, o: (i, o[b])),
                   pl.BlockSpec((d, tm), lambda b, i, o: (0, jnp.where(b == 0, i, ni - 1)))] + [ANY] * n,
        scratch_shapes=[pltpu.VMEM((2, d, cols), BF16), pltpu.SemaphoreType.DMA((N_COPIES * n,)),
                        pltpu.SemaphoreType.DMA((N_COPIES * n,)), pltpu.SemaphoreType.DMA((n,)),
                        pltpu.SemaphoreType.DMA((2,))])
    res = pl.pallas_call(
        body, name=name, grid_spec=grid_spec,
        out_shape=[jax.ShapeDtypeStruct((t, N_DEV * cols), BF16), jax.ShapeDtypeStruct((d, t), BF16)]
        + [_gathered_shape(s, cm) for s, cm in zip(shards, col_major)],
        compiler_params=_cparams("arbitrary", "arbitrary"),
    )(order, xb, *shards)
    return res[0], res[1], res[2:]


class _Exchange:
    def __init__(self, ins, io, new, n_sems, n_local, make):
        self.ins, self.io, self.new = list(ins), list(io), list(new)
        self.n_sems, self.n_local, self.make = n_sems, n_local, make


def _block_slot(ref, col_major, cols, place, rows=None):
    k = 4 * place[0] + 2 * place[1] + place[2]
    band = slice(None) if rows is None else pl.ds(rows[0], rows[1])
    if col_major:
        return ref.at[band, pl.ds(pl.multiple_of(k * cols, 128), cols)]
    return ref.at[k] if rows is None else ref.at[k, band]


def _gathered_shape(s, col_major):
    return jax.ShapeDtypeStruct((s.shape[0], N_DEV * s.shape[1]) if col_major else (N_DEV,) + s.shape, s.dtype)


def _gather_first(shards, col_major, rows=None, into=None):
    n = len(shards)
    new = [] if into is not None else [_gathered_shape(s, cm) for s, cm in zip(shards, col_major)]

    def make(in_refs, io_refs, new_refs, send_sems, recv_sems, local_sems, base=0, local_base=0):
        x, y, c = _place()
        targets = [(x, y, 1 - c)] + [(*chip, c) for chip in _other_chips(x, y)]
        gathered = io_refs if into is not None else new_refs
        copies = []
        for w in range(n):
            src = in_refs[w] if rows is None else in_refs[w].at[pl.ds(rows[0], rows[1])]
            slot = _block_slot(gathered[w], col_major[w], shards[w].shape[1], (x, y, c), rows)
            copies.append(pltpu.make_async_copy(src, slot, local_sems.at[local_base + w]))
            for s, to in enumerate(targets):
                copies.append(pltpu.make_async_remote_copy(
                    src_ref=src, dst_ref=slot, send_sem=send_sems.at[base + 4 * w + s],
                    recv_sem=recv_sems.at[base + 4 * w + s], device_id=to, device_id_type=MESH))
        return copies

    return _Exchange(shards, into or [], new, 4 * n, n, make)


def _gather_forward(gathered, col_major, cols, rows=None):
    n = len(gathered)

    def make(in_refs, io_refs, new_refs, send_sems, recv_sems, local_sems, base=0, local_base=0):
        x, y, c = _place()
        copies = []
        for w in range(n):
            for j, chip in enumerate(_other_chips(x, y)):
                slot = _block_slot(io_refs[w], col_major[w], cols[w], (*chip, c), rows)
                copies.append(pltpu.make_async_remote_copy(
                    src_ref=slot, dst_ref=slot, send_sem=send_sems.at[base + 3 * w + j],
                    recv_sem=recv_sems.at[base + 3 * w + j], device_id=(x, y, 1 - c), device_id_type=MESH))
        return copies

    return _Exchange([], gathered, [], 3 * n, 0, make)


def _both(a, b):
    def make(in_refs, io_refs, new_refs, send_sems, recv_sems, local_sems):
        na = len(a.ins)
        return (a.make(in_refs[:na], io_refs, [], send_sems, recv_sems, local_sems, 0, 0)
                + b.make(in_refs[na:], io_refs, [], send_sems, recv_sems, local_sems, a.n_sems, a.n_local))

    return _Exchange(a.ins + b.ins, a.io, [], a.n_sems + b.n_sems, a.n_local + b.n_local, make)


def _rs_sibling(parts):
    n = len(parts)

    def make(in_refs, io_refs, new_refs, send_sems, recv_sems, local_sems):
        x, y, c = _place()
        copies = []
        for w in range(n):
            for j in range(4):
                copies.append(pltpu.make_async_remote_copy(
                    src_ref=in_refs[w].at[2 * j + (1 - c)], dst_ref=new_refs[w].at[j],
                    send_sem=send_sems.at[4 * w + j], recv_sem=recv_sems.at[4 * w + j],
                    device_id=(x, y, 1 - c), device_id_type=MESH))
        return copies

    return _Exchange(parts, [], [jax.ShapeDtypeStruct((4,) + p.shape[1:], p.dtype) for p in parts], 4 * n, 0, make)


def _rs_chips(chip_parts, rows=None, into=None):
    n = len(chip_parts)
    band = slice(None) if rows is None else pl.ds(rows[0], rows[1])
    new = [] if into is not None else [jax.ShapeDtypeStruct((3,) + p.shape[1:], p.dtype) for p in chip_parts]

    def make(in_refs, io_refs, new_refs, send_sems, recv_sems, local_sems):
        x, y, c = _place()
        landing = io_refs if into is not None else new_refs
        copies = []
        for w in range(n):
            for rel, (px, py) in enumerate(_other_chips(x, y)):
                copies.append(pltpu.make_async_remote_copy(
                    src_ref=in_refs[w].at[2 * px + py, band], dst_ref=landing[w].at[rel, band],
                    send_sem=send_sems.at[3 * w + rel], recv_sem=recv_sems.at[3 * w + rel],
                    device_id=(px, py, c), device_id_type=MESH))
        return copies

    return _Exchange(chip_parts, into or [], new, 3 * n, 0, make)


class _Side:
    def __init__(self, ins, in_blocks, out_shapes, out_blocks, n_tiles, fn):
        self.ins, self.in_blocks, self.out_shapes, self.out_blocks = list(ins), in_blocks, list(out_shapes), out_blocks
        self.n_tiles, self.fn = n_tiles, fn


def _call(body, exch, *, name, grid, in_specs, out_specs, out_shape, scratch_shapes=(), semantics,
          input_output_aliases=None):
    exch = list(exch)
    in_specs, out_specs, out_shape = list(in_specs), list(out_specs), list(out_shape)
    scratch_shapes = list(scratch_shapes)
    if not exch:
        fn = pl.pallas_call(body, name=name, grid=grid, in_specs=in_specs, out_specs=out_specs, out_shape=out_shape,
                            scratch_shapes=scratch_shapes, input_output_aliases=input_output_aliases or {},
                            compiler_params=_cparams(*semantics))
        return lambda *args: (fn(*args), [])
    n_in, n_out, n_scr = len(in_specs), len(out_specs), len(scratch_shapes)
    aliases = dict(input_output_aliases or {})
    all_in, all_out_specs, all_out_shape, all_scr = list(in_specs), list(out_specs), list(out_shape), list(scratch_shapes)
    extra_args = []

    def step(idx):
        s = idx[0]
        for a in range(1, len(grid)):
            s = s * grid[a] + idx[a]
        return s

    def tile_spec(shape, where, n_tiles):
        return pl.BlockSpec(shape, lambda *idx: where(jnp.minimum(step(idx), n_tiles - 1)))

    for ex in exch:
        if isinstance(ex, _Side):
            all_in += [tile_spec(shape, where, ex.n_tiles) for shape, where in ex.in_blocks]
            extra_args += ex.ins
            all_out_specs += [tile_spec(shape, where, ex.n_tiles) for shape, where in ex.out_blocks]
            all_out_shape += ex.out_shapes
            continue
        for k, a in enumerate(ex.io):
            aliases[len(all_in) + len(ex.ins) + k] = len(all_out_specs) + k
        all_in += [ANY] * (len(ex.ins) + len(ex.io))
        extra_args += ex.ins + ex.io
        all_out_specs += [ANY] * (len(ex.io) + len(ex.new))
        all_out_shape += [jax.ShapeDtypeStruct(a.shape, a.dtype) for a in ex.io] + ex.new
        all_scr += [pltpu.SemaphoreType.DMA((ex.n_sems,)), pltpu.SemaphoreType.DMA((ex.n_sems,)),
                    pltpu.SemaphoreType.DMA((max(ex.n_local, 1),))]

    n_ins = [len(ex.ins) if isinstance(ex, _Side) else len(ex.ins) + len(ex.io) for ex in exch]
    n_outs = [len(ex.out_shapes) if isinstance(ex, _Side) else len(ex.io) + len(ex.new) for ex in exch]

    def wrapped(*refs):
        pos = n_in
        ex_in = []
        for k in n_ins:
            ex_in.append(refs[pos:pos + k])
            pos += k
        outs = refs[pos:pos + n_out]
        pos += n_out
        ex_out = []
        for k in n_outs:
            ex_out.append(refs[pos:pos + k])
            pos += k
        scr = refs[pos:pos + n_scr]
        pos += n_scr
        idx = [pl.program_id(a) for a in range(len(grid))]
        first = functools.reduce(jnp.logical_and, [i == 0 for i in idx])
        last = functools.reduce(jnp.logical_and, [i == g - 1 for i, g in zip(idx, grid)])

        def copies():
            out, at = [], pos
            for ex, ei, eo in zip(exch, ex_in, ex_out):
                if not isinstance(ex, _Side):
                    out += ex.make(ei[:len(ex.ins)], eo[:len(ex.io)], eo[len(ex.io):], *refs[at:at + 3])
                    at += 3
            return out

        @pl.when(first)
        def _():
            for cp in copies():
                cp.start()

        body(*refs[:n_in], *outs, *scr)
        for ex, ei, eo in zip(exch, ex_in, ex_out):
            if isinstance(ex, _Side):
                pl.when(step(idx) < ex.n_tiles)(functools.partial(ex.fn, ei, eo))

        @pl.when(last)
        def _():
            for cp in copies():
                cp.wait()

    fn = pl.pallas_call(wrapped, name=name, grid=grid, in_specs=all_in, out_specs=all_out_specs,
                        out_shape=all_out_shape, scratch_shapes=all_scr, input_output_aliases=aliases,
                        compiler_params=_cparams(*(["arbitrary"] * len(grid))))

    def run(*args):
        res = fn(*args, *extra_args)
        outs, pos, ex_res = res[:n_out], n_out, []
        for k in n_outs:
            ex_res.append(list(res[pos:pos + k]))
            pos += k
        return outs, ex_res

    return run


def _exchange_alone(ex, name):
    def body():
        pass

    _, res = _call(body, [ex], name=name, grid=(1,), in_specs=[], out_specs=[], out_shape=[], semantics=("arbitrary",))()
    return res[0]


def _small_gather(part):
    def make(in_refs, io_refs, new_refs, send_sems, recv_sems, local_sems):
        x, y, c = _place()
        slot = new_refs[0].at[4 * x + 2 * y + c]
        copies = [pltpu.make_async_copy(in_refs[0], slot, local_sems.at[0])]
        for d in range(1, N_DEV):
            peer = (1 - x if d & 4 else x, 1 - y if d & 2 else y, 1 - c if d & 1 else c)
            copies.append(pltpu.make_async_remote_copy(
                src_ref=in_refs[0], dst_ref=slot, send_sem=send_sems.at[d - 1], recv_sem=recv_sems.at[d - 1],
                device_id=peer, device_id_type=MESH))
        return copies

    return _Exchange([part], [], [jax.ShapeDtypeStruct((N_DEV,) + part.shape, part.dtype)], N_DEV - 1, 1, make)


def _sum_over_devices(parts):
    _, rows, lanes = parts.shape

    def body(p_ref, o_ref):
        acc = p_ref[0]
        for k in range(1, N_DEV):
            acc = acc + p_ref[k]
        o_ref[...] = acc

    return pl.pallas_call(
        body, name="small_grads_sum", grid=(1,), out_shape=jax.ShapeDtypeStruct((rows, lanes), F32),
        in_specs=[pl.BlockSpec((N_DEV, rows, lanes), lambda i: (0, 0, 0))],
        out_specs=pl.BlockSpec((rows, lanes), lambda i: (0, 0)),
        compiler_params=_cparams("arbitrary"),
    )(parts)


def _transpose_bf16(a, name, exch=(), with_copy=False):
    r, c = a.shape
    tr, tc = _tile(r, 512, 128), _tile(c, 512, 128)

    def body(a_ref, o_ref, *copy_ref):
        v = a_ref[...].astype(F32)
        o_ref[...] = v.T.astype(BF16)
        if with_copy:
            copy_ref[0][...] = v.astype(BF16)

    outs, ex = _call(
        body, exch, name=name, grid=(r // tr, c // tc),
        out_shape=[jax.ShapeDtypeStruct((c, r), BF16)] + [jax.ShapeDtypeStruct((r, c), BF16)] * with_copy,
        in_specs=[pl.BlockSpec((tr, tc), lambda i, j: (i, j))],
        out_specs=[pl.BlockSpec((tc, tr), lambda i, j: (j, i))] + [pl.BlockSpec((tr, tc), lambda i, j: (i, j))] * with_copy,
        semantics=("parallel", "parallel"),
    )(a)
    return (outs if with_copy else outs[0]), ex


def _ffn_fwd_loss(x, wgu, wd, ln_g, ln_b, target, name, exch=()):
    t, d = x.shape
    f = wd.shape[0]
    tm, tf = _tile(t, 512, 128), _tile(f, 512, 128)
    nf = f // tf

    def body(x_ref, wg_ref, wu_ref, wd_ref, lg_ref, lb_ref, t_ref,
             go_ref, uo_ref, ht_ref, dz_ref, dzb_ref, dlg_ref, dlb_ref, loss_ref, xb, acc):
        i, j = pl.program_id(0), pl.program_id(1)

        @pl.when(j == 0)
        def _():
            xb[...] = x_ref[...].astype(BF16)
            acc[...] = jnp.zeros_like(acc)

        @pl.when((i == 0) & (j == 0))
        def _():
            dlg_ref[...] = jnp.zeros_like(dlg_ref)
            dlb_ref[...] = jnp.zeros_like(dlb_ref)
            loss_ref[...] = jnp.zeros_like(loss_ref)

        g = _dot(xb[...], wg_ref[...])
        u = _dot(xb[...], wu_ref[...])
        h = g * _sigmoid(g) * u
        go_ref[...] = g.astype(BF16)
        uo_ref[...] = u.astype(BF16)
        ht_ref[...] = h.T.astype(BF16)
        acc[...] += _dot(h.astype(BF16), wd_ref[...])

        @pl.when(j == nf - 1)
        def _():
            for r in range(0, tm, EPILOGUE_ROWS):
                rows = slice(r, r + EPILOGUE_ROWS)
                xh, rstd = _ln_stats(ALPHA * x_ref[rows, :] + 0.5 * acc[rows, :])
                e = xh * lg_ref[...] + lb_ref[...] - t_ref[rows, :]
                loss_ref[...] += 0.5 * jnp.sum(jnp.sum(e * e, axis=-1, keepdims=True) * (1.0 / d), axis=0,
                                               keepdims=True)
                dy = e * (1.0 / d)
                dz = _ln_bwd(dy * lg_ref[...], xh, rstd)
                dz_ref[rows, :] = dz
                dzb_ref[rows, :] = (0.5 * dz).astype(BF16)
                dlg_ref[...] += jnp.sum(dy * xh, axis=0, keepdims=True)
                dlb_ref[...] += jnp.sum(dy, axis=0, keepdims=True)

    row = lambda i, j: (i, 0)
    fixed = lambda i, j: (0, 0)
    return _call(
        body, exch, name=name, grid=(t // tm, nf),
        out_shape=[jax.ShapeDtypeStruct((t, f), BF16), jax.ShapeDtypeStruct((t, f), BF16),
                   jax.ShapeDtypeStruct((f, t), BF16), jax.ShapeDtypeStruct((t, d), F32),
                   jax.ShapeDtypeStruct((t, d), BF16), jax.ShapeDtypeStruct((1, d), F32),
                   jax.ShapeDtypeStruct((1, d), F32), jax.ShapeDtypeStruct((8, 128), F32)],
        in_specs=[pl.BlockSpec((tm, d), row),
                  pl.BlockSpec((d, tf), lambda i, j: (0, j)),
                  pl.BlockSpec((d, tf), lambda i, j: (0, j + nf)),
                  pl.BlockSpec((tf, d), lambda i, j: (j, 0)),
                  pl.BlockSpec((1, d), fixed), pl.BlockSpec((1, d), fixed), pl.BlockSpec((tm, d), row)],
        out_specs=[pl.BlockSpec((tm, tf), lambda i, j: (i, j)), pl.BlockSpec((tm, tf), lambda i, j: (i, j)),
                   pl.BlockSpec((tf, tm), lambda i, j: (j, i)), pl.BlockSpec((tm, d), row), pl.BlockSpec((tm, d), row),
                   pl.BlockSpec((1, d), fixed), pl.BlockSpec((1, d), fixed), pl.BlockSpec((8, 128), fixed)],
        scratch_shapes=[pltpu.VMEM((tm, d), BF16), pltpu.VMEM((tm, d), F32)],
        semantics=("arbitrary", "arbitrary"),
    )(x, wgu, wgu, wd, ln_g, ln_b, target)


def _ffn_down_fwd(gu, x, wd, ln_g, ln_b, name, exch=()):
    t, d = x.shape
    f = wd.shape[0]
    tm, tf = _tile(t, 512, 128), _tile(f, 512, 128)
    nf = f // tf

    def body(g_ref, u_ref, wd_ref, x_ref, lg_ref, lb_ref, ht_ref, z_ref, xn_ref, acc):
        j = pl.program_id(1)

        @pl.when(j == 0)
        def _():
            acc[...] = jnp.zeros_like(acc)

        g = g_ref[...].astype(F32)
        h = g * _sigmoid(g) * u_ref[...].astype(F32)
        ht_ref[...] = h.T.astype(BF16)
        acc[...] += _dot(h.astype(BF16), wd_ref[...])

        @pl.when(j == nf - 1)
        def _():
            z = ALPHA * x_ref[...] + 0.5 * acc[...]
            z_ref[...] = z
            xn_ref[...] = _ln(z, lg_ref[...], lb_ref[...])

    row = lambda i, j: (i, 0)
    fixed = lambda i, j: (0, 0)
    return _call(
        body, exch, name=name, grid=(t // tm, nf),
        out_shape=[jax.ShapeDtypeStruct((f, t), BF16), jax.ShapeDtypeStruct((t, d), F32),
                   jax.ShapeDtypeStruct((t, d), F32)],
        in_specs=[pl.BlockSpec((tm, tf), lambda i, j: (i, j)), pl.BlockSpec((tm, tf), lambda i, j: (i, j + nf)),
                  pl.BlockSpec((tf, d), lambda i, j: (j, 0)), pl.BlockSpec((tm, d), row),
                  pl.BlockSpec((1, d), fixed), pl.BlockSpec((1, d), fixed)],
        out_specs=[pl.BlockSpec((tf, tm), lambda i, j: (j, i)), pl.BlockSpec((tm, d), row), pl.BlockSpec((tm, d), row)],
        scratch_shapes=[pltpu.VMEM((tm, d), F32)],
        semantics=("parallel", "arbitrary"),
    )(gu, gu, wd, x, ln_g, ln_b)


def _ffn_act_grads(dh, g_ref, u_ref):
    gg = g_ref[...].astype(F32)
    uu = u_ref[...].astype(F32)
    s = _sigmoid(gg)
    du = (dh * (gg * s)).astype(BF16)
    dg = (dh * uu * (s * (1.0 + gg * (1.0 - s)))).astype(BF16)
    return dg, du


def _ffn_bwd(dz, do, g, u, wgu, wd, name, exch=()):
    t, d = dz.shape
    f = wd.shape[0]
    tm, tf = _tile(t, 512, 128), _tile(f, 512, 128)
    nf = f // tf

    def body(dz_ref, do_ref, g_ref, u_ref, wg_ref, wu_ref, wd_ref, dg_ref, du_ref, dx_ref, acc):
        j = pl.program_id(1)

        @pl.when(j == 0)
        def _():
            acc[...] = jnp.zeros_like(acc)

        dg, du = _ffn_act_grads(_dot_nt(do_ref[...], wd_ref[...]), g_ref, u_ref)
        dg_ref[...] = dg
        du_ref[...] = du
        acc[...] += _dot_nt(dg, wg_ref[...]) + _dot_nt(du, wu_ref[...])

        @pl.when(j == nf - 1)
        def _():
            dx_ref[...] = ALPHA * dz_ref[...] + acc[...]

    row = lambda i, j: (i, 0)
    tile = lambda i, j: (i, j)
    return _call(
        body, exch, name=name, grid=(t // tm, nf),
        out_shape=[jax.ShapeDtypeStruct((t, f), BF16), jax.ShapeDtypeStruct((t, f), BF16),
                   jax.ShapeDtypeStruct((t, d), F32)],
        in_specs=[pl.BlockSpec((tm, d), row), pl.BlockSpec((tm, d), row),
                  pl.BlockSpec((tm, tf), tile), pl.BlockSpec((tm, tf), tile),
                  pl.BlockSpec((d, tf), lambda i, j: (0, j)),
                  pl.BlockSpec((d, tf), lambda i, j: (0, j + nf)),
                  pl.BlockSpec((tf, d), lambda i, j: (j, 0))],
        out_specs=[pl.BlockSpec((tm, tf), tile), pl.BlockSpec((tm, tf), tile), pl.BlockSpec((tm, d), row)],
        scratch_shapes=[pltpu.VMEM((tm, d), F32)],
        semantics=("parallel", "arbitrary"),
    )(dz, do, g, u, wgu, wgu, wd)


def _ffn_bwd_act(do, gu, wd, name, exch=()):
    t, d = do.shape
    f = wd.shape[0]
    tm, tf = _tile(t, 2048, 128), _tile(f, 512, 128)
    nf = f // tf

    def body(do_ref, g_ref, u_ref, wd_ref, dg_ref, du_ref):
        dg, du = _ffn_act_grads(_dot_nt(do_ref[...], wd_ref[...]), g_ref, u_ref)
        dg_ref[...] = dg
        du_ref[...] = du

    tile = lambda i, j: (i, j)
    return _call(
        body, exch, name=name, grid=(t // tm, f // tf),
        out_shape=[jax.ShapeDtypeStruct((t, f), BF16), jax.ShapeDtypeStruct((t, f), BF16)],
        in_specs=[pl.BlockSpec((tm, d), lambda i, j: (i, 0)), pl.BlockSpec((tm, tf), tile),
                  pl.BlockSpec((tm, tf), lambda i, j: (i, j + nf)), pl.BlockSpec((tf, d), lambda i, j: (j, 0))],
        out_specs=[pl.BlockSpec((tm, tf), tile), pl.BlockSpec((tm, tf), tile)],
        semantics=("parallel", "parallel"),
    )(do, gu, gu, wd)


def _ffn_bwd_dx(dz, dg, du, wgu, name, exch=()):
    t, d = dz.shape
    f = dg.shape[1]
    tm, tn = _tile(t, 512, 128), _tile(d, 256, 128)

    def body(dz_ref, dg_ref, du_ref, wg_ref, wu_ref, dx_ref):
        dx_ref[...] = ALPHA * dz_ref[...] + _dot_nt(dg_ref[...], wg_ref[...]) + _dot_nt(du_ref[...], wu_ref[...])

    row = lambda i, n: (i, 0)
    tile = lambda i, n: (i, n)
    return _call(
        body, exch, name=name, grid=(t // tm, d // tn), out_shape=[jax.ShapeDtypeStruct((t, d), F32)],
        in_specs=[pl.BlockSpec((tm, tn), tile), pl.BlockSpec((tm, f), row), pl.BlockSpec((tm, f), row),
                  pl.BlockSpec((tn, f), lambda i, n: (n, 0)), pl.BlockSpec((tn, f), lambda i, n: (n, 1))],
        out_specs=[pl.BlockSpec((tm, tn), tile)],
        semantics=("parallel", "arbitrary"),
    )(dz, dg, du, wgu, wgu)


def _weight_grad(at, b, tn, tmm, name, blocks=None, block_offset=0, into=None, exch=()):
    m, t = at.shape
    nn = b.shape[1]
    tmm = _tile(m, tmm, 16)
    assert nn % tn == 0

    def body(*refs):
        at_ref, b_ref, o_ref = refs[0], refs[1], refs[-1]
        r = _dot(at_ref[...], b_ref[...]).astype(BF16)
        if blocks is None:
            o_ref[...] = r
        else:
            o_ref[0] = r

    in_specs = [pl.BlockSpec((tmm, t), lambda n, i: (i, 0)), pl.BlockSpec((t, tn), lambda n, i: (0, n))]
    args = [at, b]
    aliases = {}
    if into is not None:
        in_specs.append(ANY)
        args.append(into)
        aliases = {2: 0}
    if blocks is None:
        out_shape = jax.ShapeDtypeStruct((m, nn), BF16)
        out_spec = pl.BlockSpec((tmm, tn), lambda n, i: (i, n))
    else:
        out_shape = jax.ShapeDtypeStruct((blocks, m, tn), BF16)
        out_spec = pl.BlockSpec((1, tmm, tn), lambda n, i: (n + block_offset, i, 0))
    (out,), ex = _call(
        body, exch, name=name, grid=(nn // tn, m // tmm), out_shape=[out_shape],
        in_specs=in_specs, out_specs=[out_spec], input_output_aliases=aliases,
        semantics=("parallel", "parallel"),
    )(*args)
    return out, ex


def _mix_in_proj(x, w_in, name, exch=()):
    t, d = x.shape
    n_out = w_in.shape[1]
    tm, cb = _tile(t, 512, 128), _tile(n_out, 1024, 128)

    def body(x_ref, w_ref, o_ref, xb):
        @pl.when(pl.program_id(1) == 0)
        def _():
            xb[...] = x_ref[...].astype(BF16)

        o_ref[...] = _dot(xb[...], w_ref[...])

    (out,), ex = _call(
        body, exch, name=name, grid=(t // tm, n_out // cb), out_shape=[jax.ShapeDtypeStruct((t, n_out), F32)],
        in_specs=[pl.BlockSpec((tm, d), lambda i, k: (i, 0)), pl.BlockSpec((d, cb), lambda i, k: (0, k))],
        out_specs=[pl.BlockSpec((tm, cb), lambda i, k: (i, k))],
        scratch_shapes=[pltpu.VMEM((tm, d), BF16)],
        semantics=("parallel", "arbitrary"),
    )(x, w_in)
    return out, ex


def _mix_in_bwd(dproj, w_in, dz, name, exch=()):
    t, d = dz.shape
    kk = w_in.shape[1]
    tm, tn = _tile(t, 512, 128), _tile(d, 512, 128)

    def body(dp_ref, w_ref, dz_ref, dx_ref):
        dx_ref[...] = ALPHA * dz_ref[...] + _dot_nt(dp_ref[...], w_ref[...])

    (out,), ex = _call(
        body, exch, name=name, grid=(t // tm, d // tn), out_shape=[jax.ShapeDtypeStruct((t, d), F32)],
        in_specs=[pl.BlockSpec((tm, kk), lambda i, n: (i, 0)), pl.BlockSpec((tn, kk), lambda i, n: (n, 0)),
                  pl.BlockSpec((tm, tn), lambda i, n: (i, n))],
        out_specs=[pl.BlockSpec((tm, tn), lambda i, n: (i, n))],
        semantics=("parallel", "arbitrary"),
    )(dproj, w_in, dz)
    return out, ex


def _mix_out_fwd(y, w_out, x, ln_g, ln_b, name, exch=()):
    t, d = x.shape
    kk = y.shape[1]
    tm = _tile(t, 256, 128)

    def body(y_ref, w_ref, x_ref, g_ref, b_ref, z_ref, xn_ref, xnt_ref):
        z = ALPHA * x_ref[...] + _dot(y_ref[...], w_ref[...])
        z_ref[...] = z
        xn = _ln(z, g_ref[...], b_ref[...])
        xn_ref[...] = xn
        xnt_ref[...] = xn.T.astype(BF16)

    row = lambda i: (i, 0)
    fixed = lambda i: (0, 0)
    return _call(
        body, exch, name=name, grid=(t // tm,),
        out_shape=[jax.ShapeDtypeStruct((t, d), F32), jax.ShapeDtypeStruct((t, d), F32),
                   jax.ShapeDtypeStruct((d, t), BF16)],
        in_specs=[pl.BlockSpec((tm, kk), row), pl.BlockSpec((kk, d), fixed), pl.BlockSpec((tm, d), row),
                  pl.BlockSpec((1, d), fixed), pl.BlockSpec((1, d), fixed)],
        out_specs=[pl.BlockSpec((tm, d), row), pl.BlockSpec((tm, d), row), pl.BlockSpec((d, tm), lambda i: (0, i))],
        semantics=("parallel",),
    )(y, w_out, x, ln_g, ln_b)


def _mix_out_bwd(dzb, w_out, name, exch=()):
    t, d = dzb.shape
    kk = w_out.shape[0]
    tm = _tile(t, 512, 128)

    def body(dz_ref, w_ref, dy_ref):
        dy_ref[...] = _dot_nt(dz_ref[...], w_ref[...])

    (out,), ex = _call(
        body, exch, name=name, grid=(t // tm,), out_shape=[jax.ShapeDtypeStruct((t, kk), F32)],
        in_specs=[pl.BlockSpec((tm, d), lambda i: (i, 0)), pl.BlockSpec((kk, d), lambda i: (0, 0))],
        out_specs=[pl.BlockSpec((tm, kk), lambda i: (i, 0))],
        semantics=("parallel",),
    )(dzb, w_out)
    return out, ex


def _loss_ln_bwd(z, target, ln_g, ln_b, bf16_scale, name):
    t, d = z.shape
    tm = _tile(t, 512, 8)

    def body(z_ref, t_ref, g_ref, b_ref, dz_ref, dzb_ref, dg_ref, db_ref, loss_ref):
        @pl.when(pl.program_id(0) == 0)
        def _():
            dg_ref[...] = jnp.zeros_like(dg_ref)
            db_ref[...] = jnp.zeros_like(db_ref)
            loss_ref[...] = jnp.zeros_like(loss_ref)

        xh, rstd = _ln_stats(z_ref[...])
        e = xh * g_ref[...] + b_ref[...] - t_ref[...]
        loss_ref[...] += 0.5 * jnp.sum(jnp.sum(e * e, axis=-1, keepdims=True) * (1.0 / d), axis=0, keepdims=True)
        dy = e * (1.0 / d)
        dz = _ln_bwd(dy * g_ref[...], xh, rstd)
        dz_ref[...] = dz
        dzb_ref[...] = (bf16_scale * dz).astype(BF16)
        dg_ref[...] += jnp.sum(dy * xh, axis=0, keepdims=True)
        db_ref[...] += jnp.sum(dy, axis=0, keepdims=True)

    row = lambda i: (i, 0)
    fixed = lambda i: (0, 0)
    return pl.pallas_call(
        body, name=name, grid=(t // tm,),
        out_shape=[jax.ShapeDtypeStruct((t, d), F32), jax.ShapeDtypeStruct((t, d), BF16),
                   jax.ShapeDtypeStruct((1, d), F32), jax.ShapeDtypeStruct((1, d), F32),
                   jax.ShapeDtypeStruct((8, 128), F32)],
        in_specs=[pl.BlockSpec((tm, d), row), pl.BlockSpec((tm, d), row), pl.BlockSpec((1, d), fixed),
                  pl.BlockSpec((1, d), fixed)],
        out_specs=[pl.BlockSpec((tm, d), row), pl.BlockSpec((tm, d), row), pl.BlockSpec((1, d), fixed),
                   pl.BlockSpec((1, d), fixed), pl.BlockSpec((8, 128), fixed)],
        compiler_params=_cparams("arbitrary"),
    )(z, target, ln_g, ln_b)


def _ln_bwd_call(z, dy, ln_g, bf16_scale, name, exch=()):
    t, d = z.shape
    tm = _tile(t, 512, 8)

    def body(z_ref, dy_ref, g_ref, dz_ref, dzb_ref, dg_ref, db_ref):
        @pl.when(pl.program_id(0) == 0)
        def _():
            dg_ref[...] = jnp.zeros_like(dg_ref)
            db_ref[...] = jnp.zeros_like(db_ref)

        xh, rstd = _ln_stats(z_ref[...])
        dy = dy_ref[...]
        dz = _ln_bwd(dy * g_ref[...], xh, rstd)
        dz_ref[...] = dz
        dzb_ref[...] = (bf16_scale * dz).astype(BF16)
        dg_ref[...] += jnp.sum(dy * xh, axis=0, keepdims=True)
        db_ref[...] += jnp.sum(dy, axis=0, keepdims=True)

    row = lambda i: (i, 0)
    fixed = lambda i: (0, 0)
    return _call(
        body, exch, name=name, grid=(t // tm,),
        out_shape=[jax.ShapeDtypeStruct((t, d), F32), jax.ShapeDtypeStruct((t, d), BF16),
                   jax.ShapeDtypeStruct((1, d), F32), jax.ShapeDtypeStruct((1, d), F32)],
        in_specs=[pl.BlockSpec((tm, d), row), pl.BlockSpec((tm, d), row), pl.BlockSpec((1, d), fixed)],
        out_specs=[pl.BlockSpec((tm, d), row), pl.BlockSpec((tm, d), row), pl.BlockSpec((1, d), fixed),
                   pl.BlockSpec((1, d), fixed)],
        semantics=("arbitrary",),
    )(z, dy, ln_g)


CONV_ROWS = 32
SUBLANES = 8


def _fill_shifted(ext, shifted):
    rows = ext.shape[0] - SUBLANES
    for s in range(1, SUBLANES):
        for r in range(0, rows, CONV_ROWS):
            n = min(CONV_ROWS, rows - r)
            shifted[s - 1, r:r + n, :] = ext[r + s:r + s + n, :]


def _window(ext, shifted, lo, n):
    s = lo % SUBLANES
    return ext[lo:lo + n, :] if s == 0 else shifted[s - 1, lo - s:lo - s + n, :]


def _mixer_fwd(proj, conv_w, conv_b, cln_g, cln_b, sln_g, sln_b, sg_wm, sg_bb, name, exch=()):
    t = proj.shape[0]
    tm = _tile(t, 256, CHUNK)
    hb = tm // HALO
    nc = tm // CHUNK
    ch = CONV_CH

    def body(av_ref, ag_ref, bu_ref, bv_ref, hv_ref, hg_ref, cw_ref, cb_ref, lg_ref, lb_ref, sg_ref, sb_ref,
             w_ref, bb_ref, y_ref, yt_ref, c_ref, ext, ext_s):
        i = pl.program_id(0)
        halo = hv_ref[...] * _sigmoid(hg_ref[...])
        ext[0:HALO, :] = jnp.where(i > 0, halo, 0.0)
        ext[HALO:HALO + tm, :] = av_ref[...] * _sigmoid(ag_ref[...])
        _fill_shifted(ext, ext_s)
        for r in range(0, tm, CONV_ROWS):
            acc = jnp.zeros((CONV_ROWS, ch), F32) + cb_ref[...]
            for k in range(CONV_TAPS):
                lo = r + k + HALO - (CONV_TAPS - 1)
                acc = acc + cw_ref[k:k + 1, :] * _window(ext, ext_s, lo, CONV_ROWS)
            c_ref[r:r + CONV_ROWS, :] = acc
        a = _ln(c_ref[...], lg_ref[...], lb_ref[...])
        ya = a * _sigmoid(a)
        y_ref[:, 0:ch] = ya.astype(BF16)
        yt_ref[0:ch, :] = ya.T.astype(BF16)
        for h in range(HEADS):
            sl = slice(h * HEAD_DIM, (h + 1) * HEAD_DIM)
            u, _ = _gelu_and_grad(bu_ref[:, sl])
            v, _ = _gelu_and_grad(bv_ref[:, sl])
            vn = _ln(v, sg_ref[h:h + 1, :], sb_ref[h:h + 1, :])
            vn3 = vn.astype(BF16).reshape(nc, CHUNK, HEAD_DIM)
            wb = jnp.broadcast_to(w_ref[h][None], (nc, CHUNK, CHUNK))
            mixed = jnp.einsum("cts,csd->ctd", wb, vn3, preferred_element_type=F32) + bb_ref[h][None]
            yb = u * mixed.reshape(tm, HEAD_DIM)
            y_ref[:, ch + h * HEAD_DIM:ch + (h + 1) * HEAD_DIM] = yb.astype(BF16)
            yt_ref[ch + h * HEAD_DIM:ch + (h + 1) * HEAD_DIM, :] = yb.T.astype(BF16)

    col = lambda cidx: (lambda i: (i, cidx))
    prev = lambda cidx: (lambda i: (jnp.maximum(i * hb - 1, 0), cidx))
    fix2 = lambda i: (0, 0)
    fix3 = lambda i: (0, 0, 0)
    return _call(
        body, exch, name=name, grid=(t // tm,),
        out_shape=[jax.ShapeDtypeStruct((t, 2 * ch), BF16), jax.ShapeDtypeStruct((2 * ch, t), BF16),
                   jax.ShapeDtypeStruct((t, ch), F32)],
        in_specs=[pl.BlockSpec((tm, ch), col(0)), pl.BlockSpec((tm, ch), col(1)), pl.BlockSpec((tm, ch), col(2)),
                  pl.BlockSpec((tm, ch), col(3)), pl.BlockSpec((HALO, ch), prev(0)), pl.BlockSpec((HALO, ch), prev(1)),
                  pl.BlockSpec((CONV_TAPS, ch), fix2), pl.BlockSpec((1, ch), fix2), pl.BlockSpec((1, ch), fix2),
                  pl.BlockSpec((1, ch), fix2), pl.BlockSpec((HEADS, HEAD_DIM), fix2), pl.BlockSpec((HEADS, HEAD_DIM), fix2),
                  pl.BlockSpec((HEADS, CHUNK, CHUNK), fix3), pl.BlockSpec((HEADS, CHUNK, HEAD_DIM), fix3)],
        out_specs=[pl.BlockSpec((tm, 2 * ch), lambda i: (i, 0)), pl.BlockSpec((2 * ch, tm), lambda i: (0, i)),
                   pl.BlockSpec((tm, ch), lambda i: (i, 0))],
        scratch_shapes=[pltpu.VMEM((HALO + tm, ch), F32), pltpu.VMEM((SUBLANES - 1, HALO + tm, ch), F32)],
        semantics=("parallel",),
    )(proj, proj, proj, proj, proj, proj, conv_w, conv_b, cln_g, cln_b, sln_g, sln_b, sg_wm, sg_bb)


def _mixer_bwd(proj, conv_c, dy, conv_w, cln_g, cln_b, sln_g, sln_b, sg_wm, sg_wmt, sg_bb, name, exch=()):
    t = proj.shape[0]
    tm = _tile(t, 256, CHUNK)
    hb = tm // HALO
    nc = tm // CHUNK
    nt = t // tm
    ch = CONV_CH
    last_halo = t // HALO - 1

    def body(av_ref, ag_ref, bu_ref, bv_ref, hv_ref, hg_ref, c_ref, cn_ref, dya_ref, dyan_ref, dyb_ref,
             cw_ref, lg_ref, lb_ref, sg_ref, sb_ref, w_ref, wt_ref, bb_ref,
             dp_ref, dcw_ref, dcb_ref, dlg_ref, dlb_ref, dsg_ref, dsb_ref, dw_ref, dbs_ref,
             ext_h, ext_dc, ext_hs, ext_dcs, acc_cw):
        i = pl.program_id(0)

        @pl.when(i == 0)
        def _():
            acc_cw[...] = jnp.zeros_like(acc_cw)
            for ref in (dcb_ref, dlg_ref, dlb_ref, dsg_ref, dsb_ref, dw_ref, dbs_ref):
                ref[...] = jnp.zeros_like(ref)

        lg = lg_ref[...]
        lb = lb_ref[...]

        def conv_ln_bwd(c, dya):
            xh, rstd = _ln_stats(c)
            a = xh * lg + lb
            da = dya * _silu_grad(a)
            return _ln_bwd(da * lg, xh, rstd), da, xh

        fold = lambda v: jnp.sum(v.reshape(CONV_ROWS // SUBLANES, SUBLANES, ch), axis=0)
        s_lg = s_lb = s_cb = jnp.zeros((SUBLANES, ch), F32)
        for r in range(0, tm, CONV_ROWS):
            dc, da, xh = conv_ln_bwd(c_ref[r:r + CONV_ROWS, :], dya_ref[r:r + CONV_ROWS, :])
            ext_dc[r:r + CONV_ROWS, :] = dc
            s_lg, s_lb, s_cb = s_lg + fold(da * xh), s_lb + fold(da), s_cb + fold(dc)
        dlg_ref[...] += jnp.sum(s_lg, axis=0, keepdims=True)
        dlb_ref[...] += jnp.sum(s_lb, axis=0, keepdims=True)
        dcb_ref[...] += jnp.sum(s_cb, axis=0, keepdims=True)
        dcn, _, _ = conv_ln_bwd(cn_ref[...], dyan_ref[...])
        ext_dc[tm:tm + HALO, :] = jnp.where(i < nt - 1, dcn, 0.0)
        halo = hv_ref[...] * _sigmoid(hg_ref[...])
        ext_h[0:HALO, :] = jnp.where(i > 0, halo, 0.0)
        ext_h[HALO:HALO + tm, :] = av_ref[...] * _sigmoid(ag_ref[...])
        _fill_shifted(ext_h, ext_hs)
        _fill_shifted(ext_dc, ext_dcs)
        for r in range(0, tm, CONV_ROWS):
            dcr = ext_dc[r:r + CONV_ROWS, :]
            acc = jnp.zeros((CONV_ROWS, ch), F32)
            for k in range(CONV_TAPS):
                lo = r + k + HALO - (CONV_TAPS - 1)
                prod = dcr * _window(ext_h, ext_hs, lo, CONV_ROWS)
                acc_cw[k] += jnp.sum(prod.reshape(CONV_ROWS // 8, 8, ch), axis=0)
                hi = r + (CONV_TAPS - 1) - k
                acc = acc + cw_ref[k:k + 1, :] * _window(ext_dc, ext_dcs, hi, CONV_ROWS)
            sg_r = _sigmoid(ag_ref[r:r + CONV_ROWS, :])
            av_r = av_ref[r:r + CONV_ROWS, :]
            dp_ref[r:r + CONV_ROWS, 0:ch] = (acc * sg_r).astype(BF16)
            dp_ref[r:r + CONV_ROWS, ch:2 * ch] = (acc * av_r * sg_r * (1.0 - sg_r)).astype(BF16)

        @pl.when(i == nt - 1)
        def _():
            dcw_ref[...] = jnp.sum(acc_cw[...], axis=1)

        tril = (lax.broadcasted_iota(jnp.int32, (CHUNK, CHUNK), 0)
                >= lax.broadcasted_iota(jnp.int32, (CHUNK, CHUNK), 1)).astype(F32)
        for h in range(HEADS):
            sl = slice(h * HEAD_DIM, (h + 1) * HEAD_DIM)
            u, du_dx = _gelu_and_grad(bu_ref[:, sl])
            v, dv_dx = _gelu_and_grad(bv_ref[:, sl])
            xhv, rstdv = _ln_stats(v)
            gh = sg_ref[h:h + 1, :]
            vn3 = (xhv * gh + sb_ref[h:h + 1, :]).astype(BF16).reshape(nc, CHUNK, HEAD_DIM)
            wb = jnp.broadcast_to(w_ref[h][None], (nc, CHUNK, CHUNK))
            mixed = jnp.einsum("cts,csd->ctd", wb, vn3, preferred_element_type=F32) + bb_ref[h][None]
            dyb = dyb_ref[:, sl]
            d_u = dyb * mixed.reshape(tm, HEAD_DIM)
            dm = dyb * u
            dm3 = dm.reshape(nc, CHUNK, HEAD_DIM)
            dbs_ref[h:h + 1, :] += jnp.sum(jnp.sum(dm3, axis=0).T, axis=0, keepdims=True)
            dm3b = dm3.astype(BF16)
            dw_h = jnp.sum(jnp.einsum("ctd,csd->cts", dm3b, vn3, preferred_element_type=F32), axis=0)
            dw_ref[h] += dw_h * tril
            wtb = jnp.broadcast_to(wt_ref[h][None], (nc, CHUNK, CHUNK))
            d_vn = jnp.einsum("cst,ctd->csd", wtb, dm3b, preferred_element_type=F32).reshape(tm, HEAD_DIM)
            dsg_ref[h:h + 1, :] += jnp.sum(d_vn * xhv, axis=0, keepdims=True)
            dsb_ref[h:h + 1, :] += jnp.sum(d_vn, axis=0, keepdims=True)
            dv = _ln_bwd(d_vn * gh, xhv, rstdv)
            dp_ref[:, 2 * ch + h * HEAD_DIM:2 * ch + (h + 1) * HEAD_DIM] = (d_u * du_dx).astype(BF16)
            dp_ref[:, 3 * ch + h * HEAD_DIM:3 * ch + (h + 1) * HEAD_DIM] = (dv * dv_dx).astype(BF16)

    col = lambda cidx: (lambda i: (i, cidx))
    prev = lambda cidx: (lambda i: (jnp.maximum(i * hb - 1, 0), cidx))
    nxt = lambda i: (jnp.minimum((i + 1) * hb, last_halo), 0)
    fix2 = lambda i: (0, 0)
    fix3 = lambda i: (0, 0, 0)
    out_shape = [jax.ShapeDtypeStruct((t, 4 * ch), BF16), jax.ShapeDtypeStruct((CONV_TAPS, ch), F32),
                 jax.ShapeDtypeStruct((1, ch), F32), jax.ShapeDtypeStruct((1, ch), F32), jax.ShapeDtypeStruct((1, ch), F32),
                 jax.ShapeDtypeStruct((HEADS, HEAD_DIM), F32), jax.ShapeDtypeStruct((HEADS, HEAD_DIM), F32),
                 jax.ShapeDtypeStruct((HEADS, CHUNK, CHUNK), F32), jax.ShapeDtypeStruct((HEADS, CHUNK), F32)]
    out_specs = [pl.BlockSpec((tm, 4 * ch), lambda i: (i, 0)), pl.BlockSpec((CONV_TAPS, ch), fix2),
                 pl.BlockSpec((1, ch), fix2), pl.BlockSpec((1, ch), fix2), pl.BlockSpec((1, ch), fix2),
                 pl.BlockSpec((HEADS, HEAD_DIM), fix2), pl.BlockSpec((HEADS, HEAD_DIM), fix2),
                 pl.BlockSpec((HEADS, CHUNK, CHUNK), fix3), pl.BlockSpec((HEADS, CHUNK), fix2)]
    in_specs = [pl.BlockSpec((tm, ch), col(0)), pl.BlockSpec((tm, ch), col(1)), pl.BlockSpec((tm, ch), col(2)),
                pl.BlockSpec((tm, ch), col(3)), pl.BlockSpec((HALO, ch), prev(0)), pl.BlockSpec((HALO, ch), prev(1)),
                pl.BlockSpec((tm, ch), col(0)), pl.BlockSpec((HALO, ch), nxt),
                pl.BlockSpec((tm, ch), col(0)), pl.BlockSpec((HALO, ch), nxt), pl.BlockSpec((tm, ch), col(1)),
                pl.BlockSpec((CONV_TAPS, ch), fix2), pl.BlockSpec((1, ch), fix2), pl.BlockSpec((1, ch), fix2),
                pl.BlockSpec((HEADS, HEAD_DIM), fix2), pl.BlockSpec((HEADS, HEAD_DIM), fix2),
                pl.BlockSpec((HEADS, CHUNK, CHUNK), fix3), pl.BlockSpec((HEADS, CHUNK, CHUNK), fix3),
                pl.BlockSpec((HEADS, CHUNK, HEAD_DIM), fix3)]
    return _call(
        body, exch, name=name, grid=(nt,), out_shape=out_shape, in_specs=in_specs, out_specs=out_specs,
        scratch_shapes=[pltpu.VMEM((HALO + tm, ch), F32), pltpu.VMEM((tm + HALO, ch), F32),
                        pltpu.VMEM((SUBLANES - 1, HALO + tm, ch), F32), pltpu.VMEM((SUBLANES - 1, tm + HALO, ch), F32),
                        pltpu.VMEM((CONV_TAPS, 8, ch), F32)],
        semantics=("arbitrary",),
    )(proj, proj, proj, proj, proj, proj, conv_c, conv_c, dy, dy, dy,
      conv_w, cln_g, cln_b, sln_g, sln_b, sg_wm, sg_wmt, sg_bb)


def _pair_sum(parts, from_sibling, core_chip, name):
    _, r, cc = parts.shape
    tr = _tile(r, max(16, (1 << 20) // (2 * cc)), 16)

    def body(cc_ref, p_ref, s_ref, o_ref, own_ref):
        q = (p_ref[...].astype(F32) + s_ref[...].astype(F32)).astype(BF16)
        o_ref[...] = q

        @pl.when(pl.program_id(1) == cc_ref[1])
        def _():
            own_ref[...] = q[0]

    grid_spec = pltpu.PrefetchScalarGridSpec(
        num_scalar_prefetch=1, grid=(r // tr, 4),
        in_specs=[pl.BlockSpec((1, tr, cc), lambda i, j, cc_ref: (2 * j + cc_ref[0], i, 0)),
                  pl.BlockSpec((1, tr, cc), lambda i, j, cc_ref: (j, i, 0))],
        out_specs=[pl.BlockSpec((1, tr, cc), lambda i, j, cc_ref: (j, i, 0)),
                   pl.BlockSpec((tr, cc), lambda i, j, cc_ref: (i, 0))])
    return pl.pallas_call(
        body, name=name, grid_spec=grid_spec,
        out_shape=[jax.ShapeDtypeStruct((4, r, cc), BF16), jax.ShapeDtypeStruct((r, cc), BF16)],
        compiler_params=_cparams("parallel", "arbitrary"),
    )(core_chip, parts, from_sibling)


def _adamw_math(w, g, m, v):
    m = ADAM_B1 * m + (1.0 - ADAM_B1) * g
    v = ADAM_B2 * v + (1.0 - ADAM_B2) * (g * g)
    m_hat = m / (1.0 - ADAM_B1 ** ADAM_STEP)
    v_hat = v / (1.0 - ADAM_B2 ** ADAM_STEP)
    delta = -ADAM_LR * (m_hat / (jnp.sqrt(v_hat) + ADAM_EPS) + ADAM_WD * w)
    return delta, m, v


def _adamw_tile(in_refs, out_refs):
    w_ref, m_ref, v_ref, q_ref, o_ref = in_refs
    g = q_ref[...].astype(F32)
    for k in range(3):
        g = g + o_ref[k].astype(F32)
    d, mm, vv = _adamw_math(w_ref[...], g, m_ref[...], v_ref[...])
    for ref, val in zip(out_refs, (g, d, mm, vv)):
        ref[...] = val


def _adamw_side(w, m, v, chip_part, from_chips, max_tiles):
    r, cc = w.shape
    n = max(k for k in range(1, max_tiles + 1) if r % k == 0 and (r // k) % 16 == 0)
    tr = r // n
    row = ((tr, cc), lambda s: (s, 0))
    return _Side([w, m, v, chip_part, from_chips], [row, row, row, row, ((3, tr, cc), lambda s: (0, s, 0))],
                 [jax.ShapeDtypeStruct((r, cc), F32)] * 4, [row] * 4, n, _adamw_tile)


def _adamw_sharded(w, m, v, chip_part, from_chips, name):
    r, cc = w.shape
    tr = _tile(r, max(16, (1 << 19) // (4 * cc) * 2), 16)

    def body(*refs):
        _adamw_tile(refs[:5], refs[5:])

    row = pl.BlockSpec((tr, cc), lambda i: (i, 0))
    return pl.pallas_call(
        body, name=name, grid=(r // tr,), out_shape=[jax.ShapeDtypeStruct((r, cc), F32)] * 4,
        in_specs=[row, row, row, row, pl.BlockSpec((3, tr, cc), lambda i: (0, i, 0))], out_specs=[row] * 4,
        compiler_params=_cparams("parallel"),
    )(w, m, v, chip_part, from_chips)


def _adamw_small(w, g, m, v, name):
    r, cc = w.shape

    def body(w_ref, g_ref, m_ref, v_ref, d_out, m_out, v_out):
        d, mm, vv = _adamw_math(w_ref[...], g_ref[...], m_ref[...], v_ref[...])
        d_out[...] = d
        m_out[...] = mm
        v_out[...] = vv

    full = pl.BlockSpec((r, cc), lambda i: (0, 0))
    return pl.pallas_call(
        body, name=name, grid=(1,), out_shape=[jax.ShapeDtypeStruct((r, cc), F32)] * 3,
        in_specs=[full] * 4, out_specs=[full] * 3, compiler_params=_cparams("arbitrary"),
    )(w, g, m, v)


SMALL = ("ln1_g", "ln1_b", "conv_b", "conv_ln_g", "conv_ln_b", "sg_ln_g", "sg_ln_b", "sg_w", "sg_b",
         "ln2_g", "ln2_b", "ln3_g", "ln3_b")
ORDER = ("ffn1_w_gate_up", "ffn1_w_down", "ln1_g", "ln1_b", "mix_w_in", "conv_w", "conv_b", "conv_ln_g", "conv_ln_b",
         "sg_ln_g", "sg_ln_b", "sg_w", "sg_b", "mix_w_out", "ln2_g", "ln2_b", "ffn2_w_gate_up", "ffn2_w_down",
         "ln3_g", "ln3_b")


def _rows128(a):
    return a.reshape(-1, 128)


def kernel(x, ffn1_w_gate_up, ffn1_w_down, ln1_g, ln1_b, mix_w_in, conv_w, conv_b, conv_ln_g, conv_ln_b, sg_ln_g, sg_ln_b, sg_w, sg_b, mix_w_out, ln2_g, ln2_b, ffn2_w_gate_up, ffn2_w_down, ln3_g, ln3_b, loss_target, m_ffn1_w_gate_up, m_ffn1_w_down, m_ln1_g, m_ln1_b, m_mix_w_in, m_conv_w, m_conv_b, m_conv_ln_g, m_conv_ln_b, m_sg_ln_g, m_sg_ln_b, m_sg_w, m_sg_b, m_mix_w_out, m_ln2_g, m_ln2_b, m_ffn2_w_gate_up, m_ffn2_w_down, m_ln3_g, m_ln3_b, v_ffn1_w_gate_up, v_ffn1_w_down, v_ln1_g, v_ln1_b, v_mix_w_in, v_conv_w, v_conv_b, v_conv_ln_g, v_conv_ln_b, v_sg_ln_g, v_sg_ln_b, v_sg_w, v_sg_b, v_mix_w_out, v_ln2_g, v_ln2_b, v_ffn2_w_gate_up, v_ffn2_w_down, v_ln3_g, v_ln3_b):
    args = dict(locals())
    w = {n: args[n][0] for n in ORDER}
    mom = {n: args["m_" + n][0] for n in ORDER}
    var = {n: args["v_" + n][0] for n in ORDER}
    x0 = x[0]
    target = loss_target[0]
    t, d = x0.shape
    my_x, my_y, my_c = lax.axis_index("x"), lax.axis_index("y"), lax.axis_index("c")
    my_chip = (2 * my_x + my_y).astype(jnp.int32).reshape(1)
    my_core = my_c.astype(jnp.int32).reshape(1)
    me = 4 * my_x + 2 * my_y + my_c

    big = ("ffn1_w_gate_up", "ffn1_w_down", "mix_w_in", "mix_w_out", "ffn2_w_gate_up", "ffn2_w_down")
    sh = {n: w[n].astype(BF16) for n in big}
    f2s = sh["ffn2_w_gate_up"].shape[1]
    order = jnp.stack([4 * p[0] + 2 * p[1] + p[2] for p in _visit_order(my_x, my_y, my_c)]).astype(jnp.int32)
    gu1, x0t, (wgu1, wd1, conv_w_all) = _gather_and_gate_up(
        x0, [sh["ffn1_w_gate_up"], sh["ffn1_w_down"], w["conv_w"]], [True, True, False], order, "ffn1_gate_up_fwd")
    wd1 = wd1.reshape(-1, d)
    conv_w_full = jnp.transpose(conv_w_all, (1, 0, 2)).reshape(CONV_TAPS, CONV_CH)
    tril = jnp.tril(jnp.ones((CHUNK, CHUNK), F32))
    sg_wm = w["sg_w"] * tril
    sg_wm_b = sg_wm.astype(BF16)
    sg_wmt_b = jnp.swapaxes(sg_wm, 1, 2).astype(BF16)
    sg_bb = jnp.broadcast_to(w["sg_b"][:, :, None], (HEADS, CHUNK, HEAD_DIM))
    row = lambda a: a.reshape(1, -1)

    d2 = [sh["ffn2_w_down"]]
    d2_first = d2[0].shape[0] * 3 // 5 // 16 * 16
    d2_top, d2_bottom = (0, d2_first), (d2_first, d2[0].shape[0] - d2_first)
    (h1t, z1, x1), ((g_in, g_out), (g_d2,)) = _ffn_down_fwd(
        gu1, x0, wd1, row(w["ln1_g"]), row(w["ln1_b"]), "ffn1_down_fwd",
        exch=[_gather_first([sh["mix_w_in"], sh["mix_w_out"]], [True, False]),
              _gather_first(d2, [False], rows=d2_top)])
    in_cols = sh["mix_w_in"].shape[1]
    x1t, ((w_in, w_out), (g_d2,)) = _transpose_bf16(
        x1, "x1_transpose", exch=[_gather_forward([g_in, g_out], [True, False], [in_cols, None]),
                                  _gather_forward([g_d2], [False], [None], rows=d2_top)])
    w_out = w_out.reshape(-1, d)
    top, bottom = (0, d // 2), (d // 2, d // 2)
    gu2 = [sh["ffn2_w_gate_up"]]
    proj, ((g_gu2,),) = _mix_in_proj(x1, w_in, "mix_in_fwd", exch=[_gather_first(gu2, [True], rows=top)])
    (y, yt, conv_c), ((g_gu2,),) = _mixer_fwd(
        proj, conv_w_full, row(w["conv_b"]), row(w["conv_ln_g"]), row(w["conv_ln_b"]),
        w["sg_ln_g"], w["sg_ln_b"], sg_wm_b, sg_bb, "mixer_fwd",
        exch=[_both(_gather_first(gu2, [True], rows=bottom, into=[g_gu2]),
                    _gather_forward([g_gu2], [True], [f2s], rows=top))])
    (z2, x2, x2t), ((wgu2,), (g_d2,)) = _mix_out_fwd(
        y, w_out, x1, row(w["ln2_g"]), row(w["ln2_b"]), "mix_out_fwd",
        exch=[_gather_forward([g_gu2], [True], [f2s], rows=bottom),
              _gather_first(d2, [False], rows=d2_bottom, into=[g_d2])])
    (wd2,) = _exchange_alone(_gather_forward([g_d2], [False], [None], rows=d2_bottom), "ffn2_down_gather_forward")
    wd2 = wd2.reshape(-1, d)
    grads = {}
    (g2, u2, h2t, dz3, do2, grads["ln3_g"], grads["ln3_b"], loss_tile), _ = _ffn_fwd_loss(
        x2, wgu2, wd2, row(w["ln3_g"]), row(w["ln3_b"]), target, "ffn2_fwd_loss")

    f = wd1.shape[0]
    dn = _tile(d, 1024, 128)
    core_chip = jnp.concatenate([my_core, my_chip])
    pair = lambda p, s, label: _pair_sum(p, s, core_chip, "pair_sum_" + label)
    adamw = lambda n, own, got, steps: _adamw_side(w[n], mom[n], var[n], own, got, steps)
    m_tiles = d // _tile(d, 512, 16)
    gu_first = d * 2 // 3 // 16 * 16
    out = {}
    p_d2, _ = _weight_grad(h2t, do2, dn, 512, "ffn2_dw_down")
    p_d2 = p_d2.reshape(N_DEV, f // N_DEV, d)
    (dg2, du2, dx2), ((s_d2,),) = _ffn_bwd(dz3, do2, g2, u2, wgu2, wd2, "ffn2_bwd", exch=[_rs_sibling([p_d2])])
    q_d2, own_d2 = pair(p_d2, s_d2, "ffn2_down")
    d_rows = q_d2.shape[1]
    d_half = d_rows // 2 // 16 * 16
    p_gu2, ((r_d2,),) = _weight_grad(x2t, dg2, f2s, 512, "ffn2_dw_gate", blocks=N_DEV,
                                     exch=[_rs_chips([q_d2], rows=(0, d_half))])
    p_gu2, ((r_d2,),) = _weight_grad(x2t, du2, f2s, 512, "ffn2_dw_up", blocks=N_DEV, block_offset=4, into=p_gu2,
                                     exch=[_rs_chips([q_d2], rows=(d_half, d_rows - d_half), into=[r_d2])])
    (dz2, dz2b, grads["ln2_g"], grads["ln2_b"]), ((s_gu2,),) = _ln_bwd_call(
        z2, dx2, row(w["ln2_g"]), 1.0, "ln2_bwd", exch=[_rs_sibling([p_gu2])])
    q_gu2, own_gu2 = pair(p_gu2, s_gu2, "ffn2_gate_up")
    dy, _ = _mix_out_bwd(dz2b, w_out, "mix_out_bwd")
    p_out, _ = _weight_grad(yt, dz2b, dn, 512, "mix_out_dw")
    p_out = p_out.reshape(N_DEV, -1, d)
    (dproj, grads["conv_w"], grads["conv_b"], grads["conv_ln_g"], grads["conv_ln_b"], grads["sg_ln_g"],
     grads["sg_ln_b"], grads["sg_w"], grads["sg_b"]), ((r_gu2,),) = _mixer_bwd(
        proj, conv_c, dy, conv_w_full, row(w["conv_ln_g"]), row(w["conv_ln_b"]), w["sg_ln_g"], w["sg_ln_b"],
        sg_wm_b, sg_wmt_b, sg_bb, "mixer_bwd", exch=[_rs_chips([q_gu2], rows=(0, gu_first))])
    dx1, ((s_out,), (r_gu2,)) = _mix_in_bwd(
        dproj, w_in, dz2, "mix_in_bwd",
        exch=[_rs_sibling([p_out]), _rs_chips([q_gu2], rows=(gu_first, d - gu_first), into=[r_gu2])])
    p_in, (out["ffn2_w_gate_up"], out["ffn2_w_down"]) = _weight_grad(
        x1t, dproj, in_cols, 512, "mix_in_dw", blocks=N_DEV,
        exch=[adamw("ffn2_w_gate_up", own_gu2, r_gu2, N_DEV * m_tiles), adamw("ffn2_w_down", own_d2, r_d2, N_DEV * m_tiles)])
    (dz1, do1, grads["ln1_g"], grads["ln1_b"]), ((s_in,),) = _ln_bwd_call(
        z1, dx1, row(w["ln1_g"]), 0.5, "ln1_bwd", exch=[_rs_sibling([p_in])])
    q_out, own_out = pair(p_out, s_out, "mix_out")
    q_in, own_in = pair(p_in, s_in, "mix_in")
    small_parts = [_rows128(grads[n]) for n in SMALL]
    packed = jnp.concatenate(small_parts + [_rows128(grads["conv_w"]), loss_tile], axis=0)
    p_d1, ((r_in,),) = _weight_grad(h1t, do1, dn, 512, "ffn1_dw_down", exch=[_rs_chips([q_in])])
    p_d1 = p_d1.reshape(N_DEV, f // N_DEV, d)
    (dg1, du1), ((s_d1,), (r_out,), (small_all,)) = _ffn_bwd_act(
        do1, gu1, wd1, "ffn1_bwd_act",
        exch=[_rs_sibling([p_d1]), _rs_chips([q_out]), _small_gather(packed)])
    q_d1, own_d1 = pair(p_d1, s_d1, "ffn1_down")
    p_gu1, ((r_d1,),) = _weight_grad(x0t, dg1, f2s, 512, "ffn1_dw_gate", blocks=N_DEV, exch=[_rs_chips([q_d1])])
    p_gu1, (out["mix_w_in"], out["mix_w_out"]) = _weight_grad(
        x0t, du1, f2s, 512, "ffn1_dw_up", blocks=N_DEV, block_offset=4, into=p_gu1,
        exch=[adamw("mix_w_in", own_in, r_in, 4 * m_tiles), adamw("mix_w_out", own_out, r_out, 4 * m_tiles)])
    (s_gu1,) = _exchange_alone(_rs_sibling([p_gu1]), "ffn1_gate_up_sibling_exchange")
    q_gu1, own_gu1 = pair(p_gu1, s_gu1, "ffn1_gate_up")
    (grad_x,), ((r_gu1,),) = _ffn_bwd_dx(dz1, dg1, du1, wgu1, "ffn1_bwd_dx", exch=[_rs_chips([q_gu1])])
    for n, own, got in (("ffn1_w_down", own_d1, r_d1), ("ffn1_w_gate_up", own_gu1, r_gu1)):
        out[n] = _adamw_sharded(w[n], mom[n], var[n], own, got, "adamw_" + n)

    cw_rows = CONV_TAPS * CONV_CH // 128
    total = _sum_over_devices(small_all)
    offs = [0]
    for p in small_parts:
        offs.append(offs[-1] + p.shape[0])
    n_small = offs[-1]
    loss = total[n_small + cw_rows, 0]
    g_conv_w = lax.dynamic_slice_in_dim(total[n_small:n_small + cw_rows].reshape(CONV_TAPS, CONV_CH),
                                        me * (CONV_CH // N_DEV), CONV_CH // N_DEV, axis=1)
    pad8 = lambda a: jnp.pad(a, ((0, -a.shape[0] % 8), (0, 0)))
    pack = lambda tree, cw: jnp.concatenate([_rows128(tree[n]) for n in SMALL] + [pad8(cw)], axis=0)
    g_pack = jnp.concatenate([total[:n_small], pad8(g_conv_w)], axis=0)
    d_pack, m_pack, v_pack = _adamw_small(pack(w, w["conv_w"]), g_pack, pack(mom, mom["conv_w"]),
                                          pack(var, var["conv_w"]), "adamw_small")
    for k, n in enumerate(SMALL):
        sl = slice(offs[k], offs[k + 1])
        shp = w[n].shape
        out[n] = (total[sl].reshape(shp), d_pack[sl].reshape(shp), m_pack[sl].reshape(shp), v_pack[sl].reshape(shp))
    sl = slice(n_small, n_small + CONV_TAPS)
    out["conv_w"] = (g_conv_w, d_pack[sl], m_pack[sl], v_pack[sl])

    lead = lambda a: a[None]
    res = [loss, grad_x[None]]
    for kind in range(4):
        res += [lead(out[n][kind]) for n in ORDER]
    return tuple(res)
```

```python
import functools
import math

import jax
import jax.numpy as jnp
from jax import lax
from jax.experimental import pallas as pl
from jax.experimental.pallas import tpu as pltpu

F32, BF16 = jnp.float32, jnp.bfloat16
MESH = pl.DeviceIdType.MESH
ANY = pl.BlockSpec(memory_space=pl.ANY)

N_DEV = 8
LN_EPS = 1e-5
ALPHA = 2.0 ** 0.25
CONV_CH = 1024
CONV_TAPS = 31
HALO = 32
HEADS = 8
HEAD_DIM = 128
CHUNK = 128
ADAM_LR, ADAM_B1, ADAM_B2, ADAM_EPS, ADAM_WD, ADAM_STEP = 0.001, 0.9, 0.999, 1e-08, 0.01, 10
V7X_VMEM_LIMIT = 62 * 2 ** 20
EPILOGUE_ROWS = 128

def _cparams(*sem):
    return pltpu.CompilerParams(dimension_semantics=sem, vmem_limit_bytes=V7X_VMEM_LIMIT)


def _tile(n, pref, mult):
    best = None
    for t in range(mult, min(n, pref) + 1, mult):
        if n % t == 0:
            best = t
    return best if best is not None else n


def _dot(a, b):
    return jnp.dot(a, b, preferred_element_type=F32)


def _dot_nt(a, b):
    return lax.dot_general(a, b, (((1,), (1,)), ((), ())), preferred_element_type=F32)


def _sigmoid(x):
    return 1.0 / (1.0 + jnp.exp(-x))


def _ln_stats(z):
    mu = jnp.mean(z, axis=-1, keepdims=True)
    zc = z - mu
    var = jnp.mean(zc * zc, axis=-1, keepdims=True)
    rstd = lax.rsqrt(var + LN_EPS)
    return zc * rstd, rstd


def _ln(z, g, b):
    xh, _ = _ln_stats(z)
    return xh * g + b


def _ln_bwd(dxh, xh, rstd):
    m1 = jnp.mean(dxh, axis=-1, keepdims=True)
    m2 = jnp.mean(dxh * xh, axis=-1, keepdims=True)
    return rstd * (dxh - m1 - xh * m2)


_GK = math.sqrt(2.0 / math.pi)
_GA = 0.044715


def _gelu_and_grad(x):
    x2 = x * x
    t = jnp.tanh(_GK * (x + _GA * x * x2))
    y = 0.5 * x * (1.0 + t)
    dy = 0.5 * (1.0 + t) + 0.5 * x * (1.0 - t * t) * (_GK * (1.0 + 3.0 * _GA * x2))
    return y, dy


def _silu_grad(a):
    s = _sigmoid(a)
    return s * (1.0 + a * (1.0 - s))


def _place():
    return lax.axis_index("x"), lax.axis_index("y"), lax.axis_index("c")


def _other_chips(x, y):
    return [(1 - x, y), (x, 1 - y), (1 - x, 1 - y)]


def _visit_order(x, y, c):
    chips = _other_chips(x, y)
    return [(x, y, c), (x, y, 1 - c), (*chips[0], c), (*chips[1], c), (*chips[0], 1 - c), (*chips[1], 1 - c),
            (*chips[2], c), (*chips[2], 1 - c)]


def _gather_and_gate_up(xb, shards, relayed, order, name):
    n = len(shards)
    N_COPIES = 10
    t, d = xb.shape
    cols = shards[0].shape[1]
    tm = _tile(t, 1024, 128)
    ni = t // tm
    col_major = [True] + [False] * (n - 1)

    def body(order_ref, x_ref, *refs):
        srcs, gu_ref, xt_ref, dsts = refs[:n], refs[n], refs[n + 1], refs[n + 2:2 * n + 2]
        wbuf, send_sems, recv_sems, local_sems, load_sem = refs[2 * n + 2:]
        b, i = pl.program_id(0), pl.program_id(1)
        x, y, c = _place()
        me, sib = (x, y, c), (x, y, 1 - c)
        chips = _other_chips(x, y)

        near_x, near_y, far = chips

        def slot(w, p, band=None):
            half = shards[w].shape[0] // 2
            rows = None if band is None else (band * half, half)
            return _block_slot(dsts[w], col_major[w], shards[w].shape[1], p, rows)

        def copy(w, s, block, to, band=None, from_src=False):
            return pltpu.make_async_remote_copy(
                src_ref=srcs[w] if from_src else slot(w, block, band), dst_ref=slot(w, block, band),
                send_sem=send_sems.at[N_COPIES * w + s], recv_sem=recv_sems.at[N_COPIES * w + s],
                device_id=to, device_id_type=MESH)

        def own(w):
            return pltpu.make_async_copy(srcs[w], slot(w, me), local_sems.at[w])

        def sends(w):
            out = [copy(w, 0, me, sib, from_src=True), copy(w, 1, me, (*near_x, c), from_src=True),
                   copy(w, 2, me, (*near_y, c), from_src=True)]
            if not relayed[w]:
                out.append(copy(w, 3, me, (*far, c), from_src=True))
            return out

        def passed_on(w):
            out = [copy(w, 4, (*near_x, c), sib), copy(w, 5, (*near_y, c), sib)]
            if relayed[w]:
                out += [copy(w, 6, (*far, c), sib, band=0), copy(w, 9, (*far, c), sib, band=1),
                        copy(w, 7, (*near_x, c), (*near_y, c), band=0), copy(w, 8, (*near_y, c), (*near_x, c), band=1)]
            else:
                out.append(copy(w, 6, (*far, c), sib))
            return out

        def start_sends(w):
            own(w).start()
            for cp in sends(w):
                cp.start()

        def got_near_x(w):
            copy(w, 1, (*near_x, c), me).wait_recv()
            copy(w, 4, (*near_x, c), sib).start()
            if relayed[w]:
                copy(w, 7, (*near_x, c), (*near_y, c), band=0).start()

        def got_near_y(w):
            copy(w, 2, (*near_y, c), me).wait_recv()
            copy(w, 5, (*near_y, c), sib).start()
            if relayed[w]:
                copy(w, 8, (*near_y, c), (*near_x, c), band=1).start()

        def got_far(w):
            if relayed[w]:
                copy(w, 7, (*far, c), me, band=0).wait_recv()
                copy(w, 6, (*far, c), sib, band=0).start()
                copy(w, 8, (*far, c), me, band=1).wait_recv()
                copy(w, 9, (*far, c), sib, band=1).start()
            else:
                copy(w, 3, (*far, c), me).wait_recv()
                copy(w, 6, (*far, c), sib).start()

        def got_from_sibling(w, which):
            if which == 0:
                copy(w, 0, sib, me).wait_recv()
            elif which == 3 and relayed[w]:
                copy(w, 6, (*far, 1 - c), me, band=0).wait_recv()
                copy(w, 9, (*far, 1 - c), me, band=1).wait_recv()
            else:
                copy(w, 3 + which, (*chips[which - 1], 1 - c), me).wait_recv()

        others = range(1, n)

        def arrive(k):
            if k == 0:
                own(0).wait()
            elif k == 1:
                got_from_sibling(0, 0)
            elif k == 2:
                got_near_x(0)
                for w in others:
                    start_sends(w)
            elif k == 3:
                got_near_y(0)
            elif k in (4, 5):
                got_from_sibling(0, k - 3)
            elif k == 6:
                got_far(0)
                for w in others:
                    got_near_x(w)
                    got_near_y(w)
            else:
                got_from_sibling(0, 3)
                for w in others:
                    got_far(w)

        def load(k):
            at = pl.multiple_of(order_ref[k] * cols, 128)
            return pltpu.make_async_copy(dsts[0].at[:, pl.ds(at, cols)], wbuf.at[k % 2], load_sem.at[k % 2])

        @pl.when((b == 0) & (i == 0))
        def _():
            start_sends(0)
            arrive(0)
            load(0).start()
            load(0).wait()

        early = ni - 1
        for k in range(1, N_DEV):
            @pl.when((b == k - 1) & (i == early))
            def _(k=k):
                arrive(k)
                load(k).start()

            @pl.when((b == k) & (i == 0))
            def _(k=k):
                load(k).wait()

        gu_ref[...] = _dot(x_ref[...].astype(BF16), wbuf[b % 2]).astype(BF16)

        @pl.when(b == 0)
        def _():
            xt_ref[...] = x_ref[...].T.astype(BF16)

        @pl.when((b == N_DEV - 1) & (i == ni - 1))
        def _():
            for w in others:
                for which in range(4):
                    got_from_sibling(w, which)
                own(w).wait()
            for w in range(n):
                for cp in sends(w) + passed_on(w):
                    cp.wait_send()

    grid_spec = pltpu.PrefetchScalarGridSpec(
        num_scalar_prefetch=1, grid=(N_DEV, ni),
        in_specs=[pl.BlockSpec((tm, d), lambda b, i, o: (i, 0))] + [ANY] * n,
        out_specs=[pl.BlockSpec((tm, cols), lambda b, i, o: (i, o[b])),
                   pl.BlockSpec((d, tm), lambda b, i, o: (0, jnp.where(b == 0, i, ni - 1)))] + [ANY] * n,
        scratch_shapes=[pltpu.VMEM((2, d, cols), BF16), pltpu.SemaphoreType.DMA((N_COPIES * n,)),
                        pltpu.SemaphoreType.DMA((N_COPIES * n,)), pltpu.SemaphoreType.DMA((n,)),
                        pltpu.SemaphoreType.DMA((2,))])
    res = pl.pallas_call(
        body, name=name, grid_spec=grid_spec,
        out_shape=[jax.ShapeDtypeStruct((t, N_DEV * cols), BF16), jax.ShapeDtypeStruct((d, t), BF16)]
        + [_gathered_shape(s, cm) for s, cm in zip(shards, col_major)],
        compiler_params=_cparams("arbitrary", "arbitrary"),
    )(order, xb, *shards)
    return res[0], res[1], res[2:]


class _Exchange:
    def __init__(self, ins, io, new, n_sems, n_local, make):
        self.ins, self.io, self.new = list(ins), list(io), list(new)
        self.n_sems, self.n_local, self.make = n_sems, n_local, make


def _block_slot(ref, col_major, cols, place, rows=None):
    k = 4 * place[0] + 2 * place[1] + place[2]
    band = slice(None) if rows is None else pl.ds(rows[0], rows[1])
    if col_major:
        return ref.at[band, pl.ds(pl.multiple_of(k * cols, 128), cols)]
    return ref.at[k] if rows is None else ref.at[k, band]


def _gathered_shape(s, col_major):
    return jax.ShapeDtypeStruct((s.shape[0], N_DEV * s.shape[1]) if col_major else (N_DEV,) + s.shape, s.dtype)


def _gather_first(shards, col_major, rows=None, into=None):
    n = len(shards)
    new = [] if into is not None else [_gathered_shape(s, cm) for s, cm in zip(shards, col_major)]

    def make(in_refs, io_refs, new_refs, send_sems, recv_sems, local_sems, base=0, local_base=0):
        x, y, c = _place()
        targets = [(x, y, 1 - c)] + [(*chip, c) for chip in _other_chips(x, y)]
        gathered = io_refs if into is not None else new_refs
        copies = []
        for w in range(n):
            src = in_refs[w] if rows is None else in_refs[w].at[pl.ds(rows[0], rows[1])]
            slot = _block_slot(gathered[w], col_major[w], shards[w].shape[1], (x, y, c), rows)
            copies.append(pltpu.make_async_copy(src, slot, local_sems.at[local_base + w]))
            for s, to in enumerate(targets):
                copies.append(pltpu.make_async_remote_copy(
                    src_ref=src, dst_ref=slot, send_sem=send_sems.at[base + 4 * w + s],
                    recv_sem=recv_sems.at[base + 4 * w + s], device_id=to, device_id_type=MESH))
        return copies

    return _Exchange(shards, into or [], new, 4 * n, n, make)


def _gather_forward(gathered, col_major, cols, rows=None):
    n = len(gathered)

    def make(in_refs, io_refs, new_refs, send_sems, recv_sems, local_sems, base=0, local_base=0):
        x, y, c = _place()
        copies = []
        for w in range(n):
            for j, chip in enumerate(_other_chips(x, y)):
                slot = _block_slot(io_refs[w], col_major[w], cols[w], (*chip, c), rows)
                copies.append(pltpu.make_async_remote_copy(
                    src_ref=slot, dst_ref=slot, send_sem=send_sems.at[base + 3 * w + j],
                    recv_sem=recv_sems.at[base + 3 * w + j], device_id=(x, y, 1 - c), device_id_type=MESH))
        return copies

    return _Exchange([], gathered, [], 3 * n, 0, make)


def _both(a, b):
    def make(in_refs, io_refs, new_refs, send_sems, recv_sems, local_sems):
        na = len(a.ins)
        return (a.make(in_refs[:na], io_refs, [], send_sems, recv_sems, local_sems, 0, 0)
                + b.make(in_refs[na:], io_refs, [], send_sems, recv_sems, local_sems, a.n_sems, a.n_local))

    return _Exchange(a.ins + b.ins, a.io, [], a.n_sems + b.n_sems, a.n_local + b.n_local, make)


def _rs_sibling(parts):
    n = len(parts)

    def make(in_refs, io_refs, new_refs, send_sems, recv_sems, local_sems):
        x, y, c = _place()
        copies = []
        for w in range(n):
            for j in range(4):
                copies.append(pltpu.make_async_remote_copy(
                    src_ref=in_refs[w].at[2 * j + (1 - c)], dst_ref=new_refs[w].at[j],
                    send_sem=send_sems.at[4 * w + j], recv_sem=recv_sems.at[4 * w + j],
                    device_id=(x, y, 1 - c), device_id_type=MESH))
        return copies

    return _Exchange(parts, [], [jax.ShapeDtypeStruct((4,) + p.shape[1:], p.dtype) for p in parts], 4 * n, 0, make)


def _rs_chips(chip_parts, rows=None, into=None):
    n = len(chip_parts)
    band = slice(None) if rows is None else pl.ds(rows[0], rows[1])
    new = [] if into is not None else [jax.ShapeDtypeStruct((3,) + p.shape[1:], p.dtype) for p in chip_parts]

    def make(in_refs, io_refs, new_refs, send_sems, recv_sems, local_sems):
        x, y, c = _place()
        landing = io_refs if into is not None else new_refs
        copies = []
        for w in range(n):
            for rel, (px, py) in enumerate(_other_chips(x, y)):
                copies.append(pltpu.make_async_remote_copy(
                    src_ref=in_refs[w].at[2 * px + py, band], dst_ref=landing[w].at[rel, band],
                    send_sem=send_sems.at[3 * w + rel], recv_sem=recv_sems.at[3 * w + rel],
                    device_id=(px, py, c), device_id_type=MESH))
        return copies

    return _Exchange(chip_parts, into or [], new, 3 * n, 0, make)


class _Side:
    def __init__(self, ins, in_blocks, out_shapes, out_blocks, n_tiles, fn):
        self.ins, self.in_blocks, self.out_shapes, self.out_blocks = list(ins), in_blocks, list(out_shapes), out_blocks
        self.n_tiles, self.fn = n_tiles, fn


def _call(body, exch, *, name, grid, in_specs, out_specs, out_shape, scratch_shapes=(), semantics,
          input_output_aliases=None):
    exch = list(exch)
    in_specs, out_specs, out_shape = list(in_specs), list(out_specs), list(out_shape)
    scratch_shapes = list(scratch_shapes)
    if not exch:
        fn = pl.pallas_call(body, name=name, grid=grid, in_specs=in_specs, out_specs=out_specs, out_shape=out_shape,
                            scratch_shapes=scratch_shapes, input_output_aliases=input_output_aliases or {},
                            compiler_params=_cparams(*semantics))
        return lambda *args: (fn(*args), [])
    n_in, n_out, n_scr = len(in_specs), len(out_specs), len(scratch_shapes)
    aliases = dict(input_output_aliases or {})
    all_in, all_out_specs, all_out_shape, all_scr = list(in_specs), list(out_specs), list(out_shape), list(scratch_shapes)
    extra_args = []

    def step(idx):
        s = idx[0]
        for a in range(1, len(grid)):
            s = s * grid[a] + idx[a]
        return s

    def tile_spec(shape, where, n_tiles):
        return pl.BlockSpec(shape, lambda *idx: where(jnp.minimum(step(idx), n_tiles - 1)))

    for ex in exch:
        if isinstance(ex, _Side):
            all_in += [tile_spec(shape, where, ex.n_tiles) for shape, where in ex.in_blocks]
            extra_args += ex.ins
            all_out_specs += [tile_spec(shape, where, ex.n_tiles) for shape, where in ex.out_blocks]
            all_out_shape += ex.out_shapes
            continue
        for k, a in enumerate(ex.io):
            aliases[len(all_in) + len(ex.ins) + k] = len(all_out_specs) + k
        all_in += [ANY] * (len(ex.ins) + len(ex.io))
        extra_args += ex.ins + ex.io
        all_out_specs += [ANY] * (len(ex.io) + len(ex.new))
        all_out_shape += [jax.ShapeDtypeStruct(a.shape, a.dtype) for a in ex.io] + ex.new
        all_scr += [pltpu.SemaphoreType.DMA((ex.n_sems,)), pltpu.SemaphoreType.DMA((ex.n_sems,)),
                    pltpu.SemaphoreType.DMA((max(ex.n_local, 1),))]

    n_ins = [len(ex.ins) if isinstance(ex, _Side) else len(ex.ins) + len(ex.io) for ex in exch]
    n_outs = [len(ex.out_shapes) if isinstance(ex, _Side) else len(ex.io) + len(ex.new) for ex in exch]

    def wrapped(*refs):
        pos = n_in
        ex_in = []
        for k in n_ins:
            ex_in.append(refs[pos:pos + k])
            pos += k
        outs = refs[pos:pos + n_out]
        pos += n_out
        ex_out = []
        for k in n_outs:
            ex_out.append(refs[pos:pos + k])
            pos += k
        scr = refs[pos:pos + n_scr]
        pos += n_scr
        idx = [pl.program_id(a) for a in range(len(grid))]
        first = functools.reduce(jnp.logical_and, [i == 0 for i in idx])
        last = functools.reduce(jnp.logical_and, [i == g - 1 for i, g in zip(idx, grid)])

        def copies():
            out, at = [], pos
            for ex, ei, eo in zip(exch, ex_in, ex_out):
                if not isinstance(ex, _Side):
                    out += ex.make(ei[:len(ex.ins)], eo[:len(ex.io)], eo[len(ex.io):], *refs[at:at + 3])
                    at += 3
            return out

        @pl.when(first)
        def _():
            for cp in copies():
                cp.start()

        body(*refs[:n_in], *outs, *scr)
        for ex, ei, eo in zip(exch, ex_in, ex_out):
            if isinstance(ex, _Side):
                pl.when(step(idx) < ex.n_tiles)(functools.partial(ex.fn, ei, eo))

        @pl.when(last)
        def _():
            for cp in copies():
                cp.wait()

    fn = pl.pallas_call(wrapped, name=name, grid=grid, in_specs=all_in, out_specs=all_out_specs,
                        out_shape=all_out_shape, scratch_shapes=all_scr, input_output_aliases=aliases,
                        compiler_params=_cparams(*(["arbitrary"] * len(grid))))

    def run(*args):
        res = fn(*args, *extra_args)
        outs, pos, ex_res = res[:n_out], n_out, []
        for k in n_outs:
            ex_res.append(list(res[pos:pos + k]))
            pos += k
        return outs, ex_res

    return run


def _exchange_alone(ex, name):
    def body():
        pass

    _, res = _call(body, [ex], name=name, grid=(1,), in_specs=[], out_specs=[], out_shape=[], semantics=("arbitrary",))()
    return res[0]


def _small_gather(part):
    def make(in_refs, io_refs, new_refs, send_sems, recv_sems, local_sems):
        x, y, c = _place()
        slot = new_refs[0].at[4 * x + 2 * y + c]
        copies = [pltpu.make_async_copy(in_refs[0], slot, local_sems.at[0])]
        for d in range(1, N_DEV):
            peer = (1 - x if d & 4 else x, 1 - y if d & 2 else y, 1 - c if d & 1 else c)
            copies.append(pltpu.make_async_remote_copy(
                src_ref=in_refs[0], dst_ref=slot, send_sem=send_sems.at[d - 1], recv_sem=recv_sems.at[d - 1],
                device_id=peer, device_id_type=MESH))
        return copies

    return _Exchange([part], [], [jax.ShapeDtypeStruct((N_DEV,) + part.shape, part.dtype)], N_DEV - 1, 1, make)


def _sum_over_devices(parts):
    _, rows, lanes = parts.shape

    def body(p_ref, o_ref):
        acc = p_ref[0]
        for k in range(1, N_DEV):
            acc = acc + p_ref[k]
        o_ref[...] = acc

    return pl.pallas_call(
        body, name="small_grads_sum", grid=(1,), out_shape=jax.ShapeDtypeStruct((rows, lanes), F32),
        in_specs=[pl.BlockSpec((N_DEV, rows, lanes), lambda i: (0, 0, 0))],
        out_specs=pl.BlockSpec((rows, lanes), lambda i: (0, 0)),
        compiler_params=_cparams("arbitrary"),
    )(parts)


def _transpose_bf16(a, name, exch=(), with_copy=False):
    r, c = a.shape
    tr, tc = _tile(r, 512, 128), _tile(c, 512, 128)

    def body(a_ref, o_ref, *copy_ref):
        v = a_ref[...].astype(F32)
        o_ref[...] = v.T.astype(BF16)
        if with_copy:
            copy_ref[0][...] = v.astype(BF16)

    outs, ex = _call(
        body, exch, name=name, grid=(r // tr, c // tc),
        out_shape=[jax.ShapeDtypeStruct((c, r), BF16)] + [jax.ShapeDtypeStruct((r, c), BF16)] * with_copy,
        in_specs=[pl.BlockSpec((tr, tc), lambda i, j: (i, j))],
        out_specs=[pl.BlockSpec((tc, tr), lambda i, j: (j, i))] + [pl.BlockSpec((tr, tc), lambda i, j: (i, j))] * with_copy,
        semantics=("parallel", "parallel"),
    )(a)
    return (outs if with_copy else outs[0]), ex


def _ffn_fwd_loss(x, wgu, wd, ln_g, ln_b, target, name, exch=()):
    t, d = x.shape
    f = wd.shape[0]
    tm, tf = _tile(t, 512, 128), _tile(f, 512, 128)
    nf = f // tf

    def body(x_ref, wg_ref, wu_ref, wd_ref, lg_ref, lb_ref, t_ref,
             go_ref, uo_ref, ht_ref, dz_ref, dzb_ref, dlg_ref, dlb_ref, loss_ref, xb, acc):
        i, j = pl.program_id(0), pl.program_id(1)

        @pl.when(j == 0)
        def _():
            xb[...] = x_ref[...].astype(BF16)
            acc[...] = jnp.zeros_like(acc)

        @pl.when((i == 0) & (j == 0))
        def _():
            dlg_ref[...] = jnp.zeros_like(dlg_ref)
            dlb_ref[...] = jnp.zeros_like(dlb_ref)
            loss_ref[...] = jnp.zeros_like(loss_ref)

        g = _dot(xb[...], wg_ref[...])
        u = _dot(xb[...], wu_ref[...])
        h = g * _sigmoid(g) * u
        go_ref[...] = g.astype(BF16)
        uo_ref[...] = u.astype(BF16)
        ht_ref[...] = h.T.astype(BF16)
        acc[...] += _dot(h.astype(BF16), wd_ref[...])

        @pl.when(j == nf - 1)
        def _():
            for r in range(0, tm, EPILOGUE_ROWS):
                rows = slice(r, r + EPILOGUE_ROWS)
                xh, rstd = _ln_stats(ALPHA * x_ref[rows, :] + 0.5 * acc[rows, :])
                e = xh * lg_ref[...] + lb_ref[...] - t_ref[rows, :]
                loss_ref[...] += 0.5 * jnp.sum(jnp.sum(e * e, axis=-1, keepdims=True) * (1.0 / d), axis=0,
                                               keepdims=True)
                dy = e * (1.0 / d)
                dz = _ln_bwd(dy * lg_ref[...], xh, rstd)
                dz_ref[rows, :] = dz
                dzb_ref[rows, :] = (0.5 * dz).astype(BF16)
                dlg_ref[...] += jnp.sum(dy * xh, axis=0, keepdims=True)
                dlb_ref[...] += jnp.sum(dy, axis=0, keepdims=True)

    row = lambda i, j: (i, 0)
    fixed = lambda i, j: (0, 0)
    return _call(
        body, exch, name=name, grid=(t // tm, nf),
        out_shape=[jax.ShapeDtypeStruct((t, f), BF16), jax.ShapeDtypeStruct((t, f), BF16),
                   jax.ShapeDtypeStruct((f, t), BF16), jax.ShapeDtypeStruct((t, d), F32),
                   jax.ShapeDtypeStruct((t, d), BF16), jax.ShapeDtypeStruct((1, d), F32),
                   jax.ShapeDtypeStruct((1, d), F32), jax.ShapeDtypeStruct((8, 128), F32)],
        in_specs=[pl.BlockSpec((tm, d), row),
                  pl.BlockSpec((d, tf), lambda i, j: (0, j)),
                  pl.BlockSpec((d, tf), lambda i, j: (0, j + nf)),
                  pl.BlockSpec((tf, d), lambda i, j: (j, 0)),
                  pl.BlockSpec((1, d), fixed), pl.BlockSpec((1, d), fixed), pl.BlockSpec((tm, d), row)],
        out_specs=[pl.BlockSpec((tm, tf), lambda i, j: (i, j)), pl.BlockSpec((tm, tf), lambda i, j: (i, j)),
                   pl.BlockSpec((tf, tm), lambda i, j: (j, i)), pl.BlockSpec((tm, d), row), pl.BlockSpec((tm, d), row),
                   pl.BlockSpec((1, d), fixed), pl.BlockSpec((1, d), fixed), pl.BlockSpec((8, 128), fixed)],
        scratch_shapes=[pltpu.VMEM((tm, d), BF16), pltpu.VMEM((tm, d), F32)],
        semantics=("arbitrary", "arbitrary"),
    )(x, wgu, wgu, wd, ln_g, ln_b, target)


def _ffn_down_fwd(gu, x, wd, ln_g, ln_b, name, exch=()):
    t, d = x.shape
    f = wd.shape[0]
    tm, tf = _tile(t, 512, 128), _tile(f, 512, 128)
    nf = f // tf

    def body(g_ref, u_ref, wd_ref, x_ref, lg_ref, lb_ref, ht_ref, z_ref, xn_ref, acc):
        j = pl.program_id(1)

        @pl.when(j == 0)
        def _():
            acc[...] = jnp.zeros_like(acc)

        g = g_ref[...].astype(F32)
        h = g * _sigmoid(g) * u_ref[...].astype(F32)
        ht_ref[...] = h.T.astype(BF16)
        acc[...] += _dot(h.astype(BF16), wd_ref[...])

        @pl.when(j == nf - 1)
        def _():
            z = ALPHA * x_ref[...] + 0.5 * acc[...]
            z_ref[...] = z
            xn_ref[...] = _ln(z, lg_ref[...], lb_ref[...])

    row = lambda i, j: (i, 0)
    fixed = lambda i, j: (0, 0)
    return _call(
        body, exch, name=name, grid=(t // tm, nf),
        out_shape=[jax.ShapeDtypeStruct((f, t), BF16), jax.ShapeDtypeStruct((t, d), F32),
                   jax.ShapeDtypeStruct((t, d), F32)],
        in_specs=[pl.BlockSpec((tm, tf), lambda i, j: (i, j)), pl.BlockSpec((tm, tf), lambda i, j: (i, j + nf)),
                  pl.BlockSpec((tf, d), lambda i, j: (j, 0)), pl.BlockSpec((tm, d), row),
                  pl.BlockSpec((1, d), fixed), pl.BlockSpec((1, d), fixed)],
        out_specs=[pl.BlockSpec((tf, tm), lambda i, j: (j, i)), pl.BlockSpec((tm, d), row), pl.BlockSpec((tm, d), row)],
        scratch_shapes=[pltpu.VMEM((tm, d), F32)],
        semantics=("parallel", "arbitrary"),
    )(gu, gu, wd, x, ln_g, ln_b)


def _ffn_act_grads(dh, g_ref, u_ref):
    gg = g_ref[...].astype(F32)
    uu = u_ref[...].astype(F32)
    s = _sigmoid(gg)
    du = (dh * (gg * s)).astype(BF16)
    dg = (dh * uu * (s * (1.0 + gg * (1.0 - s)))).astype(BF16)
    return dg, du


def _ffn_bwd(dz, do, g, u, wgu, wd, name, exch=()):
    t, d = dz.shape
    f = wd.shape[0]
    tm, tf = _tile(t, 512, 128), _tile(f, 512, 128)
    nf = f // tf

    def body(dz_ref, do_ref, g_ref, u_ref, wg_ref, wu_ref, wd_ref, dg_ref, du_ref, dx_ref, acc):
        j = pl.program_id(1)

        @pl.when(j == 0)
        def _():
            acc[...] = jnp.zeros_like(acc)

        dg, du = _ffn_act_grads(_dot_nt(do_ref[...], wd_ref[...]), g_ref, u_ref)
        dg_ref[...] = dg
        du_ref[...] = du
        acc[...] += _dot_nt(dg, wg_ref[...]) + _dot_nt(du, wu_ref[...])

        @pl.when(j == nf - 1)
        def _():
            dx_ref[...] = ALPHA * dz_ref[...] + acc[...]

    row = lambda i, j: (i, 0)
    tile = lambda i, j: (i, j)
    return _call(
        body, exch, name=name, grid=(t // tm, nf),
        out_shape=[jax.ShapeDtypeStruct((t, f), BF16), jax.ShapeDtypeStruct((t, f), BF16),
                   jax.ShapeDtypeStruct((t, d), F32)],
        in_specs=[pl.BlockSpec((tm, d), row), pl.BlockSpec((tm, d), row),
                  pl.BlockSpec((tm, tf), tile), pl.BlockSpec((tm, tf), tile),
                  pl.BlockSpec((d, tf), lambda i, j: (0, j)),
                  pl.BlockSpec((d, tf), lambda i, j: (0, j + nf)),
                  pl.BlockSpec((tf, d), lambda i, j: (j, 0))],
        out_specs=[pl.BlockSpec((tm, tf), tile), pl.BlockSpec((tm, tf), tile), pl.BlockSpec((tm, d), row)],
        scratch_shapes=[pltpu.VMEM((tm, d), F32)],
        semantics=("parallel", "arbitrary"),
    )(dz, do, g, u, wgu, wgu, wd)


def _ffn_bwd_act(do, gu, wd, name, exch=()):
    t, d = do.shape
    f = wd.shape[0]
    tm, tf = _tile(t, 2048, 128), _tile(f, 512, 128)
    nf = f // tf

    def body(do_ref, g_ref, u_ref, wd_ref, dg_ref, du_ref):
        dg, du = _ffn_act_grads(_dot_nt(do_ref[...], wd_ref[...]), g_ref, u_ref)
        dg_ref[...] = dg
        du_ref[...] = du

    tile = lambda i, j: (i, j)
    return _call(
        body, exch, name=name, grid=(t // tm, f // tf),
        out_shape=[jax.ShapeDtypeStruct((t, f), BF16), jax.ShapeDtypeStruct((t, f), BF16)],
        in_specs=[pl.BlockSpec((tm, d), lambda i, j: (i, 0)), pl.BlockSpec((tm, tf), tile),
                  pl.BlockSpec((tm, tf), lambda i, j: (i, j + nf)), pl.BlockSpec((tf, d), lambda i, j: (j, 0))],
        out_specs=[pl.BlockSpec((tm, tf), tile), pl.BlockSpec((tm, tf), tile)],
        semantics=("parallel", "parallel"),
    )(do, gu, gu, wd)


def _ffn_bwd_dx(dz, dg, du, wgu, name, exch=()):
    t, d = dz.shape
    f = dg.shape[1]
    tm, tn = _tile(t, 512, 128), _tile(d, 256, 128)

    def body(dz_ref, dg_ref, du_ref, wg_ref, wu_ref, dx_ref):
        dx_ref[...] = ALPHA * dz_ref[...] + _dot_nt(dg_ref[...], wg_ref[...]) + _dot_nt(du_ref[...], wu_ref[...])

    row = lambda i, n: (i, 0)
    tile = lambda i, n: (i, n)
    return _call(
        body, exch, name=name, grid=(t // tm, d // tn), out_shape=[jax.ShapeDtypeStruct((t, d), F32)],
        in_specs=[pl.BlockSpec((tm, tn), tile), pl.BlockSpec((tm, f), row), pl.BlockSpec((tm, f), row),
                  pl.BlockSpec((tn, f), lambda i, n: (n, 0)), pl.BlockSpec((tn, f), lambda i, n: (n, 1))],
        out_specs=[pl.BlockSpec((tm, tn), tile)],
        semantics=("parallel", "arbitrary"),
    )(dz, dg, du, wgu, wgu)


def _weight_grad(at, b, tn, tmm, name, blocks=None, block_offset=0, into=None, exch=()):
    m, t = at.shape
    nn = b.shape[1]
    tmm = _tile(m, tmm, 16)
    assert nn % tn == 0

    def body(*refs):
        at_ref, b_ref, o_ref = refs[0], refs[1], refs[-1]
        r = _dot(at_ref[...], b_ref[...]).astype(BF16)
        if blocks is None:
            o_ref[...] = r
        else:
            o_ref[0] = r

    in_specs = [pl.BlockSpec((tmm, t), lambda n, i: (i, 0)), pl.BlockSpec((t, tn), lambda n, i: (0, n))]
    args = [at, b]
    aliases = {}
    if into is not None:
        in_specs.append(ANY)
        args.append(into)
        aliases = {2: 0}
    if blocks is None:
        out_shape = jax.ShapeDtypeStruct((m, nn), BF16)
        out_spec = pl.BlockSpec((tmm, tn), lambda n, i: (i, n))
    else:
        out_shape = jax.ShapeDtypeStruct((blocks, m, tn), BF16)
        out_spec = pl.BlockSpec((1, tmm, tn), lambda n, i: (n + block_offset, i, 0))
    (out,), ex = _call(
        body, exch, name=name, grid=(nn // tn, m // tmm), out_shape=[out_shape],
        in_specs=in_specs, out_specs=[out_spec], input_output_aliases=aliases,
        semantics=("parallel", "parallel"),
    )(*args)
    return out, ex


def _mix_in_proj(x, w_in, name, exch=()):
    t, d = x.shape
    n_out = w_in.shape[1]
    tm, cb = _tile(t, 512, 128), _tile(n_out, 1024, 128)

    def body(x_ref, w_ref, o_ref, xb):
        @pl.when(pl.program_id(1) == 0)
        def _():
            xb[...] = x_ref[...].astype(BF16)

        o_ref[...] = _dot(xb[...], w_ref[...])

    (out,), ex = _call(
        body, exch, name=name, grid=(t // tm, n_out // cb), out_shape=[jax.ShapeDtypeStruct((t, n_out), F32)],
        in_specs=[pl.BlockSpec((tm, d), lambda i, k: (i, 0)), pl.BlockSpec((d, cb), lambda i, k: (0, k))],
        out_specs=[pl.BlockSpec((tm, cb), lambda i, k: (i, k))],
        scratch_shapes=[pltpu.VMEM((tm, d), BF16)],
        semantics=("parallel", "arbitrary"),
    )(x, w_in)
    return out, ex


def _mix_in_bwd(dproj, w_in, dz, name, exch=()):
    t, d = dz.shape
    kk = w_in.shape[1]
    tm, tn = _tile(t, 512, 128), _tile(d, 512, 128)

    def body(dp_ref, w_ref, dz_ref, dx_ref):
        dx_ref[...] = ALPHA * dz_ref[...] + _dot_nt(dp_ref[...], w_ref[...])

    (out,), ex = _call(
        body, exch, name=name, grid=(t // tm, d // tn), out_shape=[jax.ShapeDtypeStruct((t, d), F32)],
        in_specs=[pl.BlockSpec((tm, kk), lambda i, n: (i, 0)), pl.BlockSpec((tn, kk), lambda i, n: (n, 0)),
                  pl.BlockSpec((tm, tn), lambda i, n: (i, n))],
        out_specs=[pl.BlockSpec((tm, tn), lambda i, n: (i, n))],
        semantics=("parallel", "arbitrary"),
    )(dproj, w_in, dz)
    return out, ex


def _mix_out_fwd(y, w_out, x, ln_g, ln_b, name, exch=()):
    t, d = x.shape
    kk = y.shape[1]
    tm = _tile(t, 256, 128)

    def body(y_ref, w_ref, x_ref, g_ref, b_ref, z_ref, xn_ref, xnt_ref):
        z = ALPHA * x_ref[...] + _dot(y_ref[...], w_ref[...])
        z_ref[...] = z
        xn = _ln(z, g_ref[...], b_ref[...])
        xn_ref[...] = xn
        xnt_ref[...] = xn.T.astype(BF16)

    row = lambda i: (i, 0)
    fixed = lambda i: (0, 0)
    return _call(
        body, exch, name=name, grid=(t // tm,),
        out_shape=[jax.ShapeDtypeStruct((t, d), F32), jax.ShapeDtypeStruct((t, d), F32),
                   jax.ShapeDtypeStruct((d, t), BF16)],
        in_specs=[pl.BlockSpec((tm, kk), row), pl.BlockSpec((kk, d), fixed), pl.BlockSpec((tm, d), row),
                  pl.BlockSpec((1, d), fixed), pl.BlockSpec((1, d), fixed)],
        out_specs=[pl.BlockSpec((tm, d), row), pl.BlockSpec((tm, d), row), pl.BlockSpec((d, tm), lambda i: (0, i))],
        semantics=("parallel",),
    )(y, w_out, x, ln_g, ln_b)


def _mix_out_bwd(dzb, w_out, name, exch=()):
    t, d = dzb.shape
    kk = w_out.shape[0]
    tm = _tile(t, 512, 128)

    def body(dz_ref, w_ref, dy_ref):
        dy_ref[...] = _dot_nt(dz_ref[...], w_ref[...])

    (out,), ex = _call(
        body, exch, name=name, grid=(t // tm,), out_shape=[jax.ShapeDtypeStruct((t, kk), F32)],
        in_specs=[pl.BlockSpec((tm, d), lambda i: (i, 0)), pl.BlockSpec((kk, d), lambda i: (0, 0))],
        out_specs=[pl.BlockSpec((tm, kk), lambda i: (i, 0))],
        semantics=("parallel",),
    )(dzb, w_out)
    return out, ex


def _loss_ln_bwd(z, target, ln_g, ln_b, bf16_scale, name):
    t, d = z.shape
    tm = _tile(t, 512, 8)

    def body(z_ref, t_ref, g_ref, b_ref, dz_ref, dzb_ref, dg_ref, db_ref, loss_ref):
        @pl.when(pl.program_id(0) == 0)
        def _():
            dg_ref[...] = jnp.zeros_like(dg_ref)
            db_ref[...] = jnp.zeros_like(db_ref)
            loss_ref[...] = jnp.zeros_like(loss_ref)

        xh, rstd = _ln_stats(z_ref[...])
        e = xh * g_ref[...] + b_ref[...] - t_ref[...]
        loss_ref[...] += 0.5 * jnp.sum(jnp.sum(e * e, axis=-1, keepdims=True) * (1.0 / d), axis=0, keepdims=True)
        dy = e * (1.0 / d)
        dz = _ln_bwd(dy * g_ref[...], xh, rstd)
        dz_ref[...] = dz
        dzb_ref[...] = (bf16_scale * dz).astype(BF16)
        dg_ref[...] += jnp.sum(dy * xh, axis=0, keepdims=True)
        db_ref[...] += jnp.sum(dy, axis=0, keepdims=True)

    row = lambda i: (i, 0)
    fixed = lambda i: (0, 0)
    return pl.pallas_call(
        body, name=name, grid=(t // tm,),
        out_shape=[jax.ShapeDtypeStruct((t, d), F32), jax.ShapeDtypeStruct((t, d), BF16),
                   jax.ShapeDtypeStruct((1, d), F32), jax.ShapeDtypeStruct((1, d), F32),
                   jax.ShapeDtypeStruct((8, 128), F32)],
        in_specs=[pl.BlockSpec((tm, d), row), pl.BlockSpec((tm, d), row), pl.BlockSpec((1, d), fixed),
                  pl.BlockSpec((1, d), fixed)],
        out_specs=[pl.BlockSpec((tm, d), row), pl.BlockSpec((tm, d), row), pl.BlockSpec((1, d), fixed),
                   pl.BlockSpec((1, d), fixed), pl.BlockSpec((8, 128), fixed)],
        compiler_params=_cparams("arbitrary"),
    )(z, target, ln_g, ln_b)


def _ln_bwd_call(z, dy, ln_g, bf16_scale, name, exch=()):
    t, d = z.shape
    tm = _tile(t, 512, 8)

    def body(z_ref, dy_ref, g_ref, dz_ref, dzb_ref, dg_ref, db_ref):
        @pl.when(pl.program_id(0) == 0)
        def _():
            dg_ref[...] = jnp.zeros_like(dg_ref)
            db_ref[...] = jnp.zeros_like(db_ref)

        xh, rstd = _ln_stats(z_ref[...])
        dy = dy_ref[...]
        dz = _ln_bwd(dy * g_ref[...], xh, rstd)
        dz_ref[...] = dz
        dzb_ref[...] = (bf16_scale * dz).astype(BF16)
        dg_ref[...] += jnp.sum(dy * xh, axis=0, keepdims=True)
        db_ref[...] += jnp.sum(dy, axis=0, keepdims=True)

    row = lambda i: (i, 0)
    fixed = lambda i: (0, 0)
    return _call(
        body, exch, name=name, grid=(t // tm,),
        out_shape=[jax.ShapeDtypeStruct((t, d), F32), jax.ShapeDtypeStruct((t, d), BF16),
                   jax.ShapeDtypeStruct((1, d), F32), jax.ShapeDtypeStruct((1, d), F32)],
        in_specs=[pl.BlockSpec((tm, d), row), pl.BlockSpec((tm, d), row), pl.BlockSpec((1, d), fixed)],
        out_specs=[pl.BlockSpec((tm, d), row), pl.BlockSpec((tm, d), row), pl.BlockSpec((1, d), fixed),
                   pl.BlockSpec((1, d), fixed)],
        semantics=("arbitrary",),
    )(z, dy, ln_g)


CONV_ROWS = 32
CONV_LANES = 512
SUBLANES = 8


def _fill_shifted(ext, shifted):
    rows = ext.shape[0] - SUBLANES
    for s in range(1, SUBLANES):
        for r in range(0, rows, CONV_ROWS):
            n = min(CONV_ROWS, rows - r)
            shifted[s - 1, r:r + n, :] = ext[r + s:r + s + n, :]


def _window(ext, shifted, lo, n, lanes=slice(None)):
    s = lo % SUBLANES
    return ext[lo:lo + n, lanes] if s == 0 else shifted[s - 1, lo - s:lo - s + n, lanes]


def _mixer_fwd(proj, conv_w, conv_b, cln_g, cln_b, sln_g, sln_b, sg_wm, sg_bb, name, exch=()):
    t = proj.shape[0]
    tm = _tile(t, 256, CHUNK)
    hb = tm // HALO
    nc = tm // CHUNK
    ch = CONV_CH

    def body(av_ref, ag_ref, bu_ref, bv_ref, hv_ref, hg_ref, cw_ref, cb_ref, lg_ref, lb_ref, sg_ref, sb_ref,
             w_ref, bb_ref, y_ref, yt_ref, c_ref, ext, ext_s):
        i = pl.program_id(0)
        halo = hv_ref[...] * _sigmoid(hg_ref[...])
        ext[0:HALO, :] = jnp.where(i > 0, halo, 0.0)
        ext[HALO:HALO + tm, :] = av_ref[...] * _sigmoid(ag_ref[...])
        _fill_shifted(ext, ext_s)
        for r in range(0, tm, CONV_ROWS):
            acc = jnp.zeros((CONV_ROWS, ch), F32) + cb_ref[...]
            for k in range(CONV_TAPS):
                lo = r + k + HALO - (CONV_TAPS - 1)
                acc = acc + cw_ref[k:k + 1, :] * _window(ext, ext_s, lo, CONV_ROWS)
            c_ref[r:r + CONV_ROWS, :] = acc
        a = _ln(c_ref[...], lg_ref[...], lb_ref[...])
        ya = a * _sigmoid(a)
        y_ref[:, 0:ch] = ya.astype(BF16)
        yt_ref[0:ch, :] = ya.T.astype(BF16)
        for h in range(HEADS):
            sl = slice(h * HEAD_DIM, (h + 1) * HEAD_DIM)
            u, _ = _gelu_and_grad(bu_ref[:, sl])
            v, _ = _gelu_and_grad(bv_ref[:, sl])
            vn = _ln(v, sg_ref[h:h + 1, :], sb_ref[h:h + 1, :])
            vn3 = vn.astype(BF16).reshape(nc, CHUNK, HEAD_DIM)
            wb = jnp.broadcast_to(w_ref[h][None], (nc, CHUNK, CHUNK))
            mixed = jnp.einsum("cts,csd->ctd", wb, vn3, preferred_element_type=F32) + bb_ref[h][None]
            yb = u * mixed.reshape(tm, HEAD_DIM)
            y_ref[:, ch + h * HEAD_DIM:ch + (h + 1) * HEAD_DIM] = yb.astype(BF16)
            yt_ref[ch + h * HEAD_DIM:ch + (h + 1) * HEAD_DIM, :] = yb.T.astype(BF16)

    col = lambda cidx: (lambda i: (i, cidx))
    prev = lambda cidx: (lambda i: (jnp.maximum(i * hb - 1, 0), cidx))
    fix2 = lambda i: (0, 0)
    fix3 = lambda i: (0, 0, 0)
    return _call(
        body, exch, name=name, grid=(t // tm,),
        out_shape=[jax.ShapeDtypeStruct((t, 2 * ch), BF16), jax.ShapeDtypeStruct((2 * ch, t), BF16),
                   jax.ShapeDtypeStruct((t, ch), F32)],
        in_specs=[pl.BlockSpec((tm, ch), col(0)), pl.BlockSpec((tm, ch), col(1)), pl.BlockSpec((tm, ch), col(2)),
                  pl.BlockSpec((tm, ch), col(3)), pl.BlockSpec((HALO, ch), prev(0)), pl.BlockSpec((HALO, ch), prev(1)),
                  pl.BlockSpec((CONV_TAPS, ch), fix2), pl.BlockSpec((1, ch), fix2), pl.BlockSpec((1, ch), fix2),
                  pl.BlockSpec((1, ch), fix2), pl.BlockSpec((HEADS, HEAD_DIM), fix2), pl.BlockSpec((HEADS, HEAD_DIM), fix2),
                  pl.BlockSpec((HEADS, CHUNK, CHUNK), fix3), pl.BlockSpec((HEADS, CHUNK, HEAD_DIM), fix3)],
        out_specs=[pl.BlockSpec((tm, 2 * ch), lambda i: (i, 0)), pl.BlockSpec((2 * ch, tm), lambda i: (0, i)),
                   pl.BlockSpec((tm, ch), lambda i: (i, 0))],
        scratch_shapes=[pltpu.VMEM((HALO + tm, ch), F32), pltpu.VMEM((SUBLANES - 1, HALO + tm, ch), F32)],
        semantics=("parallel",),
    )(proj, proj, proj, proj, proj, proj, conv_w, conv_b, cln_g, cln_b, sln_g, sln_b, sg_wm, sg_bb)


def _mixer_bwd(proj, conv_c, dy, conv_w, cln_g, cln_b, sln_g, sln_b, sg_wm, sg_wmt, sg_bb, name, exch=()):
    t = proj.shape[0]
    tm = _tile(t, 256, CHUNK)
    hb = tm // HALO
    nc = tm // CHUNK
    nt = t // tm
    ch = CONV_CH
    last_halo = t // HALO - 1

    def body(av_ref, ag_ref, bu_ref, bv_ref, hv_ref, hg_ref, c_ref, cn_ref, dya_ref, dyan_ref, dyb_ref,
             cw_ref, lg_ref, lb_ref, sg_ref, sb_ref, w_ref, wt_ref, bb_ref,
             dp_ref, dcw_ref, dcb_ref, dlg_ref, dlb_ref, dsg_ref, dsb_ref, dw_ref, dbs_ref,
             ext_h, ext_dc, ext_hs, ext_dcs, acc_cw):
        i = pl.program_id(0)

        @pl.when(i == 0)
        def _():
            acc_cw[...] = jnp.zeros_like(acc_cw)
            for ref in (dcb_ref, dlg_ref, dlb_ref, dsg_ref, dsb_ref, dw_ref, dbs_ref):
                ref[...] = jnp.zeros_like(ref)

        lg = lg_ref[...]
        lb = lb_ref[...]

        def conv_ln_bwd(c, dya):
            xh, rstd = _ln_stats(c)
            a = xh * lg + lb
            da = dya * _silu_grad(a)
            return _ln_bwd(da * lg, xh, rstd), da, xh

        fold = lambda v: jnp.sum(v.reshape(CONV_ROWS // SUBLANES, SUBLANES, ch), axis=0)
        s_lg = s_lb = s_cb = jnp.zeros((SUBLANES, ch), F32)
        for r in range(0, tm, CONV_ROWS):
            dc, da, xh = conv_ln_bwd(c_ref[r:r + CONV_ROWS, :], dya_ref[r:r + CONV_ROWS, :])
            ext_dc[r:r + CONV_ROWS, :] = dc
            s_lg, s_lb, s_cb = s_lg + fold(da * xh), s_lb + fold(da), s_cb + fold(dc)
        dlg_ref[...] += jnp.sum(s_lg, axis=0, keepdims=True)
        dlb_ref[...] += jnp.sum(s_lb, axis=0, keepdims=True)
        dcb_ref[...] += jnp.sum(s_cb, axis=0, keepdims=True)
        dcn, _, _ = conv_ln_bwd(cn_ref[...], dyan_ref[...])
        ext_dc[tm:tm + HALO, :] = jnp.where(i < nt - 1, dcn, 0.0)
        halo = hv_ref[...] * _sigmoid(hg_ref[...])
        ext_h[0:HALO, :] = jnp.where(i > 0, halo, 0.0)
        ext_h[HALO:HALO + tm, :] = av_ref[...] * _sigmoid(ag_ref[...])
        _fill_shifted(ext_h, ext_hs)
        _fill_shifted(ext_dc, ext_dcs)
        for r, c0 in [(r, c0) for r in range(0, tm, CONV_ROWS) for c0 in range(0, ch, CONV_LANES)]:
            rows, lanes = slice(r, r + CONV_ROWS), slice(c0, c0 + CONV_LANES)
            dcr = ext_dc[rows, lanes]
            acc = jnp.zeros((CONV_ROWS, CONV_LANES), F32)
            for k in range(CONV_TAPS):
                lo = r + k + HALO - (CONV_TAPS - 1)
                prod = dcr * _window(ext_h, ext_hs, lo, CONV_ROWS, lanes)
                acc_cw[k, :, lanes] += jnp.sum(prod.reshape(CONV_ROWS // SUBLANES, SUBLANES, CONV_LANES), axis=0)
                hi = r + (CONV_TAPS - 1) - k
                acc = acc + cw_ref[k:k + 1, lanes] * _window(ext_dc, ext_dcs, hi, CONV_ROWS, lanes)
            sg_r = _sigmoid(ag_ref[rows, lanes])
            av_r = av_ref[rows, lanes]
            dp_ref[rows, lanes] = (acc * sg_r).astype(BF16)
            dp_ref[rows, slice(ch + c0, ch + c0 + CONV_LANES)] = (acc * av_r * sg_r * (1.0 - sg_r)).astype(BF16)

        @pl.when(i == nt - 1)
        def _():
            dcw_ref[...] = jnp.sum(acc_cw[...], axis=1)

        tril = (lax.broadcasted_iota(jnp.int32, (CHUNK, CHUNK), 0)
                >= lax.broadcasted_iota(jnp.int32, (CHUNK, CHUNK), 1)).astype(F32)
        for h in range(HEADS):
            sl = slice(h * HEAD_DIM, (h + 1) * HEAD_DIM)
            u, du_dx = _gelu_and_grad(bu_ref[:, sl])
            v, dv_dx = _gelu_and_grad(bv_ref[:, sl])
            xhv, rstdv = _ln_stats(v)
            gh = sg_ref[h:h + 1, :]
            vn3 = (xhv * gh + sb_ref[h:h + 1, :]).astype(BF16).reshape(nc, CHUNK, HEAD_DIM)
            wb = jnp.broadcast_to(w_ref[h][None], (nc, CHUNK, CHUNK))
            mixed = jnp.einsum("cts,csd->ctd", wb, vn3, preferred_element_type=F32) + bb_ref[h][None]
            dyb = dyb_ref[:, sl]
            d_u = dyb * mixed.reshape(tm, HEAD_DIM)
            dm = dyb * u
            dm3 = dm.reshape(nc, CHUNK, HEAD_DIM)
            dbs_ref[h:h + 1, :] += jnp.sum(jnp.sum(dm3, axis=0).T, axis=0, keepdims=True)
            dm3b = dm3.astype(BF16)
            dw_h = jnp.sum(jnp.einsum("ctd,csd->cts", dm3b, vn3, preferred_element_type=F32), axis=0)
            dw_ref[h] += dw_h * tril
            wtb = jnp.broadcast_to(wt_ref[h][None], (nc, CHUNK, CHUNK))
            d_vn = jnp.einsum("cst,ctd->csd", wtb, dm3b, preferred_element_type=F32).reshape(tm, HEAD_DIM)
            dsg_ref[h:h + 1, :] += jnp.sum(d_vn * xhv, axis=0, keepdims=True)
            dsb_ref[h:h + 1, :] += jnp.sum(d_vn, axis=0, keepdims=True)
            dv = _ln_bwd(d_vn * gh, xhv, rstdv)
            dp_ref[:, 2 * ch + h * HEAD_DIM:2 * ch + (h + 1) * HEAD_DIM] = (d_u * du_dx).astype(BF16)
            dp_ref[:, 3 * ch + h * HEAD_DIM:3 * ch + (h + 1) * HEAD_DIM] = (dv * dv_dx).astype(BF16)

    col = lambda cidx: (lambda i: (i, cidx))
    prev = lambda cidx: (lambda i: (jnp.maximum(i * hb - 1, 0), cidx))
    nxt = lambda i: (jnp.minimum((i + 1) * hb, last_halo), 0)
    fix2 = lambda i: (0, 0)
    fix3 = lambda i: (0, 0, 0)
    out_shape = [jax.ShapeDtypeStruct((t, 4 * ch), BF16), jax.ShapeDtypeStruct((CONV_TAPS, ch), F32),
                 jax.ShapeDtypeStruct((1, ch), F32), jax.ShapeDtypeStruct((1, ch), F32), jax.ShapeDtypeStruct((1, ch), F32),
                 jax.ShapeDtypeStruct((HEADS, HEAD_DIM), F32), jax.ShapeDtypeStruct((HEADS, HEAD_DIM), F32),
                 jax.ShapeDtypeStruct((HEADS, CHUNK, CHUNK), F32), jax.ShapeDtypeStruct((HEADS, CHUNK), F32)]
    out_specs = [pl.BlockSpec((tm, 4 * ch), lambda i: (i, 0)), pl.BlockSpec((CONV_TAPS, ch), fix2),
                 pl.BlockSpec((1, ch), fix2), pl.BlockSpec((1, ch), fix2), pl.BlockSpec((1, ch), fix2),
                 pl.BlockSpec((HEADS, HEAD_DIM), fix2), pl.BlockSpec((HEADS, HEAD_DIM), fix2),
                 pl.BlockSpec((HEADS, CHUNK, CHUNK), fix3), pl.BlockSpec((HEADS, CHUNK), fix2)]
    in_specs = [pl.BlockSpec((tm, ch), col(0)), pl.BlockSpec((tm, ch), col(1)), pl.BlockSpec((tm, ch), col(2)),
                pl.BlockSpec((tm, ch), col(3)), pl.BlockSpec((HALO, ch), prev(0)), pl.BlockSpec((HALO, ch), prev(1)),
                pl.BlockSpec((tm, ch), col(0)), pl.BlockSpec((HALO, ch), nxt),
                pl.BlockSpec((tm, ch), col(0)), pl.BlockSpec((HALO, ch), nxt), pl.BlockSpec((tm, ch), col(1)),
                pl.BlockSpec((CONV_TAPS, ch), fix2), pl.BlockSpec((1, ch), fix2), pl.BlockSpec((1, ch), fix2),
                pl.BlockSpec((HEADS, HEAD_DIM), fix2), pl.BlockSpec((HEADS, HEAD_DIM), fix2),
                pl.BlockSpec((HEADS, CHUNK, CHUNK), fix3), pl.BlockSpec((HEADS, CHUNK, CHUNK), fix3),
                pl.BlockSpec((HEADS, CHUNK, HEAD_DIM), fix3)]
    return _call(
        body, exch, name=name, grid=(nt,), out_shape=out_shape, in_specs=in_specs, out_specs=out_specs,
        scratch_shapes=[pltpu.VMEM((HALO + tm, ch), F32), pltpu.VMEM((tm + HALO, ch), F32),
                        pltpu.VMEM((SUBLANES - 1, HALO + tm, ch), F32), pltpu.VMEM((SUBLANES - 1, tm + HALO, ch), F32),
                        pltpu.VMEM((CONV_TAPS, 8, ch), F32)],
        semantics=("arbitrary",),
    )(proj, proj, proj, proj, proj, proj, conv_c, conv_c, dy, dy, dy,
      conv_w, cln_g, cln_b, sln_g, sln_b, sg_wm, sg_wmt, sg_bb)


def _pair_sum(parts, from_sibling, core_chip, name):
    _, r, cc = parts.shape
    tr = _tile(r, max(16, (1 << 20) // (2 * cc)), 16)

    def body(cc_ref, p_ref, s_ref, o_ref, own_ref):
        q = (p_ref[...].astype(F32) + s_ref[...].astype(F32)).astype(BF16)
        o_ref[...] = q

        @pl.when(pl.program_id(1) == cc_ref[1])
        def _():
            own_ref[...] = q[0]

    grid_spec = pltpu.PrefetchScalarGridSpec(
        num_scalar_prefetch=1, grid=(r // tr, 4),
        in_specs=[pl.BlockSpec((1, tr, cc), lambda i, j, cc_ref: (2 * j + cc_ref[0], i, 0)),
                  pl.BlockSpec((1, tr, cc), lambda i, j, cc_ref: (j, i, 0))],
        out_specs=[pl.BlockSpec((1, tr, cc), lambda i, j, cc_ref: (j, i, 0)),
                   pl.BlockSpec((tr, cc), lambda i, j, cc_ref: (i, 0))])
    return pl.pallas_call(
        body, name=name, grid_spec=grid_spec,
        out_shape=[jax.ShapeDtypeStruct((4, r, cc), BF16), jax.ShapeDtypeStruct((r, cc), BF16)],
        compiler_params=_cparams("parallel", "arbitrary"),
    )(core_chip, parts, from_sibling)


def _adamw_math(w, g, m, v):
    m = ADAM_B1 * m + (1.0 - ADAM_B1) * g
    v = ADAM_B2 * v + (1.0 - ADAM_B2) * (g * g)
    m_hat = m / (1.0 - ADAM_B1 ** ADAM_STEP)
    v_hat = v / (1.0 - ADAM_B2 ** ADAM_STEP)
    delta = -ADAM_LR * (m_hat / (jnp.sqrt(v_hat) + ADAM_EPS) + ADAM_WD * w)
    return delta, m, v


def _adamw_tile(in_refs, out_refs):
    w_ref, m_ref, v_ref, q_ref, o_ref = in_refs
    g = q_ref[...].astype(F32)
    for k in range(3):
        g = g + o_ref[k].astype(F32)
    d, mm, vv = _adamw_math(w_ref[...], g, m_ref[...], v_ref[...])
    for ref, val in zip(out_refs, (g, d, mm, vv)):
        ref[...] = val


def _adamw_side(w, m, v, chip_part, from_chips, max_tiles):
    r, cc = w.shape
    n = max(k for k in range(1, max_tiles + 1) if r % k == 0 and (r // k) % 16 == 0)
    tr = r // n
    row = ((tr, cc), lambda s: (s, 0))
    return _Side([w, m, v, chip_part, from_chips], [row, row, row, row, ((3, tr, cc), lambda s: (0, s, 0))],
                 [jax.ShapeDtypeStruct((r, cc), F32)] * 4, [row] * 4, n, _adamw_tile)


def _adamw_sharded(w, m, v, chip_part, from_chips, name):
    r, cc = w.shape
    tr = _tile(r, max(16, (1 << 19) // (4 * cc) * 2), 16)

    def body(*refs):
        _adamw_tile(refs[:5], refs[5:])

    row = pl.BlockSpec((tr, cc), lambda i: (i, 0))
    return pl.pallas_call(
        body, name=name, grid=(r // tr,), out_shape=[jax.ShapeDtypeStruct((r, cc), F32)] * 4,
        in_specs=[row, row, row, row, pl.BlockSpec((3, tr, cc), lambda i: (0, i, 0))], out_specs=[row] * 4,
        compiler_params=_cparams("parallel"),
    )(w, m, v, chip_part, from_chips)


def _adamw_small(w, g, m, v, name):
    r, cc = w.shape

    def body(w_ref, g_ref, m_ref, v_ref, d_out, m_out, v_out):
        d, mm, vv = _adamw_math(w_ref[...], g_ref[...], m_ref[...], v_ref[...])
        d_out[...] = d
        m_out[...] = mm
        v_out[...] = vv

    full = pl.BlockSpec((r, cc), lambda i: (0, 0))
    return pl.pallas_call(
        body, name=name, grid=(1,), out_shape=[jax.ShapeDtypeStruct((r, cc), F32)] * 3,
        in_specs=[full] * 4, out_specs=[full] * 3, compiler_params=_cparams("arbitrary"),
    )(w, g, m, v)


SMALL = ("ln1_g", "ln1_b", "conv_b", "conv_ln_g", "conv_ln_b", "sg_ln_g", "sg_ln_b", "sg_w", "sg_b",
         "ln2_g", "ln2_b", "ln3_g", "ln3_b")
ORDER = ("ffn1_w_gate_up", "ffn1_w_down", "ln1_g", "ln1_b", "mix_w_in", "conv_w", "conv_b", "conv_ln_g", "conv_ln_b",
         "sg_ln_g", "sg_ln_b", "sg_w", "sg_b", "mix_w_out", "ln2_g", "ln2_b", "ffn2_w_gate_up", "ffn2_w_down",
         "ln3_g", "ln3_b")


def _rows128(a):
    return a.reshape(-1, 128)


def kernel(x, ffn1_w_gate_up, ffn1_w_down, ln1_g, ln1_b, mix_w_in, conv_w, conv_b, conv_ln_g, conv_ln_b, sg_ln_g, sg_ln_b, sg_w, sg_b, mix_w_out, ln2_g, ln2_b, ffn2_w_gate_up, ffn2_w_down, ln3_g, ln3_b, loss_target, m_ffn1_w_gate_up, m_ffn1_w_down, m_ln1_g, m_ln1_b, m_mix_w_in, m_conv_w, m_conv_b, m_conv_ln_g, m_conv_ln_b, m_sg_ln_g, m_sg_ln_b, m_sg_w, m_sg_b, m_mix_w_out, m_ln2_g, m_ln2_b, m_ffn2_w_gate_up, m_ffn2_w_down, m_ln3_g, m_ln3_b, v_ffn1_w_gate_up, v_ffn1_w_down, v_ln1_g, v_ln1_b, v_mix_w_in, v_conv_w, v_conv_b, v_conv_ln_g, v_conv_ln_b, v_sg_ln_g, v_sg_ln_b, v_sg_w, v_sg_b, v_mix_w_out, v_ln2_g, v_ln2_b, v_ffn2_w_gate_up, v_ffn2_w_down, v_ln3_g, v_ln3_b):
    args = dict(locals())
    w = {n: args[n][0] for n in ORDER}
    mom = {n: args["m_" + n][0] for n in ORDER}
    var = {n: args["v_" + n][0] for n in ORDER}
    x0 = x[0]
    target = loss_target[0]
    t, d = x0.shape
    my_x, my_y, my_c = lax.axis_index("x"), lax.axis_index("y"), lax.axis_index("c")
    my_chip = (2 * my_x + my_y).astype(jnp.int32).reshape(1)
    my_core = my_c.astype(jnp.int32).reshape(1)
    me = 4 * my_x + 2 * my_y + my_c

    big = ("ffn1_w_gate_up", "ffn1_w_down", "mix_w_in", "mix_w_out", "ffn2_w_gate_up", "ffn2_w_down")
    sh = {n: w[n].astype(BF16) for n in big}
    f2s = sh["ffn2_w_gate_up"].shape[1]
    order = jnp.stack([4 * p[0] + 2 * p[1] + p[2] for p in _visit_order(my_x, my_y, my_c)]).astype(jnp.int32)
    gu1, x0t, (wgu1, wd1, conv_w_all) = _gather_and_gate_up(
        x0, [sh["ffn1_w_gate_up"], sh["ffn1_w_down"], w["conv_w"]], [True, True, False], order, "ffn1_gate_up_fwd")
    wd1 = wd1.reshape(-1, d)
    conv_w_full = jnp.transpose(conv_w_all, (1, 0, 2)).reshape(CONV_TAPS, CONV_CH)
    tril = jnp.tril(jnp.ones((CHUNK, CHUNK), F32))
    sg_wm = w["sg_w"] * tril
    sg_wm_b = sg_wm.astype(BF16)
    sg_wmt_b = jnp.swapaxes(sg_wm, 1, 2).astype(BF16)
    sg_bb = jnp.broadcast_to(w["sg_b"][:, :, None], (HEADS, CHUNK, HEAD_DIM))
    row = lambda a: a.reshape(1, -1)

    d2 = [sh["ffn2_w_down"]]
    d2_first = d2[0].shape[0] * 3 // 5 // 16 * 16
    d2_top, d2_bottom = (0, d2_first), (d2_first, d2[0].shape[0] - d2_first)
    (h1t, z1, x1), ((g_in, g_out), (g_d2,)) = _ffn_down_fwd(
        gu1, x0, wd1, row(w["ln1_g"]), row(w["ln1_b"]), "ffn1_down_fwd",
        exch=[_gather_first([sh["mix_w_in"], sh["mix_w_out"]], [True, False]),
              _gather_first(d2, [False], rows=d2_top)])
    in_cols = sh["mix_w_in"].shape[1]
    x1t, ((w_in, w_out), (g_d2,)) = _transpose_bf16(
        x1, "x1_transpose", exch=[_gather_forward([g_in, g_out], [True, False], [in_cols, None]),
                                  _gather_forward([g_d2], [False], [None], rows=d2_top)])
    w_out = w_out.reshape(-1, d)
    top, bottom = (0, d // 2), (d // 2, d // 2)
    gu2 = [sh["ffn2_w_gate_up"]]
    proj, ((g_gu2,),) = _mix_in_proj(x1, w_in, "mix_in_fwd", exch=[_gather_first(gu2, [True], rows=top)])
    (y, yt, conv_c), ((g_gu2,),) = _mixer_fwd(
        proj, conv_w_full, row(w["conv_b"]), row(w["conv_ln_g"]), row(w["conv_ln_b"]),
        w["sg_ln_g"], w["sg_ln_b"], sg_wm_b, sg_bb, "mixer_fwd",
        exch=[_both(_gather_first(gu2, [True], rows=bottom, into=[g_gu2]),
                    _gather_forward([g_gu2], [True], [f2s], rows=top))])
    (z2, x2, x2t), ((wgu2,), (g_d2,)) = _mix_out_fwd(
        y, w_out, x1, row(w["ln2_g"]), row(w["ln2_b"]), "mix_out_fwd",
        exch=[_gather_forward([g_gu2], [True], [f2s], rows=bottom),
              _gather_first(d2, [False], rows=d2_bottom, into=[g_d2])])
    (wd2,) = _exchange_alone(_gather_forward([g_d2], [False], [None], rows=d2_bottom), "ffn2_down_gather_forward")
    wd2 = wd2.reshape(-1, d)
    grads = {}
    (g2, u2, h2t, dz3, do2, grads["ln3_g"], grads["ln3_b"], loss_tile), _ = _ffn_fwd_loss(
        x2, wgu2, wd2, row(w["ln3_g"]), row(w["ln3_b"]), target, "ffn2_fwd_loss")

    f = wd1.shape[0]
    dn = _tile(d, 1024, 128)
    core_chip = jnp.concatenate([my_core, my_chip])
    pair = lambda p, s, label: _pair_sum(p, s, core_chip, "pair_sum_" + label)
    adamw = lambda n, own, got, steps: _adamw_side(w[n], mom[n], var[n], own, got, steps)
    m_tiles = d // _tile(d, 512, 16)
    gu_first = d * 2 // 3 // 16 * 16
    out = {}
    p_d2, _ = _weight_grad(h2t, do2, dn, 512, "ffn2_dw_down")
    p_d2 = p_d2.reshape(N_DEV, f // N_DEV, d)
    (dg2, du2, dx2), ((s_d2,),) = _ffn_bwd(dz3, do2, g2, u2, wgu2, wd2, "ffn2_bwd", exch=[_rs_sibling([p_d2])])
    q_d2, own_d2 = pair(p_d2, s_d2, "ffn2_down")
    d_rows = q_d2.shape[1]
    d_half = d_rows // 2 // 16 * 16
    p_gu2, ((r_d2,),) = _weight_grad(x2t, dg2, f2s, 512, "ffn2_dw_gate", blocks=N_DEV,
                                     exch=[_rs_chips([q_d2], rows=(0, d_half))])
    p_gu2, ((r_d2,),) = _weight_grad(x2t, du2, f2s, 512, "ffn2_dw_up", blocks=N_DEV, block_offset=4, into=p_gu2,
                                     exch=[_rs_chips([q_d2], rows=(d_half, d_rows - d_half), into=[r_d2])])
    (dz2, dz2b, grads["ln2_g"], grads["ln2_b"]), ((s_gu2,),) = _ln_bwd_call(
        z2, dx2, row(w["ln2_g"]), 1.0, "ln2_bwd", exch=[_rs_sibling([p_gu2])])
    q_gu2, own_gu2 = pair(p_gu2, s_gu2, "ffn2_gate_up")
    dy, _ = _mix_out_bwd(dz2b, w_out, "mix_out_bwd")
    p_out, _ = _weight_grad(yt, dz2b, dn, 512, "mix_out_dw")
    p_out = p_out.reshape(N_DEV, -1, d)
    (dproj, grads["conv_w"], grads["conv_b"], grads["conv_ln_g"], grads["conv_ln_b"], grads["sg_ln_g"],
     grads["sg_ln_b"], grads["sg_w"], grads["sg_b"]), ((r_gu2,),) = _mixer_bwd(
        proj, conv_c, dy, conv_w_full, row(w["conv_ln_g"]), row(w["conv_ln_b"]), w["sg_ln_g"], w["sg_ln_b"],
        sg_wm_b, sg_wmt_b, sg_bb, "mixer_bwd", exch=[_rs_chips([q_gu2], rows=(0, gu_first))])
    dx1, ((s_out,), (r_gu2,)) = _mix_in_bwd(
        dproj, w_in, dz2, "mix_in_bwd",
        exch=[_rs_sibling([p_out]), _rs_chips([q_gu2], rows=(gu_first, d - gu_first), into=[r_gu2])])
    p_in, (out["ffn2_w_gate_up"], out["ffn2_w_down"]) = _weight_grad(
        x1t, dproj, in_cols, 512, "mix_in_dw", blocks=N_DEV,
        exch=[adamw("ffn2_w_gate_up", own_gu2, r_gu2, N_DEV * m_tiles), adamw("ffn2_w_down", own_d2, r_d2, N_DEV * m_tiles)])
    (dz1, do1, grads["ln1_g"], grads["ln1_b"]), ((s_in,),) = _ln_bwd_call(
        z1, dx1, row(w["ln1_g"]), 0.5, "ln1_bwd", exch=[_rs_sibling([p_in])])
    q_out, own_out = pair(p_out, s_out, "mix_out")
    q_in, own_in = pair(p_in, s_in, "mix_in")
    small_parts = [_rows128(grads[n]) for n in SMALL]
    packed = jnp.concatenate(small_parts + [_rows128(grads["conv_w"]), loss_tile], axis=0)
    p_d1, ((r_in,),) = _weight_grad(h1t, do1, dn, 512, "ffn1_dw_down", exch=[_rs_chips([q_in])])
    p_d1 = p_d1.reshape(N_DEV, f // N_DEV, d)
    (dg1, du1), ((s_d1,), (r_out,), (small_all,)) = _ffn_bwd_act(
        do1, gu1, wd1, "ffn1_bwd_act",
        exch=[_rs_sibling([p_d1]), _rs_chips([q_out]), _small_gather(packed)])
    q_d1, own_d1 = pair(p_d1, s_d1, "ffn1_down")
    p_gu1, ((r_d1,),) = _weight_grad(x0t, dg1, f2s, 512, "ffn1_dw_gate", blocks=N_DEV, exch=[_rs_chips([q_d1])])
    p_gu1, (out["mix_w_in"], out["mix_w_out"]) = _weight_grad(
        x0t, du1, f2s, 512, "ffn1_dw_up", blocks=N_DEV, block_offset=4, into=p_gu1,
        exch=[adamw("mix_w_in", own_in, r_in, 4 * m_tiles), adamw("mix_w_out", own_out, r_out, 4 * m_tiles)])
    (s_gu1,) = _exchange_alone(_rs_sibling([p_gu1]), "ffn1_gate_up_sibling_exchange")
    q_gu1, own_gu1 = pair(p_gu1, s_gu1, "ffn1_gate_up")
    (grad_x,), ((r_gu1,),) = _ffn_bwd_dx(dz1, dg1, du1, wgu1, "ffn1_bwd_dx", exch=[_rs_chips([q_gu1])])
    for n, own, got in (("ffn1_w_down", own_d1, r_d1), ("ffn1_w_gate_up", own_gu1, r_gu1)):
        out[n] = _adamw_sharded(w[n], mom[n], var[n], own, got, "adamw_" + n)

    cw_rows = CONV_TAPS * CONV_CH // 128
    total = _sum_over_devices(small_all)
    offs = [0]
    for p in small_parts:
        offs.append(offs[-1] + p.shape[0])
    n_small = offs[-1]
    loss = total[n_small + cw_rows, 0]
    g_conv_w = lax.dynamic_slice_in_dim(total[n_small:n_small + cw_rows].reshape(CONV_TAPS, CONV_CH),
                                        me * (CONV_CH // N_DEV), CONV_CH // N_DEV, axis=1)
    pad8 = lambda a: jnp.pad(a, ((0, -a.shape[0] % 8), (0, 0)))
    pack = lambda tree, cw: jnp.concatenate([_rows128(tree[n]) for n in SMALL] + [pad8(cw)], axis=0)
    g_pack = jnp.concatenate([total[:n_small], pad8(g_conv_w)], axis=0)
    d_pack, m_pack, v_pack = _adamw_small(pack(w, w["conv_w"]), g_pack, pack(mom, mom["conv_w"]),
                                          pack(var, var["conv_w"]), "adamw_small")
    for k, n in enumerate(SMALL):
        sl = slice(offs[k], offs[k + 1])
        shp = w[n].shape
        out[n] = (total[sl].reshape(shp), d_pack[sl].reshape(shp), m_pack[sl].reshape(shp), v_pack[sl].reshape(shp))
    sl = slice(n_small, n_small + CONV_TAPS)
    out["conv_w"] = (g_conv_w, d_pack[sl], m_pack[sl], v_pack[sl])

    lead = lambda a: a[None]
    res = [loss, grad_x[None]]
    for kind in range(4):
        res += [lead(out[n][kind]) for n in ORDER]
    return tuple(res)
```

```python
import functools
import math

import jax
import jax.numpy as jnp
from jax import lax
from jax.experimental import pallas as pl
from jax.experimental.pallas import tpu as pltpu

F32, BF16 = jnp.float32, jnp.bfloat16
MESH = pl.DeviceIdType.MESH
ANY = pl.BlockSpec(memory_space=pl.ANY)

N_DEV = 8
LN_EPS = 1e-5
ALPHA = 2.0 ** 0.25
CONV_CH = 1024
CONV_TAPS = 31
HALO = 32
HEADS = 8
HEAD_DIM = 128
CHUNK = 128
ADAM_LR, ADAM_B1, ADAM_B2, ADAM_EPS, ADAM_WD, ADAM_STEP = 0.001, 0.9, 0.999, 1e-08, 0.01, 10
V7X_VMEM_LIMIT = 62 * 2 ** 20
EPILOGUE_ROWS = 128

def _cparams(*sem):
    return pltpu.CompilerParams(dimension_semantics=sem, vmem_limit_bytes=V7X_VMEM_LIMIT)


def _tile(n, pref, mult):
    best = None
    for t in range(mult, min(n, pref) + 1, mult):
        if n % t == 0:
            best = t
    return best if best is not None else n


def _dot(a, b):
    return jnp.dot(a, b, preferred_element_type=F32)


def _dot_nt(a, b):
    return lax.dot_general(a, b, (((1,), (1,)), ((), ())), preferred_element_type=F32)


def _sigmoid(x):
    return 1.0 / (1.0 + jnp.exp(-x))


def _ln_stats(z):
    mu = jnp.mean(z, axis=-1, keepdims=True)
    zc = z - mu
    var = jnp.mean(zc * zc, axis=-1, keepdims=True)
    rstd = lax.rsqrt(var + LN_EPS)
    return zc * rstd, rstd


def _ln(z, g, b):
    xh, _ = _ln_stats(z)
    return xh * g + b


def _ln_bwd(dxh, xh, rstd):
    m1 = jnp.mean(dxh, axis=-1, keepdims=True)
    m2 = jnp.mean(dxh * xh, axis=-1, keepdims=True)
    return rstd * (dxh - m1 - xh * m2)


_GK = math.sqrt(2.0 / math.pi)
_GA = 0.044715


def _gelu_and_grad(x):
    x2 = x * x
    t = jnp.tanh(_GK * (x + _GA * x * x2))
    y = 0.5 * x * (1.0 + t)
    dy = 0.5 * (1.0 + t) + 0.5 * x * (1.0 - t * t) * (_GK * (1.0 + 3.0 * _GA * x2))
    return y, dy


def _silu_grad(a):
    s = _sigmoid(a)
    return s * (1.0 + a * (1.0 - s))


def _place():
    return lax.axis_index("x"), lax.axis_index("y"), lax.axis_index("c")


def _other_chips(x, y):
    return [(1 - x, y), (x, 1 - y), (1 - x, 1 - y)]


def _visit_order(x, y, c):
    chips = _other_chips(x, y)
    return [(x, y, c), (x, y, 1 - c), (*chips[0], c), (*chips[1], c), (*chips[0], 1 - c), (*chips[1], 1 - c),
            (*chips[2], c), (*chips[2], 1 - c)]


def _gather_and_gate_up(xb, shards, relayed, order, name):
    n = len(shards)
    N_COPIES = 10
    t, d = xb.shape
    cols = shards[0].shape[1]
    tm = _tile(t, 1024, 128)
    ni = t // tm
    col_major = [True] + [False] * (n - 1)

    def body(order_ref, x_ref, *refs):
        srcs, gu_ref, xt_ref, dsts = refs[:n], refs[n], refs[n + 1], refs[n + 2:2 * n + 2]
        wbuf, send_sems, recv_sems, local_sems, load_sem = refs[2 * n + 2:]
        b, i = pl.program_id(0), pl.program_id(1)
        x, y, c = _place()
        me, sib = (x, y, c), (x, y, 1 - c)
        chips = _other_chips(x, y)

        near_x, near_y, far = chips

        def slot(w, p, band=None):
            half = shards[w].shape[0] // 2
            rows = None if band is None else (band * half, half)
            return _block_slot(dsts[w], col_major[w], shards[w].shape[1], p, rows)

        def copy(w, s, block, to, band=None, from_src=False):
            return pltpu.make_async_remote_copy(
                src_ref=srcs[w] if from_src else slot(w, block, band), dst_ref=slot(w, block, band),
                send_sem=send_sems.at[N_COPIES * w + s], recv_sem=recv_sems.at[N_COPIES * w + s],
                device_id=to, device_id_type=MESH)

        def own(w):
            return pltpu.make_async_copy(srcs[w], slot(w, me), local_sems.at[w])

        def sends(w):
            out = [copy(w, 0, me, sib, from_src=True), copy(w, 1, me, (*near_x, c), from_src=True),
                   copy(w, 2, me, (*near_y, c), from_src=True)]
            if not relayed[w]:
                out.append(copy(w, 3, me, (*far, c), from_src=True))
            return out

        def passed_on(w):
            out = [copy(w, 4, (*near_x, c), sib), copy(w, 5, (*near_y, c), sib)]
            if relayed[w]:
                out += [copy(w, 6, (*far, c), sib, band=0), copy(w, 9, (*far, c), sib, band=1),
                        copy(w, 7, (*near_x, c), (*near_y, c), band=0), copy(w, 8, (*near_y, c), (*near_x, c), band=1)]
            else:
                out.append(copy(w, 6, (*far, c), sib))
            return out

        def start_sends(w):
            own(w).start()
            for cp in sends(w):
                cp.start()

        def got_near_x(w):
            copy(w, 1, (*near_x, c), me).wait_recv()
            copy(w, 4, (*near_x, c), sib).start()
            if relayed[w]:
                copy(w, 7, (*near_x, c), (*near_y, c), band=0).start()

        def got_near_y(w):
            copy(w, 2, (*near_y, c), me).wait_recv()
            copy(w, 5, (*near_y, c), sib).start()
            if relayed[w]:
                copy(w, 8, (*near_y, c), (*near_x, c), band=1).start()

        def got_far(w):
            if relayed[w]:
                copy(w, 7, (*far, c), me, band=0).wait_recv()
                copy(w, 6, (*far, c), sib, band=0).start()
                copy(w, 8, (*far, c), me, band=1).wait_recv()
                copy(w, 9, (*far, c), sib, band=1).start()
            else:
                copy(w, 3, (*far, c), me).wait_recv()
                copy(w, 6, (*far, c), sib).start()

        def got_from_sibling(w, which):
            if which == 0:
                copy(w, 0, sib, me).wait_recv()
            elif which == 3 and relayed[w]:
                copy(w, 6, (*far, 1 - c), me, band=0).wait_recv()
                copy(w, 9, (*far, 1 - c), me, band=1).wait_recv()
            else:
                copy(w, 3 + which, (*chips[which - 1], 1 - c), me).wait_recv()

        others = range(1, n)

        def arrive(k):
            if k == 0:
                own(0).wait()
            elif k == 1:
                got_from_sibling(0, 0)
            elif k == 2:
                got_near_x(0)
                for w in others:
                    start_sends(w)
            elif k == 3:
                got_near_y(0)
            elif k in (4, 5):
                got_from_sibling(0, k - 3)
            elif k == 6:
                got_far(0)
                for w in others:
                    got_near_x(w)
                    got_near_y(w)
            else:
                got_from_sibling(0, 3)
                for w in others:
                    got_far(w)

        def load(k):
            at = pl.multiple_of(order_ref[k] * cols, 128)
            return pltpu.make_async_copy(dsts[0].at[:, pl.ds(at, cols)], wbuf.at[k % 2], load_sem.at[k % 2])

        @pl.when((b == 0) & (i == 0))
        def _():
            start_sends(0)
            arrive(0)
            load(0).start()
            load(0).wait()

        early = ni - 1
        for k in range(1, N_DEV):
            @pl.when((b == k - 1) & (i == early))
            def _(k=k):
                arrive(k)
                load(k).start()

            @pl.when((b == k) & (i == 0))
            def _(k=k):
                load(k).wait()

        gu_ref[...] = _dot(x_ref[...].astype(BF16), wbuf[b % 2]).astype(BF16)

        @pl.when(b == 0)
        def _():
            xt_ref[...] = x_ref[...].T.astype(BF16)

        @pl.when((b == N_DEV - 1) & (i == ni - 1))
        def _():
            for w in others:
                for which in range(4):
                    got_from_sibling(w, which)
                own(w).wait()
            for w in range(n):
                for cp in sends(w) + passed_on(w):
                    cp.wait_send()

    grid_spec = pltpu.PrefetchScalarGridSpec(
        num_scalar_prefetch=1, grid=(N_DEV, ni),
        in_specs=[pl.BlockSpec((tm, d), lambda b, i, o: (i, 0))] + [ANY] * n,
        out_specs=[pl.BlockSpec((tm, cols), lambda b, i, o: (i, o[b])),
                   pl.BlockSpec((d, tm), lambda b, i, o: (0, jnp.where(b == 0, i, ni - 1)))] + [ANY] * n,
        scratch_shapes=[pltpu.VMEM((2, d, cols), BF16), pltpu.SemaphoreType.DMA((N_COPIES * n,)),
                        pltpu.SemaphoreType.DMA((N_COPIES * n,)), pltpu.SemaphoreType.DMA((n,)),
                        pltpu.SemaphoreType.DMA((2,))])
    res = pl.pallas_call(
        body, name=name, grid_spec=grid_spec,
        out_shape=[jax.ShapeDtypeStruct((t, N_DEV * cols), BF16), jax.ShapeDtypeStruct((d, t), BF16)]
        + [_gathered_shape(s, cm) for s, cm in zip(shards, col_major)],
        compiler_params=_cparams("arbitrary", "arbitrary"),
    )(order, xb, *shards)
    return res[0], res[1], res[2:]


class _Exchange:
    def __init__(self, ins, io, new, n_sems, n_local, make):
        self.ins, self.io, self.new = list(ins), list(io), list(new)
        self.n_sems, self.n_local, self.make = n_sems, n_local, make


class _Staged(_Exchange):
    def __init__(self, ins, io, new, n_sems, n_local, start, mid, finish):
        super().__init__(ins, io, new, n_sems, n_local, None)
        self.start, self.mid, self.finish = start, mid, finish


def _gather_whole(shards, col_major, rows=None, into=None):
    n = len(shards)
    new = [] if into is not None else [_gathered_shape(s, cm) for s, cm in zip(shards, col_major)]

    def copies(in_refs, gathered, send_sems, recv_sems, local_sems):
        x, y, c = _place()
        me, sib = (x, y, c), (x, y, 1 - c)
        chips = _other_chips(x, y)
        srcs = [r if rows is None else r.at[pl.ds(rows[0], rows[1])] for r in in_refs]

        def slot(w, p):
            return _block_slot(gathered[w], col_major[w], shards[w].shape[1], p, rows)

        def copy(w, s, block, to, from_src=False):
            return pltpu.make_async_remote_copy(
                src_ref=srcs[w] if from_src else slot(w, block), dst_ref=slot(w, block),
                send_sem=send_sems.at[7 * w + s], recv_sem=recv_sems.at[7 * w + s], device_id=to, device_id_type=MESH)

        own = [pltpu.make_async_copy(srcs[w], slot(w, me), local_sems.at[w]) for w in range(n)]
        first = [[copy(w, 0, me, sib, True)] + [copy(w, 1 + j, me, (*chip, c), True) for j, chip in enumerate(chips)]
                 for w in range(n)]
        landed = [[copy(w, 1 + j, (*chip, c), me) for j, chip in enumerate(chips)] for w in range(n)]
        onward = [[copy(w, 4 + j, (*chip, c), sib) for j, chip in enumerate(chips)] for w in range(n)]
        from_sibling = [[copy(w, 0, sib, me)] + [copy(w, 4 + j, (*chip, 1 - c), me) for j, chip in enumerate(chips)]
                        for w in range(n)]
        return own, first, landed, onward, from_sibling

    def start(in_refs, io_refs, new_refs, *sems):
        own, first, _, _, _ = copies(in_refs, io_refs if into is not None else new_refs, *sems)
        for cp in own + sum(first, []):
            cp.start()

    def mid(in_refs, io_refs, new_refs, *sems):
        _, _, landed, onward, _ = copies(in_refs, io_refs if into is not None else new_refs, *sems)
        for w in range(n):
            for got, fwd in zip(landed[w], onward[w]):
                got.wait_recv()
                fwd.start()

    def finish(in_refs, io_refs, new_refs, *sems):
        own, first, _, onward, from_sibling = copies(in_refs, io_refs if into is not None else new_refs, *sems)
        for cp in sum(from_sibling, []):
            cp.wait_recv()
        for cp in sum(first, []) + sum(onward, []):
            cp.wait_send()
        for cp in own:
            cp.wait()

    return _Staged(shards, into or [], new, 7 * n, n, start, mid, finish)


def _block_slot(ref, col_major, cols, place, rows=None):
    k = 4 * place[0] + 2 * place[1] + place[2]
    band = slice(None) if rows is None else pl.ds(rows[0], rows[1])
    if col_major:
        return ref.at[band, pl.ds(pl.multiple_of(k * cols, 128), cols)]
    return ref.at[k] if rows is None else ref.at[k, band]


def _gathered_shape(s, col_major):
    return jax.ShapeDtypeStruct((s.shape[0], N_DEV * s.shape[1]) if col_major else (N_DEV,) + s.shape, s.dtype)


def _gather_first(shards, col_major, rows=None, into=None):
    n = len(shards)
    new = [] if into is not None else [_gathered_shape(s, cm) for s, cm in zip(shards, col_major)]

    def make(in_refs, io_refs, new_refs, send_sems, recv_sems, local_sems, base=0, local_base=0):
        x, y, c = _place()
        targets = [(x, y, 1 - c)] + [(*chip, c) for chip in _other_chips(x, y)]
        gathered = io_refs if into is not None else new_refs
        copies = []
        for w in range(n):
            src = in_refs[w] if rows is None else in_refs[w].at[pl.ds(rows[0], rows[1])]
            slot = _block_slot(gathered[w], col_major[w], shards[w].shape[1], (x, y, c), rows)
            copies.append(pltpu.make_async_copy(src, slot, local_sems.at[local_base + w]))
            for s, to in enumerate(targets):
                copies.append(pltpu.make_async_remote_copy(
                    src_ref=src, dst_ref=slot, send_sem=send_sems.at[base + 4 * w + s],
                    recv_sem=recv_sems.at[base + 4 * w + s], device_id=to, device_id_type=MESH))
        return copies

    return _Exchange(shards, into or [], new, 4 * n, n, make)


def _gather_forward(gathered, col_major, cols, rows=None):
    n = len(gathered)

    def make(in_refs, io_refs, new_refs, send_sems, recv_sems, local_sems, base=0, local_base=0):
        x, y, c = _place()
        copies = []
        for w in range(n):
            for j, chip in enumerate(_other_chips(x, y)):
                slot = _block_slot(io_refs[w], col_major[w], cols[w], (*chip, c), rows)
                copies.append(pltpu.make_async_remote_copy(
                    src_ref=slot, dst_ref=slot, send_sem=send_sems.at[base + 3 * w + j],
                    recv_sem=recv_sems.at[base + 3 * w + j], device_id=(x, y, 1 - c), device_id_type=MESH))
        return copies

    return _Exchange([], gathered, [], 3 * n, 0, make)


def _both(a, b):
    def make(in_refs, io_refs, new_refs, send_sems, recv_sems, local_sems):
        na = len(a.ins)
        return (a.make(in_refs[:na], io_refs, [], send_sems, recv_sems, local_sems, 0, 0)
                + b.make(in_refs[na:], io_refs, [], send_sems, recv_sems, local_sems, a.n_sems, a.n_local))

    return _Exchange(a.ins + b.ins, a.io, [], a.n_sems + b.n_sems, a.n_local + b.n_local, make)


def _rs_sibling(parts):
    n = len(parts)

    def make(in_refs, io_refs, new_refs, send_sems, recv_sems, local_sems):
        x, y, c = _place()
        copies = []
        for w in range(n):
            for j in range(4):
                copies.append(pltpu.make_async_remote_copy(
                    src_ref=in_refs[w].at[2 * j + (1 - c)], dst_ref=new_refs[w].at[j],
                    send_sem=send_sems.at[4 * w + j], recv_sem=recv_sems.at[4 * w + j],
                    device_id=(x, y, 1 - c), device_id_type=MESH))
        return copies

    return _Exchange(parts, [], [jax.ShapeDtypeStruct((4,) + p.shape[1:], p.dtype) for p in parts], 4 * n, 0, make)


def _rs_chips(chip_parts, rows=None, into=None):
    n = len(chip_parts)
    band = slice(None) if rows is None else pl.ds(rows[0], rows[1])
    new = [] if into is not None else [jax.ShapeDtypeStruct((3,) + p.shape[1:], p.dtype) for p in chip_parts]

    def make(in_refs, io_refs, new_refs, send_sems, recv_sems, local_sems):
        x, y, c = _place()
        landing = io_refs if into is not None else new_refs
        copies = []
        for w in range(n):
            for rel, (px, py) in enumerate(_other_chips(x, y)):
                copies.append(pltpu.make_async_remote_copy(
                    src_ref=in_refs[w].at[2 * px + py, band], dst_ref=landing[w].at[rel, band],
                    send_sem=send_sems.at[3 * w + rel], recv_sem=recv_sems.at[3 * w + rel],
                    device_id=(px, py, c), device_id_type=MESH))
        return copies

    return _Exchange(chip_parts, into or [], new, 3 * n, 0, make)


class _Side:
    def __init__(self, ins, in_blocks, out_shapes, out_blocks, n_tiles, fn):
        self.ins, self.in_blocks, self.out_shapes, self.out_blocks = list(ins), in_blocks, list(out_shapes), out_blocks
        self.n_tiles, self.fn = n_tiles, fn


def _call(body, exch, *, name, grid, in_specs, out_specs, out_shape, scratch_shapes=(), semantics,
          input_output_aliases=None):
    exch = list(exch)
    in_specs, out_specs, out_shape = list(in_specs), list(out_specs), list(out_shape)
    scratch_shapes = list(scratch_shapes)
    if not exch:
        fn = pl.pallas_call(body, name=name, grid=grid, in_specs=in_specs, out_specs=out_specs, out_shape=out_shape,
                            scratch_shapes=scratch_shapes, input_output_aliases=input_output_aliases or {},
                            compiler_params=_cparams(*semantics))
        return lambda *args: (fn(*args), [])
    n_in, n_out, n_scr = len(in_specs), len(out_specs), len(scratch_shapes)
    aliases = dict(input_output_aliases or {})
    all_in, all_out_specs, all_out_shape, all_scr = list(in_specs), list(out_specs), list(out_shape), list(scratch_shapes)
    extra_args = []

    def step(idx):
        s = idx[0]
        for a in range(1, len(grid)):
            s = s * grid[a] + idx[a]
        return s

    def tile_spec(shape, where, n_tiles):
        return pl.BlockSpec(shape, lambda *idx: where(jnp.minimum(step(idx), n_tiles - 1)))

    for ex in exch:
        if isinstance(ex, _Side):
            all_in += [tile_spec(shape, where, ex.n_tiles) for shape, where in ex.in_blocks]
            extra_args += ex.ins
            all_out_specs += [tile_spec(shape, where, ex.n_tiles) for shape, where in ex.out_blocks]
            all_out_shape += ex.out_shapes
            continue
        for k, a in enumerate(ex.io):
            aliases[len(all_in) + len(ex.ins) + k] = len(all_out_specs) + k
        all_in += [ANY] * (len(ex.ins) + len(ex.io))
        extra_args += ex.ins + ex.io
        all_out_specs += [ANY] * (len(ex.io) + len(ex.new))
        all_out_shape += [jax.ShapeDtypeStruct(a.shape, a.dtype) for a in ex.io] + ex.new
        all_scr += [pltpu.SemaphoreType.DMA((ex.n_sems,)), pltpu.SemaphoreType.DMA((ex.n_sems,)),
                    pltpu.SemaphoreType.DMA((max(ex.n_local, 1),))]

    n_ins = [len(ex.ins) if isinstance(ex, _Side) else len(ex.ins) + len(ex.io) for ex in exch]
    n_outs = [len(ex.out_shapes) if isinstance(ex, _Side) else len(ex.io) + len(ex.new) for ex in exch]

    def wrapped(*refs):
        pos = n_in
        ex_in = []
        for k in n_ins:
            ex_in.append(refs[pos:pos + k])
            pos += k
        outs = refs[pos:pos + n_out]
        pos += n_out
        ex_out = []
        for k in n_outs:
            ex_out.append(refs[pos:pos + k])
            pos += k
        scr = refs[pos:pos + n_scr]
        pos += n_scr
        idx = [pl.program_id(a) for a in range(len(grid))]
        first = functools.reduce(jnp.logical_and, [i == 0 for i in idx])
        last = functools.reduce(jnp.logical_and, [i == g - 1 for i, g in zip(idx, grid)])

        def exchanges():
            out, at = [], pos
            for ex, ei, eo in zip(exch, ex_in, ex_out):
                if not isinstance(ex, _Side):
                    out.append((ex, (ei[:len(ex.ins)], eo[:len(ex.io)], eo[len(ex.io):], *refs[at:at + 3])))
                    at += 3
            return out

        @pl.when(first)
        def _():
            for ex, r in exchanges():
                if isinstance(ex, _Staged):
                    ex.start(*r)
                else:
                    for cp in ex.make(*r):
                        cp.start()

        body(*refs[:n_in], *outs, *scr)
        for ex, ei, eo in zip(exch, ex_in, ex_out):
            if isinstance(ex, _Side):
                pl.when(step(idx) < ex.n_tiles)(functools.partial(ex.fn, ei, eo))

        if any(isinstance(ex, _Staged) for ex in exch):
            @pl.when(step(idx) == (math.prod(grid) * 3) // 4)
            def _():
                for ex, r in exchanges():
                    if isinstance(ex, _Staged):
                        ex.mid(*r)

        @pl.when(last)
        def _():
            for ex, r in exchanges():
                if isinstance(ex, _Staged):
                    ex.finish(*r)
                else:
                    for cp in ex.make(*r):
                        cp.wait()

    fn = pl.pallas_call(wrapped, name=name, grid=grid, in_specs=all_in, out_specs=all_out_specs,
                        out_shape=all_out_shape, scratch_shapes=all_scr, input_output_aliases=aliases,
                        compiler_params=_cparams(*(["arbitrary"] * len(grid))))

    def run(*args):
        res = fn(*args, *extra_args)
        outs, pos, ex_res = res[:n_out], n_out, []
        for k in n_outs:
            ex_res.append(list(res[pos:pos + k]))
            pos += k
        return outs, ex_res

    return run


def _exchange_alone(ex, name):
    def body():
        pass

    _, res = _call(body, [ex], name=name, grid=(1,), in_specs=[], out_specs=[], out_shape=[], semantics=("arbitrary",))()
    return res[0]


def _small_gather(part):
    def make(in_refs, io_refs, new_refs, send_sems, recv_sems, local_sems):
        x, y, c = _place()
        slot = new_refs[0].at[4 * x + 2 * y + c]
        copies = [pltpu.make_async_copy(in_refs[0], slot, local_sems.at[0])]
        for d in range(1, N_DEV):
            peer = (1 - x if d & 4 else x, 1 - y if d & 2 else y, 1 - c if d & 1 else c)
            copies.append(pltpu.make_async_remote_copy(
                src_ref=in_refs[0], dst_ref=slot, send_sem=send_sems.at[d - 1], recv_sem=recv_sems.at[d - 1],
                device_id=peer, device_id_type=MESH))
        return copies

    return _Exchange([part], [], [jax.ShapeDtypeStruct((N_DEV,) + part.shape, part.dtype)], N_DEV - 1, 1, make)


def _sum_over_devices(parts):
    _, rows, lanes = parts.shape

    def body(p_ref, o_ref):
        acc = p_ref[0]
        for k in range(1, N_DEV):
            acc = acc + p_ref[k]
        o_ref[...] = acc

    return pl.pallas_call(
        body, name="small_grads_sum", grid=(1,), out_shape=jax.ShapeDtypeStruct((rows, lanes), F32),
        in_specs=[pl.BlockSpec((N_DEV, rows, lanes), lambda i: (0, 0, 0))],
        out_specs=pl.BlockSpec((rows, lanes), lambda i: (0, 0)),
        compiler_params=_cparams("arbitrary"),
    )(parts)


def _transpose_bf16(a, name, exch=(), with_copy=False):
    r, c = a.shape
    tr, tc = _tile(r, 512, 128), _tile(c, 512, 128)

    def body(a_ref, o_ref, *copy_ref):
        v = a_ref[...].astype(F32)
        o_ref[...] = v.T.astype(BF16)
        if with_copy:
            copy_ref[0][...] = v.astype(BF16)

    outs, ex = _call(
        body, exch, name=name, grid=(r // tr, c // tc),
        out_shape=[jax.ShapeDtypeStruct((c, r), BF16)] + [jax.ShapeDtypeStruct((r, c), BF16)] * with_copy,
        in_specs=[pl.BlockSpec((tr, tc), lambda i, j: (i, j))],
        out_specs=[pl.BlockSpec((tc, tr), lambda i, j: (j, i))] + [pl.BlockSpec((tr, tc), lambda i, j: (i, j))] * with_copy,
        semantics=("parallel", "parallel"),
    )(a)
    return (outs if with_copy else outs[0]), ex


def _ffn_fwd_loss(x, wgu, wd, ln_g, ln_b, target, name, exch=()):
    t, d = x.shape
    f = wd.shape[0]
    tm, tf = _tile(t, 512, 128), _tile(f, 512, 128)
    nf = f // tf

    def body(x_ref, wg_ref, wu_ref, wd_ref, lg_ref, lb_ref, t_ref,
             go_ref, uo_ref, ht_ref, dz_ref, dzb_ref, dlg_ref, dlb_ref, loss_ref, xb, acc):
        i, j = pl.program_id(0), pl.program_id(1)

        @pl.when(j == 0)
        def _():
            xb[...] = x_ref[...].astype(BF16)
            acc[...] = jnp.zeros_like(acc)

        @pl.when((i == 0) & (j == 0))
        def _():
            dlg_ref[...] = jnp.zeros_like(dlg_ref)
            dlb_ref[...] = jnp.zeros_like(dlb_ref)
            loss_ref[...] = jnp.zeros_like(loss_ref)

        g = _dot(xb[...], wg_ref[...])
        u = _dot(xb[...], wu_ref[...])
        h = g * _sigmoid(g) * u
        go_ref[...] = g.astype(BF16)
        uo_ref[...] = u.astype(BF16)
        ht_ref[...] = h.T.astype(BF16)
        acc[...] += _dot(h.astype(BF16), wd_ref[...])

        @pl.when(j == nf - 1)
        def _():
            for r in range(0, tm, EPILOGUE_ROWS):
                rows = slice(r, r + EPILOGUE_ROWS)
                xh, rstd = _ln_stats(ALPHA * x_ref[rows, :] + 0.5 * acc[rows, :])
                e = xh * lg_ref[...] + lb_ref[...] - t_ref[rows, :]
                loss_ref[...] += 0.5 * jnp.sum(jnp.sum(e * e, axis=-1, keepdims=True) * (1.0 / d), axis=0,
                                               keepdims=True)
                dy = e * (1.0 / d)
                dz = _ln_bwd(dy * lg_ref[...], xh, rstd)
                dz_ref[rows, :] = dz
                dzb_ref[rows, :] = (0.5 * dz).astype(BF16)
                dlg_ref[...] += jnp.sum(dy * xh, axis=0, keepdims=True)
                dlb_ref[...] += jnp.sum(dy, axis=0, keepdims=True)

    row = lambda i, j: (i, 0)
    fixed = lambda i, j: (0, 0)
    return _call(
        body, exch, name=name, grid=(t // tm, nf),
        out_shape=[jax.ShapeDtypeStruct((t, f), BF16), jax.ShapeDtypeStruct((t, f), BF16),
                   jax.ShapeDtypeStruct((f, t), BF16), jax.ShapeDtypeStruct((t, d), F32),
                   jax.ShapeDtypeStruct((t, d), BF16), jax.ShapeDtypeStruct((1, d), F32),
                   jax.ShapeDtypeStruct((1, d), F32), jax.ShapeDtypeStruct((8, 128), F32)],
        in_specs=[pl.BlockSpec((tm, d), row),
                  pl.BlockSpec((d, tf), lambda i, j: (0, j)),
                  pl.BlockSpec((d, tf), lambda i, j: (0, j + nf)),
                  pl.BlockSpec((tf, d), lambda i, j: (j, 0)),
                  pl.BlockSpec((1, d), fixed), pl.BlockSpec((1, d), fixed), pl.BlockSpec((tm, d), row)],
        out_specs=[pl.BlockSpec((tm, tf), lambda i, j: (i, j)), pl.BlockSpec((tm, tf), lambda i, j: (i, j)),
                   pl.BlockSpec((tf, tm), lambda i, j: (j, i)), pl.BlockSpec((tm, d), row), pl.BlockSpec((tm, d), row),
                   pl.BlockSpec((1, d), fixed), pl.BlockSpec((1, d), fixed), pl.BlockSpec((8, 128), fixed)],
        scratch_shapes=[pltpu.VMEM((tm, d), BF16), pltpu.VMEM((tm, d), F32)],
        semantics=("arbitrary", "arbitrary"),
    )(x, wgu, wgu, wd, ln_g, ln_b, target)


def _ffn_down_fwd(gu, x, wd, ln_g, ln_b, name, exch=()):
    t, d = x.shape
    f = wd.shape[0]
    tm, tf = _tile(t, 512, 128), _tile(f, 512, 128)
    nf = f // tf

    def body(g_ref, u_ref, wd_ref, x_ref, lg_ref, lb_ref, ht_ref, z_ref, xn_ref, xnt_ref, acc):
        j = pl.program_id(1)

        @pl.when(j == 0)
        def _():
            acc[...] = jnp.zeros_like(acc)

        g = g_ref[...].astype(F32)
        h = g * _sigmoid(g) * u_ref[...].astype(F32)
        ht_ref[...] = h.T.astype(BF16)
        acc[...] += _dot(h.astype(BF16), wd_ref[...])

        @pl.when(j == nf - 1)
        def _():
            z = ALPHA * x_ref[...] + 0.5 * acc[...]
            z_ref[...] = z
            xn = _ln(z, lg_ref[...], lb_ref[...])
            xn_ref[...] = xn
            xnt_ref[...] = xn.T.astype(BF16)

    row = lambda i, j: (i, 0)
    fixed = lambda i, j: (0, 0)
    return _call(
        body, exch, name=name, grid=(t // tm, nf),
        out_shape=[jax.ShapeDtypeStruct((f, t), BF16), jax.ShapeDtypeStruct((t, d), F32),
                   jax.ShapeDtypeStruct((t, d), F32), jax.ShapeDtypeStruct((d, t), BF16)],
        in_specs=[pl.BlockSpec((tm, tf), lambda i, j: (i, j)), pl.BlockSpec((tm, tf), lambda i, j: (i, j + nf)),
                  pl.BlockSpec((tf, d), lambda i, j: (j, 0)), pl.BlockSpec((tm, d), row),
                  pl.BlockSpec((1, d), fixed), pl.BlockSpec((1, d), fixed)],
        out_specs=[pl.BlockSpec((tf, tm), lambda i, j: (j, i)), pl.BlockSpec((tm, d), row), pl.BlockSpec((tm, d), row),
                   pl.BlockSpec((d, tm), lambda i, j: (0, i))],
        scratch_shapes=[pltpu.VMEM((tm, d), F32)],
        semantics=("parallel", "arbitrary"),
    )(gu, gu, wd, x, ln_g, ln_b)


def _ffn_act_grads(dh, g_ref, u_ref):
    gg = g_ref[...].astype(F32)
    uu = u_ref[...].astype(F32)
    s = _sigmoid(gg)
    du = (dh * (gg * s)).astype(BF16)
    dg = (dh * uu * (s * (1.0 + gg * (1.0 - s)))).astype(BF16)
    return dg, du


def _ffn_bwd(dz, do, g, u, wgu, wd, name, exch=()):
    t, d = dz.shape
    f = wd.shape[0]
    tm, tf = _tile(t, 512, 128), _tile(f, 512, 128)
    nf = f // tf

    def body(dz_ref, do_ref, g_ref, u_ref, wg_ref, wu_ref, wd_ref, dg_ref, du_ref, dx_ref, acc):
        j = pl.program_id(1)

        @pl.when(j == 0)
        def _():
            acc[...] = jnp.zeros_like(acc)

        dg, du = _ffn_act_grads(_dot_nt(do_ref[...], wd_ref[...]), g_ref, u_ref)
        dg_ref[...] = dg
        du_ref[...] = du
        acc[...] += _dot_nt(dg, wg_ref[...]) + _dot_nt(du, wu_ref[...])

        @pl.when(j == nf - 1)
        def _():
            dx_ref[...] = ALPHA * dz_ref[...] + acc[...]

    row = lambda i, j: (i, 0)
    tile = lambda i, j: (i, j)
    return _call(
        body, exch, name=name, grid=(t // tm, nf),
        out_shape=[jax.ShapeDtypeStruct((t, f), BF16), jax.ShapeDtypeStruct((t, f), BF16),
                   jax.ShapeDtypeStruct((t, d), F32)],
        in_specs=[pl.BlockSpec((tm, d), row), pl.BlockSpec((tm, d), row),
                  pl.BlockSpec((tm, tf), tile), pl.BlockSpec((tm, tf), tile),
                  pl.BlockSpec((d, tf), lambda i, j: (0, j)),
                  pl.BlockSpec((d, tf), lambda i, j: (0, j + nf)),
                  pl.BlockSpec((tf, d), lambda i, j: (j, 0))],
        out_specs=[pl.BlockSpec((tm, tf), tile), pl.BlockSpec((tm, tf), tile), pl.BlockSpec((tm, d), row)],
        scratch_shapes=[pltpu.VMEM((tm, d), F32)],
        semantics=("parallel", "arbitrary"),
    )(dz, do, g, u, wgu, wgu, wd)


def _ffn_bwd_act(do, gu, wd, name, exch=()):
    t, d = do.shape
    f = wd.shape[0]
    tm, tf = _tile(t, 2048, 128), _tile(f, 512, 128)
    nf = f // tf

    def body(do_ref, g_ref, u_ref, wd_ref, dg_ref, du_ref):
        dg, du = _ffn_act_grads(_dot_nt(do_ref[...], wd_ref[...]), g_ref, u_ref)
        dg_ref[...] = dg
        du_ref[...] = du

    tile = lambda i, j: (i, j)
    return _call(
        body, exch, name=name, grid=(t // tm, f // tf),
        out_shape=[jax.ShapeDtypeStruct((t, f), BF16), jax.ShapeDtypeStruct((t, f), BF16)],
        in_specs=[pl.BlockSpec((tm, d), lambda i, j: (i, 0)), pl.BlockSpec((tm, tf), tile),
                  pl.BlockSpec((tm, tf), lambda i, j: (i, j + nf)), pl.BlockSpec((tf, d), lambda i, j: (j, 0))],
        out_specs=[pl.BlockSpec((tm, tf), tile), pl.BlockSpec((tm, tf), tile)],
        semantics=("parallel", "parallel"),
    )(do, gu, gu, wd)


def _ffn_bwd_dx(dz, dg, du, wgu, name, exch=()):
    t, d = dz.shape
    f = dg.shape[1]
    tm, tn = _tile(t, 512, 128), _tile(d, 256, 128)

    def body(dz_ref, dg_ref, du_ref, wg_ref, wu_ref, dx_ref):
        dx_ref[...] = ALPHA * dz_ref[...] + _dot_nt(dg_ref[...], wg_ref[...]) + _dot_nt(du_ref[...], wu_ref[...])

    row = lambda i, n: (i, 0)
    tile = lambda i, n: (i, n)
    return _call(
        body, exch, name=name, grid=(t // tm, d // tn), out_shape=[jax.ShapeDtypeStruct((t, d), F32)],
        in_specs=[pl.BlockSpec((tm, tn), tile), pl.BlockSpec((tm, f), row), pl.BlockSpec((tm, f), row),
                  pl.BlockSpec((tn, f), lambda i, n: (n, 0)), pl.BlockSpec((tn, f), lambda i, n: (n, 1))],
        out_specs=[pl.BlockSpec((tm, tn), tile)],
        semantics=("parallel", "arbitrary"),
    )(dz, dg, du, wgu, wgu)


def _weight_grad(at, b, tn, tmm, name, blocks=None, block_offset=0, into=None, exch=()):
    m, t = at.shape
    nn = b.shape[1]
    tmm = _tile(m, tmm, 16)
    assert nn % tn == 0

    def body(*refs):
        at_ref, b_ref, o_ref = refs[0], refs[1], refs[-1]
        r = _dot(at_ref[...], b_ref[...]).astype(BF16)
        if blocks is None:
            o_ref[...] = r
        else:
            o_ref[0] = r

    in_specs = [pl.BlockSpec((tmm, t), lambda n, i: (i, 0)), pl.BlockSpec((t, tn), lambda n, i: (0, n))]
    args = [at, b]
    aliases = {}
    if into is not None:
        in_specs.append(ANY)
        args.append(into)
        aliases = {2: 0}
    if blocks is None:
        out_shape = jax.ShapeDtypeStruct((m, nn), BF16)
        out_spec = pl.BlockSpec((tmm, tn), lambda n, i: (i, n))
    else:
        out_shape = jax.ShapeDtypeStruct((blocks, m, tn), BF16)
        out_spec = pl.BlockSpec((1, tmm, tn), lambda n, i: (n + block_offset, i, 0))
    (out,), ex = _call(
        body, exch, name=name, grid=(nn // tn, m // tmm), out_shape=[out_shape],
        in_specs=in_specs, out_specs=[out_spec], input_output_aliases=aliases,
        semantics=("parallel", "parallel"),
    )(*args)
    return out, ex


def _mix_in_proj(x, w_in, name, exch=()):
    t, d = x.shape
    n_out = w_in.shape[1]
    tm, cb = _tile(t, 512, 128), _tile(n_out, 1024, 128)

    def body(x_ref, w_ref, o_ref, xb):
        @pl.when(pl.program_id(1) == 0)
        def _():
            xb[...] = x_ref[...].astype(BF16)

        o_ref[...] = _dot(xb[...], w_ref[...])

    (out,), ex = _call(
        body, exch, name=name, grid=(t // tm, n_out // cb), out_shape=[jax.ShapeDtypeStruct((t, n_out), F32)],
        in_specs=[pl.BlockSpec((tm, d), lambda i, k: (i, 0)), pl.BlockSpec((d, cb), lambda i, k: (0, k))],
        out_specs=[pl.BlockSpec((tm, cb), lambda i, k: (i, k))],
        scratch_shapes=[pltpu.VMEM((tm, d), BF16)],
        semantics=("parallel", "arbitrary"),
    )(x, w_in)
    return out, ex


def _mix_in_bwd(dproj, w_in, dz, name, exch=()):
    t, d = dz.shape
    kk = w_in.shape[1]
    tm, tn = _tile(t, 512, 128), _tile(d, 512, 128)

    def body(dp_ref, w_ref, dz_ref, dx_ref):
        dx_ref[...] = ALPHA * dz_ref[...] + _dot_nt(dp_ref[...], w_ref[...])

    (out,), ex = _call(
        body, exch, name=name, grid=(t // tm, d // tn), out_shape=[jax.ShapeDtypeStruct((t, d), F32)],
        in_specs=[pl.BlockSpec((tm, kk), lambda i, n: (i, 0)), pl.BlockSpec((tn, kk), lambda i, n: (n, 0)),
                  pl.BlockSpec((tm, tn), lambda i, n: (i, n))],
        out_specs=[pl.BlockSpec((tm, tn), lambda i, n: (i, n))],
        semantics=("parallel", "arbitrary"),
    )(dproj, w_in, dz)
    return out, ex


def _mix_out_fwd(y, w_out, x, ln_g, ln_b, name, exch=()):
    t, d = x.shape
    kk = y.shape[1]
    tm = _tile(t, 256, 128)

    def body(y_ref, w_ref, x_ref, g_ref, b_ref, z_ref, xn_ref, xnt_ref):
        z = ALPHA * x_ref[...] + _dot(y_ref[...], w_ref[...])
        z_ref[...] = z
        xn = _ln(z, g_ref[...], b_ref[...])
        xn_ref[...] = xn
        xnt_ref[...] = xn.T.astype(BF16)

    row = lambda i: (i, 0)
    fixed = lambda i: (0, 0)
    return _call(
        body, exch, name=name, grid=(t // tm,),
        out_shape=[jax.ShapeDtypeStruct((t, d), F32), jax.ShapeDtypeStruct((t, d), F32),
                   jax.ShapeDtypeStruct((d, t), BF16)],
        in_specs=[pl.BlockSpec((tm, kk), row), pl.BlockSpec((kk, d), fixed), pl.BlockSpec((tm, d), row),
                  pl.BlockSpec((1, d), fixed), pl.BlockSpec((1, d), fixed)],
        out_specs=[pl.BlockSpec((tm, d), row), pl.BlockSpec((tm, d), row), pl.BlockSpec((d, tm), lambda i: (0, i))],
        semantics=("parallel",),
    )(y, w_out, x, ln_g, ln_b)


def _mix_out_bwd(dzb, w_out, name, exch=()):
    t, d = dzb.shape
    kk = w_out.shape[0]
    tm = _tile(t, 512, 128)

    def body(dz_ref, w_ref, dy_ref):
        dy_ref[...] = _dot_nt(dz_ref[...], w_ref[...])

    (out,), ex = _call(
        body, exch, name=name, grid=(t // tm,), out_shape=[jax.ShapeDtypeStruct((t, kk), F32)],
        in_specs=[pl.BlockSpec((tm, d), lambda i: (i, 0)), pl.BlockSpec((kk, d), lambda i: (0, 0))],
        out_specs=[pl.BlockSpec((tm, kk), lambda i: (i, 0))],
        semantics=("parallel",),
    )(dzb, w_out)
    return out, ex


def _loss_ln_bwd(z, target, ln_g, ln_b, bf16_scale, name):
    t, d = z.shape
    tm = _tile(t, 512, 8)

    def body(z_ref, t_ref, g_ref, b_ref, dz_ref, dzb_ref, dg_ref, db_ref, loss_ref):
        @pl.when(pl.program_id(0) == 0)
        def _():
            dg_ref[...] = jnp.zeros_like(dg_ref)
            db_ref[...] = jnp.zeros_like(db_ref)
            loss_ref[...] = jnp.zeros_like(loss_ref)

        xh, rstd = _ln_stats(z_ref[...])
        e = xh * g_ref[...] + b_ref[...] - t_ref[...]
        loss_ref[...] += 0.5 * jnp.sum(jnp.sum(e * e, axis=-1, keepdims=True) * (1.0 / d), axis=0, keepdims=True)
        dy = e * (1.0 / d)
        dz = _ln_bwd(dy * g_ref[...], xh, rstd)
        dz_ref[...] = dz
        dzb_ref[...] = (bf16_scale * dz).astype(BF16)
        dg_ref[...] += jnp.sum(dy * xh, axis=0, keepdims=True)
        db_ref[...] += jnp.sum(dy, axis=0, keepdims=True)

    row = lambda i: (i, 0)
    fixed = lambda i: (0, 0)
    return pl.pallas_call(
        body, name=name, grid=(t // tm,),
        out_shape=[jax.ShapeDtypeStruct((t, d), F32), jax.ShapeDtypeStruct((t, d), BF16),
                   jax.ShapeDtypeStruct((1, d), F32), jax.ShapeDtypeStruct((1, d), F32),
                   jax.ShapeDtypeStruct((8, 128), F32)],
        in_specs=[pl.BlockSpec((tm, d), row), pl.BlockSpec((tm, d), row), pl.BlockSpec((1, d), fixed),
                  pl.BlockSpec((1, d), fixed)],
        out_specs=[pl.BlockSpec((tm, d), row), pl.BlockSpec((tm, d), row), pl.BlockSpec((1, d), fixed),
                   pl.BlockSpec((1, d), fixed), pl.BlockSpec((8, 128), fixed)],
        compiler_params=_cparams("arbitrary"),
    )(z, target, ln_g, ln_b)


def _ln_bwd_call(z, dy, ln_g, bf16_scale, name, exch=()):
    t, d = z.shape
    tm = _tile(t, 512, 8)

    def body(z_ref, dy_ref, g_ref, dz_ref, dzb_ref, dg_ref, db_ref):
        @pl.when(pl.program_id(0) == 0)
        def _():
            dg_ref[...] = jnp.zeros_like(dg_ref)
            db_ref[...] = jnp.zeros_like(db_ref)

        xh, rstd = _ln_stats(z_ref[...])
        dy = dy_ref[...]
        dz = _ln_bwd(dy * g_ref[...], xh, rstd)
        dz_ref[...] = dz
        dzb_ref[...] = (bf16_scale * dz).astype(BF16)
        dg_ref[...] += jnp.sum(dy * xh, axis=0, keepdims=True)
        db_ref[...] += jnp.sum(dy, axis=0, keepdims=True)

    row = lambda i: (i, 0)
    fixed = lambda i: (0, 0)
    return _call(
        body, exch, name=name, grid=(t // tm,),
        out_shape=[jax.ShapeDtypeStruct((t, d), F32), jax.ShapeDtypeStruct((t, d), BF16),
                   jax.ShapeDtypeStruct((1, d), F32), jax.ShapeDtypeStruct((1, d), F32)],
        in_specs=[pl.BlockSpec((tm, d), row), pl.BlockSpec((tm, d), row), pl.BlockSpec((1, d), fixed)],
        out_specs=[pl.BlockSpec((tm, d), row), pl.BlockSpec((tm, d), row), pl.BlockSpec((1, d), fixed),
                   pl.BlockSpec((1, d), fixed)],
        semantics=("arbitrary",),
    )(z, dy, ln_g)


CONV_ROWS = 32
CONV_LANES = 512
SUBLANES = 8


def _fill_shifted(ext, shifted):
    rows = ext.shape[0] - SUBLANES
    for s in range(1, SUBLANES):
        for r in range(0, rows, CONV_ROWS):
            n = min(CONV_ROWS, rows - r)
            shifted[s - 1, r:r + n, :] = ext[r + s:r + s + n, :]


def _window(ext, shifted, lo, n, lanes=slice(None)):
    s = lo % SUBLANES
    return ext[lo:lo + n, lanes] if s == 0 else shifted[s - 1, lo - s:lo - s + n, lanes]


def _mixer_fwd(proj, conv_w, conv_b, cln_g, cln_b, sln_g, sln_b, sg_wm, sg_bb, name, exch=()):
    t = proj.shape[0]
    tm = _tile(t, 256, CHUNK)
    hb = tm // HALO
    nc = tm // CHUNK
    ch = CONV_CH

    def body(av_ref, ag_ref, bu_ref, bv_ref, hv_ref, hg_ref, cw_ref, cb_ref, lg_ref, lb_ref, sg_ref, sb_ref,
             w_ref, bb_ref, y_ref, yt_ref, c_ref, ext, ext_s):
        i = pl.program_id(0)
        halo = hv_ref[...] * _sigmoid(hg_ref[...])
        ext[0:HALO, :] = jnp.where(i > 0, halo, 0.0)
        ext[HALO:HALO + tm, :] = av_ref[...] * _sigmoid(ag_ref[...])
        _fill_shifted(ext, ext_s)
        for r in range(0, tm, CONV_ROWS):
            acc = jnp.zeros((CONV_ROWS, ch), F32) + cb_ref[...]
            for k in range(CONV_TAPS):
                lo = r + k + HALO - (CONV_TAPS - 1)
                acc = acc + cw_ref[k:k + 1, :] * _window(ext, ext_s, lo, CONV_ROWS)
            c_ref[r:r + CONV_ROWS, :] = acc
        a = _ln(c_ref[...], lg_ref[...], lb_ref[...])
        ya = a * _sigmoid(a)
        y_ref[:, 0:ch] = ya.astype(BF16)
        yt_ref[0:ch, :] = ya.T.astype(BF16)
        for h in range(HEADS):
            sl = slice(h * HEAD_DIM, (h + 1) * HEAD_DIM)
            u, _ = _gelu_and_grad(bu_ref[:, sl])
            v, _ = _gelu_and_grad(bv_ref[:, sl])
            vn = _ln(v, sg_ref[h:h + 1, :], sb_ref[h:h + 1, :])
            vn3 = vn.astype(BF16).reshape(nc, CHUNK, HEAD_DIM)
            wb = jnp.broadcast_to(w_ref[h][None], (nc, CHUNK, CHUNK))
            mixed = jnp.einsum("cts,csd->ctd", wb, vn3, preferred_element_type=F32) + bb_ref[h][None]
            yb = u * mixed.reshape(tm, HEAD_DIM)
            y_ref[:, ch + h * HEAD_DIM:ch + (h + 1) * HEAD_DIM] = yb.astype(BF16)
            yt_ref[ch + h * HEAD_DIM:ch + (h + 1) * HEAD_DIM, :] = yb.T.astype(BF16)

    col = lambda cidx: (lambda i: (i, cidx))
    prev = lambda cidx: (lambda i: (jnp.maximum(i * hb - 1, 0), cidx))
    fix2 = lambda i: (0, 0)
    fix3 = lambda i: (0, 0, 0)
    return _call(
        body, exch, name=name, grid=(t // tm,),
        out_shape=[jax.ShapeDtypeStruct((t, 2 * ch), BF16), jax.ShapeDtypeStruct((2 * ch, t), BF16),
                   jax.ShapeDtypeStruct((t, ch), F32)],
        in_specs=[pl.BlockSpec((tm, ch), col(0)), pl.BlockSpec((tm, ch), col(1)), pl.BlockSpec((tm, ch), col(2)),
                  pl.BlockSpec((tm, ch), col(3)), pl.BlockSpec((HALO, ch), prev(0)), pl.BlockSpec((HALO, ch), prev(1)),
                  pl.BlockSpec((CONV_TAPS, ch), fix2), pl.BlockSpec((1, ch), fix2), pl.BlockSpec((1, ch), fix2),
                  pl.BlockSpec((1, ch), fix2), pl.BlockSpec((HEADS, HEAD_DIM), fix2), pl.BlockSpec((HEADS, HEAD_DIM), fix2),
                  pl.BlockSpec((HEADS, CHUNK, CHUNK), fix3), pl.BlockSpec((HEADS, CHUNK, HEAD_DIM), fix3)],
        out_specs=[pl.BlockSpec((tm, 2 * ch), lambda i: (i, 0)), pl.BlockSpec((2 * ch, tm), lambda i: (0, i)),
                   pl.BlockSpec((tm, ch), lambda i: (i, 0))],
        scratch_shapes=[pltpu.VMEM((HALO + tm, ch), F32), pltpu.VMEM((SUBLANES - 1, HALO + tm, ch), F32)],
        semantics=("parallel",),
    )(proj, proj, proj, proj, proj, proj, conv_w, conv_b, cln_g, cln_b, sln_g, sln_b, sg_wm, sg_bb)


def _mixer_bwd(proj, conv_c, dy, conv_w, cln_g, cln_b, sln_g, sln_b, sg_wm, sg_wmt, sg_bb, name, exch=()):
    t = proj.shape[0]
    tm = _tile(t, 256, CHUNK)
    hb = tm // HALO
    nc = tm // CHUNK
    nt = t // tm
    ch = CONV_CH
    last_halo = t // HALO - 1

    def body(av_ref, ag_ref, bu_ref, bv_ref, hv_ref, hg_ref, c_ref, cn_ref, dya_ref, dyan_ref, dyb_ref,
             cw_ref, lg_ref, lb_ref, sg_ref, sb_ref, w_ref, wt_ref, bb_ref,
             dp_ref, dcw_ref, dcb_ref, dlg_ref, dlb_ref, dsg_ref, dsb_ref, dw_ref, dbs_ref,
             ext_h, ext_dc, ext_hs, ext_dcs, acc_cw):
        i = pl.program_id(0)

        @pl.when(i == 0)
        def _():
            acc_cw[...] = jnp.zeros_like(acc_cw)
            for ref in (dcb_ref, dlg_ref, dlb_ref, dsg_ref, dsb_ref, dw_ref, dbs_ref):
                ref[...] = jnp.zeros_like(ref)

        lg = lg_ref[...]
        lb = lb_ref[...]

        def conv_ln_bwd(c, dya):
            xh, rstd = _ln_stats(c)
            a = xh * lg + lb
            da = dya * _silu_grad(a)
            return _ln_bwd(da * lg, xh, rstd), da, xh

        fold = lambda v: jnp.sum(v.reshape(CONV_ROWS // SUBLANES, SUBLANES, ch), axis=0)
        s_lg = s_lb = s_cb = jnp.zeros((SUBLANES, ch), F32)
        for r in range(0, tm, CONV_ROWS):
            dc, da, xh = conv_ln_bwd(c_ref[r:r + CONV_ROWS, :], dya_ref[r:r + CONV_ROWS, :])
            ext_dc[r:r + CONV_ROWS, :] = dc
            s_lg, s_lb, s_cb = s_lg + fold(da * xh), s_lb + fold(da), s_cb + fold(dc)
        dlg_ref[...] += jnp.sum(s_lg, axis=0, keepdims=True)
        dlb_ref[...] += jnp.sum(s_lb, axis=0, keepdims=True)
        dcb_ref[...] += jnp.sum(s_cb, axis=0, keepdims=True)
        dcn, _, _ = conv_ln_bwd(cn_ref[...], dyan_ref[...])
        ext_dc[tm:tm + HALO, :] = jnp.where(i < nt - 1, dcn, 0.0)
        halo = hv_ref[...] * _sigmoid(hg_ref[...])
        ext_h[0:HALO, :] = jnp.where(i > 0, halo, 0.0)
        ext_h[HALO:HALO + tm, :] = av_ref[...] * _sigmoid(ag_ref[...])
        _fill_shifted(ext_h, ext_hs)
        _fill_shifted(ext_dc, ext_dcs)
        for r, c0 in [(r, c0) for r in range(0, tm, CONV_ROWS) for c0 in range(0, ch, CONV_LANES)]:
            rows, lanes = slice(r, r + CONV_ROWS), slice(c0, c0 + CONV_LANES)
            dcr = ext_dc[rows, lanes]
            acc = jnp.zeros((CONV_ROWS, CONV_LANES), F32)
            for k in range(CONV_TAPS):
                lo = r + k + HALO - (CONV_TAPS - 1)
                prod = dcr * _window(ext_h, ext_hs, lo, CONV_ROWS, lanes)
                acc_cw[k, :, lanes] += jnp.sum(prod.reshape(CONV_ROWS // SUBLANES, SUBLANES, CONV_LANES), axis=0)
                hi = r + (CONV_TAPS - 1) - k
                acc = acc + cw_ref[k:k + 1, lanes] * _window(ext_dc, ext_dcs, hi, CONV_ROWS, lanes)
            sg_r = _sigmoid(ag_ref[rows, lanes])
            av_r = av_ref[rows, lanes]
            dp_ref[rows, lanes] = (acc * sg_r).astype(BF16)
            dp_ref[rows, slice(ch + c0, ch + c0 + CONV_LANES)] = (acc * av_r * sg_r * (1.0 - sg_r)).astype(BF16)

        @pl.when(i == nt - 1)
        def _():
            dcw_ref[...] = jnp.sum(acc_cw[...], axis=1)

        tril = (lax.broadcasted_iota(jnp.int32, (CHUNK, CHUNK), 0)
                >= lax.broadcasted_iota(jnp.int32, (CHUNK, CHUNK), 1)).astype(F32)
        for h in range(HEADS):
            sl = slice(h * HEAD_DIM, (h + 1) * HEAD_DIM)
            u, du_dx = _gelu_and_grad(bu_ref[:, sl])
            v, dv_dx = _gelu_and_grad(bv_ref[:, sl])
            xhv, rstdv = _ln_stats(v)
            gh = sg_ref[h:h + 1, :]
            vn3 = (xhv * gh + sb_ref[h:h + 1, :]).astype(BF16).reshape(nc, CHUNK, HEAD_DIM)
            wb = jnp.broadcast_to(w_ref[h][None], (nc, CHUNK, CHUNK))
            mixed = jnp.einsum("cts,csd->ctd", wb, vn3, preferred_element_type=F32) + bb_ref[h][None]
            dyb = dyb_ref[:, sl]
            d_u = dyb * mixed.reshape(tm, HEAD_DIM)
            dm = dyb * u
            dm3 = dm.reshape(nc, CHUNK, HEAD_DIM)
            dbs_ref[h:h + 1, :] += jnp.sum(jnp.sum(dm3, axis=0).T, axis=0, keepdims=True)
            dm3b = dm3.astype(BF16)
            dw_h = jnp.sum(jnp.einsum("ctd,csd->cts", dm3b, vn3, preferred_element_type=F32), axis=0)
            dw_ref[h] += dw_h * tril
            wtb = jnp.broadcast_to(wt_ref[h][None], (nc, CHUNK, CHUNK))
            d_vn = jnp.einsum("cst,ctd->csd", wtb, dm3b, preferred_element_type=F32).reshape(tm, HEAD_DIM)
            dsg_ref[h:h + 1, :] += jnp.sum(d_vn * xhv, axis=0, keepdims=True)
            dsb_ref[h:h + 1, :] += jnp.sum(d_vn, axis=0, keepdims=True)
            dv = _ln_bwd(d_vn * gh, xhv, rstdv)
            dp_ref[:, 2 * ch + h * HEAD_DIM:2 * ch + (h + 1) * HEAD_DIM] = (d_u * du_dx).astype(BF16)
            dp_ref[:, 3 * ch + h * HEAD_DIM:3 * ch + (h + 1) * HEAD_DIM] = (dv * dv_dx).astype(BF16)

    col = lambda cidx: (lambda i: (i, cidx))
    prev = lambda cidx: (lambda i: (jnp.maximum(i * hb - 1, 0), cidx))
    nxt = lambda i: (jnp.minimum((i + 1) * hb, last_halo), 0)
    fix2 = lambda i: (0, 0)
    fix3 = lambda i: (0, 0, 0)
    out_shape = [jax.ShapeDtypeStruct((t, 4 * ch), BF16), jax.ShapeDtypeStruct((CONV_TAPS, ch), F32),
                 jax.ShapeDtypeStruct((1, ch), F32), jax.ShapeDtypeStruct((1, ch), F32), jax.ShapeDtypeStruct((1, ch), F32),
                 jax.ShapeDtypeStruct((HEADS, HEAD_DIM), F32), jax.ShapeDtypeStruct((HEADS, HEAD_DIM), F32),
                 jax.ShapeDtypeStruct((HEADS, CHUNK, CHUNK), F32), jax.ShapeDtypeStruct((HEADS, CHUNK), F32)]
    out_specs = [pl.BlockSpec((tm, 4 * ch), lambda i: (i, 0)), pl.BlockSpec((CONV_TAPS, ch), fix2),
                 pl.BlockSpec((1, ch), fix2), pl.BlockSpec((1, ch), fix2), pl.BlockSpec((1, ch), fix2),
                 pl.BlockSpec((HEADS, HEAD_DIM), fix2), pl.BlockSpec((HEADS, HEAD_DIM), fix2),
                 pl.BlockSpec((HEADS, CHUNK, CHUNK), fix3), pl.BlockSpec((HEADS, CHUNK), fix2)]
    in_specs = [pl.BlockSpec((tm, ch), col(0)), pl.BlockSpec((tm, ch), col(1)), pl.BlockSpec((tm, ch), col(2)),
                pl.BlockSpec((tm, ch), col(3)), pl.BlockSpec((HALO, ch), prev(0)), pl.BlockSpec((HALO, ch), prev(1)),
                pl.BlockSpec((tm, ch), col(0)), pl.BlockSpec((HALO, ch), nxt),
                pl.BlockSpec((tm, ch), col(0)), pl.BlockSpec((HALO, ch), nxt), pl.BlockSpec((tm, ch), col(1)),
                pl.BlockSpec((CONV_TAPS, ch), fix2), pl.BlockSpec((1, ch), fix2), pl.BlockSpec((1, ch), fix2),
                pl.BlockSpec((HEADS, HEAD_DIM), fix2), pl.BlockSpec((HEADS, HEAD_DIM), fix2),
                pl.BlockSpec((HEADS, CHUNK, CHUNK), fix3), pl.BlockSpec((HEADS, CHUNK, CHUNK), fix3),
                pl.BlockSpec((HEADS, CHUNK, HEAD_DIM), fix3)]
    return _call(
        body, exch, name=name, grid=(nt,), out_shape=out_shape, in_specs=in_specs, out_specs=out_specs,
        scratch_shapes=[pltpu.VMEM((HALO + tm, ch), F32), pltpu.VMEM((tm + HALO, ch), F32),
                        pltpu.VMEM((SUBLANES - 1, HALO + tm, ch), F32), pltpu.VMEM((SUBLANES - 1, tm + HALO, ch), F32),
                        pltpu.VMEM((CONV_TAPS, 8, ch), F32)],
        semantics=("arbitrary",),
    )(proj, proj, proj, proj, proj, proj, conv_c, conv_c, dy, dy, dy,
      conv_w, cln_g, cln_b, sln_g, sln_b, sg_wm, sg_wmt, sg_bb)


def _pair_sum(parts, from_sibling, core_chip, name):
    _, r, cc = parts.shape
    tr = _tile(r, max(16, (1 << 20) // (2 * cc)), 16)

    def body(cc_ref, p_ref, s_ref, o_ref, own_ref):
        q = (p_ref[...].astype(F32) + s_ref[...].astype(F32)).astype(BF16)
        o_ref[...] = q

        @pl.when(pl.program_id(1) == cc_ref[1])
        def _():
            own_ref[...] = q[0]

    grid_spec = pltpu.PrefetchScalarGridSpec(
        num_scalar_prefetch=1, grid=(r // tr, 4),
        in_specs=[pl.BlockSpec((1, tr, cc), lambda i, j, cc_ref: (2 * j + cc_ref[0], i, 0)),
                  pl.BlockSpec((1, tr, cc), lambda i, j, cc_ref: (j, i, 0))],
        out_specs=[pl.BlockSpec((1, tr, cc), lambda i, j, cc_ref: (j, i, 0)),
                   pl.BlockSpec((tr, cc), lambda i, j, cc_ref: (i, 0))])
    return pl.pallas_call(
        body, name=name, grid_spec=grid_spec,
        out_shape=[jax.ShapeDtypeStruct((4, r, cc), BF16), jax.ShapeDtypeStruct((r, cc), BF16)],
        compiler_params=_cparams("parallel", "arbitrary"),
    )(core_chip, parts, from_sibling)


def _adamw_math(w, g, m, v):
    m = ADAM_B1 * m + (1.0 - ADAM_B1) * g
    v = ADAM_B2 * v + (1.0 - ADAM_B2) * (g * g)
    m_hat = m / (1.0 - ADAM_B1 ** ADAM_STEP)
    v_hat = v / (1.0 - ADAM_B2 ** ADAM_STEP)
    delta = -ADAM_LR * (m_hat / (jnp.sqrt(v_hat) + ADAM_EPS) + ADAM_WD * w)
    return delta, m, v


def _adamw_tile(in_refs, out_refs):
    w_ref, m_ref, v_ref, q_ref, o_ref = in_refs
    g = q_ref[...].astype(F32)
    for k in range(3):
        g = g + o_ref[k].astype(F32)
    d, mm, vv = _adamw_math(w_ref[...], g, m_ref[...], v_ref[...])
    for ref, val in zip(out_refs, (g, d, mm, vv)):
        ref[...] = val


def _adamw_side(w, m, v, chip_part, from_chips, max_tiles):
    r, cc = w.shape
    n = max(k for k in range(1, max_tiles + 1) if r % k == 0 and (r // k) % 16 == 0)
    tr = r // n
    row = ((tr, cc), lambda s: (s, 0))
    return _Side([w, m, v, chip_part, from_chips], [row, row, row, row, ((3, tr, cc), lambda s: (0, s, 0))],
                 [jax.ShapeDtypeStruct((r, cc), F32)] * 4, [row] * 4, n, _adamw_tile)


def _adamw_sharded(w, m, v, chip_part, from_chips, name):
    r, cc = w.shape
    tr = _tile(r, max(16, (1 << 19) // (4 * cc) * 2), 16)

    def body(*refs):
        _adamw_tile(refs[:5], refs[5:])

    row = pl.BlockSpec((tr, cc), lambda i: (i, 0))
    return pl.pallas_call(
        body, name=name, grid=(r // tr,), out_shape=[jax.ShapeDtypeStruct((r, cc), F32)] * 4,
        in_specs=[row, row, row, row, pl.BlockSpec((3, tr, cc), lambda i: (0, i, 0))], out_specs=[row] * 4,
        compiler_params=_cparams("parallel"),
    )(w, m, v, chip_part, from_chips)


def _adamw_small(w, g, m, v, name):
    r, cc = w.shape

    def body(w_ref, g_ref, m_ref, v_ref, d_out, m_out, v_out):
        d, mm, vv = _adamw_math(w_ref[...], g_ref[...], m_ref[...], v_ref[...])
        d_out[...] = d
        m_out[...] = mm
        v_out[...] = vv

    full = pl.BlockSpec((r, cc), lambda i: (0, 0))
    return pl.pallas_call(
        body, name=name, grid=(1,), out_shape=[jax.ShapeDtypeStruct((r, cc), F32)] * 3,
        in_specs=[full] * 4, out_specs=[full] * 3, compiler_params=_cparams("arbitrary"),
    )(w, g, m, v)


SMALL = ("ln1_g", "ln1_b", "conv_b", "conv_ln_g", "conv_ln_b", "sg_ln_g", "sg_ln_b", "sg_w", "sg_b",
         "ln2_g", "ln2_b", "ln3_g", "ln3_b")
ORDER = ("ffn1_w_gate_up", "ffn1_w_down", "ln1_g", "ln1_b", "mix_w_in", "conv_w", "conv_b", "conv_ln_g", "conv_ln_b",
         "sg_ln_g", "sg_ln_b", "sg_w", "sg_b", "mix_w_out", "ln2_g", "ln2_b", "ffn2_w_gate_up", "ffn2_w_down",
         "ln3_g", "ln3_b")


def _rows128(a):
    return a.reshape(-1, 128)


def kernel(x, ffn1_w_gate_up, ffn1_w_down, ln1_g, ln1_b, mix_w_in, conv_w, conv_b, conv_ln_g, conv_ln_b, sg_ln_g, sg_ln_b, sg_w, sg_b, mix_w_out, ln2_g, ln2_b, ffn2_w_gate_up, ffn2_w_down, ln3_g, ln3_b, loss_target, m_ffn1_w_gate_up, m_ffn1_w_down, m_ln1_g, m_ln1_b, m_mix_w_in, m_conv_w, m_conv_b, m_conv_ln_g, m_conv_ln_b, m_sg_ln_g, m_sg_ln_b, m_sg_w, m_sg_b, m_mix_w_out, m_ln2_g, m_ln2_b, m_ffn2_w_gate_up, m_ffn2_w_down, m_ln3_g, m_ln3_b, v_ffn1_w_gate_up, v_ffn1_w_down, v_ln1_g, v_ln1_b, v_mix_w_in, v_conv_w, v_conv_b, v_conv_ln_g, v_conv_ln_b, v_sg_ln_g, v_sg_ln_b, v_sg_w, v_sg_b, v_mix_w_out, v_ln2_g, v_ln2_b, v_ffn2_w_gate_up, v_ffn2_w_down, v_ln3_g, v_ln3_b):
    args = dict(locals())
    w = {n: args[n][0] for n in ORDER}
    mom = {n: args["m_" + n][0] for n in ORDER}
    var = {n: args["v_" + n][0] for n in ORDER}
    x0 = x[0]
    target = loss_target[0]
    t, d = x0.shape
    my_x, my_y, my_c = lax.axis_index("x"), lax.axis_index("y"), lax.axis_index("c")
    my_chip = (2 * my_x + my_y).astype(jnp.int32).reshape(1)
    my_core = my_c.astype(jnp.int32).reshape(1)
    me = 4 * my_x + 2 * my_y + my_c

    big = ("ffn1_w_gate_up", "ffn1_w_down", "mix_w_in", "mix_w_out", "ffn2_w_gate_up", "ffn2_w_down")
    sh = {n: w[n].astype(BF16) for n in big}
    f2s = sh["ffn2_w_gate_up"].shape[1]
    order = jnp.stack([4 * p[0] + 2 * p[1] + p[2] for p in _visit_order(my_x, my_y, my_c)]).astype(jnp.int32)
    gu1, x0t, (wgu1, wd1, conv_w_all) = _gather_and_gate_up(
        x0, [sh["ffn1_w_gate_up"], sh["ffn1_w_down"], w["conv_w"]], [True, True, False], order, "ffn1_gate_up_fwd")
    wd1 = wd1.reshape(-1, d)
    conv_w_full = jnp.transpose(conv_w_all, (1, 0, 2)).reshape(CONV_TAPS, CONV_CH)
    tril = jnp.tril(jnp.ones((CHUNK, CHUNK), F32))
    sg_wm = w["sg_w"] * tril
    sg_wm_b = sg_wm.astype(BF16)
    sg_wmt_b = jnp.swapaxes(sg_wm, 1, 2).astype(BF16)
    sg_bb = jnp.broadcast_to(w["sg_b"][:, :, None], (HEADS, CHUNK, HEAD_DIM))
    row = lambda a: a.reshape(1, -1)

    d2 = [sh["ffn2_w_down"]]
    d2_first = d2[0].shape[0] * 3 // 5 // 16 * 16
    d2_top, d2_bottom = (0, d2_first), (d2_first, d2[0].shape[0] - d2_first)
    (h1t, z1, x1, x1t), ((w_in, w_out), (g_d2,)) = _ffn_down_fwd(
        gu1, x0, wd1, row(w["ln1_g"]), row(w["ln1_b"]), "ffn1_down_fwd",
        exch=[_gather_whole([sh["mix_w_in"], sh["mix_w_out"]], [True, False]),
              _gather_whole(d2, [False], rows=d2_top)])
    in_cols = sh["mix_w_in"].shape[1]
    w_out = w_out.reshape(-1, d)
    top, bottom = (0, d // 2), (d // 2, d // 2)
    gu2 = [sh["ffn2_w_gate_up"]]
    proj, ((g_gu2,),) = _mix_in_proj(x1, w_in, "mix_in_fwd", exch=[_gather_first(gu2, [True], rows=top)])
    (y, yt, conv_c), ((g_gu2,),) = _mixer_fwd(
        proj, conv_w_full, row(w["conv_b"]), row(w["conv_ln_g"]), row(w["conv_ln_b"]),
        w["sg_ln_g"], w["sg_ln_b"], sg_wm_b, sg_bb, "mixer_fwd",
        exch=[_both(_gather_first(gu2, [True], rows=bottom, into=[g_gu2]),
                    _gather_forward([g_gu2], [True], [f2s], rows=top))])
    (z2, x2, x2t), ((wgu2,), (wd2,)) = _mix_out_fwd(
        y, w_out, x1, row(w["ln2_g"]), row(w["ln2_b"]), "mix_out_fwd",
        exch=[_gather_forward([g_gu2], [True], [f2s], rows=bottom),
              _gather_whole(d2, [False], rows=d2_bottom, into=[g_d2])])
    wd2 = wd2.reshape(-1, d)
    grads = {}
    (g2, u2, h2t, dz3, do2, grads["ln3_g"], grads["ln3_b"], loss_tile), _ = _ffn_fwd_loss(
        x2, wgu2, wd2, row(w["ln3_g"]), row(w["ln3_b"]), target, "ffn2_fwd_loss")

    f = wd1.shape[0]
    dn = _tile(d, 1024, 128)
    core_chip = jnp.concatenate([my_core, my_chip])
    pair = lambda p, s, label: _pair_sum(p, s, core_chip, "pair_sum_" + label)
    adamw = lambda n, own, got, steps: _adamw_side(w[n], mom[n], var[n], own, got, steps)
    m_tiles = d // _tile(d, 512, 16)
    gu_first = d * 2 // 3 // 16 * 16
    out = {}
    p_d2, _ = _weight_grad(h2t, do2, dn, 512, "ffn2_dw_down")
    p_d2 = p_d2.reshape(N_DEV, f // N_DEV, d)
    (dg2, du2, dx2), ((s_d2,),) = _ffn_bwd(dz3, do2, g2, u2, wgu2, wd2, "ffn2_bwd", exch=[_rs_sibling([p_d2])])
    q_d2, own_d2 = pair(p_d2, s_d2, "ffn2_down")
    d_rows = q_d2.shape[1]
    d_half = d_rows // 2 // 16 * 16
    p_gu2, ((r_d2,),) = _weight_grad(x2t, dg2, f2s, 512, "ffn2_dw_gate", blocks=N_DEV,
                                     exch=[_rs_chips([q_d2], rows=(0, d_half))])
    p_gu2, ((r_d2,),) = _weight_grad(x2t, du2, f2s, 512, "ffn2_dw_up", blocks=N_DEV, block_offset=4, into=p_gu2,
                                     exch=[_rs_chips([q_d2], rows=(d_half, d_rows - d_half), into=[r_d2])])
    (dz2, dz2b, grads["ln2_g"], grads["ln2_b"]), ((s_gu2,),) = _ln_bwd_call(
        z2, dx2, row(w["ln2_g"]), 1.0, "ln2_bwd", exch=[_rs_sibling([p_gu2])])
    q_gu2, own_gu2 = pair(p_gu2, s_gu2, "ffn2_gate_up")
    dy, _ = _mix_out_bwd(dz2b, w_out, "mix_out_bwd")
    p_out, _ = _weight_grad(yt, dz2b, dn, 512, "mix_out_dw")
    p_out = p_out.reshape(N_DEV, -1, d)
    (dproj, grads["conv_w"], grads["conv_b"], grads["conv_ln_g"], grads["conv_ln_b"], grads["sg_ln_g"],
     grads["sg_ln_b"], grads["sg_w"], grads["sg_b"]), ((r_gu2,),) = _mixer_bwd(
        proj, conv_c, dy, conv_w_full, row(w["conv_ln_g"]), row(w["conv_ln_b"]), w["sg_ln_g"], w["sg_ln_b"],
        sg_wm_b, sg_wmt_b, sg_bb, "mixer_bwd", exch=[_rs_chips([q_gu2], rows=(0, gu_first))])
    dx1, ((s_out,), (r_gu2,)) = _mix_in_bwd(
        dproj, w_in, dz2, "mix_in_bwd",
        exch=[_rs_sibling([p_out]), _rs_chips([q_gu2], rows=(gu_first, d - gu_first), into=[r_gu2])])
    p_in, (out["ffn2_w_gate_up"], out["ffn2_w_down"]) = _weight_grad(
        x1t, dproj, in_cols, 512, "mix_in_dw", blocks=N_DEV,
        exch=[adamw("ffn2_w_gate_up", own_gu2, r_gu2, N_DEV * m_tiles), adamw("ffn2_w_down", own_d2, r_d2, N_DEV * m_tiles)])
    (dz1, do1, grads["ln1_g"], grads["ln1_b"]), ((s_in,),) = _ln_bwd_call(
        z1, dx1, row(w["ln1_g"]), 0.5, "ln1_bwd", exch=[_rs_sibling([p_in])])
    q_out, own_out = pair(p_out, s_out, "mix_out")
    q_in, own_in = pair(p_in, s_in, "mix_in")
    small_parts = [_rows128(grads[n]) for n in SMALL]
    packed = jnp.concatenate(small_parts + [_rows128(grads["conv_w"]), loss_tile], axis=0)
    p_d1, ((r_in,),) = _weight_grad(h1t, do1, dn, 512, "ffn1_dw_down", exch=[_rs_chips([q_in])])
    p_d1 = p_d1.reshape(N_DEV, f // N_DEV, d)
    (dg1, du1), ((s_d1,), (r_out,), (small_all,)) = _ffn_bwd_act(
        do1, gu1, wd1, "ffn1_bwd_act",
        exch=[_rs_sibling([p_d1]), _rs_chips([q_out]), _small_gather(packed)])
    q_d1, own_d1 = pair(p_d1, s_d1, "ffn1_down")
    p_gu1, ((r_d1,),) = _weight_grad(x0t, dg1, f2s, 512, "ffn1_dw_gate", blocks=N_DEV, exch=[_rs_chips([q_d1])])
    p_gu1, (out["mix_w_in"], out["mix_w_out"]) = _weight_grad(
        x0t, du1, f2s, 512, "ffn1_dw_up", blocks=N_DEV, block_offset=4, into=p_gu1,
        exch=[adamw("mix_w_in", own_in, r_in, 4 * m_tiles), adamw("mix_w_out", own_out, r_out, 4 * m_tiles)])
    (s_gu1,) = _exchange_alone(_rs_sibling([p_gu1]), "ffn1_gate_up_sibling_exchange")
    q_gu1, own_gu1 = pair(p_gu1, s_gu1, "ffn1_gate_up")
    (grad_x,), ((r_gu1,),) = _ffn_bwd_dx(dz1, dg1, du1, wgu1, "ffn1_bwd_dx", exch=[_rs_chips([q_gu1])])
    for n, own, got in (("ffn1_w_down", own_d1, r_d1), ("ffn1_w_gate_up", own_gu1, r_gu1)):
        out[n] = _adamw_sharded(w[n], mom[n], var[n], own, got, "adamw_" + n)

    cw_rows = CONV_TAPS * CONV_CH // 128
    total = _sum_over_devices(small_all)
    offs = [0]
    for p in small_parts:
        offs.append(offs[-1] + p.shape[0])
    n_small = offs[-1]
    loss = total[n_small + cw_rows, 0]
    g_conv_w = lax.dynamic_slice_in_dim(total[n_small:n_small + cw_rows].reshape(CONV_TAPS, CONV_CH),
                                        me * (CONV_CH // N_DEV), CONV_CH // N_DEV, axis=1)
    pad8 = lambda a: jnp.pad(a, ((0, -a.shape[0] % 8), (0, 0)))
    pack = lambda tree, cw: jnp.concatenate([_rows128(tree[n]) for n in SMALL] + [pad8(cw)], axis=0)
    g_pack = jnp.concatenate([total[:n_small], pad8(g_conv_w)], axis=0)
    d_pack, m_pack, v_pack = _adamw_small(pack(w, w["conv_w"]), g_pack, pack(mom, mom["conv_w"]),
                                          pack(var, var["conv_w"]), "adamw_small")
    for k, n in enumerate(SMALL):
        sl = slice(offs[k], offs[k + 1])
        shp = w[n].shape
        out[n] = (total[sl].reshape(shp), d_pack[sl].reshape(shp), m_pack[sl].reshape(shp), v_pack[sl].reshape(shp))
    sl = slice(n_small, n_small + CONV_TAPS)
    out["conv_w"] = (g_conv_w, d_pack[sl], m_pack[sl], v_pack[sl])

    lead = lambda a: a[None]
    res = [loss, grad_x[None]]
    for kind in range(4):
        res += [lead(out[n][kind]) for n in ORDER]
    return tuple(res)
```

```python
import functools
import math

import jax
import jax.numpy as jnp
from jax import lax
from jax.experimental import pallas as pl
from jax.experimental.pallas import tpu as pltpu

F32, BF16 = jnp.float32, jnp.bfloat16
MESH = pl.DeviceIdType.MESH
ANY = pl.BlockSpec(memory_space=pl.ANY)

N_DEV = 8
LN_EPS = 1e-5
ALPHA = 2.0 ** 0.25
CONV_CH = 1024
CONV_TAPS = 31
HALO = 32
HEADS = 8
HEAD_DIM = 128
CHUNK = 128
ADAM_LR, ADAM_B1, ADAM_B2, ADAM_EPS, ADAM_WD, ADAM_STEP = 0.001, 0.9, 0.999, 1e-08, 0.01, 10
V7X_VMEM_LIMIT = 62 * 2 ** 20
EPILOGUE_ROWS = 128

def _cparams(*sem):
    return pltpu.CompilerParams(dimension_semantics=sem, vmem_limit_bytes=V7X_VMEM_LIMIT)


def _tile(n, pref, mult):
    best = None
    for t in range(mult, min(n, pref) + 1, mult):
        if n % t == 0:
            best = t
    return best if best is not None else n


def _dot(a, b):
    return jnp.dot(a, b, preferred_element_type=F32)


def _dot_nt(a, b):
    return lax.dot_general(a, b, (((1,), (1,)), ((), ())), preferred_element_type=F32)


def _sigmoid(x):
    return 1.0 / (1.0 + jnp.exp(-x))


def _ln_stats(z):
    mu = jnp.mean(z, axis=-1, keepdims=True)
    zc = z - mu
    var = jnp.mean(zc * zc, axis=-1, keepdims=True)
    rstd = lax.rsqrt(var + LN_EPS)
    return zc * rstd, rstd


def _ln(z, g, b):
    xh, _ = _ln_stats(z)
    return xh * g + b


def _ln_bwd(dxh, xh, rstd):
    m1 = jnp.mean(dxh, axis=-1, keepdims=True)
    m2 = jnp.mean(dxh * xh, axis=-1, keepdims=True)
    return rstd * (dxh - m1 - xh * m2)


_GK = math.sqrt(2.0 / math.pi)
_GA = 0.044715


def _gelu_and_grad(x):
    x2 = x * x
    t = jnp.tanh(_GK * (x + _GA * x * x2))
    y = 0.5 * x * (1.0 + t)
    dy = 0.5 * (1.0 + t) + 0.5 * x * (1.0 - t * t) * (_GK * (1.0 + 3.0 * _GA * x2))
    return y, dy


def _silu_grad(a):
    s = _sigmoid(a)
    return s * (1.0 + a * (1.0 - s))


def _place():
    return lax.axis_index("x"), lax.axis_index("y"), lax.axis_index("c")


def _other_chips(x, y):
    return [(1 - x, y), (x, 1 - y), (1 - x, 1 - y)]


def _visit_order(x, y, c):
    chips = _other_chips(x, y)
    return [(x, y, c), (x, y, 1 - c), (*chips[0], c), (*chips[1], c), (*chips[0], 1 - c), (*chips[1], 1 - c),
            (*chips[2], c), (*chips[2], 1 - c)]


def _gather_and_gate_up(xb, shards, relayed, order, name):
    n = len(shards)
    N_COPIES = 10
    t, d = xb.shape
    cols = shards[0].shape[1]
    tm = _tile(t, 1024, 128)
    ni = t // tm
    col_major = [True] + [False] * (n - 1)

    def body(order_ref, x_ref, *refs):
        srcs, gu_ref, xt_ref, dsts = refs[:n], refs[n], refs[n + 1], refs[n + 2:2 * n + 2]
        wbuf, send_sems, recv_sems, local_sems, load_sem = refs[2 * n + 2:]
        b, i = pl.program_id(0), pl.program_id(1)
        x, y, c = _place()
        me, sib = (x, y, c), (x, y, 1 - c)
        chips = _other_chips(x, y)

        near_x, near_y, far = chips

        def slot(w, p, band=None):
            half = shards[w].shape[0] // 2
            rows = None if band is None else (band * half, half)
            return _block_slot(dsts[w], col_major[w], shards[w].shape[1], p, rows)

        def copy(w, s, block, to, band=None, from_src=False):
            return pltpu.make_async_remote_copy(
                src_ref=srcs[w] if from_src else slot(w, block, band), dst_ref=slot(w, block, band),
                send_sem=send_sems.at[N_COPIES * w + s], recv_sem=recv_sems.at[N_COPIES * w + s],
                device_id=to, device_id_type=MESH)

        def own(w):
            return pltpu.make_async_copy(srcs[w], slot(w, me), local_sems.at[w])

        def sends(w):
            out = [copy(w, 0, me, sib, from_src=True), copy(w, 1, me, (*near_x, c), from_src=True),
                   copy(w, 2, me, (*near_y, c), from_src=True)]
            if not relayed[w]:
                out.append(copy(w, 3, me, (*far, c), from_src=True))
            return out

        def passed_on(w):
            out = [copy(w, 4, (*near_x, c), sib), copy(w, 5, (*near_y, c), sib)]
            if relayed[w]:
                out += [copy(w, 6, (*far, c), sib, band=0), copy(w, 9, (*far, c), sib, band=1),
                        copy(w, 7, (*near_x, c), (*near_y, c), band=0), copy(w, 8, (*near_y, c), (*near_x, c), band=1)]
            else:
                out.append(copy(w, 6, (*far, c), sib))
            return out

        def start_sends(w):
            own(w).start()
            for cp in sends(w):
                cp.start()

        def got_near_x(w):
            copy(w, 1, (*near_x, c), me).wait_recv()
            copy(w, 4, (*near_x, c), sib).start()
            if relayed[w]:
                copy(w, 7, (*near_x, c), (*near_y, c), band=0).start()

        def got_near_y(w):
            copy(w, 2, (*near_y, c), me).wait_recv()
            copy(w, 5, (*near_y, c), sib).start()
            if relayed[w]:
                copy(w, 8, (*near_y, c), (*near_x, c), band=1).start()

        def got_far(w):
            if relayed[w]:
                copy(w, 7, (*far, c), me, band=0).wait_recv()
                copy(w, 6, (*far, c), sib, band=0).start()
                copy(w, 8, (*far, c), me, band=1).wait_recv()
                copy(w, 9, (*far, c), sib, band=1).start()
            else:
                copy(w, 3, (*far, c), me).wait_recv()
                copy(w, 6, (*far, c), sib).start()

        def got_from_sibling(w, which):
            if which == 0:
                copy(w, 0, sib, me).wait_recv()
            elif which == 3 and relayed[w]:
                copy(w, 6, (*far, 1 - c), me, band=0).wait_recv()
                copy(w, 9, (*far, 1 - c), me, band=1).wait_recv()
            else:
                copy(w, 3 + which, (*chips[which - 1], 1 - c), me).wait_recv()

        others = range(1, n)

        def arrive(k):
            if k == 0:
                own(0).wait()
            elif k == 1:
                got_from_sibling(0, 0)
            elif k == 2:
                got_near_x(0)
                for w in others:
                    start_sends(w)
            elif k == 3:
                got_near_y(0)
            elif k in (4, 5):
                got_from_sibling(0, k - 3)
            elif k == 6:
                got_far(0)
                for w in others:
                    got_near_x(w)
                    got_near_y(w)
            else:
                got_from_sibling(0, 3)
                for w in others:
                    got_far(w)

        def load(k):
            at = pl.multiple_of(order_ref[k] * cols, 128)
            return pltpu.make_async_copy(dsts[0].at[:, pl.ds(at, cols)], wbuf.at[k % 2], load_sem.at[k % 2])

        @pl.when((b == 0) & (i == 0))
        def _():
            start_sends(0)
            arrive(0)
            load(0).start()
            load(0).wait()

        early = ni - 1
        for k in range(1, N_DEV):
            @pl.when((b == k - 1) & (i == early))
            def _(k=k):
                arrive(k)
                load(k).start()

            @pl.when((b == k) & (i == 0))
            def _(k=k):
                load(k).wait()

        gu_ref[...] = _dot(x_ref[...].astype(BF16), wbuf[b % 2]).astype(BF16)

        @pl.when(b == 0)
        def _():
            xt_ref[...] = x_ref[...].T.astype(BF16)

        @pl.when((b == N_DEV - 1) & (i == ni - 1))
        def _():
            for w in others:
                for which in range(4):
                    got_from_sibling(w, which)
                own(w).wait()
            for w in range(n):
                for cp in sends(w) + passed_on(w):
                    cp.wait_send()

    grid_spec = pltpu.PrefetchScalarGridSpec(
        num_scalar_prefetch=1, grid=(N_DEV, ni),
        in_specs=[pl.BlockSpec((tm, d), lambda b, i, o: (i, 0))] + [ANY] * n,
        out_specs=[pl.BlockSpec((tm, cols), lambda b, i, o: (i, o[b])),
                   pl.BlockSpec((d, tm), lambda b, i, o: (0, jnp.where(b == 0, i, ni - 1)))] + [ANY] * n,
        scratch_shapes=[pltpu.VMEM((2, d, cols), BF16), pltpu.SemaphoreType.DMA((N_COPIES * n,)),
                        pltpu.SemaphoreType.DMA((N_COPIES * n,)), pltpu.SemaphoreType.DMA((n,)),
                        pltpu.SemaphoreType.DMA((2,))])
    res = pl.pallas_call(
        body, name=name, grid_spec=grid_spec,
        out_shape=[jax.ShapeDtypeStruct((t, N_DEV * cols), BF16), jax.ShapeDtypeStruct((d, t), BF16)]
        + [_gathered_shape(s, cm) for s, cm in zip(shards, col_major)],
        compiler_params=_cparams("arbitrary", "arbitrary"),
    )(order, xb, *shards)
    return res[0], res[1], res[2:]


class _Exchange:
    def __init__(self, ins, io, new, n_sems, n_local, make):
        self.ins, self.io, self.new = list(ins), list(io), list(new)
        self.n_sems, self.n_local, self.make = n_sems, n_local, make


def _block_slot(ref, col_major, cols, place, rows=None):
    k = 4 * place[0] + 2 * place[1] + place[2]
    band = slice(None) if rows is None else pl.ds(rows[0], rows[1])
    if col_major:
        return ref.at[band, pl.ds(pl.multiple_of(k * cols, 128), cols)]
    return ref.at[k] if rows is None else ref.at[k, band]


def _gathered_shape(s, col_major):
    return jax.ShapeDtypeStruct((s.shape[0], N_DEV * s.shape[1]) if col_major else (N_DEV,) + s.shape, s.dtype)


def _gather_first(shards, col_major, rows=None, into=None):
    n = len(shards)
    new = [] if into is not None else [_gathered_shape(s, cm) for s, cm in zip(shards, col_major)]

    def make(in_refs, io_refs, new_refs, send_sems, recv_sems, local_sems, base=0, local_base=0):
        x, y, c = _place()
        targets = [(x, y, 1 - c)] + [(*chip, c) for chip in _other_chips(x, y)]
        gathered = io_refs if into is not None else new_refs
        copies = []
        for w in range(n):
            src = in_refs[w] if rows is None else in_refs[w].at[pl.ds(rows[0], rows[1])]
            slot = _block_slot(gathered[w], col_major[w], shards[w].shape[1], (x, y, c), rows)
            copies.append(pltpu.make_async_copy(src, slot, local_sems.at[local_base + w]))
            for s, to in enumerate(targets):
                copies.append(pltpu.make_async_remote_copy(
                    src_ref=src, dst_ref=slot, send_sem=send_sems.at[base + 4 * w + s],
                    recv_sem=recv_sems.at[base + 4 * w + s], device_id=to, device_id_type=MESH))
        return copies

    return _Exchange(shards, into or [], new, 4 * n, n, make)


def _gather_forward(gathered, col_major, cols, rows=None):
    n = len(gathered)

    def make(in_refs, io_refs, new_refs, send_sems, recv_sems, local_sems, base=0, local_base=0):
        x, y, c = _place()
        copies = []
        for w in range(n):
            for j, chip in enumerate(_other_chips(x, y)):
                slot = _block_slot(io_refs[w], col_major[w], cols[w], (*chip, c), rows)
                copies.append(pltpu.make_async_remote_copy(
                    src_ref=slot, dst_ref=slot, send_sem=send_sems.at[base + 3 * w + j],
                    recv_sem=recv_sems.at[base + 3 * w + j], device_id=(x, y, 1 - c), device_id_type=MESH))
        return copies

    return _Exchange([], gathered, [], 3 * n, 0, make)


def _both(a, b):
    def make(in_refs, io_refs, new_refs, send_sems, recv_sems, local_sems):
        na = len(a.ins)
        return (a.make(in_refs[:na], io_refs, [], send_sems, recv_sems, local_sems, 0, 0)
                + b.make(in_refs[na:], io_refs, [], send_sems, recv_sems, local_sems, a.n_sems, a.n_local))

    return _Exchange(a.ins + b.ins, a.io, [], a.n_sems + b.n_sems, a.n_local + b.n_local, make)


def _rs_sibling(parts):
    n = len(parts)

    def make(in_refs, io_refs, new_refs, send_sems, recv_sems, local_sems):
        x, y, c = _place()
        copies = []
        for w in range(n):
            for j in range(4):
                copies.append(pltpu.make_async_remote_copy(
                    src_ref=in_refs[w].at[2 * j + (1 - c)], dst_ref=new_refs[w].at[j],
                    send_sem=send_sems.at[4 * w + j], recv_sem=recv_sems.at[4 * w + j],
                    device_id=(x, y, 1 - c), device_id_type=MESH))
        return copies

    return _Exchange(parts, [], [jax.ShapeDtypeStruct((4,) + p.shape[1:], p.dtype) for p in parts], 4 * n, 0, make)


def _rs_chips(chip_parts, rows=None, into=None):
    n = len(chip_parts)
    band = slice(None) if rows is None else pl.ds(rows[0], rows[1])
    new = [] if into is not None else [jax.ShapeDtypeStruct((3,) + p.shape[1:], p.dtype) for p in chip_parts]

    def make(in_refs, io_refs, new_refs, send_sems, recv_sems, local_sems):
        x, y, c = _place()
        landing = io_refs if into is not None else new_refs
        copies = []
        for w in range(n):
            for rel, (px, py) in enumerate(_other_chips(x, y)):
                copies.append(pltpu.make_async_remote_copy(
                    src_ref=in_refs[w].at[2 * px + py, band], dst_ref=landing[w].at[rel, band],
                    send_sem=send_sems.at[3 * w + rel], recv_sem=recv_sems.at[3 * w + rel],
                    device_id=(px, py, c), device_id_type=MESH))
        return copies

    return _Exchange(chip_parts, into or [], new, 3 * n, 0, make)


class _Side:
    def __init__(self, ins, in_blocks, out_shapes, out_blocks, n_tiles, fn):
        self.ins, self.in_blocks, self.out_shapes, self.out_blocks = list(ins), in_blocks, list(out_shapes), out_blocks
        self.n_tiles, self.fn = n_tiles, fn


def _call(body, exch, *, name, grid, in_specs, out_specs, out_shape, scratch_shapes=(), semantics,
          input_output_aliases=None):
    exch = list(exch)
    in_specs, out_specs, out_shape = list(in_specs), list(out_specs), list(out_shape)
    scratch_shapes = list(scratch_shapes)
    if not exch:
        fn = pl.pallas_call(body, name=name, grid=grid, in_specs=in_specs, out_specs=out_specs, out_shape=out_shape,
                            scratch_shapes=scratch_shapes, input_output_aliases=input_output_aliases or {},
                            compiler_params=_cparams(*semantics))
        return lambda *args: (fn(*args), [])
    n_in, n_out, n_scr = len(in_specs), len(out_specs), len(scratch_shapes)
    aliases = dict(input_output_aliases or {})
    all_in, all_out_specs, all_out_shape, all_scr = list(in_specs), list(out_specs), list(out_shape), list(scratch_shapes)
    extra_args = []

    def step(idx):
        s = idx[0]
        for a in range(1, len(grid)):
            s = s * grid[a] + idx[a]
        return s

    def tile_spec(shape, where, n_tiles):
        return pl.BlockSpec(shape, lambda *idx: where(jnp.minimum(step(idx), n_tiles - 1)))

    for ex in exch:
        if isinstance(ex, _Side):
            all_in += [tile_spec(shape, where, ex.n_tiles) for shape, where in ex.in_blocks]
            extra_args += ex.ins
            all_out_specs += [tile_spec(shape, where, ex.n_tiles) for shape, where in ex.out_blocks]
            all_out_shape += ex.out_shapes
            continue
        for k, a in enumerate(ex.io):
            aliases[len(all_in) + len(ex.ins) + k] = len(all_out_specs) + k
        all_in += [ANY] * (len(ex.ins) + len(ex.io))
        extra_args += ex.ins + ex.io
        all_out_specs += [ANY] * (len(ex.io) + len(ex.new))
        all_out_shape += [jax.ShapeDtypeStruct(a.shape, a.dtype) for a in ex.io] + ex.new
        all_scr += [pltpu.SemaphoreType.DMA((ex.n_sems,)), pltpu.SemaphoreType.DMA((ex.n_sems,)),
                    pltpu.SemaphoreType.DMA((max(ex.n_local, 1),))]

    n_ins = [len(ex.ins) if isinstance(ex, _Side) else len(ex.ins) + len(ex.io) for ex in exch]
    n_outs = [len(ex.out_shapes) if isinstance(ex, _Side) else len(ex.io) + len(ex.new) for ex in exch]

    def wrapped(*refs):
        pos = n_in
        ex_in = []
        for k in n_ins:
            ex_in.append(refs[pos:pos + k])
            pos += k
        outs = refs[pos:pos + n_out]
        pos += n_out
        ex_out = []
        for k in n_outs:
            ex_out.append(refs[pos:pos + k])
            pos += k
        scr = refs[pos:pos + n_scr]
        pos += n_scr
        idx = [pl.program_id(a) for a in range(len(grid))]
        first = functools.reduce(jnp.logical_and, [i == 0 for i in idx])
        last = functools.reduce(jnp.logical_and, [i == g - 1 for i, g in zip(idx, grid)])

        def copies():
            out, at = [], pos
            for ex, ei, eo in zip(exch, ex_in, ex_out):
                if not isinstance(ex, _Side):
                    out += ex.make(ei[:len(ex.ins)], eo[:len(ex.io)], eo[len(ex.io):], *refs[at:at + 3])
                    at += 3
            return out

        @pl.when(first)
        def _():
            for cp in copies():
                cp.start()

        body(*refs[:n_in], *outs, *scr)
        for ex, ei, eo in zip(exch, ex_in, ex_out):
            if isinstance(ex, _Side):
                pl.when(step(idx) < ex.n_tiles)(functools.partial(ex.fn, ei, eo))

        @pl.when(last)
        def _():
            for cp in copies():
                cp.wait()

    fn = pl.pallas_call(wrapped, name=name, grid=grid, in_specs=all_in, out_specs=all_out_specs,
                        out_shape=all_out_shape, scratch_shapes=all_scr, input_output_aliases=aliases,
                        compiler_params=_cparams(*(["arbitrary"] * len(grid))))

    def run(*args):
        res = fn(*args, *extra_args)
        outs, pos, ex_res = res[:n_out], n_out, []
        for k in n_outs:
            ex_res.append(list(res[pos:pos + k]))
            pos += k
        return outs, ex_res

    return run


def _exchange_alone(ex, name):
    def body():
        pass

    _, res = _call(body, [ex], name=name, grid=(1,), in_specs=[], out_specs=[], out_shape=[], semantics=("arbitrary",))()
    return res[0]


def _small_gather(part):
    def make(in_refs, io_refs, new_refs, send_sems, recv_sems, local_sems):
        x, y, c = _place()
        slot = new_refs[0].at[4 * x + 2 * y + c]
        copies = [pltpu.make_async_copy(in_refs[0], slot, local_sems.at[0])]
        for d in range(1, N_DEV):
            peer = (1 - x if d & 4 else x, 1 - y if d & 2 else y, 1 - c if d & 1 else c)
            copies.append(pltpu.make_async_remote_copy(
                src_ref=in_refs[0], dst_ref=slot, send_sem=send_sems.at[d - 1], recv_sem=recv_sems.at[d - 1],
                device_id=peer, device_id_type=MESH))
        return copies

    return _Exchange([part], [], [jax.ShapeDtypeStruct((N_DEV,) + part.shape, part.dtype)], N_DEV - 1, 1, make)


def _sum_over_devices(parts):
    _, rows, lanes = parts.shape

    def body(p_ref, o_ref):
        acc = p_ref[0]
        for k in range(1, N_DEV):
            acc = acc + p_ref[k]
        o_ref[...] = acc

    return pl.pallas_call(
        body, name="small_grads_sum", grid=(1,), out_shape=jax.ShapeDtypeStruct((rows, lanes), F32),
        in_specs=[pl.BlockSpec((N_DEV, rows, lanes), lambda i: (0, 0, 0))],
        out_specs=pl.BlockSpec((rows, lanes), lambda i: (0, 0)),
        compiler_params=_cparams("arbitrary"),
    )(parts)


def _transpose_bf16(a, name, exch=(), with_copy=False):
    r, c = a.shape
    tr, tc = _tile(r, 512, 128), _tile(c, 512, 128)

    def body(a_ref, o_ref, *copy_ref):
        v = a_ref[...].astype(F32)
        o_ref[...] = v.T.astype(BF16)
        if with_copy:
            copy_ref[0][...] = v.astype(BF16)

    outs, ex = _call(
        body, exch, name=name, grid=(r // tr, c // tc),
        out_shape=[jax.ShapeDtypeStruct((c, r), BF16)] + [jax.ShapeDtypeStruct((r, c), BF16)] * with_copy,
        in_specs=[pl.BlockSpec((tr, tc), lambda i, j: (i, j))],
        out_specs=[pl.BlockSpec((tc, tr), lambda i, j: (j, i))] + [pl.BlockSpec((tr, tc), lambda i, j: (i, j))] * with_copy,
        semantics=("parallel", "parallel"),
    )(a)
    return (outs if with_copy else outs[0]), ex


def _ffn_fwd_loss(x, wgu, wd, ln_g, ln_b, target, name, exch=()):
    t, d = x.shape
    f = wd.shape[0]
    tm, tf = _tile(t, 512, 128), _tile(f, 512, 128)
    nf = f // tf

    def body(x_ref, wg_ref, wu_ref, wd_ref, lg_ref, lb_ref, t_ref,
             go_ref, uo_ref, ht_ref, dz_ref, dzb_ref, dlg_ref, dlb_ref, loss_ref, xb, acc):
        i, j = pl.program_id(0), pl.program_id(1)

        @pl.when(j == 0)
        def _():
            xb[...] = x_ref[...].astype(BF16)
            acc[...] = jnp.zeros_like(acc)

        @pl.when((i == 0) & (j == 0))
        def _():
            dlg_ref[...] = jnp.zeros_like(dlg_ref)
            dlb_ref[...] = jnp.zeros_like(dlb_ref)
            loss_ref[...] = jnp.zeros_like(loss_ref)

        g = _dot(xb[...], wg_ref[...])
        u = _dot(xb[...], wu_ref[...])
        h = g * _sigmoid(g) * u
        go_ref[...] = g.astype(BF16)
        uo_ref[...] = u.astype(BF16)
        ht_ref[...] = h.T.astype(BF16)
        acc[...] += _dot(h.astype(BF16), wd_ref[...])

        @pl.when(j == nf - 1)
        def _():
            for r in range(0, tm, EPILOGUE_ROWS):
                rows = slice(r, r + EPILOGUE_ROWS)
                xh, rstd = _ln_stats(ALPHA * x_ref[rows, :] + 0.5 * acc[rows, :])
                e = xh * lg_ref[...] + lb_ref[...] - t_ref[rows, :]
                loss_ref[...] += 0.5 * jnp.sum(jnp.sum(e * e, axis=-1, keepdims=True) * (1.0 / d), axis=0,
                                               keepdims=True)
                dy = e * (1.0 / d)
                dz = _ln_bwd(dy * lg_ref[...], xh, rstd)
                dz_ref[rows, :] = dz
                dzb_ref[rows, :] = (0.5 * dz).astype(BF16)
                dlg_ref[...] += jnp.sum(dy * xh, axis=0, keepdims=True)
                dlb_ref[...] += jnp.sum(dy, axis=0, keepdims=True)

    row = lambda i, j: (i, 0)
    fixed = lambda i, j: (0, 0)
    return _call(
        body, exch, name=name, grid=(t // tm, nf),
        out_shape=[jax.ShapeDtypeStruct((t, f), BF16), jax.ShapeDtypeStruct((t, f), BF16),
                   jax.ShapeDtypeStruct((f, t), BF16), jax.ShapeDtypeStruct((t, d), F32),
                   jax.ShapeDtypeStruct((t, d), BF16), jax.ShapeDtypeStruct((1, d), F32),
                   jax.ShapeDtypeStruct((1, d), F32), jax.ShapeDtypeStruct((8, 128), F32)],
        in_specs=[pl.BlockSpec((tm, d), row),
                  pl.BlockSpec((d, tf), lambda i, j: (0, j)),
                  pl.BlockSpec((d, tf), lambda i, j: (0, j + nf)),
                  pl.BlockSpec((tf, d), lambda i, j: (j, 0)),
                  pl.BlockSpec((1, d), fixed), pl.BlockSpec((1, d), fixed), pl.BlockSpec((tm, d), row)],
        out_specs=[pl.BlockSpec((tm, tf), lambda i, j: (i, j)), pl.BlockSpec((tm, tf), lambda i, j: (i, j)),
                   pl.BlockSpec((tf, tm), lambda i, j: (j, i)), pl.BlockSpec((tm, d), row), pl.BlockSpec((tm, d), row),
                   pl.BlockSpec((1, d), fixed), pl.BlockSpec((1, d), fixed), pl.BlockSpec((8, 128), fixed)],
        scratch_shapes=[pltpu.VMEM((tm, d), BF16), pltpu.VMEM((tm, d), F32)],
        semantics=("arbitrary", "arbitrary"),
    )(x, wgu, wgu, wd, ln_g, ln_b, target)


def _ffn_down_fwd(gu, x, wd, ln_g, ln_b, name, exch=()):
    t, d = x.shape
    f = wd.shape[0]
    tm, tf = _tile(t, 512, 128), _tile(f, 512, 128)
    nf = f // tf

    def body(g_ref, u_ref, wd_ref, x_ref, lg_ref, lb_ref, ht_ref, z_ref, xn_ref, acc):
        j = pl.program_id(1)

        @pl.when(j == 0)
        def _():
            acc[...] = jnp.zeros_like(acc)

        g = g_ref[...].astype(F32)
        h = g * _sigmoid(g) * u_ref[...].astype(F32)
        ht_ref[...] = h.T.astype(BF16)
        acc[...] += _dot(h.astype(BF16), wd_ref[...])

        @pl.when(j == nf - 1)
        def _():
            z = ALPHA * x_ref[...] + 0.5 * acc[...]
            z_ref[...] = z
            xn_ref[...] = _ln(z, lg_ref[...], lb_ref[...])

    row = lambda i, j: (i, 0)
    fixed = lambda i, j: (0, 0)
    return _call(
        body, exch, name=name, grid=(t // tm, nf),
        out_shape=[jax.ShapeDtypeStruct((f, t), BF16), jax.ShapeDtypeStruct((t, d), F32),
                   jax.ShapeDtypeStruct((t, d), F32)],
        in_specs=[pl.BlockSpec((tm, tf), lambda i, j: (i, j)), pl.BlockSpec((tm, tf), lambda i, j: (i, j + nf)),
                  pl.BlockSpec((tf, d), lambda i, j: (j, 0)), pl.BlockSpec((tm, d), row),
                  pl.BlockSpec((1, d), fixed), pl.BlockSpec((1, d), fixed)],
        out_specs=[pl.BlockSpec((tf, tm), lambda i, j: (j, i)), pl.BlockSpec((tm, d), row), pl.BlockSpec((tm, d), row)],
        scratch_shapes=[pltpu.VMEM((tm, d), F32)],
        semantics=("parallel", "arbitrary"),
    )(gu, gu, wd, x, ln_g, ln_b)


def _ffn_act_grads(dh, g_ref, u_ref):
    gg = g_ref[...].astype(F32)
    uu = u_ref[...].astype(F32)
    s = _sigmoid(gg)
    du = (dh * (gg * s)).astype(BF16)
    dg = (dh * uu * (s * (1.0 + gg * (1.0 - s)))).astype(BF16)
    return dg, du


def _ffn_bwd(dz, do, g, u, wgu, wd, name, exch=()):
    t, d = dz.shape
    f = wd.shape[0]
    tm, tf = _tile(t, 512, 128), _tile(f, 512, 128)
    nf = f // tf

    def body(dz_ref, do_ref, g_ref, u_ref, wg_ref, wu_ref, wd_ref, dg_ref, du_ref, dx_ref, acc):
        j = pl.program_id(1)

        @pl.when(j == 0)
        def _():
            acc[...] = jnp.zeros_like(acc)

        dg, du = _ffn_act_grads(_dot_nt(do_ref[...], wd_ref[...]), g_ref, u_ref)
        dg_ref[...] = dg
        du_ref[...] = du
        acc[...] += _dot_nt(dg, wg_ref[...]) + _dot_nt(du, wu_ref[...])

        @pl.when(j == nf - 1)
        def _():
            dx_ref[...] = ALPHA * dz_ref[...] + acc[...]

    row = lambda i, j: (i, 0)
    tile = lambda i, j: (i, j)
    return _call(
        body, exch, name=name, grid=(t // tm, nf),
        out_shape=[jax.ShapeDtypeStruct((t, f), BF16), jax.ShapeDtypeStruct((t, f), BF16),
                   jax.ShapeDtypeStruct((t, d), F32)],
        in_specs=[pl.BlockSpec((tm, d), row), pl.BlockSpec((tm, d), row),
                  pl.BlockSpec((tm, tf), tile), pl.BlockSpec((tm, tf), tile),
                  pl.BlockSpec((d, tf), lambda i, j: (0, j)),
                  pl.BlockSpec((d, tf), lambda i, j: (0, j + nf)),
                  pl.BlockSpec((tf, d), lambda i, j: (j, 0))],
        out_specs=[pl.BlockSpec((tm, tf), tile), pl.BlockSpec((tm, tf), tile), pl.BlockSpec((tm, d), row)],
        scratch_shapes=[pltpu.VMEM((tm, d), F32)],
        semantics=("parallel", "arbitrary"),
    )(dz, do, g, u, wgu, wgu, wd)


def _ffn_bwd_act(do, gu, wd, name, exch=()):
    t, d = do.shape
    f = wd.shape[0]
    tm, tf = _tile(t, 2048, 128), _tile(f, 512, 128)
    nf = f // tf

    def body(do_ref, g_ref, u_ref, wd_ref, dg_ref, du_ref):
        dg, du = _ffn_act_grads(_dot_nt(do_ref[...], wd_ref[...]), g_ref, u_ref)
        dg_ref[...] = dg
        du_ref[...] = du

    tile = lambda i, j: (i, j)
    return _call(
        body, exch, name=name, grid=(t // tm, f // tf),
        out_shape=[jax.ShapeDtypeStruct((t, f), BF16), jax.ShapeDtypeStruct((t, f), BF16)],
        in_specs=[pl.BlockSpec((tm, d), lambda i, j: (i, 0)), pl.BlockSpec((tm, tf), tile),
                  pl.BlockSpec((tm, tf), lambda i, j: (i, j + nf)), pl.BlockSpec((tf, d), lambda i, j: (j, 0))],
        out_specs=[pl.BlockSpec((tm, tf), tile), pl.BlockSpec((tm, tf), tile)],
        semantics=("parallel", "parallel"),
    )(do, gu, gu, wd)


def _ffn_bwd_dx(dz, dg, du, wgu, name, exch=()):
    t, d = dz.shape
    f = dg.shape[1]
    tm, tn = _tile(t, 512, 128), _tile(d, 256, 128)

    def body(dz_ref, dg_ref, du_ref, wg_ref, wu_ref, dx_ref):
        dx_ref[...] = ALPHA * dz_ref[...] + _dot_nt(dg_ref[...], wg_ref[...]) + _dot_nt(du_ref[...], wu_ref[...])

    row = lambda i, n: (i, 0)
    tile = lambda i, n: (i, n)
    return _call(
        body, exch, name=name, grid=(t // tm, d // tn), out_shape=[jax.ShapeDtypeStruct((t, d), F32)],
        in_specs=[pl.BlockSpec((tm, tn), tile), pl.BlockSpec((tm, f), row), pl.BlockSpec((tm, f), row),
                  pl.BlockSpec((tn, f), lambda i, n: (n, 0)), pl.BlockSpec((tn, f), lambda i, n: (n, 1))],
        out_specs=[pl.BlockSpec((tm, tn), tile)],
        semantics=("parallel", "arbitrary"),
    )(dz, dg, du, wgu, wgu)


def _weight_grad(at, b, tn, tmm, name, blocks=None, block_offset=0, into=None, exch=()):
    m, t = at.shape
    nn = b.shape[1]
    tmm = _tile(m, tmm, 16)
    assert nn % tn == 0

    def body(*refs):
        at_ref, b_ref, o_ref = refs[0], refs[1], refs[-1]
        r = _dot(at_ref[...], b_ref[...]).astype(BF16)
        if blocks is None:
            o_ref[...] = r
        else:
            o_ref[0] = r

    in_specs = [pl.BlockSpec((tmm, t), lambda n, i: (i, 0)), pl.BlockSpec((t, tn), lambda n, i: (0, n))]
    args = [at, b]
    aliases = {}
    if into is not None:
        in_specs.append(ANY)
        args.append(into)
        aliases = {2: 0}
    if blocks is None:
        out_shape = jax.ShapeDtypeStruct((m, nn), BF16)
        out_spec = pl.BlockSpec((tmm, tn), lambda n, i: (i, n))
    else:
        out_shape = jax.ShapeDtypeStruct((blocks, m, tn), BF16)
        out_spec = pl.BlockSpec((1, tmm, tn), lambda n, i: (n + block_offset, i, 0))
    (out,), ex = _call(
        body, exch, name=name, grid=(nn // tn, m // tmm), out_shape=[out_shape],
        in_specs=in_specs, out_specs=[out_spec], input_output_aliases=aliases,
        semantics=("parallel", "parallel"),
    )(*args)
    return out, ex


def _mix_in_proj(x, w_in, name, exch=()):
    t, d = x.shape
    n_out = w_in.shape[1]
    tm, cb = _tile(t, 512, 128), _tile(n_out, 1024, 128)

    def body(x_ref, w_ref, o_ref, xb):
        @pl.when(pl.program_id(1) == 0)
        def _():
            xb[...] = x_ref[...].astype(BF16)

        o_ref[...] = _dot(xb[...], w_ref[...])

    (out,), ex = _call(
        body, exch, name=name, grid=(t // tm, n_out // cb), out_shape=[jax.ShapeDtypeStruct((t, n_out), F32)],
        in_specs=[pl.BlockSpec((tm, d), lambda i, k: (i, 0)), pl.BlockSpec((d, cb), lambda i, k: (0, k))],
        out_specs=[pl.BlockSpec((tm, cb), lambda i, k: (i, k))],
        scratch_shapes=[pltpu.VMEM((tm, d), BF16)],
        semantics=("parallel", "arbitrary"),
    )(x, w_in)
    return out, ex


def _mix_in_bwd(dproj, w_in, dz, name, exch=()):
    t, d = dz.shape
    kk = w_in.shape[1]
    tm, tn = _tile(t, 512, 128), _tile(d, 512, 128)

    def body(dp_ref, w_ref, dz_ref, dx_ref):
        dx_ref[...] = ALPHA * dz_ref[...] + _dot_nt(dp_ref[...], w_ref[...])

    (out,), ex = _call(
        body, exch, name=name, grid=(t // tm, d // tn), out_shape=[jax.ShapeDtypeStruct((t, d), F32)],
        in_specs=[pl.BlockSpec((tm, kk), lambda i, n: (i, 0)), pl.BlockSpec((tn, kk), lambda i, n: (n, 0)),
                  pl.BlockSpec((tm, tn), lambda i, n: (i, n))],
        out_specs=[pl.BlockSpec((tm, tn), lambda i, n: (i, n))],
        semantics=("parallel", "arbitrary"),
    )(dproj, w_in, dz)
    return out, ex


def _mix_out_fwd(y, w_out, x, ln_g, ln_b, name, exch=()):
    t, d = x.shape
    kk = y.shape[1]
    tm = _tile(t, 256, 128)

    def body(y_ref, w_ref, x_ref, g_ref, b_ref, z_ref, xn_ref, xnt_ref):
        z = ALPHA * x_ref[...] + _dot(y_ref[...], w_ref[...])
        z_ref[...] = z
        xn = _ln(z, g_ref[...], b_ref[...])
        xn_ref[...] = xn
        xnt_ref[...] = xn.T.astype(BF16)

    row = lambda i: (i, 0)
    fixed = lambda i: (0, 0)
    return _call(
        body, exch, name=name, grid=(t // tm,),
        out_shape=[jax.ShapeDtypeStruct((t, d), F32), jax.ShapeDtypeStruct((t, d), F32),
                   jax.ShapeDtypeStruct((d, t), BF16)],
        in_specs=[pl.BlockSpec((tm, kk), row), pl.BlockSpec((kk, d), fixed), pl.BlockSpec((tm, d), row),
                  pl.BlockSpec((1, d), fixed), pl.BlockSpec((1, d), fixed)],
        out_specs=[pl.BlockSpec((tm, d), row), pl.BlockSpec((tm, d), row), pl.BlockSpec((d, tm), lambda i: (0, i))],
        semantics=("parallel",),
    )(y, w_out, x, ln_g, ln_b)


def _mix_out_bwd(dzb, w_out, name, exch=()):
    t, d = dzb.shape
    kk = w_out.shape[0]
    tm = _tile(t, 512, 128)

    def body(dz_ref, w_ref, dy_ref):
        dy_ref[...] = _dot_nt(dz_ref[...], w_ref[...])

    (out,), ex = _call(
        body, exch, name=name, grid=(t // tm,), out_shape=[jax.ShapeDtypeStruct((t, kk), F32)],
        in_specs=[pl.BlockSpec((tm, d), lambda i: (i, 0)), pl.BlockSpec((kk, d), lambda i: (0, 0))],
        out_specs=[pl.BlockSpec((tm, kk), lambda i: (i, 0))],
        semantics=("parallel",),
    )(dzb, w_out)
    return out, ex


def _loss_ln_bwd(z, target, ln_g, ln_b, bf16_scale, name):
    t, d = z.shape
    tm = _tile(t, 512, 8)

    def body(z_ref, t_ref, g_ref, b_ref, dz_ref, dzb_ref, dg_ref, db_ref, loss_ref):
        @pl.when(pl.program_id(0) == 0)
        def _():
            dg_ref[...] = jnp.zeros_like(dg_ref)
            db_ref[...] = jnp.zeros_like(db_ref)
            loss_ref[...] = jnp.zeros_like(loss_ref)

        xh, rstd = _ln_stats(z_ref[...])
        e = xh * g_ref[...] + b_ref[...] - t_ref[...]
        loss_ref[...] += 0.5 * jnp.sum(jnp.sum(e * e, axis=-1, keepdims=True) * (1.0 / d), axis=0, keepdims=True)
        dy = e * (1.0 / d)
        dz = _ln_bwd(dy * g_ref[...], xh, rstd)
        dz_ref[...] = dz
        dzb_ref[...] = (bf16_scale * dz).astype(BF16)
        dg_ref[...] += jnp.sum(dy * xh, axis=0, keepdims=True)
        db_ref[...] += jnp.sum(dy, axis=0, keepdims=True)

    row = lambda i: (i, 0)
    fixed = lambda i: (0, 0)
    return pl.pallas_call(
        body, name=name, grid=(t // tm,),
        out_shape=[jax.ShapeDtypeStruct((t, d), F32), jax.ShapeDtypeStruct((t, d), BF16),
                   jax.ShapeDtypeStruct((1, d), F32), jax.ShapeDtypeStruct((1, d), F32),
                   jax.ShapeDtypeStruct((8, 128), F32)],
        in_specs=[pl.BlockSpec((tm, d), row), pl.BlockSpec((tm, d), row), pl.BlockSpec((1, d), fixed),
                  pl.BlockSpec((1, d), fixed)],
        out_specs=[pl.BlockSpec((tm, d), row), pl.BlockSpec((tm, d), row), pl.BlockSpec((1, d), fixed),
                   pl.BlockSpec((1, d), fixed), pl.BlockSpec((8, 128), fixed)],
        compiler_params=_cparams("arbitrary"),
    )(z, target, ln_g, ln_b)


def _ln_bwd_call(z, dy, ln_g, bf16_scale, name, exch=()):
    t, d = z.shape
    tm = _tile(t, 512, 8)

    def body(z_ref, dy_ref, g_ref, dz_ref, dzb_ref, dg_ref, db_ref):
        @pl.when(pl.program_id(0) == 0)
        def _():
            dg_ref[...] = jnp.zeros_like(dg_ref)
            db_ref[...] = jnp.zeros_like(db_ref)

        xh, rstd = _ln_stats(z_ref[...])
        dy = dy_ref[...]
        dz = _ln_bwd(dy * g_ref[...], xh, rstd)
        dz_ref[...] = dz
        dzb_ref[...] = (bf16_scale * dz).astype(BF16)
        dg_ref[...] += jnp.sum(dy * xh, axis=0, keepdims=True)
        db_ref[...] += jnp.sum(dy, axis=0, keepdims=True)

    row = lambda i: (i, 0)
    fixed = lambda i: (0, 0)
    return _call(
        body, exch, name=name, grid=(t // tm,),
        out_shape=[jax.ShapeDtypeStruct((t, d), F32), jax.ShapeDtypeStruct((t, d), BF16),
                   jax.ShapeDtypeStruct((1, d), F32), jax.ShapeDtypeStruct((1, d), F32)],
        in_specs=[pl.BlockSpec((tm, d), row), pl.BlockSpec((tm, d), row), pl.BlockSpec((1, d), fixed)],
        out_specs=[pl.BlockSpec((tm, d), row), pl.BlockSpec((tm, d), row), pl.BlockSpec((1, d), fixed),
                   pl.BlockSpec((1, d), fixed)],
        semantics=("arbitrary",),
    )(z, dy, ln_g)


CONV_ROWS = 32
CONV_LANES = 512
SUBLANES = 8


def _fill_shifted(ext, shifted):
    rows = ext.shape[0] - SUBLANES
    for s in range(1, SUBLANES):
        for r in range(0, rows, CONV_ROWS):
            n = min(CONV_ROWS, rows - r)
            shifted[s - 1, r:r + n, :] = ext[r + s:r + s + n, :]


def _window(ext, shifted, lo, n, lanes=slice(None)):
    s = lo % SUBLANES
    return ext[lo:lo + n, lanes] if s == 0 else shifted[s - 1, lo - s:lo - s + n, lanes]


def _mixer_fwd(proj, conv_w, conv_b, cln_g, cln_b, sln_g, sln_b, sg_wm, sg_bb, name, exch=()):
    t = proj.shape[0]
    tm = _tile(t, 256, CHUNK)
    hb = tm // HALO
    nc = tm // CHUNK
    ch = CONV_CH

    def body(av_ref, ag_ref, bu_ref, bv_ref, hv_ref, hg_ref, cw_ref, cb_ref, lg_ref, lb_ref, sg_ref, sb_ref,
             w_ref, bb_ref, y_ref, yt_ref, c_ref, ext, ext_s):
        i = pl.program_id(0)
        halo = hv_ref[...] * _sigmoid(hg_ref[...])
        ext[0:HALO, :] = jnp.where(i > 0, halo, 0.0)
        ext[HALO:HALO + tm, :] = av_ref[...] * _sigmoid(ag_ref[...])
        _fill_shifted(ext, ext_s)
        for r in range(0, tm, CONV_ROWS):
            acc = jnp.zeros((CONV_ROWS, ch), F32) + cb_ref[...]
            for k in range(CONV_TAPS):
                lo = r + k + HALO - (CONV_TAPS - 1)
                acc = acc + cw_ref[k:k + 1, :] * _window(ext, ext_s, lo, CONV_ROWS)
            c_ref[r:r + CONV_ROWS, :] = acc
        a = _ln(c_ref[...], lg_ref[...], lb_ref[...])
        ya = a * _sigmoid(a)
        y_ref[:, 0:ch] = ya.astype(BF16)
        yt_ref[0:ch, :] = ya.T.astype(BF16)
        for h in range(HEADS):
            sl = slice(h * HEAD_DIM, (h + 1) * HEAD_DIM)
            u, _ = _gelu_and_grad(bu_ref[:, sl])
            v, _ = _gelu_and_grad(bv_ref[:, sl])
            vn = _ln(v, sg_ref[h:h + 1, :], sb_ref[h:h + 1, :])
            vn3 = vn.astype(BF16).reshape(nc, CHUNK, HEAD_DIM)
            wb = jnp.broadcast_to(w_ref[h][None], (nc, CHUNK, CHUNK))
            mixed = jnp.einsum("cts,csd->ctd", wb, vn3, preferred_element_type=F32) + bb_ref[h][None]
            yb = u * mixed.reshape(tm, HEAD_DIM)
            y_ref[:, ch + h * HEAD_DIM:ch + (h + 1) * HEAD_DIM] = yb.astype(BF16)
            yt_ref[ch + h * HEAD_DIM:ch + (h + 1) * HEAD_DIM, :] = yb.T.astype(BF16)

    col = lambda cidx: (lambda i: (i, cidx))
    prev = lambda cidx: (lambda i: (jnp.maximum(i * hb - 1, 0), cidx))
    fix2 = lambda i: (0, 0)
    fix3 = lambda i: (0, 0, 0)
    return _call(
        body, exch, name=name, grid=(t // tm,),
        out_shape=[jax.ShapeDtypeStruct((t, 2 * ch), BF16), jax.ShapeDtypeStruct((2 * ch, t), BF16),
                   jax.ShapeDtypeStruct((t, ch), F32)],
        in_specs=[pl.BlockSpec((tm, ch), col(0)), pl.BlockSpec((tm, ch), col(1)), pl.BlockSpec((tm, ch), col(2)),
                  pl.BlockSpec((tm, ch), col(3)), pl.BlockSpec((HALO, ch), prev(0)), pl.BlockSpec((HALO, ch), prev(1)),
                  pl.BlockSpec((CONV_TAPS, ch), fix2), pl.BlockSpec((1, ch), fix2), pl.BlockSpec((1, ch), fix2),
                  pl.BlockSpec((1, ch), fix2), pl.BlockSpec((HEADS, HEAD_DIM), fix2), pl.BlockSpec((HEADS, HEAD_DIM), fix2),
                  pl.BlockSpec((HEADS, CHUNK, CHUNK), fix3), pl.BlockSpec((HEADS, CHUNK, HEAD_DIM), fix3)],
        out_specs=[pl.BlockSpec((tm, 2 * ch), lambda i: (i, 0)), pl.BlockSpec((2 * ch, tm), lambda i: (0, i)),
                   pl.BlockSpec((tm, ch), lambda i: (i, 0))],
        scratch_shapes=[pltpu.VMEM((HALO + tm, ch), F32), pltpu.VMEM((SUBLANES - 1, HALO + tm, ch), F32)],
        semantics=("parallel",),
    )(proj, proj, proj, proj, proj, proj, conv_w, conv_b, cln_g, cln_b, sln_g, sln_b, sg_wm, sg_bb)


def _mixer_bwd(proj, conv_c, dy, conv_w, cln_g, cln_b, sln_g, sln_b, sg_wm, sg_wmt, sg_bb, name, exch=()):
    t = proj.shape[0]
    tm = _tile(t, 256, CHUNK)
    hb = tm // HALO
    nc = tm // CHUNK
    nt = t // tm
    ch = CONV_CH
    last_halo = t // HALO - 1

    def body(av_ref, ag_ref, bu_ref, bv_ref, hv_ref, hg_ref, c_ref, cn_ref, dya_ref, dyan_ref, dyb_ref,
             cw_ref, lg_ref, lb_ref, sg_ref, sb_ref, w_ref, wt_ref, bb_ref,
             dp_ref, dcw_ref, dcb_ref, dlg_ref, dlb_ref, dsg_ref, dsb_ref, dw_ref, dbs_ref,
             ext_h, ext_dc, ext_hs, ext_dcs, acc_cw):
        i = pl.program_id(0)

        @pl.when(i == 0)
        def _():
            acc_cw[...] = jnp.zeros_like(acc_cw)
            for ref in (dcb_ref, dlg_ref, dlb_ref, dsg_ref, dsb_ref, dw_ref, dbs_ref):
                ref[...] = jnp.zeros_like(ref)

        lg = lg_ref[...]
        lb = lb_ref[...]

        def conv_ln_bwd(c, dya):
            xh, rstd = _ln_stats(c)
            a = xh * lg + lb
            da = dya * _silu_grad(a)
            return _ln_bwd(da * lg, xh, rstd), da, xh

        fold = lambda v: jnp.sum(v.reshape(CONV_ROWS // SUBLANES, SUBLANES, ch), axis=0)
        s_lg = s_lb = s_cb = jnp.zeros((SUBLANES, ch), F32)
        for r in range(0, tm, CONV_ROWS):
            dc, da, xh = conv_ln_bwd(c_ref[r:r + CONV_ROWS, :], dya_ref[r:r + CONV_ROWS, :])
            ext_dc[r:r + CONV_ROWS, :] = dc
            s_lg, s_lb, s_cb = s_lg + fold(da * xh), s_lb + fold(da), s_cb + fold(dc)
        dlg_ref[...] += jnp.sum(s_lg, axis=0, keepdims=True)
        dlb_ref[...] += jnp.sum(s_lb, axis=0, keepdims=True)
        dcb_ref[...] += jnp.sum(s_cb, axis=0, keepdims=True)
        dcn, _, _ = conv_ln_bwd(cn_ref[...], dyan_ref[...])
        ext_dc[tm:tm + HALO, :] = jnp.where(i < nt - 1, dcn, 0.0)
        halo = hv_ref[...] * _sigmoid(hg_ref[...])
        ext_h[0:HALO, :] = jnp.where(i > 0, halo, 0.0)
        ext_h[HALO:HALO + tm, :] = av_ref[...] * _sigmoid(ag_ref[...])
        _fill_shifted(ext_h, ext_hs)
        _fill_shifted(ext_dc, ext_dcs)
        for r, c0 in [(r, c0) for r in range(0, tm, CONV_ROWS) for c0 in range(0, ch, CONV_LANES)]:
            rows, lanes = slice(r, r + CONV_ROWS), slice(c0, c0 + CONV_LANES)
            dcr = ext_dc[rows, lanes]
            acc = jnp.zeros((CONV_ROWS, CONV_LANES), F32)
            for k in range(CONV_TAPS):
                lo = r + k + HALO - (CONV_TAPS - 1)
                prod = dcr * _window(ext_h, ext_hs, lo, CONV_ROWS, lanes)
                acc_cw[k, :, lanes] += jnp.sum(prod.reshape(CONV_ROWS // SUBLANES, SUBLANES, CONV_LANES), axis=0)
                hi = r + (CONV_TAPS - 1) - k
                acc = acc + cw_ref[k:k + 1, lanes] * _window(ext_dc, ext_dcs, hi, CONV_ROWS, lanes)
            sg_r = _sigmoid(ag_ref[rows, lanes])
            av_r = av_ref[rows, lanes]
            dp_ref[rows, lanes] = (acc * sg_r).astype(BF16)
            dp_ref[rows, slice(ch + c0, ch + c0 + CONV_LANES)] = (acc * av_r * sg_r * (1.0 - sg_r)).astype(BF16)

        @pl.when(i == nt - 1)
        def _():
            dcw_ref[...] = jnp.sum(acc_cw[...], axis=1)

        tril = (lax.broadcasted_iota(jnp.int32, (CHUNK, CHUNK), 0)
                >= lax.broadcasted_iota(jnp.int32, (CHUNK, CHUNK), 1)).astype(F32)
        for h in range(HEADS):
            sl = slice(h * HEAD_DIM, (h + 1) * HEAD_DIM)
            u, du_dx = _gelu_and_grad(bu_ref[:, sl])
            v, dv_dx = _gelu_and_grad(bv_ref[:, sl])
            xhv, rstdv = _ln_stats(v)
            gh = sg_ref[h:h + 1, :]
            vn3 = (xhv * gh + sb_ref[h:h + 1, :]).astype(BF16).reshape(nc, CHUNK, HEAD_DIM)
            wb = jnp.broadcast_to(w_ref[h][None], (nc, CHUNK, CHUNK))
            mixed = jnp.einsum("cts,csd->ctd", wb, vn3, preferred_element_type=F32) + bb_ref[h][None]
            dyb = dyb_ref[:, sl]
            d_u = dyb * mixed.reshape(tm, HEAD_DIM)
            dm = dyb * u
            dm3 = dm.reshape(nc, CHUNK, HEAD_DIM)
            dbs_ref[h:h + 1, :] += jnp.sum(jnp.sum(dm3, axis=0).T, axis=0, keepdims=True)
            dm3b = dm3.astype(BF16)
            dw_h = jnp.sum(jnp.einsum("ctd,csd->cts", dm3b, vn3, preferred_element_type=F32), axis=0)
            dw_ref[h] += dw_h * tril
            wtb = jnp.broadcast_to(wt_ref[h][None], (nc, CHUNK, CHUNK))
            d_vn = jnp.einsum("cst,ctd->csd", wtb, dm3b, preferred_element_type=F32).reshape(tm, HEAD_DIM)
            dsg_ref[h:h + 1, :] += jnp.sum(d_vn * xhv, axis=0, keepdims=True)
            dsb_ref[h:h + 1, :] += jnp.sum(d_vn, axis=0, keepdims=True)
            dv = _ln_bwd(d_vn * gh, xhv, rstdv)
            dp_ref[:, 2 * ch + h * HEAD_DIM:2 * ch + (h + 1) * HEAD_DIM] = (d_u * du_dx).astype(BF16)
            dp_ref[:, 3 * ch + h * HEAD_DIM:3 * ch + (h + 1) * HEAD_DIM] = (dv * dv_dx).astype(BF16)

    col = lambda cidx: (lambda i: (i, cidx))
    prev = lambda cidx: (lambda i: (jnp.maximum(i * hb - 1, 0), cidx))
    nxt = lambda i: (jnp.minimum((i + 1) * hb, last_halo), 0)
    fix2 = lambda i: (0, 0)
    fix3 = lambda i: (0, 0, 0)
    out_shape = [jax.ShapeDtypeStruct((t, 4 * ch), BF16), jax.ShapeDtypeStruct((CONV_TAPS, ch), F32),
                 jax.ShapeDtypeStruct((1, ch), F32), jax.ShapeDtypeStruct((1, ch), F32), jax.ShapeDtypeStruct((1, ch), F32),
                 jax.ShapeDtypeStruct((HEADS, HEAD_DIM), F32), jax.ShapeDtypeStruct((HEADS, HEAD_DIM), F32),
                 jax.ShapeDtypeStruct((HEADS, CHUNK, CHUNK), F32), jax.ShapeDtypeStruct((HEADS, CHUNK), F32)]
    out_specs = [pl.BlockSpec((tm, 4 * ch), lambda i: (i, 0)), pl.BlockSpec((CONV_TAPS, ch), fix2),
                 pl.BlockSpec((1, ch), fix2), pl.BlockSpec((1, ch), fix2), pl.BlockSpec((1, ch), fix2),
                 pl.BlockSpec((HEADS, HEAD_DIM), fix2), pl.BlockSpec((HEADS, HEAD_DIM), fix2),
                 pl.BlockSpec((HEADS, CHUNK, CHUNK), fix3), pl.BlockSpec((HEADS, CHUNK), fix2)]
    in_specs = [pl.BlockSpec((tm, ch), col(0)), pl.BlockSpec((tm, ch), col(1)), pl.BlockSpec((tm, ch), col(2)),
                pl.BlockSpec((tm, ch), col(3)), pl.BlockSpec((HALO, ch), prev(0)), pl.BlockSpec((HALO, ch), prev(1)),
                pl.BlockSpec((tm, ch), col(0)), pl.BlockSpec((HALO, ch), nxt),
                pl.BlockSpec((tm, ch), col(0)), pl.BlockSpec((HALO, ch), nxt), pl.BlockSpec((tm, ch), col(1)),
                pl.BlockSpec((CONV_TAPS, ch), fix2), pl.BlockSpec((1, ch), fix2), pl.BlockSpec((1, ch), fix2),
                pl.BlockSpec((HEADS, HEAD_DIM), fix2), pl.BlockSpec((HEADS, HEAD_DIM), fix2),
                pl.BlockSpec((HEADS, CHUNK, CHUNK), fix3), pl.BlockSpec((HEADS, CHUNK, CHUNK), fix3),
                pl.BlockSpec((HEADS, CHUNK, HEAD_DIM), fix3)]
    return _call(
        body, exch, name=name, grid=(nt,), out_shape=out_shape, in_specs=in_specs, out_specs=out_specs,
        scratch_shapes=[pltpu.VMEM((HALO + tm, ch), F32), pltpu.VMEM((tm + HALO, ch), F32),
                        pltpu.VMEM((SUBLANES - 1, HALO + tm, ch), F32), pltpu.VMEM((SUBLANES - 1, tm + HALO, ch), F32),
                        pltpu.VMEM((CONV_TAPS, 8, ch), F32)],
        semantics=("arbitrary",),
    )(proj, proj, proj, proj, proj, proj, conv_c, conv_c, dy, dy, dy,
      conv_w, cln_g, cln_b, sln_g, sln_b, sg_wm, sg_wmt, sg_bb)


def _pair_sum(parts, from_sibling, core_chip, name):
    _, r, cc = parts.shape
    tr = _tile(r, max(16, (1 << 20) // (2 * cc)), 16)

    def body(cc_ref, p_ref, s_ref, o_ref, own_ref):
        q = (p_ref[...].astype(F32) + s_ref[...].astype(F32)).astype(BF16)
        o_ref[...] = q

        @pl.when(pl.program_id(1) == cc_ref[1])
        def _():
            own_ref[...] = q[0]

    grid_spec = pltpu.PrefetchScalarGridSpec(
        num_scalar_prefetch=1, grid=(r // tr, 4),
        in_specs=[pl.BlockSpec((1, tr, cc), lambda i, j, cc_ref: (2 * j + cc_ref[0], i, 0)),
                  pl.BlockSpec((1, tr, cc), lambda i, j, cc_ref: (j, i, 0))],
        out_specs=[pl.BlockSpec((1, tr, cc), lambda i, j, cc_ref: (j, i, 0)),
                   pl.BlockSpec((tr, cc), lambda i, j, cc_ref: (i, 0))])
    return pl.pallas_call(
        body, name=name, grid_spec=grid_spec,
        out_shape=[jax.ShapeDtypeStruct((4, r, cc), BF16), jax.ShapeDtypeStruct((r, cc), BF16)],
        compiler_params=_cparams("parallel", "arbitrary"),
    )(core_chip, parts, from_sibling)


def _adamw_math(w, g, m, v):
    m = ADAM_B1 * m + (1.0 - ADAM_B1) * g
    v = ADAM_B2 * v + (1.0 - ADAM_B2) * (g * g)
    m_hat = m / (1.0 - ADAM_B1 ** ADAM_STEP)
    v_hat = v / (1.0 - ADAM_B2 ** ADAM_STEP)
    delta = -ADAM_LR * (m_hat / (jnp.sqrt(v_hat) + ADAM_EPS) + ADAM_WD * w)
    return delta, m, v


def _adamw_tile(in_refs, out_refs):
    w_ref, m_ref, v_ref, q_ref, o_ref = in_refs
    g = q_ref[...].astype(F32)
    for k in range(3):
        g = g + o_ref[k].astype(F32)
    d, mm, vv = _adamw_math(w_ref[...], g, m_ref[...], v_ref[...])
    for ref, val in zip(out_refs, (g, d, mm, vv)):
        ref[...] = val


def _adamw_side(w, m, v, chip_part, from_chips, max_tiles):
    r, cc = w.shape
    n = max(k for k in range(1, max_tiles + 1) if r % k == 0 and (r // k) % 16 == 0)
    tr = r // n
    row = ((tr, cc), lambda s: (s, 0))
    return _Side([w, m, v, chip_part, from_chips], [row, row, row, row, ((3, tr, cc), lambda s: (0, s, 0))],
                 [jax.ShapeDtypeStruct((r, cc), F32)] * 4, [row] * 4, n, _adamw_tile)


def _adamw_sharded(w, m, v, chip_part, from_chips, name):
    r, cc = w.shape
    tr = _tile(r, max(16, (1 << 19) // (4 * cc) * 2), 16)

    def body(*refs):
        _adamw_tile(refs[:5], refs[5:])

    row = pl.BlockSpec((tr, cc), lambda i: (i, 0))
    return pl.pallas_call(
        body, name=name, grid=(r // tr,), out_shape=[jax.ShapeDtypeStruct((r, cc), F32)] * 4,
        in_specs=[row, row, row, row, pl.BlockSpec((3, tr, cc), lambda i: (0, i, 0))], out_specs=[row] * 4,
        compiler_params=_cparams("parallel"),
    )(w, m, v, chip_part, from_chips)


def _adamw_small(w, g, m, v, name):
    r, cc = w.shape

    def body(w_ref, g_ref, m_ref, v_ref, d_out, m_out, v_out):
        d, mm, vv = _adamw_math(w_ref[...], g_ref[...], m_ref[...], v_ref[...])
        d_out[...] = d
        m_out[...] = mm
        v_out[...] = vv

    full = pl.BlockSpec((r, cc), lambda i: (0, 0))
    return pl.pallas_call(
        body, name=name, grid=(1,), out_shape=[jax.ShapeDtypeStruct((r, cc), F32)] * 3,
        in_specs=[full] * 4, out_specs=[full] * 3, compiler_params=_cparams("arbitrary"),
    )(w, g, m, v)


SMALL = ("ln1_g", "ln1_b", "conv_b", "conv_ln_g", "conv_ln_b", "sg_ln_g", "sg_ln_b", "sg_w", "sg_b",
         "ln2_g", "ln2_b", "ln3_g", "ln3_b")
ORDER = ("ffn1_w_gate_up", "ffn1_w_down", "ln1_g", "ln1_b", "mix_w_in", "conv_w", "conv_b", "conv_ln_g", "conv_ln_b",
         "sg_ln_g", "sg_ln_b", "sg_w", "sg_b", "mix_w_out", "ln2_g", "ln2_b", "ffn2_w_gate_up", "ffn2_w_down",
         "ln3_g", "ln3_b")


def _rows128(a):
    return a.reshape(-1, 128)


def kernel(x, ffn1_w_gate_up, ffn1_w_down, ln1_g, ln1_b, mix_w_in, conv_w, conv_b, conv_ln_g, conv_ln_b, sg_ln_g, sg_ln_b, sg_w, sg_b, mix_w_out, ln2_g, ln2_b, ffn2_w_gate_up, ffn2_w_down, ln3_g, ln3_b, loss_target, m_ffn1_w_gate_up, m_ffn1_w_down, m_ln1_g, m_ln1_b, m_mix_w_in, m_conv_w, m_conv_b, m_conv_ln_g, m_conv_ln_b, m_sg_ln_g, m_sg_ln_b, m_sg_w, m_sg_b, m_mix_w_out, m_ln2_g, m_ln2_b, m_ffn2_w_gate_up, m_ffn2_w_down, m_ln3_g, m_ln3_b, v_ffn1_w_gate_up, v_ffn1_w_down, v_ln1_g, v_ln1_b, v_mix_w_in, v_conv_w, v_conv_b, v_conv_ln_g, v_conv_ln_b, v_sg_ln_g, v_sg_ln_b, v_sg_w, v_sg_b, v_mix_w_out, v_ln2_g, v_ln2_b, v_ffn2_w_gate_up, v_ffn2_w_down, v_ln3_g, v_ln3_b):
    args = dict(locals())
    w = {n: args[n][0] for n in ORDER}
    mom = {n: args["m_" + n][0] for n in ORDER}
    var = {n: args["v_" + n][0] for n in ORDER}
    x0 = x[0]
    target = loss_target[0]
    t, d = x0.shape
    my_x, my_y, my_c = lax.axis_index("x"), lax.axis_index("y"), lax.axis_index("c")
    my_chip = (2 * my_x + my_y).astype(jnp.int32).reshape(1)
    my_core = my_c.astype(jnp.int32).reshape(1)
    me = 4 * my_x + 2 * my_y + my_c

    big = ("ffn1_w_gate_up", "ffn1_w_down", "mix_w_in", "mix_w_out", "ffn2_w_gate_up", "ffn2_w_down")
    sh = {n: w[n].astype(BF16) for n in big}
    f2s = sh["ffn2_w_gate_up"].shape[1]
    order = jnp.stack([4 * p[0] + 2 * p[1] + p[2] for p in _visit_order(my_x, my_y, my_c)]).astype(jnp.int32)
    gu1, x0t, (wgu1, wd1, conv_w_all) = _gather_and_gate_up(
        x0, [sh["ffn1_w_gate_up"], sh["ffn1_w_down"], w["conv_w"]], [True, True, False], order, "ffn1_gate_up_fwd")
    wd1 = wd1.reshape(-1, d)
    conv_w_full = jnp.transpose(conv_w_all, (1, 0, 2)).reshape(CONV_TAPS, CONV_CH)
    tril = jnp.tril(jnp.ones((CHUNK, CHUNK), F32))
    sg_wm = w["sg_w"] * tril
    sg_wm_b = sg_wm.astype(BF16)
    sg_wmt_b = jnp.swapaxes(sg_wm, 1, 2).astype(BF16)
    sg_bb = jnp.broadcast_to(w["sg_b"][:, :, None], (HEADS, CHUNK, HEAD_DIM))
    row = lambda a: a.reshape(1, -1)

    d2 = [sh["ffn2_w_down"]]
    d2_first = d2[0].shape[0] // 2 // 16 * 16
    d2_top, d2_bottom = (0, d2_first), (d2_first, d2[0].shape[0] - d2_first)
    (h1t, z1, x1), ((g_in, g_out), (g_d2,)) = _ffn_down_fwd(
        gu1, x0, wd1, row(w["ln1_g"]), row(w["ln1_b"]), "ffn1_down_fwd",
        exch=[_gather_first([sh["mix_w_in"], sh["mix_w_out"]], [True, False]),
              _gather_first(d2, [False], rows=d2_top)])
    in_cols = sh["mix_w_in"].shape[1]
    x1t, ((w_in, w_out), (g_d2,)) = _transpose_bf16(
        x1, "x1_transpose", exch=[_gather_forward([g_in, g_out], [True, False], [in_cols, None]),
                                  _gather_forward([g_d2], [False], [None], rows=d2_top)])
    w_out = w_out.reshape(-1, d)
    top, bottom = (0, d // 2), (d // 2, d // 2)
    gu2 = [sh["ffn2_w_gate_up"]]
    proj, ((g_gu2,),) = _mix_in_proj(x1, w_in, "mix_in_fwd", exch=[_gather_first(gu2, [True], rows=top)])
    (y, yt, conv_c), ((g_gu2,),) = _mixer_fwd(
        proj, conv_w_full, row(w["conv_b"]), row(w["conv_ln_g"]), row(w["conv_ln_b"]),
        w["sg_ln_g"], w["sg_ln_b"], sg_wm_b, sg_bb, "mixer_fwd",
        exch=[_both(_gather_first(gu2, [True], rows=bottom, into=[g_gu2]),
                    _gather_forward([g_gu2], [True], [f2s], rows=top))])
    (z2, x2, x2t), ((wgu2,), (g_d2,)) = _mix_out_fwd(
        y, w_out, x1, row(w["ln2_g"]), row(w["ln2_b"]), "mix_out_fwd",
        exch=[_gather_forward([g_gu2], [True], [f2s], rows=bottom),
              _gather_first(d2, [False], rows=d2_bottom, into=[g_d2])])
    (wd2,) = _exchange_alone(_gather_forward([g_d2], [False], [None], rows=d2_bottom), "ffn2_down_gather_forward")
    wd2 = wd2.reshape(-1, d)
    grads = {}
    (g2, u2, h2t, dz3, do2, grads["ln3_g"], grads["ln3_b"], loss_tile), _ = _ffn_fwd_loss(
        x2, wgu2, wd2, row(w["ln3_g"]), row(w["ln3_b"]), target, "ffn2_fwd_loss")

    f = wd1.shape[0]
    dn = _tile(d, 1024, 128)
    core_chip = jnp.concatenate([my_core, my_chip])
    pair = lambda p, s, label: _pair_sum(p, s, core_chip, "pair_sum_" + label)
    adamw = lambda n, own, got, steps: _adamw_side(w[n], mom[n], var[n], own, got, steps)
    m_tiles = d // _tile(d, 512, 16)
    gu_first = d * 3 // 4 // 16 * 16
    out = {}
    p_d2, _ = _weight_grad(h2t, do2, dn, 512, "ffn2_dw_down")
    p_d2 = p_d2.reshape(N_DEV, f // N_DEV, d)
    (dg2, du2, dx2), ((s_d2,),) = _ffn_bwd(dz3, do2, g2, u2, wgu2, wd2, "ffn2_bwd", exch=[_rs_sibling([p_d2])])
    q_d2, own_d2 = pair(p_d2, s_d2, "ffn2_down")
    d_rows = q_d2.shape[1]
    d_half = d_rows // 2 // 16 * 16
    p_gu2, ((r_d2,),) = _weight_grad(x2t, dg2, f2s, 512, "ffn2_dw_gate", blocks=N_DEV,
                                     exch=[_rs_chips([q_d2], rows=(0, d_half))])
    p_gu2, ((r_d2,),) = _weight_grad(x2t, du2, f2s, 512, "ffn2_dw_up", blocks=N_DEV, block_offset=4, into=p_gu2,
                                     exch=[_rs_chips([q_d2], rows=(d_half, d_rows - d_half), into=[r_d2])])
    (dz2, dz2b, grads["ln2_g"], grads["ln2_b"]), ((s_gu2,),) = _ln_bwd_call(
        z2, dx2, row(w["ln2_g"]), 1.0, "ln2_bwd", exch=[_rs_sibling([p_gu2])])
    q_gu2, own_gu2 = pair(p_gu2, s_gu2, "ffn2_gate_up")
    dy, _ = _mix_out_bwd(dz2b, w_out, "mix_out_bwd")
    p_out, _ = _weight_grad(yt, dz2b, dn, 512, "mix_out_dw")
    p_out = p_out.reshape(N_DEV, -1, d)
    (dproj, grads["conv_w"], grads["conv_b"], grads["conv_ln_g"], grads["conv_ln_b"], grads["sg_ln_g"],
     grads["sg_ln_b"], grads["sg_w"], grads["sg_b"]), ((r_gu2,),) = _mixer_bwd(
        proj, conv_c, dy, conv_w_full, row(w["conv_ln_g"]), row(w["conv_ln_b"]), w["sg_ln_g"], w["sg_ln_b"],
        sg_wm_b, sg_wmt_b, sg_bb, "mixer_bwd", exch=[_rs_chips([q_gu2], rows=(0, gu_first))])
    dx1, ((s_out,), (r_gu2,)) = _mix_in_bwd(
        dproj, w_in, dz2, "mix_in_bwd",
        exch=[_rs_sibling([p_out]), _rs_chips([q_gu2], rows=(gu_first, d - gu_first), into=[r_gu2])])
    p_in, (out["ffn2_w_gate_up"], out["ffn2_w_down"]) = _weight_grad(
        x1t, dproj, in_cols, 512, "mix_in_dw", blocks=N_DEV,
        exch=[adamw("ffn2_w_gate_up", own_gu2, r_gu2, N_DEV * m_tiles), adamw("ffn2_w_down", own_d2, r_d2, N_DEV * m_tiles)])
    (dz1, do1, grads["ln1_g"], grads["ln1_b"]), ((s_in,),) = _ln_bwd_call(
        z1, dx1, row(w["ln1_g"]), 0.5, "ln1_bwd", exch=[_rs_sibling([p_in])])
    q_out, own_out = pair(p_out, s_out, "mix_out")
    q_in, own_in = pair(p_in, s_in, "mix_in")
    small_parts = [_rows128(grads[n]) for n in SMALL]
    packed = jnp.concatenate(small_parts + [_rows128(grads["conv_w"]), loss_tile], axis=0)
    p_d1, ((r_in,),) = _weight_grad(h1t, do1, dn, 512, "ffn1_dw_down", exch=[_rs_chips([q_in])])
    p_d1 = p_d1.reshape(N_DEV, f // N_DEV, d)
    (dg1, du1), ((s_d1,), (r_out,), (small_all,)) = _ffn_bwd_act(
        do1, gu1, wd1, "ffn1_bwd_act",
        exch=[_rs_sibling([p_d1]), _rs_chips([q_out]), _small_gather(packed)])
    q_d1, own_d1 = pair(p_d1, s_d1, "ffn1_down")
    p_gu1, ((r_d1,),) = _weight_grad(x0t, dg1, f2s, 512, "ffn1_dw_gate", blocks=N_DEV, exch=[_rs_chips([q_d1])])
    p_gu1, (out["mix_w_in"], out["mix_w_out"]) = _weight_grad(
        x0t, du1, f2s, 512, "ffn1_dw_up", blocks=N_DEV, block_offset=4, into=p_gu1,
        exch=[adamw("mix_w_in", own_in, r_in, 4 * m_tiles), adamw("mix_w_out", own_out, r_out, 4 * m_tiles)])
    (s_gu1,) = _exchange_alone(_rs_sibling([p_gu1]), "ffn1_gate_up_sibling_exchange")
    q_gu1, own_gu1 = pair(p_gu1, s_gu1, "ffn1_gate_up")
    (grad_x,), ((r_gu1,),) = _ffn_bwd_dx(dz1, dg1, du1, wgu1, "ffn1_bwd_dx", exch=[_rs_chips([q_gu1])])
    for n, own, got in (("ffn1_w_down", own_d1, r_d1), ("ffn1_w_gate_up", own_gu1, r_gu1)):
        out[n] = _adamw_sharded(w[n], mom[n], var[n], own, got, "adamw_" + n)

    cw_rows = CONV_TAPS * CONV_CH // 128
    total = _sum_over_devices(small_all)
    offs = [0]
    for p in small_parts:
        offs.append(offs[-1] + p.shape[0])
    n_small = offs[-1]
    loss = total[n_small + cw_rows, 0]
    g_conv_w = lax.dynamic_slice_in_dim(total[n_small:n_small + cw_rows].reshape(CONV_TAPS, CONV_CH),
                                        me * (CONV_CH // N_DEV), CONV_CH // N_DEV, axis=1)
    pad8 = lambda a: jnp.pad(a, ((0, -a.shape[0] % 8), (0, 0)))
    pack = lambda tree, cw: jnp.concatenate([_rows128(tree[n]) for n in SMALL] + [pad8(cw)], axis=0)
    g_pack = jnp.concatenate([total[:n_small], pad8(g_conv_w)], axis=0)
    d_pack, m_pack, v_pack = _adamw_small(pack(w, w["conv_w"]), g_pack, pack(mom, mom["conv_w"]),
                                          pack(var, var["conv_w"]), "adamw_small")
    for k, n in enumerate(SMALL):
        sl = slice(offs[k], offs[k + 1])
        shp = w[n].shape
        out[n] = (total[sl].reshape(shp), d_pack[sl].reshape(shp), m_pack[sl].reshape(shp), v_pack[sl].reshape(shp))
    sl = slice(n_small, n_small + CONV_TAPS)
    out["conv_w"] = (g_conv_w, d_pack[sl], m_pack[sl], v_pack[sl])

    lead = lambda a: a[None]
    res = [loss, grad_x[None]]
    for kind in range(4):
        res += [lead(out[n][kind]) for n in ORDER]
    return tuple(res)
```

```python
import functools
import math

import jax
import jax.numpy as jnp
from jax import lax
from jax.experimental import pallas as pl
from jax.experimental.pallas import tpu as pltpu

F32, BF16 = jnp.float32, jnp.bfloat16
MESH = pl.DeviceIdType.MESH
ANY = pl.BlockSpec(memory_space=pl.ANY)

N_DEV = 8
LN_EPS = 1e-5
ALPHA = 2.0 ** 0.25
CONV_CH = 1024
CONV_TAPS = 31
HALO = 32
HEADS = 8
HEAD_DIM = 128
CHUNK = 128
ADAM_LR, ADAM_B1, ADAM_B2, ADAM_EPS, ADAM_WD, ADAM_STEP = 0.001, 0.9, 0.999, 1e-08, 0.01, 10
V7X_VMEM_LIMIT = 62 * 2 ** 20
EPILOGUE_ROWS = 128

def _cparams(*sem):
    return pltpu.CompilerParams(dimension_semantics=sem, vmem_limit_bytes=V7X_VMEM_LIMIT)


def _tile(n, pref, mult):
    best = None
    for t in range(mult, min(n, pref) + 1, mult):
        if n % t == 0:
            best = t
    return best if best is not None else n


def _dot(a, b):
    return jnp.dot(a, b, preferred_element_type=F32)


def _dot_nt(a, b):
    return lax.dot_general(a, b, (((1,), (1,)), ((), ())), preferred_element_type=F32)


def _sigmoid(x):
    return 1.0 / (1.0 + jnp.exp(-x))


def _ln_stats(z):
    mu = jnp.mean(z, axis=-1, keepdims=True)
    zc = z - mu
    var = jnp.mean(zc * zc, axis=-1, keepdims=True)
    rstd = lax.rsqrt(var + LN_EPS)
    return zc * rstd, rstd


def _ln(z, g, b):
    xh, _ = _ln_stats(z)
    return xh * g + b


def _ln_bwd(dxh, xh, rstd):
    m1 = jnp.mean(dxh, axis=-1, keepdims=True)
    m2 = jnp.mean(dxh * xh, axis=-1, keepdims=True)
    return rstd * (dxh - m1 - xh * m2)


_GK = math.sqrt(2.0 / math.pi)
_GA = 0.044715


def _gelu_and_grad(x):
    x2 = x * x
    t = jnp.tanh(_GK * (x + _GA * x * x2))
    y = 0.5 * x * (1.0 + t)
    dy = 0.5 * (1.0 + t) + 0.5 * x * (1.0 - t * t) * (_GK * (1.0 + 3.0 * _GA * x2))
    return y, dy


def _silu_grad(a):
    s = _sigmoid(a)
    return s * (1.0 + a * (1.0 - s))


def _place():
    return lax.axis_index("x"), lax.axis_index("y"), lax.axis_index("c")


def _other_chips(x, y):
    return [(1 - x, y), (x, 1 - y), (1 - x, 1 - y)]


def _visit_order(x, y, c):
    chips = _other_chips(x, y)
    return [(x, y, c), (x, y, 1 - c), (*chips[0], c), (*chips[1], c), (*chips[0], 1 - c), (*chips[1], 1 - c),
            (*chips[2], c), (*chips[2], 1 - c)]


def _gather_and_gate_up(xb, shards, relayed, order, name):
    n = len(shards)
    N_COPIES = 10
    t, d = xb.shape
    cols = shards[0].shape[1]
    tm = _tile(t, 1024, 128)
    ni = t // tm
    col_major = [True] + [False] * (n - 1)

    def body(order_ref, x_ref, *refs):
        srcs, gu_ref, xt_ref, dsts = refs[:n], refs[n], refs[n + 1], refs[n + 2:2 * n + 2]
        wbuf, send_sems, recv_sems, local_sems, load_sem = refs[2 * n + 2:]
        b, i = pl.program_id(0), pl.program_id(1)
        x, y, c = _place()
        me, sib = (x, y, c), (x, y, 1 - c)
        chips = _other_chips(x, y)

        near_x, near_y, far = chips

        def slot(w, p, band=None):
            half = shards[w].shape[0] // 2
            rows = None if band is None else (band * half, half)
            return _block_slot(dsts[w], col_major[w], shards[w].shape[1], p, rows)

        def copy(w, s, block, to, band=None, from_src=False):
            return pltpu.make_async_remote_copy(
                src_ref=srcs[w] if from_src else slot(w, block, band), dst_ref=slot(w, block, band),
                send_sem=send_sems.at[N_COPIES * w + s], recv_sem=recv_sems.at[N_COPIES * w + s],
                device_id=to, device_id_type=MESH)

        def own(w):
            return pltpu.make_async_copy(srcs[w], slot(w, me), local_sems.at[w])

        def sends(w):
            out = [copy(w, 0, me, sib, from_src=True), copy(w, 1, me, (*near_x, c), from_src=True),
                   copy(w, 2, me, (*near_y, c), from_src=True)]
            if not relayed[w]:
                out.append(copy(w, 3, me, (*far, c), from_src=True))
            return out

        def passed_on(w):
            out = [copy(w, 4, (*near_x, c), sib), copy(w, 5, (*near_y, c), sib)]
            if relayed[w]:
                out += [copy(w, 6, (*far, c), sib, band=0), copy(w, 9, (*far, c), sib, band=1),
                        copy(w, 7, (*near_x, c), (*near_y, c), band=0), copy(w, 8, (*near_y, c), (*near_x, c), band=1)]
            else:
                out.append(copy(w, 6, (*far, c), sib))
            return out

        def start_sends(w):
            own(w).start()
            for cp in sends(w):
                cp.start()

        def got_near_x(w):
            copy(w, 1, (*near_x, c), me).wait_recv()
            copy(w, 4, (*near_x, c), sib).start()
            if relayed[w]:
                copy(w, 7, (*near_x, c), (*near_y, c), band=0).start()

        def got_near_y(w):
            copy(w, 2, (*near_y, c), me).wait_recv()
            copy(w, 5, (*near_y, c), sib).start()
            if relayed[w]:
                copy(w, 8, (*near_y, c), (*near_x, c), band=1).start()

        def got_far(w):
            if relayed[w]:
                copy(w, 7, (*far, c), me, band=0).wait_recv()
                copy(w, 6, (*far, c), sib, band=0).start()
                copy(w, 8, (*far, c), me, band=1).wait_recv()
                copy(w, 9, (*far, c), sib, band=1).start()
            else:
                copy(w, 3, (*far, c), me).wait_recv()
                copy(w, 6, (*far, c), sib).start()

        def got_from_sibling(w, which):
            if which == 0:
                copy(w, 0, sib, me).wait_recv()
            elif which == 3 and relayed[w]:
                copy(w, 6, (*far, 1 - c), me, band=0).wait_recv()
                copy(w, 9, (*far, 1 - c), me, band=1).wait_recv()
            else:
                copy(w, 3 + which, (*chips[which - 1], 1 - c), me).wait_recv()

        others = range(1, n)

        def arrive(k):
            if k == 0:
                own(0).wait()
            elif k == 1:
                got_from_sibling(0, 0)
            elif k == 2:
                got_near_x(0)
                for w in others:
                    start_sends(w)
            elif k == 3:
                got_near_y(0)
            elif k in (4, 5):
                got_from_sibling(0, k - 3)
            elif k == 6:
                got_far(0)
                for w in others:
                    got_near_x(w)
                    got_near_y(w)
            else:
                got_from_sibling(0, 3)
                for w in others:
                    got_far(w)

        def load(k):
            at = pl.multiple_of(order_ref[k] * cols, 128)
            return pltpu.make_async_copy(dsts[0].at[:, pl.ds(at, cols)], wbuf.at[k % 2], load_sem.at[k % 2])

        @pl.when((b == 0) & (i == 0))
        def _():
            start_sends(0)
            arrive(0)
            load(0).start()
            load(0).wait()

        early = ni - 1
        for k in range(1, N_DEV):
            @pl.when((b == k - 1) & (i == early))
            def _(k=k):
                arrive(k)
                load(k).start()

            @pl.when((b == k) & (i == 0))
            def _(k=k):
                load(k).wait()

        gu_ref[...] = _dot(x_ref[...].astype(BF16), wbuf[b % 2]).astype(BF16)

        @pl.when(b == 0)
        def _():
            xt_ref[...] = x_ref[...].T.astype(BF16)

        @pl.when((b == N_DEV - 1) & (i == ni - 1))
        def _():
            for w in others:
                for which in range(4):
                    got_from_sibling(w, which)
                own(w).wait()
            for w in range(n):
                for cp in sends(w) + passed_on(w):
                    cp.wait_send()

    grid_spec = pltpu.PrefetchScalarGridSpec(
        num_scalar_prefetch=1, grid=(N_DEV, ni),
        in_specs=[pl.BlockSpec((tm, d), lambda b, i, o: (i, 0))] + [ANY] * n,
        out_specs=[pl.BlockSpec((tm, cols), lambda b, i, o: (i, o[b])),
                   pl.BlockSpec((d, tm), lambda b, i, o: (0, jnp.where(b == 0, i, ni - 1)))] + [ANY] * n,
        scratch_shapes=[pltpu.VMEM((2, d, cols), BF16), pltpu.SemaphoreType.DMA((N_COPIES * n,)),
                        pltpu.SemaphoreType.DMA((N_COPIES * n,)), pltpu.SemaphoreType.DMA((n,)),
                        pltpu.SemaphoreType.DMA((2,))])
    res = pl.pallas_call(
        body, name=name, grid_spec=grid_spec,
        out_shape=[jax.ShapeDtypeStruct((t, N_DEV * cols), BF16), jax.ShapeDtypeStruct((d, t), BF16)]
        + [_gathered_shape(s, cm) for s, cm in zip(shards, col_major)],
        compiler_params=_cparams("arbitrary", "arbitrary"),
    )(order, xb, *shards)
    return res[0], res[1], res[2:]


class _Exchange:
    def __init__(self, ins, io, new, n_sems, n_local, make):
        self.ins, self.io, self.new = list(ins), list(io), list(new)
        self.n_sems, self.n_local, self.make = n_sems, n_local, make


def _block_slot(ref, col_major, cols, place, rows=None):
    k = 4 * place[0] + 2 * place[1] + place[2]
    band = slice(None) if rows is None else pl.ds(rows[0], rows[1])
    if col_major:
        return ref.at[band, pl.ds(pl.multiple_of(k * cols, 128), cols)]
    return ref.at[k] if rows is None else ref.at[k, band]


def _gathered_shape(s, col_major):
    return jax.ShapeDtypeStruct((s.shape[0], N_DEV * s.shape[1]) if col_major else (N_DEV,) + s.shape, s.dtype)


def _gather_first(shards, col_major, rows=None, into=None):
    n = len(shards)
    new = [] if into is not None else [_gathered_shape(s, cm) for s, cm in zip(shards, col_major)]

    def make(in_refs, io_refs, new_refs, send_sems, recv_sems, local_sems, base=0, local_base=0):
        x, y, c = _place()
        targets = [(x, y, 1 - c)] + [(*chip, c) for chip in _other_chips(x, y)]
        gathered = io_refs if into is not None else new_refs
        copies = []
        for w in range(n):
            src = in_refs[w] if rows is None else in_refs[w].at[pl.ds(rows[0], rows[1])]
            slot = _block_slot(gathered[w], col_major[w], shards[w].shape[1], (x, y, c), rows)
            copies.append(pltpu.make_async_copy(src, slot, local_sems.at[local_base + w]))
            for s, to in enumerate(targets):
                copies.append(pltpu.make_async_remote_copy(
                    src_ref=src, dst_ref=slot, send_sem=send_sems.at[base + 4 * w + s],
                    recv_sem=recv_sems.at[base + 4 * w + s], device_id=to, device_id_type=MESH))
        return copies

    return _Exchange(shards, into or [], new, 4 * n, n, make)


def _gather_forward(gathered, col_major, cols, rows=None):
    n = len(gathered)

    def make(in_refs, io_refs, new_refs, send_sems, recv_sems, local_sems, base=0, local_base=0):
        x, y, c = _place()
        copies = []
        for w in range(n):
            for j, chip in enumerate(_other_chips(x, y)):
                slot = _block_slot(io_refs[w], col_major[w], cols[w], (*chip, c), rows)
                copies.append(pltpu.make_async_remote_copy(
                    src_ref=slot, dst_ref=slot, send_sem=send_sems.at[base + 3 * w + j],
                    recv_sem=recv_sems.at[base + 3 * w + j], device_id=(x, y, 1 - c), device_id_type=MESH))
        return copies

    return _Exchange([], gathered, [], 3 * n, 0, make)


def _both(a, b):
    def make(in_refs, io_refs, new_refs, send_sems, recv_sems, local_sems):
        na = len(a.ins)
        return (a.make(in_refs[:na], io_refs, [], send_sems, recv_sems, local_sems, 0, 0)
                + b.make(in_refs[na:], io_refs, [], send_sems, recv_sems, local_sems, a.n_sems, a.n_local))

    return _Exchange(a.ins + b.ins, a.io, [], a.n_sems + b.n_sems, a.n_local + b.n_local, make)


def _rs_sibling(parts):
    n = len(parts)

    def make(in_refs, io_refs, new_refs, send_sems, recv_sems, local_sems):
        x, y, c = _place()
        copies = []
        for w in range(n):
            for j in range(4):
                copies.append(pltpu.make_async_remote_copy(
                    src_ref=in_refs[w].at[2 * j + (1 - c)], dst_ref=new_refs[w].at[j],
                    send_sem=send_sems.at[4 * w + j], recv_sem=recv_sems.at[4 * w + j],
                    device_id=(x, y, 1 - c), device_id_type=MESH))
        return copies

    return _Exchange(parts, [], [jax.ShapeDtypeStruct((4,) + p.shape[1:], p.dtype) for p in parts], 4 * n, 0, make)


def _rs_chips(chip_parts, rows=None, into=None):
    n = len(chip_parts)
    band = slice(None) if rows is None else pl.ds(rows[0], rows[1])
    new = [] if into is not None else [jax.ShapeDtypeStruct((3,) + p.shape[1:], p.dtype) for p in chip_parts]

    def make(in_refs, io_refs, new_refs, send_sems, recv_sems, local_sems):
        x, y, c = _place()
        landing = io_refs if into is not None else new_refs
        copies = []
        for w in range(n):
            for rel, (px, py) in enumerate(_other_chips(x, y)):
                copies.append(pltpu.make_async_remote_copy(
                    src_ref=in_refs[w].at[2 * px + py, band], dst_ref=landing[w].at[rel, band],
                    send_sem=send_sems.at[3 * w + rel], recv_sem=recv_sems.at[3 * w + rel],
                    device_id=(px, py, c), device_id_type=MESH))
        return copies

    return _Exchange(chip_parts, into or [], new, 3 * n, 0, make)


class _Side:
    def __init__(self, ins, in_blocks, out_shapes, out_blocks, n_tiles, fn):
        self.ins, self.in_blocks, self.out_shapes, self.out_blocks = list(ins), in_blocks, list(out_shapes), out_blocks
        self.n_tiles, self.fn = n_tiles, fn


def _call(body, exch, *, name, grid, in_specs, out_specs, out_shape, scratch_shapes=(), semantics,
          input_output_aliases=None):
    exch = list(exch)
    in_specs, out_specs, out_shape = list(in_specs), list(out_specs), list(out_shape)
    scratch_shapes = list(scratch_shapes)
    if not exch:
        fn = pl.pallas_call(body, name=name, grid=grid, in_specs=in_specs, out_specs=out_specs, out_shape=out_shape,
                            scratch_shapes=scratch_shapes, input_output_aliases=input_output_aliases or {},
                            compiler_params=_cparams(*semantics))
        return lambda *args: (fn(*args), [])
    n_in, n_out, n_scr = len(in_specs), len(out_specs), len(scratch_shapes)
    aliases = dict(input_output_aliases or {})
    all_in, all_out_specs, all_out_shape, all_scr = list(in_specs), list(out_specs), list(out_shape), list(scratch_shapes)
    extra_args = []

    def step(idx):
        s = idx[0]
        for a in range(1, len(grid)):
            s = s * grid[a] + idx[a]
        return s

    def tile_spec(shape, where, n_tiles):
        return pl.BlockSpec(shape, lambda *idx: where(jnp.minimum(step(idx), n_tiles - 1)))

    for ex in exch:
        if isinstance(ex, _Side):
            all_in += [tile_spec(shape, where, ex.n_tiles) for shape, where in ex.in_blocks]
            extra_args += ex.ins
            all_out_specs += [tile_spec(shape, where, ex.n_tiles) for shape, where in ex.out_blocks]
            all_out_shape += ex.out_shapes
            continue
        for k, a in enumerate(ex.io):
            aliases[len(all_in) + len(ex.ins) + k] = len(all_out_specs) + k
        all_in += [ANY] * (len(ex.ins) + len(ex.io))
        extra_args += ex.ins + ex.io
        all_out_specs += [ANY] * (len(ex.io) + len(ex.new))
        all_out_shape += [jax.ShapeDtypeStruct(a.shape, a.dtype) for a in ex.io] + ex.new
        all_scr += [pltpu.SemaphoreType.DMA((ex.n_sems,)), pltpu.SemaphoreType.DMA((ex.n_sems,)),
                    pltpu.SemaphoreType.DMA((max(ex.n_local, 1),))]

    n_ins = [len(ex.ins) if isinstance(ex, _Side) else len(ex.ins) + len(ex.io) for ex in exch]
    n_outs = [len(ex.out_shapes) if isinstance(ex, _Side) else len(ex.io) + len(ex.new) for ex in exch]

    def wrapped(*refs):
        pos = n_in
        ex_in = []
        for k in n_ins:
            ex_in.append(refs[pos:pos + k])
            pos += k
        outs = refs[pos:pos + n_out]
        pos += n_out
        ex_out = []
        for k in n_outs:
            ex_out.append(refs[pos:pos + k])
            pos += k
        scr = refs[pos:pos + n_scr]
        pos += n_scr
        idx = [pl.program_id(a) for a in range(len(grid))]
        first = functools.reduce(jnp.logical_and, [i == 0 for i in idx])
        last = functools.reduce(jnp.logical_and, [i == g - 1 for i, g in zip(idx, grid)])

        def copies():
            out, at = [], pos
            for ex, ei, eo in zip(exch, ex_in, ex_out):
                if not isinstance(ex, _Side):
                    out += ex.make(ei[:len(ex.ins)], eo[:len(ex.io)], eo[len(ex.io):], *refs[at:at + 3])
                    at += 3
            return out

        @pl.when(first)
        def _():
            for cp in copies():
                cp.start()

        body(*refs[:n_in], *outs, *scr)
        for ex, ei, eo in zip(exch, ex_in, ex_out):
            if isinstance(ex, _Side):
                pl.when(step(idx) < ex.n_tiles)(functools.partial(ex.fn, ei, eo))

        @pl.when(last)
        def _():
            for cp in copies():
                cp.wait()

    fn = pl.pallas_call(wrapped, name=name, grid=grid, in_specs=all_in, out_specs=all_out_specs,
                        out_shape=all_out_shape, scratch_shapes=all_scr, input_output_aliases=aliases,
                        compiler_params=_cparams(*(["arbitrary"] * len(grid))))

    def run(*args):
        res = fn(*args, *extra_args)
        outs, pos, ex_res = res[:n_out], n_out, []
        for k in n_outs:
            ex_res.append(list(res[pos:pos + k]))
            pos += k
        return outs, ex_res

    return run


def _exchange_alone(ex, name):
    def body():
        pass

    _, res = _call(body, [ex], name=name, grid=(1,), in_specs=[], out_specs=[], out_shape=[], semantics=("arbitrary",))()
    return res[0]


def _small_gather(part):
    def make(in_refs, io_refs, new_refs, send_sems, recv_sems, local_sems):
        x, y, c = _place()
        slot = new_refs[0].at[4 * x + 2 * y + c]
        copies = [pltpu.make_async_copy(in_refs[0], slot, local_sems.at[0])]
        for d in range(1, N_DEV):
            peer = (1 - x if d & 4 else x, 1 - y if d & 2 else y, 1 - c if d & 1 else c)
            copies.append(pltpu.make_async_remote_copy(
                src_ref=in_refs[0], dst_ref=slot, send_sem=send_sems.at[d - 1], recv_sem=recv_sems.at[d - 1],
                device_id=peer, device_id_type=MESH))
        return copies

    return _Exchange([part], [], [jax.ShapeDtypeStruct((N_DEV,) + part.shape, part.dtype)], N_DEV - 1, 1, make)


def _sum_over_devices(parts):
    _, rows, lanes = parts.shape

    def body(p_ref, o_ref):
        acc = p_ref[0]
        for k in range(1, N_DEV):
            acc = acc + p_ref[k]
        o_ref[...] = acc

    return pl.pallas_call(
        body, name="small_grads_sum", grid=(1,), out_shape=jax.ShapeDtypeStruct((rows, lanes), F32),
        in_specs=[pl.BlockSpec((N_DEV, rows, lanes), lambda i: (0, 0, 0))],
        out_specs=pl.BlockSpec((rows, lanes), lambda i: (0, 0)),
        compiler_params=_cparams("arbitrary"),
    )(parts)


def _transpose_bf16(a, name, exch=(), with_copy=False):
    r, c = a.shape
    tr, tc = _tile(r, 512, 128), _tile(c, 512, 128)

    def body(a_ref, o_ref, *copy_ref):
        v = a_ref[...].astype(F32)
        o_ref[...] = v.T.astype(BF16)
        if with_copy:
            copy_ref[0][...] = v.astype(BF16)

    outs, ex = _call(
        body, exch, name=name, grid=(r // tr, c // tc),
        out_shape=[jax.ShapeDtypeStruct((c, r), BF16)] + [jax.ShapeDtypeStruct((r, c), BF16)] * with_copy,
        in_specs=[pl.BlockSpec((tr, tc), lambda i, j: (i, j))],
        out_specs=[pl.BlockSpec((tc, tr), lambda i, j: (j, i))] + [pl.BlockSpec((tr, tc), lambda i, j: (i, j))] * with_copy,
        semantics=("parallel", "parallel"),
    )(a)
    return (outs if with_copy else outs[0]), ex


def _ffn_fwd_loss(x, wgu, wd, ln_g, ln_b, target, name, exch=()):
    t, d = x.shape
    f = wd.shape[0]
    tm, tf = _tile(t, 512, 128), _tile(f, 512, 128)
    nf = f // tf

    def body(x_ref, wg_ref, wu_ref, wd_ref, lg_ref, lb_ref, t_ref,
             go_ref, uo_ref, ht_ref, dz_ref, dzb_ref, dlg_ref, dlb_ref, loss_ref, xb, acc):
        i, j = pl.program_id(0), pl.program_id(1)

        @pl.when(j == 0)
        def _():
            xb[...] = x_ref[...].astype(BF16)
            acc[...] = jnp.zeros_like(acc)

        @pl.when((i == 0) & (j == 0))
        def _():
            dlg_ref[...] = jnp.zeros_like(dlg_ref)
            dlb_ref[...] = jnp.zeros_like(dlb_ref)
            loss_ref[...] = jnp.zeros_like(loss_ref)

        g = _dot(xb[...], wg_ref[...])
        u = _dot(xb[...], wu_ref[...])
        h = g * _sigmoid(g) * u
        go_ref[...] = g.astype(BF16)
        uo_ref[...] = u.astype(BF16)
        ht_ref[...] = h.T.astype(BF16)
        acc[...] += _dot(h.astype(BF16), wd_ref[...])

        @pl.when(j == nf - 1)
        def _():
            for r in range(0, tm, EPILOGUE_ROWS):
                rows = slice(r, r + EPILOGUE_ROWS)
                xh, rstd = _ln_stats(ALPHA * x_ref[rows, :] + 0.5 * acc[rows, :])
                e = xh * lg_ref[...] + lb_ref[...] - t_ref[rows, :]
                loss_ref[...] += 0.5 * jnp.sum(jnp.sum(e * e, axis=-1, keepdims=True) * (1.0 / d), axis=0,
                                               keepdims=True)
                dy = e * (1.0 / d)
                dz = _ln_bwd(dy * lg_ref[...], xh, rstd)
                dz_ref[rows, :] = dz
                dzb_ref[rows, :] = (0.5 * dz).astype(BF16)
                dlg_ref[...] += jnp.sum(dy * xh, axis=0, keepdims=True)
                dlb_ref[...] += jnp.sum(dy, axis=0, keepdims=True)

    row = lambda i, j: (i, 0)
    fixed = lambda i, j: (0, 0)
    return _call(
        body, exch, name=name, grid=(t // tm, nf),
        out_shape=[jax.ShapeDtypeStruct((t, f), BF16), jax.ShapeDtypeStruct((t, f), BF16),
                   jax.ShapeDtypeStruct((f, t), BF16), jax.ShapeDtypeStruct((t, d), F32),
                   jax.ShapeDtypeStruct((t, d), BF16), jax.ShapeDtypeStruct((1, d), F32),
                   jax.ShapeDtypeStruct((1, d), F32), jax.ShapeDtypeStruct((8, 128), F32)],
        in_specs=[pl.BlockSpec((tm, d), row),
                  pl.BlockSpec((d, tf), lambda i, j: (0, j)),
                  pl.BlockSpec((d, tf), lambda i, j: (0, j + nf)),
                  pl.BlockSpec((tf, d), lambda i, j: (j, 0)),
                  pl.BlockSpec((1, d), fixed), pl.BlockSpec((1, d), fixed), pl.BlockSpec((tm, d), row)],
        out_specs=[pl.BlockSpec((tm, tf), lambda i, j: (i, j)), pl.BlockSpec((tm, tf), lambda i, j: (i, j)),
                   pl.BlockSpec((tf, tm), lambda i, j: (j, i)), pl.BlockSpec((tm, d), row), pl.BlockSpec((tm, d), row),
                   pl.BlockSpec((1, d), fixed), pl.BlockSpec((1, d), fixed), pl.BlockSpec((8, 128), fixed)],
        scratch_shapes=[pltpu.VMEM((tm, d), BF16), pltpu.VMEM((tm, d), F32)],
        semantics=("arbitrary", "arbitrary"),
    )(x, wgu, wgu, wd, ln_g, ln_b, target)


def _ffn_down_fwd(gu, x, wd, ln_g, ln_b, name, exch=()):
    t, d = x.shape
    f = wd.shape[0]
    tm, tf = _tile(t, 512, 128), _tile(f, 512, 128)
    nf = f // tf

    def body(g_ref, u_ref, wd_ref, x_ref, lg_ref, lb_ref, ht_ref, z_ref, xn_ref, acc):
        j = pl.program_id(1)

        @pl.when(j == 0)
        def _():
            acc[...] = jnp.zeros_like(acc)

        g = g_ref[...].astype(F32)
        h = g * _sigmoid(g) * u_ref[...].astype(F32)
        ht_ref[...] = h.T.astype(BF16)
        acc[...] += _dot(h.astype(BF16), wd_ref[...])

        @pl.when(j == nf - 1)
        def _():
            z = ALPHA * x_ref[...] + 0.5 * acc[...]
            z_ref[...] = z
            xn_ref[...] = _ln(z, lg_ref[...], lb_ref[...])

    row = lambda i, j: (i, 0)
    fixed = lambda i, j: (0, 0)
    return _call(
        body, exch, name=name, grid=(t // tm, nf),
        out_shape=[jax.ShapeDtypeStruct((f, t), BF16), jax.ShapeDtypeStruct((t, d), F32),
                   jax.ShapeDtypeStruct((t, d), F32)],
        in_specs=[pl.BlockSpec((tm, tf), lambda i, j: (i, j)), pl.BlockSpec((tm, tf), lambda i, j: (i, j + nf)),
                  pl.BlockSpec((tf, d), lambda i, j: (j, 0)), pl.BlockSpec((tm, d), row),
                  pl.BlockSpec((1, d), fixed), pl.BlockSpec((1, d), fixed)],
        out_specs=[pl.BlockSpec((tf, tm), lambda i, j: (j, i)), pl.BlockSpec((tm, d), row), pl.BlockSpec((tm, d), row)],
        scratch_shapes=[pltpu.VMEM((tm, d), F32)],
        semantics=("parallel", "arbitrary"),
    )(gu, gu, wd, x, ln_g, ln_b)


def _ffn_act_grads(dh, g_ref, u_ref):
    gg = g_ref[...].astype(F32)
    uu = u_ref[...].astype(F32)
    s = _sigmoid(gg)
    du = (dh * (gg * s)).astype(BF16)
    dg = (dh * uu * (s * (1.0 + gg * (1.0 - s)))).astype(BF16)
    return dg, du


def _ffn_bwd(dz, do, g, u, wgu, wd, name, exch=()):
    t, d = dz.shape
    f = wd.shape[0]
    tm, tf = _tile(t, 512, 128), _tile(f, 512, 128)
    nf = f // tf

    def body(dz_ref, do_ref, g_ref, u_ref, wg_ref, wu_ref, wd_ref, dg_ref, du_ref, dx_ref, acc):
        j = pl.program_id(1)

        @pl.when(j == 0)
        def _():
            acc[...] = jnp.zeros_like(acc)

        dg, du = _ffn_act_grads(_dot_nt(do_ref[...], wd_ref[...]), g_ref, u_ref)
        dg_ref[...] = dg
        du_ref[...] = du
        acc[...] += _dot_nt(dg, wg_ref[...]) + _dot_nt(du, wu_ref[...])

        @pl.when(j == nf - 1)
        def _():
            dx_ref[...] = ALPHA * dz_ref[...] + acc[...]

    row = lambda i, j: (i, 0)
    tile = lambda i, j: (i, j)
    return _call(
        body, exch, name=name, grid=(t // tm, nf),
        out_shape=[jax.ShapeDtypeStruct((t, f), BF16), jax.ShapeDtypeStruct((t, f), BF16),
                   jax.ShapeDtypeStruct((t, d), F32)],
        in_specs=[pl.BlockSpec((tm, d), row), pl.BlockSpec((tm, d), row),
                  pl.BlockSpec((tm, tf), tile), pl.BlockSpec((tm, tf), tile),
                  pl.BlockSpec((d, tf), lambda i, j: (0, j)),
                  pl.BlockSpec((d, tf), lambda i, j: (0, j + nf)),
                  pl.BlockSpec((tf, d), lambda i, j: (j, 0))],
        out_specs=[pl.BlockSpec((tm, tf), tile), pl.BlockSpec((tm, tf), tile), pl.BlockSpec((tm, d), row)],
        scratch_shapes=[pltpu.VMEM((tm, d), F32)],
        semantics=("parallel", "arbitrary"),
    )(dz, do, g, u, wgu, wgu, wd)


def _ffn_bwd_act(do, gu, wd, name, exch=()):
    t, d = do.shape
    f = wd.shape[0]
    tm, tf = _tile(t, 2048, 128), _tile(f, 512, 128)
    nf = f // tf

    def body(do_ref, g_ref, u_ref, wd_ref, dg_ref, du_ref):
        dg, du = _ffn_act_grads(_dot_nt(do_ref[...], wd_ref[...]), g_ref, u_ref)
        dg_ref[...] = dg
        du_ref[...] = du

    tile = lambda i, j: (i, j)
    return _call(
        body, exch, name=name, grid=(t // tm, f // tf),
        out_shape=[jax.ShapeDtypeStruct((t, f), BF16), jax.ShapeDtypeStruct((t, f), BF16)],
        in_specs=[pl.BlockSpec((tm, d), lambda i, j: (i, 0)), pl.BlockSpec((tm, tf), tile),
                  pl.BlockSpec((tm, tf), lambda i, j: (i, j + nf)), pl.BlockSpec((tf, d), lambda i, j: (j, 0))],
        out_specs=[pl.BlockSpec((tm, tf), tile), pl.BlockSpec((tm, tf), tile)],
        semantics=("parallel", "parallel"),
    )(do, gu, gu, wd)


def _ffn_bwd_dx(dz, dg, du, wgu, name, exch=()):
    t, d = dz.shape
    f = dg.shape[1]
    tm, tn = _tile(t, 512, 128), _tile(d, 256, 128)

    def body(dz_ref, dg_ref, du_ref, wg_ref, wu_ref, dx_ref):
        dx_ref[...] = ALPHA * dz_ref[...] + _dot_nt(dg_ref[...], wg_ref[...]) + _dot_nt(du_ref[...], wu_ref[...])

    row = lambda i, n: (i, 0)
    tile = lambda i, n: (i, n)
    return _call(
        body, exch, name=name, grid=(t // tm, d // tn), out_shape=[jax.ShapeDtypeStruct((t, d), F32)],
        in_specs=[pl.BlockSpec((tm, tn), tile), pl.BlockSpec((tm, f), row), pl.BlockSpec((tm, f), row),
                  pl.BlockSpec((tn, f), lambda i, n: (n, 0)), pl.BlockSpec((tn, f), lambda i, n: (n, 1))],
        out_specs=[pl.BlockSpec((tm, tn), tile)],
        semantics=("parallel", "arbitrary"),
    )(dz, dg, du, wgu, wgu)


def _weight_grad(at, b, tn, tmm, name, blocks=None, block_offset=0, into=None, exch=()):
    m, t = at.shape
    nn = b.shape[1]
    tmm = _tile(m, tmm, 16)
    assert nn % tn == 0

    def body(*refs):
        at_ref, b_ref, o_ref = refs[0], refs[1], refs[-1]
        r = _dot(at_ref[...], b_ref[...]).astype(BF16)
        if blocks is None:
            o_ref[...] = r
        else:
            o_ref[0] = r

    in_specs = [pl.BlockSpec((tmm, t), lambda n, i: (i, 0)), pl.BlockSpec((t, tn), lambda n, i: (0, n))]
    args = [at, b]
    aliases = {}
    if into is not None:
        in_specs.append(ANY)
        args.append(into)
        aliases = {2: 0}
    if blocks is None:
        out_shape = jax.ShapeDtypeStruct((m, nn), BF16)
        out_spec = pl.BlockSpec((tmm, tn), lambda n, i: (i, n))
    else:
        out_shape = jax.ShapeDtypeStruct((blocks, m, tn), BF16)
        out_spec = pl.BlockSpec((1, tmm, tn), lambda n, i: (n + block_offset, i, 0))
    (out,), ex = _call(
        body, exch, name=name, grid=(nn // tn, m // tmm), out_shape=[out_shape],
        in_specs=in_specs, out_specs=[out_spec], input_output_aliases=aliases,
        semantics=("parallel", "parallel"),
    )(*args)
    return out, ex


def _mix_in_proj(x, w_in, name, exch=()):
    t, d = x.shape
    n_out = w_in.shape[1]
    tm, cb = _tile(t, 512, 128), _tile(n_out, 1024, 128)

    def body(x_ref, w_ref, o_ref, xb):
        @pl.when(pl.program_id(1) == 0)
        def _():
            xb[...] = x_ref[...].astype(BF16)

        o_ref[...] = _dot(xb[...], w_ref[...])

    (out,), ex = _call(
        body, exch, name=name, grid=(t // tm, n_out // cb), out_shape=[jax.ShapeDtypeStruct((t, n_out), F32)],
        in_specs=[pl.BlockSpec((tm, d), lambda i, k: (i, 0)), pl.BlockSpec((d, cb), lambda i, k: (0, k))],
        out_specs=[pl.BlockSpec((tm, cb), lambda i, k: (i, k))],
        scratch_shapes=[pltpu.VMEM((tm, d), BF16)],
        semantics=("parallel", "arbitrary"),
    )(x, w_in)
    return out, ex


def _mix_in_bwd(dproj, w_in, dz, name, exch=()):
    t, d = dz.shape
    kk = w_in.shape[1]
    tm, tn = _tile(t, 512, 128), _tile(d, 512, 128)

    def body(dp_ref, w_ref, dz_ref, dx_ref):
        dx_ref[...] = ALPHA * dz_ref[...] + _dot_nt(dp_ref[...], w_ref[...])

    (out,), ex = _call(
        body, exch, name=name, grid=(t // tm, d // tn), out_shape=[jax.ShapeDtypeStruct((t, d), F32)],
        in_specs=[pl.BlockSpec((tm, kk), lambda i, n: (i, 0)), pl.BlockSpec((tn, kk), lambda i, n: (n, 0)),
                  pl.BlockSpec((tm, tn), lambda i, n: (i, n))],
        out_specs=[pl.BlockSpec((tm, tn), lambda i, n: (i, n))],
        semantics=("parallel", "arbitrary"),
    )(dproj, w_in, dz)
    return out, ex


def _mix_out_fwd(y, w_out, x, ln_g, ln_b, name, exch=()):
    t, d = x.shape
    kk = y.shape[1]
    tm = _tile(t, 256, 128)

    def body(y_ref, w_ref, x_ref, g_ref, b_ref, z_ref, xn_ref, xnt_ref):
        z = ALPHA * x_ref[...] + _dot(y_ref[...], w_ref[...])
        z_ref[...] = z
        xn = _ln(z, g_ref[...], b_ref[...])
        xn_ref[...] = xn
        xnt_ref[...] = xn.T.astype(BF16)

    row = lambda i: (i, 0)
    fixed = lambda i: (0, 0)
    return _call(
        body, exch, name=name, grid=(t // tm,),
        out_shape=[jax.ShapeDtypeStruct((t, d), F32), jax.ShapeDtypeStruct((t, d), F32),
                   jax.ShapeDtypeStruct((d, t), BF16)],
        in_specs=[pl.BlockSpec((tm, kk), row), pl.BlockSpec((kk, d), fixed), pl.BlockSpec((tm, d), row),
                  pl.BlockSpec((1, d), fixed), pl.BlockSpec((1, d), fixed)],
        out_specs=[pl.BlockSpec((tm, d), row), pl.BlockSpec((tm, d), row), pl.BlockSpec((d, tm), lambda i: (0, i))],
        semantics=("parallel",),
    )(y, w_out, x, ln_g, ln_b)


def _mix_out_bwd(dzb, w_out, name, exch=()):
    t, d = dzb.shape
    kk = w_out.shape[0]
    tm = _tile(t, 512, 128)

    def body(dz_ref, w_ref, dy_ref):
        dy_ref[...] = _dot_nt(dz_ref[...], w_ref[...])

    (out,), ex = _call(
        body, exch, name=name, grid=(t // tm,), out_shape=[jax.ShapeDtypeStruct((t, kk), F32)],
        in_specs=[pl.BlockSpec((tm, d), lambda i: (i, 0)), pl.BlockSpec((kk, d), lambda i: (0, 0))],
        out_specs=[pl.BlockSpec((tm, kk), lambda i: (i, 0))],
        semantics=("parallel",),
    )(dzb, w_out)
    return out, ex


def _loss_ln_bwd(z, target, ln_g, ln_b, bf16_scale, name):
    t, d = z.shape
    tm = _tile(t, 512, 8)

    def body(z_ref, t_ref, g_ref, b_ref, dz_ref, dzb_ref, dg_ref, db_ref, loss_ref):
        @pl.when(pl.program_id(0) == 0)
        def _():
            dg_ref[...] = jnp.zeros_like(dg_ref)
            db_ref[...] = jnp.zeros_like(db_ref)
            loss_ref[...] = jnp.zeros_like(loss_ref)

        xh, rstd = _ln_stats(z_ref[...])
        e = xh * g_ref[...] + b_ref[...] - t_ref[...]
        loss_ref[...] += 0.5 * jnp.sum(jnp.sum(e * e, axis=-1, keepdims=True) * (1.0 / d), axis=0, keepdims=True)
        dy = e * (1.0 / d)
        dz = _ln_bwd(dy * g_ref[...], xh, rstd)
        dz_ref[...] = dz
        dzb_ref[...] = (bf16_scale * dz).astype(BF16)
        dg_ref[...] += jnp.sum(dy * xh, axis=0, keepdims=True)
        db_ref[...] += jnp.sum(dy, axis=0, keepdims=True)

    row = lambda i: (i, 0)
    fixed = lambda i: (0, 0)
    return pl.pallas_call(
        body, name=name, grid=(t // tm,),
        out_shape=[jax.ShapeDtypeStruct((t, d), F32), jax.ShapeDtypeStruct((t, d), BF16),
                   jax.ShapeDtypeStruct((1, d), F32), jax.ShapeDtypeStruct((1, d), F32),
                   jax.ShapeDtypeStruct((8, 128), F32)],
        in_specs=[pl.BlockSpec((tm, d), row), pl.BlockSpec((tm, d), row), pl.BlockSpec((1, d), fixed),
                  pl.BlockSpec((1, d), fixed)],
        out_specs=[pl.BlockSpec((tm, d), row), pl.BlockSpec((tm, d), row), pl.BlockSpec((1, d), fixed),
                   pl.BlockSpec((1, d), fixed), pl.BlockSpec((8, 128), fixed)],
        compiler_params=_cparams("arbitrary"),
    )(z, target, ln_g, ln_b)


def _ln_bwd_call(z, dy, ln_g, bf16_scale, name, exch=()):
    t, d = z.shape
    tm = _tile(t, 512, 8)

    def body(z_ref, dy_ref, g_ref, dz_ref, dzb_ref, dg_ref, db_ref):
        @pl.when(pl.program_id(0) == 0)
        def _():
            dg_ref[...] = jnp.zeros_like(dg_ref)
            db_ref[...] = jnp.zeros_like(db_ref)

        xh, rstd = _ln_stats(z_ref[...])
        dy = dy_ref[...]
        dz = _ln_bwd(dy * g_ref[...], xh, rstd)
        dz_ref[...] = dz
        dzb_ref[...] = (bf16_scale * dz).astype(BF16)
        dg_ref[...] += jnp.sum(dy * xh, axis=0, keepdims=True)
        db_ref[...] += jnp.sum(dy, axis=0, keepdims=True)

    row = lambda i: (i, 0)
    fixed = lambda i: (0, 0)
    return _call(
        body, exch, name=name, grid=(t // tm,),
        out_shape=[jax.ShapeDtypeStruct((t, d), F32), jax.ShapeDtypeStruct((t, d), BF16),
                   jax.ShapeDtypeStruct((1, d), F32), jax.ShapeDtypeStruct((1, d), F32)],
        in_specs=[pl.BlockSpec((tm, d), row), pl.BlockSpec((tm, d), row), pl.BlockSpec((1, d), fixed)],
        out_specs=[pl.BlockSpec((tm, d), row), pl.BlockSpec((tm, d), row), pl.BlockSpec((1, d), fixed),
                   pl.BlockSpec((1, d), fixed)],
        semantics=("arbitrary",),
    )(z, dy, ln_g)


CONV_ROWS = 32
CONV_LANES = 512
SUBLANES = 8


def _fill_shifted(ext, shifted):
    rows = ext.shape[0] - SUBLANES
    for s in range(1, SUBLANES):
        for r in range(0, rows, CONV_ROWS):
            n = min(CONV_ROWS, rows - r)
            shifted[s - 1, r:r + n, :] = ext[r + s:r + s + n, :]


def _window(ext, shifted, lo, n, lanes=slice(None)):
    s = lo % SUBLANES
    return ext[lo:lo + n, lanes] if s == 0 else shifted[s - 1, lo - s:lo - s + n, lanes]


def _mixer_fwd(proj, conv_w, conv_b, cln_g, cln_b, sln_g, sln_b, sg_wm, sg_bb, name, exch=()):
    t = proj.shape[0]
    tm = _tile(t, 256, CHUNK)
    hb = tm // HALO
    nc = tm // CHUNK
    ch = CONV_CH

    def body(av_ref, ag_ref, bu_ref, bv_ref, hv_ref, hg_ref, cw_ref, cb_ref, lg_ref, lb_ref, sg_ref, sb_ref,
             w_ref, bb_ref, y_ref, yt_ref, c_ref, ext, ext_s):
        i = pl.program_id(0)
        halo = hv_ref[...] * _sigmoid(hg_ref[...])
        ext[0:HALO, :] = jnp.where(i > 0, halo, 0.0)
        ext[HALO:HALO + tm, :] = av_ref[...] * _sigmoid(ag_ref[...])
        _fill_shifted(ext, ext_s)
        for r in range(0, tm, CONV_ROWS):
            acc = jnp.zeros((CONV_ROWS, ch), F32) + cb_ref[...]
            for k in range(CONV_TAPS):
                lo = r + k + HALO - (CONV_TAPS - 1)
                acc = acc + cw_ref[k:k + 1, :] * _window(ext, ext_s, lo, CONV_ROWS)
            c_ref[r:r + CONV_ROWS, :] = acc
        a = _ln(c_ref[...], lg_ref[...], lb_ref[...])
        ya = a * _sigmoid(a)
        y_ref[:, 0:ch] = ya.astype(BF16)
        yt_ref[0:ch, :] = ya.T.astype(BF16)
        for h in range(HEADS):
            sl = slice(h * HEAD_DIM, (h + 1) * HEAD_DIM)
            u, _ = _gelu_and_grad(bu_ref[:, sl])
            v, _ = _gelu_and_grad(bv_ref[:, sl])
            vn = _ln(v, sg_ref[h:h + 1, :], sb_ref[h:h + 1, :])
            vn3 = vn.astype(BF16).reshape(nc, CHUNK, HEAD_DIM)
            wb = jnp.broadcast_to(w_ref[h][None], (nc, CHUNK, CHUNK))
            mixed = jnp.einsum("cts,csd->ctd", wb, vn3, preferred_element_type=F32) + bb_ref[h][None]
            yb = u * mixed.reshape(tm, HEAD_DIM)
            y_ref[:, ch + h * HEAD_DIM:ch + (h + 1) * HEAD_DIM] = yb.astype(BF16)
            yt_ref[ch + h * HEAD_DIM:ch + (h + 1) * HEAD_DIM, :] = yb.T.astype(BF16)

    col = lambda cidx: (lambda i: (i, cidx))
    prev = lambda cidx: (lambda i: (jnp.maximum(i * hb - 1, 0), cidx))
    fix2 = lambda i: (0, 0)
    fix3 = lambda i: (0, 0, 0)
    return _call(
        body, exch, name=name, grid=(t // tm,),
        out_shape=[jax.ShapeDtypeStruct((t, 2 * ch), BF16), jax.ShapeDtypeStruct((2 * ch, t), BF16),
                   jax.ShapeDtypeStruct((t, ch), F32)],
        in_specs=[pl.BlockSpec((tm, ch), col(0)), pl.BlockSpec((tm, ch), col(1)), pl.BlockSpec((tm, ch), col(2)),
                  pl.BlockSpec((tm, ch), col(3)), pl.BlockSpec((HALO, ch), prev(0)), pl.BlockSpec((HALO, ch), prev(1)),
                  pl.BlockSpec((CONV_TAPS, ch), fix2), pl.BlockSpec((1, ch), fix2), pl.BlockSpec((1, ch), fix2),
                  pl.BlockSpec((1, ch), fix2), pl.BlockSpec((HEADS, HEAD_DIM), fix2), pl.BlockSpec((HEADS, HEAD_DIM), fix2),
                  pl.BlockSpec((HEADS, CHUNK, CHUNK), fix3), pl.BlockSpec((HEADS, CHUNK, HEAD_DIM), fix3)],
        out_specs=[pl.BlockSpec((tm, 2 * ch), lambda i: (i, 0)), pl.BlockSpec((2 * ch, tm), lambda i: (0, i)),
                   pl.BlockSpec((tm, ch), lambda i: (i, 0))],
        scratch_shapes=[pltpu.VMEM((HALO + tm, ch), F32), pltpu.VMEM((SUBLANES - 1, HALO + tm, ch), F32)],
        semantics=("parallel",),
    )(proj, proj, proj, proj, proj, proj, conv_w, conv_b, cln_g, cln_b, sln_g, sln_b, sg_wm, sg_bb)


def _mixer_bwd(proj, conv_c, dy, conv_w, cln_g, cln_b, sln_g, sln_b, sg_wm, sg_wmt, sg_bb, name, exch=()):
    t = proj.shape[0]
    tm = _tile(t, 256, CHUNK)
    hb = tm // HALO
    nc = tm // CHUNK
    nt = t // tm
    ch = CONV_CH
    last_halo = t // HALO - 1

    def body(av_ref, ag_ref, bu_ref, bv_ref, hv_ref, hg_ref, c_ref, cn_ref, dya_ref, dyan_ref, dyb_ref,
             cw_ref, lg_ref, lb_ref, sg_ref, sb_ref, w_ref, wt_ref, bb_ref,
             dp_ref, dcw_ref, dcb_ref, dlg_ref, dlb_ref, dsg_ref, dsb_ref, dw_ref, dbs_ref,
             ext_h, ext_dc, ext_hs, ext_dcs, acc_cw):
        i = pl.program_id(0)

        @pl.when(i == 0)
        def _():
            acc_cw[...] = jnp.zeros_like(acc_cw)
            for ref in (dcb_ref, dlg_ref, dlb_ref, dsg_ref, dsb_ref, dw_ref, dbs_ref):
                ref[...] = jnp.zeros_like(ref)

        lg = lg_ref[...]
        lb = lb_ref[...]

        def conv_ln_bwd(c, dya):
            xh, rstd = _ln_stats(c)
            a = xh * lg + lb
            da = dya * _silu_grad(a)
            return _ln_bwd(da * lg, xh, rstd), da, xh

        fold = lambda v: jnp.sum(v.reshape(CONV_ROWS // SUBLANES, SUBLANES, ch), axis=0)
        s_lg = s_lb = s_cb = jnp.zeros((SUBLANES, ch), F32)
        for r in range(0, tm, CONV_ROWS):
            dc, da, xh = conv_ln_bwd(c_ref[r:r + CONV_ROWS, :], dya_ref[r:r + CONV_ROWS, :])
            ext_dc[r:r + CONV_ROWS, :] = dc
            s_lg, s_lb, s_cb = s_lg + fold(da * xh), s_lb + fold(da), s_cb + fold(dc)
        dlg_ref[...] += jnp.sum(s_lg, axis=0, keepdims=True)
        dlb_ref[...] += jnp.sum(s_lb, axis=0, keepdims=True)
        dcb_ref[...] += jnp.sum(s_cb, axis=0, keepdims=True)
        dcn, _, _ = conv_ln_bwd(cn_ref[...], dyan_ref[...])
        ext_dc[tm:tm + HALO, :] = jnp.where(i < nt - 1, dcn, 0.0)
        halo = hv_ref[...] * _sigmoid(hg_ref[...])
        ext_h[0:HALO, :] = jnp.where(i > 0, halo, 0.0)
        ext_h[HALO:HALO + tm, :] = av_ref[...] * _sigmoid(ag_ref[...])
        _fill_shifted(ext_h, ext_hs)
        _fill_shifted(ext_dc, ext_dcs)
        for r, c0 in [(r, c0) for r in range(0, tm, CONV_ROWS) for c0 in range(0, ch, CONV_LANES)]:
            rows, lanes = slice(r, r + CONV_ROWS), slice(c0, c0 + CONV_LANES)
            dcr = ext_dc[rows, lanes]
            acc = jnp.zeros((CONV_ROWS, CONV_LANES), F32)
            for k in range(CONV_TAPS):
                lo = r + k + HALO - (CONV_TAPS - 1)
                prod = dcr * _window(ext_h, ext_hs, lo, CONV_ROWS, lanes)
                acc_cw[k, :, lanes] += jnp.sum(prod.reshape(CONV_ROWS // SUBLANES, SUBLANES, CONV_LANES), axis=0)
                hi = r + (CONV_TAPS - 1) - k
                acc = acc + cw_ref[k:k + 1, lanes] * _window(ext_dc, ext_dcs, hi, CONV_ROWS, lanes)
            sg_r = _sigmoid(ag_ref[rows, lanes])
            av_r = av_ref[rows, lanes]
            dp_ref[rows, lanes] = (acc * sg_r).astype(BF16)
            dp_ref[rows, slice(ch + c0, ch + c0 + CONV_LANES)] = (acc * av_r * sg_r * (1.0 - sg_r)).astype(BF16)

        @pl.when(i == nt - 1)
        def _():
            dcw_ref[...] = jnp.sum(acc_cw[...], axis=1)

        tril = (lax.broadcasted_iota(jnp.int32, (CHUNK, CHUNK), 0)
                >= lax.broadcasted_iota(jnp.int32, (CHUNK, CHUNK), 1)).astype(F32)
        for h in range(HEADS):
            sl = slice(h * HEAD_DIM, (h + 1) * HEAD_DIM)
            u, du_dx = _gelu_and_grad(bu_ref[:, sl])
            v, dv_dx = _gelu_and_grad(bv_ref[:, sl])
            xhv, rstdv = _ln_stats(v)
            gh = sg_ref[h:h + 1, :]
            vn3 = (xhv * gh + sb_ref[h:h + 1, :]).astype(BF16).reshape(nc, CHUNK, HEAD_DIM)
            wb = jnp.broadcast_to(w_ref[h][None], (nc, CHUNK, CHUNK))
            mixed = jnp.einsum("cts,csd->ctd", wb, vn3, preferred_element_type=F32) + bb_ref[h][None]
            dyb = dyb_ref[:, sl]
            d_u = dyb * mixed.reshape(tm, HEAD_DIM)
            dm = dyb * u
            dm3 = dm.reshape(nc, CHUNK, HEAD_DIM)
            dbs_ref[h:h + 1, :] += jnp.sum(jnp.sum(dm3, axis=0).T, axis=0, keepdims=True)
            dm3b = dm3.astype(BF16)
            dw_h = jnp.sum(jnp.einsum("ctd,csd->cts", dm3b, vn3, preferred_element_type=F32), axis=0)
            dw_ref[h] += dw_h * tril
            wtb = jnp.broadcast_to(wt_ref[h][None], (nc, CHUNK, CHUNK))
            d_vn = jnp.einsum("cst,ctd->csd", wtb, dm3b, preferred_element_type=F32).reshape(tm, HEAD_DIM)
            dsg_ref[h:h + 1, :] += jnp.sum(d_vn * xhv, axis=0, keepdims=True)
            dsb_ref[h:h + 1, :] += jnp.sum(d_vn, axis=0, keepdims=True)
            dv = _ln_bwd(d_vn * gh, xhv, rstdv)
            dp_ref[:, 2 * ch + h * HEAD_DIM:2 * ch + (h + 1) * HEAD_DIM] = (d_u * du_dx).astype(BF16)
            dp_ref[:, 3 * ch + h * HEAD_DIM:3 * ch + (h + 1) * HEAD_DIM] = (dv * dv_dx).astype(BF16)

    col = lambda cidx: (lambda i: (i, cidx))
    prev = lambda cidx: (lambda i: (jnp.maximum(i * hb - 1, 0), cidx))
    nxt = lambda i: (jnp.minimum((i + 1) * hb, last_halo), 0)
    fix2 = lambda i: (0, 0)
    fix3 = lambda i: (0, 0, 0)
    out_shape = [jax.ShapeDtypeStruct((t, 4 * ch), BF16), jax.ShapeDtypeStruct((CONV_TAPS, ch), F32),
                 jax.ShapeDtypeStruct((1, ch), F32), jax.ShapeDtypeStruct((1, ch), F32), jax.ShapeDtypeStruct((1, ch), F32),
                 jax.ShapeDtypeStruct((HEADS, HEAD_DIM), F32), jax.ShapeDtypeStruct((HEADS, HEAD_DIM), F32),
                 jax.ShapeDtypeStruct((HEADS, CHUNK, CHUNK), F32), jax.ShapeDtypeStruct((HEADS, CHUNK), F32)]
    out_specs = [pl.BlockSpec((tm, 4 * ch), lambda i: (i, 0)), pl.BlockSpec((CONV_TAPS, ch), fix2),
                 pl.BlockSpec((1, ch), fix2), pl.BlockSpec((1, ch), fix2), pl.BlockSpec((1, ch), fix2),
                 pl.BlockSpec((HEADS, HEAD_DIM), fix2), pl.BlockSpec((HEADS, HEAD_DIM), fix2),
                 pl.BlockSpec((HEADS, CHUNK, CHUNK), fix3), pl.BlockSpec((HEADS, CHUNK), fix2)]
    in_specs = [pl.BlockSpec((tm, ch), col(0)), pl.BlockSpec((tm, ch), col(1)), pl.BlockSpec((tm, ch), col(2)),
                pl.BlockSpec((tm, ch), col(3)), pl.BlockSpec((HALO, ch), prev(0)), pl.BlockSpec((HALO, ch), prev(1)),
                pl.BlockSpec((tm, ch), col(0)), pl.BlockSpec((HALO, ch), nxt),
                pl.BlockSpec((tm, ch), col(0)), pl.BlockSpec((HALO, ch), nxt), pl.BlockSpec((tm, ch), col(1)),
                pl.BlockSpec((CONV_TAPS, ch), fix2), pl.BlockSpec((1, ch), fix2), pl.BlockSpec((1, ch), fix2),
                pl.BlockSpec((HEADS, HEAD_DIM), fix2), pl.BlockSpec((HEADS, HEAD_DIM), fix2),
                pl.BlockSpec((HEADS, CHUNK, CHUNK), fix3), pl.BlockSpec((HEADS, CHUNK, CHUNK), fix3),
                pl.BlockSpec((HEADS, CHUNK, HEAD_DIM), fix3)]
    return _call(
        body, exch, name=name, grid=(nt,), out_shape=out_shape, in_specs=in_specs, out_specs=out_specs,
        scratch_shapes=[pltpu.VMEM((HALO + tm, ch), F32), pltpu.VMEM((tm + HALO, ch), F32),
                        pltpu.VMEM((SUBLANES - 1, HALO + tm, ch), F32), pltpu.VMEM((SUBLANES - 1, tm + HALO, ch), F32),
                        pltpu.VMEM((CONV_TAPS, 8, ch), F32)],
        semantics=("arbitrary",),
    )(proj, proj, proj, proj, proj, proj, conv_c, conv_c, dy, dy, dy,
      conv_w, cln_g, cln_b, sln_g, sln_b, sg_wm, sg_wmt, sg_bb)


def _pair_sum(parts, from_sibling, core_chip, name):
    _, r, cc = parts.shape
    tr = _tile(r, max(16, (1 << 20) // (2 * cc)), 16)

    def body(cc_ref, p_ref, s_ref, o_ref, own_ref):
        q = (p_ref[...].astype(F32) + s_ref[...].astype(F32)).astype(BF16)
        o_ref[...] = q

        @pl.when(pl.program_id(1) == cc_ref[1])
        def _():
            own_ref[...] = q[0]

    grid_spec = pltpu.PrefetchScalarGridSpec(
        num_scalar_prefetch=1, grid=(r // tr, 4),
        in_specs=[pl.BlockSpec((1, tr, cc), lambda i, j, cc_ref: (2 * j + cc_ref[0], i, 0)),
                  pl.BlockSpec((1, tr, cc), lambda i, j, cc_ref: (j, i, 0))],
        out_specs=[pl.BlockSpec((1, tr, cc), lambda i, j, cc_ref: (j, i, 0)),
                   pl.BlockSpec((tr, cc), lambda i, j, cc_ref: (i, 0))])
    return pl.pallas_call(
        body, name=name, grid_spec=grid_spec,
        out_shape=[jax.ShapeDtypeStruct((4, r, cc), BF16), jax.ShapeDtypeStruct((r, cc), BF16)],
        compiler_params=_cparams("parallel", "arbitrary"),
    )(core_chip, parts, from_sibling)


def _adamw_math(w, g, m, v):
    m = ADAM_B1 * m + (1.0 - ADAM_B1) * g
    v = ADAM_B2 * v + (1.0 - ADAM_B2) * (g * g)
    m_hat = m / (1.0 - ADAM_B1 ** ADAM_STEP)
    v_hat = v / (1.0 - ADAM_B2 ** ADAM_STEP)
    delta = -ADAM_LR * (m_hat / (jnp.sqrt(v_hat) + ADAM_EPS) + ADAM_WD * w)
    return delta, m, v


def _adamw_tile(in_refs, out_refs):
    w_ref, m_ref, v_ref, q_ref, o_ref = in_refs
    g = q_ref[...].astype(F32)
    for k in range(3):
        g = g + o_ref[k].astype(F32)
    d, mm, vv = _adamw_math(w_ref[...], g, m_ref[...], v_ref[...])
    for ref, val in zip(out_refs, (g, d, mm, vv)):
        ref[...] = val


def _adamw_side(w, m, v, chip_part, from_chips, max_tiles):
    r, cc = w.shape
    n = max(k for k in range(1, max_tiles + 1) if r % k == 0 and (r // k) % 16 == 0)
    tr = r // n
    row = ((tr, cc), lambda s: (s, 0))
    return _Side([w, m, v, chip_part, from_chips], [row, row, row, row, ((3, tr, cc), lambda s: (0, s, 0))],
                 [jax.ShapeDtypeStruct((r, cc), F32)] * 4, [row] * 4, n, _adamw_tile)


def _adamw_sharded(w, m, v, chip_part, from_chips, name):
    r, cc = w.shape
    tr = _tile(r, max(16, (1 << 19) // (4 * cc) * 2), 16)

    def body(*refs):
        _adamw_tile(refs[:5], refs[5:])

    row = pl.BlockSpec((tr, cc), lambda i: (i, 0))
    return pl.pallas_call(
        body, name=name, grid=(r // tr,), out_shape=[jax.ShapeDtypeStruct((r, cc), F32)] * 4,
        in_specs=[row, row, row, row, pl.BlockSpec((3, tr, cc), lambda i: (0, i, 0))], out_specs=[row] * 4,
        compiler_params=_cparams("parallel"),
    )(w, m, v, chip_part, from_chips)


def _adamw_small(w, g, m, v, name):
    r, cc = w.shape

    def body(w_ref, g_ref, m_ref, v_ref, d_out, m_out, v_out):
        d, mm, vv = _adamw_math(w_ref[...], g_ref[...], m_ref[...], v_ref[...])
        d_out[...] = d
        m_out[...] = mm
        v_out[...] = vv

    full = pl.BlockSpec((r, cc), lambda i: (0, 0))
    return pl.pallas_call(
        body, name=name, grid=(1,), out_shape=[jax.ShapeDtypeStruct((r, cc), F32)] * 3,
        in_specs=[full] * 4, out_specs=[full] * 3, compiler_params=_cparams("arbitrary"),
    )(w, g, m, v)


SMALL = ("ln1_g", "ln1_b", "conv_b", "conv_ln_g", "conv_ln_b", "sg_ln_g", "sg_ln_b", "sg_w", "sg_b",
         "ln2_g", "ln2_b", "ln3_g", "ln3_b")
ORDER = ("ffn1_w_gate_up", "ffn1_w_down", "ln1_g", "ln1_b", "mix_w_in", "conv_w", "conv_b", "conv_ln_g", "conv_ln_b",
         "sg_ln_g", "sg_ln_b", "sg_w", "sg_b", "mix_w_out", "ln2_g", "ln2_b", "ffn2_w_gate_up", "ffn2_w_down",
         "ln3_g", "ln3_b")


def _rows128(a):
    return a.reshape(-1, 128)


def kernel(x, ffn1_w_gate_up, ffn1_w_down, ln1_g, ln1_b, mix_w_in, conv_w, conv_b, conv_ln_g, conv_ln_b, sg_ln_g, sg_ln_b, sg_w, sg_b, mix_w_out, ln2_g, ln2_b, ffn2_w_gate_up, ffn2_w_down, ln3_g, ln3_b, loss_target, m_ffn1_w_gate_up, m_ffn1_w_down, m_ln1_g, m_ln1_b, m_mix_w_in, m_conv_w, m_conv_b, m_conv_ln_g, m_conv_ln_b, m_sg_ln_g, m_sg_ln_b, m_sg_w, m_sg_b, m_mix_w_out, m_ln2_g, m_ln2_b, m_ffn2_w_gate_up, m_ffn2_w_down, m_ln3_g, m_ln3_b, v_ffn1_w_gate_up, v_ffn1_w_down, v_ln1_g, v_ln1_b, v_mix_w_in, v_conv_w, v_conv_b, v_conv_ln_g, v_conv_ln_b, v_sg_ln_g, v_sg_ln_b, v_sg_w, v_sg_b, v_mix_w_out, v_ln2_g, v_ln2_b, v_ffn2_w_gate_up, v_ffn2_w_down, v_ln3_g, v_ln3_b):
    args = dict(locals())
    w = {n: args[n][0] for n in ORDER}
    mom = {n: args["m_" + n][0] for n in ORDER}
    var = {n: args["v_" + n][0] for n in ORDER}
    x0 = x[0]
    target = loss_target[0]
    t, d = x0.shape
    my_x, my_y, my_c = lax.axis_index("x"), lax.axis_index("y"), lax.axis_index("c")
    my_chip = (2 * my_x + my_y).astype(jnp.int32).reshape(1)
    my_core = my_c.astype(jnp.int32).reshape(1)
    me = 4 * my_x + 2 * my_y + my_c

    big = ("ffn1_w_gate_up", "ffn1_w_down", "mix_w_in", "mix_w_out", "ffn2_w_gate_up", "ffn2_w_down")
    sh = {n: w[n].astype(BF16) for n in big}
    f2s = sh["ffn2_w_gate_up"].shape[1]
    order = jnp.stack([4 * p[0] + 2 * p[1] + p[2] for p in _visit_order(my_x, my_y, my_c)]).astype(jnp.int32)
    gu1, x0t, (wgu1, wd1, conv_w_all) = _gather_and_gate_up(
        x0, [sh["ffn1_w_gate_up"], sh["ffn1_w_down"], w["conv_w"]], [True, True, False], order, "ffn1_gate_up_fwd")
    wd1 = wd1.reshape(-1, d)
    conv_w_full = jnp.transpose(conv_w_all, (1, 0, 2)).reshape(CONV_TAPS, CONV_CH)
    tril = jnp.tril(jnp.ones((CHUNK, CHUNK), F32))
    sg_wm = w["sg_w"] * tril
    sg_wm_b = sg_wm.astype(BF16)
    sg_wmt_b = jnp.swapaxes(sg_wm, 1, 2).astype(BF16)
    sg_bb = jnp.broadcast_to(w["sg_b"][:, :, None], (HEADS, CHUNK, HEAD_DIM))
    row = lambda a: a.reshape(1, -1)

    d2 = [sh["ffn2_w_down"]]
    d2_first = d2[0].shape[0] // 2 // 16 * 16
    d2_top, d2_bottom = (0, d2_first), (d2_first, d2[0].shape[0] - d2_first)
    (h1t, z1, x1), ((g_in, g_out), (g_d2,)) = _ffn_down_fwd(
        gu1, x0, wd1, row(w["ln1_g"]), row(w["ln1_b"]), "ffn1_down_fwd",
        exch=[_gather_first([sh["mix_w_in"], sh["mix_w_out"]], [True, False]),
              _gather_first(d2, [False], rows=d2_top)])
    in_cols = sh["mix_w_in"].shape[1]
    x1t, ((w_in, w_out),) = _transpose_bf16(
        x1, "x1_transpose", exch=[_gather_forward([g_in, g_out], [True, False], [in_cols, None])])
    w_out = w_out.reshape(-1, d)
    top, bottom = (0, d // 2), (d // 2, d // 2)
    gu2 = [sh["ffn2_w_gate_up"]]
    proj, ((g_gu2,), (g_d2,)) = _mix_in_proj(
        x1, w_in, "mix_in_fwd",
        exch=[_gather_first(gu2, [True], rows=top), _gather_forward([g_d2], [False], [None], rows=d2_top)])
    (y, yt, conv_c), ((g_gu2,),) = _mixer_fwd(
        proj, conv_w_full, row(w["conv_b"]), row(w["conv_ln_g"]), row(w["conv_ln_b"]),
        w["sg_ln_g"], w["sg_ln_b"], sg_wm_b, sg_bb, "mixer_fwd",
        exch=[_both(_gather_first(gu2, [True], rows=bottom, into=[g_gu2]),
                    _gather_forward([g_gu2], [True], [f2s], rows=top))])
    (z2, x2, x2t), ((wgu2,), (g_d2,)) = _mix_out_fwd(
        y, w_out, x1, row(w["ln2_g"]), row(w["ln2_b"]), "mix_out_fwd",
        exch=[_gather_forward([g_gu2], [True], [f2s], rows=bottom),
              _gather_first(d2, [False], rows=d2_bottom, into=[g_d2])])
    (wd2,) = _exchange_alone(_gather_forward([g_d2], [False], [None], rows=d2_bottom), "ffn2_down_gather_forward")
    wd2 = wd2.reshape(-1, d)
    grads = {}
    (g2, u2, h2t, dz3, do2, grads["ln3_g"], grads["ln3_b"], loss_tile), _ = _ffn_fwd_loss(
        x2, wgu2, wd2, row(w["ln3_g"]), row(w["ln3_b"]), target, "ffn2_fwd_loss")

    f = wd1.shape[0]
    dn = _tile(d, 1024, 128)
    core_chip = jnp.concatenate([my_core, my_chip])
    pair = lambda p, s, label: _pair_sum(p, s, core_chip, "pair_sum_" + label)
    adamw = lambda n, own, got, steps: _adamw_side(w[n], mom[n], var[n], own, got, steps)
    m_tiles = d // _tile(d, 512, 16)
    gu_first = d * 3 // 4 // 16 * 16
    out = {}
    p_d2, _ = _weight_grad(h2t, do2, dn, 512, "ffn2_dw_down")
    p_d2 = p_d2.reshape(N_DEV, f // N_DEV, d)
    (dg2, du2, dx2), ((s_d2,),) = _ffn_bwd(dz3, do2, g2, u2, wgu2, wd2, "ffn2_bwd", exch=[_rs_sibling([p_d2])])
    q_d2, own_d2 = pair(p_d2, s_d2, "ffn2_down")
    d_rows = q_d2.shape[1]
    d_half = d_rows // 2 // 16 * 16
    p_gu2, ((r_d2,),) = _weight_grad(x2t, dg2, f2s, 512, "ffn2_dw_gate", blocks=N_DEV,
                                     exch=[_rs_chips([q_d2], rows=(0, d_half))])
    p_gu2, ((r_d2,),) = _weight_grad(x2t, du2, f2s, 512, "ffn2_dw_up", blocks=N_DEV, block_offset=4, into=p_gu2,
                                     exch=[_rs_chips([q_d2], rows=(d_half, d_rows - d_half), into=[r_d2])])
    (dz2, dz2b, grads["ln2_g"], grads["ln2_b"]), ((s_gu2,),) = _ln_bwd_call(
        z2, dx2, row(w["ln2_g"]), 1.0, "ln2_bwd", exch=[_rs_sibling([p_gu2])])
    q_gu2, own_gu2 = pair(p_gu2, s_gu2, "ffn2_gate_up")
    dy, _ = _mix_out_bwd(dz2b, w_out, "mix_out_bwd")
    p_out, _ = _weight_grad(yt, dz2b, dn, 512, "mix_out_dw")
    p_out = p_out.reshape(N_DEV, -1, d)
    (dproj, grads["conv_w"], grads["conv_b"], grads["conv_ln_g"], grads["conv_ln_b"], grads["sg_ln_g"],
     grads["sg_ln_b"], grads["sg_w"], grads["sg_b"]), ((r_gu2,),) = _mixer_bwd(
        proj, conv_c, dy, conv_w_full, row(w["conv_ln_g"]), row(w["conv_ln_b"]), w["sg_ln_g"], w["sg_ln_b"],
        sg_wm_b, sg_wmt_b, sg_bb, "mixer_bwd", exch=[_rs_chips([q_gu2], rows=(0, gu_first))])
    dx1, ((s_out,), (r_gu2,)) = _mix_in_bwd(
        dproj, w_in, dz2, "mix_in_bwd",
        exch=[_rs_sibling([p_out]), _rs_chips([q_gu2], rows=(gu_first, d - gu_first), into=[r_gu2])])
    p_in, (out["ffn2_w_gate_up"], out["ffn2_w_down"]) = _weight_grad(
        x1t, dproj, in_cols, 512, "mix_in_dw", blocks=N_DEV,
        exch=[adamw("ffn2_w_gate_up", own_gu2, r_gu2, N_DEV * m_tiles), adamw("ffn2_w_down", own_d2, r_d2, N_DEV * m_tiles)])
    (dz1, do1, grads["ln1_g"], grads["ln1_b"]), ((s_in,),) = _ln_bwd_call(
        z1, dx1, row(w["ln1_g"]), 0.5, "ln1_bwd", exch=[_rs_sibling([p_in])])
    q_out, own_out = pair(p_out, s_out, "mix_out")
    q_in, own_in = pair(p_in, s_in, "mix_in")
    small_parts = [_rows128(grads[n]) for n in SMALL]
    packed = jnp.concatenate(small_parts + [_rows128(grads["conv_w"]), loss_tile], axis=0)
    p_d1, ((r_in,),) = _weight_grad(h1t, do1, dn, 512, "ffn1_dw_down", exch=[_rs_chips([q_in])])
    p_d1 = p_d1.reshape(N_DEV, f // N_DEV, d)
    (dg1, du1), ((s_d1,), (r_out,), (small_all,)) = _ffn_bwd_act(
        do1, gu1, wd1, "ffn1_bwd_act",
        exch=[_rs_sibling([p_d1]), _rs_chips([q_out]), _small_gather(packed)])
    q_d1, own_d1 = pair(p_d1, s_d1, "ffn1_down")
    p_gu1, ((r_d1,),) = _weight_grad(x0t, dg1, f2s, 512, "ffn1_dw_gate", blocks=N_DEV, exch=[_rs_chips([q_d1])])
    p_gu1, (out["mix_w_in"], out["mix_w_out"]) = _weight_grad(
        x0t, du1, f2s, 512, "ffn1_dw_up", blocks=N_DEV, block_offset=4, into=p_gu1,
        exch=[adamw("mix_w_in", own_in, r_in, 4 * m_tiles), adamw("mix_w_out", own_out, r_out, 4 * m_tiles)])
    (s_gu1,) = _exchange_alone(_rs_sibling([p_gu1]), "ffn1_gate_up_sibling_exchange")
    q_gu1, own_gu1 = pair(p_gu1, s_gu1, "ffn1_gate_up")
    (grad_x,), ((r_gu1,),) = _ffn_bwd_dx(dz1, dg1, du1, wgu1, "ffn1_bwd_dx", exch=[_rs_chips([q_gu1])])
    for n, own, got in (("ffn1_w_down", own_d1, r_d1), ("ffn1_w_gate_up", own_gu1, r_gu1)):
        out[n] = _adamw_sharded(w[n], mom[n], var[n], own, got, "adamw_" + n)

    cw_rows = CONV_TAPS * CONV_CH // 128
    total = _sum_over_devices(small_all)
    offs = [0]
    for p in small_parts:
        offs.append(offs[-1] + p.shape[0])
    n_small = offs[-1]
    loss = total[n_small + cw_rows, 0]
    g_conv_w = lax.dynamic_slice_in_dim(total[n_small:n_small + cw_rows].reshape(CONV_TAPS, CONV_CH),
                                        me * (CONV_CH // N_DEV), CONV_CH // N_DEV, axis=1)
    pad8 = lambda a: jnp.pad(a, ((0, -a.shape[0] % 8), (0, 0)))
    pack = lambda tree, cw: jnp.concatenate([_rows128(tree[n]) for n in SMALL] + [pad8(cw)], axis=0)
    g_pack = jnp.concatenate([total[:n_small], pad8(g_conv_w)], axis=0)
    d_pack, m_pack, v_pack = _adamw_small(pack(w, w["conv_w"]), g_pack, pack(mom, mom["conv_w"]),
                                          pack(var, var["conv_w"]), "adamw_small")
    for k, n in enumerate(SMALL):
        sl = slice(offs[k], offs[k + 1])
        shp = w[n].shape
        out[n] = (total[sl].reshape(shp), d_pack[sl].reshape(shp), m_pack[sl].reshape(shp), v_pack[sl].reshape(shp))
    sl = slice(n_small, n_small + CONV_TAPS)
    out["conv_w"] = (g_conv_w, d_pack[sl], m_pack[sl], v_pack[sl])

    lead = lambda a: a[None]
    res = [loss, grad_x[None]]
    for kind in range(4):
        res += [lead(out[n][kind]) for n in ORDER]
    return tuple(res)
```

```python
import functools
import math

import jax
import jax.numpy as jnp
from jax import lax
from jax.experimental import pallas as pl
from jax.experimental.pallas import tpu as pltpu

F32, BF16 = jnp.float32, jnp.bfloat16
MESH = pl.DeviceIdType.MESH
ANY = pl.BlockSpec(memory_space=pl.ANY)

N_DEV = 8
LN_EPS = 1e-5
ALPHA = 2.0 ** 0.25
CONV_CH = 1024
CONV_TAPS = 31
HALO = 32
HEADS = 8
HEAD_DIM = 128
CHUNK = 128
ADAM_LR, ADAM_B1, ADAM_B2, ADAM_EPS, ADAM_WD, ADAM_STEP = 0.001, 0.9, 0.999, 1e-08, 0.01, 10
V7X_VMEM_LIMIT = 62 * 2 ** 20
EPILOGUE_ROWS = 128

def _cparams(*sem):
    return pltpu.CompilerParams(dimension_semantics=sem, vmem_limit_bytes=V7X_VMEM_LIMIT)


def _tile(n, pref, mult):
    best = None
    for t in range(mult, min(n, pref) + 1, mult):
        if n % t == 0:
            best = t
    return best if best is not None else n


def _dot(a, b):
    return jnp.dot(a, b, preferred_element_type=F32)


def _dot_nt(a, b):
    return lax.dot_general(a, b, (((1,), (1,)), ((), ())), preferred_element_type=F32)


def _sigmoid(x):
    return 1.0 / (1.0 + jnp.exp(-x))


def _ln_stats(z):
    mu = jnp.mean(z, axis=-1, keepdims=True)
    zc = z - mu
    var = jnp.mean(zc * zc, axis=-1, keepdims=True)
    rstd = lax.rsqrt(var + LN_EPS)
    return zc * rstd, rstd


def _ln(z, g, b):
    xh, _ = _ln_stats(z)
    return xh * g + b


def _ln_bwd(dxh, xh, rstd):
    m1 = jnp.mean(dxh, axis=-1, keepdims=True)
    m2 = jnp.mean(dxh * xh, axis=-1, keepdims=True)
    return rstd * (dxh - m1 - xh * m2)


_GK = math.sqrt(2.0 / math.pi)
_GA = 0.044715


def _gelu_and_grad(x):
    x2 = x * x
    t = jnp.tanh(_GK * (x + _GA * x * x2))
    y = 0.5 * x * (1.0 + t)
    dy = 0.5 * (1.0 + t) + 0.5 * x * (1.0 - t * t) * (_GK * (1.0 + 3.0 * _GA * x2))
    return y, dy


def _silu_grad(a):
    s = _sigmoid(a)
    return s * (1.0 + a * (1.0 - s))


def _place():
    return lax.axis_index("x"), lax.axis_index("y"), lax.axis_index("c")


def _other_chips(x, y):
    return [(1 - x, y), (x, 1 - y), (1 - x, 1 - y)]


def _visit_order(x, y, c):
    chips = _other_chips(x, y)
    return [(x, y, c), (x, y, 1 - c), (*chips[0], c), (*chips[1], c), (*chips[0], 1 - c), (*chips[1], 1 - c),
            (*chips[2], c), (*chips[2], 1 - c)]


def _gather_and_gate_up(xb, shards, relayed, order, name):
    n = len(shards)
    N_COPIES = 10
    t, d = xb.shape
    cols = shards[0].shape[1]
    tm = _tile(t, 1024, 128)
    ni = t // tm
    col_major = [True] + [False] * (n - 1)

    def body(order_ref, x_ref, *refs):
        srcs, gu_ref, xt_ref, dsts = refs[:n], refs[n], refs[n + 1], refs[n + 2:2 * n + 2]
        wbuf, send_sems, recv_sems, local_sems, load_sem = refs[2 * n + 2:]
        b, i = pl.program_id(0), pl.program_id(1)
        x, y, c = _place()
        me, sib = (x, y, c), (x, y, 1 - c)
        chips = _other_chips(x, y)

        near_x, near_y, far = chips

        def slot(w, p, band=None):
            half = shards[w].shape[0] // 2
            rows = None if band is None else (band * half, half)
            return _block_slot(dsts[w], col_major[w], shards[w].shape[1], p, rows)

        def copy(w, s, block, to, band=None, from_src=False):
            return pltpu.make_async_remote_copy(
                src_ref=srcs[w] if from_src else slot(w, block, band), dst_ref=slot(w, block, band),
                send_sem=send_sems.at[N_COPIES * w + s], recv_sem=recv_sems.at[N_COPIES * w + s],
                device_id=to, device_id_type=MESH)

        def own(w):
            return pltpu.make_async_copy(srcs[w], slot(w, me), local_sems.at[w])

        def sends(w):
            out = [copy(w, 0, me, sib, from_src=True), copy(w, 1, me, (*near_x, c), from_src=True),
                   copy(w, 2, me, (*near_y, c), from_src=True)]
            if not relayed[w]:
                out.append(copy(w, 3, me, (*far, c), from_src=True))
            return out

        def passed_on(w):
            out = [copy(w, 4, (*near_x, c), sib), copy(w, 5, (*near_y, c), sib)]
            if relayed[w]:
                out += [copy(w, 6, (*far, c), sib, band=0), copy(w, 9, (*far, c), sib, band=1),
                        copy(w, 7, (*near_x, c), (*near_y, c), band=0), copy(w, 8, (*near_y, c), (*near_x, c), band=1)]
            else:
                out.append(copy(w, 6, (*far, c), sib))
            return out

        def start_sends(w):
            own(w).start()
            for cp in sends(w):
                cp.start()

        def got_near_x(w):
            copy(w, 1, (*near_x, c), me).wait_recv()
            copy(w, 4, (*near_x, c), sib).start()
            if relayed[w]:
                copy(w, 7, (*near_x, c), (*near_y, c), band=0).start()

        def got_near_y(w):
            copy(w, 2, (*near_y, c), me).wait_recv()
            copy(w, 5, (*near_y, c), sib).start()
            if relayed[w]:
                copy(w, 8, (*near_y, c), (*near_x, c), band=1).start()

        def got_far(w):
            if relayed[w]:
                copy(w, 7, (*far, c), me, band=0).wait_recv()
                copy(w, 6, (*far, c), sib, band=0).start()
                copy(w, 8, (*far, c), me, band=1).wait_recv()
                copy(w, 9, (*far, c), sib, band=1).start()
            else:
                copy(w, 3, (*far, c), me).wait_recv()
                copy(w, 6, (*far, c), sib).start()

        def got_from_sibling(w, which):
            if which == 0:
                copy(w, 0, sib, me).wait_recv()
            elif which == 3 and relayed[w]:
                copy(w, 6, (*far, 1 - c), me, band=0).wait_recv()
                copy(w, 9, (*far, 1 - c), me, band=1).wait_recv()
            else:
                copy(w, 3 + which, (*chips[which - 1], 1 - c), me).wait_recv()

        others = range(1, n)

        def arrive(k):
            if k == 0:
                own(0).wait()
            elif k == 1:
                got_from_sibling(0, 0)
            elif k == 2:
                got_near_x(0)
                for w in others:
                    start_sends(w)
            elif k == 3:
                got_near_y(0)
            elif k in (4, 5):
                got_from_sibling(0, k - 3)
            elif k == 6:
                got_far(0)
                for w in others:
                    got_near_x(w)
                    got_near_y(w)
            else:
                got_from_sibling(0, 3)
                for w in others:
                    got_far(w)

        def load(k):
            at = pl.multiple_of(order_ref[k] * cols, 128)
            return pltpu.make_async_copy(dsts[0].at[:, pl.ds(at, cols)], wbuf.at[k % 2], load_sem.at[k % 2])

        @pl.when((b == 0) & (i == 0))
        def _():
            start_sends(0)
            arrive(0)
            load(0).start()
            load(0).wait()

        early = ni - 1
        for k in range(1, N_DEV):
            @pl.when((b == k - 1) & (i == early))
            def _(k=k):
                arrive(k)
                load(k).start()

            @pl.when((b == k) & (i == 0))
            def _(k=k):
                load(k).wait()

        gu_ref[...] = _dot(x_ref[...].astype(BF16), wbuf[b % 2]).astype(BF16)

        @pl.when(b == 0)
        def _():
            xt_ref[...] = x_ref[...].T.astype(BF16)

        @pl.when((b == N_DEV - 1) & (i == ni - 1))
        def _():
            for w in others:
                for which in range(4):
                    got_from_sibling(w, which)
                own(w).wait()
            for w in range(n):
                for cp in sends(w) + passed_on(w):
                    cp.wait_send()

    grid_spec = pltpu.PrefetchScalarGridSpec(
        num_scalar_prefetch=1, grid=(N_DEV, ni),
        in_specs=[pl.BlockSpec((tm, d), lambda b, i, o: (i, 0))] + [ANY] * n,
        out_specs=[pl.BlockSpec((tm, cols), lambda b, i, o: (i, o[b])),
                   pl.BlockSpec((d, tm), lambda b, i, o: (0, jnp.where(b == 0, i, ni - 1)))] + [ANY] * n,
        scratch_shapes=[pltpu.VMEM((2, d, cols), BF16), pltpu.SemaphoreType.DMA((N_COPIES * n,)),
                        pltpu.SemaphoreType.DMA((N_COPIES * n,)), pltpu.SemaphoreType.DMA((n,)),
                        pltpu.SemaphoreType.DMA((2,))])
    res = pl.pallas_call(
        body, name=name, grid_spec=grid_spec,
        out_shape=[jax.ShapeDtypeStruct((t, N_DEV * cols), BF16), jax.ShapeDtypeStruct((d, t), BF16)]
        + [_gathered_shape(s, cm) for s, cm in zip(shards, col_major)],
        compiler_params=_cparams("arbitrary", "arbitrary"),
    )(order, xb, *shards)
    return res[0], res[1], res[2:]


class _Exchange:
    def __init__(self, ins, io, new, n_sems, n_local, make):
        self.ins, self.io, self.new = list(ins), list(io), list(new)
        self.n_sems, self.n_local, self.make = n_sems, n_local, make


def _block_slot(ref, col_major, cols, place, rows=None):
    k = 4 * place[0] + 2 * place[1] + place[2]
    band = slice(None) if rows is None else pl.ds(rows[0], rows[1])
    if col_major:
        return ref.at[band, pl.ds(pl.multiple_of(k * cols, 128), cols)]
    return ref.at[k] if rows is None else ref.at[k, band]


def _gathered_shape(s, col_major):
    return jax.ShapeDtypeStruct((s.shape[0], N_DEV * s.shape[1]) if col_major else (N_DEV,) + s.shape, s.dtype)


def _gather_first(shards, col_major, rows=None, into=None):
    n = len(shards)
    new = [] if into is not None else [_gathered_shape(s, cm) for s, cm in zip(shards, col_major)]

    def make(in_refs, io_refs, new_refs, send_sems, recv_sems, local_sems, base=0, local_base=0):
        x, y, c = _place()
        targets = [(x, y, 1 - c)] + [(*chip, c) for chip in _other_chips(x, y)]
        gathered = io_refs if into is not None else new_refs
        copies = []
        for w in range(n):
            src = in_refs[w] if rows is None else in_refs[w].at[pl.ds(rows[0], rows[1])]
            slot = _block_slot(gathered[w], col_major[w], shards[w].shape[1], (x, y, c), rows)
            copies.append(pltpu.make_async_copy(src, slot, local_sems.at[local_base + w]))
            for s, to in enumerate(targets):
                copies.append(pltpu.make_async_remote_copy(
                    src_ref=src, dst_ref=slot, send_sem=send_sems.at[base + 4 * w + s],
                    recv_sem=recv_sems.at[base + 4 * w + s], device_id=to, device_id_type=MESH))
        return copies

    return _Exchange(shards, into or [], new, 4 * n, n, make)


def _gather_forward(gathered, col_major, cols, rows=None):
    n = len(gathered)

    def make(in_refs, io_refs, new_refs, send_sems, recv_sems, local_sems, base=0, local_base=0):
        x, y, c = _place()
        copies = []
        for w in range(n):
            for j, chip in enumerate(_other_chips(x, y)):
                slot = _block_slot(io_refs[w], col_major[w], cols[w], (*chip, c), rows)
                copies.append(pltpu.make_async_remote_copy(
                    src_ref=slot, dst_ref=slot, send_sem=send_sems.at[base + 3 * w + j],
                    recv_sem=recv_sems.at[base + 3 * w + j], device_id=(x, y, 1 - c), device_id_type=MESH))
        return copies

    return _Exchange([], gathered, [], 3 * n, 0, make)


def _both(a, b):
    def make(in_refs, io_refs, new_refs, send_sems, recv_sems, local_sems):
        na = len(a.ins)
        return (a.make(in_refs[:na], io_refs, [], send_sems, recv_sems, local_sems, 0, 0)
                + b.make(in_refs[na:], io_refs, [], send_sems, recv_sems, local_sems, a.n_sems, a.n_local))

    return _Exchange(a.ins + b.ins, a.io, [], a.n_sems + b.n_sems, a.n_local + b.n_local, make)


def _rs_sibling(parts):
    n = len(parts)

    def make(in_refs, io_refs, new_refs, send_sems, recv_sems, local_sems):
        x, y, c = _place()
        copies = []
        for w in range(n):
            for j in range(4):
                copies.append(pltpu.make_async_remote_copy(
                    src_ref=in_refs[w].at[2 * j + (1 - c)], dst_ref=new_refs[w].at[j],
                    send_sem=send_sems.at[4 * w + j], recv_sem=recv_sems.at[4 * w + j],
                    device_id=(x, y, 1 - c), device_id_type=MESH))
        return copies

    return _Exchange(parts, [], [jax.ShapeDtypeStruct((4,) + p.shape[1:], p.dtype) for p in parts], 4 * n, 0, make)


def _rs_chips(chip_parts, rows=None, into=None):
    n = len(chip_parts)
    band = slice(None) if rows is None else pl.ds(rows[0], rows[1])
    new = [] if into is not None else [jax.ShapeDtypeStruct((3,) + p.shape[1:], p.dtype) for p in chip_parts]

    def make(in_refs, io_refs, new_refs, send_sems, recv_sems, local_sems):
        x, y, c = _place()
        landing = io_refs if into is not None else new_refs
        copies = []
        for w in range(n):
            for rel, (px, py) in enumerate(_other_chips(x, y)):
                copies.append(pltpu.make_async_remote_copy(
                    src_ref=in_refs[w].at[2 * px + py, band], dst_ref=landing[w].at[rel, band],
                    send_sem=send_sems.at[3 * w + rel], recv_sem=recv_sems.at[3 * w + rel],
                    device_id=(px, py, c), device_id_type=MESH))
        return copies

    return _Exchange(chip_parts, into or [], new, 3 * n, 0, make)


class _Side:
    def __init__(self, ins, in_blocks, out_shapes, out_blocks, n_tiles, fn):
        self.ins, self.in_blocks, self.out_shapes, self.out_blocks = list(ins), in_blocks, list(out_shapes), out_blocks
        self.n_tiles, self.fn = n_tiles, fn


def _call(body, exch, *, name, grid, in_specs, out_specs, out_shape, scratch_shapes=(), semantics,
          input_output_aliases=None):
    exch = list(exch)
    in_specs, out_specs, out_shape = list(in_specs), list(out_specs), list(out_shape)
    scratch_shapes = list(scratch_shapes)
    if not exch:
        fn = pl.pallas_call(body, name=name, grid=grid, in_specs=in_specs, out_specs=out_specs, out_shape=out_shape,
                            scratch_shapes=scratch_shapes, input_output_aliases=input_output_aliases or {},
                            compiler_params=_cparams(*semantics))
        return lambda *args: (fn(*args), [])
    n_in, n_out, n_scr = len(in_specs), len(out_specs), len(scratch_shapes)
    aliases = dict(input_output_aliases or {})
    all_in, all_out_specs, all_out_shape, all_scr = list(in_specs), list(out_specs), list(out_shape), list(scratch_shapes)
    extra_args = []

    def step(idx):
        s = idx[0]
        for a in range(1, len(grid)):
            s = s * grid[a] + idx[a]
        return s

    def tile_spec(shape, where, n_tiles):
        return pl.BlockSpec(shape, lambda *idx: where(jnp.minimum(step(idx), n_tiles - 1)))

    for ex in exch:
        if isinstance(ex, _Side):
            all_in += [tile_spec(shape, where, ex.n_tiles) for shape, where in ex.in_blocks]
            extra_args += ex.ins
            all_out_specs += [tile_spec(shape, where, ex.n_tiles) for shape, where in ex.out_blocks]
            all_out_shape += ex.out_shapes
            continue
        for k, a in enumerate(ex.io):
            aliases[len(all_in) + len(ex.ins) + k] = len(all_out_specs) + k
        all_in += [ANY] * (len(ex.ins) + len(ex.io))
        extra_args += ex.ins + ex.io
        all_out_specs += [ANY] * (len(ex.io) + len(ex.new))
        all_out_shape += [jax.ShapeDtypeStruct(a.shape, a.dtype) for a in ex.io] + ex.new
        all_scr += [pltpu.SemaphoreType.DMA((ex.n_sems,)), pltpu.SemaphoreType.DMA((ex.n_sems,)),
                    pltpu.SemaphoreType.DMA((max(ex.n_local, 1),))]

    n_ins = [len(ex.ins) if isinstance(ex, _Side) else len(ex.ins) + len(ex.io) for ex in exch]
    n_outs = [len(ex.out_shapes) if isinstance(ex, _Side) else len(ex.io) + len(ex.new) for ex in exch]

    def wrapped(*refs):
        pos = n_in
        ex_in = []
        for k in n_ins:
            ex_in.append(refs[pos:pos + k])
            pos += k
        outs = refs[pos:pos + n_out]
        pos += n_out
        ex_out = []
        for k in n_outs:
            ex_out.append(refs[pos:pos + k])
            pos += k
        scr = refs[pos:pos + n_scr]
        pos += n_scr
        idx = [pl.program_id(a) for a in range(len(grid))]
        first = functools.reduce(jnp.logical_and, [i == 0 for i in idx])
        last = functools.reduce(jnp.logical_and, [i == g - 1 for i, g in zip(idx, grid)])

        def copies():
            out, at = [], pos
            for ex, ei, eo in zip(exch, ex_in, ex_out):
                if not isinstance(ex, _Side):
                    out += ex.make(ei[:len(ex.ins)], eo[:len(ex.io)], eo[len(ex.io):], *refs[at:at + 3])
                    at += 3
            return out

        @pl.when(first)
        def _():
            for cp in copies():
                cp.start()

        body(*refs[:n_in], *outs, *scr)
        for ex, ei, eo in zip(exch, ex_in, ex_out):
            if isinstance(ex, _Side):
                pl.when(step(idx) < ex.n_tiles)(functools.partial(ex.fn, ei, eo))

        @pl.when(last)
        def _():
            for cp in copies():
                cp.wait()

    fn = pl.pallas_call(wrapped, name=name, grid=grid, in_specs=all_in, out_specs=all_out_specs,
                        out_shape=all_out_shape, scratch_shapes=all_scr, input_output_aliases=aliases,
                        compiler_params=_cparams(*(["arbitrary"] * len(grid))))

    def run(*args):
        res = fn(*args, *extra_args)
        outs, pos, ex_res = res[:n_out], n_out, []
        for k in n_outs:
            ex_res.append(list(res[pos:pos + k]))
            pos += k
        return outs, ex_res

    return run


def _exchange_alone(ex, name):
    def body():
        pass

    _, res = _call(body, [ex], name=name, grid=(1,), in_specs=[], out_specs=[], out_shape=[], semantics=("arbitrary",))()
    return res[0]


def _small_gather(part):
    def make(in_refs, io_refs, new_refs, send_sems, recv_sems, local_sems):
        x, y, c = _place()
        slot = new_refs[0].at[4 * x + 2 * y + c]
        copies = [pltpu.make_async_copy(in_refs[0], slot, local_sems.at[0])]
        for d in range(1, N_DEV):
            peer = (1 - x if d & 4 else x, 1 - y if d & 2 else y, 1 - c if d & 1 else c)
            copies.append(pltpu.make_async_remote_copy(
                src_ref=in_refs[0], dst_ref=slot, send_sem=send_sems.at[d - 1], recv_sem=recv_sems.at[d - 1],
                device_id=peer, device_id_type=MESH))
        return copies

    return _Exchange([part], [], [jax.ShapeDtypeStruct((N_DEV,) + part.shape, part.dtype)], N_DEV - 1, 1, make)


def _sum_over_devices(parts):
    _, rows, lanes = parts.shape

    def body(p_ref, o_ref):
        acc = p_ref[0]
        for k in range(1, N_DEV):
            acc = acc + p_ref[k]
        o_ref[...] = acc

    return pl.pallas_call(
        body, name="small_grads_sum", grid=(1,), out_shape=jax.ShapeDtypeStruct((rows, lanes), F32),
        in_specs=[pl.BlockSpec((N_DEV, rows, lanes), lambda i: (0, 0, 0))],
        out_specs=pl.BlockSpec((rows, lanes), lambda i: (0, 0)),
        compiler_params=_cparams("arbitrary"),
    )(parts)


def _transpose_bf16(a, name, exch=(), with_copy=False):
    r, c = a.shape
    tr, tc = _tile(r, 512, 128), _tile(c, 512, 128)

    def body(a_ref, o_ref, *copy_ref):
        v = a_ref[...].astype(F32)
        o_ref[...] = v.T.astype(BF16)
        if with_copy:
            copy_ref[0][...] = v.astype(BF16)

    outs, ex = _call(
        body, exch, name=name, grid=(r // tr, c // tc),
        out_shape=[jax.ShapeDtypeStruct((c, r), BF16)] + [jax.ShapeDtypeStruct((r, c), BF16)] * with_copy,
        in_specs=[pl.BlockSpec((tr, tc), lambda i, j: (i, j))],
        out_specs=[pl.BlockSpec((tc, tr), lambda i, j: (j, i))] + [pl.BlockSpec((tr, tc), lambda i, j: (i, j))] * with_copy,
        semantics=("parallel", "parallel"),
    )(a)
    return (outs if with_copy else outs[0]), ex


def _ffn_fwd_loss(x, wgu, wd, ln_g, ln_b, target, name, exch=()):
    t, d = x.shape
    f = wd.shape[0]
    tm, tf = _tile(t, 512, 128), _tile(f, 512, 128)
    nf = f // tf

    def body(x_ref, wg_ref, wu_ref, wd_ref, lg_ref, lb_ref, t_ref,
             go_ref, uo_ref, ht_ref, dz_ref, dzb_ref, dlg_ref, dlb_ref, loss_ref, xb, acc):
        i, j = pl.program_id(0), pl.program_id(1)

        @pl.when(j == 0)
        def _():
            xb[...] = x_ref[...].astype(BF16)
            acc[...] = jnp.zeros_like(acc)

        @pl.when((i == 0) & (j == 0))
        def _():
            dlg_ref[...] = jnp.zeros_like(dlg_ref)
            dlb_ref[...] = jnp.zeros_like(dlb_ref)
            loss_ref[...] = jnp.zeros_like(loss_ref)

        g = _dot(xb[...], wg_ref[...])
        u = _dot(xb[...], wu_ref[...])
        h = g * _sigmoid(g) * u
        go_ref[...] = g.astype(BF16)
        uo_ref[...] = u.astype(BF16)
        ht_ref[...] = h.T.astype(BF16)
        acc[...] += _dot(h.astype(BF16), wd_ref[...])

        @pl.when(j == nf - 1)
        def _():
            for r in range(0, tm, EPILOGUE_ROWS):
                rows = slice(r, r + EPILOGUE_ROWS)
                xh, rstd = _ln_stats(ALPHA * x_ref[rows, :] + 0.5 * acc[rows, :])
                e = xh * lg_ref[...] + lb_ref[...] - t_ref[rows, :]
                loss_ref[...] += 0.5 * jnp.sum(jnp.sum(e * e, axis=-1, keepdims=True) * (1.0 / d), axis=0,
                                               keepdims=True)
                dy = e * (1.0 / d)
                dz = _ln_bwd(dy * lg_ref[...], xh, rstd)
                dz_ref[rows, :] = dz
                dzb_ref[rows, :] = (0.5 * dz).astype(BF16)
                dlg_ref[...] += jnp.sum(dy * xh, axis=0, keepdims=True)
                dlb_ref[...] += jnp.sum(dy, axis=0, keepdims=True)

    row = lambda i, j: (i, 0)
    fixed = lambda i, j: (0, 0)
    return _call(
        body, exch, name=name, grid=(t // tm, nf),
        out_shape=[jax.ShapeDtypeStruct((t, f), BF16), jax.ShapeDtypeStruct((t, f), BF16),
                   jax.ShapeDtypeStruct((f, t), BF16), jax.ShapeDtypeStruct((t, d), F32),
                   jax.ShapeDtypeStruct((t, d), BF16), jax.ShapeDtypeStruct((1, d), F32),
                   jax.ShapeDtypeStruct((1, d), F32), jax.ShapeDtypeStruct((8, 128), F32)],
        in_specs=[pl.BlockSpec((tm, d), row),
                  pl.BlockSpec((d, tf), lambda i, j: (0, j)),
                  pl.BlockSpec((d, tf), lambda i, j: (0, j + nf)),
                  pl.BlockSpec((tf, d), lambda i, j: (j, 0)),
                  pl.BlockSpec((1, d), fixed), pl.BlockSpec((1, d), fixed), pl.BlockSpec((tm, d), row)],
        out_specs=[pl.BlockSpec((tm, tf), lambda i, j: (i, j)), pl.BlockSpec((tm, tf), lambda i, j: (i, j)),
                   pl.BlockSpec((tf, tm), lambda i, j: (j, i)), pl.BlockSpec((tm, d), row), pl.BlockSpec((tm, d), row),
                   pl.BlockSpec((1, d), fixed), pl.BlockSpec((1, d), fixed), pl.BlockSpec((8, 128), fixed)],
        scratch_shapes=[pltpu.VMEM((tm, d), BF16), pltpu.VMEM((tm, d), F32)],
        semantics=("arbitrary", "arbitrary"),
    )(x, wgu, wgu, wd, ln_g, ln_b, target)


def _ffn_down_fwd(gu, x, wd, ln_g, ln_b, name, exch=()):
    t, d = x.shape
    f = wd.shape[0]
    tm, tf = _tile(t, 512, 128), _tile(f, 512, 128)
    nf = f // tf

    def body(g_ref, u_ref, wd_ref, x_ref, lg_ref, lb_ref, ht_ref, z_ref, xn_ref, acc):
        j = pl.program_id(1)

        @pl.when(j == 0)
        def _():
            acc[...] = jnp.zeros_like(acc)

        g = g_ref[...].astype(F32)
        h = g * _sigmoid(g) * u_ref[...].astype(F32)
        ht_ref[...] = h.T.astype(BF16)
        acc[...] += _dot(h.astype(BF16), wd_ref[...])

        @pl.when(j == nf - 1)
        def _():
            z = ALPHA * x_ref[...] + 0.5 * acc[...]
            z_ref[...] = z
            xn_ref[...] = _ln(z, lg_ref[...], lb_ref[...])

    row = lambda i, j: (i, 0)
    fixed = lambda i, j: (0, 0)
    return _call(
        body, exch, name=name, grid=(t // tm, nf),
        out_shape=[jax.ShapeDtypeStruct((f, t), BF16), jax.ShapeDtypeStruct((t, d), F32),
                   jax.ShapeDtypeStruct((t, d), F32)],
        in_specs=[pl.BlockSpec((tm, tf), lambda i, j: (i, j)), pl.BlockSpec((tm, tf), lambda i, j: (i, j + nf)),
                  pl.BlockSpec((tf, d), lambda i, j: (j, 0)), pl.BlockSpec((tm, d), row),
                  pl.BlockSpec((1, d), fixed), pl.BlockSpec((1, d), fixed)],
        out_specs=[pl.BlockSpec((tf, tm), lambda i, j: (j, i)), pl.BlockSpec((tm, d), row), pl.BlockSpec((tm, d), row)],
        scratch_shapes=[pltpu.VMEM((tm, d), F32)],
        semantics=("parallel", "arbitrary"),
    )(gu, gu, wd, x, ln_g, ln_b)


def _ffn_act_grads(dh, g_ref, u_ref):
    gg = g_ref[...].astype(F32)
    uu = u_ref[...].astype(F32)
    s = _sigmoid(gg)
    du = (dh * (gg * s)).astype(BF16)
    dg = (dh * uu * (s * (1.0 + gg * (1.0 - s)))).astype(BF16)
    return dg, du


def _ffn_bwd(dz, do, g, u, wgu, wd, name, exch=()):
    t, d = dz.shape
    f = wd.shape[0]
    tm, tf = _tile(t, 512, 128), _tile(f, 512, 128)
    nf = f // tf

    def body(dz_ref, do_ref, g_ref, u_ref, wg_ref, wu_ref, wd_ref, dg_ref, du_ref, dx_ref, acc):
        j = pl.program_id(1)

        @pl.when(j == 0)
        def _():
            acc[...] = jnp.zeros_like(acc)

        dg, du = _ffn_act_grads(_dot_nt(do_ref[...], wd_ref[...]), g_ref, u_ref)
        dg_ref[...] = dg
        du_ref[...] = du
        acc[...] += _dot_nt(dg, wg_ref[...]) + _dot_nt(du, wu_ref[...])

        @pl.when(j == nf - 1)
        def _():
            dx_ref[...] = ALPHA * dz_ref[...] + acc[...]

    row = lambda i, j: (i, 0)
    tile = lambda i, j: (i, j)
    return _call(
        body, exch, name=name, grid=(t // tm, nf),
        out_shape=[jax.ShapeDtypeStruct((t, f), BF16), jax.ShapeDtypeStruct((t, f), BF16),
                   jax.ShapeDtypeStruct((t, d), F32)],
        in_specs=[pl.BlockSpec((tm, d), row), pl.BlockSpec((tm, d), row),
                  pl.BlockSpec((tm, tf), tile), pl.BlockSpec((tm, tf), tile),
                  pl.BlockSpec((d, tf), lambda i, j: (0, j)),
                  pl.BlockSpec((d, tf), lambda i, j: (0, j + nf)),
                  pl.BlockSpec((tf, d), lambda i, j: (j, 0))],
        out_specs=[pl.BlockSpec((tm, tf), tile), pl.BlockSpec((tm, tf), tile), pl.BlockSpec((tm, d), row)],
        scratch_shapes=[pltpu.VMEM((tm, d), F32)],
        semantics=("parallel", "arbitrary"),
    )(dz, do, g, u, wgu, wgu, wd)


def _ffn_bwd_act(do, gu, wd, name, exch=()):
    t, d = do.shape
    f = wd.shape[0]
    tm, tf = _tile(t, 2048, 128), _tile(f, 512, 128)
    nf = f // tf

    def body(do_ref, g_ref, u_ref, wd_ref, dg_ref, du_ref):
        dg, du = _ffn_act_grads(_dot_nt(do_ref[...], wd_ref[...]), g_ref, u_ref)
        dg_ref[...] = dg
        du_ref[...] = du

    tile = lambda i, j: (i, j)
    return _call(
        body, exch, name=name, grid=(t // tm, f // tf),
        out_shape=[jax.ShapeDtypeStruct((t, f), BF16), jax.ShapeDtypeStruct((t, f), BF16)],
        in_specs=[pl.BlockSpec((tm, d), lambda i, j: (i, 0)), pl.BlockSpec((tm, tf), tile),
                  pl.BlockSpec((tm, tf), lambda i, j: (i, j + nf)), pl.BlockSpec((tf, d), lambda i, j: (j, 0))],
        out_specs=[pl.BlockSpec((tm, tf), tile), pl.BlockSpec((tm, tf), tile)],
        semantics=("parallel", "parallel"),
    )(do, gu, gu, wd)


def _ffn_bwd_dx(dz, dg, du, wgu, name, exch=()):
    t, d = dz.shape
    f = dg.shape[1]
    tm, tn = _tile(t, 512, 128), _tile(d, 256, 128)

    def body(dz_ref, dg_ref, du_ref, wg_ref, wu_ref, dx_ref):
        dx_ref[...] = ALPHA * dz_ref[...] + _dot_nt(dg_ref[...], wg_ref[...]) + _dot_nt(du_ref[...], wu_ref[...])

    row = lambda i, n: (i, 0)
    tile = lambda i, n: (i, n)
    return _call(
        body, exch, name=name, grid=(t // tm, d // tn), out_shape=[jax.ShapeDtypeStruct((t, d), F32)],
        in_specs=[pl.BlockSpec((tm, tn), tile), pl.BlockSpec((tm, f), row), pl.BlockSpec((tm, f), row),
                  pl.BlockSpec((tn, f), lambda i, n: (n, 0)), pl.BlockSpec((tn, f), lambda i, n: (n, 1))],
        out_specs=[pl.BlockSpec((tm, tn), tile)],
        semantics=("parallel", "arbitrary"),
    )(dz, dg, du, wgu, wgu)


def _weight_grad(at, b, tn, tmm, name, blocks=None, block_offset=0, into=None, exch=()):
    m, t = at.shape
    nn = b.shape[1]
    tmm = _tile(m, tmm, 16)
    assert nn % tn == 0

    def body(*refs):
        at_ref, b_ref, o_ref = refs[0], refs[1], refs[-1]
        r = _dot(at_ref[...], b_ref[...]).astype(BF16)
        if blocks is None:
            o_ref[...] = r
        else:
            o_ref[0] = r

    in_specs = [pl.BlockSpec((tmm, t), lambda n, i: (i, 0)), pl.BlockSpec((t, tn), lambda n, i: (0, n))]
    args = [at, b]
    aliases = {}
    if into is not None:
        in_specs.append(ANY)
        args.append(into)
        aliases = {2: 0}
    if blocks is None:
        out_shape = jax.ShapeDtypeStruct((m, nn), BF16)
        out_spec = pl.BlockSpec((tmm, tn), lambda n, i: (i, n))
    else:
        out_shape = jax.ShapeDtypeStruct((blocks, m, tn), BF16)
        out_spec = pl.BlockSpec((1, tmm, tn), lambda n, i: (n + block_offset, i, 0))
    (out,), ex = _call(
        body, exch, name=name, grid=(nn // tn, m // tmm), out_shape=[out_shape],
        in_specs=in_specs, out_specs=[out_spec], input_output_aliases=aliases,
        semantics=("parallel", "parallel"),
    )(*args)
    return out, ex


def _mix_in_proj(x, w_in, name, exch=()):
    t, d = x.shape
    n_out = w_in.shape[1]
    tm, cb = _tile(t, 512, 128), _tile(n_out, 1024, 128)

    def body(x_ref, w_ref, o_ref, xb):
        @pl.when(pl.program_id(1) == 0)
        def _():
            xb[...] = x_ref[...].astype(BF16)

        o_ref[...] = _dot(xb[...], w_ref[...])

    (out,), ex = _call(
        body, exch, name=name, grid=(t // tm, n_out // cb), out_shape=[jax.ShapeDtypeStruct((t, n_out), F32)],
        in_specs=[pl.BlockSpec((tm, d), lambda i, k: (i, 0)), pl.BlockSpec((d, cb), lambda i, k: (0, k))],
        out_specs=[pl.BlockSpec((tm, cb), lambda i, k: (i, k))],
        scratch_shapes=[pltpu.VMEM((tm, d), BF16)],
        semantics=("parallel", "arbitrary"),
    )(x, w_in)
    return out, ex


def _mix_in_bwd(dproj, w_in, dz, name, exch=()):
    t, d = dz.shape
    kk = w_in.shape[1]
    tm, tn = _tile(t, 512, 128), _tile(d, 512, 128)

    def body(dp_ref, w_ref, dz_ref, dx_ref):
        dx_ref[...] = ALPHA * dz_ref[...] + _dot_nt(dp_ref[...], w_ref[...])

    (out,), ex = _call(
        body, exch, name=name, grid=(t // tm, d // tn), out_shape=[jax.ShapeDtypeStruct((t, d), F32)],
        in_specs=[pl.BlockSpec((tm, kk), lambda i, n: (i, 0)), pl.BlockSpec((tn, kk), lambda i, n: (n, 0)),
                  pl.BlockSpec((tm, tn), lambda i, n: (i, n))],
        out_specs=[pl.BlockSpec((tm, tn), lambda i, n: (i, n))],
        semantics=("parallel", "arbitrary"),
    )(dproj, w_in, dz)
    return out, ex


def _mix_out_fwd(y, w_out, x, ln_g, ln_b, name, exch=()):
    t, d = x.shape
    kk = y.shape[1]
    tm = _tile(t, 256, 128)

    def body(y_ref, w_ref, x_ref, g_ref, b_ref, z_ref, xn_ref, xnt_ref):
        z = ALPHA * x_ref[...] + _dot(y_ref[...], w_ref[...])
        z_ref[...] = z
        xn = _ln(z, g_ref[...], b_ref[...])
        xn_ref[...] = xn
        xnt_ref[...] = xn.T.astype(BF16)

    row = lambda i: (i, 0)
    fixed = lambda i: (0, 0)
    return _call(
        body, exch, name=name, grid=(t // tm,),
        out_shape=[jax.ShapeDtypeStruct((t, d), F32), jax.ShapeDtypeStruct((t, d), F32),
                   jax.ShapeDtypeStruct((d, t), BF16)],
        in_specs=[pl.BlockSpec((tm, kk), row), pl.BlockSpec((kk, d), fixed), pl.BlockSpec((tm, d), row),
                  pl.BlockSpec((1, d), fixed), pl.BlockSpec((1, d), fixed)],
        out_specs=[pl.BlockSpec((tm, d), row), pl.BlockSpec((tm, d), row), pl.BlockSpec((d, tm), lambda i: (0, i))],
        semantics=("parallel",),
    )(y, w_out, x, ln_g, ln_b)


def _mix_out_bwd(dzb, w_out, name, exch=()):
    t, d = dzb.shape
    kk = w_out.shape[0]
    tm = _tile(t, 512, 128)

    def body(dz_ref, w_ref, dy_ref):
        dy_ref[...] = _dot_nt(dz_ref[...], w_ref[...])

    (out,), ex = _call(
        body, exch, name=name, grid=(t // tm,), out_shape=[jax.ShapeDtypeStruct((t, kk), F32)],
        in_specs=[pl.BlockSpec((tm, d), lambda i: (i, 0)), pl.BlockSpec((kk, d), lambda i: (0, 0))],
        out_specs=[pl.BlockSpec((tm, kk), lambda i: (i, 0))],
        semantics=("parallel",),
    )(dzb, w_out)
    return out, ex


def _loss_ln_bwd(z, target, ln_g, ln_b, bf16_scale, name):
    t, d = z.shape
    tm = _tile(t, 512, 8)

    def body(z_ref, t_ref, g_ref, b_ref, dz_ref, dzb_ref, dg_ref, db_ref, loss_ref):
        @pl.when(pl.program_id(0) == 0)
        def _():
            dg_ref[...] = jnp.zeros_like(dg_ref)
            db_ref[...] = jnp.zeros_like(db_ref)
            loss_ref[...] = jnp.zeros_like(loss_ref)

        xh, rstd = _ln_stats(z_ref[...])
        e = xh * g_ref[...] + b_ref[...] - t_ref[...]
        loss_ref[...] += 0.5 * jnp.sum(jnp.sum(e * e, axis=-1, keepdims=True) * (1.0 / d), axis=0, keepdims=True)
        dy = e * (1.0 / d)
        dz = _ln_bwd(dy * g_ref[...], xh, rstd)
        dz_ref[...] = dz
        dzb_ref[...] = (bf16_scale * dz).astype(BF16)
        dg_ref[...] += jnp.sum(dy * xh, axis=0, keepdims=True)
        db_ref[...] += jnp.sum(dy, axis=0, keepdims=True)

    row = lambda i: (i, 0)
    fixed = lambda i: (0, 0)
    return pl.pallas_call(
        body, name=name, grid=(t // tm,),
        out_shape=[jax.ShapeDtypeStruct((t, d), F32), jax.ShapeDtypeStruct((t, d), BF16),
                   jax.ShapeDtypeStruct((1, d), F32), jax.ShapeDtypeStruct((1, d), F32),
                   jax.ShapeDtypeStruct((8, 128), F32)],
        in_specs=[pl.BlockSpec((tm, d), row), pl.BlockSpec((tm, d), row), pl.BlockSpec((1, d), fixed),
                  pl.BlockSpec((1, d), fixed)],
        out_specs=[pl.BlockSpec((tm, d), row), pl.BlockSpec((tm, d), row), pl.BlockSpec((1, d), fixed),
                   pl.BlockSpec((1, d), fixed), pl.BlockSpec((8, 128), fixed)],
        compiler_params=_cparams("arbitrary"),
    )(z, target, ln_g, ln_b)


def _ln_bwd_call(z, dy, ln_g, bf16_scale, name, exch=()):
    t, d = z.shape
    tm = _tile(t, 512, 8)

    def body(z_ref, dy_ref, g_ref, dz_ref, dzb_ref, dg_ref, db_ref):
        @pl.when(pl.program_id(0) == 0)
        def _():
            dg_ref[...] = jnp.zeros_like(dg_ref)
            db_ref[...] = jnp.zeros_like(db_ref)

        xh, rstd = _ln_stats(z_ref[...])
        dy = dy_ref[...]
        dz = _ln_bwd(dy * g_ref[...], xh, rstd)
        dz_ref[...] = dz
        dzb_ref[...] = (bf16_scale * dz).astype(BF16)
        dg_ref[...] += jnp.sum(dy * xh, axis=0, keepdims=True)
        db_ref[...] += jnp.sum(dy, axis=0, keepdims=True)

    row = lambda i: (i, 0)
    fixed = lambda i: (0, 0)
    return _call(
        body, exch, name=name, grid=(t // tm,),
        out_shape=[jax.ShapeDtypeStruct((t, d), F32), jax.ShapeDtypeStruct((t, d), BF16),
                   jax.ShapeDtypeStruct((1, d), F32), jax.ShapeDtypeStruct((1, d), F32)],
        in_specs=[pl.BlockSpec((tm, d), row), pl.BlockSpec((tm, d), row), pl.BlockSpec((1, d), fixed)],
        out_specs=[pl.BlockSpec((tm, d), row), pl.BlockSpec((tm, d), row), pl.BlockSpec((1, d), fixed),
                   pl.BlockSpec((1, d), fixed)],
        semantics=("arbitrary",),
    )(z, dy, ln_g)


CONV_ROWS = 32
CONV_LANES = 512
SUBLANES = 8


def _fill_shifted(ext, shifted):
    rows = ext.shape[0] - SUBLANES
    for s in range(1, SUBLANES):
        for r in range(0, rows, CONV_ROWS):
            n = min(CONV_ROWS, rows - r)
            shifted[s - 1, r:r + n, :] = ext[r + s:r + s + n, :]


def _window(ext, shifted, lo, n, lanes=slice(None)):
    s = lo % SUBLANES
    return ext[lo:lo + n, lanes] if s == 0 else shifted[s - 1, lo - s:lo - s + n, lanes]


def _mixer_fwd(proj, conv_w, conv_b, cln_g, cln_b, sln_g, sln_b, sg_wm, sg_bb, name, exch=()):
    t = proj.shape[0]
    tm = _tile(t, 256, CHUNK)
    hb = tm // HALO
    nc = tm // CHUNK
    ch = CONV_CH

    def body(av_ref, ag_ref, bu_ref, bv_ref, hv_ref, hg_ref, cw_ref, cb_ref, lg_ref, lb_ref, sg_ref, sb_ref,
             w_ref, bb_ref, y_ref, yt_ref, c_ref, ext, ext_s):
        i = pl.program_id(0)
        halo = hv_ref[...] * _sigmoid(hg_ref[...])
        ext[0:HALO, :] = jnp.where(i > 0, halo, 0.0)
        ext[HALO:HALO + tm, :] = av_ref[...] * _sigmoid(ag_ref[...])
        _fill_shifted(ext, ext_s)
        for r in range(0, tm, CONV_ROWS):
            acc = jnp.zeros((CONV_ROWS, ch), F32) + cb_ref[...]
            for k in range(CONV_TAPS):
                lo = r + k + HALO - (CONV_TAPS - 1)
                acc = acc + cw_ref[k:k + 1, :] * _window(ext, ext_s, lo, CONV_ROWS)
            c_ref[r:r + CONV_ROWS, :] = acc
        a = _ln(c_ref[...], lg_ref[...], lb_ref[...])
        ya = a * _sigmoid(a)
        y_ref[:, 0:ch] = ya.astype(BF16)
        yt_ref[0:ch, :] = ya.T.astype(BF16)
        for h in range(HEADS):
            sl = slice(h * HEAD_DIM, (h + 1) * HEAD_DIM)
            u, _ = _gelu_and_grad(bu_ref[:, sl])
            v, _ = _gelu_and_grad(bv_ref[:, sl])
            vn = _ln(v, sg_ref[h:h + 1, :], sb_ref[h:h + 1, :])
            vn3 = vn.astype(BF16).reshape(nc, CHUNK, HEAD_DIM)
            wb = jnp.broadcast_to(w_ref[h][None], (nc, CHUNK, CHUNK))
            mixed = jnp.einsum("cts,csd->ctd", wb, vn3, preferred_element_type=F32) + bb_ref[h][None]
            yb = u * mixed.reshape(tm, HEAD_DIM)
            y_ref[:, ch + h * HEAD_DIM:ch + (h + 1) * HEAD_DIM] = yb.astype(BF16)
            yt_ref[ch + h * HEAD_DIM:ch + (h + 1) * HEAD_DIM, :] = yb.T.astype(BF16)

    col = lambda cidx: (lambda i: (i, cidx))
    prev = lambda cidx: (lambda i: (jnp.maximum(i * hb - 1, 0), cidx))
    fix2 = lambda i: (0, 0)
    fix3 = lambda i: (0, 0, 0)
    return _call(
        body, exch, name=name, grid=(t // tm,),
        out_shape=[jax.ShapeDtypeStruct((t, 2 * ch), BF16), jax.ShapeDtypeStruct((2 * ch, t), BF16),
                   jax.ShapeDtypeStruct((t, ch), F32)],
        in_specs=[pl.BlockSpec((tm, ch), col(0)), pl.BlockSpec((tm, ch), col(1)), pl.BlockSpec((tm, ch), col(2)),
                  pl.BlockSpec((tm, ch), col(3)), pl.BlockSpec((HALO, ch), prev(0)), pl.BlockSpec((HALO, ch), prev(1)),
                  pl.BlockSpec((CONV_TAPS, ch), fix2), pl.BlockSpec((1, ch), fix2), pl.BlockSpec((1, ch), fix2),
                  pl.BlockSpec((1, ch), fix2), pl.BlockSpec((HEADS, HEAD_DIM), fix2), pl.BlockSpec((HEADS, HEAD_DIM), fix2),
                  pl.BlockSpec((HEADS, CHUNK, CHUNK), fix3), pl.BlockSpec((HEADS, CHUNK, HEAD_DIM), fix3)],
        out_specs=[pl.BlockSpec((tm, 2 * ch), lambda i: (i, 0)), pl.BlockSpec((2 * ch, tm), lambda i: (0, i)),
                   pl.BlockSpec((tm, ch), lambda i: (i, 0))],
        scratch_shapes=[pltpu.VMEM((HALO + tm, ch), F32), pltpu.VMEM((SUBLANES - 1, HALO + tm, ch), F32)],
        semantics=("parallel",),
    )(proj, proj, proj, proj, proj, proj, conv_w, conv_b, cln_g, cln_b, sln_g, sln_b, sg_wm, sg_bb)


def _mixer_bwd(proj, conv_c, dy, conv_w, cln_g, cln_b, sln_g, sln_b, sg_wm, sg_wmt, sg_bb, name, exch=()):
    t = proj.shape[0]
    tm = _tile(t, 256, CHUNK)
    hb = tm // HALO
    nc = tm // CHUNK
    nt = t // tm
    ch = CONV_CH
    last_halo = t // HALO - 1

    def body(av_ref, ag_ref, bu_ref, bv_ref, hv_ref, hg_ref, c_ref, cn_ref, dya_ref, dyan_ref, dyb_ref,
             cw_ref, lg_ref, lb_ref, sg_ref, sb_ref, w_ref, wt_ref, bb_ref,
             dp_ref, dcw_ref, dcb_ref, dlg_ref, dlb_ref, dsg_ref, dsb_ref, dw_ref, dbs_ref,
             ext_h, ext_dc, ext_hs, ext_dcs, acc_cw):
        i = pl.program_id(0)

        @pl.when(i == 0)
        def _():
            acc_cw[...] = jnp.zeros_like(acc_cw)
            for ref in (dcb_ref, dlg_ref, dlb_ref, dsg_ref, dsb_ref, dw_ref, dbs_ref):
                ref[...] = jnp.zeros_like(ref)

        lg = lg_ref[...]
        lb = lb_ref[...]

        def conv_ln_bwd(c, dya):
            xh, rstd = _ln_stats(c)
            a = xh * lg + lb
            da = dya * _silu_grad(a)
            return _ln_bwd(da * lg, xh, rstd), da, xh

        fold = lambda v: jnp.sum(v.reshape(CONV_ROWS // SUBLANES, SUBLANES, ch), axis=0)
        s_lg = s_lb = s_cb = jnp.zeros((SUBLANES, ch), F32)
        for r in range(0, tm, CONV_ROWS):
            dc, da, xh = conv_ln_bwd(c_ref[r:r + CONV_ROWS, :], dya_ref[r:r + CONV_ROWS, :])
            ext_dc[r:r + CONV_ROWS, :] = dc
            s_lg, s_lb, s_cb = s_lg + fold(da * xh), s_lb + fold(da), s_cb + fold(dc)
        dlg_ref[...] += jnp.sum(s_lg, axis=0, keepdims=True)
        dlb_ref[...] += jnp.sum(s_lb, axis=0, keepdims=True)
        dcb_ref[...] += jnp.sum(s_cb, axis=0, keepdims=True)
        dcn, _, _ = conv_ln_bwd(cn_ref[...], dyan_ref[...])
        ext_dc[tm:tm + HALO, :] = jnp.where(i < nt - 1, dcn, 0.0)
        halo = hv_ref[...] * _sigmoid(hg_ref[...])
        ext_h[0:HALO, :] = jnp.where(i > 0, halo, 0.0)
        ext_h[HALO:HALO + tm, :] = av_ref[...] * _sigmoid(ag_ref[...])
        _fill_shifted(ext_h, ext_hs)
        _fill_shifted(ext_dc, ext_dcs)
        for r, c0 in [(r, c0) for r in range(0, tm, CONV_ROWS) for c0 in range(0, ch, CONV_LANES)]:
            rows, lanes = slice(r, r + CONV_ROWS), slice(c0, c0 + CONV_LANES)
            dcr = ext_dc[rows, lanes]
            acc = jnp.zeros((CONV_ROWS, CONV_LANES), F32)
            for k in range(CONV_TAPS):
                lo = r + k + HALO - (CONV_TAPS - 1)
                prod = dcr * _window(ext_h, ext_hs, lo, CONV_ROWS, lanes)
                acc_cw[k, :, lanes] += jnp.sum(prod.reshape(CONV_ROWS // SUBLANES, SUBLANES, CONV_LANES), axis=0)
                hi = r + (CONV_TAPS - 1) - k
                acc = acc + cw_ref[k:k + 1, lanes] * _window(ext_dc, ext_dcs, hi, CONV_ROWS, lanes)
            sg_r = _sigmoid(ag_ref[rows, lanes])
            av_r = av_ref[rows, lanes]
            dp_ref[rows, lanes] = (acc * sg_r).astype(BF16)
            dp_ref[rows, slice(ch + c0, ch + c0 + CONV_LANES)] = (acc * av_r * sg_r * (1.0 - sg_r)).astype(BF16)

        @pl.when(i == nt - 1)
        def _():
            dcw_ref[...] = jnp.sum(acc_cw[...], axis=1)

        tril = (lax.broadcasted_iota(jnp.int32, (CHUNK, CHUNK), 0)
                >= lax.broadcasted_iota(jnp.int32, (CHUNK, CHUNK), 1)).astype(F32)
        for h in range(HEADS):
            sl = slice(h * HEAD_DIM, (h + 1) * HEAD_DIM)
            u, du_dx = _gelu_and_grad(bu_ref[:, sl])
            v, dv_dx = _gelu_and_grad(bv_ref[:, sl])
            xhv, rstdv = _ln_stats(v)
            gh = sg_ref[h:h + 1, :]
            vn3 = (xhv * gh + sb_ref[h:h + 1, :]).astype(BF16).reshape(nc, CHUNK, HEAD_DIM)
            wb = jnp.broadcast_to(w_ref[h][None], (nc, CHUNK, CHUNK))
            mixed = jnp.einsum("cts,csd->ctd", wb, vn3, preferred_element_type=F32) + bb_ref[h][None]
            dyb = dyb_ref[:, sl]
            d_u = dyb * mixed.reshape(tm, HEAD_DIM)
            dm = dyb * u
            dm3 = dm.reshape(nc, CHUNK, HEAD_DIM)
            dbs_ref[h:h + 1, :] += jnp.sum(jnp.sum(dm3, axis=0).T, axis=0, keepdims=True)
            dm3b = dm3.astype(BF16)
            dw_h = jnp.sum(jnp.einsum("ctd,csd->cts", dm3b, vn3, preferred_element_type=F32), axis=0)
            dw_ref[h] += dw_h * tril
            wtb = jnp.broadcast_to(wt_ref[h][None], (nc, CHUNK, CHUNK))
            d_vn = jnp.einsum("cst,ctd->csd", wtb, dm3b, preferred_element_type=F32).reshape(tm, HEAD_DIM)
            dsg_ref[h:h + 1, :] += jnp.sum(d_vn * xhv, axis=0, keepdims=True)
            dsb_ref[h:h + 1, :] += jnp.sum(d_vn, axis=0, keepdims=True)
            dv = _ln_bwd(d_vn * gh, xhv, rstdv)
            dp_ref[:, 2 * ch + h * HEAD_DIM:2 * ch + (h + 1) * HEAD_DIM] = (d_u * du_dx).astype(BF16)
            dp_ref[:, 3 * ch + h * HEAD_DIM:3 * ch + (h + 1) * HEAD_DIM] = (dv * dv_dx).astype(BF16)

    col = lambda cidx: (lambda i: (i, cidx))
    prev = lambda cidx: (lambda i: (jnp.maximum(i * hb - 1, 0), cidx))
    nxt = lambda i: (jnp.minimum((i + 1) * hb, last_halo), 0)
    fix2 = lambda i: (0, 0)
    fix3 = lambda i: (0, 0, 0)
    out_shape = [jax.ShapeDtypeStruct((t, 4 * ch), BF16), jax.ShapeDtypeStruct((CONV_TAPS, ch), F32),
                 jax.ShapeDtypeStruct((1, ch), F32), jax.ShapeDtypeStruct((1, ch), F32), jax.ShapeDtypeStruct((1, ch), F32),
                 jax.ShapeDtypeStruct((HEADS, HEAD_DIM), F32), jax.ShapeDtypeStruct((HEADS, HEAD_DIM), F32),
                 jax.ShapeDtypeStruct((HEADS, CHUNK, CHUNK), F32), jax.ShapeDtypeStruct((HEADS, CHUNK), F32)]
    out_specs = [pl.BlockSpec((tm, 4 * ch), lambda i: (i, 0)), pl.BlockSpec((CONV_TAPS, ch), fix2),
                 pl.BlockSpec((1, ch), fix2), pl.BlockSpec((1, ch), fix2), pl.BlockSpec((1, ch), fix2),
                 pl.BlockSpec((HEADS, HEAD_DIM), fix2), pl.BlockSpec((HEADS, HEAD_DIM), fix2),
                 pl.BlockSpec((HEADS, CHUNK, CHUNK), fix3), pl.BlockSpec((HEADS, CHUNK), fix2)]
    in_specs = [pl.BlockSpec((tm, ch), col(0)), pl.BlockSpec((tm, ch), col(1)), pl.BlockSpec((tm, ch), col(2)),
                pl.BlockSpec((tm, ch), col(3)), pl.BlockSpec((HALO, ch), prev(0)), pl.BlockSpec((HALO, ch), prev(1)),
                pl.BlockSpec((tm, ch), col(0)), pl.BlockSpec((HALO, ch), nxt),
                pl.BlockSpec((tm, ch), col(0)), pl.BlockSpec((HALO, ch), nxt), pl.BlockSpec((tm, ch), col(1)),
                pl.BlockSpec((CONV_TAPS, ch), fix2), pl.BlockSpec((1, ch), fix2), pl.BlockSpec((1, ch), fix2),
                pl.BlockSpec((HEADS, HEAD_DIM), fix2), pl.BlockSpec((HEADS, HEAD_DIM), fix2),
                pl.BlockSpec((HEADS, CHUNK, CHUNK), fix3), pl.BlockSpec((HEADS, CHUNK, CHUNK), fix3),
                pl.BlockSpec((HEADS, CHUNK, HEAD_DIM), fix3)]
    return _call(
        body, exch, name=name, grid=(nt,), out_shape=out_shape, in_specs=in_specs, out_specs=out_specs,
        scratch_shapes=[pltpu.VMEM((HALO + tm, ch), F32), pltpu.VMEM((tm + HALO, ch), F32),
                        pltpu.VMEM((SUBLANES - 1, HALO + tm, ch), F32), pltpu.VMEM((SUBLANES - 1, tm + HALO, ch), F32),
                        pltpu.VMEM((CONV_TAPS, 8, ch), F32)],
        semantics=("arbitrary",),
    )(proj, proj, proj, proj, proj, proj, conv_c, conv_c, dy, dy, dy,
      conv_w, cln_g, cln_b, sln_g, sln_b, sg_wm, sg_wmt, sg_bb)


def _pair_sum(parts, from_sibling, core_chip, name):
    _, r, cc = parts.shape
    tr = _tile(r, max(16, (1 << 21) // (2 * cc)), 16)

    def body(cc_ref, p_ref, s_ref, o_ref, own_ref):
        q = (p_ref[...].astype(F32) + s_ref[...].astype(F32)).astype(BF16)
        o_ref[...] = q

        @pl.when(pl.program_id(1) == cc_ref[1])
        def _():
            own_ref[...] = q[0]

    grid_spec = pltpu.PrefetchScalarGridSpec(
        num_scalar_prefetch=1, grid=(r // tr, 4),
        in_specs=[pl.BlockSpec((1, tr, cc), lambda i, j, cc_ref: (2 * j + cc_ref[0], i, 0)),
                  pl.BlockSpec((1, tr, cc), lambda i, j, cc_ref: (j, i, 0))],
        out_specs=[pl.BlockSpec((1, tr, cc), lambda i, j, cc_ref: (j, i, 0)),
                   pl.BlockSpec((tr, cc), lambda i, j, cc_ref: (i, 0))])
    return pl.pallas_call(
        body, name=name, grid_spec=grid_spec,
        out_shape=[jax.ShapeDtypeStruct((4, r, cc), BF16), jax.ShapeDtypeStruct((r, cc), BF16)],
        compiler_params=_cparams("parallel", "arbitrary"),
    )(core_chip, parts, from_sibling)


def _adamw_math(w, g, m, v):
    m = ADAM_B1 * m + (1.0 - ADAM_B1) * g
    v = ADAM_B2 * v + (1.0 - ADAM_B2) * (g * g)
    m_hat = m / (1.0 - ADAM_B1 ** ADAM_STEP)
    v_hat = v / (1.0 - ADAM_B2 ** ADAM_STEP)
    delta = -ADAM_LR * (m_hat / (jnp.sqrt(v_hat) + ADAM_EPS) + ADAM_WD * w)
    return delta, m, v


def _adamw_tile(in_refs, out_refs):
    w_ref, m_ref, v_ref, q_ref, o_ref = in_refs
    g = q_ref[...].astype(F32)
    for k in range(3):
        g = g + o_ref[k].astype(F32)
    d, mm, vv = _adamw_math(w_ref[...], g, m_ref[...], v_ref[...])
    for ref, val in zip(out_refs, (g, d, mm, vv)):
        ref[...] = val


def _adamw_side(w, m, v, chip_part, from_chips, max_tiles):
    r, cc = w.shape
    n = max(k for k in range(1, max_tiles + 1) if r % k == 0 and (r // k) % 16 == 0)
    tr = r // n
    row = ((tr, cc), lambda s: (s, 0))
    return _Side([w, m, v, chip_part, from_chips], [row, row, row, row, ((3, tr, cc), lambda s: (0, s, 0))],
                 [jax.ShapeDtypeStruct((r, cc), F32)] * 4, [row] * 4, n, _adamw_tile)


def _adamw_sharded(w, m, v, chip_part, from_chips, name):
    r, cc = w.shape
    tr = _tile(r, max(16, (1 << 20) // (4 * cc) * 2), 16)

    def body(*refs):
        _adamw_tile(refs[:5], refs[5:])

    row = pl.BlockSpec((tr, cc), lambda i: (i, 0))
    return pl.pallas_call(
        body, name=name, grid=(r // tr,), out_shape=[jax.ShapeDtypeStruct((r, cc), F32)] * 4,
        in_specs=[row, row, row, row, pl.BlockSpec((3, tr, cc), lambda i: (0, i, 0))], out_specs=[row] * 4,
        compiler_params=_cparams("parallel"),
    )(w, m, v, chip_part, from_chips)


def _adamw_small(w, g, m, v, name):
    r, cc = w.shape

    def body(w_ref, g_ref, m_ref, v_ref, d_out, m_out, v_out):
        d, mm, vv = _adamw_math(w_ref[...], g_ref[...], m_ref[...], v_ref[...])
        d_out[...] = d
        m_out[...] = mm
        v_out[...] = vv

    full = pl.BlockSpec((r, cc), lambda i: (0, 0))
    return pl.pallas_call(
        body, name=name, grid=(1,), out_shape=[jax.ShapeDtypeStruct((r, cc), F32)] * 3,
        in_specs=[full] * 4, out_specs=[full] * 3, compiler_params=_cparams("arbitrary"),
    )(w, g, m, v)


SMALL = ("ln1_g", "ln1_b", "conv_b", "conv_ln_g", "conv_ln_b", "sg_ln_g", "sg_ln_b", "sg_w", "sg_b",
         "ln2_g", "ln2_b", "ln3_g", "ln3_b")
ORDER = ("ffn1_w_gate_up", "ffn1_w_down", "ln1_g", "ln1_b", "mix_w_in", "conv_w", "conv_b", "conv_ln_g", "conv_ln_b",
         "sg_ln_g", "sg_ln_b", "sg_w", "sg_b", "mix_w_out", "ln2_g", "ln2_b", "ffn2_w_gate_up", "ffn2_w_down",
         "ln3_g", "ln3_b")


def _rows128(a):
    return a.reshape(-1, 128)


def kernel(x, ffn1_w_gate_up, ffn1_w_down, ln1_g, ln1_b, mix_w_in, conv_w, conv_b, conv_ln_g, conv_ln_b, sg_ln_g, sg_ln_b, sg_w, sg_b, mix_w_out, ln2_g, ln2_b, ffn2_w_gate_up, ffn2_w_down, ln3_g, ln3_b, loss_target, m_ffn1_w_gate_up, m_ffn1_w_down, m_ln1_g, m_ln1_b, m_mix_w_in, m_conv_w, m_conv_b, m_conv_ln_g, m_conv_ln_b, m_sg_ln_g, m_sg_ln_b, m_sg_w, m_sg_b, m_mix_w_out, m_ln2_g, m_ln2_b, m_ffn2_w_gate_up, m_ffn2_w_down, m_ln3_g, m_ln3_b, v_ffn1_w_gate_up, v_ffn1_w_down, v_ln1_g, v_ln1_b, v_mix_w_in, v_conv_w, v_conv_b, v_conv_ln_g, v_conv_ln_b, v_sg_ln_g, v_sg_ln_b, v_sg_w, v_sg_b, v_mix_w_out, v_ln2_g, v_ln2_b, v_ffn2_w_gate_up, v_ffn2_w_down, v_ln3_g, v_ln3_b):
    args = dict(locals())
    w = {n: args[n][0] for n in ORDER}
    mom = {n: args["m_" + n][0] for n in ORDER}
    var = {n: args["v_" + n][0] for n in ORDER}
    x0 = x[0]
    target = loss_target[0]
    t, d = x0.shape
    my_x, my_y, my_c = lax.axis_index("x"), lax.axis_index("y"), lax.axis_index("c")
    my_chip = (2 * my_x + my_y).astype(jnp.int32).reshape(1)
    my_core = my_c.astype(jnp.int32).reshape(1)
    me = 4 * my_x + 2 * my_y + my_c

    big = ("ffn1_w_gate_up", "ffn1_w_down", "mix_w_in", "mix_w_out", "ffn2_w_gate_up", "ffn2_w_down")
    sh = {n: w[n].astype(BF16) for n in big}
    f2s = sh["ffn2_w_gate_up"].shape[1]
    order = jnp.stack([4 * p[0] + 2 * p[1] + p[2] for p in _visit_order(my_x, my_y, my_c)]).astype(jnp.int32)
    gu1, x0t, (wgu1, wd1, conv_w_all) = _gather_and_gate_up(
        x0, [sh["ffn1_w_gate_up"], sh["ffn1_w_down"], w["conv_w"]], [True, True, False], order, "ffn1_gate_up_fwd")
    wd1 = wd1.reshape(-1, d)
    conv_w_full = jnp.transpose(conv_w_all, (1, 0, 2)).reshape(CONV_TAPS, CONV_CH)
    tril = jnp.tril(jnp.ones((CHUNK, CHUNK), F32))
    sg_wm = w["sg_w"] * tril
    sg_wm_b = sg_wm.astype(BF16)
    sg_wmt_b = jnp.swapaxes(sg_wm, 1, 2).astype(BF16)
    sg_bb = jnp.broadcast_to(w["sg_b"][:, :, None], (HEADS, CHUNK, HEAD_DIM))
    row = lambda a: a.reshape(1, -1)

    d2 = [sh["ffn2_w_down"]]
    d2_first = d2[0].shape[0] // 2 // 16 * 16
    d2_top, d2_bottom = (0, d2_first), (d2_first, d2[0].shape[0] - d2_first)
    (h1t, z1, x1), ((g_in, g_out), (g_d2,)) = _ffn_down_fwd(
        gu1, x0, wd1, row(w["ln1_g"]), row(w["ln1_b"]), "ffn1_down_fwd",
        exch=[_gather_first([sh["mix_w_in"], sh["mix_w_out"]], [True, False]),
              _gather_first(d2, [False], rows=d2_top)])
    in_cols = sh["mix_w_in"].shape[1]
    x1t, ((w_in, w_out), (g_d2,)) = _transpose_bf16(
        x1, "x1_transpose", exch=[_gather_forward([g_in, g_out], [True, False], [in_cols, None]),
                                  _gather_forward([g_d2], [False], [None], rows=d2_top)])
    w_out = w_out.reshape(-1, d)
    top, bottom = (0, d // 2), (d // 2, d // 2)
    gu2 = [sh["ffn2_w_gate_up"]]
    proj, ((g_gu2,),) = _mix_in_proj(x1, w_in, "mix_in_fwd", exch=[_gather_first(gu2, [True], rows=top)])
    (y, yt, conv_c), ((g_gu2,),) = _mixer_fwd(
        proj, conv_w_full, row(w["conv_b"]), row(w["conv_ln_g"]), row(w["conv_ln_b"]),
        w["sg_ln_g"], w["sg_ln_b"], sg_wm_b, sg_bb, "mixer_fwd",
        exch=[_both(_gather_first(gu2, [True], rows=bottom, into=[g_gu2]),
                    _gather_forward([g_gu2], [True], [f2s], rows=top))])
    (z2, x2, x2t), ((wgu2,), (g_d2,)) = _mix_out_fwd(
        y, w_out, x1, row(w["ln2_g"]), row(w["ln2_b"]), "mix_out_fwd",
        exch=[_gather_forward([g_gu2], [True], [f2s], rows=bottom),
              _gather_first(d2, [False], rows=d2_bottom, into=[g_d2])])
    (wd2,) = _exchange_alone(_gather_forward([g_d2], [False], [None], rows=d2_bottom), "ffn2_down_gather_forward")
    wd2 = wd2.reshape(-1, d)
    grads = {}
    (g2, u2, h2t, dz3, do2, grads["ln3_g"], grads["ln3_b"], loss_tile), _ = _ffn_fwd_loss(
        x2, wgu2, wd2, row(w["ln3_g"]), row(w["ln3_b"]), target, "ffn2_fwd_loss")

    f = wd1.shape[0]
    dn = _tile(d, 1024, 128)
    core_chip = jnp.concatenate([my_core, my_chip])
    pair = lambda p, s, label: _pair_sum(p, s, core_chip, "pair_sum_" + label)
    adamw = lambda n, own, got, steps: _adamw_side(w[n], mom[n], var[n], own, got, steps)
    m_tiles = d // _tile(d, 512, 16)
    gu_first = d * 3 // 4 // 16 * 16
    out = {}
    p_d2, _ = _weight_grad(h2t, do2, dn, 512, "ffn2_dw_down")
    p_d2 = p_d2.reshape(N_DEV, f // N_DEV, d)
    (dg2, du2, dx2), ((s_d2,),) = _ffn_bwd(dz3, do2, g2, u2, wgu2, wd2, "ffn2_bwd", exch=[_rs_sibling([p_d2])])
    q_d2, own_d2 = pair(p_d2, s_d2, "ffn2_down")
    d_rows = q_d2.shape[1]
    d_half = d_rows // 2 // 16 * 16
    p_gu2, ((r_d2,),) = _weight_grad(x2t, dg2, f2s, 512, "ffn2_dw_gate", blocks=N_DEV,
                                     exch=[_rs_chips([q_d2], rows=(0, d_half))])
    p_gu2, ((r_d2,),) = _weight_grad(x2t, du2, f2s, 512, "ffn2_dw_up", blocks=N_DEV, block_offset=4, into=p_gu2,
                                     exch=[_rs_chips([q_d2], rows=(d_half, d_rows - d_half), into=[r_d2])])
    (dz2, dz2b, grads["ln2_g"], grads["ln2_b"]), ((s_gu2,),) = _ln_bwd_call(
        z2, dx2, row(w["ln2_g"]), 1.0, "ln2_bwd", exch=[_rs_sibling([p_gu2])])
    q_gu2, own_gu2 = pair(p_gu2, s_gu2, "ffn2_gate_up")
    dy, _ = _mix_out_bwd(dz2b, w_out, "mix_out_bwd")
    p_out, _ = _weight_grad(yt, dz2b, dn, 512, "mix_out_dw")
    p_out = p_out.reshape(N_DEV, -1, d)
    (dproj, grads["conv_w"], grads["conv_b"], grads["conv_ln_g"], grads["conv_ln_b"], grads["sg_ln_g"],
     grads["sg_ln_b"], grads["sg_w"], grads["sg_b"]), ((r_gu2,),) = _mixer_bwd(
        proj, conv_c, dy, conv_w_full, row(w["conv_ln_g"]), row(w["conv_ln_b"]), w["sg_ln_g"], w["sg_ln_b"],
        sg_wm_b, sg_wmt_b, sg_bb, "mixer_bwd", exch=[_rs_chips([q_gu2], rows=(0, gu_first))])
    dx1, ((s_out,), (r_gu2,)) = _mix_in_bwd(
        dproj, w_in, dz2, "mix_in_bwd",
        exch=[_rs_sibling([p_out]), _rs_chips([q_gu2], rows=(gu_first, d - gu_first), into=[r_gu2])])
    p_in, (out["ffn2_w_gate_up"], out["ffn2_w_down"]) = _weight_grad(
        x1t, dproj, in_cols, 512, "mix_in_dw", blocks=N_DEV,
        exch=[adamw("ffn2_w_gate_up", own_gu2, r_gu2, N_DEV * m_tiles), adamw("ffn2_w_down", own_d2, r_d2, N_DEV * m_tiles)])
    (dz1, do1, grads["ln1_g"], grads["ln1_b"]), ((s_in,),) = _ln_bwd_call(
        z1, dx1, row(w["ln1_g"]), 0.5, "ln1_bwd", exch=[_rs_sibling([p_in])])
    q_out, own_out = pair(p_out, s_out, "mix_out")
    q_in, own_in = pair(p_in, s_in, "mix_in")
    small_parts = [_rows128(grads[n]) for n in SMALL]
    packed = jnp.concatenate(small_parts + [_rows128(grads["conv_w"]), loss_tile], axis=0)
    p_d1, ((r_in,),) = _weight_grad(h1t, do1, dn, 512, "ffn1_dw_down", exch=[_rs_chips([q_in])])
    p_d1 = p_d1.reshape(N_DEV, f // N_DEV, d)
    (dg1, du1), ((s_d1,), (r_out,), (small_all,)) = _ffn_bwd_act(
        do1, gu1, wd1, "ffn1_bwd_act",
        exch=[_rs_sibling([p_d1]), _rs_chips([q_out]), _small_gather(packed)])
    q_d1, own_d1 = pair(p_d1, s_d1, "ffn1_down")
    p_gu1, ((r_d1,),) = _weight_grad(x0t, dg1, f2s, 512, "ffn1_dw_gate", blocks=N_DEV, exch=[_rs_chips([q_d1])])
    p_gu1, (out["mix_w_in"], out["mix_w_out"]) = _weight_grad(
        x0t, du1, f2s, 512, "ffn1_dw_up", blocks=N_DEV, block_offset=4, into=p_gu1,
        exch=[adamw("mix_w_in", own_in, r_in, 4 * m_tiles), adamw("mix_w_out", own_out, r_out, 4 * m_tiles)])
    (s_gu1,) = _exchange_alone(_rs_sibling([p_gu1]), "ffn1_gate_up_sibling_exchange")
    q_gu1, own_gu1 = pair(p_gu1, s_gu1, "ffn1_gate_up")
    (grad_x,), ((r_gu1,),) = _ffn_bwd_dx(dz1, dg1, du1, wgu1, "ffn1_bwd_dx", exch=[_rs_chips([q_gu1])])
    for n, own, got in (("ffn1_w_down", own_d1, r_d1), ("ffn1_w_gate_up", own_gu1, r_gu1)):
        out[n] = _adamw_sharded(w[n], mom[n], var[n], own, got, "adamw_" + n)

    cw_rows = CONV_TAPS * CONV_CH // 128
    total = _sum_over_devices(small_all)
    offs = [0]
    for p in small_parts:
        offs.append(offs[-1] + p.shape[0])
    n_small = offs[-1]
    loss = total[n_small + cw_rows, 0]
    g_conv_w = lax.dynamic_slice_in_dim(total[n_small:n_small + cw_rows].reshape(CONV_TAPS, CONV_CH),
                                        me * (CONV_CH // N_DEV), CONV_CH // N_DEV, axis=1)
    pad8 = lambda a: jnp.pad(a, ((0, -a.shape[0] % 8), (0, 0)))
    pack = lambda tree, cw: jnp.concatenate([_rows128(tree[n]) for n in SMALL] + [pad8(cw)], axis=0)
    g_pack = jnp.concatenate([total[:n_small], pad8(g_conv_w)], axis=0)
    d_pack, m_pack, v_pack = _adamw_small(pack(w, w["conv_w"]), g_pack, pack(mom, mom["conv_w"]),
                                          pack(var, var["conv_w"]), "adamw_small")
    for k, n in enumerate(SMALL):
        sl = slice(offs[k], offs[k + 1])
        shp = w[n].shape
        out[n] = (total[sl].reshape(shp), d_pack[sl].reshape(shp), m_pack[sl].reshape(shp), v_pack[sl].reshape(shp))
    sl = slice(n_small, n_small + CONV_TAPS)
    out["conv_w"] = (g_conv_w, d_pack[sl], m_pack[sl], v_pack[sl])

    lead = lambda a: a[None]
    res = [loss, grad_x[None]]
    for kind in range(4):
        res += [lead(out[n][kind]) for n in ORDER]
    return tuple(res)
```

```python
import functools
import math

import jax
import jax.numpy as jnp
from jax import lax
from jax.experimental import pallas as pl
from jax.experimental.pallas import tpu as pltpu

F32, BF16 = jnp.float32, jnp.bfloat16
MESH = pl.DeviceIdType.MESH
ANY = pl.BlockSpec(memory_space=pl.ANY)

N_DEV = 8
LN_EPS = 1e-5
ALPHA = 2.0 ** 0.25
CONV_CH = 1024
CONV_TAPS = 31
HALO = 32
HEADS = 8
HEAD_DIM = 128
CHUNK = 128
ADAM_LR, ADAM_B1, ADAM_B2, ADAM_EPS, ADAM_WD, ADAM_STEP = 0.001, 0.9, 0.999, 1e-08, 0.01, 10
V7X_VMEM_LIMIT = 62 * 2 ** 20
EPILOGUE_ROWS = 128

def _cparams(*sem):
    return pltpu.CompilerParams(dimension_semantics=sem, vmem_limit_bytes=V7X_VMEM_LIMIT)


def _tile(n, pref, mult):
    best = None
    for t in range(mult, min(n, pref) + 1, mult):
        if n % t == 0:
            best = t
    return best if best is not None else n


def _dot(a, b):
    return jnp.dot(a, b, preferred_element_type=F32)


def _dot_nt(a, b):
    return lax.dot_general(a, b, (((1,), (1,)), ((), ())), preferred_element_type=F32)


def _sigmoid(x):
    return 1.0 / (1.0 + jnp.exp(-x))


def _ln_stats(z):
    mu = jnp.mean(z, axis=-1, keepdims=True)
    zc = z - mu
    var = jnp.mean(zc * zc, axis=-1, keepdims=True)
    rstd = lax.rsqrt(var + LN_EPS)
    return zc * rstd, rstd


def _ln(z, g, b):
    xh, _ = _ln_stats(z)
    return xh * g + b


def _ln_bwd(dxh, xh, rstd):
    m1 = jnp.mean(dxh, axis=-1, keepdims=True)
    m2 = jnp.mean(dxh * xh, axis=-1, keepdims=True)
    return rstd * (dxh - m1 - xh * m2)


_GK = math.sqrt(2.0 / math.pi)
_GA = 0.044715


def _gelu_and_grad(x):
    x2 = x * x
    t = jnp.tanh(_GK * (x + _GA * x * x2))
    y = 0.5 * x * (1.0 + t)
    dy = 0.5 * (1.0 + t) + 0.5 * x * (1.0 - t * t) * (_GK * (1.0 + 3.0 * _GA * x2))
    return y, dy


def _silu_grad(a):
    s = _sigmoid(a)
    return s * (1.0 + a * (1.0 - s))


def _place():
    return lax.axis_index("x"), lax.axis_index("y"), lax.axis_index("c")


def _other_chips(x, y):
    return [(1 - x, y), (x, 1 - y), (1 - x, 1 - y)]


def _visit_order(x, y, c):
    chips = _other_chips(x, y)
    return [(x, y, c), (x, y, 1 - c), (*chips[0], c), (*chips[1], c), (*chips[0], 1 - c), (*chips[1], 1 - c),
            (*chips[2], c), (*chips[2], 1 - c)]


def _gather_and_gate_up(xb, shards, relayed, order, name):
    n = len(shards)
    N_COPIES = 10
    t, d = xb.shape
    cols = shards[0].shape[1]
    tm = _tile(t, 1024, 128)
    ni = t // tm
    col_major = [True] + [False] * (n - 1)

    def body(order_ref, x_ref, *refs):
        srcs, gu_ref, xt_ref, dsts = refs[:n], refs[n], refs[n + 1], refs[n + 2:2 * n + 2]
        wbuf, send_sems, recv_sems, local_sems, load_sem = refs[2 * n + 2:]
        b, i = pl.program_id(0), pl.program_id(1)
        x, y, c = _place()
        me, sib = (x, y, c), (x, y, 1 - c)
        chips = _other_chips(x, y)

        near_x, near_y, far = chips

        def slot(w, p, band=None):
            half = shards[w].shape[0] // 2
            rows = None if band is None else (band * half, half)
            return _block_slot(dsts[w], col_major[w], shards[w].shape[1], p, rows)

        def copy(w, s, block, to, band=None, from_src=False):
            return pltpu.make_async_remote_copy(
                src_ref=srcs[w] if from_src else slot(w, block, band), dst_ref=slot(w, block, band),
                send_sem=send_sems.at[N_COPIES * w + s], recv_sem=recv_sems.at[N_COPIES * w + s],
                device_id=to, device_id_type=MESH)

        def own(w):
            return pltpu.make_async_copy(srcs[w], slot(w, me), local_sems.at[w])

        def sends(w):
            out = [copy(w, 0, me, sib, from_src=True), copy(w, 1, me, (*near_x, c), from_src=True),
                   copy(w, 2, me, (*near_y, c), from_src=True)]
            if not relayed[w]:
                out.append(copy(w, 3, me, (*far, c), from_src=True))
            return out

        def passed_on(w):
            out = [copy(w, 4, (*near_x, c), sib), copy(w, 5, (*near_y, c), sib)]
            if relayed[w]:
                out += [copy(w, 6, (*far, c), sib, band=0), copy(w, 9, (*far, c), sib, band=1),
                        copy(w, 7, (*near_x, c), (*near_y, c), band=0), copy(w, 8, (*near_y, c), (*near_x, c), band=1)]
            else:
                out.append(copy(w, 6, (*far, c), sib))
            return out

        def start_sends(w):
            own(w).start()
            for cp in sends(w):
                cp.start()

        def got_near_x(w):
            copy(w, 1, (*near_x, c), me).wait_recv()
            copy(w, 4, (*near_x, c), sib).start()
            if relayed[w]:
                copy(w, 7, (*near_x, c), (*near_y, c), band=0).start()

        def got_near_y(w):
            copy(w, 2, (*near_y, c), me).wait_recv()
            copy(w, 5, (*near_y, c), sib).start()
            if relayed[w]:
                copy(w, 8, (*near_y, c), (*near_x, c), band=1).start()

        def got_far(w):
            if relayed[w]:
                copy(w, 7, (*far, c), me, band=0).wait_recv()
                copy(w, 6, (*far, c), sib, band=0).start()
                copy(w, 8, (*far, c), me, band=1).wait_recv()
                copy(w, 9, (*far, c), sib, band=1).start()
            else:
                copy(w, 3, (*far, c), me).wait_recv()
                copy(w, 6, (*far, c), sib).start()

        def got_from_sibling(w, which):
            if which == 0:
                copy(w, 0, sib, me).wait_recv()
            elif which == 3 and relayed[w]:
                copy(w, 6, (*far, 1 - c), me, band=0).wait_recv()
                copy(w, 9, (*far, 1 - c), me, band=1).wait_recv()
            else:
                copy(w, 3 + which, (*chips[which - 1], 1 - c), me).wait_recv()

        others = range(1, n)

        def arrive(k):
            if k == 0:
                own(0).wait()
            elif k == 1:
                got_from_sibling(0, 0)
            elif k == 2:
                got_near_x(0)
                for w in others:
                    start_sends(w)
            elif k == 3:
                got_near_y(0)
            elif k in (4, 5):
                got_from_sibling(0, k - 3)
            elif k == 6:
                got_far(0)
                for w in others:
                    got_near_x(w)
                    got_near_y(w)
            else:
                got_from_sibling(0, 3)
                for w in others:
                    got_far(w)

        def load(k):
            at = pl.multiple_of(order_ref[k] * cols, 128)
            return pltpu.make_async_copy(dsts[0].at[:, pl.ds(at, cols)], wbuf.at[k % 2], load_sem.at[k % 2])

        @pl.when((b == 0) & (i == 0))
        def _():
            start_sends(0)
            arrive(0)
            load(0).start()
            load(0).wait()

        early = ni - 1
        for k in range(1, N_DEV):
            @pl.when((b == k - 1) & (i == early))
            def _(k=k):
                arrive(k)
                load(k).start()

            @pl.when((b == k) & (i == 0))
            def _(k=k):
                load(k).wait()

        gu_ref[...] = _dot(x_ref[...].astype(BF16), wbuf[b % 2]).astype(BF16)

        @pl.when(b == 0)
        def _():
            xt_ref[...] = x_ref[...].T.astype(BF16)

        @pl.when((b == N_DEV - 1) & (i == ni - 1))
        def _():
            for w in others:
                for which in range(4):
                    got_from_sibling(w, which)
                own(w).wait()
            for w in range(n):
                for cp in sends(w) + passed_on(w):
                    cp.wait_send()

    grid_spec = pltpu.PrefetchScalarGridSpec(
        num_scalar_prefetch=1, grid=(N_DEV, ni),
        in_specs=[pl.BlockSpec((tm, d), lambda b, i, o: (i, 0))] + [ANY] * n,
        out_specs=[pl.BlockSpec((tm, cols), lambda b, i, o: (i, o[b])),
                   pl.BlockSpec((d, tm), lambda b, i, o: (0, jnp.where(b == 0, i, ni - 1)))] + [ANY] * n,
        scratch_shapes=[pltpu.VMEM((2, d, cols), BF16), pltpu.SemaphoreType.DMA((N_COPIES * n,)),
                        pltpu.SemaphoreType.DMA((N_COPIES * n,)), pltpu.SemaphoreType.DMA((n,)),
                        pltpu.SemaphoreType.DMA((2,))])
    res = pl.pallas_call(
        body, name=name, grid_spec=grid_spec,
        out_shape=[jax.ShapeDtypeStruct((t, N_DEV * cols), BF16), jax.ShapeDtypeStruct((d, t), BF16)]
        + [_gathered_shape(s, cm) for s, cm in zip(shards, col_major)],
        compiler_params=_cparams("arbitrary", "arbitrary"),
    )(order, xb, *shards)
    return res[0], res[1], res[2:]


class _Exchange:
    def __init__(self, ins, io, new, n_sems, n_local, make):
        self.ins, self.io, self.new = list(ins), list(io), list(new)
        self.n_sems, self.n_local, self.make = n_sems, n_local, make


def _block_slot(ref, col_major, cols, place, rows=None):
    k = 4 * place[0] + 2 * place[1] + place[2]
    band = slice(None) if rows is None else pl.ds(rows[0], rows[1])
    if col_major:
        return ref.at[band, pl.ds(pl.multiple_of(k * cols, 128), cols)]
    return ref.at[k] if rows is None else ref.at[k, band]


def _gathered_shape(s, col_major):
    return jax.ShapeDtypeStruct((s.shape[0], N_DEV * s.shape[1]) if col_major else (N_DEV,) + s.shape, s.dtype)


def _gather_first(shards, col_major, rows=None, into=None):
    n = len(shards)
    new = [] if into is not None else [_gathered_shape(s, cm) for s, cm in zip(shards, col_major)]

    def make(in_refs, io_refs, new_refs, send_sems, recv_sems, local_sems, base=0, local_base=0):
        x, y, c = _place()
        targets = [(x, y, 1 - c)] + [(*chip, c) for chip in _other_chips(x, y)]
        gathered = io_refs if into is not None else new_refs
        copies = []
        for w in range(n):
            src = in_refs[w] if rows is None else in_refs[w].at[pl.ds(rows[0], rows[1])]
            slot = _block_slot(gathered[w], col_major[w], shards[w].shape[1], (x, y, c), rows)
            copies.append(pltpu.make_async_copy(src, slot, local_sems.at[local_base + w]))
            for s, to in enumerate(targets):
                copies.append(pltpu.make_async_remote_copy(
                    src_ref=src, dst_ref=slot, send_sem=send_sems.at[base + 4 * w + s],
                    recv_sem=recv_sems.at[base + 4 * w + s], device_id=to, device_id_type=MESH))
        return copies

    return _Exchange(shards, into or [], new, 4 * n, n, make)


def _gather_forward(gathered, col_major, cols, rows=None):
    n = len(gathered)

    def make(in_refs, io_refs, new_refs, send_sems, recv_sems, local_sems, base=0, local_base=0):
        x, y, c = _place()
        copies = []
        for w in range(n):
            for j, chip in enumerate(_other_chips(x, y)):
                slot = _block_slot(io_refs[w], col_major[w], cols[w], (*chip, c), rows)
                copies.append(pltpu.make_async_remote_copy(
                    src_ref=slot, dst_ref=slot, send_sem=send_sems.at[base + 3 * w + j],
                    recv_sem=recv_sems.at[base + 3 * w + j], device_id=(x, y, 1 - c), device_id_type=MESH))
        return copies

    return _Exchange([], gathered, [], 3 * n, 0, make)


def _both(a, b):
    def make(in_refs, io_refs, new_refs, send_sems, recv_sems, local_sems):
        na = len(a.ins)
        return (a.make(in_refs[:na], io_refs, [], send_sems, recv_sems, local_sems, 0, 0)
                + b.make(in_refs[na:], io_refs, [], send_sems, recv_sems, local_sems, a.n_sems, a.n_local))

    return _Exchange(a.ins + b.ins, a.io, [], a.n_sems + b.n_sems, a.n_local + b.n_local, make)


def _rs_sibling(parts):
    n = len(parts)

    def make(in_refs, io_refs, new_refs, send_sems, recv_sems, local_sems):
        x, y, c = _place()
        copies = []
        for w in range(n):
            for j in range(4):
                copies.append(pltpu.make_async_remote_copy(
                    src_ref=in_refs[w].at[2 * j + (1 - c)], dst_ref=new_refs[w].at[j],
                    send_sem=send_sems.at[4 * w + j], recv_sem=recv_sems.at[4 * w + j],
                    device_id=(x, y, 1 - c), device_id_type=MESH))
        return copies

    return _Exchange(parts, [], [jax.ShapeDtypeStruct((4,) + p.shape[1:], p.dtype) for p in parts], 4 * n, 0, make)


def _rs_chips(chip_parts, rows=None, into=None):
    n = len(chip_parts)
    band = slice(None) if rows is None else pl.ds(rows[0], rows[1])
    new = [] if into is not None else [jax.ShapeDtypeStruct((3,) + p.shape[1:], p.dtype) for p in chip_parts]

    def make(in_refs, io_refs, new_refs, send_sems, recv_sems, local_sems):
        x, y, c = _place()
        landing = io_refs if into is not None else new_refs
        copies = []
        for w in range(n):
            for rel, (px, py) in enumerate(_other_chips(x, y)):
                copies.append(pltpu.make_async_remote_copy(
                    src_ref=in_refs[w].at[2 * px + py, band], dst_ref=landing[w].at[rel, band],
                    send_sem=send_sems.at[3 * w + rel], recv_sem=recv_sems.at[3 * w + rel],
                    device_id=(px, py, c), device_id_type=MESH))
        return copies

    return _Exchange(chip_parts, into or [], new, 3 * n, 0, make)


class _Side:
    def __init__(self, ins, in_blocks, out_shapes, out_blocks, n_tiles, fn):
        self.ins, self.in_blocks, self.out_shapes, self.out_blocks = list(ins), in_blocks, list(out_shapes), out_blocks
        self.n_tiles, self.fn = n_tiles, fn


def _call(body, exch, *, name, grid, in_specs, out_specs, out_shape, scratch_shapes=(), semantics,
          input_output_aliases=None):
    exch = list(exch)
    in_specs, out_specs, out_shape = list(in_specs), list(out_specs), list(out_shape)
    scratch_shapes = list(scratch_shapes)
    if not exch:
        fn = pl.pallas_call(body, name=name, grid=grid, in_specs=in_specs, out_specs=out_specs, out_shape=out_shape,
                            scratch_shapes=scratch_shapes, input_output_aliases=input_output_aliases or {},
                            compiler_params=_cparams(*semantics))
        return lambda *args: (fn(*args), [])
    n_in, n_out, n_scr = len(in_specs), len(out_specs), len(scratch_shapes)
    aliases = dict(input_output_aliases or {})
    all_in, all_out_specs, all_out_shape, all_scr = list(in_specs), list(out_specs), list(out_shape), list(scratch_shapes)
    extra_args = []

    def step(idx):
        s = idx[0]
        for a in range(1, len(grid)):
            s = s * grid[a] + idx[a]
        return s

    def tile_spec(shape, where, n_tiles):
        return pl.BlockSpec(shape, lambda *idx: where(jnp.minimum(step(idx), n_tiles - 1)))

    for ex in exch:
        if isinstance(ex, _Side):
            all_in += [tile_spec(shape, where, ex.n_tiles) for shape, where in ex.in_blocks]
            extra_args += ex.ins
            all_out_specs += [tile_spec(shape, where, ex.n_tiles) for shape, where in ex.out_blocks]
            all_out_shape += ex.out_shapes
            continue
        for k, a in enumerate(ex.io):
            aliases[len(all_in) + len(ex.ins) + k] = len(all_out_specs) + k
        all_in += [ANY] * (len(ex.ins) + len(ex.io))
        extra_args += ex.ins + ex.io
        all_out_specs += [ANY] * (len(ex.io) + len(ex.new))
        all_out_shape += [jax.ShapeDtypeStruct(a.shape, a.dtype) for a in ex.io] + ex.new
        all_scr += [pltpu.SemaphoreType.DMA((ex.n_sems,)), pltpu.SemaphoreType.DMA((ex.n_sems,)),
                    pltpu.SemaphoreType.DMA((max(ex.n_local, 1),))]

    n_ins = [len(ex.ins) if isinstance(ex, _Side) else len(ex.ins) + len(ex.io) for ex in exch]
    n_outs = [len(ex.out_shapes) if isinstance(ex, _Side) else len(ex.io) + len(ex.new) for ex in exch]

    def wrapped(*refs):
        pos = n_in
        ex_in = []
        for k in n_ins:
            ex_in.append(refs[pos:pos + k])
            pos += k
        outs = refs[pos:pos + n_out]
        pos += n_out
        ex_out = []
        for k in n_outs:
            ex_out.append(refs[pos:pos + k])
            pos += k
        scr = refs[pos:pos + n_scr]
        pos += n_scr
        idx = [pl.program_id(a) for a in range(len(grid))]
        first = functools.reduce(jnp.logical_and, [i == 0 for i in idx])
        last = functools.reduce(jnp.logical_and, [i == g - 1 for i, g in zip(idx, grid)])

        def copies():
            out, at = [], pos
            for ex, ei, eo in zip(exch, ex_in, ex_out):
                if not isinstance(ex, _Side):
                    out += ex.make(ei[:len(ex.ins)], eo[:len(ex.io)], eo[len(ex.io):], *refs[at:at + 3])
                    at += 3
            return out

        @pl.when(first)
        def _():
            for cp in copies():
                cp.start()

        body(*refs[:n_in], *outs, *scr)
        for ex, ei, eo in zip(exch, ex_in, ex_out):
            if isinstance(ex, _Side):
                pl.when(step(idx) < ex.n_tiles)(functools.partial(ex.fn, ei, eo))

        @pl.when(last)
        def _():
            for cp in copies():
                cp.wait()

    fn = pl.pallas_call(wrapped, name=name, grid=grid, in_specs=all_in, out_specs=all_out_specs,
                        out_shape=all_out_shape, scratch_shapes=all_scr, input_output_aliases=aliases,
                        compiler_params=_cparams(*(["arbitrary"] * len(grid))))

    def run(*args):
        res = fn(*args, *extra_args)
        outs, pos, ex_res = res[:n_out], n_out, []
        for k in n_outs:
            ex_res.append(list(res[pos:pos + k]))
            pos += k
        return outs, ex_res

    return run


def _exchange_alone(ex, name):
    def body():
        pass

    _, res = _call(body, [ex], name=name, grid=(1,), in_specs=[], out_specs=[], out_shape=[], semantics=("arbitrary",))()
    return res[0]


def _small_gather(part):
    def make(in_refs, io_refs, new_refs, send_sems, recv_sems, local_sems):
        x, y, c = _place()
        slot = new_refs[0].at[4 * x + 2 * y + c]
        copies = [pltpu.make_async_copy(in_refs[0], slot, local_sems.at[0])]
        for d in range(1, N_DEV):
            peer = (1 - x if d & 4 else x, 1 - y if d & 2 else y, 1 - c if d & 1 else c)
            copies.append(pltpu.make_async_remote_copy(
                src_ref=in_refs[0], dst_ref=slot, send_sem=send_sems.at[d - 1], recv_sem=recv_sems.at[d - 1],
                device_id=peer, device_id_type=MESH))
        return copies

    return _Exchange([part], [], [jax.ShapeDtypeStruct((N_DEV,) + part.shape, part.dtype)], N_DEV - 1, 1, make)


def _sum_over_devices(parts):
    _, rows, lanes = parts.shape

    def body(p_ref, o_ref):
        acc = p_ref[0]
        for k in range(1, N_DEV):
            acc = acc + p_ref[k]
        o_ref[...] = acc

    return pl.pallas_call(
        body, name="small_grads_sum", grid=(1,), out_shape=jax.ShapeDtypeStruct((rows, lanes), F32),
        in_specs=[pl.BlockSpec((N_DEV, rows, lanes), lambda i: (0, 0, 0))],
        out_specs=pl.BlockSpec((rows, lanes), lambda i: (0, 0)),
        compiler_params=_cparams("arbitrary"),
    )(parts)


def _transpose_bf16(a, name, exch=(), with_copy=False):
    r, c = a.shape
    tr, tc = _tile(r, 512, 128), _tile(c, 1024, 128)

    def body(a_ref, o_ref, *copy_ref):
        v = a_ref[...].astype(F32)
        o_ref[...] = v.T.astype(BF16)
        if with_copy:
            copy_ref[0][...] = v.astype(BF16)

    outs, ex = _call(
        body, exch, name=name, grid=(r // tr, c // tc),
        out_shape=[jax.ShapeDtypeStruct((c, r), BF16)] + [jax.ShapeDtypeStruct((r, c), BF16)] * with_copy,
        in_specs=[pl.BlockSpec((tr, tc), lambda i, j: (i, j))],
        out_specs=[pl.BlockSpec((tc, tr), lambda i, j: (j, i))] + [pl.BlockSpec((tr, tc), lambda i, j: (i, j))] * with_copy,
        semantics=("parallel", "parallel"),
    )(a)
    return (outs if with_copy else outs[0]), ex


def _ffn_fwd_loss(x, wgu, wd, ln_g, ln_b, target, name, exch=()):
    t, d = x.shape
    f = wd.shape[0]
    tm, tf = _tile(t, 512, 128), _tile(f, 512, 128)
    nf = f // tf

    def body(x_ref, wg_ref, wu_ref, wd_ref, lg_ref, lb_ref, t_ref,
             go_ref, uo_ref, ht_ref, dz_ref, dzb_ref, dlg_ref, dlb_ref, loss_ref, xb, acc):
        i, j = pl.program_id(0), pl.program_id(1)

        @pl.when(j == 0)
        def _():
            xb[...] = x_ref[...].astype(BF16)
            acc[...] = jnp.zeros_like(acc)

        @pl.when((i == 0) & (j == 0))
        def _():
            dlg_ref[...] = jnp.zeros_like(dlg_ref)
            dlb_ref[...] = jnp.zeros_like(dlb_ref)
            loss_ref[...] = jnp.zeros_like(loss_ref)

        g = _dot(xb[...], wg_ref[...])
        u = _dot(xb[...], wu_ref[...])
        h = g * _sigmoid(g) * u
        go_ref[...] = g.astype(BF16)
        uo_ref[...] = u.astype(BF16)
        ht_ref[...] = h.T.astype(BF16)
        acc[...] += _dot(h.astype(BF16), wd_ref[...])

        @pl.when(j == nf - 1)
        def _():
            for r in range(0, tm, EPILOGUE_ROWS):
                rows = slice(r, r + EPILOGUE_ROWS)
                xh, rstd = _ln_stats(ALPHA * x_ref[rows, :] + 0.5 * acc[rows, :])
                e = xh * lg_ref[...] + lb_ref[...] - t_ref[rows, :]
                loss_ref[...] += 0.5 * jnp.sum(jnp.sum(e * e, axis=-1, keepdims=True) * (1.0 / d), axis=0,
                                               keepdims=True)
                dy = e * (1.0 / d)
                dz = _ln_bwd(dy * lg_ref[...], xh, rstd)
                dz_ref[rows, :] = dz
                dzb_ref[rows, :] = (0.5 * dz).astype(BF16)
                dlg_ref[...] += jnp.sum(dy * xh, axis=0, keepdims=True)
                dlb_ref[...] += jnp.sum(dy, axis=0, keepdims=True)

    row = lambda i, j: (i, 0)
    fixed = lambda i, j: (0, 0)
    return _call(
        body, exch, name=name, grid=(t // tm, nf),
        out_shape=[jax.ShapeDtypeStruct((t, f), BF16), jax.ShapeDtypeStruct((t, f), BF16),
                   jax.ShapeDtypeStruct((f, t), BF16), jax.ShapeDtypeStruct((t, d), F32),
                   jax.ShapeDtypeStruct((t, d), BF16), jax.ShapeDtypeStruct((1, d), F32),
                   jax.ShapeDtypeStruct((1, d), F32), jax.ShapeDtypeStruct((8, 128), F32)],
        in_specs=[pl.BlockSpec((tm, d), row),
                  pl.BlockSpec((d, tf), lambda i, j: (0, j)),
                  pl.BlockSpec((d, tf), lambda i, j: (0, j + nf)),
                  pl.BlockSpec((tf, d), lambda i, j: (j, 0)),
                  pl.BlockSpec((1, d), fixed), pl.BlockSpec((1, d), fixed), pl.BlockSpec((tm, d), row)],
        out_specs=[pl.BlockSpec((tm, tf), lambda i, j: (i, j)), pl.BlockSpec((tm, tf), lambda i, j: (i, j)),
                   pl.BlockSpec((tf, tm), lambda i, j: (j, i)), pl.BlockSpec((tm, d), row), pl.BlockSpec((tm, d), row),
                   pl.BlockSpec((1, d), fixed), pl.BlockSpec((1, d), fixed), pl.BlockSpec((8, 128), fixed)],
        scratch_shapes=[pltpu.VMEM((tm, d), BF16), pltpu.VMEM((tm, d), F32)],
        semantics=("arbitrary", "arbitrary"),
    )(x, wgu, wgu, wd, ln_g, ln_b, target)


def _ffn_down_fwd(gu, x, wd, ln_g, ln_b, name, exch=()):
    t, d = x.shape
    f = wd.shape[0]
    tm, tf = _tile(t, 512, 128), _tile(f, 512, 128)
    nf = f // tf

    def body(g_ref, u_ref, wd_ref, x_ref, lg_ref, lb_ref, ht_ref, z_ref, xn_ref, acc):
        j = pl.program_id(1)

        @pl.when(j == 0)
        def _():
            acc[...] = jnp.zeros_like(acc)

        g = g_ref[...].astype(F32)
        h = g * _sigmoid(g) * u_ref[...].astype(F32)
        ht_ref[...] = h.T.astype(BF16)
        acc[...] += _dot(h.astype(BF16), wd_ref[...])

        @pl.when(j == nf - 1)
        def _():
            z = ALPHA * x_ref[...] + 0.5 * acc[...]
            z_ref[...] = z
            xn_ref[...] = _ln(z, lg_ref[...], lb_ref[...])

    row = lambda i, j: (i, 0)
    fixed = lambda i, j: (0, 0)
    return _call(
        body, exch, name=name, grid=(t // tm, nf),
        out_shape=[jax.ShapeDtypeStruct((f, t), BF16), jax.ShapeDtypeStruct((t, d), F32),
                   jax.ShapeDtypeStruct((t, d), F32)],
        in_specs=[pl.BlockSpec((tm, tf), lambda i, j: (i, j)), pl.BlockSpec((tm, tf), lambda i, j: (i, j + nf)),
                  pl.BlockSpec((tf, d), lambda i, j: (j, 0)), pl.BlockSpec((tm, d), row),
                  pl.BlockSpec((1, d), fixed), pl.BlockSpec((1, d), fixed)],
        out_specs=[pl.BlockSpec((tf, tm), lambda i, j: (j, i)), pl.BlockSpec((tm, d), row), pl.BlockSpec((tm, d), row)],
        scratch_shapes=[pltpu.VMEM((tm, d), F32)],
        semantics=("parallel", "arbitrary"),
    )(gu, gu, wd, x, ln_g, ln_b)


def _ffn_act_grads(dh, g_ref, u_ref):
    gg = g_ref[...].astype(F32)
    uu = u_ref[...].astype(F32)
    s = _sigmoid(gg)
    du = (dh * (gg * s)).astype(BF16)
    dg = (dh * uu * (s * (1.0 + gg * (1.0 - s)))).astype(BF16)
    return dg, du


def _ffn_bwd(dz, do, g, u, wgu, wd, name, exch=()):
    t, d = dz.shape
    f = wd.shape[0]
    tm, tf = _tile(t, 512, 128), _tile(f, 512, 128)
    nf = f // tf

    def body(dz_ref, do_ref, g_ref, u_ref, wg_ref, wu_ref, wd_ref, dg_ref, du_ref, dx_ref, acc):
        j = pl.program_id(1)

        @pl.when(j == 0)
        def _():
            acc[...] = jnp.zeros_like(acc)

        dg, du = _ffn_act_grads(_dot_nt(do_ref[...], wd_ref[...]), g_ref, u_ref)
        dg_ref[...] = dg
        du_ref[...] = du
        acc[...] += _dot_nt(dg, wg_ref[...]) + _dot_nt(du, wu_ref[...])

        @pl.when(j == nf - 1)
        def _():
            dx_ref[...] = ALPHA * dz_ref[...] + acc[...]

    row = lambda i, j: (i, 0)
    tile = lambda i, j: (i, j)
    return _call(
        body, exch, name=name, grid=(t // tm, nf),
        out_shape=[jax.ShapeDtypeStruct((t, f), BF16), jax.ShapeDtypeStruct((t, f), BF16),
                   jax.ShapeDtypeStruct((t, d), F32)],
        in_specs=[pl.BlockSpec((tm, d), row), pl.BlockSpec((tm, d), row),
                  pl.BlockSpec((tm, tf), tile), pl.BlockSpec((tm, tf), tile),
                  pl.BlockSpec((d, tf), lambda i, j: (0, j)),
                  pl.BlockSpec((d, tf), lambda i, j: (0, j + nf)),
                  pl.BlockSpec((tf, d), lambda i, j: (j, 0))],
        out_specs=[pl.BlockSpec((tm, tf), tile), pl.BlockSpec((tm, tf), tile), pl.BlockSpec((tm, d), row)],
        scratch_shapes=[pltpu.VMEM((tm, d), F32)],
        semantics=("parallel", "arbitrary"),
    )(dz, do, g, u, wgu, wgu, wd)


def _ffn_bwd_act(do, gu, wd, name, exch=()):
    t, d = do.shape
    f = wd.shape[0]
    tm, tf = _tile(t, 2048, 128), _tile(f, 512, 128)
    nf = f // tf

    def body(do_ref, g_ref, u_ref, wd_ref, dg_ref, du_ref):
        dg, du = _ffn_act_grads(_dot_nt(do_ref[...], wd_ref[...]), g_ref, u_ref)
        dg_ref[...] = dg
        du_ref[...] = du

    tile = lambda i, j: (i, j)
    return _call(
        body, exch, name=name, grid=(t // tm, f // tf),
        out_shape=[jax.ShapeDtypeStruct((t, f), BF16), jax.ShapeDtypeStruct((t, f), BF16)],
        in_specs=[pl.BlockSpec((tm, d), lambda i, j: (i, 0)), pl.BlockSpec((tm, tf), tile),
                  pl.BlockSpec((tm, tf), lambda i, j: (i, j + nf)), pl.BlockSpec((tf, d), lambda i, j: (j, 0))],
        out_specs=[pl.BlockSpec((tm, tf), tile), pl.BlockSpec((tm, tf), tile)],
        semantics=("parallel", "parallel"),
    )(do, gu, gu, wd)


def _ffn_bwd_dx(dz, dg, du, wgu, name, exch=()):
    t, d = dz.shape
    f = dg.shape[1]
    tm, tn = _tile(t, 512, 128), _tile(d, 256, 128)

    def body(dz_ref, dg_ref, du_ref, wg_ref, wu_ref, dx_ref):
        dx_ref[...] = ALPHA * dz_ref[...] + _dot_nt(dg_ref[...], wg_ref[...]) + _dot_nt(du_ref[...], wu_ref[...])

    row = lambda i, n: (i, 0)
    tile = lambda i, n: (i, n)
    return _call(
        body, exch, name=name, grid=(t // tm, d // tn), out_shape=[jax.ShapeDtypeStruct((t, d), F32)],
        in_specs=[pl.BlockSpec((tm, tn), tile), pl.BlockSpec((tm, f), row), pl.BlockSpec((tm, f), row),
                  pl.BlockSpec((tn, f), lambda i, n: (n, 0)), pl.BlockSpec((tn, f), lambda i, n: (n, 1))],
        out_specs=[pl.BlockSpec((tm, tn), tile)],
        semantics=("parallel", "arbitrary"),
    )(dz, dg, du, wgu, wgu)


def _weight_grad(at, b, tn, tmm, name, blocks=None, block_offset=0, into=None, exch=()):
    m, t = at.shape
    nn = b.shape[1]
    tmm = _tile(m, tmm, 16)
    assert nn % tn == 0

    def body(*refs):
        at_ref, b_ref, o_ref = refs[0], refs[1], refs[-1]
        r = _dot(at_ref[...], b_ref[...]).astype(BF16)
        if blocks is None:
            o_ref[...] = r
        else:
            o_ref[0] = r

    in_specs = [pl.BlockSpec((tmm, t), lambda n, i: (i, 0)), pl.BlockSpec((t, tn), lambda n, i: (0, n))]
    args = [at, b]
    aliases = {}
    if into is not None:
        in_specs.append(ANY)
        args.append(into)
        aliases = {2: 0}
    if blocks is None:
        out_shape = jax.ShapeDtypeStruct((m, nn), BF16)
        out_spec = pl.BlockSpec((tmm, tn), lambda n, i: (i, n))
    else:
        out_shape = jax.ShapeDtypeStruct((blocks, m, tn), BF16)
        out_spec = pl.BlockSpec((1, tmm, tn), lambda n, i: (n + block_offset, i, 0))
    (out,), ex = _call(
        body, exch, name=name, grid=(nn // tn, m // tmm), out_shape=[out_shape],
        in_specs=in_specs, out_specs=[out_spec], input_output_aliases=aliases,
        semantics=("parallel", "parallel"),
    )(*args)
    return out, ex


def _mix_in_proj(x, w_in, name, exch=()):
    t, d = x.shape
    n_out = w_in.shape[1]
    tm, cb = _tile(t, 512, 128), _tile(n_out, 1024, 128)

    def body(x_ref, w_ref, o_ref, xb):
        @pl.when(pl.program_id(1) == 0)
        def _():
            xb[...] = x_ref[...].astype(BF16)

        o_ref[...] = _dot(xb[...], w_ref[...])

    (out,), ex = _call(
        body, exch, name=name, grid=(t // tm, n_out // cb), out_shape=[jax.ShapeDtypeStruct((t, n_out), F32)],
        in_specs=[pl.BlockSpec((tm, d), lambda i, k: (i, 0)), pl.BlockSpec((d, cb), lambda i, k: (0, k))],
        out_specs=[pl.BlockSpec((tm, cb), lambda i, k: (i, k))],
        scratch_shapes=[pltpu.VMEM((tm, d), BF16)],
        semantics=("parallel", "arbitrary"),
    )(x, w_in)
    return out, ex


def _mix_in_bwd(dproj, w_in, dz, name, exch=()):
    t, d = dz.shape
    kk = w_in.shape[1]
    tm, tn = _tile(t, 512, 128), _tile(d, 512, 128)

    def body(dp_ref, w_ref, dz_ref, dx_ref):
        dx_ref[...] = ALPHA * dz_ref[...] + _dot_nt(dp_ref[...], w_ref[...])

    (out,), ex = _call(
        body, exch, name=name, grid=(t // tm, d // tn), out_shape=[jax.ShapeDtypeStruct((t, d), F32)],
        in_specs=[pl.BlockSpec((tm, kk), lambda i, n: (i, 0)), pl.BlockSpec((tn, kk), lambda i, n: (n, 0)),
                  pl.BlockSpec((tm, tn), lambda i, n: (i, n))],
        out_specs=[pl.BlockSpec((tm, tn), lambda i, n: (i, n))],
        semantics=("parallel", "arbitrary"),
    )(dproj, w_in, dz)
    return out, ex


def _mix_out_fwd(y, w_out, x, ln_g, ln_b, name, exch=()):
    t, d = x.shape
    kk = y.shape[1]
    tm = _tile(t, 256, 128)

    def body(y_ref, w_ref, x_ref, g_ref, b_ref, z_ref, xn_ref, xnt_ref):
        z = ALPHA * x_ref[...] + _dot(y_ref[...], w_ref[...])
        z_ref[...] = z
        xn = _ln(z, g_ref[...], b_ref[...])
        xn_ref[...] = xn
        xnt_ref[...] = xn.T.astype(BF16)

    row = lambda i: (i, 0)
    fixed = lambda i: (0, 0)
    return _call(
        body, exch, name=name, grid=(t // tm,),
        out_shape=[jax.ShapeDtypeStruct((t, d), F32), jax.ShapeDtypeStruct((t, d), F32),
                   jax.ShapeDtypeStruct((d, t), BF16)],
        in_specs=[pl.BlockSpec((tm, kk), row), pl.BlockSpec((kk, d), fixed), pl.BlockSpec((tm, d), row),
                  pl.BlockSpec((1, d), fixed), pl.BlockSpec((1, d), fixed)],
        out_specs=[pl.BlockSpec((tm, d), row), pl.BlockSpec((tm, d), row), pl.BlockSpec((d, tm), lambda i: (0, i))],
        semantics=("parallel",),
    )(y, w_out, x, ln_g, ln_b)


def _mix_out_bwd(dzb, w_out, name, exch=()):
    t, d = dzb.shape
    kk = w_out.shape[0]
    tm = _tile(t, 512, 128)

    def body(dz_ref, w_ref, dy_ref):
        dy_ref[...] = _dot_nt(dz_ref[...], w_ref[...])

    (out,), ex = _call(
        body, exch, name=name, grid=(t // tm,), out_shape=[jax.ShapeDtypeStruct((t, kk), F32)],
        in_specs=[pl.BlockSpec((tm, d), lambda i: (i, 0)), pl.BlockSpec((kk, d), lambda i: (0, 0))],
        out_specs=[pl.BlockSpec((tm, kk), lambda i: (i, 0))],
        semantics=("parallel",),
    )(dzb, w_out)
    return out, ex


def _loss_ln_bwd(z, target, ln_g, ln_b, bf16_scale, name):
    t, d = z.shape
    tm = _tile(t, 512, 8)

    def body(z_ref, t_ref, g_ref, b_ref, dz_ref, dzb_ref, dg_ref, db_ref, loss_ref):
        @pl.when(pl.program_id(0) == 0)
        def _():
            dg_ref[...] = jnp.zeros_like(dg_ref)
            db_ref[...] = jnp.zeros_like(db_ref)
            loss_ref[...] = jnp.zeros_like(loss_ref)

        xh, rstd = _ln_stats(z_ref[...])
        e = xh * g_ref[...] + b_ref[...] - t_ref[...]
        loss_ref[...] += 0.5 * jnp.sum(jnp.sum(e * e, axis=-1, keepdims=True) * (1.0 / d), axis=0, keepdims=True)
        dy = e * (1.0 / d)
        dz = _ln_bwd(dy * g_ref[...], xh, rstd)
        dz_ref[...] = dz
        dzb_ref[...] = (bf16_scale * dz).astype(BF16)
        dg_ref[...] += jnp.sum(dy * xh, axis=0, keepdims=True)
        db_ref[...] += jnp.sum(dy, axis=0, keepdims=True)

    row = lambda i: (i, 0)
    fixed = lambda i: (0, 0)
    return pl.pallas_call(
        body, name=name, grid=(t // tm,),
        out_shape=[jax.ShapeDtypeStruct((t, d), F32), jax.ShapeDtypeStruct((t, d), BF16),
                   jax.ShapeDtypeStruct((1, d), F32), jax.ShapeDtypeStruct((1, d), F32),
                   jax.ShapeDtypeStruct((8, 128), F32)],
        in_specs=[pl.BlockSpec((tm, d), row), pl.BlockSpec((tm, d), row), pl.BlockSpec((1, d), fixed),
                  pl.BlockSpec((1, d), fixed)],
        out_specs=[pl.BlockSpec((tm, d), row), pl.BlockSpec((tm, d), row), pl.BlockSpec((1, d), fixed),
                   pl.BlockSpec((1, d), fixed), pl.BlockSpec((8, 128), fixed)],
        compiler_params=_cparams("arbitrary"),
    )(z, target, ln_g, ln_b)


def _ln_bwd_call(z, dy, ln_g, bf16_scale, name, exch=()):
    t, d = z.shape
    tm = _tile(t, 512, 8)

    def body(z_ref, dy_ref, g_ref, dz_ref, dzb_ref, dg_ref, db_ref):
        @pl.when(pl.program_id(0) == 0)
        def _():
            dg_ref[...] = jnp.zeros_like(dg_ref)
            db_ref[...] = jnp.zeros_like(db_ref)

        xh, rstd = _ln_stats(z_ref[...])
        dy = dy_ref[...]
        dz = _ln_bwd(dy * g_ref[...], xh, rstd)
        dz_ref[...] = dz
        dzb_ref[...] = (bf16_scale * dz).astype(BF16)
        dg_ref[...] += jnp.sum(dy * xh, axis=0, keepdims=True)
        db_ref[...] += jnp.sum(dy, axis=0, keepdims=True)

    row = lambda i: (i, 0)
    fixed = lambda i: (0, 0)
    return _call(
        body, exch, name=name, grid=(t // tm,),
        out_shape=[jax.ShapeDtypeStruct((t, d), F32), jax.ShapeDtypeStruct((t, d), BF16),
                   jax.ShapeDtypeStruct((1, d), F32), jax.ShapeDtypeStruct((1, d), F32)],
        in_specs=[pl.BlockSpec((tm, d), row), pl.BlockSpec((tm, d), row), pl.BlockSpec((1, d), fixed)],
        out_specs=[pl.BlockSpec((tm, d), row), pl.BlockSpec((tm, d), row), pl.BlockSpec((1, d), fixed),
                   pl.BlockSpec((1, d), fixed)],
        semantics=("arbitrary",),
    )(z, dy, ln_g)


CONV_ROWS = 32
CONV_LANES = 512
SUBLANES = 8


def _fill_shifted(ext, shifted):
    rows = ext.shape[0] - SUBLANES
    for s in range(1, SUBLANES):
        for r in range(0, rows, CONV_ROWS):
            n = min(CONV_ROWS, rows - r)
            shifted[s - 1, r:r + n, :] = ext[r + s:r + s + n, :]


def _window(ext, shifted, lo, n, lanes=slice(None)):
    s = lo % SUBLANES
    return ext[lo:lo + n, lanes] if s == 0 else shifted[s - 1, lo - s:lo - s + n, lanes]


def _mixer_fwd(proj, conv_w, conv_b, cln_g, cln_b, sln_g, sln_b, sg_wm, sg_bb, name, exch=()):
    t = proj.shape[0]
    tm = _tile(t, 256, CHUNK)
    hb = tm // HALO
    nc = tm // CHUNK
    ch = CONV_CH

    def body(av_ref, ag_ref, bu_ref, bv_ref, hv_ref, hg_ref, cw_ref, cb_ref, lg_ref, lb_ref, sg_ref, sb_ref,
             w_ref, bb_ref, y_ref, yt_ref, c_ref, ext, ext_s):
        i = pl.program_id(0)
        halo = hv_ref[...] * _sigmoid(hg_ref[...])
        ext[0:HALO, :] = jnp.where(i > 0, halo, 0.0)
        ext[HALO:HALO + tm, :] = av_ref[...] * _sigmoid(ag_ref[...])
        _fill_shifted(ext, ext_s)
        for r in range(0, tm, CONV_ROWS):
            acc = jnp.zeros((CONV_ROWS, ch), F32) + cb_ref[...]
            for k in range(CONV_TAPS):
                lo = r + k + HALO - (CONV_TAPS - 1)
                acc = acc + cw_ref[k:k + 1, :] * _window(ext, ext_s, lo, CONV_ROWS)
            c_ref[r:r + CONV_ROWS, :] = acc
        a = _ln(c_ref[...], lg_ref[...], lb_ref[...])
        ya = a * _sigmoid(a)
        y_ref[:, 0:ch] = ya.astype(BF16)
        yt_ref[0:ch, :] = ya.T.astype(BF16)
        for h in range(HEADS):
            sl = slice(h * HEAD_DIM, (h + 1) * HEAD_DIM)
            u, _ = _gelu_and_grad(bu_ref[:, sl])
            v, _ = _gelu_and_grad(bv_ref[:, sl])
            vn = _ln(v, sg_ref[h:h + 1, :], sb_ref[h:h + 1, :])
            vn3 = vn.astype(BF16).reshape(nc, CHUNK, HEAD_DIM)
            wb = jnp.broadcast_to(w_ref[h][None], (nc, CHUNK, CHUNK))
            mixed = jnp.einsum("cts,csd->ctd", wb, vn3, preferred_element_type=F32) + bb_ref[h][None]
            yb = u * mixed.reshape(tm, HEAD_DIM)
            y_ref[:, ch + h * HEAD_DIM:ch + (h + 1) * HEAD_DIM] = yb.astype(BF16)
            yt_ref[ch + h * HEAD_DIM:ch + (h + 1) * HEAD_DIM, :] = yb.T.astype(BF16)

    col = lambda cidx: (lambda i: (i, cidx))
    prev = lambda cidx: (lambda i: (jnp.maximum(i * hb - 1, 0), cidx))
    fix2 = lambda i: (0, 0)
    fix3 = lambda i: (0, 0, 0)
    return _call(
        body, exch, name=name, grid=(t // tm,),
        out_shape=[jax.ShapeDtypeStruct((t, 2 * ch), BF16), jax.ShapeDtypeStruct((2 * ch, t), BF16),
                   jax.ShapeDtypeStruct((t, ch), F32)],
        in_specs=[pl.BlockSpec((tm, ch), col(0)), pl.BlockSpec((tm, ch), col(1)), pl.BlockSpec((tm, ch), col(2)),
                  pl.BlockSpec((tm, ch), col(3)), pl.BlockSpec((HALO, ch), prev(0)), pl.BlockSpec((HALO, ch), prev(1)),
                  pl.BlockSpec((CONV_TAPS, ch), fix2), pl.BlockSpec((1, ch), fix2), pl.BlockSpec((1, ch), fix2),
                  pl.BlockSpec((1, ch), fix2), pl.BlockSpec((HEADS, HEAD_DIM), fix2), pl.BlockSpec((HEADS, HEAD_DIM), fix2),
                  pl.BlockSpec((HEADS, CHUNK, CHUNK), fix3), pl.BlockSpec((HEADS, CHUNK, HEAD_DIM), fix3)],
        out_specs=[pl.BlockSpec((tm, 2 * ch), lambda i: (i, 0)), pl.BlockSpec((2 * ch, tm), lambda i: (0, i)),
                   pl.BlockSpec((tm, ch), lambda i: (i, 0))],
        scratch_shapes=[pltpu.VMEM((HALO + tm, ch), F32), pltpu.VMEM((SUBLANES - 1, HALO + tm, ch), F32)],
        semantics=("parallel",),
    )(proj, proj, proj, proj, proj, proj, conv_w, conv_b, cln_g, cln_b, sln_g, sln_b, sg_wm, sg_bb)


def _mixer_bwd(proj, conv_c, dy, conv_w, cln_g, cln_b, sln_g, sln_b, sg_wm, sg_wmt, sg_bb, name, exch=()):
    t = proj.shape[0]
    tm = _tile(t, 256, CHUNK)
    hb = tm // HALO
    nc = tm // CHUNK
    nt = t // tm
    ch = CONV_CH
    last_halo = t // HALO - 1

    def body(av_ref, ag_ref, bu_ref, bv_ref, hv_ref, hg_ref, c_ref, cn_ref, dya_ref, dyan_ref, dyb_ref,
             cw_ref, lg_ref, lb_ref, sg_ref, sb_ref, w_ref, wt_ref, bb_ref,
             dp_ref, dcw_ref, dcb_ref, dlg_ref, dlb_ref, dsg_ref, dsb_ref, dw_ref, dbs_ref,
             ext_h, ext_dc, ext_hs, ext_dcs, acc_cw):
        i = pl.program_id(0)

        @pl.when(i == 0)
        def _():
            acc_cw[...] = jnp.zeros_like(acc_cw)
            for ref in (dcb_ref, dlg_ref, dlb_ref, dsg_ref, dsb_ref, dw_ref, dbs_ref):
                ref[...] = jnp.zeros_like(ref)

        lg = lg_ref[...]
        lb = lb_ref[...]

        def conv_ln_bwd(c, dya):
            xh, rstd = _ln_stats(c)
            a = xh * lg + lb
            da = dya * _silu_grad(a)
            return _ln_bwd(da * lg, xh, rstd), da, xh

        fold = lambda v: jnp.sum(v.reshape(CONV_ROWS // SUBLANES, SUBLANES, ch), axis=0)
        s_lg = s_lb = s_cb = jnp.zeros((SUBLANES, ch), F32)
        for r in range(0, tm, CONV_ROWS):
            dc, da, xh = conv_ln_bwd(c_ref[r:r + CONV_ROWS, :], dya_ref[r:r + CONV_ROWS, :])
            ext_dc[r:r + CONV_ROWS, :] = dc
            s_lg, s_lb, s_cb = s_lg + fold(da * xh), s_lb + fold(da), s_cb + fold(dc)
        dlg_ref[...] += jnp.sum(s_lg, axis=0, keepdims=True)
        dlb_ref[...] += jnp.sum(s_lb, axis=0, keepdims=True)
        dcb_ref[...] += jnp.sum(s_cb, axis=0, keepdims=True)
        dcn, _, _ = conv_ln_bwd(cn_ref[...], dyan_ref[...])
        ext_dc[tm:tm + HALO, :] = jnp.where(i < nt - 1, dcn, 0.0)
        halo = hv_ref[...] * _sigmoid(hg_ref[...])
        ext_h[0:HALO, :] = jnp.where(i > 0, halo, 0.0)
        ext_h[HALO:HALO + tm, :] = av_ref[...] * _sigmoid(ag_ref[...])
        _fill_shifted(ext_h, ext_hs)
        _fill_shifted(ext_dc, ext_dcs)
        for r, c0 in [(r, c0) for r in range(0, tm, CONV_ROWS) for c0 in range(0, ch, CONV_LANES)]:
            rows, lanes = slice(r, r + CONV_ROWS), slice(c0, c0 + CONV_LANES)
            dcr = ext_dc[rows, lanes]
            acc = jnp.zeros((CONV_ROWS, CONV_LANES), F32)
            for k in range(CONV_TAPS):
                lo = r + k + HALO - (CONV_TAPS - 1)
                prod = dcr * _window(ext_h, ext_hs, lo, CONV_ROWS, lanes)
                acc_cw[k, :, lanes] += jnp.sum(prod.reshape(CONV_ROWS // SUBLANES, SUBLANES, CONV_LANES), axis=0)
                hi = r + (CONV_TAPS - 1) - k
                acc = acc + cw_ref[k:k + 1, lanes] * _window(ext_dc, ext_dcs, hi, CONV_ROWS, lanes)
            sg_r = _sigmoid(ag_ref[rows, lanes])
            av_r = av_ref[rows, lanes]
            dp_ref[rows, lanes] = (acc * sg_r).astype(BF16)
            dp_ref[rows, slice(ch + c0, ch + c0 + CONV_LANES)] = (acc * av_r * sg_r * (1.0 - sg_r)).astype(BF16)

        @pl.when(i == nt - 1)
        def _():
            dcw_ref[...] = jnp.sum(acc_cw[...], axis=1)

        tril = (lax.broadcasted_iota(jnp.int32, (CHUNK, CHUNK), 0)
                >= lax.broadcasted_iota(jnp.int32, (CHUNK, CHUNK), 1)).astype(F32)
        for h in range(HEADS):
            sl = slice(h * HEAD_DIM, (h + 1) * HEAD_DIM)
            u, du_dx = _gelu_and_grad(bu_ref[:, sl])
            v, dv_dx = _gelu_and_grad(bv_ref[:, sl])
            xhv, rstdv = _ln_stats(v)
            gh = sg_ref[h:h + 1, :]
            vn3 = (xhv * gh + sb_ref[h:h + 1, :]).astype(BF16).reshape(nc, CHUNK, HEAD_DIM)
            wb = jnp.broadcast_to(w_ref[h][None], (nc, CHUNK, CHUNK))
            mixed = jnp.einsum("cts,csd->ctd", wb, vn3, preferred_element_type=F32) + bb_ref[h][None]
            dyb = dyb_ref[:, sl]
            d_u = dyb * mixed.reshape(tm, HEAD_DIM)
            dm = dyb * u
            dm3 = dm.reshape(nc, CHUNK, HEAD_DIM)
            dbs_ref[h:h + 1, :] += jnp.sum(jnp.sum(dm3, axis=0).T, axis=0, keepdims=True)
            dm3b = dm3.astype(BF16)
            dw_h = jnp.sum(jnp.einsum("ctd,csd->cts", dm3b, vn3, preferred_element_type=F32), axis=0)
            dw_ref[h] += dw_h * tril
            wtb = jnp.broadcast_to(wt_ref[h][None], (nc, CHUNK, CHUNK))
            d_vn = jnp.einsum("cst,ctd->csd", wtb, dm3b, preferred_element_type=F32).reshape(tm, HEAD_DIM)
            dsg_ref[h:h + 1, :] += jnp.sum(d_vn * xhv, axis=0, keepdims=True)
            dsb_ref[h:h + 1, :] += jnp.sum(d_vn, axis=0, keepdims=True)
            dv = _ln_bwd(d_vn * gh, xhv, rstdv)
            dp_ref[:, 2 * ch + h * HEAD_DIM:2 * ch + (h + 1) * HEAD_DIM] = (d_u * du_dx).astype(BF16)
            dp_ref[:, 3 * ch + h * HEAD_DIM:3 * ch + (h + 1) * HEAD_DIM] = (dv * dv_dx).astype(BF16)

    col = lambda cidx: (lambda i: (i, cidx))
    prev = lambda cidx: (lambda i: (jnp.maximum(i * hb - 1, 0), cidx))
    nxt = lambda i: (jnp.minimum((i + 1) * hb, last_halo), 0)
    fix2 = lambda i: (0, 0)
    fix3 = lambda i: (0, 0, 0)
    out_shape = [jax.ShapeDtypeStruct((t, 4 * ch), BF16), jax.ShapeDtypeStruct((CONV_TAPS, ch), F32),
                 jax.ShapeDtypeStruct((1, ch), F32), jax.ShapeDtypeStruct((1, ch), F32), jax.ShapeDtypeStruct((1, ch), F32),
                 jax.ShapeDtypeStruct((HEADS, HEAD_DIM), F32), jax.ShapeDtypeStruct((HEADS, HEAD_DIM), F32),
                 jax.ShapeDtypeStruct((HEADS, CHUNK, CHUNK), F32), jax.ShapeDtypeStruct((HEADS, CHUNK), F32)]
    out_specs = [pl.BlockSpec((tm, 4 * ch), lambda i: (i, 0)), pl.BlockSpec((CONV_TAPS, ch), fix2),
                 pl.BlockSpec((1, ch), fix2), pl.BlockSpec((1, ch), fix2), pl.BlockSpec((1, ch), fix2),
                 pl.BlockSpec((HEADS, HEAD_DIM), fix2), pl.BlockSpec((HEADS, HEAD_DIM), fix2),
                 pl.BlockSpec((HEADS, CHUNK, CHUNK), fix3), pl.BlockSpec((HEADS, CHUNK), fix2)]
    in_specs = [pl.BlockSpec((tm, ch), col(0)), pl.BlockSpec((tm, ch), col(1)), pl.BlockSpec((tm, ch), col(2)),
                pl.BlockSpec((tm, ch), col(3)), pl.BlockSpec((HALO, ch), prev(0)), pl.BlockSpec((HALO, ch), prev(1)),
                pl.BlockSpec((tm, ch), col(0)), pl.BlockSpec((HALO, ch), nxt),
                pl.BlockSpec((tm, ch), col(0)), pl.BlockSpec((HALO, ch), nxt), pl.BlockSpec((tm, ch), col(1)),
                pl.BlockSpec((CONV_TAPS, ch), fix2), pl.BlockSpec((1, ch), fix2), pl.BlockSpec((1, ch), fix2),
                pl.BlockSpec((HEADS, HEAD_DIM), fix2), pl.BlockSpec((HEADS, HEAD_DIM), fix2),
                pl.BlockSpec((HEADS, CHUNK, CHUNK), fix3), pl.BlockSpec((HEADS, CHUNK, CHUNK), fix3),
                pl.BlockSpec((HEADS, CHUNK, HEAD_DIM), fix3)]
    return _call(
        body, exch, name=name, grid=(nt,), out_shape=out_shape, in_specs=in_specs, out_specs=out_specs,
        scratch_shapes=[pltpu.VMEM((HALO + tm, ch), F32), pltpu.VMEM((tm + HALO, ch), F32),
                        pltpu.VMEM((SUBLANES - 1, HALO + tm, ch), F32), pltpu.VMEM((SUBLANES - 1, tm + HALO, ch), F32),
                        pltpu.VMEM((CONV_TAPS, 8, ch), F32)],
        semantics=("arbitrary",),
    )(proj, proj, proj, proj, proj, proj, conv_c, conv_c, dy, dy, dy,
      conv_w, cln_g, cln_b, sln_g, sln_b, sg_wm, sg_wmt, sg_bb)


def _pair_sum(parts, from_sibling, core_chip, name):
    _, r, cc = parts.shape
    tr = _tile(r, max(16, (1 << 22) // (2 * cc)), 16)

    def body(cc_ref, p_ref, s_ref, o_ref, own_ref):
        q = (p_ref[...].astype(F32) + s_ref[...].astype(F32)).astype(BF16)
        o_ref[...] = q

        @pl.when(pl.program_id(1) == cc_ref[1])
        def _():
            own_ref[...] = q[0]

    grid_spec = pltpu.PrefetchScalarGridSpec(
        num_scalar_prefetch=1, grid=(r // tr, 4),
        in_specs=[pl.BlockSpec((1, tr, cc), lambda i, j, cc_ref: (2 * j + cc_ref[0], i, 0)),
                  pl.BlockSpec((1, tr, cc), lambda i, j, cc_ref: (j, i, 0))],
        out_specs=[pl.BlockSpec((1, tr, cc), lambda i, j, cc_ref: (j, i, 0)),
                   pl.BlockSpec((tr, cc), lambda i, j, cc_ref: (i, 0))])
    return pl.pallas_call(
        body, name=name, grid_spec=grid_spec,
        out_shape=[jax.ShapeDtypeStruct((4, r, cc), BF16), jax.ShapeDtypeStruct((r, cc), BF16)],
        compiler_params=_cparams("parallel", "arbitrary"),
    )(core_chip, parts, from_sibling)


def _adamw_math(w, g, m, v):
    m = ADAM_B1 * m + (1.0 - ADAM_B1) * g
    v = ADAM_B2 * v + (1.0 - ADAM_B2) * (g * g)
    m_hat = m / (1.0 - ADAM_B1 ** ADAM_STEP)
    v_hat = v / (1.0 - ADAM_B2 ** ADAM_STEP)
    delta = -ADAM_LR * (m_hat / (jnp.sqrt(v_hat) + ADAM_EPS) + ADAM_WD * w)
    return delta, m, v


def _adamw_tile(in_refs, out_refs):
    w_ref, m_ref, v_ref, q_ref, o_ref = in_refs
    g = q_ref[...].astype(F32)
    for k in range(3):
        g = g + o_ref[k].astype(F32)
    d, mm, vv = _adamw_math(w_ref[...], g, m_ref[...], v_ref[...])
    for ref, val in zip(out_refs, (g, d, mm, vv)):
        ref[...] = val


def _adamw_side(w, m, v, chip_part, from_chips, max_tiles):
    r, cc = w.shape
    n = max(k for k in range(1, max_tiles + 1) if r % k == 0 and (r // k) % 16 == 0)
    tr = r // n
    row = ((tr, cc), lambda s: (s, 0))
    return _Side([w, m, v, chip_part, from_chips], [row, row, row, row, ((3, tr, cc), lambda s: (0, s, 0))],
                 [jax.ShapeDtypeStruct((r, cc), F32)] * 4, [row] * 4, n, _adamw_tile)


def _adamw_sharded(w, m, v, chip_part, from_chips, name):
    r, cc = w.shape
    tr = _tile(r, max(16, (1 << 20) // (4 * cc) * 2), 16)

    def body(*refs):
        _adamw_tile(refs[:5], refs[5:])

    row = pl.BlockSpec((tr, cc), lambda i: (i, 0))
    return pl.pallas_call(
        body, name=name, grid=(r // tr,), out_shape=[jax.ShapeDtypeStruct((r, cc), F32)] * 4,
        in_specs=[row, row, row, row, pl.BlockSpec((3, tr, cc), lambda i: (0, i, 0))], out_specs=[row] * 4,
        compiler_params=_cparams("parallel"),
    )(w, m, v, chip_part, from_chips)


def _adamw_small(w, g, m, v, name):
    r, cc = w.shape

    def body(w_ref, g_ref, m_ref, v_ref, d_out, m_out, v_out):
        d, mm, vv = _adamw_math(w_ref[...], g_ref[...], m_ref[...], v_ref[...])
        d_out[...] = d
        m_out[...] = mm
        v_out[...] = vv

    full = pl.BlockSpec((r, cc), lambda i: (0, 0))
    return pl.pallas_call(
        body, name=name, grid=(1,), out_shape=[jax.ShapeDtypeStruct((r, cc), F32)] * 3,
        in_specs=[full] * 4, out_specs=[full] * 3, compiler_params=_cparams("arbitrary"),
    )(w, g, m, v)


SMALL = ("ln1_g", "ln1_b", "conv_b", "conv_ln_g", "conv_ln_b", "sg_ln_g", "sg_ln_b", "sg_w", "sg_b",
         "ln2_g", "ln2_b", "ln3_g", "ln3_b")
ORDER = ("ffn1_w_gate_up", "ffn1_w_down", "ln1_g", "ln1_b", "mix_w_in", "conv_w", "conv_b", "conv_ln_g", "conv_ln_b",
         "sg_ln_g", "sg_ln_b", "sg_w", "sg_b", "mix_w_out", "ln2_g", "ln2_b", "ffn2_w_gate_up", "ffn2_w_down",
         "ln3_g", "ln3_b")


def _rows128(a):
    return a.reshape(-1, 128)


def kernel(x, ffn1_w_gate_up, ffn1_w_down, ln1_g, ln1_b, mix_w_in, conv_w, conv_b, conv_ln_g, conv_ln_b, sg_ln_g, sg_ln_b, sg_w, sg_b, mix_w_out, ln2_g, ln2_b, ffn2_w_gate_up, ffn2_w_down, ln3_g, ln3_b, loss_target, m_ffn1_w_gate_up, m_ffn1_w_down, m_ln1_g, m_ln1_b, m_mix_w_in, m_conv_w, m_conv_b, m_conv_ln_g, m_conv_ln_b, m_sg_ln_g, m_sg_ln_b, m_sg_w, m_sg_b, m_mix_w_out, m_ln2_g, m_ln2_b, m_ffn2_w_gate_up, m_ffn2_w_down, m_ln3_g, m_ln3_b, v_ffn1_w_gate_up, v_ffn1_w_down, v_ln1_g, v_ln1_b, v_mix_w_in, v_conv_w, v_conv_b, v_conv_ln_g, v_conv_ln_b, v_sg_ln_g, v_sg_ln_b, v_sg_w, v_sg_b, v_mix_w_out, v_ln2_g, v_ln2_b, v_ffn2_w_gate_up, v_ffn2_w_down, v_ln3_g, v_ln3_b):
    args = dict(locals())
    w = {n: args[n][0] for n in ORDER}
    mom = {n: args["m_" + n][0] for n in ORDER}
    var = {n: args["v_" + n][0] for n in ORDER}
    x0 = x[0]
    target = loss_target[0]
    t, d = x0.shape
    my_x, my_y, my_c = lax.axis_index("x"), lax.axis_index("y"), lax.axis_index("c")
    my_chip = (2 * my_x + my_y).astype(jnp.int32).reshape(1)
    my_core = my_c.astype(jnp.int32).reshape(1)
    me = 4 * my_x + 2 * my_y + my_c

    big = ("ffn1_w_gate_up", "ffn1_w_down", "mix_w_in", "mix_w_out", "ffn2_w_gate_up", "ffn2_w_down")
    sh = {n: w[n].astype(BF16) for n in big}
    f2s = sh["ffn2_w_gate_up"].shape[1]
    order = jnp.stack([4 * p[0] + 2 * p[1] + p[2] for p in _visit_order(my_x, my_y, my_c)]).astype(jnp.int32)
    gu1, x0t, (wgu1, wd1, conv_w_all) = _gather_and_gate_up(
        x0, [sh["ffn1_w_gate_up"], sh["ffn1_w_down"], w["conv_w"]], [True, True, False], order, "ffn1_gate_up_fwd")
    wd1 = wd1.reshape(-1, d)
    conv_w_full = jnp.transpose(conv_w_all, (1, 0, 2)).reshape(CONV_TAPS, CONV_CH)
    tril = jnp.tril(jnp.ones((CHUNK, CHUNK), F32))
    sg_wm = w["sg_w"] * tril
    sg_wm_b = sg_wm.astype(BF16)
    sg_wmt_b = jnp.swapaxes(sg_wm, 1, 2).astype(BF16)
    sg_bb = jnp.broadcast_to(w["sg_b"][:, :, None], (HEADS, CHUNK, HEAD_DIM))
    row = lambda a: a.reshape(1, -1)

    d2 = [sh["ffn2_w_down"]]
    d2_first = d2[0].shape[0] // 2 // 16 * 16
    d2_top, d2_bottom = (0, d2_first), (d2_first, d2[0].shape[0] - d2_first)
    (h1t, z1, x1), ((g_in, g_out), (g_d2,)) = _ffn_down_fwd(
        gu1, x0, wd1, row(w["ln1_g"]), row(w["ln1_b"]), "ffn1_down_fwd",
        exch=[_gather_first([sh["mix_w_in"], sh["mix_w_out"]], [True, False]),
              _gather_first(d2, [False], rows=d2_top)])
    in_cols = sh["mix_w_in"].shape[1]
    x1t, ((w_in, w_out), (g_d2,)) = _transpose_bf16(
        x1, "x1_transpose", exch=[_gather_forward([g_in, g_out], [True, False], [in_cols, None]),
                                  _gather_forward([g_d2], [False], [None], rows=d2_top)])
    w_out = w_out.reshape(-1, d)
    top, bottom = (0, d // 2), (d // 2, d // 2)
    gu2 = [sh["ffn2_w_gate_up"]]
    proj, ((g_gu2,),) = _mix_in_proj(x1, w_in, "mix_in_fwd", exch=[_gather_first(gu2, [True], rows=top)])
    (y, yt, conv_c), ((g_gu2,),) = _mixer_fwd(
        proj, conv_w_full, row(w["conv_b"]), row(w["conv_ln_g"]), row(w["conv_ln_b"]),
        w["sg_ln_g"], w["sg_ln_b"], sg_wm_b, sg_bb, "mixer_fwd",
        exch=[_both(_gather_first(gu2, [True], rows=bottom, into=[g_gu2]),
                    _gather_forward([g_gu2], [True], [f2s], rows=top))])
    (z2, x2, x2t), ((wgu2,), (g_d2,)) = _mix_out_fwd(
        y, w_out, x1, row(w["ln2_g"]), row(w["ln2_b"]), "mix_out_fwd",
        exch=[_gather_forward([g_gu2], [True], [f2s], rows=bottom),
              _gather_first(d2, [False], rows=d2_bottom, into=[g_d2])])
    (wd2,) = _exchange_alone(_gather_forward([g_d2], [False], [None], rows=d2_bottom), "ffn2_down_gather_forward")
    wd2 = wd2.reshape(-1, d)
    grads = {}
    (g2, u2, h2t, dz3, do2, grads["ln3_g"], grads["ln3_b"], loss_tile), _ = _ffn_fwd_loss(
        x2, wgu2, wd2, row(w["ln3_g"]), row(w["ln3_b"]), target, "ffn2_fwd_loss")

    f = wd1.shape[0]
    dn = _tile(d, 1024, 128)
    core_chip = jnp.concatenate([my_core, my_chip])
    pair = lambda p, s, label: _pair_sum(p, s, core_chip, "pair_sum_" + label)
    adamw = lambda n, own, got, steps: _adamw_side(w[n], mom[n], var[n], own, got, steps)
    m_tiles = d // _tile(d, 512, 16)
    gu_first = d * 3 // 4 // 16 * 16
    out = {}
    p_d2, _ = _weight_grad(h2t, do2, dn, 512, "ffn2_dw_down")
    p_d2 = p_d2.reshape(N_DEV, f // N_DEV, d)
    (dg2, du2, dx2), ((s_d2,),) = _ffn_bwd(dz3, do2, g2, u2, wgu2, wd2, "ffn2_bwd", exch=[_rs_sibling([p_d2])])
    q_d2, own_d2 = pair(p_d2, s_d2, "ffn2_down")
    d_rows = q_d2.shape[1]
    d_half = d_rows // 2 // 16 * 16
    p_gu2, ((r_d2,),) = _weight_grad(x2t, dg2, f2s, 512, "ffn2_dw_gate", blocks=N_DEV,
                                     exch=[_rs_chips([q_d2], rows=(0, d_half))])
    p_gu2, ((r_d2,),) = _weight_grad(x2t, du2, f2s, 512, "ffn2_dw_up", blocks=N_DEV, block_offset=4, into=p_gu2,
                                     exch=[_rs_chips([q_d2], rows=(d_half, d_rows - d_half), into=[r_d2])])
    (dz2, dz2b, grads["ln2_g"], grads["ln2_b"]), ((s_gu2,),) = _ln_bwd_call(
        z2, dx2, row(w["ln2_g"]), 1.0, "ln2_bwd", exch=[_rs_sibling([p_gu2])])
    q_gu2, own_gu2 = pair(p_gu2, s_gu2, "ffn2_gate_up")
    dy, _ = _mix_out_bwd(dz2b, w_out, "mix_out_bwd")
    p_out, _ = _weight_grad(yt, dz2b, dn, 512, "mix_out_dw")
    p_out = p_out.reshape(N_DEV, -1, d)
    (dproj, grads["conv_w"], grads["conv_b"], grads["conv_ln_g"], grads["conv_ln_b"], grads["sg_ln_g"],
     grads["sg_ln_b"], grads["sg_w"], grads["sg_b"]), ((r_gu2,),) = _mixer_bwd(
        proj, conv_c, dy, conv_w_full, row(w["conv_ln_g"]), row(w["conv_ln_b"]), w["sg_ln_g"], w["sg_ln_b"],
        sg_wm_b, sg_wmt_b, sg_bb, "mixer_bwd", exch=[_rs_chips([q_gu2], rows=(0, gu_first))])
    dx1, ((s_out,), (r_gu2,)) = _mix_in_bwd(
        dproj, w_in, dz2, "mix_in_bwd",
        exch=[_rs_sibling([p_out]), _rs_chips([q_gu2], rows=(gu_first, d - gu_first), into=[r_gu2])])
    p_in, (out["ffn2_w_gate_up"], out["ffn2_w_down"]) = _weight_grad(
        x1t, dproj, in_cols, 512, "mix_in_dw", blocks=N_DEV,
        exch=[adamw("ffn2_w_gate_up", own_gu2, r_gu2, N_DEV * m_tiles), adamw("ffn2_w_down", own_d2, r_d2, N_DEV * m_tiles)])
    (dz1, do1, grads["ln1_g"], grads["ln1_b"]), ((s_in,),) = _ln_bwd_call(
        z1, dx1, row(w["ln1_g"]), 0.5, "ln1_bwd", exch=[_rs_sibling([p_in])])
    q_out, own_out = pair(p_out, s_out, "mix_out")
    q_in, own_in = pair(p_in, s_in, "mix_in")
    small_parts = [_rows128(grads[n]) for n in SMALL]
    packed = jnp.concatenate(small_parts + [_rows128(grads["conv_w"]), loss_tile], axis=0)
    p_d1, ((r_in,),) = _weight_grad(h1t, do1, dn, 512, "ffn1_dw_down", exch=[_rs_chips([q_in])])
    p_d1 = p_d1.reshape(N_DEV, f // N_DEV, d)
    (dg1, du1), ((s_d1,), (r_out,), (small_all,)) = _ffn_bwd_act(
        do1, gu1, wd1, "ffn1_bwd_act",
        exch=[_rs_sibling([p_d1]), _rs_chips([q_out]), _small_gather(packed)])
    q_d1, own_d1 = pair(p_d1, s_d1, "ffn1_down")
    p_gu1, ((r_d1,),) = _weight_grad(x0t, dg1, f2s, 512, "ffn1_dw_gate", blocks=N_DEV, exch=[_rs_chips([q_d1])])
    p_gu1, (out["mix_w_in"], out["mix_w_out"]) = _weight_grad(
        x0t, du1, f2s, 512, "ffn1_dw_up", blocks=N_DEV, block_offset=4, into=p_gu1,
        exch=[adamw("mix_w_in", own_in, r_in, 4 * m_tiles), adamw("mix_w_out", own_out, r_out, 4 * m_tiles)])
    (s_gu1,) = _exchange_alone(_rs_sibling([p_gu1]), "ffn1_gate_up_sibling_exchange")
    q_gu1, own_gu1 = pair(p_gu1, s_gu1, "ffn1_gate_up")
    (grad_x,), ((r_gu1,),) = _ffn_bwd_dx(dz1, dg1, du1, wgu1, "ffn1_bwd_dx", exch=[_rs_chips([q_gu1])])
    for n, own, got in (("ffn1_w_down", own_d1, r_d1), ("ffn1_w_gate_up", own_gu1, r_gu1)):
        out[n] = _adamw_sharded(w[n], mom[n], var[n], own, got, "adamw_" + n)

    cw_rows = CONV_TAPS * CONV_CH // 128
    total = _sum_over_devices(small_all)
    offs = [0]
    for p in small_parts:
        offs.append(offs[-1] + p.shape[0])
    n_small = offs[-1]
    loss = total[n_small + cw_rows, 0]
    g_conv_w = lax.dynamic_slice_in_dim(total[n_small:n_small + cw_rows].reshape(CONV_TAPS, CONV_CH),
                                        me * (CONV_CH // N_DEV), CONV_CH // N_DEV, axis=1)
    pad8 = lambda a: jnp.pad(a, ((0, -a.shape[0] % 8), (0, 0)))
    pack = lambda tree, cw: jnp.concatenate([_rows128(tree[n]) for n in SMALL] + [pad8(cw)], axis=0)
    g_pack = jnp.concatenate([total[:n_small], pad8(g_conv_w)], axis=0)
    d_pack, m_pack, v_pack = _adamw_small(pack(w, w["conv_w"]), g_pack, pack(mom, mom["conv_w"]),
                                          pack(var, var["conv_w"]), "adamw_small")
    for k, n in enumerate(SMALL):
        sl = slice(offs[k], offs[k + 1])
        shp = w[n].shape
        out[n] = (total[sl].reshape(shp), d_pack[sl].reshape(shp), m_pack[sl].reshape(shp), v_pack[sl].reshape(shp))
    sl = slice(n_small, n_small + CONV_TAPS)
    out["conv_w"] = (g_conv_w, d_pack[sl], m_pack[sl], v_pack[sl])

    lead = lambda a: a[None]
    res = [loss, grad_x[None]]
    for kind in range(4):
        res += [lead(out[n][kind]) for n in ORDER]
    return tuple(res)
```

```python
import functools
import math

import jax
import jax.numpy as jnp
from jax import lax
from jax.experimental import pallas as pl
from jax.experimental.pallas import tpu as pltpu

F32, BF16 = jnp.float32, jnp.bfloat16
MESH = pl.DeviceIdType.MESH
ANY = pl.BlockSpec(memory_space=pl.ANY)

N_DEV = 8
LN_EPS = 1e-5
ALPHA = 2.0 ** 0.25
CONV_CH = 1024
CONV_TAPS = 31
HALO = 32
HEADS = 8
HEAD_DIM = 128
CHUNK = 128
ADAM_LR, ADAM_B1, ADAM_B2, ADAM_EPS, ADAM_WD, ADAM_STEP = 0.001, 0.9, 0.999, 1e-08, 0.01, 10
V7X_VMEM_LIMIT = 62 * 2 ** 20
EPILOGUE_ROWS = 128

def _cparams(*sem):
    return pltpu.CompilerParams(dimension_semantics=sem, vmem_limit_bytes=V7X_VMEM_LIMIT)


def _tile(n, pref, mult):
    best = None
    for t in range(mult, min(n, pref) + 1, mult):
        if n % t == 0:
            best = t
    return best if best is not None else n


def _dot(a, b):
    return jnp.dot(a, b, preferred_element_type=F32)


def _dot_nt(a, b):
    return lax.dot_general(a, b, (((1,), (1,)), ((), ())), preferred_element_type=F32)


def _sigmoid(x):
    return 1.0 / (1.0 + jnp.exp(-x))


def _ln_stats(z):
    mu = jnp.mean(z, axis=-1, keepdims=True)
    zc = z - mu
    var = jnp.mean(zc * zc, axis=-1, keepdims=True)
    rstd = lax.rsqrt(var + LN_EPS)
    return zc * rstd, rstd


def _ln(z, g, b):
    xh, _ = _ln_stats(z)
    return xh * g + b


def _ln_bwd(dxh, xh, rstd):
    m1 = jnp.mean(dxh, axis=-1, keepdims=True)
    m2 = jnp.mean(dxh * xh, axis=-1, keepdims=True)
    return rstd * (dxh - m1 - xh * m2)


_GK = math.sqrt(2.0 / math.pi)
_GA = 0.044715


def _gelu_and_grad(x):
    x2 = x * x
    t = jnp.tanh(_GK * (x + _GA * x * x2))
    y = 0.5 * x * (1.0 + t)
    dy = 0.5 * (1.0 + t) + 0.5 * x * (1.0 - t * t) * (_GK * (1.0 + 3.0 * _GA * x2))
    return y, dy


def _silu_grad(a):
    s = _sigmoid(a)
    return s * (1.0 + a * (1.0 - s))


def _place():
    return lax.axis_index("x"), lax.axis_index("y"), lax.axis_index("c")


def _other_chips(x, y):
    return [(1 - x, y), (x, 1 - y), (1 - x, 1 - y)]


def _visit_order(x, y, c):
    chips = _other_chips(x, y)
    return [(x, y, c), (x, y, 1 - c), (*chips[0], c), (*chips[1], c), (*chips[0], 1 - c), (*chips[1], 1 - c),
            (*chips[2], c), (*chips[2], 1 - c)]


def _gather_and_gate_up(xb, shards, relayed, order, name):
    n = len(shards)
    N_COPIES = 10
    t, d = xb.shape
    cols = shards[0].shape[1]
    tm = _tile(t, 1024, 128)
    ni = t // tm
    col_major = [True] + [False] * (n - 1)

    def body(order_ref, x_ref, *refs):
        srcs, gu_ref, xt_ref, dsts = refs[:n], refs[n], refs[n + 1], refs[n + 2:2 * n + 2]
        wbuf, send_sems, recv_sems, local_sems, load_sem = refs[2 * n + 2:]
        b, i = pl.program_id(0), pl.program_id(1)
        x, y, c = _place()
        me, sib = (x, y, c), (x, y, 1 - c)
        chips = _other_chips(x, y)

        near_x, near_y, far = chips

        def slot(w, p, band=None):
            half = shards[w].shape[0] // 2
            rows = None if band is None else (band * half, half)
            return _block_slot(dsts[w], col_major[w], shards[w].shape[1], p, rows)

        def copy(w, s, block, to, band=None, from_src=False):
            return pltpu.make_async_remote_copy(
                src_ref=srcs[w] if from_src else slot(w, block, band), dst_ref=slot(w, block, band),
                send_sem=send_sems.at[N_COPIES * w + s], recv_sem=recv_sems.at[N_COPIES * w + s],
                device_id=to, device_id_type=MESH)

        def own(w):
            return pltpu.make_async_copy(srcs[w], slot(w, me), local_sems.at[w])

        def sends(w):
            out = [copy(w, 0, me, sib, from_src=True), copy(w, 1, me, (*near_x, c), from_src=True),
                   copy(w, 2, me, (*near_y, c), from_src=True)]
            if not relayed[w]:
                out.append(copy(w, 3, me, (*far, c), from_src=True))
            return out

        def passed_on(w):
            out = [copy(w, 4, (*near_x, c), sib), copy(w, 5, (*near_y, c), sib)]
            if relayed[w]:
                out += [copy(w, 6, (*far, c), sib, band=0), copy(w, 9, (*far, c), sib, band=1),
                        copy(w, 7, (*near_x, c), (*near_y, c), band=0), copy(w, 8, (*near_y, c), (*near_x, c), band=1)]
            else:
                out.append(copy(w, 6, (*far, c), sib))
            return out

        def start_sends(w):
            own(w).start()
            for cp in sends(w):
                cp.start()

        def got_near_x(w):
            copy(w, 1, (*near_x, c), me).wait_recv()
            copy(w, 4, (*near_x, c), sib).start()
            if relayed[w]:
                copy(w, 7, (*near_x, c), (*near_y, c), band=0).start()

        def got_near_y(w):
            copy(w, 2, (*near_y, c), me).wait_recv()
            copy(w, 5, (*near_y, c), sib).start()
            if relayed[w]:
                copy(w, 8, (*near_y, c), (*near_x, c), band=1).start()

        def got_far(w):
            if relayed[w]:
                copy(w, 7, (*far, c), me, band=0).wait_recv()
                copy(w, 6, (*far, c), sib, band=0).start()
                copy(w, 8, (*far, c), me, band=1).wait_recv()
                copy(w, 9, (*far, c), sib, band=1).start()
            else:
                copy(w, 3, (*far, c), me).wait_recv()
                copy(w, 6, (*far, c), sib).start()

        def got_from_sibling(w, which):
            if which == 0:
                copy(w, 0, sib, me).wait_recv()
            elif which == 3 and relayed[w]:
                copy(w, 6, (*far, 1 - c), me, band=0).wait_recv()
                copy(w, 9, (*far, 1 - c), me, band=1).wait_recv()
            else:
                copy(w, 3 + which, (*chips[which - 1], 1 - c), me).wait_recv()

        others = range(1, n)

        def arrive(k):
            if k == 0:
                own(0).wait()
            elif k == 1:
                got_from_sibling(0, 0)
            elif k == 2:
                got_near_x(0)
                for w in others:
                    start_sends(w)
            elif k == 3:
                got_near_y(0)
            elif k in (4, 5):
                got_from_sibling(0, k - 3)
            elif k == 6:
                got_far(0)
                for w in others:
                    got_near_x(w)
                    got_near_y(w)
            else:
                got_from_sibling(0, 3)
                for w in others:
                    got_far(w)

        def load(k):
            at = pl.multiple_of(order_ref[k] * cols, 128)
            return pltpu.make_async_copy(dsts[0].at[:, pl.ds(at, cols)], wbuf.at[k % 2], load_sem.at[k % 2])

        @pl.when((b == 0) & (i == 0))
        def _():
            start_sends(0)
            arrive(0)
            load(0).start()
            load(0).wait()

        early = ni - 1
        for k in range(1, N_DEV):
            @pl.when((b == k - 1) & (i == early))
            def _(k=k):
                arrive(k)
                load(k).start()

            @pl.when((b == k) & (i == 0))
            def _(k=k):
                load(k).wait()

        gu_ref[...] = _dot(x_ref[...].astype(BF16), wbuf[b % 2]).astype(BF16)

        @pl.when(b == 0)
        def _():
            xt_ref[...] = x_ref[...].T.astype(BF16)

        @pl.when((b == N_DEV - 1) & (i == ni - 1))
        def _():
            for w in others:
                for which in range(4):
                    got_from_sibling(w, which)
                own(w).wait()
            for w in range(n):
                for cp in sends(w) + passed_on(w):
                    cp.wait_send()

    grid_spec = pltpu.PrefetchScalarGridSpec(
        num_scalar_prefetch=1, grid=(N_DEV, ni),
        in_specs=[pl.BlockSpec((tm, d), lambda b, i, o: (i, 0))] + [ANY] * n,
        out_specs=[pl.BlockSpec((tm, cols), lambda b, i, o: (i, o[b])),
                   pl.BlockSpec((d, tm), lambda b, i, o: (0, jnp.where(b == 0, i, ni - 1)))] + [ANY] * n,
        scratch_shapes=[pltpu.VMEM((2, d, cols), BF16), pltpu.SemaphoreType.DMA((N_COPIES * n,)),
                        pltpu.SemaphoreType.DMA((N_COPIES * n,)), pltpu.SemaphoreType.DMA((n,)),
                        pltpu.SemaphoreType.DMA((2,))])
    res = pl.pallas_call(
        body, name=name, grid_spec=grid_spec,
        out_shape=[jax.ShapeDtypeStruct((t, N_DEV * cols), BF16), jax.ShapeDtypeStruct((d, t), BF16)]
        + [_gathered_shape(s, cm) for s, cm in zip(shards, col_major)],
        compiler_params=_cparams("arbitrary", "arbitrary"),
    )(order, xb, *shards)
    return res[0], res[1], res[2:]


class _Exchange:
    def __init__(self, ins, io, new, n_sems, n_local, make):
        self.ins, self.io, self.new = list(ins), list(io), list(new)
        self.n_sems, self.n_local, self.make = n_sems, n_local, make


def _block_slot(ref, col_major, cols, place, rows=None):
    k = 4 * place[0] + 2 * place[1] + place[2]
    band = slice(None) if rows is None else pl.ds(rows[0], rows[1])
    if col_major:
        return ref.at[band, pl.ds(pl.multiple_of(k * cols, 128), cols)]
    return ref.at[k] if rows is None else ref.at[k, band]


def _gathered_shape(s, col_major):
    return jax.ShapeDtypeStruct((s.shape[0], N_DEV * s.shape[1]) if col_major else (N_DEV,) + s.shape, s.dtype)


def _gather_first(shards, col_major, rows=None, into=None):
    n = len(shards)
    new = [] if into is not None else [_gathered_shape(s, cm) for s, cm in zip(shards, col_major)]

    def make(in_refs, io_refs, new_refs, send_sems, recv_sems, local_sems, base=0, local_base=0):
        x, y, c = _place()
        targets = [(x, y, 1 - c)] + [(*chip, c) for chip in _other_chips(x, y)]
        gathered = io_refs if into is not None else new_refs
        copies = []
        for w in range(n):
            src = in_refs[w] if rows is None else in_refs[w].at[pl.ds(rows[0], rows[1])]
            slot = _block_slot(gathered[w], col_major[w], shards[w].shape[1], (x, y, c), rows)
            copies.append(pltpu.make_async_copy(src, slot, local_sems.at[local_base + w]))
            for s, to in enumerate(targets):
                copies.append(pltpu.make_async_remote_copy(
                    src_ref=src, dst_ref=slot, send_sem=send_sems.at[base + 4 * w + s],
                    recv_sem=recv_sems.at[base + 4 * w + s], device_id=to, device_id_type=MESH))
        return copies

    return _Exchange(shards, into or [], new, 4 * n, n, make)


def _gather_forward(gathered, col_major, cols, rows=None):
    n = len(gathered)

    def make(in_refs, io_refs, new_refs, send_sems, recv_sems, local_sems, base=0, local_base=0):
        x, y, c = _place()
        copies = []
        for w in range(n):
            for j, chip in enumerate(_other_chips(x, y)):
                slot = _block_slot(io_refs[w], col_major[w], cols[w], (*chip, c), rows)
                copies.append(pltpu.make_async_remote_copy(
                    src_ref=slot, dst_ref=slot, send_sem=send_sems.at[base + 3 * w + j],
                    recv_sem=recv_sems.at[base + 3 * w + j], device_id=(x, y, 1 - c), device_id_type=MESH))
        return copies

    return _Exchange([], gathered, [], 3 * n, 0, make)


def _both(a, b):
    def make(in_refs, io_refs, new_refs, send_sems, recv_sems, local_sems):
        na = len(a.ins)
        return (a.make(in_refs[:na], io_refs, [], send_sems, recv_sems, local_sems, 0, 0)
                + b.make(in_refs[na:], io_refs, [], send_sems, recv_sems, local_sems, a.n_sems, a.n_local))

    return _Exchange(a.ins + b.ins, a.io, [], a.n_sems + b.n_sems, a.n_local + b.n_local, make)


def _rs_sibling(parts):
    n = len(parts)

    def make(in_refs, io_refs, new_refs, send_sems, recv_sems, local_sems):
        x, y, c = _place()
        copies = []
        for w in range(n):
            for j in range(4):
                copies.append(pltpu.make_async_remote_copy(
                    src_ref=in_refs[w].at[2 * j + (1 - c)], dst_ref=new_refs[w].at[j],
                    send_sem=send_sems.at[4 * w + j], recv_sem=recv_sems.at[4 * w + j],
                    device_id=(x, y, 1 - c), device_id_type=MESH))
        return copies

    return _Exchange(parts, [], [jax.ShapeDtypeStruct((4,) + p.shape[1:], p.dtype) for p in parts], 4 * n, 0, make)


def _rs_chips(chip_parts, rows=None, into=None):
    n = len(chip_parts)
    band = slice(None) if rows is None else pl.ds(rows[0], rows[1])
    new = [] if into is not None else [jax.ShapeDtypeStruct((3,) + p.shape[1:], p.dtype) for p in chip_parts]

    def make(in_refs, io_refs, new_refs, send_sems, recv_sems, local_sems):
        x, y, c = _place()
        landing = io_refs if into is not None else new_refs
        copies = []
        for w in range(n):
            for rel, (px, py) in enumerate(_other_chips(x, y)):
                copies.append(pltpu.make_async_remote_copy(
                    src_ref=in_refs[w].at[2 * px + py, band], dst_ref=landing[w].at[rel, band],
                    send_sem=send_sems.at[3 * w + rel], recv_sem=recv_sems.at[3 * w + rel],
                    device_id=(px, py, c), device_id_type=MESH))
        return copies

    return _Exchange(chip_parts, into or [], new, 3 * n, 0, make)


class _Side:
    def __init__(self, ins, in_blocks, out_shapes, out_blocks, n_tiles, fn):
        self.ins, self.in_blocks, self.out_shapes, self.out_blocks = list(ins), in_blocks, list(out_shapes), out_blocks
        self.n_tiles, self.fn = n_tiles, fn


def _call(body, exch, *, name, grid, in_specs, out_specs, out_shape, scratch_shapes=(), semantics,
          input_output_aliases=None):
    exch = list(exch)
    in_specs, out_specs, out_shape = list(in_specs), list(out_specs), list(out_shape)
    scratch_shapes = list(scratch_shapes)
    if not exch:
        fn = pl.pallas_call(body, name=name, grid=grid, in_specs=in_specs, out_specs=out_specs, out_shape=out_shape,
                            scratch_shapes=scratch_shapes, input_output_aliases=input_output_aliases or {},
                            compiler_params=_cparams(*semantics))
        return lambda *args: (fn(*args), [])
    n_in, n_out, n_scr = len(in_specs), len(out_specs), len(scratch_shapes)
    aliases = dict(input_output_aliases or {})
    all_in, all_out_specs, all_out_shape, all_scr = list(in_specs), list(out_specs), list(out_shape), list(scratch_shapes)
    extra_args = []

    def step(idx):
        s = idx[0]
        for a in range(1, len(grid)):
            s = s * grid[a] + idx[a]
        return s

    def tile_spec(shape, where, n_tiles):
        return pl.BlockSpec(shape, lambda *idx: where(jnp.minimum(step(idx), n_tiles - 1)))

    for ex in exch:
        if isinstance(ex, _Side):
            all_in += [tile_spec(shape, where, ex.n_tiles) for shape, where in ex.in_blocks]
            extra_args += ex.ins
            all_out_specs += [tile_spec(shape, where, ex.n_tiles) for shape, where in ex.out_blocks]
            all_out_shape += ex.out_shapes
            continue
        for k, a in enumerate(ex.io):
            aliases[len(all_in) + len(ex.ins) + k] = len(all_out_specs) + k
        all_in += [ANY] * (len(ex.ins) + len(ex.io))
        extra_args += ex.ins + ex.io
        all_out_specs += [ANY] * (len(ex.io) + len(ex.new))
        all_out_shape += [jax.ShapeDtypeStruct(a.shape, a.dtype) for a in ex.io] + ex.new
        all_scr += [pltpu.SemaphoreType.DMA((ex.n_sems,)), pltpu.SemaphoreType.DMA((ex.n_sems,)),
                    pltpu.SemaphoreType.DMA((max(ex.n_local, 1),))]

    n_ins = [len(ex.ins) if isinstance(ex, _Side) else len(ex.ins) + len(ex.io) for ex in exch]
    n_outs = [len(ex.out_shapes) if isinstance(ex, _Side) else len(ex.io) + len(ex.new) for ex in exch]

    def wrapped(*refs):
        pos = n_in
        ex_in = []
        for k in n_ins:
            ex_in.append(refs[pos:pos + k])
            pos += k
        outs = refs[pos:pos + n_out]
        pos += n_out
        ex_out = []
        for k in n_outs:
            ex_out.append(refs[pos:pos + k])
            pos += k
        scr = refs[pos:pos + n_scr]
        pos += n_scr
        idx = [pl.program_id(a) for a in range(len(grid))]
        first = functools.reduce(jnp.logical_and, [i == 0 for i in idx])
        last = functools.reduce(jnp.logical_and, [i == g - 1 for i, g in zip(idx, grid)])

        def copies():
            out, at = [], pos
            for ex, ei, eo in zip(exch, ex_in, ex_out):
                if not isinstance(ex, _Side):
                    out += ex.make(ei[:len(ex.ins)], eo[:len(ex.io)], eo[len(ex.io):], *refs[at:at + 3])
                    at += 3
            return out

        @pl.when(first)
        def _():
            for cp in copies():
                cp.start()

        body(*refs[:n_in], *outs, *scr)
        for ex, ei, eo in zip(exch, ex_in, ex_out):
            if isinstance(ex, _Side):
                pl.when(step(idx) < ex.n_tiles)(functools.partial(ex.fn, ei, eo))

        @pl.when(last)
        def _():
            for cp in copies():
                cp.wait()

    fn = pl.pallas_call(wrapped, name=name, grid=grid, in_specs=all_in, out_specs=all_out_specs,
                        out_shape=all_out_shape, scratch_shapes=all_scr, input_output_aliases=aliases,
                        compiler_params=_cparams(*(["arbitrary"] * len(grid))))

    def run(*args):
        res = fn(*args, *extra_args)
        outs, pos, ex_res = res[:n_out], n_out, []
        for k in n_outs:
            ex_res.append(list(res[pos:pos + k]))
            pos += k
        return outs, ex_res

    return run


def _exchange_alone(ex, name):
    def body():
        pass

    _, res = _call(body, [ex], name=name, grid=(1,), in_specs=[], out_specs=[], out_shape=[], semantics=("arbitrary",))()
    return res[0]


def _small_gather(part):
    def make(in_refs, io_refs, new_refs, send_sems, recv_sems, local_sems):
        x, y, c = _place()
        slot = new_refs[0].at[4 * x + 2 * y + c]
        copies = [pltpu.make_async_copy(in_refs[0], slot, local_sems.at[0])]
        for d in range(1, N_DEV):
            peer = (1 - x if d & 4 else x, 1 - y if d & 2 else y, 1 - c if d & 1 else c)
            copies.append(pltpu.make_async_remote_copy(
                src_ref=in_refs[0], dst_ref=slot, send_sem=send_sems.at[d - 1], recv_sem=recv_sems.at[d - 1],
                device_id=peer, device_id_type=MESH))
        return copies

    return _Exchange([part], [], [jax.ShapeDtypeStruct((N_DEV,) + part.shape, part.dtype)], N_DEV - 1, 1, make)


def _sum_over_devices(parts):
    _, rows, lanes = parts.shape

    def body(p_ref, o_ref):
        acc = p_ref[0]
        for k in range(1, N_DEV):
            acc = acc + p_ref[k]
        o_ref[...] = acc

    return pl.pallas_call(
        body, name="small_grads_sum", grid=(1,), out_shape=jax.ShapeDtypeStruct((rows, lanes), F32),
        in_specs=[pl.BlockSpec((N_DEV, rows, lanes), lambda i: (0, 0, 0))],
        out_specs=pl.BlockSpec((rows, lanes), lambda i: (0, 0)),
        compiler_params=_cparams("arbitrary"),
    )(parts)


def _transpose_bf16(a, name, exch=(), with_copy=False):
    r, c = a.shape
    tr, tc = _tile(r, 512, 128), _tile(c, 1024, 128)

    def body(a_ref, o_ref, *copy_ref):
        v = a_ref[...].astype(F32)
        o_ref[...] = v.T.astype(BF16)
        if with_copy:
            copy_ref[0][...] = v.astype(BF16)

    outs, ex = _call(
        body, exch, name=name, grid=(r // tr, c // tc),
        out_shape=[jax.ShapeDtypeStruct((c, r), BF16)] + [jax.ShapeDtypeStruct((r, c), BF16)] * with_copy,
        in_specs=[pl.BlockSpec((tr, tc), lambda i, j: (i, j))],
        out_specs=[pl.BlockSpec((tc, tr), lambda i, j: (j, i))] + [pl.BlockSpec((tr, tc), lambda i, j: (i, j))] * with_copy,
        semantics=("parallel", "parallel"),
    )(a)
    return (outs if with_copy else outs[0]), ex


def _ffn_fwd_loss(x, wgu, wd, ln_g, ln_b, target, name, exch=()):
    t, d = x.shape
    f = wd.shape[0]
    tm, tf = _tile(t, 512, 128), _tile(f, 512, 128)
    nf = f // tf

    def body(x_ref, wg_ref, wu_ref, wd_ref, lg_ref, lb_ref, t_ref,
             go_ref, uo_ref, ht_ref, dz_ref, dzb_ref, dlg_ref, dlb_ref, loss_ref, xb, acc):
        i, j = pl.program_id(0), pl.program_id(1)

        @pl.when(j == 0)
        def _():
            xb[...] = x_ref[...].astype(BF16)
            acc[...] = jnp.zeros_like(acc)

        @pl.when((i == 0) & (j == 0))
        def _():
            dlg_ref[...] = jnp.zeros_like(dlg_ref)
            dlb_ref[...] = jnp.zeros_like(dlb_ref)
            loss_ref[...] = jnp.zeros_like(loss_ref)

        g = _dot(xb[...], wg_ref[...])
        u = _dot(xb[...], wu_ref[...])
        h = g * _sigmoid(g) * u
        go_ref[...] = g.astype(BF16)
        uo_ref[...] = u.astype(BF16)
        ht_ref[...] = h.T.astype(BF16)
        acc[...] += _dot(h.astype(BF16), wd_ref[...])

        @pl.when(j == nf - 1)
        def _():
            for r in range(0, tm, EPILOGUE_ROWS):
                rows = slice(r, r + EPILOGUE_ROWS)
                xh, rstd = _ln_stats(ALPHA * x_ref[rows, :] + 0.5 * acc[rows, :])
                e = xh * lg_ref[...] + lb_ref[...] - t_ref[rows, :]
                loss_ref[...] += 0.5 * jnp.sum(jnp.sum(e * e, axis=-1, keepdims=True) * (1.0 / d), axis=0,
                                               keepdims=True)
                dy = e * (1.0 / d)
                dz = _ln_bwd(dy * lg_ref[...], xh, rstd)
                dz_ref[rows, :] = dz
                dzb_ref[rows, :] = (0.5 * dz).astype(BF16)
                dlg_ref[...] += jnp.sum(dy * xh, axis=0, keepdims=True)
                dlb_ref[...] += jnp.sum(dy, axis=0, keepdims=True)

    row = lambda i, j: (i, 0)
    fixed = lambda i, j: (0, 0)
    return _call(
        body, exch, name=name, grid=(t // tm, nf),
        out_shape=[jax.ShapeDtypeStruct((t, f), BF16), jax.ShapeDtypeStruct((t, f), BF16),
                   jax.ShapeDtypeStruct((f, t), BF16), jax.ShapeDtypeStruct((t, d), F32),
                   jax.ShapeDtypeStruct((t, d), BF16), jax.ShapeDtypeStruct((1, d), F32),
                   jax.ShapeDtypeStruct((1, d), F32), jax.ShapeDtypeStruct((8, 128), F32)],
        in_specs=[pl.BlockSpec((tm, d), row),
                  pl.BlockSpec((d, tf), lambda i, j: (0, j)),
                  pl.BlockSpec((d, tf), lambda i, j: (0, j + nf)),
                  pl.BlockSpec((tf, d), lambda i, j: (j, 0)),
                  pl.BlockSpec((1, d), fixed), pl.BlockSpec((1, d), fixed), pl.BlockSpec((tm, d), row)],
        out_specs=[pl.BlockSpec((tm, tf), lambda i, j: (i, j)), pl.BlockSpec((tm, tf), lambda i, j: (i, j)),
                   pl.BlockSpec((tf, tm), lambda i, j: (j, i)), pl.BlockSpec((tm, d), row), pl.BlockSpec((tm, d), row),
                   pl.BlockSpec((1, d), fixed), pl.BlockSpec((1, d), fixed), pl.BlockSpec((8, 128), fixed)],
        scratch_shapes=[pltpu.VMEM((tm, d), BF16), pltpu.VMEM((tm, d), F32)],
        semantics=("arbitrary", "arbitrary"),
    )(x, wgu, wgu, wd, ln_g, ln_b, target)


def _ffn_down_fwd(gu, x, wd, ln_g, ln_b, name, exch=()):
    t, d = x.shape
    f = wd.shape[0]
    tm, tf = _tile(t, 512, 128), _tile(f, 512, 128)
    nf = f // tf

    def body(g_ref, u_ref, wd_ref, x_ref, lg_ref, lb_ref, ht_ref, z_ref, xn_ref, acc):
        j = pl.program_id(1)

        @pl.when(j == 0)
        def _():
            acc[...] = jnp.zeros_like(acc)

        g = g_ref[...].astype(F32)
        h = g * _sigmoid(g) * u_ref[...].astype(F32)
        ht_ref[...] = h.T.astype(BF16)
        acc[...] += _dot(h.astype(BF16), wd_ref[...])

        @pl.when(j == nf - 1)
        def _():
            z = ALPHA * x_ref[...] + 0.5 * acc[...]
            z_ref[...] = z
            xn_ref[...] = _ln(z, lg_ref[...], lb_ref[...])

    row = lambda i, j: (i, 0)
    fixed = lambda i, j: (0, 0)
    return _call(
        body, exch, name=name, grid=(t // tm, nf),
        out_shape=[jax.ShapeDtypeStruct((f, t), BF16), jax.ShapeDtypeStruct((t, d), F32),
                   jax.ShapeDtypeStruct((t, d), F32)],
        in_specs=[pl.BlockSpec((tm, tf), lambda i, j: (i, j)), pl.BlockSpec((tm, tf), lambda i, j: (i, j + nf)),
                  pl.BlockSpec((tf, d), lambda i, j: (j, 0)), pl.BlockSpec((tm, d), row),
                  pl.BlockSpec((1, d), fixed), pl.BlockSpec((1, d), fixed)],
        out_specs=[pl.BlockSpec((tf, tm), lambda i, j: (j, i)), pl.BlockSpec((tm, d), row), pl.BlockSpec((tm, d), row)],
        scratch_shapes=[pltpu.VMEM((tm, d), F32)],
        semantics=("parallel", "arbitrary"),
    )(gu, gu, wd, x, ln_g, ln_b)


def _ffn_act_grads(dh, g_ref, u_ref):
    gg = g_ref[...].astype(F32)
    uu = u_ref[...].astype(F32)
    s = _sigmoid(gg)
    du = (dh * (gg * s)).astype(BF16)
    dg = (dh * uu * (s * (1.0 + gg * (1.0 - s)))).astype(BF16)
    return dg, du


def _ffn_bwd(dz, do, g, u, wgu, wd, name, exch=()):
    t, d = dz.shape
    f = wd.shape[0]
    tm, tf = _tile(t, 1024, 128), _tile(f, 512, 128)
    nf = f // tf

    def body(dz_ref, do_ref, g_ref, u_ref, wg_ref, wu_ref, wd_ref, dg_ref, du_ref, dx_ref):
        @pl.when(pl.program_id(1) == 0)
        def _():
            dx_ref[...] = ALPHA * dz_ref[...]

        dg, du = _ffn_act_grads(_dot_nt(do_ref[...], wd_ref[...]), g_ref, u_ref)
        dg_ref[...] = dg
        du_ref[...] = du
        dx_ref[...] += _dot_nt(dg, wg_ref[...]) + _dot_nt(du, wu_ref[...])

    row = lambda i, j: (i, 0)
    tile = lambda i, j: (i, j)
    once = pl.Buffered(1)
    return _call(
        body, exch, name=name, grid=(t // tm, nf),
        out_shape=[jax.ShapeDtypeStruct((t, f), BF16), jax.ShapeDtypeStruct((t, f), BF16),
                   jax.ShapeDtypeStruct((t, d), F32)],
        in_specs=[pl.BlockSpec((tm, d), row, pipeline_mode=once), pl.BlockSpec((tm, d), row, pipeline_mode=once),
                  pl.BlockSpec((tm, tf), tile), pl.BlockSpec((tm, tf), tile),
                  pl.BlockSpec((d, tf), lambda i, j: (0, j)),
                  pl.BlockSpec((d, tf), lambda i, j: (0, j + nf)),
                  pl.BlockSpec((tf, d), lambda i, j: (j, 0))],
        out_specs=[pl.BlockSpec((tm, tf), tile), pl.BlockSpec((tm, tf), tile), pl.BlockSpec((tm, d), row)],
        semantics=("parallel", "arbitrary"),
    )(dz, do, g, u, wgu, wgu, wd)


def _ffn_bwd_act(do, gu, wd, name, exch=()):
    t, d = do.shape
    f = wd.shape[0]
    tm, tf = _tile(t, 2048, 128), _tile(f, 512, 128)
    nf = f // tf

    def body(do_ref, g_ref, u_ref, wd_ref, dg_ref, du_ref):
        dg, du = _ffn_act_grads(_dot_nt(do_ref[...], wd_ref[...]), g_ref, u_ref)
        dg_ref[...] = dg
        du_ref[...] = du

    tile = lambda i, j: (i, j)
    return _call(
        body, exch, name=name, grid=(t // tm, f // tf),
        out_shape=[jax.ShapeDtypeStruct((t, f), BF16), jax.ShapeDtypeStruct((t, f), BF16)],
        in_specs=[pl.BlockSpec((tm, d), lambda i, j: (i, 0)), pl.BlockSpec((tm, tf), tile),
                  pl.BlockSpec((tm, tf), lambda i, j: (i, j + nf)), pl.BlockSpec((tf, d), lambda i, j: (j, 0))],
        out_specs=[pl.BlockSpec((tm, tf), tile), pl.BlockSpec((tm, tf), tile)],
        semantics=("parallel", "parallel"),
    )(do, gu, gu, wd)


def _ffn_bwd_dx(dz, dg, du, wgu, name, exch=()):
    t, d = dz.shape
    f = dg.shape[1]
    tm, tn = _tile(t, 512, 128), _tile(d, 256, 128)

    def body(dz_ref, dg_ref, du_ref, wg_ref, wu_ref, dx_ref):
        dx_ref[...] = ALPHA * dz_ref[...] + _dot_nt(dg_ref[...], wg_ref[...]) + _dot_nt(du_ref[...], wu_ref[...])

    row = lambda i, n: (i, 0)
    tile = lambda i, n: (i, n)
    return _call(
        body, exch, name=name, grid=(t // tm, d // tn), out_shape=[jax.ShapeDtypeStruct((t, d), F32)],
        in_specs=[pl.BlockSpec((tm, tn), tile), pl.BlockSpec((tm, f), row), pl.BlockSpec((tm, f), row),
                  pl.BlockSpec((tn, f), lambda i, n: (n, 0)), pl.BlockSpec((tn, f), lambda i, n: (n, 1))],
        out_specs=[pl.BlockSpec((tm, tn), tile)],
        semantics=("parallel", "arbitrary"),
    )(dz, dg, du, wgu, wgu)


def _weight_grad(at, b, tn, tmm, name, blocks=None, block_offset=0, into=None, exch=()):
    m, t = at.shape
    nn = b.shape[1]
    tmm = _tile(m, tmm, 16)
    assert nn % tn == 0

    def body(*refs):
        at_ref, b_ref, o_ref = refs[0], refs[1], refs[-1]
        r = _dot(at_ref[...], b_ref[...]).astype(BF16)
        if blocks is None:
            o_ref[...] = r
        else:
            o_ref[0] = r

    in_specs = [pl.BlockSpec((tmm, t), lambda n, i: (i, 0)), pl.BlockSpec((t, tn), lambda n, i: (0, n))]
    args = [at, b]
    aliases = {}
    if into is not None:
        in_specs.append(ANY)
        args.append(into)
        aliases = {2: 0}
    if blocks is None:
        out_shape = jax.ShapeDtypeStruct((m, nn), BF16)
        out_spec = pl.BlockSpec((tmm, tn), lambda n, i: (i, n))
    else:
        out_shape = jax.ShapeDtypeStruct((blocks, m, tn), BF16)
        out_spec = pl.BlockSpec((1, tmm, tn), lambda n, i: (n + block_offset, i, 0))
    (out,), ex = _call(
        body, exch, name=name, grid=(nn // tn, m // tmm), out_shape=[out_shape],
        in_specs=in_specs, out_specs=[out_spec], input_output_aliases=aliases,
        semantics=("parallel", "parallel"),
    )(*args)
    return out, ex


def _mix_in_proj(x, w_in, name, exch=()):
    t, d = x.shape
    n_out = w_in.shape[1]
    tm, cb = _tile(t, 512, 128), _tile(n_out, 1024, 128)

    def body(x_ref, w_ref, o_ref, xb):
        @pl.when(pl.program_id(1) == 0)
        def _():
            xb[...] = x_ref[...].astype(BF16)

        o_ref[...] = _dot(xb[...], w_ref[...])

    (out,), ex = _call(
        body, exch, name=name, grid=(t // tm, n_out // cb), out_shape=[jax.ShapeDtypeStruct((t, n_out), F32)],
        in_specs=[pl.BlockSpec((tm, d), lambda i, k: (i, 0)), pl.BlockSpec((d, cb), lambda i, k: (0, k))],
        out_specs=[pl.BlockSpec((tm, cb), lambda i, k: (i, k))],
        scratch_shapes=[pltpu.VMEM((tm, d), BF16)],
        semantics=("parallel", "arbitrary"),
    )(x, w_in)
    return out, ex


def _mix_in_bwd(dproj, w_in, dz, name, exch=()):
    t, d = dz.shape
    kk = w_in.shape[1]
    tm, tn = _tile(t, 512, 128), _tile(d, 512, 128)

    def body(dp_ref, w_ref, dz_ref, dx_ref):
        dx_ref[...] = ALPHA * dz_ref[...] + _dot_nt(dp_ref[...], w_ref[...])

    (out,), ex = _call(
        body, exch, name=name, grid=(t // tm, d // tn), out_shape=[jax.ShapeDtypeStruct((t, d), F32)],
        in_specs=[pl.BlockSpec((tm, kk), lambda i, n: (i, 0)), pl.BlockSpec((tn, kk), lambda i, n: (n, 0)),
                  pl.BlockSpec((tm, tn), lambda i, n: (i, n))],
        out_specs=[pl.BlockSpec((tm, tn), lambda i, n: (i, n))],
        semantics=("parallel", "arbitrary"),
    )(dproj, w_in, dz)
    return out, ex


def _mix_out_fwd(y, w_out, x, ln_g, ln_b, name, exch=()):
    t, d = x.shape
    kk = y.shape[1]
    tm = _tile(t, 256, 128)

    def body(y_ref, w_ref, x_ref, g_ref, b_ref, z_ref, xn_ref, xnt_ref):
        z = ALPHA * x_ref[...] + _dot(y_ref[...], w_ref[...])
        z_ref[...] = z
        xn = _ln(z, g_ref[...], b_ref[...])
        xn_ref[...] = xn
        xnt_ref[...] = xn.T.astype(BF16)

    row = lambda i: (i, 0)
    fixed = lambda i: (0, 0)
    return _call(
        body, exch, name=name, grid=(t // tm,),
        out_shape=[jax.ShapeDtypeStruct((t, d), F32), jax.ShapeDtypeStruct((t, d), F32),
                   jax.ShapeDtypeStruct((d, t), BF16)],
        in_specs=[pl.BlockSpec((tm, kk), row), pl.BlockSpec((kk, d), fixed), pl.BlockSpec((tm, d), row),
                  pl.BlockSpec((1, d), fixed), pl.BlockSpec((1, d), fixed)],
        out_specs=[pl.BlockSpec((tm, d), row), pl.BlockSpec((tm, d), row), pl.BlockSpec((d, tm), lambda i: (0, i))],
        semantics=("parallel",),
    )(y, w_out, x, ln_g, ln_b)


def _mix_out_bwd(dzb, w_out, name, exch=()):
    t, d = dzb.shape
    kk = w_out.shape[0]
    tm = _tile(t, 512, 128)

    def body(dz_ref, w_ref, dy_ref):
        dy_ref[...] = _dot_nt(dz_ref[...], w_ref[...])

    (out,), ex = _call(
        body, exch, name=name, grid=(t // tm,), out_shape=[jax.ShapeDtypeStruct((t, kk), F32)],
        in_specs=[pl.BlockSpec((tm, d), lambda i: (i, 0)), pl.BlockSpec((kk, d), lambda i: (0, 0))],
        out_specs=[pl.BlockSpec((tm, kk), lambda i: (i, 0))],
        semantics=("parallel",),
    )(dzb, w_out)
    return out, ex


def _loss_ln_bwd(z, target, ln_g, ln_b, bf16_scale, name):
    t, d = z.shape
    tm = _tile(t, 512, 8)

    def body(z_ref, t_ref, g_ref, b_ref, dz_ref, dzb_ref, dg_ref, db_ref, loss_ref):
        @pl.when(pl.program_id(0) == 0)
        def _():
            dg_ref[...] = jnp.zeros_like(dg_ref)
            db_ref[...] = jnp.zeros_like(db_ref)
            loss_ref[...] = jnp.zeros_like(loss_ref)

        xh, rstd = _ln_stats(z_ref[...])
        e = xh * g_ref[...] + b_ref[...] - t_ref[...]
        loss_ref[...] += 0.5 * jnp.sum(jnp.sum(e * e, axis=-1, keepdims=True) * (1.0 / d), axis=0, keepdims=True)
        dy = e * (1.0 / d)
        dz = _ln_bwd(dy * g_ref[...], xh, rstd)
        dz_ref[...] = dz
        dzb_ref[...] = (bf16_scale * dz).astype(BF16)
        dg_ref[...] += jnp.sum(dy * xh, axis=0, keepdims=True)
        db_ref[...] += jnp.sum(dy, axis=0, keepdims=True)

    row = lambda i: (i, 0)
    fixed = lambda i: (0, 0)
    return pl.pallas_call(
        body, name=name, grid=(t // tm,),
        out_shape=[jax.ShapeDtypeStruct((t, d), F32), jax.ShapeDtypeStruct((t, d), BF16),
                   jax.ShapeDtypeStruct((1, d), F32), jax.ShapeDtypeStruct((1, d), F32),
                   jax.ShapeDtypeStruct((8, 128), F32)],
        in_specs=[pl.BlockSpec((tm, d), row), pl.BlockSpec((tm, d), row), pl.BlockSpec((1, d), fixed),
                  pl.BlockSpec((1, d), fixed)],
        out_specs=[pl.BlockSpec((tm, d), row), pl.BlockSpec((tm, d), row), pl.BlockSpec((1, d), fixed),
                   pl.BlockSpec((1, d), fixed), pl.BlockSpec((8, 128), fixed)],
        compiler_params=_cparams("arbitrary"),
    )(z, target, ln_g, ln_b)


def _ln_bwd_call(z, dy, ln_g, bf16_scale, name, exch=()):
    t, d = z.shape
    tm = _tile(t, 512, 8)

    def body(z_ref, dy_ref, g_ref, dz_ref, dzb_ref, dg_ref, db_ref):
        @pl.when(pl.program_id(0) == 0)
        def _():
            dg_ref[...] = jnp.zeros_like(dg_ref)
            db_ref[...] = jnp.zeros_like(db_ref)

        xh, rstd = _ln_stats(z_ref[...])
        dy = dy_ref[...]
        dz = _ln_bwd(dy * g_ref[...], xh, rstd)
        dz_ref[...] = dz
        dzb_ref[...] = (bf16_scale * dz).astype(BF16)
        dg_ref[...] += jnp.sum(dy * xh, axis=0, keepdims=True)
        db_ref[...] += jnp.sum(dy, axis=0, keepdims=True)

    row = lambda i: (i, 0)
    fixed = lambda i: (0, 0)
    return _call(
        body, exch, name=name, grid=(t // tm,),
        out_shape=[jax.ShapeDtypeStruct((t, d), F32), jax.ShapeDtypeStruct((t, d), BF16),
                   jax.ShapeDtypeStruct((1, d), F32), jax.ShapeDtypeStruct((1, d), F32)],
        in_specs=[pl.BlockSpec((tm, d), row), pl.BlockSpec((tm, d), row), pl.BlockSpec((1, d), fixed)],
        out_specs=[pl.BlockSpec((tm, d), row), pl.BlockSpec((tm, d), row), pl.BlockSpec((1, d), fixed),
                   pl.BlockSpec((1, d), fixed)],
        semantics=("arbitrary",),
    )(z, dy, ln_g)


CONV_ROWS = 32
CONV_LANES = 512
SUBLANES = 8


def _fill_shifted(ext, shifted):
    rows = ext.shape[0] - SUBLANES
    for s in range(1, SUBLANES):
        for r in range(0, rows, CONV_ROWS):
            n = min(CONV_ROWS, rows - r)
            shifted[s - 1, r:r + n, :] = ext[r + s:r + s + n, :]


def _window(ext, shifted, lo, n, lanes=slice(None)):
    s = lo % SUBLANES
    return ext[lo:lo + n, lanes] if s == 0 else shifted[s - 1, lo - s:lo - s + n, lanes]


def _mixer_fwd(proj, conv_w, conv_b, cln_g, cln_b, sln_g, sln_b, sg_wm, sg_bb, name, exch=()):
    t = proj.shape[0]
    tm = _tile(t, 256, CHUNK)
    hb = tm // HALO
    nc = tm // CHUNK
    ch = CONV_CH

    def body(av_ref, ag_ref, bu_ref, bv_ref, hv_ref, hg_ref, cw_ref, cb_ref, lg_ref, lb_ref, sg_ref, sb_ref,
             w_ref, bb_ref, y_ref, yt_ref, c_ref, ext, ext_s):
        i = pl.program_id(0)
        halo = hv_ref[...] * _sigmoid(hg_ref[...])
        ext[0:HALO, :] = jnp.where(i > 0, halo, 0.0)
        ext[HALO:HALO + tm, :] = av_ref[...] * _sigmoid(ag_ref[...])
        _fill_shifted(ext, ext_s)
        for r in range(0, tm, CONV_ROWS):
            acc = jnp.zeros((CONV_ROWS, ch), F32) + cb_ref[...]
            for k in range(CONV_TAPS):
                lo = r + k + HALO - (CONV_TAPS - 1)
                acc = acc + cw_ref[k:k + 1, :] * _window(ext, ext_s, lo, CONV_ROWS)
            c_ref[r:r + CONV_ROWS, :] = acc
        a = _ln(c_ref[...], lg_ref[...], lb_ref[...])
        ya = a * _sigmoid(a)
        y_ref[:, 0:ch] = ya.astype(BF16)
        yt_ref[0:ch, :] = ya.T.astype(BF16)
        for h in range(HEADS):
            sl = slice(h * HEAD_DIM, (h + 1) * HEAD_DIM)
            u, _ = _gelu_and_grad(bu_ref[:, sl])
            v, _ = _gelu_and_grad(bv_ref[:, sl])
            vn = _ln(v, sg_ref[h:h + 1, :], sb_ref[h:h + 1, :])
            vn3 = vn.astype(BF16).reshape(nc, CHUNK, HEAD_DIM)
            wb = jnp.broadcast_to(w_ref[h][None], (nc, CHUNK, CHUNK))
            mixed = jnp.einsum("cts,csd->ctd", wb, vn3, preferred_element_type=F32) + bb_ref[h][None]
            yb = u * mixed.reshape(tm, HEAD_DIM)
            y_ref[:, ch + h * HEAD_DIM:ch + (h + 1) * HEAD_DIM] = yb.astype(BF16)
            yt_ref[ch + h * HEAD_DIM:ch + (h + 1) * HEAD_DIM, :] = yb.T.astype(BF16)

    col = lambda cidx: (lambda i: (i, cidx))
    prev = lambda cidx: (lambda i: (jnp.maximum(i * hb - 1, 0), cidx))
    fix2 = lambda i: (0, 0)
    fix3 = lambda i: (0, 0, 0)
    return _call(
        body, exch, name=name, grid=(t // tm,),
        out_shape=[jax.ShapeDtypeStruct((t, 2 * ch), BF16), jax.ShapeDtypeStruct((2 * ch, t), BF16),
                   jax.ShapeDtypeStruct((t, ch), F32)],
        in_specs=[pl.BlockSpec((tm, ch), col(0)), pl.BlockSpec((tm, ch), col(1)), pl.BlockSpec((tm, ch), col(2)),
                  pl.BlockSpec((tm, ch), col(3)), pl.BlockSpec((HALO, ch), prev(0)), pl.BlockSpec((HALO, ch), prev(1)),
                  pl.BlockSpec((CONV_TAPS, ch), fix2), pl.BlockSpec((1, ch), fix2), pl.BlockSpec((1, ch), fix2),
                  pl.BlockSpec((1, ch), fix2), pl.BlockSpec((HEADS, HEAD_DIM), fix2), pl.BlockSpec((HEADS, HEAD_DIM), fix2),
                  pl.BlockSpec((HEADS, CHUNK, CHUNK), fix3), pl.BlockSpec((HEADS, CHUNK, HEAD_DIM), fix3)],
        out_specs=[pl.BlockSpec((tm, 2 * ch), lambda i: (i, 0)), pl.BlockSpec((2 * ch, tm), lambda i: (0, i)),
                   pl.BlockSpec((tm, ch), lambda i: (i, 0))],
        scratch_shapes=[pltpu.VMEM((HALO + tm, ch), F32), pltpu.VMEM((SUBLANES - 1, HALO + tm, ch), F32)],
        semantics=("parallel",),
    )(proj, proj, proj, proj, proj, proj, conv_w, conv_b, cln_g, cln_b, sln_g, sln_b, sg_wm, sg_bb)


def _mixer_bwd(proj, conv_c, dy, conv_w, cln_g, cln_b, sln_g, sln_b, sg_wm, sg_wmt, sg_bb, name, exch=()):
    t = proj.shape[0]
    tm = _tile(t, 256, CHUNK)
    hb = tm // HALO
    nc = tm // CHUNK
    nt = t // tm
    ch = CONV_CH
    last_halo = t // HALO - 1

    def body(av_ref, ag_ref, bu_ref, bv_ref, hv_ref, hg_ref, c_ref, cn_ref, dya_ref, dyan_ref, dyb_ref,
             cw_ref, lg_ref, lb_ref, sg_ref, sb_ref, w_ref, wt_ref, bb_ref,
             dp_ref, dcw_ref, dcb_ref, dlg_ref, dlb_ref, dsg_ref, dsb_ref, dw_ref, dbs_ref,
             ext_h, ext_dc, ext_hs, ext_dcs, acc_cw):
        i = pl.program_id(0)

        @pl.when(i == 0)
        def _():
            acc_cw[...] = jnp.zeros_like(acc_cw)
            for ref in (dcb_ref, dlg_ref, dlb_ref, dsg_ref, dsb_ref, dw_ref, dbs_ref):
                ref[...] = jnp.zeros_like(ref)

        lg = lg_ref[...]
        lb = lb_ref[...]

        def conv_ln_bwd(c, dya):
            xh, rstd = _ln_stats(c)
            a = xh * lg + lb
            da = dya * _silu_grad(a)
            return _ln_bwd(da * lg, xh, rstd), da, xh

        fold = lambda v: jnp.sum(v.reshape(CONV_ROWS // SUBLANES, SUBLANES, ch), axis=0)
        s_lg = s_lb = s_cb = jnp.zeros((SUBLANES, ch), F32)
        for r in range(0, tm, CONV_ROWS):
            dc, da, xh = conv_ln_bwd(c_ref[r:r + CONV_ROWS, :], dya_ref[r:r + CONV_ROWS, :])
            ext_dc[r:r + CONV_ROWS, :] = dc
            s_lg, s_lb, s_cb = s_lg + fold(da * xh), s_lb + fold(da), s_cb + fold(dc)
        dlg_ref[...] += jnp.sum(s_lg, axis=0, keepdims=True)
        dlb_ref[...] += jnp.sum(s_lb, axis=0, keepdims=True)
        dcb_ref[...] += jnp.sum(s_cb, axis=0, keepdims=True)
        dcn, _, _ = conv_ln_bwd(cn_ref[...], dyan_ref[...])
        ext_dc[tm:tm + HALO, :] = jnp.where(i < nt - 1, dcn, 0.0)
        halo = hv_ref[...] * _sigmoid(hg_ref[...])
        ext_h[0:HALO, :] = jnp.where(i > 0, halo, 0.0)
        ext_h[HALO:HALO + tm, :] = av_ref[...] * _sigmoid(ag_ref[...])
        _fill_shifted(ext_h, ext_hs)
        _fill_shifted(ext_dc, ext_dcs)
        for r, c0 in [(r, c0) for r in range(0, tm, CONV_ROWS) for c0 in range(0, ch, CONV_LANES)]:
            rows, lanes = slice(r, r + CONV_ROWS), slice(c0, c0 + CONV_LANES)
            dcr = ext_dc[rows, lanes]
            acc = jnp.zeros((CONV_ROWS, CONV_LANES), F32)
            for k in range(CONV_TAPS):
                lo = r + k + HALO - (CONV_TAPS - 1)
                prod = dcr * _window(ext_h, ext_hs, lo, CONV_ROWS, lanes)
                acc_cw[k, :, lanes] += jnp.sum(prod.reshape(CONV_ROWS // SUBLANES, SUBLANES, CONV_LANES), axis=0)
                hi = r + (CONV_TAPS - 1) - k
                acc = acc + cw_ref[k:k + 1, lanes] * _window(ext_dc, ext_dcs, hi, CONV_ROWS, lanes)
            sg_r = _sigmoid(ag_ref[rows, lanes])
            av_r = av_ref[rows, lanes]
            dp_ref[rows, lanes] = (acc * sg_r).astype(BF16)
            dp_ref[rows, slice(ch + c0, ch + c0 + CONV_LANES)] = (acc * av_r * sg_r * (1.0 - sg_r)).astype(BF16)

        @pl.when(i == nt - 1)
        def _():
            dcw_ref[...] = jnp.sum(acc_cw[...], axis=1)

        tril = (lax.broadcasted_iota(jnp.int32, (CHUNK, CHUNK), 0)
                >= lax.broadcasted_iota(jnp.int32, (CHUNK, CHUNK), 1)).astype(F32)
        for h in range(HEADS):
            sl = slice(h * HEAD_DIM, (h + 1) * HEAD_DIM)
            u, du_dx = _gelu_and_grad(bu_ref[:, sl])
            v, dv_dx = _gelu_and_grad(bv_ref[:, sl])
            xhv, rstdv = _ln_stats(v)
            gh = sg_ref[h:h + 1, :]
            vn3 = (xhv * gh + sb_ref[h:h + 1, :]).astype(BF16).reshape(nc, CHUNK, HEAD_DIM)
            wb = jnp.broadcast_to(w_ref[h][None], (nc, CHUNK, CHUNK))
            mixed = jnp.einsum("cts,csd->ctd", wb, vn3, preferred_element_type=F32) + bb_ref[h][None]
            dyb = dyb_ref[:, sl]
            d_u = dyb * mixed.reshape(tm, HEAD_DIM)
            dm = dyb * u
            dm3 = dm.reshape(nc, CHUNK, HEAD_DIM)
            dbs_ref[h:h + 1, :] += jnp.sum(jnp.sum(dm3, axis=0).T, axis=0, keepdims=True)
            dm3b = dm3.astype(BF16)
            dw_h = jnp.sum(jnp.einsum("ctd,csd->cts", dm3b, vn3, preferred_element_type=F32), axis=0)
            dw_ref[h] += dw_h * tril
            wtb = jnp.broadcast_to(wt_ref[h][None], (nc, CHUNK, CHUNK))
            d_vn = jnp.einsum("cst,ctd->csd", wtb, dm3b, preferred_element_type=F32).reshape(tm, HEAD_DIM)
            dsg_ref[h:h + 1, :] += jnp.sum(d_vn * xhv, axis=0, keepdims=True)
            dsb_ref[h:h + 1, :] += jnp.sum(d_vn, axis=0, keepdims=True)
            dv = _ln_bwd(d_vn * gh, xhv, rstdv)
            dp_ref[:, 2 * ch + h * HEAD_DIM:2 * ch + (h + 1) * HEAD_DIM] = (d_u * du_dx).astype(BF16)
            dp_ref[:, 3 * ch + h * HEAD_DIM:3 * ch + (h + 1) * HEAD_DIM] = (dv * dv_dx).astype(BF16)

    col = lambda cidx: (lambda i: (i, cidx))
    prev = lambda cidx: (lambda i: (jnp.maximum(i * hb - 1, 0), cidx))
    nxt = lambda i: (jnp.minimum((i + 1) * hb, last_halo), 0)
    fix2 = lambda i: (0, 0)
    fix3 = lambda i: (0, 0, 0)
    out_shape = [jax.ShapeDtypeStruct((t, 4 * ch), BF16), jax.ShapeDtypeStruct((CONV_TAPS, ch), F32),
                 jax.ShapeDtypeStruct((1, ch), F32), jax.ShapeDtypeStruct((1, ch), F32), jax.ShapeDtypeStruct((1, ch), F32),
                 jax.ShapeDtypeStruct((HEADS, HEAD_DIM), F32), jax.ShapeDtypeStruct((HEADS, HEAD_DIM), F32),
                 jax.ShapeDtypeStruct((HEADS, CHUNK, CHUNK), F32), jax.ShapeDtypeStruct((HEADS, CHUNK), F32)]
    out_specs = [pl.BlockSpec((tm, 4 * ch), lambda i: (i, 0)), pl.BlockSpec((CONV_TAPS, ch), fix2),
                 pl.BlockSpec((1, ch), fix2), pl.BlockSpec((1, ch), fix2), pl.BlockSpec((1, ch), fix2),
                 pl.BlockSpec((HEADS, HEAD_DIM), fix2), pl.BlockSpec((HEADS, HEAD_DIM), fix2),
                 pl.BlockSpec((HEADS, CHUNK, CHUNK), fix3), pl.BlockSpec((HEADS, CHUNK), fix2)]
    in_specs = [pl.BlockSpec((tm, ch), col(0)), pl.BlockSpec((tm, ch), col(1)), pl.BlockSpec((tm, ch), col(2)),
                pl.BlockSpec((tm, ch), col(3)), pl.BlockSpec((HALO, ch), prev(0)), pl.BlockSpec((HALO, ch), prev(1)),
                pl.BlockSpec((tm, ch), col(0)), pl.BlockSpec((HALO, ch), nxt),
                pl.BlockSpec((tm, ch), col(0)), pl.BlockSpec((HALO, ch), nxt), pl.BlockSpec((tm, ch), col(1)),
                pl.BlockSpec((CONV_TAPS, ch), fix2), pl.BlockSpec((1, ch), fix2), pl.BlockSpec((1, ch), fix2),
                pl.BlockSpec((HEADS, HEAD_DIM), fix2), pl.BlockSpec((HEADS, HEAD_DIM), fix2),
                pl.BlockSpec((HEADS, CHUNK, CHUNK), fix3), pl.BlockSpec((HEADS, CHUNK, CHUNK), fix3),
                pl.BlockSpec((HEADS, CHUNK, HEAD_DIM), fix3)]
    return _call(
        body, exch, name=name, grid=(nt,), out_shape=out_shape, in_specs=in_specs, out_specs=out_specs,
        scratch_shapes=[pltpu.VMEM((HALO + tm, ch), F32), pltpu.VMEM((tm + HALO, ch), F32),
                        pltpu.VMEM((SUBLANES - 1, HALO + tm, ch), F32), pltpu.VMEM((SUBLANES - 1, tm + HALO, ch), F32),
                        pltpu.VMEM((CONV_TAPS, 8, ch), F32)],
        semantics=("arbitrary",),
    )(proj, proj, proj, proj, proj, proj, conv_c, conv_c, dy, dy, dy,
      conv_w, cln_g, cln_b, sln_g, sln_b, sg_wm, sg_wmt, sg_bb)


def _pair_sum(parts, from_sibling, core_chip, name):
    _, r, cc = parts.shape
    tr = _tile(r, max(16, (1 << 22) // (2 * cc)), 16)

    def body(cc_ref, p_ref, s_ref, o_ref, own_ref):
        q = (p_ref[...].astype(F32) + s_ref[...].astype(F32)).astype(BF16)
        o_ref[...] = q

        @pl.when(pl.program_id(1) == cc_ref[1])
        def _():
            own_ref[...] = q[0]

    grid_spec = pltpu.PrefetchScalarGridSpec(
        num_scalar_prefetch=1, grid=(r // tr, 4),
        in_specs=[pl.BlockSpec((1, tr, cc), lambda i, j, cc_ref: (2 * j + cc_ref[0], i, 0)),
                  pl.BlockSpec((1, tr, cc), lambda i, j, cc_ref: (j, i, 0))],
        out_specs=[pl.BlockSpec((1, tr, cc), lambda i, j, cc_ref: (j, i, 0)),
                   pl.BlockSpec((tr, cc), lambda i, j, cc_ref: (i, 0))])
    return pl.pallas_call(
        body, name=name, grid_spec=grid_spec,
        out_shape=[jax.ShapeDtypeStruct((4, r, cc), BF16), jax.ShapeDtypeStruct((r, cc), BF16)],
        compiler_params=_cparams("parallel", "arbitrary"),
    )(core_chip, parts, from_sibling)


def _adamw_math(w, g, m, v):
    m = ADAM_B1 * m + (1.0 - ADAM_B1) * g
    v = ADAM_B2 * v + (1.0 - ADAM_B2) * (g * g)
    m_hat = m / (1.0 - ADAM_B1 ** ADAM_STEP)
    v_hat = v / (1.0 - ADAM_B2 ** ADAM_STEP)
    delta = -ADAM_LR * (m_hat / (jnp.sqrt(v_hat) + ADAM_EPS) + ADAM_WD * w)
    return delta, m, v


def _adamw_tile(in_refs, out_refs):
    w_ref, m_ref, v_ref, q_ref, o_ref = in_refs
    g = q_ref[...].astype(F32)
    for k in range(3):
        g = g + o_ref[k].astype(F32)
    d, mm, vv = _adamw_math(w_ref[...], g, m_ref[...], v_ref[...])
    for ref, val in zip(out_refs, (g, d, mm, vv)):
        ref[...] = val


def _adamw_side(w, m, v, chip_part, from_chips, max_tiles):
    r, cc = w.shape
    n = max(k for k in range(1, max_tiles + 1) if r % k == 0 and (r // k) % 16 == 0)
    tr = r // n
    row = ((tr, cc), lambda s: (s, 0))
    return _Side([w, m, v, chip_part, from_chips], [row, row, row, row, ((3, tr, cc), lambda s: (0, s, 0))],
                 [jax.ShapeDtypeStruct((r, cc), F32)] * 4, [row] * 4, n, _adamw_tile)


def _adamw_sharded(w, m, v, chip_part, from_chips, name):
    r, cc = w.shape
    tr = _tile(r, max(16, (1 << 20) // (4 * cc) * 2), 16)

    def body(*refs):
        _adamw_tile(refs[:5], refs[5:])

    row = pl.BlockSpec((tr, cc), lambda i: (i, 0))
    return pl.pallas_call(
        body, name=name, grid=(r // tr,), out_shape=[jax.ShapeDtypeStruct((r, cc), F32)] * 4,
        in_specs=[row, row, row, row, pl.BlockSpec((3, tr, cc), lambda i: (0, i, 0))], out_specs=[row] * 4,
        compiler_params=_cparams("parallel"),
    )(w, m, v, chip_part, from_chips)


def _adamw_small(w, g, m, v, name):
    r, cc = w.shape

    def body(w_ref, g_ref, m_ref, v_ref, d_out, m_out, v_out):
        d, mm, vv = _adamw_math(w_ref[...], g_ref[...], m_ref[...], v_ref[...])
        d_out[...] = d
        m_out[...] = mm
        v_out[...] = vv

    full = pl.BlockSpec((r, cc), lambda i: (0, 0))
    return pl.pallas_call(
        body, name=name, grid=(1,), out_shape=[jax.ShapeDtypeStruct((r, cc), F32)] * 3,
        in_specs=[full] * 4, out_specs=[full] * 3, compiler_params=_cparams("arbitrary"),
    )(w, g, m, v)


SMALL = ("ln1_g", "ln1_b", "conv_b", "conv_ln_g", "conv_ln_b", "sg_ln_g", "sg_ln_b", "sg_w", "sg_b",
         "ln2_g", "ln2_b", "ln3_g", "ln3_b")
ORDER = ("ffn1_w_gate_up", "ffn1_w_down", "ln1_g", "ln1_b", "mix_w_in", "conv_w", "conv_b", "conv_ln_g", "conv_ln_b",
         "sg_ln_g", "sg_ln_b", "sg_w", "sg_b", "mix_w_out", "ln2_g", "ln2_b", "ffn2_w_gate_up", "ffn2_w_down",
         "ln3_g", "ln3_b")


def _rows128(a):
    return a.reshape(-1, 128)


def kernel(x, ffn1_w_gate_up, ffn1_w_down, ln1_g, ln1_b, mix_w_in, conv_w, conv_b, conv_ln_g, conv_ln_b, sg_ln_g, sg_ln_b, sg_w, sg_b, mix_w_out, ln2_g, ln2_b, ffn2_w_gate_up, ffn2_w_down, ln3_g, ln3_b, loss_target, m_ffn1_w_gate_up, m_ffn1_w_down, m_ln1_g, m_ln1_b, m_mix_w_in, m_conv_w, m_conv_b, m_conv_ln_g, m_conv_ln_b, m_sg_ln_g, m_sg_ln_b, m_sg_w, m_sg_b, m_mix_w_out, m_ln2_g, m_ln2_b, m_ffn2_w_gate_up, m_ffn2_w_down, m_ln3_g, m_ln3_b, v_ffn1_w_gate_up, v_ffn1_w_down, v_ln1_g, v_ln1_b, v_mix_w_in, v_conv_w, v_conv_b, v_conv_ln_g, v_conv_ln_b, v_sg_ln_g, v_sg_ln_b, v_sg_w, v_sg_b, v_mix_w_out, v_ln2_g, v_ln2_b, v_ffn2_w_gate_up, v_ffn2_w_down, v_ln3_g, v_ln3_b):
    args = dict(locals())
    w = {n: args[n][0] for n in ORDER}
    mom = {n: args["m_" + n][0] for n in ORDER}
    var = {n: args["v_" + n][0] for n in ORDER}
    x0 = x[0]
    target = loss_target[0]
    t, d = x0.shape
    my_x, my_y, my_c = lax.axis_index("x"), lax.axis_index("y"), lax.axis_index("c")
    my_chip = (2 * my_x + my_y).astype(jnp.int32).reshape(1)
    my_core = my_c.astype(jnp.int32).reshape(1)
    me = 4 * my_x + 2 * my_y + my_c

    big = ("ffn1_w_gate_up", "ffn1_w_down", "mix_w_in", "mix_w_out", "ffn2_w_gate_up", "ffn2_w_down")
    sh = {n: w[n].astype(BF16) for n in big}
    f2s = sh["ffn2_w_gate_up"].shape[1]
    order = jnp.stack([4 * p[0] + 2 * p[1] + p[2] for p in _visit_order(my_x, my_y, my_c)]).astype(jnp.int32)
    gu1, x0t, (wgu1, wd1, conv_w_all) = _gather_and_gate_up(
        x0, [sh["ffn1_w_gate_up"], sh["ffn1_w_down"], w["conv_w"]], [True, True, False], order, "ffn1_gate_up_fwd")
    wd1 = wd1.reshape(-1, d)
    conv_w_full = jnp.transpose(conv_w_all, (1, 0, 2)).reshape(CONV_TAPS, CONV_CH)
    tril = jnp.tril(jnp.ones((CHUNK, CHUNK), F32))
    sg_wm = w["sg_w"] * tril
    sg_wm_b = sg_wm.astype(BF16)
    sg_wmt_b = jnp.swapaxes(sg_wm, 1, 2).astype(BF16)
    sg_bb = jnp.broadcast_to(w["sg_b"][:, :, None], (HEADS, CHUNK, HEAD_DIM))
    row = lambda a: a.reshape(1, -1)

    d2 = [sh["ffn2_w_down"]]
    d2_first = d2[0].shape[0] // 2 // 16 * 16
    d2_top, d2_bottom = (0, d2_first), (d2_first, d2[0].shape[0] - d2_first)
    (h1t, z1, x1), ((g_in, g_out), (g_d2,)) = _ffn_down_fwd(
        gu1, x0, wd1, row(w["ln1_g"]), row(w["ln1_b"]), "ffn1_down_fwd",
        exch=[_gather_first([sh["mix_w_in"], sh["mix_w_out"]], [True, False]),
              _gather_first(d2, [False], rows=d2_top)])
    in_cols = sh["mix_w_in"].shape[1]
    x1t, ((w_in, w_out), (g_d2,)) = _transpose_bf16(
        x1, "x1_transpose", exch=[_gather_forward([g_in, g_out], [True, False], [in_cols, None]),
                                  _gather_forward([g_d2], [False], [None], rows=d2_top)])
    w_out = w_out.reshape(-1, d)
    top, bottom = (0, d // 2), (d // 2, d // 2)
    gu2 = [sh["ffn2_w_gate_up"]]
    proj, ((g_gu2,),) = _mix_in_proj(x1, w_in, "mix_in_fwd", exch=[_gather_first(gu2, [True], rows=top)])
    (y, yt, conv_c), ((g_gu2,),) = _mixer_fwd(
        proj, conv_w_full, row(w["conv_b"]), row(w["conv_ln_g"]), row(w["conv_ln_b"]),
        w["sg_ln_g"], w["sg_ln_b"], sg_wm_b, sg_bb, "mixer_fwd",
        exch=[_both(_gather_first(gu2, [True], rows=bottom, into=[g_gu2]),
                    _gather_forward([g_gu2], [True], [f2s], rows=top))])
    (z2, x2, x2t), ((wgu2,), (g_d2,)) = _mix_out_fwd(
        y, w_out, x1, row(w["ln2_g"]), row(w["ln2_b"]), "mix_out_fwd",
        exch=[_gather_forward([g_gu2], [True], [f2s], rows=bottom),
              _gather_first(d2, [False], rows=d2_bottom, into=[g_d2])])
    (wd2,) = _exchange_alone(_gather_forward([g_d2], [False], [None], rows=d2_bottom), "ffn2_down_gather_forward")
    wd2 = wd2.reshape(-1, d)
    grads = {}
    (g2, u2, h2t, dz3, do2, grads["ln3_g"], grads["ln3_b"], loss_tile), _ = _ffn_fwd_loss(
        x2, wgu2, wd2, row(w["ln3_g"]), row(w["ln3_b"]), target, "ffn2_fwd_loss")

    f = wd1.shape[0]
    dn = _tile(d, 1024, 128)
    core_chip = jnp.concatenate([my_core, my_chip])
    pair = lambda p, s, label: _pair_sum(p, s, core_chip, "pair_sum_" + label)
    adamw = lambda n, own, got, steps: _adamw_side(w[n], mom[n], var[n], own, got, steps)
    m_tiles = d // _tile(d, 512, 16)
    gu_first = d * 3 // 4 // 16 * 16
    out = {}
    p_d2, _ = _weight_grad(h2t, do2, dn, 512, "ffn2_dw_down")
    p_d2 = p_d2.reshape(N_DEV, f // N_DEV, d)
    (dg2, du2, dx2), ((s_d2,),) = _ffn_bwd(dz3, do2, g2, u2, wgu2, wd2, "ffn2_bwd", exch=[_rs_sibling([p_d2])])
    q_d2, own_d2 = pair(p_d2, s_d2, "ffn2_down")
    d_rows = q_d2.shape[1]
    d_half = d_rows // 2 // 16 * 16
    p_gu2, ((r_d2,),) = _weight_grad(x2t, dg2, f2s, 512, "ffn2_dw_gate", blocks=N_DEV,
                                     exch=[_rs_chips([q_d2], rows=(0, d_half))])
    p_gu2, ((r_d2,),) = _weight_grad(x2t, du2, f2s, 512, "ffn2_dw_up", blocks=N_DEV, block_offset=4, into=p_gu2,
                                     exch=[_rs_chips([q_d2], rows=(d_half, d_rows - d_half), into=[r_d2])])
    (dz2, dz2b, grads["ln2_g"], grads["ln2_b"]), ((s_gu2,),) = _ln_bwd_call(
        z2, dx2, row(w["ln2_g"]), 1.0, "ln2_bwd", exch=[_rs_sibling([p_gu2])])
    q_gu2, own_gu2 = pair(p_gu2, s_gu2, "ffn2_gate_up")
    dy, _ = _mix_out_bwd(dz2b, w_out, "mix_out_bwd")
    p_out, _ = _weight_grad(yt, dz2b, dn, 512, "mix_out_dw")
    p_out = p_out.reshape(N_DEV, -1, d)
    (dproj, grads["conv_w"], grads["conv_b"], grads["conv_ln_g"], grads["conv_ln_b"], grads["sg_ln_g"],
     grads["sg_ln_b"], grads["sg_w"], grads["sg_b"]), ((r_gu2,),) = _mixer_bwd(
        proj, conv_c, dy, conv_w_full, row(w["conv_ln_g"]), row(w["conv_ln_b"]), w["sg_ln_g"], w["sg_ln_b"],
        sg_wm_b, sg_wmt_b, sg_bb, "mixer_bwd", exch=[_rs_chips([q_gu2], rows=(0, gu_first))])
    dx1, ((s_out,), (r_gu2,)) = _mix_in_bwd(
        dproj, w_in, dz2, "mix_in_bwd",
        exch=[_rs_sibling([p_out]), _rs_chips([q_gu2], rows=(gu_first, d - gu_first), into=[r_gu2])])
    p_in, (out["ffn2_w_gate_up"], out["ffn2_w_down"]) = _weight_grad(
        x1t, dproj, in_cols, 512, "mix_in_dw", blocks=N_DEV,
        exch=[adamw("ffn2_w_gate_up", own_gu2, r_gu2, N_DEV * m_tiles), adamw("ffn2_w_down", own_d2, r_d2, N_DEV * m_tiles)])
    (dz1, do1, grads["ln1_g"], grads["ln1_b"]), ((s_in,),) = _ln_bwd_call(
        z1, dx1, row(w["ln1_g"]), 0.5, "ln1_bwd", exch=[_rs_sibling([p_in])])
    q_out, own_out = pair(p_out, s_out, "mix_out")
    q_in, own_in = pair(p_in, s_in, "mix_in")
    small_parts = [_rows128(grads[n]) for n in SMALL]
    packed = jnp.concatenate(small_parts + [_rows128(grads["conv_w"]), loss_tile], axis=0)
    p_d1, ((r_in,),) = _weight_grad(h1t, do1, dn, 512, "ffn1_dw_down", exch=[_rs_chips([q_in])])
    p_d1 = p_d1.reshape(N_DEV, f // N_DEV, d)
    (dg1, du1), ((s_d1,), (r_out,), (small_all,)) = _ffn_bwd_act(
        do1, gu1, wd1, "ffn1_bwd_act",
        exch=[_rs_sibling([p_d1]), _rs_chips([q_out]), _small_gather(packed)])
    q_d1, own_d1 = pair(p_d1, s_d1, "ffn1_down")
    p_gu1, ((r_d1,),) = _weight_grad(x0t, dg1, f2s, 512, "ffn1_dw_gate", blocks=N_DEV, exch=[_rs_chips([q_d1])])
    p_gu1, (out["mix_w_in"], out["mix_w_out"]) = _weight_grad(
        x0t, du1, f2s, 512, "ffn1_dw_up", blocks=N_DEV, block_offset=4, into=p_gu1,
        exch=[adamw("mix_w_in", own_in, r_in, 4 * m_tiles), adamw("mix_w_out", own_out, r_out, 4 * m_tiles)])
    (s_gu1,) = _exchange_alone(_rs_sibling([p_gu1]), "ffn1_gate_up_sibling_exchange")
    q_gu1, own_gu1 = pair(p_gu1, s_gu1, "ffn1_gate_up")
    (grad_x,), ((r_gu1,),) = _ffn_bwd_dx(dz1, dg1, du1, wgu1, "ffn1_bwd_dx", exch=[_rs_chips([q_gu1])])
    for n, own, got in (("ffn1_w_down", own_d1, r_d1), ("ffn1_w_gate_up", own_gu1, r_gu1)):
        out[n] = _adamw_sharded(w[n], mom[n], var[n], own, got, "adamw_" + n)

    cw_rows = CONV_TAPS * CONV_CH // 128
    total = _sum_over_devices(small_all)
    offs = [0]
    for p in small_parts:
        offs.append(offs[-1] + p.shape[0])
    n_small = offs[-1]
    loss = total[n_small + cw_rows, 0]
    g_conv_w = lax.dynamic_slice_in_dim(total[n_small:n_small + cw_rows].reshape(CONV_TAPS, CONV_CH),
                                        me * (CONV_CH // N_DEV), CONV_CH // N_DEV, axis=1)
    pad8 = lambda a: jnp.pad(a, ((0, -a.shape[0] % 8), (0, 0)))
    pack = lambda tree, cw: jnp.concatenate([_rows128(tree[n]) for n in SMALL] + [pad8(cw)], axis=0)
    g_pack = jnp.concatenate([total[:n_small], pad8(g_conv_w)], axis=0)
    d_pack, m_pack, v_pack = _adamw_small(pack(w, w["conv_w"]), g_pack, pack(mom, mom["conv_w"]),
                                          pack(var, var["conv_w"]), "adamw_small")
    for k, n in enumerate(SMALL):
        sl = slice(offs[k], offs[k + 1])
        shp = w[n].shape
        out[n] = (total[sl].reshape(shp), d_pack[sl].reshape(shp), m_pack[sl].reshape(shp), v_pack[sl].reshape(shp))
    sl = slice(n_small, n_small + CONV_TAPS)
    out["conv_w"] = (g_conv_w, d_pack[sl], m_pack[sl], v_pack[sl])

    lead = lambda a: a[None]
    res = [loss, grad_x[None]]
    for kind in range(4):
        res += [lead(out[n][kind]) for n in ORDER]
    return tuple(res)
```
